```python
import jax, jax.numpy as jnp
from jax import lax
import numpy as np

D_MODEL = 1024
BATCH = 8
SEQ = 8192
DEPTH = 4

HEAD_DIM = 64
SB_WIDTH = D_MODEL // 2
N_SB_HEADS = SB_WIDTH // HEAD_DIM
SG_WIDTH = D_MODEL // 2
SG_GROUP_DIM = 64
SG_GROUPS = SG_WIDTH // SG_GROUP_DIM
MIX_WIDTH = SB_WIDTH + SG_WIDTH
IN_WIDTH = 3 * SB_WIDTH + 2 * SG_WIDTH
CHUNK = 128
Q_BLOCK = 128
CONV_K = 31
CONV_WIDTH = D_MODEL
D_FF = ((8 * D_MODEL // 3 + 127) // 128) * 128
FFN_K = 3
N_EVEN = (DEPTH + 1) // 2
N_ODD = DEPTH // 2
EPS = 1e-6

kernel_name = "stickbreak_sgu_conformer_convglu_hybrid"


def rms_norm(x, g):
    xf = x.astype(jnp.float32)
    y = xf * lax.rsqrt(jnp.mean(xf * xf, axis=-1, keepdims=True) + EPS)
    return (y * g.astype(jnp.float32)).astype(x.dtype)


def layer_norm(x, g, b):
    xf = x.astype(jnp.float32)
    mu = jnp.mean(xf, axis=-1, keepdims=True)
    xc = xf - mu
    y = xc * lax.rsqrt(jnp.mean(xc * xc, axis=-1, keepdims=True) + EPS)
    return (y * g.astype(jnp.float32) + b.astype(jnp.float32)).astype(x.dtype)


def causal_depthwise_conv(x, w, b):
    k, c = w.shape
    y = lax.conv_general_dilated(
        x, w[:, None, :].astype(x.dtype), window_strides=(1,),
        padding=[(k - 1, 0)], dimension_numbers=("NWC", "WIO", "NWC"),
        feature_group_count=c)
    return y + b.astype(x.dtype)


def stick_breaking_attention(q, k, v):
    b_, s_, h_, dh = q.shape
    n_blk = s_ // Q_BLOCK
    scale = dh ** -0.5
    qb = q.reshape(b_, n_blk, Q_BLOCK, h_, dh).transpose(1, 0, 3, 2, 4)
    kt = k.transpose(0, 2, 1, 3).astype(jnp.float32)
    vt = v.transpose(0, 2, 1, 3).astype(jnp.float32)
    key_pos = jnp.arange(s_)

    def one_block(args):
        q_blk, blk = args
        z = jnp.einsum("bhqd,bhkd->bhqk", q_blk.astype(jnp.float32), kt) * scale
        q_pos = blk * Q_BLOCK + jnp.arange(Q_BLOCK)
        causal = key_pos[None, :] < q_pos[:, None]
        log_keep = jnp.where(causal, jax.nn.log_sigmoid(-z), 0.0)
        rev = lax.cumsum(log_keep, axis=3, reverse=True)
        tail = jnp.concatenate([rev[..., 1:], jnp.zeros_like(rev[..., :1])], axis=-1)
        w = jnp.where(causal, jnp.exp(jax.nn.log_sigmoid(z) + tail), 0.0)
        return jnp.einsum("bhqk,bhkd->bhqd", w, vt)

    out = lax.map(one_block, (qb, jnp.arange(n_blk)))
    return out.transpose(1, 0, 3, 2, 4).reshape(b_, s_, h_ * dh).astype(q.dtype)


def spatial_gating(u, z, g_z, w_s, b_s):
    b_, s_, _ = u.shape
    u = jax.nn.gelu(u, approximate=False)
    z = jax.nn.gelu(z, approximate=False)
    zg = z.reshape(b_, s_, SG_GROUPS, SG_GROUP_DIM)
    zf = zg.astype(jnp.float32)
    zg = (zf * lax.rsqrt(jnp.mean(zf * zf, axis=-1, keepdims=True) + EPS)
          * g_z.reshape(SG_GROUPS, SG_GROUP_DIM).astype(jnp.float32)).astype(z.dtype)
    zc = zg.reshape(b_, s_ // CHUNK, CHUNK, SG_GROUPS, SG_GROUP_DIM)
    mask = jnp.tril(jnp.ones((CHUNK, CHUNK), dtype=bool))
    wm = jnp.where(mask[None], w_s, 0.0).astype(z.dtype)
    s = jnp.einsum("gts,bcsgd->bctgd", wm, zc) + b_s.T[:, :, None].astype(z.dtype)
    return u * s.reshape(b_, s_, SG_WIDTH)


def attn_sgu_mixer(h, w_in, q_g, k_g, z_g, w_s, b_s, w_out):
    b_, s_, _ = h.shape
    proj = h @ w_in
    q, k, v, u, z = jnp.split(
        proj, [SB_WIDTH, 2 * SB_WIDTH, 3 * SB_WIDTH, 3 * SB_WIDTH + SG_WIDTH], axis=-1)
    q = rms_norm(q.reshape(b_, s_, N_SB_HEADS, HEAD_DIM), q_g)
    k = rms_norm(k.reshape(b_, s_, N_SB_HEADS, HEAD_DIM), k_g)
    v = v.reshape(b_, s_, N_SB_HEADS, HEAD_DIM)
    a = stick_breaking_attention(q, k, v)
    g = spatial_gating(u, z, z_g, w_s, b_s)
    return jnp.concatenate([a, g], axis=-1) @ w_out


def conformer_conv(h, w1, b1, w_dw, b_dw, ln_g, ln_b, w2, b2):
    a, gate = jnp.split(h @ w1 + b1, 2, axis=-1)
    y = a * jax.nn.sigmoid(gate)
    y = causal_depthwise_conv(y, w_dw, b_dw)
    y = jax.nn.silu(layer_norm(y, ln_g, ln_b))
    return y @ w2 + b2


def conv_glu_ffn(h, w_up, w_dw, b_dw, w_down):
    gate, val = jnp.split(h @ w_up, 2, axis=-1)
    gate = causal_depthwise_conv(gate, w_dw, b_dw)
    return (jax.nn.silu(gate) * val) @ w_down


def _fwd_setup_inputs(seed: int = 0) -> dict:
    key = jax.random.key(seed)
    ks = jax.random.split(key, 24)
    f32 = jnp.float32

    def nrm(k, shape, scale):
        return jax.random.normal(k, shape, f32) * scale

    def gain(k, shape):
        return 1.0 + 0.02 * jax.random.normal(k, shape, f32)

    return {
        "x": jax.random.normal(ks[0], (BATCH, SEQ, D_MODEL), f32),
        "mix_norm_g": gain(ks[1], (DEPTH, D_MODEL)),
        "sb_w_in": nrm(ks[2], (N_EVEN, D_MODEL, IN_WIDTH), D_MODEL ** -0.5),
        "sb_q_norm_g": gain(ks[3], (N_EVEN, HEAD_DIM)),
        "sb_k_norm_g": gain(ks[4], (N_EVEN, HEAD_DIM)),
        "sg_z_norm_g": gain(ks[5], (N_EVEN, SG_WIDTH)),
        "sg_w_spatial": nrm(ks[6], (N_EVEN, SG_GROUPS, CHUNK, CHUNK), CHUNK ** -0.5),
        "sg_b_spatial": gain(ks[7], (N_EVEN, SG_GROUPS, CHUNK)),
        "hyb_w_out": nrm(ks[8], (N_EVEN, MIX_WIDTH, D_MODEL), MIX_WIDTH ** -0.5),
        "cv_w_pw1": nrm(ks[9], (N_ODD, D_MODEL, 2 * CONV_WIDTH), D_MODEL ** -0.5),
        "cv_b_pw1": nrm(ks[10], (N_ODD, 2 * CONV_WIDTH), 0.02),
        "cv_w_dw": nrm(ks[11], (N_ODD, CONV_K, CONV_WIDTH), CONV_K ** -0.5),
        "cv_b_dw": nrm(ks[12], (N_ODD, CONV_WIDTH), 0.02),
        "cv_ln_g": gain(ks[13], (N_ODD, CONV_WIDTH)),
        "cv_ln_b": nrm(ks[14], (N_ODD, CONV_WIDTH), 0.02),
        "cv_w_pw2": nrm(ks[15], (N_ODD, CONV_WIDTH, D_MODEL), CONV_WIDTH ** -0.5),
        "cv_b_pw2": nrm(ks[16], (N_ODD, D_MODEL), 0.02),
        "ffn_norm_g": gain(ks[17], (DEPTH, D_MODEL)),
        "ffn_w_up": nrm(ks[18], (DEPTH, D_MODEL, 2 * D_FF), D_MODEL ** -0.5),
        "ffn_w_dw": nrm(ks[19], (DEPTH, FFN_K, D_FF), FFN_K ** -0.5),
        "ffn_b_dw": nrm(ks[20], (DEPTH, D_FF), 0.02),
        "ffn_w_down": nrm(ks[21], (DEPTH, D_FF, D_MODEL), D_FF ** -0.5),
    }


def _fwd_reference(x, mix_norm_g, sb_w_in, sb_q_norm_g, sb_k_norm_g, sg_z_norm_g,
              sg_w_spatial, sg_b_spatial, hyb_w_out, cv_w_pw1, cv_b_pw1, cv_w_dw,
              cv_b_dw, cv_ln_g, cv_ln_b, cv_w_pw2, cv_b_pw2, ffn_norm_g, ffn_w_up,
              ffn_w_dw, ffn_b_dw, ffn_w_down):
    for i in range(DEPTH):
        h = rms_norm(x, mix_norm_g[i])
        j = i // 2
        if i % 2 == 0:
            x = x + attn_sgu_mixer(h, sb_w_in[j], sb_q_norm_g[j], sb_k_norm_g[j],
                                   sg_z_norm_g[j], sg_w_spatial[j], sg_b_spatial[j],
                                   hyb_w_out[j])
        else:
            x = x + conformer_conv(h, cv_w_pw1[j], cv_b_pw1[j], cv_w_dw[j], cv_b_dw[j],
                                   cv_ln_g[j], cv_ln_b[j], cv_w_pw2[j], cv_b_pw2[j])
        h = rms_norm(x, ffn_norm_g[i])
        x = x + conv_glu_ffn(h, ffn_w_up[i], ffn_w_dw[i], ffn_b_dw[i], ffn_w_down[i])
    return x


import jax as _jax
import jax.numpy as _jnp

TWIN_FORMAT = 'train_step'
FWD_PARAMS = ['x', 'mix_norm_g', 'sb_w_in', 'sb_q_norm_g', 'sb_k_norm_g', 'sg_z_norm_g', 'sg_w_spatial', 'sg_b_spatial', 'hyb_w_out', 'cv_w_pw1', 'cv_b_pw1', 'cv_w_dw', 'cv_b_dw', 'cv_ln_g', 'cv_ln_b', 'cv_w_pw2', 'cv_b_pw2', 'ffn_norm_g', 'ffn_w_up', 'ffn_w_dw', 'ffn_b_dw', 'ffn_w_down']
TWIN_WEIGHTS = ['mix_norm_g', 'sb_w_in', 'sb_q_norm_g', 'sb_k_norm_g', 'sg_z_norm_g', 'sg_w_spatial', 'sg_b_spatial', 'hyb_w_out', 'cv_w_pw1', 'cv_b_pw1', 'cv_w_dw', 'cv_b_dw', 'cv_ln_g', 'cv_ln_b', 'cv_w_pw2', 'cv_b_pw2', 'ffn_norm_g', 'ffn_w_up', 'ffn_w_dw', 'ffn_b_dw', 'ffn_w_down']
TWIN_DIFF_INPUT = 'x'
TWIN_INPUTS = ['x', 'mix_norm_g', 'sb_w_in', 'sb_q_norm_g', 'sb_k_norm_g', 'sg_z_norm_g', 'sg_w_spatial', 'sg_b_spatial', 'hyb_w_out', 'cv_w_pw1', 'cv_b_pw1', 'cv_w_dw', 'cv_b_dw', 'cv_ln_g', 'cv_ln_b', 'cv_w_pw2', 'cv_b_pw2', 'ffn_norm_g', 'ffn_w_up', 'ffn_w_dw', 'ffn_b_dw', 'ffn_w_down', 'loss_target', 'm_mix_norm_g', 'm_sb_w_in', 'm_sb_q_norm_g', 'm_sb_k_norm_g', 'm_sg_z_norm_g', 'm_sg_w_spatial', 'm_sg_b_spatial', 'm_hyb_w_out', 'm_cv_w_pw1', 'm_cv_b_pw1', 'm_cv_w_dw', 'm_cv_b_dw', 'm_cv_ln_g', 'm_cv_ln_b', 'm_cv_w_pw2', 'm_cv_b_pw2', 'm_ffn_norm_g', 'm_ffn_w_up', 'm_ffn_w_dw', 'm_ffn_b_dw', 'm_ffn_w_down', 'v_mix_norm_g', 'v_sb_w_in', 'v_sb_q_norm_g', 'v_sb_k_norm_g', 'v_sg_z_norm_g', 'v_sg_w_spatial', 'v_sg_b_spatial', 'v_hyb_w_out', 'v_cv_w_pw1', 'v_cv_b_pw1', 'v_cv_w_dw', 'v_cv_b_dw', 'v_cv_ln_g', 'v_cv_ln_b', 'v_cv_w_pw2', 'v_cv_b_pw2', 'v_ffn_norm_g', 'v_ffn_w_up', 'v_ffn_w_dw', 'v_ffn_b_dw', 'v_ffn_w_down']
TWIN_OUTPUTS = ['loss', 'grad_x', 'grad_mix_norm_g', 'grad_sb_w_in', 'grad_sb_q_norm_g', 'grad_sb_k_norm_g', 'grad_sg_z_norm_g', 'grad_sg_w_spatial', 'grad_sg_b_spatial', 'grad_hyb_w_out', 'grad_cv_w_pw1', 'grad_cv_b_pw1', 'grad_cv_w_dw', 'grad_cv_b_dw', 'grad_cv_ln_g', 'grad_cv_ln_b', 'grad_cv_w_pw2', 'grad_cv_b_pw2', 'grad_ffn_norm_g', 'grad_ffn_w_up', 'grad_ffn_w_dw', 'grad_ffn_b_dw', 'grad_ffn_w_down', 'delta_mix_norm_g', 'delta_sb_w_in', 'delta_sb_q_norm_g', 'delta_sb_k_norm_g', 'delta_sg_z_norm_g', 'delta_sg_w_spatial', 'delta_sg_b_spatial', 'delta_hyb_w_out', 'delta_cv_w_pw1', 'delta_cv_b_pw1', 'delta_cv_w_dw', 'delta_cv_b_dw', 'delta_cv_ln_g', 'delta_cv_ln_b', 'delta_cv_w_pw2', 'delta_cv_b_pw2', 'delta_ffn_norm_g', 'delta_ffn_w_up', 'delta_ffn_w_dw', 'delta_ffn_b_dw', 'delta_ffn_w_down', 'new_m_mix_norm_g', 'new_m_sb_w_in', 'new_m_sb_q_norm_g', 'new_m_sb_k_norm_g', 'new_m_sg_z_norm_g', 'new_m_sg_w_spatial', 'new_m_sg_b_spatial', 'new_m_hyb_w_out', 'new_m_cv_w_pw1', 'new_m_cv_b_pw1', 'new_m_cv_w_dw', 'new_m_cv_b_dw', 'new_m_cv_ln_g', 'new_m_cv_ln_b', 'new_m_cv_w_pw2', 'new_m_cv_b_pw2', 'new_m_ffn_norm_g', 'new_m_ffn_w_up', 'new_m_ffn_w_dw', 'new_m_ffn_b_dw', 'new_m_ffn_w_down', 'new_v_mix_norm_g', 'new_v_sb_w_in', 'new_v_sb_q_norm_g', 'new_v_sb_k_norm_g', 'new_v_sg_z_norm_g', 'new_v_sg_w_spatial', 'new_v_sg_b_spatial', 'new_v_hyb_w_out', 'new_v_cv_w_pw1', 'new_v_cv_b_pw1', 'new_v_cv_w_dw', 'new_v_cv_b_dw', 'new_v_cv_ln_g', 'new_v_cv_ln_b', 'new_v_cv_w_pw2', 'new_v_cv_b_pw2', 'new_v_ffn_norm_g', 'new_v_ffn_w_up', 'new_v_ffn_w_dw', 'new_v_ffn_b_dw', 'new_v_ffn_w_down']
TWIN_LEAF_KINDS = {'loss': 'loss', 'grad_x': 'grad_x', 'grad_mix_norm_g': 'grad_w', 'grad_sb_w_in': 'grad_w', 'grad_sb_q_norm_g': 'grad_w', 'grad_sb_k_norm_g': 'grad_w', 'grad_sg_z_norm_g': 'grad_w', 'grad_sg_w_spatial': 'grad_w', 'grad_sg_b_spatial': 'grad_w', 'grad_hyb_w_out': 'grad_w', 'grad_cv_w_pw1': 'grad_w', 'grad_cv_b_pw1': 'grad_w', 'grad_cv_w_dw': 'grad_w', 'grad_cv_b_dw': 'grad_w', 'grad_cv_ln_g': 'grad_w', 'grad_cv_ln_b': 'grad_w', 'grad_cv_w_pw2': 'grad_w', 'grad_cv_b_pw2': 'grad_w', 'grad_ffn_norm_g': 'grad_w', 'grad_ffn_w_up': 'grad_w', 'grad_ffn_w_dw': 'grad_w', 'grad_ffn_b_dw': 'grad_w', 'grad_ffn_w_down': 'grad_w', 'delta_mix_norm_g': 'delta_w', 'delta_sb_w_in': 'delta_w', 'delta_sb_q_norm_g': 'delta_w', 'delta_sb_k_norm_g': 'delta_w', 'delta_sg_z_norm_g': 'delta_w', 'delta_sg_w_spatial': 'delta_w', 'delta_sg_b_spatial': 'delta_w', 'delta_hyb_w_out': 'delta_w', 'delta_cv_w_pw1': 'delta_w', 'delta_cv_b_pw1': 'delta_w', 'delta_cv_w_dw': 'delta_w', 'delta_cv_b_dw': 'delta_w', 'delta_cv_ln_g': 'delta_w', 'delta_cv_ln_b': 'delta_w', 'delta_cv_w_pw2': 'delta_w', 'delta_cv_b_pw2': 'delta_w', 'delta_ffn_norm_g': 'delta_w', 'delta_ffn_w_up': 'delta_w', 'delta_ffn_w_dw': 'delta_w', 'delta_ffn_b_dw': 'delta_w', 'delta_ffn_w_down': 'delta_w', 'new_m_mix_norm_g': 'new_m', 'new_m_sb_w_in': 'new_m', 'new_m_sb_q_norm_g': 'new_m', 'new_m_sb_k_norm_g': 'new_m', 'new_m_sg_z_norm_g': 'new_m', 'new_m_sg_w_spatial': 'new_m', 'new_m_sg_b_spatial': 'new_m', 'new_m_hyb_w_out': 'new_m', 'new_m_cv_w_pw1': 'new_m', 'new_m_cv_b_pw1': 'new_m', 'new_m_cv_w_dw': 'new_m', 'new_m_cv_b_dw': 'new_m', 'new_m_cv_ln_g': 'new_m', 'new_m_cv_ln_b': 'new_m', 'new_m_cv_w_pw2': 'new_m', 'new_m_cv_b_pw2': 'new_m', 'new_m_ffn_norm_g': 'new_m', 'new_m_ffn_w_up': 'new_m', 'new_m_ffn_w_dw': 'new_m', 'new_m_ffn_b_dw': 'new_m', 'new_m_ffn_w_down': 'new_m', 'new_v_mix_norm_g': 'new_v', 'new_v_sb_w_in': 'new_v', 'new_v_sb_q_norm_g': 'new_v', 'new_v_sb_k_norm_g': 'new_v', 'new_v_sg_z_norm_g': 'new_v', 'new_v_sg_w_spatial': 'new_v', 'new_v_sg_b_spatial': 'new_v', 'new_v_hyb_w_out': 'new_v', 'new_v_cv_w_pw1': 'new_v', 'new_v_cv_b_pw1': 'new_v', 'new_v_cv_w_dw': 'new_v', 'new_v_cv_b_dw': 'new_v', 'new_v_cv_ln_g': 'new_v', 'new_v_cv_ln_b': 'new_v', 'new_v_cv_w_pw2': 'new_v', 'new_v_cv_b_pw2': 'new_v', 'new_v_ffn_norm_g': 'new_v', 'new_v_ffn_w_up': 'new_v', 'new_v_ffn_w_dw': 'new_v', 'new_v_ffn_b_dw': 'new_v', 'new_v_ffn_w_down': 'new_v'}


def _forward(args):
    return _fwd_reference(*[args[k] for k in FWD_PARAMS])


def _output_shape():
    out = _jax.eval_shape(lambda: _forward(_fwd_setup_inputs(0)))
    return out.shape, out.dtype

N_MICROBATCH = 1
ADAM_LR = 0.001
ADAM_B1 = 0.9
ADAM_B2 = 0.999
ADAM_EPS = 1e-08
ADAM_WD = 0.01
ADAM_STEP = 10
PER_EXAMPLE_BATCH_AXIS = {'x': 0, 'loss_target': 0}
SHARED_INPUTS = []
_WEIGHT_DTYPES = {'mix_norm_g': _jnp.float32, 'sb_w_in': _jnp.float32, 'sb_q_norm_g': _jnp.float32, 'sb_k_norm_g': _jnp.float32, 'sg_z_norm_g': _jnp.float32, 'sg_w_spatial': _jnp.float32, 'sg_b_spatial': _jnp.float32, 'hyb_w_out': _jnp.float32, 'cv_w_pw1': _jnp.float32, 'cv_b_pw1': _jnp.float32, 'cv_w_dw': _jnp.float32, 'cv_b_dw': _jnp.float32, 'cv_ln_g': _jnp.float32, 'cv_ln_b': _jnp.float32, 'cv_w_pw2': _jnp.float32, 'cv_b_pw2': _jnp.float32, 'ffn_norm_g': _jnp.float32, 'ffn_w_up': _jnp.float32, 'ffn_w_dw': _jnp.float32, 'ffn_b_dw': _jnp.float32, 'ffn_w_down': _jnp.float32}
MOMENT_SCALE = {'mix_norm_g': 2.568375e+01, 'sb_w_in': 2.074057e+00, 'sb_q_norm_g': 3.115255e+01, 'sb_k_norm_g': 3.120912e+01, 'sg_z_norm_g': 1.257128e+01, 'sg_w_spatial': 3.466409e+00, 'sg_b_spatial': 1.322406e+01, 'hyb_w_out': 7.661691e+00, 'cv_w_pw1': 1.598005e+00, 'cv_b_pw1': 1.422273e+01, 'cv_w_dw': 3.276601e+00, 'cv_b_dw': 3.216764e+01, 'cv_ln_g': 3.401972e+01, 'cv_ln_b': 2.705248e+01, 'cv_w_pw2': 7.946049e+00, 'cv_b_pw2': 3.843625e+01, 'ffn_norm_g': 5.064223e+01, 'ffn_w_up': 1.245106e+00, 'ffn_w_dw': 5.871367e+00, 'ffn_b_dw': 7.241511e+00, 'ffn_w_down': 1.359510e+00}


def _to_microbatches(a, axis):
    t = _jnp.moveaxis(a, axis, 0)
    t = t.reshape((N_MICROBATCH, t.shape[0] // N_MICROBATCH) + t.shape[1:])
    return _jnp.moveaxis(t, 1, axis + 1)


def setup_inputs(seed: int = 0) -> dict:
    inp = _fwd_setup_inputs(seed)
    key = _jax.random.fold_in(_jax.random.key(seed), 7919)
    shape, _ = _output_shape()
    out = dict(inp)
    out["loss_target"] = _jax.random.normal(_jax.random.fold_in(key, 0), shape, _jnp.float32)
    for i, name in enumerate(TWIN_WEIGHTS):
        w = inp[name].astype(_jnp.float32)
        if MOMENT_SCALE is None:
            s = _jnp.sqrt(_jnp.mean(_jnp.square(w)) + 1e-30)
        else:
            s = MOMENT_SCALE[name]
        km, kv = _jax.random.split(_jax.random.fold_in(key, i + 1))
        out[name] = w
        out["m_" + name] = s * _jax.random.normal(km, w.shape, _jnp.float32)
        out["v_" + name] = (s * s) * _jax.random.uniform(kv, w.shape, _jnp.float32, 0.5, 1.5)
    if N_MICROBATCH > 1:
        for name, axis in PER_EXAMPLE_BATCH_AXIS.items():
            out[name] = _to_microbatches(out[name], axis)
    return {'x': out['x'], 'mix_norm_g': out['mix_norm_g'], 'sb_w_in': out['sb_w_in'], 'sb_q_norm_g': out['sb_q_norm_g'], 'sb_k_norm_g': out['sb_k_norm_g'], 'sg_z_norm_g': out['sg_z_norm_g'], 'sg_w_spatial': out['sg_w_spatial'], 'sg_b_spatial': out['sg_b_spatial'], 'hyb_w_out': out['hyb_w_out'], 'cv_w_pw1': out['cv_w_pw1'], 'cv_b_pw1': out['cv_b_pw1'], 'cv_w_dw': out['cv_w_dw'], 'cv_b_dw': out['cv_b_dw'], 'cv_ln_g': out['cv_ln_g'], 'cv_ln_b': out['cv_ln_b'], 'cv_w_pw2': out['cv_w_pw2'], 'cv_b_pw2': out['cv_b_pw2'], 'ffn_norm_g': out['ffn_norm_g'], 'ffn_w_up': out['ffn_w_up'], 'ffn_w_dw': out['ffn_w_dw'], 'ffn_b_dw': out['ffn_b_dw'], 'ffn_w_down': out['ffn_w_down'], 'loss_target': out['loss_target'], 'm_mix_norm_g': out['m_mix_norm_g'], 'm_sb_w_in': out['m_sb_w_in'], 'm_sb_q_norm_g': out['m_sb_q_norm_g'], 'm_sb_k_norm_g': out['m_sb_k_norm_g'], 'm_sg_z_norm_g': out['m_sg_z_norm_g'], 'm_sg_w_spatial': out['m_sg_w_spatial'], 'm_sg_b_spatial': out['m_sg_b_spatial'], 'm_hyb_w_out': out['m_hyb_w_out'], 'm_cv_w_pw1': out['m_cv_w_pw1'], 'm_cv_b_pw1': out['m_cv_b_pw1'], 'm_cv_w_dw': out['m_cv_w_dw'], 'm_cv_b_dw': out['m_cv_b_dw'], 'm_cv_ln_g': out['m_cv_ln_g'], 'm_cv_ln_b': out['m_cv_ln_b'], 'm_cv_w_pw2': out['m_cv_w_pw2'], 'm_cv_b_pw2': out['m_cv_b_pw2'], 'm_ffn_norm_g': out['m_ffn_norm_g'], 'm_ffn_w_up': out['m_ffn_w_up'], 'm_ffn_w_dw': out['m_ffn_w_dw'], 'm_ffn_b_dw': out['m_ffn_b_dw'], 'm_ffn_w_down': out['m_ffn_w_down'], 'v_mix_norm_g': out['v_mix_norm_g'], 'v_sb_w_in': out['v_sb_w_in'], 'v_sb_q_norm_g': out['v_sb_q_norm_g'], 'v_sb_k_norm_g': out['v_sb_k_norm_g'], 'v_sg_z_norm_g': out['v_sg_z_norm_g'], 'v_sg_w_spatial': out['v_sg_w_spatial'], 'v_sg_b_spatial': out['v_sg_b_spatial'], 'v_hyb_w_out': out['v_hyb_w_out'], 'v_cv_w_pw1': out['v_cv_w_pw1'], 'v_cv_b_pw1': out['v_cv_b_pw1'], 'v_cv_w_dw': out['v_cv_w_dw'], 'v_cv_b_dw': out['v_cv_b_dw'], 'v_cv_ln_g': out['v_cv_ln_g'], 'v_cv_ln_b': out['v_cv_ln_b'], 'v_cv_w_pw2': out['v_cv_w_pw2'], 'v_cv_b_pw2': out['v_cv_b_pw2'], 'v_ffn_norm_g': out['v_ffn_norm_g'], 'v_ffn_w_up': out['v_ffn_w_up'], 'v_ffn_w_dw': out['v_ffn_w_dw'], 'v_ffn_b_dw': out['v_ffn_b_dw'], 'v_ffn_w_down': out['v_ffn_w_down']}


def _loss(weights, diff, rest, loss_target):
    with _jax.named_scope("forward"):
        args = {**rest, TWIN_DIFF_INPUT: diff, **{k: w.astype(_WEIGHT_DTYPES[k]) for k, w in weights.items()}}
        y = _forward(args)
    with _jax.named_scope("loss_head"):
        err = _jnp.square(y.astype(_jnp.float32) - loss_target)
        return 0.5 * _jnp.sum(_jnp.mean(err, axis=-1)) if err.ndim else 0.5 * err


def _adamw(w, g, m, v):
    m = ADAM_B1 * m + (1.0 - ADAM_B1) * g
    v = ADAM_B2 * v + (1.0 - ADAM_B2) * _jnp.square(g)
    m_hat = m / (1.0 - ADAM_B1 ** ADAM_STEP)
    v_hat = v / (1.0 - ADAM_B2 ** ADAM_STEP)
    delta = -ADAM_LR * (m_hat / (_jnp.sqrt(v_hat) + ADAM_EPS) + ADAM_WD * w)
    return delta, m, v


def reference(x, mix_norm_g, sb_w_in, sb_q_norm_g, sb_k_norm_g, sg_z_norm_g, sg_w_spatial, sg_b_spatial, hyb_w_out, cv_w_pw1, cv_b_pw1, cv_w_dw, cv_b_dw, cv_ln_g, cv_ln_b, cv_w_pw2, cv_b_pw2, ffn_norm_g, ffn_w_up, ffn_w_dw, ffn_b_dw, ffn_w_down, loss_target, m_mix_norm_g, m_sb_w_in, m_sb_q_norm_g, m_sb_k_norm_g, m_sg_z_norm_g, m_sg_w_spatial, m_sg_b_spatial, m_hyb_w_out, m_cv_w_pw1, m_cv_b_pw1, m_cv_w_dw, m_cv_b_dw, m_cv_ln_g, m_cv_ln_b, m_cv_w_pw2, m_cv_b_pw2, m_ffn_norm_g, m_ffn_w_up, m_ffn_w_dw, m_ffn_b_dw, m_ffn_w_down, v_mix_norm_g, v_sb_w_in, v_sb_q_norm_g, v_sb_k_norm_g, v_sg_z_norm_g, v_sg_w_spatial, v_sg_b_spatial, v_hyb_w_out, v_cv_w_pw1, v_cv_b_pw1, v_cv_w_dw, v_cv_b_dw, v_cv_ln_g, v_cv_ln_b, v_cv_w_pw2, v_cv_b_pw2, v_ffn_norm_g, v_ffn_w_up, v_ffn_w_dw, v_ffn_b_dw, v_ffn_w_down):
    given = dict(x=x, mix_norm_g=mix_norm_g, sb_w_in=sb_w_in, sb_q_norm_g=sb_q_norm_g, sb_k_norm_g=sb_k_norm_g, sg_z_norm_g=sg_z_norm_g, sg_w_spatial=sg_w_spatial, sg_b_spatial=sg_b_spatial, hyb_w_out=hyb_w_out, cv_w_pw1=cv_w_pw1, cv_b_pw1=cv_b_pw1, cv_w_dw=cv_w_dw, cv_b_dw=cv_b_dw, cv_ln_g=cv_ln_g, cv_ln_b=cv_ln_b, cv_w_pw2=cv_w_pw2, cv_b_pw2=cv_b_pw2, ffn_norm_g=ffn_norm_g, ffn_w_up=ffn_w_up, ffn_w_dw=ffn_w_dw, ffn_b_dw=ffn_b_dw, ffn_w_down=ffn_w_down, loss_target=loss_target, m_mix_norm_g=m_mix_norm_g, m_sb_w_in=m_sb_w_in, m_sb_q_norm_g=m_sb_q_norm_g, m_sb_k_norm_g=m_sb_k_norm_g, m_sg_z_norm_g=m_sg_z_norm_g, m_sg_w_spatial=m_sg_w_spatial, m_sg_b_spatial=m_sg_b_spatial, m_hyb_w_out=m_hyb_w_out, m_cv_w_pw1=m_cv_w_pw1, m_cv_b_pw1=m_cv_b_pw1, m_cv_w_dw=m_cv_w_dw, m_cv_b_dw=m_cv_b_dw, m_cv_ln_g=m_cv_ln_g, m_cv_ln_b=m_cv_ln_b, m_cv_w_pw2=m_cv_w_pw2, m_cv_b_pw2=m_cv_b_pw2, m_ffn_norm_g=m_ffn_norm_g, m_ffn_w_up=m_ffn_w_up, m_ffn_w_dw=m_ffn_w_dw, m_ffn_b_dw=m_ffn_b_dw, m_ffn_w_down=m_ffn_w_down, v_mix_norm_g=v_mix_norm_g, v_sb_w_in=v_sb_w_in, v_sb_q_norm_g=v_sb_q_norm_g, v_sb_k_norm_g=v_sb_k_norm_g, v_sg_z_norm_g=v_sg_z_norm_g, v_sg_w_spatial=v_sg_w_spatial, v_sg_b_spatial=v_sg_b_spatial, v_hyb_w_out=v_hyb_w_out, v_cv_w_pw1=v_cv_w_pw1, v_cv_b_pw1=v_cv_b_pw1, v_cv_w_dw=v_cv_w_dw, v_cv_b_dw=v_cv_b_dw, v_cv_ln_g=v_cv_ln_g, v_cv_ln_b=v_cv_ln_b, v_cv_w_pw2=v_cv_w_pw2, v_cv_b_pw2=v_cv_b_pw2, v_ffn_norm_g=v_ffn_norm_g, v_ffn_w_up=v_ffn_w_up, v_ffn_w_dw=v_ffn_w_dw, v_ffn_b_dw=v_ffn_b_dw, v_ffn_w_down=v_ffn_w_down)
    weights = {n: given[n] for n in TWIN_WEIGHTS}
    shared = {n: given[n] for n in SHARED_INPUTS}
    per_example = {n: given[n] for n in ['x']}
    grad_fn = _jax.value_and_grad(_loss, argnums=(0, 1))

    def one_microbatch(ex, loss_target):
        ex = dict(ex)
        diff = ex.pop(TWIN_DIFF_INPUT)
        return grad_fn(weights, diff, {**shared, **ex}, loss_target)

    if N_MICROBATCH == 1:
        loss, (grad_w, grad_x) = one_microbatch(per_example, given["loss_target"])
    else:
        def body(carry, xs):
            loss_sum, grad_sum = carry
            l_k, (gw_k, gx_k) = one_microbatch(xs[0], xs[1])
            with _jax.named_scope("update"):
                return (loss_sum + l_k, _jax.tree.map(_jnp.add, grad_sum, gw_k)), gx_k

        init = (_jnp.zeros((), _jnp.float32), _jax.tree.map(_jnp.zeros_like, weights))
        (loss, grad_w), grad_x = _jax.lax.scan(body, init, (per_example, given["loss_target"]))
    with _jax.named_scope("update"):
        delta_w, new_m, new_v = {}, {}, {}
        for n in TWIN_WEIGHTS:
            delta_w[n], new_m[n], new_v[n] = _adamw(weights[n], grad_w[n], given["m_" + n], given["v_" + n])
    return (loss, grad_x, *[grad_w[n] for n in TWIN_WEIGHTS], *[delta_w[n] for n in TWIN_WEIGHTS],
            *[new_m[n] for n in TWIN_WEIGHTS], *[new_v[n] for n in TWIN_WEIGHTS])
```

```python
import functools

import jax
import jax.numpy as jnp
from jax import lax
from jax.experimental import pallas as pl
from jax.experimental.pallas import tpu as pltpu

F32 = jnp.float32
BF = jnp.bfloat16
SDS = jax.ShapeDtypeStruct
HI = lax.Precision.HIGHEST
MESH = pl.DeviceIdType.MESH

NORM_EPS = 1e-6
HEAD_DIM = 64
ATT_BLOCK = 128
CHUNK = 128
CONV_K = 31
CONV_HALO = 32
FFN_K = 3
FFN_HALO = 8
LANES = 128
N_CHIPS = 4
VMEM_LIMIT_BYTES = 56 * 2**20

ADAM_LR = 0.001
ADAM_B1 = 0.9
ADAM_B2 = 0.999
ADAM_EPS = 1e-08
ADAM_WD = 0.01
ADAM_STEP = 10

NT_DIMS = (((1,), (1,)), ((), ()))
TN_DIMS = (((0,), (0,)), ((), ()))


def _call(body, **kw):
    return pl.pallas_call(body, **kw)


def _params(*sem):
    return pltpu.CompilerParams(dimension_semantics=sem, vmem_limit_bytes=VMEM_LIMIT_BYTES)


def _gelu(x):
    return 0.5 * x * (1.0 + lax.erf(x * 0.7071067811865476))


def _rms(x, g):
    y = x * lax.rsqrt(jnp.mean(x * x, axis=-1, keepdims=True) + NORM_EPS)
    return y * g


def _rms_fwd(x, g):
    t, d = x.shape
    tm = min(512, t)

    def body(x_ref, g_ref, o_ref):
        o_ref[...] = _rms(x_ref[...], g_ref[...]).astype(o_ref.dtype)

    return _call(
        body, name="rms_fwd", grid=(t // tm,),
        in_specs=[pl.BlockSpec((tm, d), lambda i: (i, 0)), pl.BlockSpec((1, d), lambda i: (0, 0))],
        out_specs=pl.BlockSpec((tm, d), lambda i: (i, 0)),
        out_shape=SDS((t, d), BF), compiler_params=_params("parallel"))(x, g)


def _rms_bwd(dh, x, g, dres):
    t, d = x.shape
    tm = min(512, t)

    def body(dh_ref, x_ref, g_ref, r_ref, dx_ref, dxb_ref, dg_ref):
        _, vjp = jax.vjp(_rms, x_ref[...], g_ref[...])
        dx, dg = vjp(dh_ref[...])
        dx = dx + r_ref[...]
        dx_ref[...] = dx
        dxb_ref[...] = dx.astype(BF)

        @pl.when(pl.program_id(0) == 0)
        def _():
            dg_ref[...] = jnp.zeros_like(dg_ref)

        dg_ref[...] += dg

    row = pl.BlockSpec((tm, d), lambda i: (i, 0))
    vec = pl.BlockSpec((1, d), lambda i: (0, 0))
    return _call(
        body, name="rms_bwd", grid=(t // tm,), in_specs=[row, row, vec, row], out_specs=[row, row, vec],
        out_shape=[SDS((t, d), F32), SDS((t, d), BF), SDS((1, d), F32)],
        compiler_params=_params("arbitrary"))(dh, x, g, dres)


def _mm_nn(a, w, l, bias=None, resid=None, out_dtype=F32):
    m, k = a.shape
    _, p_n, kw, n = w.shape
    assert k == kw
    tm = min(512, m)
    tn = n if k * n * 2 <= 4 * 2**20 else n // 2
    nj = n // tn
    in_specs = [pl.BlockSpec((tm, k), lambda i, p, j: (i, 0)),
                pl.BlockSpec((None, None, k, tn), lambda i, p, j: (l, p, 0, j))]
    args = [a, w]
    if bias is not None:
        in_specs.append(pl.BlockSpec((1, tn), lambda i, p, j: (0, p * nj + j)))
        args.append(bias)
    if resid is not None:
        in_specs.append(pl.BlockSpec((tm, tn), lambda i, p, j: (i, p * nj + j)))
        args.append(resid)

    def body(*refs):
        acc = jnp.dot(refs[0][...], refs[1][...], preferred_element_type=F32)
        nxt = 2
        if bias is not None:
            acc = acc + refs[nxt][...]
            nxt += 1
        if resid is not None:
            acc = refs[nxt][...] + acc
        refs[-1][...] = acc.astype(refs[-1].dtype)

    return _call(
        body, name="mm_nn", grid=(m // tm, p_n, nj), in_specs=in_specs,
        out_specs=pl.BlockSpec((tm, tn), lambda i, p, j: (i, p * nj + j)),
        out_shape=SDS((m, p_n * n), out_dtype),
        compiler_params=_params("parallel", "parallel", "parallel"))(*args)


def _mm_nt(dy, w, l):
    m, n_all = dy.shape
    _, p_n, r, n = w.shape
    assert n_all == p_n * n
    tm = min(512, m)
    tr = r if r <= 1024 else r // 2

    def body(dy_ref, w_ref, o_ref, acc_ref):
        p = pl.program_id(2)
        part = lax.dot_general(dy_ref[...], w_ref[...], NT_DIMS, preferred_element_type=F32)

        @pl.when(p == 0)
        def _():
            acc_ref[...] = part

        @pl.when(p > 0)
        def _():
            acc_ref[...] += part

        @pl.when(p == p_n - 1)
        def _():
            o_ref[...] = acc_ref[...]

    return _call(
        body, name="mm_nt", grid=(m // tm, r // tr, p_n),
        in_specs=[pl.BlockSpec((tm, n), lambda i, j, p: (i, p)),
                  pl.BlockSpec((None, None, tr, n), lambda i, j, p: (l, p, j, 0))],
        out_specs=pl.BlockSpec((tm, tr), lambda i, j, p: (i, j)),
        out_shape=SDS((m, r), F32), scratch_shapes=[pltpu.VMEM((tm, tr), F32)],
        compiler_params=_params("parallel", "parallel", "arbitrary"))(dy, w)


def _mm_tn(a, dy, p_n, n_layers, l, buf=None):
    m, k = a.shape
    n = dy.shape[1] // p_n
    tm = min(512, m)
    tk = k if k <= 512 else k // 2
    nm = m // tm

    def body(a_ref, dy_ref, *rest):
        o_ref, acc_ref = rest[-2], rest[-1]
        mi = pl.program_id(2)
        part = lax.dot_general(a_ref[...], dy_ref[...], TN_DIMS, preferred_element_type=F32)

        @pl.when(mi == 0)
        def _():
            acc_ref[...] = part

        @pl.when(mi > 0)
        def _():
            acc_ref[...] += part

        @pl.when(mi == nm - 1)
        def _():
            o_ref[...] = acc_ref[...].astype(o_ref.dtype)

    in_specs = [pl.BlockSpec((tm, tk), lambda p, kk, mi: (mi, kk)),
                pl.BlockSpec((tm, n), lambda p, kk, mi: (mi, p))]
    args = [a, dy]
    aliases = {}
    if buf is not None:
        in_specs.append(pl.BlockSpec(memory_space=pl.ANY))
        args.append(buf)
        aliases = {2: 0}
    return _call(
        body, name="mm_tn", grid=(p_n, k // tk, nm), in_specs=in_specs,
        out_specs=pl.BlockSpec((None, None, tk, n), lambda p, kk, mi: (l, p, kk, 0)),
        out_shape=SDS((n_layers, p_n, k, n), BF), scratch_shapes=[pltpu.VMEM((tk, n), F32)],
        input_output_aliases=aliases,
        compiler_params=_params("parallel", "parallel", "arbitrary"))(*args)


def _loss_grad(y, tgt):
    t, d = y.shape
    tm = min(512, t)

    def body(y_ref, t_ref, l_ref, d_ref, db_ref):
        err = y_ref[...] - t_ref[...]
        dy = err * (1.0 / d)
        d_ref[...] = dy
        db_ref[...] = dy.astype(BF)
        part = 0.5 * jnp.sum(jnp.sum(err * err, axis=1, keepdims=True) * (1.0 / d), axis=0, keepdims=True)

        @pl.when(pl.program_id(0) == 0)
        def _():
            l_ref[...] = jnp.zeros_like(l_ref)

        l_ref[...] += jnp.broadcast_to(part, l_ref.shape)

    row = pl.BlockSpec((tm, d), lambda i: (i, 0))
    return _call(
        body, name="loss_grad", grid=(t // tm,), in_specs=[row, row],
        out_specs=[pl.BlockSpec((1, LANES), lambda i: (0, 0)), row, row],
        out_shape=[SDS((1, LANES), F32), SDS((t, d), F32), SDS((t, d), BF)],
        compiler_params=_params("arbitrary"))(y, tgt)


def _prev_halo(tr, halo, col):
    return lambda i: (jnp.maximum(i * (tr // halo) - 1, 0), col)


def _next_halo(tr, halo, n_rows, col):
    return lambda i: (jnp.minimum((i + 1) * (tr // halo), n_rows // halo - 1), col)


def _ffn_mid_fwd(up, w_dw, b_dw):
    t, f2 = up.shape
    f = f2 // 2
    tr = min(256, t)
    h = FFN_HALO

    def body(g_ref, gp_ref, v_ref, w_ref, b_ref, o_ref, xp_ref):
        i = pl.program_id(0)
        xp_ref[pl.ds(0, h), :] = jnp.where(i > 0, gp_ref[...], 0.0)
        xp_ref[pl.ds(h, tr), :] = g_ref[...]

        def strip(c, carry):
            col = pl.ds(pl.multiple_of(c * LANES, LANES), LANES)
            gc = jnp.broadcast_to(b_ref[:, col], (tr, LANES))
            for k in range(FFN_K):
                gc = gc + w_ref[pl.ds(k, 1), col] * xp_ref[pl.ds(h - (FFN_K - 1 - k), tr), col]
            o_ref[:, col] = (gc * jax.nn.sigmoid(gc) * v_ref[:, col]).astype(o_ref.dtype)
            return carry

        lax.fori_loop(0, f // LANES, strip, 0)

    return _call(
        body, name="ffn_mid_fwd", grid=(t // tr,),
        in_specs=[pl.BlockSpec((tr, f), lambda i: (i, 0)), pl.BlockSpec((h, f), _prev_halo(tr, h, 0)),
                  pl.BlockSpec((tr, f), lambda i: (i, 1)),
                  pl.BlockSpec((8, f), lambda i: (0, 0)), pl.BlockSpec((1, f), lambda i: (0, 0))],
        out_specs=pl.BlockSpec((tr, f), lambda i: (i, 0)), out_shape=SDS((t, f), BF),
        scratch_shapes=[pltpu.VMEM((h + tr, f), F32)], compiler_params=_params("parallel"))(up, up, up, w_dw, b_dw)


def _ffn_mid_bwd(up, da, w_dw, b_dw):
    t, f2 = up.shape
    f = f2 // 2
    tr = min(256, t)
    h = FFN_HALO
    n_tiles = t // tr

    def body(g_ref, gp_ref, gn_ref, v_ref, vn_ref, da_ref, dan_ref, w_ref, b_ref, dup_ref, dw_ref, db_ref, xg_ref, dgc_ref):
        i = pl.program_id(0)
        xg_ref[pl.ds(0, h), :] = jnp.where(i > 0, gp_ref[...], 0.0)
        xg_ref[pl.ds(h, tr), :] = g_ref[...]
        xg_ref[pl.ds(h + tr, h), :] = gn_ref[...]
        last = i == n_tiles - 1

        @pl.when(i == 0)
        def _():
            dw_ref[...] = jnp.zeros_like(dw_ref)
            db_ref[...] = jnp.zeros_like(db_ref)

        def dsilu_gate(rows0, n_rows, dav, vv, col):
            gc = jnp.broadcast_to(b_ref[:, col], (n_rows, LANES))
            for k in range(FFN_K):
                gc = gc + w_ref[pl.ds(k, 1), col] * xg_ref[pl.ds(h + rows0 - (FFN_K - 1 - k), n_rows), col]
            sg = jax.nn.sigmoid(gc)
            return gc * sg, dav * vv * (sg * (1.0 + gc * (1.0 - sg)))

        def strip(c, carry):
            col = pl.ds(pl.multiple_of(c * LANES, LANES), LANES)
            dav = da_ref[:, col]
            silu_gc, dgc = dsilu_gate(0, tr, dav, v_ref[:, col], col)
            dup_ref[:, pl.ds(pl.multiple_of(f + c * LANES, LANES), LANES)] = (dav * silu_gc).astype(dup_ref.dtype)
            dgc_ref[pl.ds(0, tr), col] = dgc
            _, dgc_next = dsilu_gate(tr, h, dan_ref[:, col], vn_ref[:, col], col)
            dgc_ref[pl.ds(tr, h), col] = jnp.where(last, 0.0, dgc_next)
            dg = jnp.zeros((tr, LANES), F32)
            for k in range(FFN_K):
                s = FFN_K - 1 - k
                dg = dg + w_ref[pl.ds(k, 1), col] * dgc_ref[pl.ds(s, tr), col]
                dw_ref[pl.ds(k, 1), col] += jnp.sum(xg_ref[pl.ds(h - s, tr), col] * dgc, axis=0, keepdims=True)
            dup_ref[:, col] = dg.astype(dup_ref.dtype)
            db_ref[:, col] += jnp.sum(dgc, axis=0, keepdims=True)
            return carry

        lax.fori_loop(0, f // LANES, strip, 0)

    tile = lambda col: pl.BlockSpec((tr, f), lambda i: (i, col))
    nxt = lambda col: pl.BlockSpec((h, f), _next_halo(tr, h, t, col))
    return _call(
        body, name="ffn_mid_bwd", grid=(n_tiles,),
        in_specs=[tile(0), pl.BlockSpec((h, f), _prev_halo(tr, h, 0)), nxt(0), tile(1), nxt(1), tile(0), nxt(0),
                  pl.BlockSpec((8, f), lambda i: (0, 0)), pl.BlockSpec((1, f), lambda i: (0, 0))],
        out_specs=[pl.BlockSpec((tr, f2), lambda i: (i, 0)), pl.BlockSpec((8, f), lambda i: (0, 0)),
                   pl.BlockSpec((1, f), lambda i: (0, 0))],
        out_shape=[SDS((t, f2), BF), SDS((8, f), F32), SDS((1, f), F32)],
        scratch_shapes=[pltpu.VMEM((h + tr + h, f), F32), pltpu.VMEM((tr + h, f), F32)],
        compiler_params=_params("arbitrary"))(up, up, up, up, up, da, da, w_dw, b_dw)


def _ln_silu(yc, g, b):
    mu = jnp.mean(yc, axis=-1, keepdims=True)
    xc = yc - mu
    y = xc * lax.rsqrt(jnp.mean(xc * xc, axis=-1, keepdims=True) + NORM_EPS)
    return jax.nn.silu(y * g + b)


def _conv_rows(w_ref, b_ref, src_ref, src_row0, n_rows, col):
    acc = jnp.broadcast_to(b_ref[:, col], (n_rows, LANES))
    for k in range(CONV_K):
        acc = acc + w_ref[pl.ds(k, 1), col] * src_ref[pl.ds(src_row0 - (CONV_K - 1 - k), n_rows), col]
    return acc


def _conf_mid_fwd(p1, w_dw, b_dw, ln_g, ln_b):
    t, w2 = p1.shape
    w = w2 // 2
    tr = min(256, t)
    h = CONV_HALO
    rc = 32

    def body(a_ref, ap_ref, g_ref, gp_ref, w_ref, b_ref, lg_ref, lb_ref, o_ref, yg_ref, yc_ref):
        i = pl.program_id(0)

        def strip(c, carry):
            col = pl.ds(pl.multiple_of(c * LANES, LANES), LANES)
            yg_ref[pl.ds(0, h), col] = jnp.where(i > 0, ap_ref[:, col] * jax.nn.sigmoid(gp_ref[:, col]), 0.0)
            yg_ref[pl.ds(h, tr), col] = a_ref[:, col] * jax.nn.sigmoid(g_ref[:, col])
            yc_ref[:, col] = _conv_rows(w_ref, b_ref, yg_ref, h, tr, col)
            return carry

        lax.fori_loop(0, w // LANES, strip, 0)

        def rows(r, carry):
            rs = pl.ds(pl.multiple_of(r * rc, rc), rc)
            o_ref[rs, :] = _ln_silu(yc_ref[rs, :], lg_ref[...], lb_ref[...]).astype(o_ref.dtype)
            return carry

        lax.fori_loop(0, tr // rc, rows, 0)

    vec = pl.BlockSpec((1, w), lambda i: (0, 0))
    return _call(
        body, name="conf_mid_fwd", grid=(t // tr,),
        in_specs=[pl.BlockSpec((tr, w), lambda i: (i, 0)), pl.BlockSpec((h, w), _prev_halo(tr, h, 0)),
                  pl.BlockSpec((tr, w), lambda i: (i, 1)), pl.BlockSpec((h, w), _prev_halo(tr, h, 1)),
                  pl.BlockSpec((32, w), lambda i: (0, 0)), vec, vec, vec],
        out_specs=pl.BlockSpec((tr, w), lambda i: (i, 0)), out_shape=SDS((t, w), BF),
        scratch_shapes=[pltpu.VMEM((h + tr, w), F32), pltpu.VMEM((tr, w), F32)],
        compiler_params=_params("parallel"))(p1, p1, p1, p1, w_dw, b_dw, ln_g, ln_b)


def _conf_mid_bwd(p1, dys, dy, w_dw, b_dw, ln_g, ln_b):
    t, w2 = p1.shape
    w = w2 // 2
    tr = min(256, t)
    h = CONV_HALO
    rc = 32
    n_tiles = t // tr

    def body(a_ref, ap_ref, an_ref, g_ref, gp_ref, gn_ref, dys_ref, dysn_ref, dy_ref, w_ref, b_ref, lg_ref, lb_ref,
             dp_ref, dw_ref, db_ref, dlg_ref, dlb_ref, db1_ref, db2_ref, yg_ref, yc_ref, dyc_ref):
        i = pl.program_id(0)
        last = i == n_tiles - 1

        @pl.when(i == 0)
        def _():
            for ref in (dw_ref, db_ref, dlg_ref, dlb_ref, db1_ref, db2_ref):
                ref[...] = jnp.zeros_like(ref)

        def glu_conv(c, carry):
            col = pl.ds(pl.multiple_of(c * LANES, LANES), LANES)
            yg_ref[pl.ds(0, h), col] = jnp.where(i > 0, ap_ref[:, col] * jax.nn.sigmoid(gp_ref[:, col]), 0.0)
            yg_ref[pl.ds(h, tr), col] = a_ref[:, col] * jax.nn.sigmoid(g_ref[:, col])
            yg_ref[pl.ds(h + tr, h), col] = an_ref[:, col] * jax.nn.sigmoid(gn_ref[:, col])
            yc_ref[:, col] = _conv_rows(w_ref, b_ref, yg_ref, h, tr + h, col)
            return carry

        lax.fori_loop(0, w // LANES, glu_conv, 0)

        def ln_rows(r, carry):
            rs = pl.ds(pl.multiple_of(r * rc, rc), rc)
            _, vjp = jax.vjp(_ln_silu, yc_ref[rs, :], lg_ref[...], lb_ref[...])
            dyc, dlg, dlb = vjp(dys_ref[rs, :])
            dyc_ref[rs, :] = dyc
            dlg_ref[...] += dlg
            dlb_ref[...] += dlb
            return carry

        lax.fori_loop(0, tr // rc, ln_rows, 0)
        halo_rows = pl.ds(tr, h)
        _, vjp = jax.vjp(_ln_silu, yc_ref[halo_rows, :], lg_ref[...], lb_ref[...])
        dyc_ref[halo_rows, :] = jnp.where(last, 0.0, vjp(dysn_ref[...])[0])
        db2_ref[...] += jnp.sum(dy_ref[...], axis=0, keepdims=True)

        def back(c, carry):
            col = pl.ds(pl.multiple_of(c * LANES, LANES), LANES)
            gcol = pl.ds(pl.multiple_of(w + c * LANES, LANES), LANES)
            dyc = dyc_ref[pl.ds(0, tr), col]
            dyg = jnp.zeros((tr, LANES), F32)
            for k in range(CONV_K):
                s = CONV_K - 1 - k
                dyg = dyg + w_ref[pl.ds(k, 1), col] * dyc_ref[pl.ds(s, tr), col]
                dw_ref[pl.ds(k, 1), col] += jnp.sum(yg_ref[pl.ds(h - s, tr), col] * dyc, axis=0, keepdims=True)
            db_ref[:, col] += jnp.sum(dyc, axis=0, keepdims=True)
            sg = jax.nn.sigmoid(g_ref[:, col])
            da = dyg * sg
            dg = dyg * a_ref[:, col] * sg * (1.0 - sg)
            dp_ref[:, col] = da.astype(dp_ref.dtype)
            dp_ref[:, gcol] = dg.astype(dp_ref.dtype)
            db1_ref[:, col] += jnp.sum(da, axis=0, keepdims=True)
            db1_ref[:, gcol] += jnp.sum(dg, axis=0, keepdims=True)
            return carry

        lax.fori_loop(0, w // LANES, back, 0)

    tile = lambda col: pl.BlockSpec((tr, w), lambda i: (i, col))
    prv = lambda col: pl.BlockSpec((h, w), _prev_halo(tr, h, col))
    nxt = lambda col: pl.BlockSpec((h, w), _next_halo(tr, h, t, col))
    vec = pl.BlockSpec((1, w), lambda i: (0, 0))
    return _call(
        body, name="conf_mid_bwd", grid=(n_tiles,),
        in_specs=[tile(0), prv(0), nxt(0), tile(1), prv(1), nxt(1), tile(0), nxt(0), tile(0),
                  pl.BlockSpec((32, w), lambda i: (0, 0)), vec, vec, vec],
        out_specs=[pl.BlockSpec((tr, w2), lambda i: (i, 0)), pl.BlockSpec((32, w), lambda i: (0, 0)), vec, vec, vec,
                   pl.BlockSpec((1, w2), lambda i: (0, 0)), vec],
        out_shape=[SDS((t, w2), BF), SDS((32, w), F32), SDS((1, w), F32), SDS((1, w), F32), SDS((1, w), F32),
                   SDS((1, w2), F32), SDS((1, w), F32)],
        scratch_shapes=[pltpu.VMEM((h + tr + h, w), F32), pltpu.VMEM((tr + h, w), F32), pltpu.VMEM((tr + h, w), F32)],
        compiler_params=_params("arbitrary"))(p1, p1, p1, p1, p1, p1, dys, dys, dy, w_dw, b_dw, ln_g, ln_b)


def _group_matrices():
    i = lax.broadcasted_iota(jnp.int32, (512, 512), 0)
    j = lax.broadcasted_iota(jnp.int32, (512, 512), 1)
    mean64 = jnp.where(i // HEAD_DIM == j // HEAD_DIM, 1.0 / HEAD_DIM, 0.0).astype(F32)
    fold64 = jnp.where(i % HEAD_DIM == j % HEAD_DIM, 1.0, 0.0).astype(F32)
    return mean64, fold64


def _prep_tile(proj, qg, kg, zg, ws, bexp, mean64):
    sw = 512
    q, k, v, u, z = (proj[:, n * sw:(n + 1) * sw] for n in range(5))

    def group_norm(x):
        ms = jnp.dot(x * x, mean64, precision=HI, preferred_element_type=F32)
        return x * lax.rsqrt(ms + NORM_EPS)

    qn = group_norm(q) * qg
    kn = group_norm(k) * kg
    zn = group_norm(_gelu(z)) * zg
    row = lax.broadcasted_iota(jnp.int32, (CHUNK, CHUNK), 0)
    col = lax.broadcasted_iota(jnp.int32, (CHUNK, CHUNK), 1)
    first = lax.broadcasted_iota(jnp.int32, (1, LANES), 1) < HEAD_DIM
    parts = []
    for pr in range(sw // LANES):
        zp = zn[:, pr * LANES:(pr + 1) * LANES]
        s0 = jnp.dot(jnp.where(col <= row, ws[2 * pr], 0.0), zp, preferred_element_type=F32)
        s1 = jnp.dot(jnp.where(col <= row, ws[2 * pr + 1], 0.0), zp, preferred_element_type=F32)
        parts.append(jnp.where(first, s0, s1))
    s = jnp.concatenate(parts, axis=1) + bexp
    return qn, kn, v, _gelu(u) * s


def _mix_prep_fwd(proj, qg, kg, zg, w_s, l, bexp, mean64):
    t = proj.shape[0]
    tr = CHUNK

    def body(p_ref, qg_ref, kg_ref, zg_ref, ws_ref, be_ref, m_ref, qkv_ref, go_ref):
        qn, kn, v, go = _prep_tile(p_ref[...], qg_ref[...], kg_ref[...], zg_ref[...], ws_ref[...], be_ref[...], m_ref[...])
        qkv_ref[:, 0:512] = qn.astype(BF)
        qkv_ref[:, 512:1024] = kn.astype(BF)
        qkv_ref[:, 1024:1536] = v.astype(BF)
        go_ref[...] = go.astype(BF)

    vec = pl.BlockSpec((1, 512), lambda i: (0, 0))
    return _call(
        body, name="mix_prep_fwd", grid=(t // tr,),
        in_specs=[pl.BlockSpec((tr, 2560), lambda i: (i, 0)), vec, vec, vec,
                  pl.BlockSpec((None, 8, CHUNK, CHUNK), lambda i: (l, 0, 0, 0)),
                  pl.BlockSpec((CHUNK, 512), lambda i: (0, 0)), pl.BlockSpec((512, 512), lambda i: (0, 0))],
        out_specs=[pl.BlockSpec((tr, 1536), lambda i: (i, 0)), pl.BlockSpec((tr, 512), lambda i: (i, 0))],
        out_shape=[SDS((t, 1536), BF), SDS((t, 512), BF)],
        compiler_params=_params("parallel"))(proj, qg, kg, zg, w_s, bexp, mean64)


def _mix_prep_bwd(proj, dq, dk, dv, dmix, qg, kg, zg, w_s, l, bexp, mean64, fold64):
    t = proj.shape[0]
    tr = CHUNK
    n_tiles = t // tr

    def body(p_ref, dq_ref, dk_ref, dv_ref, dgo_ref, qg_ref, kg_ref, zg_ref, ws_ref, be_ref, m_ref, f_ref,
             dp_ref, dqg_ref, dkg_ref, dzg_ref, dws_ref, dbe_ref):
        i = pl.program_id(0)

        @pl.when(i == 0)
        def _():
            for ref in (dqg_ref, dkg_ref, dzg_ref, dws_ref, dbe_ref):
                ref[...] = jnp.zeros_like(ref)

        fn = functools.partial(_prep_tile, mean64=m_ref[...])
        _, vjp = jax.vjp(fn, p_ref[...], qg_ref[...], kg_ref[...], zg_ref[...], ws_ref[...], be_ref[...])
        dp, dqg, dkg, dzg, dws, dbe = vjp((dq_ref[...], dk_ref[...], dv_ref[...], dgo_ref[...]))
        dp_ref[...] = dp.astype(BF)
        dqg_ref[pl.ds(0, 1), :] += dqg
        dkg_ref[pl.ds(0, 1), :] += dkg
        dzg_ref[pl.ds(0, 1), :] += dzg
        dws_ref[...] += dws
        dbe_ref[...] += dbe

        @pl.when(i == n_tiles - 1)
        def _():
            dqg_ref[...] = jnp.dot(dqg_ref[...], f_ref[...], precision=HI, preferred_element_type=F32)
            dkg_ref[...] = jnp.dot(dkg_ref[...], f_ref[...], precision=HI, preferred_element_type=F32)
            dbe_ref[...] = jnp.dot(dbe_ref[...], m_ref[...] * float(HEAD_DIM), precision=HI, preferred_element_type=F32)

    vec = pl.BlockSpec((1, 512), lambda i: (0, 0))
    acc = pl.BlockSpec((8, 512), lambda i: (0, 0))
    sq = pl.BlockSpec((512, 512), lambda i: (0, 0))
    row = pl.BlockSpec((tr, 512), lambda i: (i, 0))
    return _call(
        body, name="mix_prep_bwd", grid=(n_tiles,),
        in_specs=[pl.BlockSpec((tr, 2560), lambda i: (i, 0)), row, row, row, pl.BlockSpec((tr, 512), lambda i: (i, 1)),
                  vec, vec, vec, pl.BlockSpec((None, 8, CHUNK, CHUNK), lambda i: (l, 0, 0, 0)),
                  pl.BlockSpec((CHUNK, 512), lambda i: (0, 0)), sq, sq],
        out_specs=[pl.BlockSpec((tr, 2560), lambda i: (i, 0)), acc, acc, acc,
                   pl.BlockSpec((8, CHUNK, CHUNK), lambda i: (0, 0, 0)), pl.BlockSpec((CHUNK, 512), lambda i: (0, 0))],
        out_shape=[SDS((t, 2560), BF), SDS((8, 512), F32), SDS((8, 512), F32), SDS((8, 512), F32),
                   SDS((8, CHUNK, CHUNK), F32), SDS((CHUNK, 512), F32)],
        compiler_params=_params("arbitrary"))(proj, dq, dk, dv, dmix, qg, kg, zg, w_s, bexp, mean64, fold64)


def _sb_block(qh, kb, valid, upper, run):
    z = lax.dot_general(qh, kb, NT_DIMS, preferred_element_type=F32) * (HEAD_DIM ** -0.5)
    soft = jnp.log1p(jnp.exp(-jnp.abs(z)))
    lk_raw = -(jnp.maximum(z, 0.0) + soft)
    ls = -(jnp.maximum(-z, 0.0) + soft)
    lk = jnp.where(valid, lk_raw, 0.0)
    tail = jnp.dot(lk, upper, precision=HI, preferred_element_type=F32)
    wgt = jnp.where(valid, jnp.exp(ls + run + tail), 0.0)
    return lk_raw, ls, lk, wgt


def _att_masks(b):
    row = lax.broadcasted_iota(jnp.int32, (b, b), 0)
    col = lax.broadcasted_iota(jnp.int32, (b, b), 1)
    first = lax.broadcasted_iota(jnp.int32, (1, LANES), 1) < HEAD_DIM
    return row, col, first


def _attn_fwd(qkv):
    t = qkv.shape[0]
    b = ATT_BLOCK
    nq = t // b

    def body(q_ref, k_ref, v_ref, ob_ref, o32_ref, acc_ref, run_ref):
        qi = pl.program_id(1)
        row, col, first = _att_masks(b)
        q = q_ref[...]
        zero = jnp.zeros_like(q)
        qh = (jnp.where(first, q, zero), jnp.where(first, zero, q))
        upper = jnp.where(row > col, 1.0, 0.0).astype(F32)
        acc_ref[...] = jnp.zeros_like(acc_ref)
        run_ref[...] = jnp.zeros_like(run_ref)

        def step(carry):
            j, _ = carry
            rows = pl.ds(pl.multiple_of(j * b, b), b)
            kb = k_ref[rows, :]
            vb = v_ref[rows, :]
            valid = jnp.logical_or(j != qi, col < row)
            alive = jnp.bool_(False)
            for h in range(2):
                run = run_ref[h]
                _, _, lk, wgt = _sb_block(qh[h], kb, valid, upper, run)
                acc_ref[h] += jnp.dot(wgt.astype(BF), vb, preferred_element_type=F32)
                run = run + jnp.sum(lk, axis=1, keepdims=True)
                run_ref[h] = run
                alive = jnp.logical_or(alive, jnp.max(jnp.exp(run)) > 0.0)
            return j - 1, alive

        lax.while_loop(lambda c: jnp.logical_and(c[0] >= 0, c[1]), step, (qi, jnp.bool_(True)))
        out = jnp.where(first, acc_ref[0], acc_ref[1])
        ob_ref[...] = out.astype(BF)
        o32_ref[...] = out

    return _call(
        body, name="attn_fwd", grid=(4, nq),
        in_specs=[pl.BlockSpec((b, LANES), lambda pr, qi: (qi, pr)),
                  pl.BlockSpec((t, LANES), lambda pr, qi: (0, 4 + pr)),
                  pl.BlockSpec((t, LANES), lambda pr, qi: (0, 8 + pr))],
        out_specs=[pl.BlockSpec((b, LANES), lambda pr, qi: (qi, pr)), pl.BlockSpec((b, LANES), lambda pr, qi: (qi, pr))],
        out_shape=[SDS((t, 512), BF), SDS((t, 512), F32)],
        scratch_shapes=[pltpu.VMEM((2, b, LANES), F32), pltpu.VMEM((2, b, 1), F32)],
        compiler_params=_params("parallel", "parallel"))(qkv, qkv, qkv)


def _attn_bwd(qkv, a32, dmix):
    t = qkv.shape[0]
    b = ATT_BLOCK
    nq = t // b

    def body(q_ref, k_ref, v_ref, a_ref, da_ref, dq_ref, dk_ref, dv_ref, dqa_ref, run_ref, rung_ref):
        qi = pl.program_id(1)
        row, col, first = _att_masks(b)

        @pl.when(qi == 0)
        def _():
            dk_ref[...] = jnp.zeros_like(dk_ref)
            dv_ref[...] = jnp.zeros_like(dv_ref)

        q = q_ref[...]
        zero = jnp.zeros_like(q)
        qh = (jnp.where(first, q, zero), jnp.where(first, zero, q))
        da = da_ref[...]
        prod = da * a_ref[...]
        dtot = (jnp.sum(jnp.where(first, prod, 0.0), axis=1, keepdims=True),
                jnp.sum(jnp.where(first, 0.0, prod), axis=1, keepdims=True))
        dah = (jnp.where(first, da, 0.0).astype(BF), jnp.where(first, 0.0, da).astype(BF))
        upper = jnp.where(row > col, 1.0, 0.0).astype(F32)
        lower_incl = jnp.where(row >= col, 1.0, 0.0).astype(F32)
        dqa_ref[...] = jnp.zeros_like(dqa_ref)
        run_ref[...] = jnp.zeros_like(run_ref)
        rung_ref[...] = jnp.zeros_like(rung_ref)

        def step(carry):
            j, _ = carry
            rows = pl.ds(pl.multiple_of(j * b, b), b)
            kb = k_ref[rows, :]
            vb = v_ref[rows, :]
            valid = jnp.logical_or(j != qi, col < row)
            alive = jnp.bool_(False)
            for h in range(2):
                run = run_ref[h]
                rung = rung_ref[h]
                lk_raw, ls, lk, wgt = _sb_block(qh[h], kb, valid, upper, run)
                dp = lax.dot_general(dah[h], vb, NT_DIMS, preferred_element_type=F32)
                g = wgt * dp
                g_from = jnp.dot(g, lower_incl, precision=HI, preferred_element_type=F32)
                dlk = jnp.where(valid, dtot[h] - rung - g_from, 0.0)
                dz = ((g * jnp.exp(lk_raw) - dlk * jnp.exp(ls)) * (HEAD_DIM ** -0.5)).astype(BF)
                dqa_ref[h] += jnp.dot(dz, kb, preferred_element_type=F32)
                dk_ref[rows, :] += lax.dot_general(dz, qh[h], TN_DIMS, preferred_element_type=F32)
                dv_ref[rows, :] += lax.dot_general(wgt.astype(BF), dah[h], TN_DIMS, preferred_element_type=F32)
                rung_ref[h] = rung + jnp.sum(g, axis=1, keepdims=True)
                run = run + jnp.sum(lk, axis=1, keepdims=True)
                run_ref[h] = run
                alive = jnp.logical_or(alive, jnp.max(jnp.exp(run)) > 0.0)
            return j - 1, alive

        lax.while_loop(lambda c: jnp.logical_and(c[0] >= 0, c[1]), step, (qi, jnp.bool_(True)))
        dq_ref[...] = jnp.where(first, dqa_ref[0], dqa_ref[1])

    blk = lambda off: pl.BlockSpec((b, LANES), lambda pr, qi: (qi, off + pr))
    full = lambda off: pl.BlockSpec((t, LANES), lambda pr, qi: (0, off + pr))
    return _call(
        body, name="attn_bwd", grid=(4, nq),
        in_specs=[blk(0), full(4), full(8), blk(0), blk(0)],
        out_specs=[blk(0), full(0), full(0)],
        out_shape=[SDS((t, 512), F32), SDS((t, 512), F32), SDS((t, 512), F32)],
        scratch_shapes=[pltpu.VMEM((2, b, LANES), F32), pltpu.VMEM((2, b, 1), F32), pltpu.VMEM((2, b, 1), F32)],
        compiler_params=_params("arbitrary", "arbitrary"))(qkv, qkv, qkv, a32, dmix)


def _adamw(w, g, m, v):
    n, c = w.shape
    tr = min(256, n)
    assert n % tr == 0

    def body(w_ref, g_ref, m_ref, v_ref, d_ref, nm_ref, nv_ref):
        g = g_ref[...]
        m = ADAM_B1 * m_ref[...] + (1.0 - ADAM_B1) * g
        v = ADAM_B2 * v_ref[...] + (1.0 - ADAM_B2) * jnp.square(g)
        m_hat = m / (1.0 - ADAM_B1 ** ADAM_STEP)
        v_hat = v / (1.0 - ADAM_B2 ** ADAM_STEP)
        d_ref[...] = -ADAM_LR * (m_hat / (jnp.sqrt(v_hat) + ADAM_EPS) + ADAM_WD * w_ref[...])
        nm_ref[...] = m
        nv_ref[...] = v

    blk = pl.BlockSpec((tr, c), lambda i: (i, 0))
    return _call(
        body, name="adamw", grid=(n // tr,), in_specs=[blk] * 4, out_specs=[blk] * 3,
        out_shape=[SDS((n, c), F32)] * 3, compiler_params=_params("parallel"))(w, g, m, v)


def _mesh_pos():
    return lax.axis_index("x"), lax.axis_index("y"), lax.axis_index("c")


def _other_chips(x, y):
    return [(1 - x, y), (x, 1 - y), (1 - x, 1 - y)]


HBM_SPEC = pl.BlockSpec(memory_space=pltpu.HBM)


def _all_gather(shard):
    n_l, r, c_w = shard.shape
    h = r // 2

    def body(s_ref, o_ref, send_sems, recv_sems, local_sem):
        x, y, c = _mesh_pos()
        sibling = (x, y, 1 - c)
        chips = _other_chips(x, y)

        def half(px, py, hc):
            return o_ref.at[:, 2 * px + py, pl.ds(hc * h, h), :]

        def copy(k, dst, to, src=None):
            return pltpu.make_async_remote_copy(
                src_ref=dst if src is None else src, dst_ref=dst, send_sem=send_sems.at[k], recv_sem=recv_sems.at[k],
                device_id=to, device_id_type=MESH)

        mine = pltpu.make_async_copy(s_ref, o_ref.at[:, 2 * x + y], local_sem)
        mine.start()
        first = [copy(j, half(x, y, c), (*chip, c), src=s_ref.at[:, pl.ds(c * h, h), :]) for j, chip in enumerate(chips)]
        for cp in first:
            cp.start()
        passed = [copy(3 + j, half(*chip, c), sibling) for j, chip in enumerate(chips)]
        for j, chip in enumerate(chips):
            copy(j, half(*chip, c), (x, y, c)).wait_recv()
            passed[j].start()
        for j, chip in enumerate(chips):
            copy(3 + j, half(*chip, 1 - c), (x, y, c)).wait_recv()
        for cp in first + passed:
            cp.wait_send()
        mine.wait()

    return _call(
        body, name="all_gather", in_specs=[HBM_SPEC], out_specs=HBM_SPEC,
        out_shape=SDS((n_l, N_CHIPS, r, c_w), shard.dtype),
        scratch_shapes=[pltpu.SemaphoreType.DMA((6,)), pltpu.SemaphoreType.DMA((6,)), pltpu.SemaphoreType.DMA],
        )(shard)


def _row_tile(h):
    for cand in (256, 176, 128, 64, 32, 16):
        if h % cand == 0:
            return cand
    raise ValueError(h)


def _reduce_scatter(g, mid_dtype):
    n_l, n_p, r, c_w = g.shape
    h = r // 2
    tr = _row_tile(h)
    nt = h // tr
    x, y, c = _mesh_pos()
    c_arr = jnp.reshape(c, (1,)).astype(jnp.int32)
    p_arr = jnp.reshape(2 * x + y, (1,)).astype(jnp.int32)

    def to_sibling_body(g_ref, a_ref, send_sem, recv_sem):
        x, y, c = _mesh_pos()
        cp = pltpu.make_async_remote_copy(
            src_ref=g_ref.at[:, :, pl.ds((1 - c) * h, h), :], dst_ref=a_ref, send_sem=send_sem, recv_sem=recv_sem,
            device_id=(x, y, 1 - c), device_id_type=MESH)
        cp.start()
        cp.wait()

    from_sibling = _call(
        to_sibling_body, name="rs_pair", in_specs=[HBM_SPEC], out_specs=HBM_SPEC,
        out_shape=SDS((n_l, n_p, h, c_w), g.dtype),
        scratch_shapes=[pltpu.SemaphoreType.DMA, pltpu.SemaphoreType.DMA],
        )(g)

    def pair_add_body(c_ref, g_ref, a_ref, o_ref):
        o_ref[...] = (g_ref[...].astype(F32) + a_ref[...].astype(F32)).astype(o_ref.dtype)

    blk = (None, None, tr, c_w)
    pair_sum = _call(
        pair_add_body, name="rs_pair_add",
        grid_spec=pltpu.PrefetchScalarGridSpec(
            num_scalar_prefetch=1, grid=(n_l, n_p, nt),
            in_specs=[pl.BlockSpec(blk, lambda l, p, t, c_ref: (l, p, c_ref[0] * nt + t, 0)),
                      pl.BlockSpec(blk, lambda l, p, t, c_ref: (l, p, t, 0))],
            out_specs=pl.BlockSpec(blk, lambda l, p, t, c_ref: (l, p, t, 0))),
        out_shape=SDS((n_l, n_p, h, c_w), mid_dtype),
        compiler_params=_params("parallel", "parallel", "parallel"))(c_arr, g, from_sibling)

    def to_chips_body(s_ref, b_ref, send_sems, recv_sems):
        x, y, c = _mesh_pos()
        cps = [pltpu.make_async_remote_copy(
            src_ref=s_ref.at[:, 2 * chip[0] + chip[1]], dst_ref=b_ref.at[j], send_sem=send_sems.at[j],
            recv_sem=recv_sems.at[j], device_id=(*chip, c), device_id_type=MESH)
            for j, chip in enumerate(_other_chips(x, y))]
        for cp in cps:
            cp.start()
        for cp in cps:
            cp.wait()

    from_chips = _call(
        to_chips_body, name="rs_chips", in_specs=[HBM_SPEC], out_specs=HBM_SPEC,
        out_shape=SDS((3, n_l, h, c_w), mid_dtype),
        scratch_shapes=[pltpu.SemaphoreType.DMA((3,)), pltpu.SemaphoreType.DMA((3,))],
        )(pair_sum)

    def chip_add_body(p_ref, c_ref, s_ref, b_ref, o_ref):
        acc = s_ref[...].astype(F32)
        for j in range(3):
            acc = acc + b_ref[j].astype(F32)
        o_ref[...] = acc

    half_sum = _call(
        chip_add_body, name="rs_chip_add",
        grid_spec=pltpu.PrefetchScalarGridSpec(
            num_scalar_prefetch=2, grid=(n_l, nt),
            in_specs=[pl.BlockSpec((None, None, tr, c_w), lambda l, t, p_ref, c_ref: (l, p_ref[0], t, 0)),
                      pl.BlockSpec((3, None, tr, c_w), lambda l, t, p_ref, c_ref: (0, l, t, 0))],
            out_specs=pl.BlockSpec((None, tr, c_w), lambda l, t, p_ref, c_ref: (l, c_ref[0] * nt + t, 0))),
        out_shape=SDS((n_l, r, c_w), F32),
        compiler_params=_params("parallel", "parallel"))(p_arr, c_arr, pair_sum, from_chips)

    def swap_body(i_ref, o_ref, send_sem, recv_sem):
        x, y, c = _mesh_pos()
        mine = o_ref.at[:, pl.ds(c * h, h), :]
        theirs = o_ref.at[:, pl.ds((1 - c) * h, h), :]
        pltpu.make_async_remote_copy(src_ref=mine, dst_ref=mine, send_sem=send_sem, recv_sem=recv_sem,
                                     device_id=(x, y, 1 - c), device_id_type=MESH).start()
        wait = pltpu.make_async_remote_copy(src_ref=mine, dst_ref=theirs, send_sem=send_sem, recv_sem=recv_sem,
                                            device_id=(x, y, 1 - c), device_id_type=MESH)
        wait.wait_send()
        wait.wait_recv()

    return _call(
        swap_body, name="rs_swap", in_specs=[HBM_SPEC], out_specs=HBM_SPEC, out_shape=SDS((n_l, r, c_w), F32),
        input_output_aliases={0: 0},
        scratch_shapes=[pltpu.SemaphoreType.DMA, pltpu.SemaphoreType.DMA],
        )(half_sum)


def _pack(arrays, row_multiple):
    flat = jnp.concatenate([a.reshape(-1).astype(F32) for a in arrays])
    unit = row_multiple * LANES
    padded = -(-flat.shape[0] // unit) * unit
    return jnp.pad(flat, (0, padded - flat.shape[0])).reshape(padded // LANES, LANES)


def _unpack(packed, shapes):
    flat = packed.reshape(-1)
    out, pos = [], 0
    for s in shapes:
        size = 1
        for dim in s:
            size *= dim
        out.append(flat[pos:pos + size].reshape(s))
        pos += size
    return out


BIG_COL = ("sb_w_in", "cv_w_pw1", "ffn_w_up")
BIG_ROW = ("hyb_w_out", "cv_w_pw2", "ffn_w_down")
SMALL_SHARDED = ("cv_b_pw1", "cv_w_dw", "cv_b_dw", "cv_ln_g", "cv_ln_b", "cv_b_pw2", "ffn_w_dw")
SMALL_REPLICATED = ("mix_norm_g", "sb_q_norm_g", "sb_k_norm_g", "sg_z_norm_g", "sg_w_spatial", "sg_b_spatial",
                    "ffn_norm_g", "ffn_b_dw")
WEIGHTS = ("mix_norm_g", "sb_w_in", "sb_q_norm_g", "sb_k_norm_g", "sg_z_norm_g", "sg_w_spatial", "sg_b_spatial",
           "hyb_w_out", "cv_w_pw1", "cv_b_pw1", "cv_w_dw", "cv_b_dw", "cv_ln_g", "cv_ln_b", "cv_w_pw2", "cv_b_pw2",
           "ffn_norm_g", "ffn_w_up", "ffn_w_dw", "ffn_b_dw", "ffn_w_down")


def _pad_rows(a, rows):
    return jnp.pad(a, ((0, rows - a.shape[0]), (0, 0)))


def _step(x, tgt, w, m, v):
    n_layers = w["mix_norm_g"].shape[0]
    xi, yi, ci = _mesh_pos()
    chip = 2 * xi + yi

    full = {}
    for name in BIG_COL:
        full[name] = _all_gather(w[name].astype(BF))
    for name in BIG_ROW:
        g4 = _all_gather(w[name].astype(BF))
        full[name] = g4.reshape(g4.shape[0], 1, g4.shape[1] * g4.shape[2], g4.shape[3])
    small_local = [w[name] for name in SMALL_SHARDED]
    gathered = _all_gather(_pack(small_local, 32)[None])[0]
    per_chip = [_unpack(gathered[p], [a.shape for a in small_local]) for p in range(N_CHIPS)]
    for k, name in enumerate(SMALL_SHARDED):
        full[name] = jnp.concatenate([per_chip[p][k] for p in range(N_CHIPS)], axis=-1)
    for name in SMALL_REPLICATED:
        full[name] = w[name]

    mean64, fold64 = _group_matrices()
    ffn_wdw = [_pad_rows(full["ffn_w_dw"][i], 8) for i in range(n_layers)]
    cv_wdw = [_pad_rows(full["cv_w_dw"][j], 32) for j in range(n_layers // 2)]
    row = lambda a: a.reshape(1, -1)

    saved = []
    cur = x
    for i in range(n_layers):
        j = i // 2
        rec = {"x_in": cur}
        h = _rms_fwd(cur, row(full["mix_norm_g"][i]))
        rec["h_mix"] = h
        if i % 2 == 0:
            proj = _mm_nn(h, full["sb_w_in"], j)
            qg = row(jnp.tile(full["sb_q_norm_g"][j], 512 // HEAD_DIM))
            kg = row(jnp.tile(full["sb_k_norm_g"][j], 512 // HEAD_DIM))
            zg = row(full["sg_z_norm_g"][j])
            bexp = jnp.repeat(full["sg_b_spatial"][j].T, HEAD_DIM, axis=1)
            qkv, gated = _mix_prep_fwd(proj, qg, kg, zg, full["sg_w_spatial"], j, bexp, mean64)
            att_bf, att_32 = _attn_fwd(qkv)
            mix = jnp.concatenate([att_bf, gated], axis=1)
            cur = _mm_nn(mix, full["hyb_w_out"], j, resid=cur)
            rec.update(proj=proj, qkv=qkv, att_32=att_32, mix=mix, qg=qg, kg=kg, zg=zg, bexp=bexp)
        else:
            p1 = _mm_nn(h, full["cv_w_pw1"], j, bias=row(full["cv_b_pw1"][j]))
            ys = _conf_mid_fwd(p1, cv_wdw[j], row(full["cv_b_dw"][j]), row(full["cv_ln_g"][j]), row(full["cv_ln_b"][j]))
            cur = _mm_nn(ys, full["cv_w_pw2"], j, bias=row(full["cv_b_pw2"][j]), resid=cur)
            rec.update(p1=p1, ys=ys)
        rec["x_mid"] = cur
        h = _rms_fwd(cur, row(full["ffn_norm_g"][i]))
        up = _mm_nn(h, full["ffn_w_up"], i)
        act = _ffn_mid_fwd(up, ffn_wdw[i], row(full["ffn_b_dw"][i]))
        cur = _mm_nn(act, full["ffn_w_down"], i, resid=cur)
        rec.update(h_ffn=h, up=up, act=act)
        saved.append(rec)

    loss_vec, dy, dy_bf = _loss_grad(cur, tgt)
    loss = lax.psum(loss_vec[0, 0], ("x", "y", "c"))

    gbig = {name: None for name in BIG_COL + BIG_ROW}
    gsmall = {name: [None] * w[name].shape[0] for name in SMALL_SHARDED + SMALL_REPLICATED}
    n_of = {name: full[name].shape[0] for name in BIG_COL + BIG_ROW}
    for i in reversed(range(n_layers)):
        j = i // 2
        rec = saved[i]
        dact = _mm_nt(dy_bf, full["ffn_w_down"], i)
        gbig["ffn_w_down"] = _mm_tn(rec["act"], dy_bf, 1, n_of["ffn_w_down"], i, gbig["ffn_w_down"])
        dup, dwdw, dbdw = _ffn_mid_bwd(rec["up"], dact, ffn_wdw[i], row(full["ffn_b_dw"][i]))
        gsmall["ffn_w_dw"][i] = dwdw[:FFN_K]
        gsmall["ffn_b_dw"][i] = dbdw[0]
        dh = _mm_nt(dup, full["ffn_w_up"], i)
        gbig["ffn_w_up"] = _mm_tn(rec["h_ffn"], dup, N_CHIPS, n_of["ffn_w_up"], i, gbig["ffn_w_up"])
        dy, dy_bf, dg = _rms_bwd(dh, rec["x_mid"], row(full["ffn_norm_g"][i]), dy)
        gsmall["ffn_norm_g"][i] = dg[0]
        if i % 2 == 0:
            dmix = _mm_nt(dy_bf, full["hyb_w_out"], j)
            gbig["hyb_w_out"] = _mm_tn(rec["mix"], dy_bf, 1, n_of["hyb_w_out"], j, gbig["hyb_w_out"])
            dq, dk, dv = _attn_bwd(rec["qkv"], rec["att_32"], dmix)
            dproj, dqg, dkg, dzg, dws, dbe = _mix_prep_bwd(
                rec["proj"], dq, dk, dv, dmix, rec["qg"], rec["kg"], rec["zg"], full["sg_w_spatial"], j, rec["bexp"],
                mean64, fold64)
            gsmall["sb_q_norm_g"][j] = dqg[0, :HEAD_DIM]
            gsmall["sb_k_norm_g"][j] = dkg[0, :HEAD_DIM]
            gsmall["sg_z_norm_g"][j] = dzg[0]
            gsmall["sg_w_spatial"][j] = dws
            gsmall["sg_b_spatial"][j] = dbe[:, ::HEAD_DIM].T
            dh = _mm_nt(dproj, full["sb_w_in"], j)
            gbig["sb_w_in"] = _mm_tn(rec["h_mix"], dproj, N_CHIPS, n_of["sb_w_in"], j, gbig["sb_w_in"])
        else:
            dys = _mm_nt(dy_bf, full["cv_w_pw2"], j)
            gbig["cv_w_pw2"] = _mm_tn(rec["ys"], dy_bf, 1, n_of["cv_w_pw2"], j, gbig["cv_w_pw2"])
            dp1, dwdw, dbdw, dlg, dlb, db1, db2 = _conf_mid_bwd(
                rec["p1"], dys, dy, cv_wdw[j], row(full["cv_b_dw"][j]), row(full["cv_ln_g"][j]), row(full["cv_ln_b"][j]))
            gsmall["cv_w_dw"][j] = dwdw[:CONV_K]
            gsmall["cv_b_dw"][j] = dbdw[0]
            gsmall["cv_ln_g"][j] = dlg[0]
            gsmall["cv_ln_b"][j] = dlb[0]
            gsmall["cv_b_pw1"][j] = db1[0]
            gsmall["cv_b_pw2"][j] = db2[0]
            dh = _mm_nt(dp1, full["cv_w_pw1"], j)
            gbig["cv_w_pw1"] = _mm_tn(rec["h_mix"], dp1, N_CHIPS, n_of["cv_w_pw1"], j, gbig["cv_w_pw1"])
        dy, dy_bf, dg = _rms_bwd(dh, rec["x_in"], row(full["mix_norm_g"][i]), dy)
        gsmall["mix_norm_g"][i] = dg[0]

    grads = {}
    for name in BIG_COL:
        grads[name] = _reduce_scatter(gbig[name], BF)
    for name in BIG_ROW:
        g4 = gbig[name]
        r = g4.shape[2] // N_CHIPS
        grads[name] = _reduce_scatter(g4.reshape(g4.shape[0], N_CHIPS, r, g4.shape[3]), BF)
    small_names = SMALL_REPLICATED + SMALL_SHARDED
    small_full = [jnp.stack(gsmall[name]) for name in small_names]
    packed = _pack(small_full, 32 * N_CHIPS)
    rows_q = packed.shape[0] // N_CHIPS
    summed = _reduce_scatter(packed.reshape(1, N_CHIPS, rows_q, LANES), F32)
    summed = _all_gather(summed).reshape(-1, LANES)
    for name, gsum in zip(small_names, _unpack(summed, [a.shape for a in small_full])):
        if name in SMALL_SHARDED:
            n_loc = w[name].shape[-1]
            split = gsum.reshape(gsum.shape[:-1] + (N_CHIPS, n_loc))
            gsum = lax.dynamic_index_in_dim(split, chip, axis=split.ndim - 2, keepdims=False)
        grads[name] = gsum

    delta, new_m, new_v = {}, {}, {}
    for name in BIG_COL + BIG_ROW:
        shp = w[name].shape
        two_d = lambda a: a.reshape(shp[0] * shp[1], shp[2])
        d, nm, nv = _adamw(two_d(w[name]), two_d(grads[name]), two_d(m[name]), two_d(v[name]))
        delta[name], new_m[name], new_v[name] = d.reshape(shp), nm.reshape(shp), nv.reshape(shp)
    shapes = [w[name].shape for name in small_names]
    d, nm, nv = _adamw(*(_pack([src[name] for name in small_names], 256) for src in (w, grads, m, v)))
    for name, a, b_, c_ in zip(small_names, _unpack(d, shapes), _unpack(nm, shapes), _unpack(nv, shapes)):
        delta[name], new_m[name], new_v[name] = a, b_, c_

    return (loss, dy, *[grads[n] for n in WEIGHTS], *[delta[n] for n in WEIGHTS],
            *[new_m[n] for n in WEIGHTS], *[new_v[n] for n in WEIGHTS])


def kernel(x, mix_norm_g, sb_w_in, sb_q_norm_g, sb_k_norm_g, sg_z_norm_g, sg_w_spatial, sg_b_spatial, hyb_w_out, cv_w_pw1, cv_b_pw1, cv_w_dw, cv_b_dw, cv_ln_g, cv_ln_b, cv_w_pw2, cv_b_pw2, ffn_norm_g, ffn_w_up, ffn_w_dw, ffn_b_dw, ffn_w_down, loss_target, m_mix_norm_g, m_sb_w_in, m_sb_q_norm_g, m_sb_k_norm_g, m_sg_z_norm_g, m_sg_w_spatial, m_sg_b_spatial, m_hyb_w_out, m_cv_w_pw1, m_cv_b_pw1, m_cv_w_dw, m_cv_b_dw, m_cv_ln_g, m_cv_ln_b, m_cv_w_pw2, m_cv_b_pw2, m_ffn_norm_g, m_ffn_w_up, m_ffn_w_dw, m_ffn_b_dw, m_ffn_w_down, v_mix_norm_g, v_sb_w_in, v_sb_q_norm_g, v_sb_k_norm_g, v_sg_z_norm_g, v_sg_w_spatial, v_sg_b_spatial, v_hyb_w_out, v_cv_w_pw1, v_cv_b_pw1, v_cv_w_dw, v_cv_b_dw, v_cv_ln_g, v_cv_ln_b, v_cv_w_pw2, v_cv_b_pw2, v_ffn_norm_g, v_ffn_w_up, v_ffn_w_dw, v_ffn_b_dw, v_ffn_w_down):
    given = dict(locals())
    w = {n: given[n] for n in WEIGHTS}
    m = {n: given["m_" + n] for n in WEIGHTS}
    v = {n: given["v_" + n] for n in WEIGHTS}
    out = _step(x[0], loss_target[0], w, m, v)
    return (out[0], out[1][None], *out[2:])
```

```python
import functools

import jax
import jax.numpy as jnp
from jax import lax
from jax.experimental import pallas as pl
from jax.experimental.pallas import tpu as pltpu

F32 = jnp.float32
BF = jnp.bfloat16
SDS = jax.ShapeDtypeStruct
HI = lax.Precision.HIGHEST
MESH = pl.DeviceIdType.MESH

NORM_EPS = 1e-6
HEAD_DIM = 64
ATT_BLOCK = 128
CHUNK = 128
CONV_K = 31
CONV_HALO = 32
FFN_K = 3
FFN_HALO = 16
LANES = 128
N_CHIPS = 4
VMEM_LIMIT_BYTES = 56 * 2**20

ADAM_LR = 0.001
ADAM_B1 = 0.9
ADAM_B2 = 0.999
ADAM_EPS = 1e-08
ADAM_WD = 0.01
ADAM_STEP = 10

NT_DIMS = (((1,), (1,)), ((), ()))
TN_DIMS = (((0,), (0,)), ((), ()))


def _call(body, **kw):
    return pl.pallas_call(body, **kw)


def _params(*sem):
    return pltpu.CompilerParams(dimension_semantics=sem, vmem_limit_bytes=VMEM_LIMIT_BYTES)


def _gelu(x):
    return 0.5 * x * (1.0 + lax.erf(x * 0.7071067811865476))


def _rms(x, g):
    y = x * lax.rsqrt(jnp.mean(x * x, axis=-1, keepdims=True) + NORM_EPS)
    return y * g


def _rms_fwd(x, g):
    t, d = x.shape
    tm = min(512, t)

    def body(x_ref, g_ref, o_ref):
        o_ref[...] = _rms(x_ref[...], g_ref[...]).astype(o_ref.dtype)

    return _call(
        body, name="rms_fwd", grid=(t // tm,),
        in_specs=[pl.BlockSpec((tm, d), lambda i: (i, 0)), pl.BlockSpec((1, d), lambda i: (0, 0))],
        out_specs=pl.BlockSpec((tm, d), lambda i: (i, 0)),
        out_shape=SDS((t, d), BF), compiler_params=_params("parallel"))(x, g)


def _rms_bwd(dh, x, g, dres):
    t, d = x.shape
    tm = min(512, t)

    def body(dh_ref, x_ref, g_ref, r_ref, dx_ref, dxb_ref, dg_ref):
        _, vjp = jax.vjp(_rms, x_ref[...], g_ref[...])
        dx, dg = vjp(dh_ref[...])
        dx = dx + r_ref[...]
        dx_ref[...] = dx
        dxb_ref[...] = dx.astype(BF)

        @pl.when(pl.program_id(0) == 0)
        def _():
            dg_ref[...] = jnp.zeros_like(dg_ref)

        dg_ref[...] += dg

    row = pl.BlockSpec((tm, d), lambda i: (i, 0))
    vec = pl.BlockSpec((1, d), lambda i: (0, 0))
    return _call(
        body, name="rms_bwd", grid=(t // tm,), in_specs=[row, row, vec, row], out_specs=[row, row, vec],
        out_shape=[SDS((t, d), F32), SDS((t, d), BF), SDS((1, d), F32)],
        compiler_params=_params("arbitrary"))(dh, x, g, dres)


def _mm_nn(a, w, l, bias=None, resid=None, out_dtype=F32):
    m, k = a.shape
    _, p_n, kw, n = w.shape
    assert k == kw
    tm = min(1024, m)
    tn = n if k * n * 2 <= 4 * 2**20 else n // 2
    nj = n // tn
    in_specs = [pl.BlockSpec((tm, k), lambda i, p, j: (i, 0)),
                pl.BlockSpec((None, None, k, tn), lambda i, p, j: (l, p, 0, j))]
    args = [a, w]
    if bias is not None:
        in_specs.append(pl.BlockSpec((1, tn), lambda i, p, j: (0, p * nj + j)))
        args.append(bias)
    if resid is not None:
        in_specs.append(pl.BlockSpec((tm, tn), lambda i, p, j: (i, p * nj + j)))
        args.append(resid)

    def body(*refs):
        acc = jnp.dot(refs[0][...], refs[1][...], preferred_element_type=F32)
        nxt = 2
        if bias is not None:
            acc = acc + refs[nxt][...]
            nxt += 1
        if resid is not None:
            acc = refs[nxt][...] + acc
        refs[-1][...] = acc.astype(refs[-1].dtype)

    return _call(
        body, name="mm_nn", grid=(m // tm, p_n, nj), in_specs=in_specs,
        out_specs=pl.BlockSpec((tm, tn), lambda i, p, j: (i, p * nj + j)),
        out_shape=SDS((m, p_n * n), out_dtype),
        compiler_params=_params("parallel", "parallel", "parallel"))(*args)


def _mm_nt(dy, w, l, out_dtype=F32):
    m, n_all = dy.shape
    _, p_n, r, n = w.shape
    assert n_all == p_n * n
    tm = min(1024, m)
    tr = r if r <= 1024 else r // 2

    def body(dy_ref, w_ref, o_ref, acc_ref):
        p = pl.program_id(2)
        part = lax.dot_general(dy_ref[...], w_ref[...], NT_DIMS, preferred_element_type=F32)

        @pl.when(p == 0)
        def _():
            acc_ref[...] = part

        @pl.when(p > 0)
        def _():
            acc_ref[...] += part

        @pl.when(p == p_n - 1)
        def _():
            o_ref[...] = acc_ref[...].astype(o_ref.dtype)

    return _call(
        body, name="mm_nt", grid=(m // tm, r // tr, p_n),
        in_specs=[pl.BlockSpec((tm, n), lambda i, j, p: (i, p)),
                  pl.BlockSpec((None, None, tr, n), lambda i, j, p: (l, p, j, 0))],
        out_specs=pl.BlockSpec((tm, tr), lambda i, j, p: (i, j)),
        out_shape=SDS((m, r), out_dtype), scratch_shapes=[pltpu.VMEM((tm, tr), F32)],
        compiler_params=_params("parallel", "parallel", "arbitrary"))(dy, w)


def _mm_tn(a, dy, p_n, n_layers, l, buf=None):
    m, k = a.shape
    n = dy.shape[1] // p_n
    tm = min(512, m)
    tk = k if k <= 1024 else k // 2
    nm = m // tm

    def body(a_ref, dy_ref, *rest):
        o_ref, acc_ref = rest[-2], rest[-1]
        mi = pl.program_id(2)
        part = lax.dot_general(a_ref[...], dy_ref[...], TN_DIMS, preferred_element_type=F32)

        @pl.when(mi == 0)
        def _():
            acc_ref[...] = part

        @pl.when(mi > 0)
        def _():
            acc_ref[...] += part

        @pl.when(mi == nm - 1)
        def _():
            o_ref[...] = acc_ref[...].astype(o_ref.dtype)

    in_specs = [pl.BlockSpec((tm, tk), lambda p, kk, mi: (mi, kk)),
                pl.BlockSpec((tm, n), lambda p, kk, mi: (mi, p))]
    args = [a, dy]
    aliases = {}
    if buf is not None:
        in_specs.append(pl.BlockSpec(memory_space=pl.ANY))
        args.append(buf)
        aliases = {2: 0}
    return _call(
        body, name="mm_tn", grid=(p_n, k // tk, nm), in_specs=in_specs,
        out_specs=pl.BlockSpec((None, None, tk, n), lambda p, kk, mi: (l, p, kk, 0)),
        out_shape=SDS((n_layers, p_n, k, n), BF), scratch_shapes=[pltpu.VMEM((tk, n), F32)],
        input_output_aliases=aliases,
        compiler_params=_params("parallel", "parallel", "arbitrary"))(*args)


def _loss_grad(y, tgt):
    t, d = y.shape
    tm = min(512, t)

    def body(y_ref, t_ref, l_ref, d_ref, db_ref):
        err = y_ref[...] - t_ref[...]
        dy = err * (1.0 / d)
        d_ref[...] = dy
        db_ref[...] = dy.astype(BF)
        part = 0.5 * jnp.sum(jnp.sum(err * err, axis=1, keepdims=True) * (1.0 / d), axis=0, keepdims=True)

        @pl.when(pl.program_id(0) == 0)
        def _():
            l_ref[...] = jnp.zeros_like(l_ref)

        l_ref[...] += jnp.broadcast_to(part, l_ref.shape)

    row = pl.BlockSpec((tm, d), lambda i: (i, 0))
    return _call(
        body, name="loss_grad", grid=(t // tm,), in_specs=[row, row],
        out_specs=[pl.BlockSpec((1, LANES), lambda i: (0, 0)), row, row],
        out_shape=[SDS((1, LANES), F32), SDS((t, d), F32), SDS((t, d), BF)],
        compiler_params=_params("arbitrary"))(y, tgt)


def _prev_halo(tr, halo, col):
    return lambda i: (jnp.maximum(i * (tr // halo) - 1, 0), col)


def _next_halo(tr, halo, n_rows, col):
    return lambda i: (jnp.minimum((i + 1) * (tr // halo), n_rows // halo - 1), col)


def _ffn_mid_fwd(up, w_dw, b_dw):
    t, f2 = up.shape
    f = f2 // 2
    tr = min(256, t)
    h = FFN_HALO

    def body(g_ref, gp_ref, v_ref, w_ref, b_ref, o_ref, xp_ref):
        i = pl.program_id(0)
        xp_ref[pl.ds(0, h), :] = jnp.where(i > 0, gp_ref[...].astype(F32), 0.0)
        xp_ref[pl.ds(h, tr), :] = g_ref[...].astype(F32)

        def strip(c, carry):
            col = pl.ds(pl.multiple_of(c * LANES, LANES), LANES)
            gc = jnp.broadcast_to(b_ref[:, col], (tr, LANES))
            for k in range(FFN_K):
                gc = gc + w_ref[pl.ds(k, 1), col] * xp_ref[pl.ds(h - (FFN_K - 1 - k), tr), col]
            o_ref[:, col] = (gc * jax.nn.sigmoid(gc) * v_ref[:, col].astype(F32)).astype(o_ref.dtype)
            return carry

        lax.fori_loop(0, f // LANES, strip, 0)

    return _call(
        body, name="ffn_mid_fwd", grid=(t // tr,),
        in_specs=[pl.BlockSpec((tr, f), lambda i: (i, 0)), pl.BlockSpec((h, f), _prev_halo(tr, h, 0)),
                  pl.BlockSpec((tr, f), lambda i: (i, 1)),
                  pl.BlockSpec((8, f), lambda i: (0, 0)), pl.BlockSpec((1, f), lambda i: (0, 0))],
        out_specs=pl.BlockSpec((tr, f), lambda i: (i, 0)), out_shape=SDS((t, f), BF),
        scratch_shapes=[pltpu.VMEM((h + tr, f), F32)], compiler_params=_params("parallel"))(up, up, up, w_dw, b_dw)


def _ffn_mid_bwd(up, da, w_dw, b_dw):
    t, f2 = up.shape
    f = f2 // 2
    tr = min(256, t)
    h = FFN_HALO
    n_tiles = t // tr

    def body(g_ref, gp_ref, gn_ref, v_ref, vn_ref, da_ref, dan_ref, w_ref, b_ref, dup_ref, dw_ref, db_ref, xg_ref, dgc_ref):
        i = pl.program_id(0)
        xg_ref[pl.ds(0, h), :] = jnp.where(i > 0, gp_ref[...].astype(F32), 0.0)
        xg_ref[pl.ds(h, tr), :] = g_ref[...].astype(F32)
        xg_ref[pl.ds(h + tr, h), :] = gn_ref[...].astype(F32)
        last = i == n_tiles - 1

        @pl.when(i == 0)
        def _():
            dw_ref[...] = jnp.zeros_like(dw_ref)
            db_ref[...] = jnp.zeros_like(db_ref)

        def dsilu_gate(rows0, n_rows, dav, vv, col):
            gc = jnp.broadcast_to(b_ref[:, col], (n_rows, LANES))
            for k in range(FFN_K):
                gc = gc + w_ref[pl.ds(k, 1), col] * xg_ref[pl.ds(h + rows0 - (FFN_K - 1 - k), n_rows), col]
            sg = jax.nn.sigmoid(gc)
            return gc * sg, dav * vv * (sg * (1.0 + gc * (1.0 - sg)))

        def strip(c, carry):
            col = pl.ds(pl.multiple_of(c * LANES, LANES), LANES)
            dav = da_ref[:, col].astype(F32)
            silu_gc, dgc = dsilu_gate(0, tr, dav, v_ref[:, col].astype(F32), col)
            dup_ref[:, pl.ds(pl.multiple_of(f + c * LANES, LANES), LANES)] = (dav * silu_gc).astype(dup_ref.dtype)
            dgc_ref[pl.ds(0, tr), col] = dgc
            _, dgc_next = dsilu_gate(tr, h, dan_ref[:, col].astype(F32), vn_ref[:, col].astype(F32), col)
            dgc_ref[pl.ds(tr, h), col] = jnp.where(last, 0.0, dgc_next)
            dg = jnp.zeros((tr, LANES), F32)
            for k in range(FFN_K):
                s = FFN_K - 1 - k
                dg = dg + w_ref[pl.ds(k, 1), col] * dgc_ref[pl.ds(s, tr), col]
                dw_ref[pl.ds(k, 1), col] += jnp.sum(xg_ref[pl.ds(h - s, tr), col] * dgc, axis=0, keepdims=True)
            dup_ref[:, col] = dg.astype(dup_ref.dtype)
            db_ref[:, col] += jnp.sum(dgc, axis=0, keepdims=True)
            return carry

        lax.fori_loop(0, f // LANES, strip, 0)

    tile = lambda col: pl.BlockSpec((tr, f), lambda i: (i, col))
    nxt = lambda col: pl.BlockSpec((h, f), _next_halo(tr, h, t, col))
    return _call(
        body, name="ffn_mid_bwd", grid=(n_tiles,),
        in_specs=[tile(0), pl.BlockSpec((h, f), _prev_halo(tr, h, 0)), nxt(0), tile(1), nxt(1), tile(0), nxt(0),
                  pl.BlockSpec((8, f), lambda i: (0, 0)), pl.BlockSpec((1, f), lambda i: (0, 0))],
        out_specs=[pl.BlockSpec((tr, f2), lambda i: (i, 0)), pl.BlockSpec((8, f), lambda i: (0, 0)),
                   pl.BlockSpec((1, f), lambda i: (0, 0))],
        out_shape=[SDS((t, f2), BF), SDS((8, f), F32), SDS((1, f), F32)],
        scratch_shapes=[pltpu.VMEM((h + tr + h, f), F32), pltpu.VMEM((tr + h, f), F32)],
        compiler_params=_params("arbitrary"))(up, up, up, up, up, da, da, w_dw, b_dw)


def _ln_silu(yc, g, b):
    mu = jnp.mean(yc, axis=-1, keepdims=True)
    xc = yc - mu
    y = xc * lax.rsqrt(jnp.mean(xc * xc, axis=-1, keepdims=True) + NORM_EPS)
    return jax.nn.silu(y * g + b)


SUBLANES = 8
CONV_PAD = 24


def _glu(a, g):
    return a.astype(F32) * jax.nn.sigmoid(g.astype(F32))


def _glu_strip(ygs_ref, first_tile, a_ref, ap_ref, g_ref, gp_ref, col, h, tr):
    ygs_ref[pl.ds(0, h), :] = jnp.where(first_tile, 0.0, _glu(ap_ref[:, col], gp_ref[:, col]))
    ygs_ref[pl.ds(h, tr), :] = _glu(a_ref[:, col], g_ref[:, col])


def _shift_past(sh_ref, ygs_ref, h, n):
    for r in range(1, SUBLANES):
        sh_ref[r, pl.ds(0, n + CONV_PAD), :] = ygs_ref[pl.ds(h - CONV_PAD - r, n + CONV_PAD), :]


def _past_rows(sh_ref, ygs_ref, h, n, s):
    a, r = divmod(s, SUBLANES)
    if r == 0:
        return ygs_ref[pl.ds(h - SUBLANES * a, n), :]
    return sh_ref[r, pl.ds(CONV_PAD - SUBLANES * a, n), :]


def _conf_mid_fwd(p1, w_dw, b_dw, ln_g, ln_b):
    t, w2 = p1.shape
    w = w2 // 2
    tr = min(256, t)
    h = CONV_HALO
    rc = 32

    def body(a_ref, ap_ref, g_ref, gp_ref, w_ref, b_ref, lg_ref, lb_ref, o_ref, yc_ref, ygs_ref, sh_ref):
        first_tile = pl.program_id(0) == 0

        def strip(c, carry):
            col = pl.ds(pl.multiple_of(c * LANES, LANES), LANES)
            _glu_strip(ygs_ref, first_tile, a_ref, ap_ref, g_ref, gp_ref, col, h, tr)
            _shift_past(sh_ref, ygs_ref, h, tr)
            acc = jnp.broadcast_to(b_ref[:, col], (tr, LANES))
            for k in range(CONV_K):
                acc = acc + w_ref[pl.ds(k, 1), col] * _past_rows(sh_ref, ygs_ref, h, tr, CONV_K - 1 - k)
            yc_ref[:, col] = acc
            return carry

        lax.fori_loop(0, w // LANES, strip, 0)

        def rows(r, carry):
            rs = pl.ds(pl.multiple_of(r * rc, rc), rc)
            o_ref[rs, :] = _ln_silu(yc_ref[rs, :], lg_ref[...], lb_ref[...]).astype(o_ref.dtype)
            return carry

        lax.fori_loop(0, tr // rc, rows, 0)

    vec = pl.BlockSpec((1, w), lambda i: (0, 0))
    tile = pl.BlockSpec((tr, w), lambda i: (i, 0))
    return _call(
        body, name="conf_mid_fwd", grid=(t // tr,),
        in_specs=[tile, pl.BlockSpec((h, w), _prev_halo(tr, h, 0)),
                  pl.BlockSpec((tr, w), lambda i: (i, 1)), pl.BlockSpec((h, w), _prev_halo(tr, h, 1)),
                  pl.BlockSpec((32, w), lambda i: (0, 0)), vec, vec, vec],
        out_specs=[tile, tile], out_shape=[SDS((t, w), BF), SDS((t, w), F32)],
        scratch_shapes=[pltpu.VMEM((h + tr, LANES), F32), pltpu.VMEM((SUBLANES, tr + CONV_PAD, LANES), F32)],
        compiler_params=_params("parallel"))(p1, p1, p1, p1, w_dw, b_dw, ln_g, ln_b)


def _conf_mid_bwd(p1, yc, dys, dy, w_dw, ln_g, ln_b):
    t, w2 = p1.shape
    w = w2 // 2
    tr = min(256, t)
    h = CONV_HALO
    rc = 32
    n_tiles = t // tr

    def body(a_ref, ap_ref, g_ref, gp_ref, yc_ref, ycn_ref, dys_ref, dysn_ref, dy_ref, w_ref, lg_ref, lb_ref,
             dp_ref, dw_ref, db_ref, dlg_ref, dlb_ref, db1_ref, db2_ref, dyc_ref, ygs_ref, sh_ref, shf_ref):
        i = pl.program_id(0)
        last = i == n_tiles - 1

        @pl.when(i == 0)
        def _():
            for ref in (dw_ref, db_ref, dlg_ref, dlb_ref, db1_ref, db2_ref):
                ref[...] = jnp.zeros_like(ref)

        def ln_rows(r, carry):
            rs = pl.ds(pl.multiple_of(r * rc, rc), rc)
            _, vjp = jax.vjp(_ln_silu, yc_ref[rs, :], lg_ref[...], lb_ref[...])
            dyc, dlg, dlb = vjp(dys_ref[rs, :].astype(F32))
            dyc_ref[rs, :] = dyc
            dlg_ref[...] += dlg
            dlb_ref[...] += dlb
            return carry

        lax.fori_loop(0, tr // rc, ln_rows, 0)
        _, vjp = jax.vjp(_ln_silu, ycn_ref[...], lg_ref[...], lb_ref[...])
        dyc_ref[pl.ds(tr, h), :] = jnp.where(last, 0.0, vjp(dysn_ref[...].astype(F32))[0])
        db2_ref[...] += jnp.sum(dy_ref[...], axis=0, keepdims=True)

        def back(c, carry):
            col = pl.ds(pl.multiple_of(c * LANES, LANES), LANES)
            gcol = pl.ds(pl.multiple_of(w + c * LANES, LANES), LANES)
            _glu_strip(ygs_ref, i == 0, a_ref, ap_ref, g_ref, gp_ref, col, h, tr)
            _shift_past(sh_ref, ygs_ref, h, tr)
            for r in range(1, SUBLANES):
                shf_ref[r, pl.ds(0, tr + CONV_PAD), :] = dyc_ref[pl.ds(r, tr + CONV_PAD), col]
            dyc = dyc_ref[pl.ds(0, tr), col]
            dyg = jnp.zeros((tr, LANES), F32)
            for k in range(CONV_K):
                s = CONV_K - 1 - k
                a, r = divmod(s, SUBLANES)
                if r == 0:
                    future = dyc_ref[pl.ds(SUBLANES * a, tr), col]
                else:
                    future = shf_ref[r, pl.ds(SUBLANES * a, tr), :]
                dyg = dyg + w_ref[pl.ds(k, 1), col] * future
                dw_ref[pl.ds(k, 1), col] += jnp.sum(_past_rows(sh_ref, ygs_ref, h, tr, s) * dyc, axis=0, keepdims=True)
            db_ref[:, col] += jnp.sum(dyc, axis=0, keepdims=True)
            sg = jax.nn.sigmoid(g_ref[:, col].astype(F32))
            da = dyg * sg
            dg = dyg * a_ref[:, col].astype(F32) * sg * (1.0 - sg)
            dp_ref[:, col] = da.astype(dp_ref.dtype)
            dp_ref[:, gcol] = dg.astype(dp_ref.dtype)
            db1_ref[:, col] += jnp.sum(da, axis=0, keepdims=True)
            db1_ref[:, gcol] += jnp.sum(dg, axis=0, keepdims=True)
            return carry

        lax.fori_loop(0, w // LANES, back, 0)

    tile = lambda col: pl.BlockSpec((tr, w), lambda i: (i, col))
    prv = lambda col: pl.BlockSpec((h, w), _prev_halo(tr, h, col))
    nxt = pl.BlockSpec((h, w), _next_halo(tr, h, t, 0))
    vec = pl.BlockSpec((1, w), lambda i: (0, 0))
    return _call(
        body, name="conf_mid_bwd", grid=(n_tiles,),
        in_specs=[tile(0), prv(0), tile(1), prv(1), tile(0), nxt, tile(0), nxt, tile(0),
                  pl.BlockSpec((32, w), lambda i: (0, 0)), vec, vec],
        out_specs=[pl.BlockSpec((tr, w2), lambda i: (i, 0)), pl.BlockSpec((32, w), lambda i: (0, 0)), vec, vec, vec,
                   pl.BlockSpec((1, w2), lambda i: (0, 0)), vec],
        out_shape=[SDS((t, w2), BF), SDS((32, w), F32), SDS((1, w), F32), SDS((1, w), F32), SDS((1, w), F32),
                   SDS((1, w2), F32), SDS((1, w), F32)],
        scratch_shapes=[pltpu.VMEM((tr + h, w), F32), pltpu.VMEM((h + tr, LANES), F32),
                        pltpu.VMEM((SUBLANES, tr + CONV_PAD, LANES), F32), pltpu.VMEM((SUBLANES, tr + CONV_PAD, LANES), F32)],
        compiler_params=_params("arbitrary"))(p1, p1, p1, p1, yc, yc, dys, dys, dy, w_dw, ln_g, ln_b)


def _group_matrices():
    i = lax.broadcasted_iota(jnp.int32, (512, 512), 0)
    j = lax.broadcasted_iota(jnp.int32, (512, 512), 1)
    mean64 = jnp.where(i // HEAD_DIM == j // HEAD_DIM, 1.0 / HEAD_DIM, 0.0).astype(F32)
    fold64 = jnp.where(i % HEAD_DIM == j % HEAD_DIM, 1.0, 0.0).astype(F32)
    return mean64, fold64


def _split_dot(x, mat):
    hi = x.astype(BF)
    lo = (x - hi.astype(F32)).astype(BF)
    mb = mat.astype(BF)
    return jnp.dot(hi, mb, preferred_element_type=F32) + jnp.dot(lo, mb, preferred_element_type=F32)


@jax.custom_vjp
def _group_sum(x, mat):
    return _split_dot(x, mat)


_group_sum.defvjp(lambda x, mat: (_split_dot(x, mat), mat), lambda mat, ct: (_split_dot(ct, mat), jnp.zeros_like(mat)))


def _bf_dot_plain(a, b):
    return jnp.dot(a.astype(BF), b.astype(BF), preferred_element_type=F32)


@jax.custom_vjp
def _bf_dot(a, b):
    return _bf_dot_plain(a, b)


def _bf_dot_bwd(res, ct):
    a, b = res
    cb = ct.astype(BF)
    return (lax.dot_general(cb, b.astype(BF), NT_DIMS, preferred_element_type=F32),
            lax.dot_general(a.astype(BF), cb, TN_DIMS, preferred_element_type=F32))


_bf_dot.defvjp(lambda a, b: (_bf_dot_plain(a, b), (a, b)), _bf_dot_bwd)


def _prep_tile(proj, qg, kg, zg, ws, bexp, mean64, differentiated=False):
    sw = 512
    q, k, v, u, z = (proj[:, n * sw:(n + 1) * sw] for n in range(5))
    group_sum, dot = (_group_sum, _bf_dot) if differentiated else (_split_dot, _bf_dot_plain)

    def group_norm(x):
        return x * lax.rsqrt(group_sum(x * x, mean64) + NORM_EPS)

    qn = group_norm(q) * qg
    kn = group_norm(k) * kg
    zn = group_norm(_gelu(z)) * zg
    row = lax.broadcasted_iota(jnp.int32, (CHUNK, CHUNK), 0)
    col = lax.broadcasted_iota(jnp.int32, (CHUNK, CHUNK), 1)
    first = lax.broadcasted_iota(jnp.int32, (1, LANES), 1) < HEAD_DIM
    parts = []
    for pr in range(sw // LANES):
        zp = zn[:, pr * LANES:(pr + 1) * LANES]
        s0 = dot(jnp.where(col <= row, ws[2 * pr], 0.0), zp)
        s1 = dot(jnp.where(col <= row, ws[2 * pr + 1], 0.0), zp)
        parts.append(jnp.where(first, s0, s1))
    s = jnp.concatenate(parts, axis=1) + bexp
    return qn, kn, v, _gelu(u) * s


def _mix_prep_fwd(proj, qg, kg, zg, w_s, l, bexp, mean64):
    t = proj.shape[0]
    tr = CHUNK

    def body(p_ref, qg_ref, kg_ref, zg_ref, ws_ref, be_ref, m_ref, qkv_ref, go_ref):
        qn, kn, v, go = _prep_tile(p_ref[...], qg_ref[...], kg_ref[...], zg_ref[...], ws_ref[...], be_ref[...], m_ref[...])
        qkv_ref[:, 0:512] = qn.astype(BF)
        qkv_ref[:, 512:1024] = kn.astype(BF)
        qkv_ref[:, 1024:1536] = v.astype(BF)
        go_ref[...] = go.astype(BF)

    vec = pl.BlockSpec((1, 512), lambda i: (0, 0))
    return _call(
        body, name="mix_prep_fwd", grid=(t // tr,),
        in_specs=[pl.BlockSpec((tr, 2560), lambda i: (i, 0)), vec, vec, vec,
                  pl.BlockSpec((None, 8, CHUNK, CHUNK), lambda i: (l, 0, 0, 0)),
                  pl.BlockSpec((CHUNK, 512), lambda i: (0, 0)), pl.BlockSpec((512, 512), lambda i: (0, 0))],
        out_specs=[pl.BlockSpec((tr, 1536), lambda i: (i, 0)), pl.BlockSpec((tr, 512), lambda i: (i, 0))],
        out_shape=[SDS((t, 1536), BF), SDS((t, 512), BF)],
        compiler_params=_params("parallel"))(proj, qg, kg, zg, w_s, bexp, mean64)


def _mix_prep_bwd(proj, dq, dk, dv, dmix, qg, kg, zg, w_s, l, bexp, mean64, fold64):
    t = proj.shape[0]
    tr = CHUNK
    n_tiles = t // tr

    def body(p_ref, dq_ref, dk_ref, dv_ref, dgo_ref, qg_ref, kg_ref, zg_ref, ws_ref, be_ref, m_ref, f_ref,
             dp_ref, dqg_ref, dkg_ref, dzg_ref, dws_ref, dbe_ref):
        i = pl.program_id(0)

        @pl.when(i == 0)
        def _():
            for ref in (dqg_ref, dkg_ref, dzg_ref, dws_ref, dbe_ref):
                ref[...] = jnp.zeros_like(ref)

        fn = functools.partial(_prep_tile, mean64=m_ref[...], differentiated=True)
        _, vjp = jax.vjp(fn, p_ref[...], qg_ref[...], kg_ref[...], zg_ref[...], ws_ref[...], be_ref[...])
        dp, dqg, dkg, dzg, dws, dbe = vjp((dq_ref[...], dk_ref[...], dv_ref[...], dgo_ref[...]))
        dp_ref[...] = dp.astype(BF)
        dqg_ref[pl.ds(0, 1), :] += dqg
        dkg_ref[pl.ds(0, 1), :] += dkg
        dzg_ref[pl.ds(0, 1), :] += dzg
        dws_ref[...] += dws
        dbe_ref[...] += dbe

        @pl.when(i == n_tiles - 1)
        def _():
            dqg_ref[...] = jnp.dot(dqg_ref[...], f_ref[...], precision=HI, preferred_element_type=F32)
            dkg_ref[...] = jnp.dot(dkg_ref[...], f_ref[...], precision=HI, preferred_element_type=F32)
            dbe_ref[...] = jnp.dot(dbe_ref[...], m_ref[...] * float(HEAD_DIM), precision=HI, preferred_element_type=F32)

    vec = pl.BlockSpec((1, 512), lambda i: (0, 0))
    acc = pl.BlockSpec((8, 512), lambda i: (0, 0))
    sq = pl.BlockSpec((512, 512), lambda i: (0, 0))
    row = pl.BlockSpec((tr, 512), lambda i: (i, 0))
    return _call(
        body, name="mix_prep_bwd", grid=(n_tiles,),
        in_specs=[pl.BlockSpec((tr, 2560), lambda i: (i, 0)), row, row, row, pl.BlockSpec((tr, 512), lambda i: (i, 1)),
                  vec, vec, vec, pl.BlockSpec((None, 8, CHUNK, CHUNK), lambda i: (l, 0, 0, 0)),
                  pl.BlockSpec((CHUNK, 512), lambda i: (0, 0)), sq, sq],
        out_specs=[pl.BlockSpec((tr, 2560), lambda i: (i, 0)), acc, acc, acc,
                   pl.BlockSpec((8, CHUNK, CHUNK), lambda i: (0, 0, 0)), pl.BlockSpec((CHUNK, 512), lambda i: (0, 0))],
        out_shape=[SDS((t, 2560), BF), SDS((8, 512), F32), SDS((8, 512), F32), SDS((8, 512), F32),
                   SDS((8, CHUNK, CHUNK), F32), SDS((CHUNK, 512), F32)],
        compiler_params=_params("arbitrary"))(proj, dq, dk, dv, dmix, qg, kg, zg, w_s, bexp, mean64, fold64)


def _sb_block(qh, kb, valid, upper, run):
    z = lax.dot_general(qh, kb, NT_DIMS, preferred_element_type=F32) * (HEAD_DIM ** -0.5)
    soft = jnp.log1p(jnp.exp(-jnp.abs(z)))
    lk_raw = -(jnp.maximum(z, 0.0) + soft)
    ls = -(jnp.maximum(-z, 0.0) + soft)
    lk = jnp.where(valid, lk_raw, 0.0)
    tail = jnp.dot(lk, upper, precision=HI, preferred_element_type=F32)
    wgt = jnp.where(valid, jnp.exp(ls + run + tail), 0.0)
    return lk_raw, ls, lk, wgt


def _att_masks(b):
    row = lax.broadcasted_iota(jnp.int32, (b, b), 0)
    col = lax.broadcasted_iota(jnp.int32, (b, b), 1)
    first = lax.broadcasted_iota(jnp.int32, (1, LANES), 1) < HEAD_DIM
    return row, col, first


N_PAIRS = 4


def _load_kv(qkv_hbm, k_scr, v_scr, sems, group, width):
    ck = pltpu.make_async_copy(qkv_hbm.at[:, pl.ds(pl.multiple_of(512 + group * width, LANES), width)], k_scr, sems.at[0])
    cv = pltpu.make_async_copy(qkv_hbm.at[:, pl.ds(pl.multiple_of(1024 + group * width, LANES), width)], v_scr, sems.at[1])
    ck.start()
    cv.start()
    ck.wait()
    cv.wait()


def _split_heads(ref, pair, first):
    x = ref[:, pair * LANES:(pair + 1) * LANES]
    zero = jnp.zeros_like(x)
    return jnp.where(first, x, zero), jnp.where(first, zero, x)


def _any_weight_left(run_ref, n_heads):
    top = run_ref[0]
    for hh in range(1, n_heads):
        top = jnp.maximum(top, run_ref[hh])
    return jnp.max(jnp.exp(top)) > 0.0


def _attn_fwd(qkv, pairs_per_step=4):
    t = qkv.shape[0]
    b = ATT_BLOCK
    nq = t // b
    width = pairs_per_step * LANES
    n_heads = 2 * pairs_per_step

    def body(q_ref, qkv_hbm, ob_ref, o32_ref, k_scr, v_scr, acc_ref, run_ref, sems):
        group, qi = pl.program_id(0), pl.program_id(1)

        @pl.when(qi == 0)
        def _():
            _load_kv(qkv_hbm, k_scr, v_scr, sems, group, width)

        row, col, first = _att_masks(b)
        qh = [x for pr in range(pairs_per_step) for x in _split_heads(q_ref, pr, first)]
        upper = jnp.where(row > col, 1.0, 0.0).astype(F32)
        acc_ref[...] = jnp.zeros_like(acc_ref)
        run_ref[...] = jnp.zeros_like(run_ref)

        def step(carry):
            j, _ = carry
            rows = pl.ds(pl.multiple_of(j * b, b), b)
            valid = jnp.logical_or(j != qi, col < row)
            for hh in range(n_heads):
                lanes = pl.ds((hh // 2) * LANES, LANES)
                run = run_ref[hh]
                _, _, lk, wgt = _sb_block(qh[hh], k_scr[rows, lanes], valid, upper, run)
                acc_ref[hh] += jnp.dot(wgt.astype(BF), v_scr[rows, lanes], preferred_element_type=F32)
                run_ref[hh] = run + jnp.sum(lk, axis=1, keepdims=True)
            return j - 1, _any_weight_left(run_ref, n_heads)

        lax.while_loop(lambda c: jnp.logical_and(c[0] >= 0, c[1]), step, (qi, jnp.bool_(True)))
        for pr in range(pairs_per_step):
            out = jnp.where(first, acc_ref[2 * pr], acc_ref[2 * pr + 1])
            ob_ref[:, pr * LANES:(pr + 1) * LANES] = out.astype(BF)
            o32_ref[:, pr * LANES:(pr + 1) * LANES] = out

    blk = pl.BlockSpec((b, width), lambda g, qi: (qi, g))
    return _call(
        body, name="attn_fwd", grid=(N_PAIRS // pairs_per_step, nq),
        in_specs=[blk, pl.BlockSpec(memory_space=pl.ANY)], out_specs=[blk, blk],
        out_shape=[SDS((t, 512), BF), SDS((t, 512), F32)],
        scratch_shapes=[pltpu.VMEM((t, width), BF), pltpu.VMEM((t, width), BF),
                        pltpu.VMEM((n_heads, b, LANES), F32), pltpu.VMEM((n_heads, b, 1), F32),
                        pltpu.SemaphoreType.DMA((2,))],
        compiler_params=_params("arbitrary", "arbitrary"))(qkv, qkv)


def _attn_bwd(qkv, a32, dmix, pairs_per_step=2):
    t = qkv.shape[0]
    b = ATT_BLOCK
    nq = t // b
    width = pairs_per_step * LANES
    n_heads = 2 * pairs_per_step

    def body(q_ref, a_ref, da_ref, qkv_hbm, dq_ref, dk_hbm, dv_hbm,
             k_scr, v_scr, dk_scr, dv_scr, dqa_ref, run_ref, rung_ref, sems):
        group, qi = pl.program_id(0), pl.program_id(1)

        @pl.when(qi == 0)
        def _():
            _load_kv(qkv_hbm, k_scr, v_scr, sems, group, width)
            dk_scr[...] = jnp.zeros_like(dk_scr)
            dv_scr[...] = jnp.zeros_like(dv_scr)

        row, col, first = _att_masks(b)
        qh, dah, dtot = [], [], []
        for pr in range(pairs_per_step):
            qh += _split_heads(q_ref, pr, first)
            da = da_ref[:, pr * LANES:(pr + 1) * LANES]
            prod = da * a_ref[:, pr * LANES:(pr + 1) * LANES]
            dtot += [jnp.sum(jnp.where(first, prod, 0.0), axis=1, keepdims=True),
                     jnp.sum(jnp.where(first, 0.0, prod), axis=1, keepdims=True)]
            dah += [jnp.where(first, da, 0.0).astype(BF), jnp.where(first, 0.0, da).astype(BF)]
        upper = jnp.where(row > col, 1.0, 0.0).astype(F32)
        lower_incl = jnp.where(row >= col, 1.0, 0.0).astype(F32)
        dqa_ref[...] = jnp.zeros_like(dqa_ref)
        run_ref[...] = jnp.zeros_like(run_ref)
        rung_ref[...] = jnp.zeros_like(rung_ref)

        def step(carry):
            j, _ = carry
            rows = pl.ds(pl.multiple_of(j * b, b), b)
            valid = jnp.logical_or(j != qi, col < row)
            for hh in range(n_heads):
                lanes = pl.ds((hh // 2) * LANES, LANES)
                kb = k_scr[rows, lanes]
                run = run_ref[hh]
                rung = rung_ref[hh]
                lk_raw, ls, lk, wgt = _sb_block(qh[hh], kb, valid, upper, run)
                dp = lax.dot_general(dah[hh], v_scr[rows, lanes], NT_DIMS, preferred_element_type=F32)
                g = wgt * dp
                g_from = jnp.dot(g, lower_incl, precision=HI, preferred_element_type=F32)
                dlk = jnp.where(valid, dtot[hh] - rung - g_from, 0.0)
                dz = ((g * jnp.exp(lk_raw) - dlk * jnp.exp(ls)) * (HEAD_DIM ** -0.5)).astype(BF)
                dqa_ref[hh] += jnp.dot(dz, kb, preferred_element_type=F32)
                dk_scr[rows, lanes] += lax.dot_general(dz, qh[hh], TN_DIMS, preferred_element_type=F32)
                dv_scr[rows, lanes] += lax.dot_general(wgt.astype(BF), dah[hh], TN_DIMS, preferred_element_type=F32)
                rung_ref[hh] = rung + jnp.sum(g, axis=1, keepdims=True)
                run_ref[hh] = run + jnp.sum(lk, axis=1, keepdims=True)
            return j - 1, _any_weight_left(run_ref, n_heads)

        lax.while_loop(lambda c: jnp.logical_and(c[0] >= 0, c[1]), step, (qi, jnp.bool_(True)))
        for pr in range(pairs_per_step):
            dq_ref[:, pr * LANES:(pr + 1) * LANES] = jnp.where(first, dqa_ref[2 * pr], dqa_ref[2 * pr + 1])

        @pl.when(qi == nq - 1)
        def _():
            cols = pl.ds(pl.multiple_of(group * width, LANES), width)
            ck = pltpu.make_async_copy(dk_scr, dk_hbm.at[:, cols], sems.at[0])
            cv = pltpu.make_async_copy(dv_scr, dv_hbm.at[:, cols], sems.at[1])
            ck.start()
            cv.start()
            ck.wait()
            cv.wait()

    blk = pl.BlockSpec((b, width), lambda g, qi: (qi, g))
    anywhere = pl.BlockSpec(memory_space=pl.ANY)
    return _call(
        body, name="attn_bwd", grid=(N_PAIRS // pairs_per_step, nq),
        in_specs=[blk, blk, blk, anywhere], out_specs=[blk, anywhere, anywhere],
        out_shape=[SDS((t, 512), F32), SDS((t, 512), F32), SDS((t, 512), F32)],
        scratch_shapes=[pltpu.VMEM((t, width), BF), pltpu.VMEM((t, width), BF),
                        pltpu.VMEM((t, width), F32), pltpu.VMEM((t, width), F32),
                        pltpu.VMEM((n_heads, b, LANES), F32), pltpu.VMEM((n_heads, b, 1), F32),
                        pltpu.VMEM((n_heads, b, 1), F32), pltpu.SemaphoreType.DMA((2,))],
        compiler_params=_params("arbitrary", "arbitrary"))(qkv, a32, dmix, qkv)


def _adamw(w, g, m, v):
    n, c = w.shape
    tr = min(256, n)
    assert n % tr == 0

    def body(w_ref, g_ref, m_ref, v_ref, d_ref, nm_ref, nv_ref):
        g = g_ref[...]
        m = ADAM_B1 * m_ref[...] + (1.0 - ADAM_B1) * g
        v = ADAM_B2 * v_ref[...] + (1.0 - ADAM_B2) * jnp.square(g)
        m_hat = m / (1.0 - ADAM_B1 ** ADAM_STEP)
        v_hat = v / (1.0 - ADAM_B2 ** ADAM_STEP)
        d_ref[...] = -ADAM_LR * (m_hat / (jnp.sqrt(v_hat) + ADAM_EPS) + ADAM_WD * w_ref[...])
        nm_ref[...] = m
        nv_ref[...] = v

    blk = pl.BlockSpec((tr, c), lambda i: (i, 0))
    return _call(
        body, name="adamw", grid=(n // tr,), in_specs=[blk] * 4, out_specs=[blk] * 3,
        out_shape=[SDS((n, c), F32)] * 3, compiler_params=_params("parallel"))(w, g, m, v)


def _mesh_pos():
    return lax.axis_index("x"), lax.axis_index("y"), lax.axis_index("c")


def _other_chips(x, y):
    return [(1 - x, y), (x, 1 - y), (1 - x, 1 - y)]


HBM_SPEC = pl.BlockSpec(memory_space=pltpu.HBM)


def _all_gather(shard):
    n_l, r, c_w = shard.shape
    h = r // 2

    def body(s_ref, o_ref, send_sems, recv_sems, local_sem):
        x, y, c = _mesh_pos()
        sibling = (x, y, 1 - c)
        chips = _other_chips(x, y)

        def half(px, py, hc):
            return o_ref.at[:, 2 * px + py, pl.ds(hc * h, h), :]

        def copy(k, dst, to, src=None):
            return pltpu.make_async_remote_copy(
                src_ref=dst if src is None else src, dst_ref=dst, send_sem=send_sems.at[k], recv_sem=recv_sems.at[k],
                device_id=to, device_id_type=MESH)

        mine = pltpu.make_async_copy(s_ref, o_ref.at[:, 2 * x + y], local_sem)
        mine.start()
        first = [copy(j, half(x, y, c), (*chip, c), src=s_ref.at[:, pl.ds(c * h, h), :]) for j, chip in enumerate(chips)]
        for cp in first:
            cp.start()
        passed = [copy(3 + j, half(*chip, c), sibling) for j, chip in enumerate(chips)]
        for j, chip in enumerate(chips):
            copy(j, half(*chip, c), (x, y, c)).wait_recv()
            passed[j].start()
        for j, chip in enumerate(chips):
            copy(3 + j, half(*chip, 1 - c), (x, y, c)).wait_recv()
        for cp in first + passed:
            cp.wait_send()
        mine.wait()

    return _call(
        body, name="all_gather", in_specs=[HBM_SPEC], out_specs=HBM_SPEC,
        out_shape=SDS((n_l, N_CHIPS, r, c_w), shard.dtype),
        scratch_shapes=[pltpu.SemaphoreType.DMA((6,)), pltpu.SemaphoreType.DMA((6,)), pltpu.SemaphoreType.DMA],
        )(shard)


def _row_tile(h):
    for cand in (256, 176, 128, 64, 32, 16):
        if h % cand == 0:
            return cand
    raise ValueError(h)


def _reduce_scatter(g, mid_dtype):
    n_l, n_p, r, c_w = g.shape
    h = r // 2
    tr = _row_tile(h)
    nt = h // tr
    x, y, c = _mesh_pos()
    c_arr = jnp.reshape(c, (1,)).astype(jnp.int32)
    p_arr = jnp.reshape(2 * x + y, (1,)).astype(jnp.int32)

    def to_sibling_body(g_ref, a_ref, send_sem, recv_sem):
        x, y, c = _mesh_pos()
        cp = pltpu.make_async_remote_copy(
            src_ref=g_ref.at[:, :, pl.ds((1 - c) * h, h), :], dst_ref=a_ref, send_sem=send_sem, recv_sem=recv_sem,
            device_id=(x, y, 1 - c), device_id_type=MESH)
        cp.start()
        cp.wait()

    from_sibling = _call(
        to_sibling_body, name="rs_pair", in_specs=[HBM_SPEC], out_specs=HBM_SPEC,
        out_shape=SDS((n_l, n_p, h, c_w), g.dtype),
        scratch_shapes=[pltpu.SemaphoreType.DMA, pltpu.SemaphoreType.DMA],
        )(g)

    def pair_add_body(c_ref, g_ref, a_ref, o_ref):
        o_ref[...] = (g_ref[...].astype(F32) + a_ref[...].astype(F32)).astype(o_ref.dtype)

    blk = (None, None, tr, c_w)
    pair_sum = _call(
        pair_add_body, name="rs_pair_add",
        grid_spec=pltpu.PrefetchScalarGridSpec(
            num_scalar_prefetch=1, grid=(n_l, n_p, nt),
            in_specs=[pl.BlockSpec(blk, lambda l, p, t, c_ref: (l, p, c_ref[0] * nt + t, 0)),
                      pl.BlockSpec(blk, lambda l, p, t, c_ref: (l, p, t, 0))],
            out_specs=pl.BlockSpec(blk, lambda l, p, t, c_ref: (l, p, t, 0))),
        out_shape=SDS((n_l, n_p, h, c_w), mid_dtype),
        compiler_params=_params("parallel", "parallel", "parallel"))(c_arr, g, from_sibling)

    def to_chips_body(s_ref, b_ref, send_sems, recv_sems):
        x, y, c = _mesh_pos()
        cps = [pltpu.make_async_remote_copy(
            src_ref=s_ref.at[:, 2 * chip[0] + chip[1]], dst_ref=b_ref.at[j], send_sem=send_sems.at[j],
            recv_sem=recv_sems.at[j], device_id=(*chip, c), device_id_type=MESH)
            for j, chip in enumerate(_other_chips(x, y))]
        for cp in cps:
            cp.start()
        for cp in cps:
            cp.wait()

    from_chips = _call(
        to_chips_body, name="rs_chips", in_specs=[HBM_SPEC], out_specs=HBM_SPEC,
        out_shape=SDS((3, n_l, h, c_w), mid_dtype),
        scratch_shapes=[pltpu.SemaphoreType.DMA((3,)), pltpu.SemaphoreType.DMA((3,))],
        )(pair_sum)

    def chip_add_body(p_ref, c_ref, s_ref, b_ref, o_ref):
        acc = s_ref[...].astype(F32)
        for j in range(3):
            acc = acc + b_ref[j].astype(F32)
        o_ref[...] = acc

    half_sum = _call(
        chip_add_body, name="rs_chip_add",
        grid_spec=pltpu.PrefetchScalarGridSpec(
            num_scalar_prefetch=2, grid=(n_l, nt),
            in_specs=[pl.BlockSpec((None, None, tr, c_w), lambda l, t, p_ref, c_ref: (l, p_ref[0], t, 0)),
                      pl.BlockSpec((3, None, tr, c_w), lambda l, t, p_ref, c_ref: (0, l, t, 0))],
            out_specs=pl.BlockSpec((None, tr, c_w), lambda l, t, p_ref, c_ref: (l, c_ref[0] * nt + t, 0))),
        out_shape=SDS((n_l, r, c_w), F32),
        compiler_params=_params("parallel", "parallel"))(p_arr, c_arr, pair_sum, from_chips)

    def swap_body(i_ref, o_ref, send_sem, recv_sem):
        x, y, c = _mesh_pos()
        mine = o_ref.at[:, pl.ds(c * h, h), :]
        theirs = o_ref.at[:, pl.ds((1 - c) * h, h), :]
        pltpu.make_async_remote_copy(src_ref=mine, dst_ref=mine, send_sem=send_sem, recv_sem=recv_sem,
                                     device_id=(x, y, 1 - c), device_id_type=MESH).start()
        wait = pltpu.make_async_remote_copy(src_ref=mine, dst_ref=theirs, send_sem=send_sem, recv_sem=recv_sem,
                                            device_id=(x, y, 1 - c), device_id_type=MESH)
        wait.wait_send()
        wait.wait_recv()

    return _call(
        swap_body, name="rs_swap", in_specs=[HBM_SPEC], out_specs=HBM_SPEC, out_shape=SDS((n_l, r, c_w), F32),
        input_output_aliases={0: 0},
        scratch_shapes=[pltpu.SemaphoreType.DMA, pltpu.SemaphoreType.DMA],
        )(half_sum)


def _pack(arrays, row_multiple):
    flat = jnp.concatenate([a.reshape(-1).astype(F32) for a in arrays])
    unit = row_multiple * LANES
    padded = -(-flat.shape[0] // unit) * unit
    return jnp.pad(flat, (0, padded - flat.shape[0])).reshape(padded // LANES, LANES)


def _unpack(packed, shapes):
    flat = packed.reshape(-1)
    out, pos = [], 0
    for s in shapes:
        size = 1
        for dim in s:
            size *= dim
        out.append(flat[pos:pos + size].reshape(s))
        pos += size
    return out


BIG_COL = ("sb_w_in", "cv_w_pw1", "ffn_w_up")
BIG_ROW = ("hyb_w_out", "cv_w_pw2", "ffn_w_down")
SMALL_SHARDED = ("cv_b_pw1", "cv_w_dw", "cv_b_dw", "cv_ln_g", "cv_ln_b", "cv_b_pw2", "ffn_w_dw")
SMALL_REPLICATED = ("mix_norm_g", "sb_q_norm_g", "sb_k_norm_g", "sg_z_norm_g", "sg_w_spatial", "sg_b_spatial",
                    "ffn_norm_g", "ffn_b_dw")
WEIGHTS = ("mix_norm_g", "sb_w_in", "sb_q_norm_g", "sb_k_norm_g", "sg_z_norm_g", "sg_w_spatial", "sg_b_spatial",
           "hyb_w_out", "cv_w_pw1", "cv_b_pw1", "cv_w_dw", "cv_b_dw", "cv_ln_g", "cv_ln_b", "cv_w_pw2", "cv_b_pw2",
           "ffn_norm_g", "ffn_w_up", "ffn_w_dw", "ffn_b_dw", "ffn_w_down")


def _pad_rows(a, rows):
    return jnp.pad(a, ((0, rows - a.shape[0]), (0, 0)))


def _step(x, tgt, w, m, v):
    n_layers = w["mix_norm_g"].shape[0]
    xi, yi, ci = _mesh_pos()
    chip = 2 * xi + yi

    full = {}
    for name in BIG_COL:
        full[name] = _all_gather(w[name].astype(BF))
    for name in BIG_ROW:
        g4 = _all_gather(w[name].astype(BF))
        full[name] = g4.reshape(g4.shape[0], 1, g4.shape[1] * g4.shape[2], g4.shape[3])
    small_local = [w[name] for name in SMALL_SHARDED]
    gathered = _all_gather(_pack(small_local, 32)[None])[0]
    per_chip = [_unpack(gathered[p], [a.shape for a in small_local]) for p in range(N_CHIPS)]
    for k, name in enumerate(SMALL_SHARDED):
        full[name] = jnp.concatenate([per_chip[p][k] for p in range(N_CHIPS)], axis=-1)
    for name in SMALL_REPLICATED:
        full[name] = w[name]

    mean64, fold64 = _group_matrices()
    ffn_wdw = [_pad_rows(full["ffn_w_dw"][i], 8) for i in range(n_layers)]
    cv_wdw = [_pad_rows(full["cv_w_dw"][j], 32) for j in range(n_layers // 2)]
    row = lambda a: a.reshape(1, -1)

    saved = []
    cur = x
    for i in range(n_layers):
        j = i // 2
        rec = {"x_in": cur}
        h = _rms_fwd(cur, row(full["mix_norm_g"][i]))
        rec["h_mix"] = h
        if i % 2 == 0:
            proj = _mm_nn(h, full["sb_w_in"], j)
            qg = row(jnp.tile(full["sb_q_norm_g"][j], 512 // HEAD_DIM))
            kg = row(jnp.tile(full["sb_k_norm_g"][j], 512 // HEAD_DIM))
            zg = row(full["sg_z_norm_g"][j])
            bexp = jnp.repeat(full["sg_b_spatial"][j].T, HEAD_DIM, axis=1)
            qkv, gated = _mix_prep_fwd(proj, qg, kg, zg, full["sg_w_spatial"], j, bexp, mean64)
            att_bf, att_32 = _attn_fwd(qkv)
            mix = jnp.concatenate([att_bf, gated], axis=1)
            cur = _mm_nn(mix, full["hyb_w_out"], j, resid=cur)
            rec.update(proj=proj, qkv=qkv, att_32=att_32, mix=mix, qg=qg, kg=kg, zg=zg, bexp=bexp)
        else:
            p1 = _mm_nn(h, full["cv_w_pw1"], j, bias=row(full["cv_b_pw1"][j]), out_dtype=BF)
            ys, yc = _conf_mid_fwd(p1, cv_wdw[j], row(full["cv_b_dw"][j]), row(full["cv_ln_g"][j]),
                                   row(full["cv_ln_b"][j]))
            cur = _mm_nn(ys, full["cv_w_pw2"], j, bias=row(full["cv_b_pw2"][j]), resid=cur)
            rec.update(p1=p1, ys=ys, yc=yc)
        rec["x_mid"] = cur
        h = _rms_fwd(cur, row(full["ffn_norm_g"][i]))
        up = _mm_nn(h, full["ffn_w_up"], i, out_dtype=BF)
        act = _ffn_mid_fwd(up, ffn_wdw[i], row(full["ffn_b_dw"][i]))
        cur = _mm_nn(act, full["ffn_w_down"], i, resid=cur)
        rec.update(h_ffn=h, up=up, act=act)
        saved.append(rec)

    loss_vec, dy, dy_bf = _loss_grad(cur, tgt)
    loss = lax.psum(loss_vec[0, 0], ("x", "y", "c"))

    gbig = {name: None for name in BIG_COL + BIG_ROW}
    gsmall = {name: [None] * w[name].shape[0] for name in SMALL_SHARDED + SMALL_REPLICATED}
    n_of = {name: full[name].shape[0] for name in BIG_COL + BIG_ROW}
    for i in reversed(range(n_layers)):
        j = i // 2
        rec = saved[i]
        dact = _mm_nt(dy_bf, full["ffn_w_down"], i, out_dtype=BF)
        gbig["ffn_w_down"] = _mm_tn(rec["act"], dy_bf, 1, n_of["ffn_w_down"], i, gbig["ffn_w_down"])
        dup, dwdw, dbdw = _ffn_mid_bwd(rec["up"], dact, ffn_wdw[i], row(full["ffn_b_dw"][i]))
        gsmall["ffn_w_dw"][i] = dwdw[:FFN_K]
        gsmall["ffn_b_dw"][i] = dbdw[0]
        dh = _mm_nt(dup, full["ffn_w_up"], i)
        gbig["ffn_w_up"] = _mm_tn(rec["h_ffn"], dup, N_CHIPS, n_of["ffn_w_up"], i, gbig["ffn_w_up"])
        dy, dy_bf, dg = _rms_bwd(dh, rec["x_mid"], row(full["ffn_norm_g"][i]), dy)
        gsmall["ffn_norm_g"][i] = dg[0]
        if i % 2 == 0:
            dmix = _mm_nt(dy_bf, full["hyb_w_out"], j)
            gbig["hyb_w_out"] = _mm_tn(rec["mix"], dy_bf, 1, n_of["hyb_w_out"], j, gbig["hyb_w_out"])
            dq, dk, dv = _attn_bwd(rec["qkv"], rec["att_32"], dmix)
            dproj, dqg, dkg, dzg, dws, dbe = _mix_prep_bwd(
                rec["proj"], dq, dk, dv, dmix, rec["qg"], rec["kg"], rec["zg"], full["sg_w_spatial"], j, rec["bexp"],
                mean64, fold64)
            gsmall["sb_q_norm_g"][j] = dqg[0, :HEAD_DIM]
            gsmall["sb_k_norm_g"][j] = dkg[0, :HEAD_DIM]
            gsmall["sg_z_norm_g"][j] = dzg[0]
            gsmall["sg_w_spatial"][j] = dws
            gsmall["sg_b_spatial"][j] = dbe[:, ::HEAD_DIM].T
            dh = _mm_nt(dproj, full["sb_w_in"], j)
            gbig["sb_w_in"] = _mm_tn(rec["h_mix"], dproj, N_CHIPS, n_of["sb_w_in"], j, gbig["sb_w_in"])
        else:
            dys = _mm_nt(dy_bf, full["cv_w_pw2"], j, out_dtype=BF)
            gbig["cv_w_pw2"] = _mm_tn(rec["ys"], dy_bf, 1, n_of["cv_w_pw2"], j, gbig["cv_w_pw2"])
            dp1, dwdw, dbdw, dlg, dlb, db1, db2 = _conf_mid_bwd(
                rec["p1"], rec["yc"], dys, dy, cv_wdw[j], row(full["cv_ln_g"][j]), row(full["cv_ln_b"][j]))
            gsmall["cv_w_dw"][j] = dwdw[:CONV_K]
            gsmall["cv_b_dw"][j] = dbdw[0]
            gsmall["cv_ln_g"][j] = dlg[0]
            gsmall["cv_ln_b"][j] = dlb[0]
            gsmall["cv_b_pw1"][j] = db1[0]
            gsmall["cv_b_pw2"][j] = db2[0]
            dh = _mm_nt(dp1, full["cv_w_pw1"], j)
            gbig["cv_w_pw1"] = _mm_tn(rec["h_mix"], dp1, N_CHIPS, n_of["cv_w_pw1"], j, gbig["cv_w_pw1"])
        dy, dy_bf, dg = _rms_bwd(dh, rec["x_in"], row(full["mix_norm_g"][i]), dy)
        gsmall["mix_norm_g"][i] = dg[0]

    grads = {}
    for name in BIG_COL:
        grads[name] = _reduce_scatter(gbig[name], BF)
    for name in BIG_ROW:
        g4 = gbig[name]
        r = g4.shape[2] // N_CHIPS
        grads[name] = _reduce_scatter(g4.reshape(g4.shape[0], N_CHIPS, r, g4.shape[3]), BF)
    small_names = SMALL_REPLICATED + SMALL_SHARDED
    small_full = [jnp.stack(gsmall[name]) for name in small_names]
    packed = _pack(small_full, 32 * N_CHIPS)
    rows_q = packed.shape[0] // N_CHIPS
    summed = _reduce_scatter(packed.reshape(1, N_CHIPS, rows_q, LANES), F32)
    summed = _all_gather(summed).reshape(-1, LANES)
    for name, gsum in zip(small_names, _unpack(summed, [a.shape for a in small_full])):
        if name in SMALL_SHARDED:
            n_loc = w[name].shape[-1]
            split = gsum.reshape(gsum.shape[:-1] + (N_CHIPS, n_loc))
            gsum = lax.dynamic_index_in_dim(split, chip, axis=split.ndim - 2, keepdims=False)
        grads[name] = gsum

    delta, new_m, new_v = {}, {}, {}
    for name in BIG_COL + BIG_ROW:
        shp = w[name].shape
        two_d = lambda a: a.reshape(shp[0] * shp[1], shp[2])
        d, nm, nv = _adamw(two_d(w[name]), two_d(grads[name]), two_d(m[name]), two_d(v[name]))
        delta[name], new_m[name], new_v[name] = d.reshape(shp), nm.reshape(shp), nv.reshape(shp)
    shapes = [w[name].shape for name in small_names]
    d, nm, nv = _adamw(*(_pack([src[name] for name in small_names], 256) for src in (w, grads, m, v)))
    for name, a, b_, c_ in zip(small_names, _unpack(d, shapes), _unpack(nm, shapes), _unpack(nv, shapes)):
        delta[name], new_m[name], new_v[name] = a, b_, c_

    return (loss, dy, *[grads[n] for n in WEIGHTS], *[delta[n] for n in WEIGHTS],
            *[new_m[n] for n in WEIGHTS], *[new_v[n] for n in WEIGHTS])


def kernel(x, mix_norm_g, sb_w_in, sb_q_norm_g, sb_k_norm_g, sg_z_norm_g, sg_w_spatial, sg_b_spatial, hyb_w_out, cv_w_pw1, cv_b_pw1, cv_w_dw, cv_b_dw, cv_ln_g, cv_ln_b, cv_w_pw2, cv_b_pw2, ffn_norm_g, ffn_w_up, ffn_w_dw, ffn_b_dw, ffn_w_down, loss_target, m_mix_norm_g, m_sb_w_in, m_sb_q_norm_g, m_sb_k_norm_g, m_sg_z_norm_g, m_sg_w_spatial, m_sg_b_spatial, m_hyb_w_out, m_cv_w_pw1, m_cv_b_pw1, m_cv_w_dw, m_cv_b_dw, m_cv_ln_g, m_cv_ln_b, m_cv_w_pw2, m_cv_b_pw2, m_ffn_norm_g, m_ffn_w_up, m_ffn_w_dw, m_ffn_b_dw, m_ffn_w_down, v_mix_norm_g, v_sb_w_in, v_sb_q_norm_g, v_sb_k_norm_g, v_sg_z_norm_g, v_sg_w_spatial, v_sg_b_spatial, v_hyb_w_out, v_cv_w_pw1, v_cv_b_pw1, v_cv_w_dw, v_cv_b_dw, v_cv_ln_g, v_cv_ln_b, v_cv_w_pw2, v_cv_b_pw2, v_ffn_norm_g, v_ffn_w_up, v_ffn_w_dw, v_ffn_b_dw, v_ffn_w_down):
    given = dict(locals())
    w = {n: given[n] for n in WEIGHTS}
    m = {n: given["m_" + n] for n in WEIGHTS}
    v = {n: given["v_" + n] for n in WEIGHTS}
    out = _step(x[0], loss_target[0], w, m, v)
    return (out[0], out[1][None], *out[2:])
```

```python
import functools

import jax
import jax.numpy as jnp
from jax import lax
from jax.experimental import pallas as pl
from jax.experimental.pallas import tpu as pltpu

F32 = jnp.float32
BF = jnp.bfloat16
SDS = jax.ShapeDtypeStruct
HI = lax.Precision.HIGHEST
MESH = pl.DeviceIdType.MESH

NORM_EPS = 1e-6
HEAD_DIM = 64
ATT_BLOCK = 128
CHUNK = 128
CONV_K = 31
CONV_HALO = 32
FFN_K = 3
FFN_HALO = 16
LANES = 128
N_CHIPS = 4
VMEM_LIMIT_BYTES = 56 * 2**20

ADAM_LR = 0.001
ADAM_B1 = 0.9
ADAM_B2 = 0.999
ADAM_EPS = 1e-08
ADAM_WD = 0.01
ADAM_STEP = 10

NT_DIMS = (((1,), (1,)), ((), ()))
TN_DIMS = (((0,), (0,)), ((), ()))


def _call(body, **kw):
    return pl.pallas_call(body, **kw)


def _params(*sem):
    return pltpu.CompilerParams(dimension_semantics=sem, vmem_limit_bytes=VMEM_LIMIT_BYTES)


def _gelu(x):
    return 0.5 * x * (1.0 + lax.erf(x * 0.7071067811865476))


def _rms(x, g):
    y = x * lax.rsqrt(jnp.mean(x * x, axis=-1, keepdims=True) + NORM_EPS)
    return y * g


def _rms_fwd(x, g):
    t, d = x.shape
    tm = min(512, t)

    def body(x_ref, g_ref, o_ref):
        o_ref[...] = _rms(x_ref[...], g_ref[...]).astype(o_ref.dtype)

    return _call(
        body, name="rms_fwd", grid=(t // tm,),
        in_specs=[pl.BlockSpec((tm, d), lambda i: (i, 0)), pl.BlockSpec((1, d), lambda i: (0, 0))],
        out_specs=pl.BlockSpec((tm, d), lambda i: (i, 0)),
        out_shape=SDS((t, d), BF), compiler_params=_params("parallel"))(x, g)


def _rms_bwd(dh, x, g, dres):
    t, d = x.shape
    tm = min(512, t)

    def body(dh_ref, x_ref, g_ref, r_ref, dx_ref, dxb_ref, dg_ref):
        _, vjp = jax.vjp(_rms, x_ref[...], g_ref[...])
        dx, dg = vjp(dh_ref[...])
        dx = dx + r_ref[...]
        dx_ref[...] = dx
        dxb_ref[...] = dx.astype(BF)

        @pl.when(pl.program_id(0) == 0)
        def _():
            dg_ref[...] = jnp.zeros_like(dg_ref)

        dg_ref[...] += dg

    row = pl.BlockSpec((tm, d), lambda i: (i, 0))
    vec = pl.BlockSpec((1, d), lambda i: (0, 0))
    return _call(
        body, name="rms_bwd", grid=(t // tm,), in_specs=[row, row, vec, row], out_specs=[row, row, vec],
        out_shape=[SDS((t, d), F32), SDS((t, d), BF), SDS((1, d), F32)],
        compiler_params=_params("arbitrary"))(dh, x, g, dres)


def _mm_nn(a, w, l, bias=None, resid=None, out_dtype=F32):
    m, k = a.shape
    _, p_n, kw, n = w.shape
    assert k == kw
    tm = min(1024, m)
    tn = n if k * n * 2 <= 4 * 2**20 else n // 2
    nj = n // tn
    in_specs = [pl.BlockSpec((tm, k), lambda i, p, j: (i, 0)),
                pl.BlockSpec((None, None, k, tn), lambda i, p, j: (l, p, 0, j))]
    args = [a, w]
    if bias is not None:
        in_specs.append(pl.BlockSpec((1, tn), lambda i, p, j: (0, p * nj + j)))
        args.append(bias)
    if resid is not None:
        in_specs.append(pl.BlockSpec((tm, tn), lambda i, p, j: (i, p * nj + j)))
        args.append(resid)

    def body(*refs):
        acc = jnp.dot(refs[0][...], refs[1][...], preferred_element_type=F32)
        nxt = 2
        if bias is not None:
            acc = acc + refs[nxt][...]
            nxt += 1
        if resid is not None:
            acc = refs[nxt][...] + acc
        refs[-1][...] = acc.astype(refs[-1].dtype)

    return _call(
        body, name="mm_nn", grid=(m // tm, p_n, nj), in_specs=in_specs,
        out_specs=pl.BlockSpec((tm, tn), lambda i, p, j: (i, p * nj + j)),
        out_shape=SDS((m, p_n * n), out_dtype),
        compiler_params=_params("parallel", "parallel", "parallel"))(*args)


def _mm_nt(dy, w, l, out_dtype=F32):
    m, n_all = dy.shape
    _, p_n, r, n = w.shape
    assert n_all == p_n * n
    tm = min(512, m)

    def body(dy_ref, w_ref, o_ref):
        acc = lax.dot_general(dy_ref[:, 0:n], w_ref[0], NT_DIMS, preferred_element_type=F32)
        for p in range(1, p_n):
            acc = acc + lax.dot_general(dy_ref[:, p * n:(p + 1) * n], w_ref[p], NT_DIMS, preferred_element_type=F32)
        o_ref[...] = acc.astype(o_ref.dtype)

    return _call(
        body, name="mm_nt", grid=(m // tm,),
        in_specs=[pl.BlockSpec((tm, n_all), lambda i: (i, 0)),
                  pl.BlockSpec((None, p_n, r, n), lambda i: (l, 0, 0, 0))],
        out_specs=pl.BlockSpec((tm, r), lambda i: (i, 0)),
        out_shape=SDS((m, r), out_dtype),
        compiler_params=_params("parallel"))(dy, w)


def _mm_tn(a, dy, p_n, n_layers, l, buf=None):
    m, k = a.shape
    n = dy.shape[1] // p_n
    tm = min(2048, m)
    tk = k if k <= 1024 else k // 2
    nm = m // tm

    def body(a_ref, dy_ref, *rest):
        o_ref, acc_ref = rest[-2], rest[-1]
        mi = pl.program_id(2)
        part = lax.dot_general(a_ref[...], dy_ref[...], TN_DIMS, preferred_element_type=F32)

        @pl.when(mi == 0)
        def _():
            acc_ref[...] = part

        @pl.when(mi > 0)
        def _():
            acc_ref[...] += part

        @pl.when(mi == nm - 1)
        def _():
            o_ref[...] = acc_ref[...].astype(o_ref.dtype)

    in_specs = [pl.BlockSpec((tm, tk), lambda p, kk, mi: (mi, kk)),
                pl.BlockSpec((tm, n), lambda p, kk, mi: (mi, p))]
    args = [a, dy]
    aliases = {}
    if buf is not None:
        in_specs.append(pl.BlockSpec(memory_space=pl.ANY))
        args.append(buf)
        aliases = {2: 0}
    return _call(
        body, name="mm_tn", grid=(p_n, k // tk, nm), in_specs=in_specs,
        out_specs=pl.BlockSpec((None, None, tk, n), lambda p, kk, mi: (l, p, kk, 0)),
        out_shape=SDS((n_layers, p_n, k, n), BF), scratch_shapes=[pltpu.VMEM((tk, n), F32)],
        input_output_aliases=aliases,
        compiler_params=_params("parallel", "parallel", "arbitrary"))(*args)


def _loss_grad(y, tgt):
    t, d = y.shape
    tm = min(512, t)

    def body(y_ref, t_ref, l_ref, d_ref, db_ref):
        err = y_ref[...] - t_ref[...]
        dy = err * (1.0 / d)
        d_ref[...] = dy
        db_ref[...] = dy.astype(BF)
        part = 0.5 * jnp.sum(jnp.sum(err * err, axis=1, keepdims=True) * (1.0 / d), axis=0, keepdims=True)

        @pl.when(pl.program_id(0) == 0)
        def _():
            l_ref[...] = jnp.zeros_like(l_ref)

        l_ref[...] += jnp.broadcast_to(part, l_ref.shape)

    row = pl.BlockSpec((tm, d), lambda i: (i, 0))
    return _call(
        body, name="loss_grad", grid=(t // tm,), in_specs=[row, row],
        out_specs=[pl.BlockSpec((1, LANES), lambda i: (0, 0)), row, row],
        out_shape=[SDS((1, LANES), F32), SDS((t, d), F32), SDS((t, d), BF)],
        compiler_params=_params("arbitrary"))(y, tgt)


def _prev_halo(tr, halo, col):
    return lambda i: (jnp.maximum(i * (tr // halo) - 1, 0), col)


def _next_halo(tr, halo, n_rows, col):
    return lambda i: (jnp.minimum((i + 1) * (tr // halo), n_rows // halo - 1), col)


def _ffn_mid_fwd(up, w_dw, b_dw):
    t, f2 = up.shape
    f = f2 // 2
    tr = min(256, t)
    h = FFN_HALO

    def body(g_ref, gp_ref, v_ref, w_ref, b_ref, o_ref, xp_ref):
        i = pl.program_id(0)
        xp_ref[pl.ds(0, h), :] = jnp.where(i > 0, gp_ref[...].astype(F32), 0.0)
        xp_ref[pl.ds(h, tr), :] = g_ref[...].astype(F32)

        def strip(c, carry):
            col = pl.ds(pl.multiple_of(c * LANES, LANES), LANES)
            gc = jnp.broadcast_to(b_ref[:, col], (tr, LANES))
            for k in range(FFN_K):
                gc = gc + w_ref[pl.ds(k, 1), col] * xp_ref[pl.ds(h - (FFN_K - 1 - k), tr), col]
            o_ref[:, col] = (gc * jax.nn.sigmoid(gc) * v_ref[:, col].astype(F32)).astype(o_ref.dtype)
            return carry

        lax.fori_loop(0, f // LANES, strip, 0)

    return _call(
        body, name="ffn_mid_fwd", grid=(t // tr,),
        in_specs=[pl.BlockSpec((tr, f), lambda i: (i, 0)), pl.BlockSpec((h, f), _prev_halo(tr, h, 0)),
                  pl.BlockSpec((tr, f), lambda i: (i, 1)),
                  pl.BlockSpec((8, f), lambda i: (0, 0)), pl.BlockSpec((1, f), lambda i: (0, 0))],
        out_specs=pl.BlockSpec((tr, f), lambda i: (i, 0)), out_shape=SDS((t, f), BF),
        scratch_shapes=[pltpu.VMEM((h + tr, f), F32)], compiler_params=_params("parallel"))(up, up, up, w_dw, b_dw)


def _ffn_mid_bwd(up, da, w_dw, b_dw):
    t, f2 = up.shape
    f = f2 // 2
    tr = min(256, t)
    h = FFN_HALO
    n_tiles = t // tr

    def body(g_ref, gp_ref, gn_ref, v_ref, vn_ref, da_ref, dan_ref, w_ref, b_ref, dup_ref, dw_ref, db_ref, xg_ref, dgc_ref):
        i = pl.program_id(0)
        xg_ref[pl.ds(0, h), :] = jnp.where(i > 0, gp_ref[...].astype(F32), 0.0)
        xg_ref[pl.ds(h, tr), :] = g_ref[...].astype(F32)
        xg_ref[pl.ds(h + tr, h), :] = gn_ref[...].astype(F32)
        last = i == n_tiles - 1

        @pl.when(i == 0)
        def _():
            dw_ref[...] = jnp.zeros_like(dw_ref)
            db_ref[...] = jnp.zeros_like(db_ref)

        def dsilu_gate(rows0, n_rows, dav, vv, col):
            gc = jnp.broadcast_to(b_ref[:, col], (n_rows, LANES))
            for k in range(FFN_K):
                gc = gc + w_ref[pl.ds(k, 1), col] * xg_ref[pl.ds(h + rows0 - (FFN_K - 1 - k), n_rows), col]
            sg = jax.nn.sigmoid(gc)
            return gc * sg, dav * vv * (sg * (1.0 + gc * (1.0 - sg)))

        def strip(c, carry):
            col = pl.ds(pl.multiple_of(c * LANES, LANES), LANES)
            dav = da_ref[:, col].astype(F32)
            silu_gc, dgc = dsilu_gate(0, tr, dav, v_ref[:, col].astype(F32), col)
            dup_ref[:, pl.ds(pl.multiple_of(f + c * LANES, LANES), LANES)] = (dav * silu_gc).astype(dup_ref.dtype)
            dgc_ref[pl.ds(0, tr), col] = dgc
            _, dgc_next = dsilu_gate(tr, h, dan_ref[:, col].astype(F32), vn_ref[:, col].astype(F32), col)
            dgc_ref[pl.ds(tr, h), col] = jnp.where(last, 0.0, dgc_next)
            dg = jnp.zeros((tr, LANES), F32)
            for k in range(FFN_K):
                s = FFN_K - 1 - k
                dg = dg + w_ref[pl.ds(k, 1), col] * dgc_ref[pl.ds(s, tr), col]
                dw_ref[pl.ds(k, 1), col] += jnp.sum(xg_ref[pl.ds(h - s, tr), col] * dgc, axis=0, keepdims=True)
            dup_ref[:, col] = dg.astype(dup_ref.dtype)
            db_ref[:, col] += jnp.sum(dgc, axis=0, keepdims=True)
            return carry

        lax.fori_loop(0, f // LANES, strip, 0)

    tile = lambda col: pl.BlockSpec((tr, f), lambda i: (i, col))
    nxt = lambda col: pl.BlockSpec((h, f), _next_halo(tr, h, t, col))
    return _call(
        body, name="ffn_mid_bwd", grid=(n_tiles,),
        in_specs=[tile(0), pl.BlockSpec((h, f), _prev_halo(tr, h, 0)), nxt(0), tile(1), nxt(1), tile(0), nxt(0),
                  pl.BlockSpec((8, f), lambda i: (0, 0)), pl.BlockSpec((1, f), lambda i: (0, 0))],
        out_specs=[pl.BlockSpec((tr, f2), lambda i: (i, 0)), pl.BlockSpec((8, f), lambda i: (0, 0)),
                   pl.BlockSpec((1, f), lambda i: (0, 0))],
        out_shape=[SDS((t, f2), BF), SDS((8, f), F32), SDS((1, f), F32)],
        scratch_shapes=[pltpu.VMEM((h + tr + h, f), F32), pltpu.VMEM((tr + h, f), F32)],
        compiler_params=_params("arbitrary"))(up, up, up, up, up, da, da, w_dw, b_dw)


def _ln_silu(yc, g, b):
    mu = jnp.mean(yc, axis=-1, keepdims=True)
    xc = yc - mu
    y = xc * lax.rsqrt(jnp.mean(xc * xc, axis=-1, keepdims=True) + NORM_EPS)
    return jax.nn.silu(y * g + b)


SUBLANES = 8
CONV_PAD = 24


def _glu(a, g):
    return a.astype(F32) * jax.nn.sigmoid(g.astype(F32))


def _glu_strip(ygs_ref, first_tile, a_ref, ap_ref, g_ref, gp_ref, col, h, tr):
    ygs_ref[pl.ds(0, h), :] = jnp.where(first_tile, 0.0, _glu(ap_ref[:, col], gp_ref[:, col]))
    ygs_ref[pl.ds(h, tr), :] = _glu(a_ref[:, col], g_ref[:, col])


def _shift_past(sh_ref, ygs_ref, h, n):
    for r in range(1, SUBLANES):
        sh_ref[r, pl.ds(0, n + CONV_PAD), :] = ygs_ref[pl.ds(h - CONV_PAD - r, n + CONV_PAD), :]


def _past_rows(sh_ref, ygs_ref, h, n, s):
    a, r = divmod(s, SUBLANES)
    if r == 0:
        return ygs_ref[pl.ds(h - SUBLANES * a, n), :]
    return sh_ref[r, pl.ds(CONV_PAD - SUBLANES * a, n), :]


def _conf_mid_fwd(p1, w_dw, b_dw, ln_g, ln_b):
    t, w2 = p1.shape
    w = w2 // 2
    tr = min(256, t)
    h = CONV_HALO
    rc = 32

    def body(a_ref, ap_ref, g_ref, gp_ref, w_ref, b_ref, lg_ref, lb_ref, o_ref, yc_ref, ygs_ref, sh_ref):
        first_tile = pl.program_id(0) == 0

        def strip(c, carry):
            col = pl.ds(pl.multiple_of(c * LANES, LANES), LANES)
            _glu_strip(ygs_ref, first_tile, a_ref, ap_ref, g_ref, gp_ref, col, h, tr)
            _shift_past(sh_ref, ygs_ref, h, tr)
            acc = jnp.broadcast_to(b_ref[:, col], (tr, LANES))
            for k in range(CONV_K):
                acc = acc + w_ref[pl.ds(k, 1), col] * _past_rows(sh_ref, ygs_ref, h, tr, CONV_K - 1 - k)
            yc_ref[:, col] = acc
            return carry

        lax.fori_loop(0, w // LANES, strip, 0)

        def rows(r, carry):
            rs = pl.ds(pl.multiple_of(r * rc, rc), rc)
            o_ref[rs, :] = _ln_silu(yc_ref[rs, :], lg_ref[...], lb_ref[...]).astype(o_ref.dtype)
            return carry

        lax.fori_loop(0, tr // rc, rows, 0)

    vec = pl.BlockSpec((1, w), lambda i: (0, 0))
    tile = pl.BlockSpec((tr, w), lambda i: (i, 0))
    return _call(
        body, name="conf_mid_fwd", grid=(t // tr,),
        in_specs=[tile, pl.BlockSpec((h, w), _prev_halo(tr, h, 0)),
                  pl.BlockSpec((tr, w), lambda i: (i, 1)), pl.BlockSpec((h, w), _prev_halo(tr, h, 1)),
                  pl.BlockSpec((32, w), lambda i: (0, 0)), vec, vec, vec],
        out_specs=[tile, tile], out_shape=[SDS((t, w), BF), SDS((t, w), F32)],
        scratch_shapes=[pltpu.VMEM((h + tr, LANES), F32), pltpu.VMEM((SUBLANES, tr + CONV_PAD, LANES), F32)],
        compiler_params=_params("parallel"))(p1, p1, p1, p1, w_dw, b_dw, ln_g, ln_b)


def _conf_mid_bwd(p1, yc, dys, dy, w_dw, ln_g, ln_b):
    t, w2 = p1.shape
    w = w2 // 2
    tr = min(256, t)
    h = CONV_HALO
    rc = 32
    n_tiles = t // tr

    def body(a_ref, ap_ref, g_ref, gp_ref, yc_ref, ycn_ref, dys_ref, dysn_ref, dy_ref, w_ref, lg_ref, lb_ref,
             dp_ref, dw_ref, db_ref, dlg_ref, dlb_ref, db1_ref, db2_ref, dyc_ref, ygs_ref, sh_ref, shf_ref):
        i = pl.program_id(0)
        last = i == n_tiles - 1

        @pl.when(i == 0)
        def _():
            for ref in (dw_ref, db_ref, dlg_ref, dlb_ref, db1_ref, db2_ref):
                ref[...] = jnp.zeros_like(ref)

        def ln_rows(r, carry):
            rs = pl.ds(pl.multiple_of(r * rc, rc), rc)
            _, vjp = jax.vjp(_ln_silu, yc_ref[rs, :], lg_ref[...], lb_ref[...])
            dyc, dlg, dlb = vjp(dys_ref[rs, :].astype(F32))
            dyc_ref[rs, :] = dyc
            dlg_ref[...] += dlg
            dlb_ref[...] += dlb
            return carry

        lax.fori_loop(0, tr // rc, ln_rows, 0)
        _, vjp = jax.vjp(_ln_silu, ycn_ref[...], lg_ref[...], lb_ref[...])
        dyc_ref[pl.ds(tr, h), :] = jnp.where(last, 0.0, vjp(dysn_ref[...].astype(F32))[0])
        db2_ref[...] += jnp.sum(dy_ref[...], axis=0, keepdims=True)

        def back(c, carry):
            col = pl.ds(pl.multiple_of(c * LANES, LANES), LANES)
            gcol = pl.ds(pl.multiple_of(w + c * LANES, LANES), LANES)
            _glu_strip(ygs_ref, i == 0, a_ref, ap_ref, g_ref, gp_ref, col, h, tr)
            _shift_past(sh_ref, ygs_ref, h, tr)
            for r in range(1, SUBLANES):
                shf_ref[r, pl.ds(0, tr + CONV_PAD), :] = dyc_ref[pl.ds(r, tr + CONV_PAD), col]
            dyc = dyc_ref[pl.ds(0, tr), col]
            dyg = jnp.zeros((tr, LANES), F32)
            for k in range(CONV_K):
                s = CONV_K - 1 - k
                a, r = divmod(s, SUBLANES)
                if r == 0:
                    future = dyc_ref[pl.ds(SUBLANES * a, tr), col]
                else:
                    future = shf_ref[r, pl.ds(SUBLANES * a, tr), :]
                dyg = dyg + w_ref[pl.ds(k, 1), col] * future
                dw_ref[pl.ds(k, 1), col] += jnp.sum(_past_rows(sh_ref, ygs_ref, h, tr, s) * dyc, axis=0, keepdims=True)
            db_ref[:, col] += jnp.sum(dyc, axis=0, keepdims=True)
            sg = jax.nn.sigmoid(g_ref[:, col].astype(F32))
            da = dyg * sg
            dg = dyg * a_ref[:, col].astype(F32) * sg * (1.0 - sg)
            dp_ref[:, col] = da.astype(dp_ref.dtype)
            dp_ref[:, gcol] = dg.astype(dp_ref.dtype)
            db1_ref[:, col] += jnp.sum(da, axis=0, keepdims=True)
            db1_ref[:, gcol] += jnp.sum(dg, axis=0, keepdims=True)
            return carry

        lax.fori_loop(0, w // LANES, back, 0)

    tile = lambda col: pl.BlockSpec((tr, w), lambda i: (i, col))
    prv = lambda col: pl.BlockSpec((h, w), _prev_halo(tr, h, col))
    nxt = pl.BlockSpec((h, w), _next_halo(tr, h, t, 0))
    vec = pl.BlockSpec((1, w), lambda i: (0, 0))
    return _call(
        body, name="conf_mid_bwd", grid=(n_tiles,),
        in_specs=[tile(0), prv(0), tile(1), prv(1), tile(0), nxt, tile(0), nxt, tile(0),
                  pl.BlockSpec((32, w), lambda i: (0, 0)), vec, vec],
        out_specs=[pl.BlockSpec((tr, w2), lambda i: (i, 0)), pl.BlockSpec((32, w), lambda i: (0, 0)), vec, vec, vec,
                   pl.BlockSpec((1, w2), lambda i: (0, 0)), vec],
        out_shape=[SDS((t, w2), BF), SDS((32, w), F32), SDS((1, w), F32), SDS((1, w), F32), SDS((1, w), F32),
                   SDS((1, w2), F32), SDS((1, w), F32)],
        scratch_shapes=[pltpu.VMEM((tr + h, w), F32), pltpu.VMEM((h + tr, LANES), F32),
                        pltpu.VMEM((SUBLANES, tr + CONV_PAD, LANES), F32), pltpu.VMEM((SUBLANES, tr + CONV_PAD, LANES), F32)],
        compiler_params=_params("arbitrary"))(p1, p1, p1, p1, yc, yc, dys, dys, dy, w_dw, ln_g, ln_b)


def _group_matrices():
    i = lax.broadcasted_iota(jnp.int32, (512, 512), 0)
    j = lax.broadcasted_iota(jnp.int32, (512, 512), 1)
    mean64 = jnp.where(i // HEAD_DIM == j // HEAD_DIM, 1.0 / HEAD_DIM, 0.0).astype(F32)
    fold64 = jnp.where(i % HEAD_DIM == j % HEAD_DIM, 1.0, 0.0).astype(F32)
    return mean64, fold64


def _split_dot(x, mat):
    hi = x.astype(BF)
    lo = (x - hi.astype(F32)).astype(BF)
    mb = mat.astype(BF)
    return jnp.dot(hi, mb, preferred_element_type=F32) + jnp.dot(lo, mb, preferred_element_type=F32)


@jax.custom_vjp
def _group_sum(x, mat):
    return _split_dot(x, mat)


_group_sum.defvjp(lambda x, mat: (_split_dot(x, mat), mat), lambda mat, ct: (_split_dot(ct, mat), jnp.zeros_like(mat)))


def _bf_dot_plain(a, b):
    return jnp.dot(a.astype(BF), b.astype(BF), preferred_element_type=F32)


@jax.custom_vjp
def _bf_dot(a, b):
    return _bf_dot_plain(a, b)


def _bf_dot_bwd(res, ct):
    a, b = res
    cb = ct.astype(BF)
    return (lax.dot_general(cb, b.astype(BF), NT_DIMS, preferred_element_type=F32),
            lax.dot_general(a.astype(BF), cb, TN_DIMS, preferred_element_type=F32))


_bf_dot.defvjp(lambda a, b: (_bf_dot_plain(a, b), (a, b)), _bf_dot_bwd)


def _prep_tile(proj, qg, kg, zg, ws, bexp, mean64, differentiated=False):
    sw = 512
    q, k, v, u, z = (proj[:, n * sw:(n + 1) * sw] for n in range(5))
    group_sum, dot = (_group_sum, _bf_dot) if differentiated else (_split_dot, _bf_dot_plain)

    def group_norm(x):
        return x * lax.rsqrt(group_sum(x * x, mean64) + NORM_EPS)

    qn = group_norm(q) * qg
    kn = group_norm(k) * kg
    zn = group_norm(_gelu(z)) * zg
    row = lax.broadcasted_iota(jnp.int32, (CHUNK, CHUNK), 0)
    col = lax.broadcasted_iota(jnp.int32, (CHUNK, CHUNK), 1)
    first = lax.broadcasted_iota(jnp.int32, (1, LANES), 1) < HEAD_DIM
    parts = []
    for pr in range(sw // LANES):
        zp = zn[:, pr * LANES:(pr + 1) * LANES]
        s0 = dot(jnp.where(col <= row, ws[2 * pr], 0.0), zp)
        s1 = dot(jnp.where(col <= row, ws[2 * pr + 1], 0.0), zp)
        parts.append(jnp.where(first, s0, s1))
    s = jnp.concatenate(parts, axis=1) + bexp
    return qn, kn, v, _gelu(u) * s


def _mix_prep_fwd(proj, qg, kg, zg, w_s, l, bexp, mean64):
    t = proj.shape[0]
    tr = CHUNK

    def body(p_ref, qg_ref, kg_ref, zg_ref, ws_ref, be_ref, m_ref, qkv_ref, go_ref):
        qn, kn, v, go = _prep_tile(p_ref[...], qg_ref[...], kg_ref[...], zg_ref[...], ws_ref[...], be_ref[...], m_ref[...])
        qkv_ref[:, 0:512] = qn.astype(BF)
        qkv_ref[:, 512:1024] = kn.astype(BF)
        qkv_ref[:, 1024:1536] = v.astype(BF)
        go_ref[...] = go.astype(BF)

    vec = pl.BlockSpec((1, 512), lambda i: (0, 0))
    return _call(
        body, name="mix_prep_fwd", grid=(t // tr,),
        in_specs=[pl.BlockSpec((tr, 2560), lambda i: (i, 0)), vec, vec, vec,
                  pl.BlockSpec((None, 8, CHUNK, CHUNK), lambda i: (l, 0, 0, 0)),
                  pl.BlockSpec((CHUNK, 512), lambda i: (0, 0)), pl.BlockSpec((512, 512), lambda i: (0, 0))],
        out_specs=[pl.BlockSpec((tr, 1536), lambda i: (i, 0)), pl.BlockSpec((tr, 512), lambda i: (i, 0))],
        out_shape=[SDS((t, 1536), BF), SDS((t, 512), BF)],
        compiler_params=_params("parallel"))(proj, qg, kg, zg, w_s, bexp, mean64)


def _mix_prep_bwd(proj, dq, dk, dv, dmix, qg, kg, zg, w_s, l, bexp, mean64, fold64):
    t = proj.shape[0]
    tr = CHUNK
    n_tiles = t // tr

    def body(p_ref, dq_ref, dk_ref, dv_ref, dgo_ref, qg_ref, kg_ref, zg_ref, ws_ref, be_ref, m_ref, f_ref,
             dp_ref, dqg_ref, dkg_ref, dzg_ref, dws_ref, dbe_ref):
        i = pl.program_id(0)

        @pl.when(i == 0)
        def _():
            for ref in (dqg_ref, dkg_ref, dzg_ref, dws_ref, dbe_ref):
                ref[...] = jnp.zeros_like(ref)

        fn = functools.partial(_prep_tile, mean64=m_ref[...], differentiated=True)
        _, vjp = jax.vjp(fn, p_ref[...], qg_ref[...], kg_ref[...], zg_ref[...], ws_ref[...], be_ref[...])
        dp, dqg, dkg, dzg, dws, dbe = vjp((dq_ref[...], dk_ref[...], dv_ref[...], dgo_ref[...]))
        dp_ref[...] = dp.astype(BF)
        dqg_ref[pl.ds(0, 1), :] += dqg
        dkg_ref[pl.ds(0, 1), :] += dkg
        dzg_ref[pl.ds(0, 1), :] += dzg
        dws_ref[...] += dws
        dbe_ref[...] += dbe

        @pl.when(i == n_tiles - 1)
        def _():
            dqg_ref[...] = jnp.dot(dqg_ref[...], f_ref[...], precision=HI, preferred_element_type=F32)
            dkg_ref[...] = jnp.dot(dkg_ref[...], f_ref[...], precision=HI, preferred_element_type=F32)
            dbe_ref[...] = jnp.dot(dbe_ref[...], m_ref[...] * float(HEAD_DIM), precision=HI, preferred_element_type=F32)

    vec = pl.BlockSpec((1, 512), lambda i: (0, 0))
    acc = pl.BlockSpec((8, 512), lambda i: (0, 0))
    sq = pl.BlockSpec((512, 512), lambda i: (0, 0))
    row = pl.BlockSpec((tr, 512), lambda i: (i, 0))
    return _call(
        body, name="mix_prep_bwd", grid=(n_tiles,),
        in_specs=[pl.BlockSpec((tr, 2560), lambda i: (i, 0)), row, row, row, pl.BlockSpec((tr, 512), lambda i: (i, 1)),
                  vec, vec, vec, pl.BlockSpec((None, 8, CHUNK, CHUNK), lambda i: (l, 0, 0, 0)),
                  pl.BlockSpec((CHUNK, 512), lambda i: (0, 0)), sq, sq],
        out_specs=[pl.BlockSpec((tr, 2560), lambda i: (i, 0)), acc, acc, acc,
                   pl.BlockSpec((8, CHUNK, CHUNK), lambda i: (0, 0, 0)), pl.BlockSpec((CHUNK, 512), lambda i: (0, 0))],
        out_shape=[SDS((t, 2560), BF), SDS((8, 512), F32), SDS((8, 512), F32), SDS((8, 512), F32),
                   SDS((8, CHUNK, CHUNK), F32), SDS((CHUNK, 512), F32)],
        compiler_params=_params("arbitrary"))(proj, dq, dk, dv, dmix, qg, kg, zg, w_s, bexp, mean64, fold64)


def _sb_logs(qh, kb, valid):
    z = lax.dot_general(qh, kb, NT_DIMS, preferred_element_type=F32) * (HEAD_DIM ** -0.5)
    soft = jnp.log1p(jnp.exp(-jnp.abs(z)))
    lk_raw = -(jnp.maximum(z, 0.0) + soft)
    ls = -(jnp.maximum(-z, 0.0) + soft)
    return lk_raw, ls, jnp.where(valid, lk_raw, 0.0)


def _sb_weights(ls, run, tail, valid):
    return jnp.where(valid, jnp.exp(ls + run + tail), 0.0)


def _att_masks(b):
    row = lax.broadcasted_iota(jnp.int32, (b, b), 0)
    col = lax.broadcasted_iota(jnp.int32, (b, b), 1)
    first = lax.broadcasted_iota(jnp.int32, (1, LANES), 1) < HEAD_DIM
    return row, col, first


N_PAIRS = 4


def _load_kv(qkv_hbm, k_scr, v_scr, sems, group, width):
    ck = pltpu.make_async_copy(qkv_hbm.at[:, pl.ds(pl.multiple_of(512 + group * width, LANES), width)], k_scr, sems.at[0])
    cv = pltpu.make_async_copy(qkv_hbm.at[:, pl.ds(pl.multiple_of(1024 + group * width, LANES), width)], v_scr, sems.at[1])
    ck.start()
    cv.start()
    ck.wait()
    cv.wait()


def _split_heads(ref, pair, first):
    x = ref[:, pair * LANES:(pair + 1) * LANES]
    zero = jnp.zeros_like(x)
    return jnp.where(first, x, zero), jnp.where(first, zero, x)


def _any_weight_left(run_ref, n_heads):
    top = run_ref[0]
    for hh in range(1, n_heads):
        top = jnp.maximum(top, run_ref[hh])
    return jnp.max(jnp.exp(top)) > 0.0


def _attn_fwd(qkv, pairs_per_step=4):
    t = qkv.shape[0]
    b = ATT_BLOCK
    nq = t // b
    width = pairs_per_step * LANES
    n_heads = 2 * pairs_per_step

    def body(q_ref, qkv_hbm, ob_ref, o32_ref, k_scr, v_scr, acc_ref, run_ref, sems):
        group, qi = pl.program_id(0), pl.program_id(1)

        @pl.when(qi == 0)
        def _():
            _load_kv(qkv_hbm, k_scr, v_scr, sems, group, width)

        row, col, first = _att_masks(b)
        qh = [x for pr in range(pairs_per_step) for x in _split_heads(q_ref, pr, first)]
        upper = jnp.where(row > col, 1.0, 0.0).astype(BF)
        acc_ref[...] = jnp.zeros_like(acc_ref)
        run_ref[...] = jnp.zeros_like(run_ref)
        heads = range(n_heads)

        def step(carry):
            j, _ = carry
            rows = pl.ds(pl.multiple_of(j * b, b), b)
            valid = jnp.logical_or(j != qi, col < row)
            lanes = [pl.ds((hh // 2) * LANES, LANES) for hh in heads]
            logs = [_sb_logs(qh[hh], k_scr[rows, lanes[hh]], valid) for hh in heads]
            tails = [_split_dot(logs[hh][2], upper) for hh in heads]
            for hh in heads:
                wgt = _sb_weights(logs[hh][1], run_ref[hh], tails[hh], valid)
                acc_ref[hh] += jnp.dot(wgt.astype(BF), v_scr[rows, lanes[hh]], preferred_element_type=F32)
            for hh in heads:
                run_ref[hh] += jnp.sum(logs[hh][2], axis=1, keepdims=True)
            return j - 1, _any_weight_left(run_ref, n_heads)

        lax.while_loop(lambda c: jnp.logical_and(c[0] >= 0, c[1]), step, (qi, jnp.bool_(True)))
        for pr in range(pairs_per_step):
            out = jnp.where(first, acc_ref[2 * pr], acc_ref[2 * pr + 1])
            ob_ref[:, pr * LANES:(pr + 1) * LANES] = out.astype(BF)
            o32_ref[:, pr * LANES:(pr + 1) * LANES] = out

    blk = pl.BlockSpec((b, width), lambda g, qi: (qi, g))
    return _call(
        body, name="attn_fwd", grid=(N_PAIRS // pairs_per_step, nq),
        in_specs=[blk, pl.BlockSpec(memory_space=pl.ANY)], out_specs=[blk, blk],
        out_shape=[SDS((t, 512), BF), SDS((t, 512), F32)],
        scratch_shapes=[pltpu.VMEM((t, width), BF), pltpu.VMEM((t, width), BF),
                        pltpu.VMEM((n_heads, b, LANES), F32), pltpu.VMEM((n_heads, b, 1), F32),
                        pltpu.SemaphoreType.DMA((2,))],
        compiler_params=_params("arbitrary", "arbitrary"))(qkv, qkv)


def _attn_bwd(qkv, a32, dmix, pairs_per_step=2):
    t = qkv.shape[0]
    b = ATT_BLOCK
    nq = t // b
    width = pairs_per_step * LANES
    n_heads = 2 * pairs_per_step

    def body(q_ref, a_ref, da_ref, qkv_hbm, dq_ref, dk_hbm, dv_hbm,
             k_scr, v_scr, dk_scr, dv_scr, dqa_ref, run_ref, rung_ref, sems):
        group, qi = pl.program_id(0), pl.program_id(1)

        @pl.when(qi == 0)
        def _():
            _load_kv(qkv_hbm, k_scr, v_scr, sems, group, width)
            dk_scr[...] = jnp.zeros_like(dk_scr)
            dv_scr[...] = jnp.zeros_like(dv_scr)

        row, col, first = _att_masks(b)
        qh, dah, dtot = [], [], []
        for pr in range(pairs_per_step):
            qh += _split_heads(q_ref, pr, first)
            da = da_ref[:, pr * LANES:(pr + 1) * LANES]
            prod = da * a_ref[:, pr * LANES:(pr + 1) * LANES]
            dtot += [jnp.sum(jnp.where(first, prod, 0.0), axis=1, keepdims=True),
                     jnp.sum(jnp.where(first, 0.0, prod), axis=1, keepdims=True)]
            dah += [jnp.where(first, da, 0.0).astype(BF), jnp.where(first, 0.0, da).astype(BF)]
        upper = jnp.where(row > col, 1.0, 0.0).astype(BF)
        lower_incl = jnp.where(row >= col, 1.0, 0.0).astype(BF)
        dqa_ref[...] = jnp.zeros_like(dqa_ref)
        run_ref[...] = jnp.zeros_like(run_ref)
        rung_ref[...] = jnp.zeros_like(rung_ref)
        heads = range(n_heads)

        def step(carry):
            j, _ = carry
            rows = pl.ds(pl.multiple_of(j * b, b), b)
            valid = jnp.logical_or(j != qi, col < row)
            lanes = [pl.ds((hh // 2) * LANES, LANES) for hh in heads]
            logs = [_sb_logs(qh[hh], k_scr[rows, lanes[hh]], valid) for hh in heads]
            dps = [lax.dot_general(dah[hh], v_scr[rows, lanes[hh]], NT_DIMS, preferred_element_type=F32) for hh in heads]
            tails = [_split_dot(logs[hh][2], upper) for hh in heads]
            wgts = [_sb_weights(logs[hh][1], run_ref[hh], tails[hh], valid) for hh in heads]
            gs = [wgts[hh] * dps[hh] for hh in heads]
            g_froms = [_split_dot(gs[hh], lower_incl) for hh in heads]
            for hh in heads:
                lk_raw, ls, _ = logs[hh]
                dlk = jnp.where(valid, dtot[hh] - rung_ref[hh] - g_froms[hh], 0.0)
                dz = ((gs[hh] * jnp.exp(lk_raw) - dlk * jnp.exp(ls)) * (HEAD_DIM ** -0.5)).astype(BF)
                dqa_ref[hh] += jnp.dot(dz, k_scr[rows, lanes[hh]], preferred_element_type=F32)
                dk_scr[rows, lanes[hh]] += lax.dot_general(dz, qh[hh], TN_DIMS, preferred_element_type=F32)
                dv_scr[rows, lanes[hh]] += lax.dot_general(wgts[hh].astype(BF), dah[hh], TN_DIMS, preferred_element_type=F32)
            for hh in heads:
                rung_ref[hh] += jnp.sum(gs[hh], axis=1, keepdims=True)
                run_ref[hh] += jnp.sum(logs[hh][2], axis=1, keepdims=True)
            return j - 1, _any_weight_left(run_ref, n_heads)

        lax.while_loop(lambda c: jnp.logical_and(c[0] >= 0, c[1]), step, (qi, jnp.bool_(True)))
        for pr in range(pairs_per_step):
            dq_ref[:, pr * LANES:(pr + 1) * LANES] = jnp.where(first, dqa_ref[2 * pr], dqa_ref[2 * pr + 1])

        @pl.when(qi == nq - 1)
        def _():
            cols = pl.ds(pl.multiple_of(group * width, LANES), width)
            ck = pltpu.make_async_copy(dk_scr, dk_hbm.at[:, cols], sems.at[0])
            cv = pltpu.make_async_copy(dv_scr, dv_hbm.at[:, cols], sems.at[1])
            ck.start()
            cv.start()
            ck.wait()
            cv.wait()

    blk = pl.BlockSpec((b, width), lambda g, qi: (qi, g))
    anywhere = pl.BlockSpec(memory_space=pl.ANY)
    return _call(
        body, name="attn_bwd", grid=(N_PAIRS // pairs_per_step, nq),
        in_specs=[blk, blk, blk, anywhere], out_specs=[blk, anywhere, anywhere],
        out_shape=[SDS((t, 512), F32), SDS((t, 512), F32), SDS((t, 512), F32)],
        scratch_shapes=[pltpu.VMEM((t, width), BF), pltpu.VMEM((t, width), BF),
                        pltpu.VMEM((t, width), F32), pltpu.VMEM((t, width), F32),
                        pltpu.VMEM((n_heads, b, LANES), F32), pltpu.VMEM((n_heads, b, 1), F32),
                        pltpu.VMEM((n_heads, b, 1), F32), pltpu.SemaphoreType.DMA((2,))],
        compiler_params=_params("arbitrary", "arbitrary"))(qkv, a32, dmix, qkv)


def _adamw(w, g, m, v):
    n, c = w.shape
    tr = min(256, n)
    assert n % tr == 0

    def body(w_ref, g_ref, m_ref, v_ref, d_ref, nm_ref, nv_ref):
        g = g_ref[...]
        m = ADAM_B1 * m_ref[...] + (1.0 - ADAM_B1) * g
        v = ADAM_B2 * v_ref[...] + (1.0 - ADAM_B2) * jnp.square(g)
        m_hat = m / (1.0 - ADAM_B1 ** ADAM_STEP)
        v_hat = v / (1.0 - ADAM_B2 ** ADAM_STEP)
        d_ref[...] = -ADAM_LR * (m_hat / (jnp.sqrt(v_hat) + ADAM_EPS) + ADAM_WD * w_ref[...])
        nm_ref[...] = m
        nv_ref[...] = v

    blk = pl.BlockSpec((tr, c), lambda i: (i, 0))
    return _call(
        body, name="adamw", grid=(n // tr,), in_specs=[blk] * 4, out_specs=[blk] * 3,
        out_shape=[SDS((n, c), F32)] * 3, compiler_params=_params("parallel"))(w, g, m, v)


def _mesh_pos():
    return lax.axis_index("x"), lax.axis_index("y"), lax.axis_index("c")


def _other_chips(x, y):
    return [(1 - x, y), (x, 1 - y), (1 - x, 1 - y)]


HBM_SPEC = pl.BlockSpec(memory_space=pltpu.HBM)


def _all_gather(shard):
    n_l, r, c_w = shard.shape
    h = r // 2

    def body(s_ref, o_ref, send_sems, recv_sems, local_sem):
        x, y, c = _mesh_pos()
        sibling = (x, y, 1 - c)
        chips = _other_chips(x, y)

        def half(px, py, hc):
            return o_ref.at[:, 2 * px + py, pl.ds(hc * h, h), :]

        def copy(k, dst, to, src=None):
            return pltpu.make_async_remote_copy(
                src_ref=dst if src is None else src, dst_ref=dst, send_sem=send_sems.at[k], recv_sem=recv_sems.at[k],
                device_id=to, device_id_type=MESH)

        mine = pltpu.make_async_copy(s_ref, o_ref.at[:, 2 * x + y], local_sem)
        mine.start()
        first = [copy(j, half(x, y, c), (*chip, c), src=s_ref.at[:, pl.ds(c * h, h), :]) for j, chip in enumerate(chips)]
        for cp in first:
            cp.start()
        passed = [copy(3 + j, half(*chip, c), sibling) for j, chip in enumerate(chips)]
        for j, chip in enumerate(chips):
            copy(j, half(*chip, c), (x, y, c)).wait_recv()
            passed[j].start()
        for j, chip in enumerate(chips):
            copy(3 + j, half(*chip, 1 - c), (x, y, c)).wait_recv()
        for cp in first + passed:
            cp.wait_send()
        mine.wait()

    return _call(
        body, name="all_gather", in_specs=[HBM_SPEC], out_specs=HBM_SPEC,
        out_shape=SDS((n_l, N_CHIPS, r, c_w), shard.dtype),
        scratch_shapes=[pltpu.SemaphoreType.DMA((6,)), pltpu.SemaphoreType.DMA((6,)), pltpu.SemaphoreType.DMA],
        )(shard)


def _row_tile(h):
    for cand in (256, 176, 128, 64, 32, 16):
        if h % cand == 0:
            return cand
    raise ValueError(h)


def _reduce_scatter(g, mid_dtype):
    n_l, n_p, r, c_w = g.shape
    h = r // 2
    tr = _row_tile(h)
    nt = h // tr
    x, y, c = _mesh_pos()
    c_arr = jnp.reshape(c, (1,)).astype(jnp.int32)
    p_arr = jnp.reshape(2 * x + y, (1,)).astype(jnp.int32)

    def to_sibling_body(g_ref, a_ref, send_sem, recv_sem):
        x, y, c = _mesh_pos()
        cp = pltpu.make_async_remote_copy(
            src_ref=g_ref.at[:, :, pl.ds((1 - c) * h, h), :], dst_ref=a_ref, send_sem=send_sem, recv_sem=recv_sem,
            device_id=(x, y, 1 - c), device_id_type=MESH)
        cp.start()
        cp.wait()

    from_sibling = _call(
        to_sibling_body, name="rs_pair", in_specs=[HBM_SPEC], out_specs=HBM_SPEC,
        out_shape=SDS((n_l, n_p, h, c_w), g.dtype),
        scratch_shapes=[pltpu.SemaphoreType.DMA, pltpu.SemaphoreType.DMA],
        )(g)

    def pair_add_body(c_ref, g_ref, a_ref, o_ref):
        o_ref[...] = (g_ref[...].astype(F32) + a_ref[...].astype(F32)).astype(o_ref.dtype)

    blk = (None, None, tr, c_w)
    pair_sum = _call(
        pair_add_body, name="rs_pair_add",
        grid_spec=pltpu.PrefetchScalarGridSpec(
            num_scalar_prefetch=1, grid=(n_l, n_p, nt),
            in_specs=[pl.BlockSpec(blk, lambda l, p, t, c_ref: (l, p, c_ref[0] * nt + t, 0)),
                      pl.BlockSpec(blk, lambda l, p, t, c_ref: (l, p, t, 0))],
            out_specs=pl.BlockSpec(blk, lambda l, p, t, c_ref: (l, p, t, 0))),
        out_shape=SDS((n_l, n_p, h, c_w), mid_dtype),
        compiler_params=_params("parallel", "parallel", "parallel"))(c_arr, g, from_sibling)

    def to_chips_body(s_ref, b_ref, send_sems, recv_sems):
        x, y, c = _mesh_pos()
        cps = [pltpu.make_async_remote_copy(
            src_ref=s_ref.at[:, 2 * chip[0] + chip[1]], dst_ref=b_ref.at[j], send_sem=send_sems.at[j],
            recv_sem=recv_sems.at[j], device_id=(*chip, c), device_id_type=MESH)
            for j, chip in enumerate(_other_chips(x, y))]
        for cp in cps:
            cp.start()
        for cp in cps:
            cp.wait()

    from_chips = _call(
        to_chips_body, name="rs_chips", in_specs=[HBM_SPEC], out_specs=HBM_SPEC,
        out_shape=SDS((3, n_l, h, c_w), mid_dtype),
        scratch_shapes=[pltpu.SemaphoreType.DMA((3,)), pltpu.SemaphoreType.DMA((3,))],
        )(pair_sum)

    def chip_add_body(p_ref, c_ref, s_ref, b_ref, o_ref):
        acc = s_ref[...].astype(F32)
        for j in range(3):
            acc = acc + b_ref[j].astype(F32)
        o_ref[...] = acc

    half_sum = _call(
        chip_add_body, name="rs_chip_add",
        grid_spec=pltpu.PrefetchScalarGridSpec(
            num_scalar_prefetch=2, grid=(n_l, nt),
            in_specs=[pl.BlockSpec((None, None, tr, c_w), lambda l, t, p_ref, c_ref: (l, p_ref[0], t, 0)),
                      pl.BlockSpec((3, None, tr, c_w), lambda l, t, p_ref, c_ref: (0, l, t, 0))],
            out_specs=pl.BlockSpec((None, tr, c_w), lambda l, t, p_ref, c_ref: (l, c_ref[0] * nt + t, 0))),
        out_shape=SDS((n_l, r, c_w), F32),
        compiler_params=_params("parallel", "parallel"))(p_arr, c_arr, pair_sum, from_chips)

    def swap_body(i_ref, o_ref, send_sem, recv_sem):
        x, y, c = _mesh_pos()
        mine = o_ref.at[:, pl.ds(c * h, h), :]
        theirs = o_ref.at[:, pl.ds((1 - c) * h, h), :]
        pltpu.make_async_remote_copy(src_ref=mine, dst_ref=mine, send_sem=send_sem, recv_sem=recv_sem,
                                     device_id=(x, y, 1 - c), device_id_type=MESH).start()
        wait = pltpu.make_async_remote_copy(src_ref=mine, dst_ref=theirs, send_sem=send_sem, recv_sem=recv_sem,
                                            device_id=(x, y, 1 - c), device_id_type=MESH)
        wait.wait_send()
        wait.wait_recv()

    return _call(
        swap_body, name="rs_swap", in_specs=[HBM_SPEC], out_specs=HBM_SPEC, out_shape=SDS((n_l, r, c_w), F32),
        input_output_aliases={0: 0},
        scratch_shapes=[pltpu.SemaphoreType.DMA, pltpu.SemaphoreType.DMA],
        )(half_sum)


def _pack(arrays, row_multiple):
    flat = jnp.concatenate([a.reshape(-1).astype(F32) for a in arrays])
    unit = row_multiple * LANES
    padded = -(-flat.shape[0] // unit) * unit
    return jnp.pad(flat, (0, padded - flat.shape[0])).reshape(padded // LANES, LANES)


def _unpack(packed, shapes):
    flat = packed.reshape(-1)
    out, pos = [], 0
    for s in shapes:
        size = 1
        for dim in s:
            size *= dim
        out.append(flat[pos:pos + size].reshape(s))
        pos += size
    return out


BIG_COL = ("sb_w_in", "cv_w_pw1", "ffn_w_up")
BIG_ROW = ("hyb_w_out", "cv_w_pw2", "ffn_w_down")
SMALL_SHARDED = ("cv_b_pw1", "cv_w_dw", "cv_b_dw", "cv_ln_g", "cv_ln_b", "cv_b_pw2", "ffn_w_dw")
SMALL_REPLICATED = ("mix_norm_g", "sb_q_norm_g", "sb_k_norm_g", "sg_z_norm_g", "sg_w_spatial", "sg_b_spatial",
                    "ffn_norm_g", "ffn_b_dw")
WEIGHTS = ("mix_norm_g", "sb_w_in", "sb_q_norm_g", "sb_k_norm_g", "sg_z_norm_g", "sg_w_spatial", "sg_b_spatial",
           "hyb_w_out", "cv_w_pw1", "cv_b_pw1", "cv_w_dw", "cv_b_dw", "cv_ln_g", "cv_ln_b", "cv_w_pw2", "cv_b_pw2",
           "ffn_norm_g", "ffn_w_up", "ffn_w_dw", "ffn_b_dw", "ffn_w_down")


def _pad_rows(a, rows):
    return jnp.pad(a, ((0, rows - a.shape[0]), (0, 0)))


def _step(x, tgt, w, m, v):
    n_layers = w["mix_norm_g"].shape[0]
    xi, yi, ci = _mesh_pos()
    chip = 2 * xi + yi

    full = {}
    for name in BIG_COL:
        full[name] = _all_gather(w[name].astype(BF))
    for name in BIG_ROW:
        g4 = _all_gather(w[name].astype(BF))
        full[name] = g4.reshape(g4.shape[0], 1, g4.shape[1] * g4.shape[2], g4.shape[3])
    small_local = [w[name] for name in SMALL_SHARDED]
    gathered = _all_gather(_pack(small_local, 32)[None])[0]
    per_chip = [_unpack(gathered[p], [a.shape for a in small_local]) for p in range(N_CHIPS)]
    for k, name in enumerate(SMALL_SHARDED):
        full[name] = jnp.concatenate([per_chip[p][k] for p in range(N_CHIPS)], axis=-1)
    for name in SMALL_REPLICATED:
        full[name] = w[name]

    mean64, fold64 = _group_matrices()
    ffn_wdw = [_pad_rows(full["ffn_w_dw"][i], 8) for i in range(n_layers)]
    cv_wdw = [_pad_rows(full["cv_w_dw"][j], 32) for j in range(n_layers // 2)]
    row = lambda a: a.reshape(1, -1)

    saved = []
    cur = x
    for i in range(n_layers):
        j = i // 2
        rec = {"x_in": cur}
        h = _rms_fwd(cur, row(full["mix_norm_g"][i]))
        rec["h_mix"] = h
        if i % 2 == 0:
            proj = _mm_nn(h, full["sb_w_in"], j)
            qg = row(jnp.tile(full["sb_q_norm_g"][j], 512 // HEAD_DIM))
            kg = row(jnp.tile(full["sb_k_norm_g"][j], 512 // HEAD_DIM))
            zg = row(full["sg_z_norm_g"][j])
            bexp = jnp.repeat(full["sg_b_spatial"][j].T, HEAD_DIM, axis=1)
            qkv, gated = _mix_prep_fwd(proj, qg, kg, zg, full["sg_w_spatial"], j, bexp, mean64)
            att_bf, att_32 = _attn_fwd(qkv)
            mix = jnp.concatenate([att_bf, gated], axis=1)
            cur = _mm_nn(mix, full["hyb_w_out"], j, resid=cur)
            rec.update(proj=proj, qkv=qkv, att_32=att_32, mix=mix, qg=qg, kg=kg, zg=zg, bexp=bexp)
        else:
            p1 = _mm_nn(h, full["cv_w_pw1"], j, bias=row(full["cv_b_pw1"][j]), out_dtype=BF)
            ys, yc = _conf_mid_fwd(p1, cv_wdw[j], row(full["cv_b_dw"][j]), row(full["cv_ln_g"][j]),
                                   row(full["cv_ln_b"][j]))
            cur = _mm_nn(ys, full["cv_w_pw2"], j, bias=row(full["cv_b_pw2"][j]), resid=cur)
            rec.update(p1=p1, ys=ys, yc=yc)
        rec["x_mid"] = cur
        h = _rms_fwd(cur, row(full["ffn_norm_g"][i]))
        up = _mm_nn(h, full["ffn_w_up"], i, out_dtype=BF)
        act = _ffn_mid_fwd(up, ffn_wdw[i], row(full["ffn_b_dw"][i]))
        cur = _mm_nn(act, full["ffn_w_down"], i, resid=cur)
        rec.update(h_ffn=h, up=up, act=act)
        saved.append(rec)

    loss_vec, dy, dy_bf = _loss_grad(cur, tgt)
    loss = lax.psum(loss_vec[0, 0], ("x", "y", "c"))

    gbig = {name: None for name in BIG_COL + BIG_ROW}
    gsmall = {name: [None] * w[name].shape[0] for name in SMALL_SHARDED + SMALL_REPLICATED}
    n_of = {name: full[name].shape[0] for name in BIG_COL + BIG_ROW}
    for i in reversed(range(n_layers)):
        j = i // 2
        rec = saved[i]
        dact = _mm_nt(dy_bf, full["ffn_w_down"], i, out_dtype=BF)
        gbig["ffn_w_down"] = _mm_tn(rec["act"], dy_bf, 1, n_of["ffn_w_down"], i, gbig["ffn_w_down"])
        dup, dwdw, dbdw = _ffn_mid_bwd(rec["up"], dact, ffn_wdw[i], row(full["ffn_b_dw"][i]))
        gsmall["ffn_w_dw"][i] = dwdw[:FFN_K]
        gsmall["ffn_b_dw"][i] = dbdw[0]
        dh = _mm_nt(dup, full["ffn_w_up"], i)
        gbig["ffn_w_up"] = _mm_tn(rec["h_ffn"], dup, N_CHIPS, n_of["ffn_w_up"], i, gbig["ffn_w_up"])
        dy, dy_bf, dg = _rms_bwd(dh, rec["x_mid"], row(full["ffn_norm_g"][i]), dy)
        gsmall["ffn_norm_g"][i] = dg[0]
        if i % 2 == 0:
            dmix = _mm_nt(dy_bf, full["hyb_w_out"], j)
            gbig["hyb_w_out"] = _mm_tn(rec["mix"], dy_bf, 1, n_of["hyb_w_out"], j, gbig["hyb_w_out"])
            dq, dk, dv = _attn_bwd(rec["qkv"], rec["att_32"], dmix)
            dproj, dqg, dkg, dzg, dws, dbe = _mix_prep_bwd(
                rec["proj"], dq, dk, dv, dmix, rec["qg"], rec["kg"], rec["zg"], full["sg_w_spatial"], j, rec["bexp"],
                mean64, fold64)
            gsmall["sb_q_norm_g"][j] = dqg[0, :HEAD_DIM]
            gsmall["sb_k_norm_g"][j] = dkg[0, :HEAD_DIM]
            gsmall["sg_z_norm_g"][j] = dzg[0]
            gsmall["sg_w_spatial"][j] = dws
            gsmall["sg_b_spatial"][j] = dbe[:, ::HEAD_DIM].T
            dh = _mm_nt(dproj, full["sb_w_in"], j)
            gbig["sb_w_in"] = _mm_tn(rec["h_mix"], dproj, N_CHIPS, n_of["sb_w_in"], j, gbig["sb_w_in"])
        else:
            dys = _mm_nt(dy_bf, full["cv_w_pw2"], j, out_dtype=BF)
            gbig["cv_w_pw2"] = _mm_tn(rec["ys"], dy_bf, 1, n_of["cv_w_pw2"], j, gbig["cv_w_pw2"])
            dp1, dwdw, dbdw, dlg, dlb, db1, db2 = _conf_mid_bwd(
                rec["p1"], rec["yc"], dys, dy, cv_wdw[j], row(full["cv_ln_g"][j]), row(full["cv_ln_b"][j]))
            gsmall["cv_w_dw"][j] = dwdw[:CONV_K]
            gsmall["cv_b_dw"][j] = dbdw[0]
            gsmall["cv_ln_g"][j] = dlg[0]
            gsmall["cv_ln_b"][j] = dlb[0]
            gsmall["cv_b_pw1"][j] = db1[0]
            gsmall["cv_b_pw2"][j] = db2[0]
            dh = _mm_nt(dp1, full["cv_w_pw1"], j)
            gbig["cv_w_pw1"] = _mm_tn(rec["h_mix"], dp1, N_CHIPS, n_of["cv_w_pw1"], j, gbig["cv_w_pw1"])
        dy, dy_bf, dg = _rms_bwd(dh, rec["x_in"], row(full["mix_norm_g"][i]), dy)
        gsmall["mix_norm_g"][i] = dg[0]

    grads = {}
    for name in BIG_COL:
        grads[name] = _reduce_scatter(gbig[name], BF)
    for name in BIG_ROW:
        g4 = gbig[name]
        r = g4.shape[2] // N_CHIPS
        grads[name] = _reduce_scatter(g4.reshape(g4.shape[0], N_CHIPS, r, g4.shape[3]), BF)
    small_names = SMALL_REPLICATED + SMALL_SHARDED
    small_full = [jnp.stack(gsmall[name]) for name in small_names]
    packed = _pack(small_full, 32 * N_CHIPS)
    rows_q = packed.shape[0] // N_CHIPS
    summed = _reduce_scatter(packed.reshape(1, N_CHIPS, rows_q, LANES), F32)
    summed = _all_gather(summed).reshape(-1, LANES)
    for name, gsum in zip(small_names, _unpack(summed, [a.shape for a in small_full])):
        if name in SMALL_SHARDED:
            n_loc = w[name].shape[-1]
            split = gsum.reshape(gsum.shape[:-1] + (N_CHIPS, n_loc))
            gsum = lax.dynamic_index_in_dim(split, chip, axis=split.ndim - 2, keepdims=False)
        grads[name] = gsum

    delta, new_m, new_v = {}, {}, {}
    for name in BIG_COL + BIG_ROW:
        shp = w[name].shape
        two_d = lambda a: a.reshape(shp[0] * shp[1], shp[2])
        d, nm, nv = _adamw(two_d(w[name]), two_d(grads[name]), two_d(m[name]), two_d(v[name]))
        delta[name], new_m[name], new_v[name] = d.reshape(shp), nm.reshape(shp), nv.reshape(shp)
    shapes = [w[name].shape for name in small_names]
    d, nm, nv = _adamw(*(_pack([src[name] for name in small_names], 256) for src in (w, grads, m, v)))
    for name, a, b_, c_ in zip(small_names, _unpack(d, shapes), _unpack(nm, shapes), _unpack(nv, shapes)):
        delta[name], new_m[name], new_v[name] = a, b_, c_

    return (loss, dy, *[grads[n] for n in WEIGHTS], *[delta[n] for n in WEIGHTS],
            *[new_m[n] for n in WEIGHTS], *[new_v[n] for n in WEIGHTS])


def kernel(x, mix_norm_g, sb_w_in, sb_q_norm_g, sb_k_norm_g, sg_z_norm_g, sg_w_spatial, sg_b_spatial, hyb_w_out, cv_w_pw1, cv_b_pw1, cv_w_dw, cv_b_dw, cv_ln_g, cv_ln_b, cv_w_pw2, cv_b_pw2, ffn_norm_g, ffn_w_up, ffn_w_dw, ffn_b_dw, ffn_w_down, loss_target, m_mix_norm_g, m_sb_w_in, m_sb_q_norm_g, m_sb_k_norm_g, m_sg_z_norm_g, m_sg_w_spatial, m_sg_b_spatial, m_hyb_w_out, m_cv_w_pw1, m_cv_b_pw1, m_cv_w_dw, m_cv_b_dw, m_cv_ln_g, m_cv_ln_b, m_cv_w_pw2, m_cv_b_pw2, m_ffn_norm_g, m_ffn_w_up, m_ffn_w_dw, m_ffn_b_dw, m_ffn_w_down, v_mix_norm_g, v_sb_w_in, v_sb_q_norm_g, v_sb_k_norm_g, v_sg_z_norm_g, v_sg_w_spatial, v_sg_b_spatial, v_hyb_w_out, v_cv_w_pw1, v_cv_b_pw1, v_cv_w_dw, v_cv_b_dw, v_cv_ln_g, v_cv_ln_b, v_cv_w_pw2, v_cv_b_pw2, v_ffn_norm_g, v_ffn_w_up, v_ffn_w_dw, v_ffn_b_dw, v_ffn_w_down):
    given = dict(locals())
    w = {n: given[n] for n in WEIGHTS}
    m = {n: given["m_" + n] for n in WEIGHTS}
    v = {n: given["v_" + n] for n in WEIGHTS}
    out = _step(x[0], loss_target[0], w, m, v)
    return (out[0], out[1][None], *out[2:])
```

```python
import functools

import jax
import jax.numpy as jnp
from jax import lax
from jax.experimental import pallas as pl
from jax.experimental.pallas import tpu as pltpu

F32 = jnp.float32
BF = jnp.bfloat16
SDS = jax.ShapeDtypeStruct
HI = lax.Precision.HIGHEST
MESH = pl.DeviceIdType.MESH

NORM_EPS = 1e-6
HEAD_DIM = 64
ATT_BLOCK = 128
CHUNK = 128
CONV_K = 31
CONV_HALO = 32
FFN_K = 3
FFN_HALO = 16
LANES = 128
N_CHIPS = 4
VMEM_LIMIT_BYTES = 56 * 2**20

ADAM_LR = 0.001
ADAM_B1 = 0.9
ADAM_B2 = 0.999
ADAM_EPS = 1e-08
ADAM_WD = 0.01
ADAM_STEP = 10

NT_DIMS = (((1,), (1,)), ((), ()))
TN_DIMS = (((0,), (0,)), ((), ()))


def _call(body, **kw):
    return pl.pallas_call(body, **kw)


def _params(*sem):
    return pltpu.CompilerParams(dimension_semantics=sem, vmem_limit_bytes=VMEM_LIMIT_BYTES)


def _gelu(x):
    return 0.5 * x * (1.0 + lax.erf(x * 0.7071067811865476))


def _rms(x, g):
    y = x * lax.rsqrt(jnp.mean(x * x, axis=-1, keepdims=True) + NORM_EPS)
    return y * g


def _rms_fwd(x, g):
    t, d = x.shape
    tm = min(512, t)

    def body(x_ref, g_ref, o_ref):
        o_ref[...] = _rms(x_ref[...], g_ref[...]).astype(o_ref.dtype)

    return _call(
        body, name="rms_fwd", grid=(t // tm,),
        in_specs=[pl.BlockSpec((tm, d), lambda i: (i, 0)), pl.BlockSpec((1, d), lambda i: (0, 0))],
        out_specs=pl.BlockSpec((tm, d), lambda i: (i, 0)),
        out_shape=SDS((t, d), BF), compiler_params=_params("parallel"))(x, g)


def _rms_bwd(dh, x, g, dres):
    t, d = x.shape
    tm = min(512, t)

    def body(dh_ref, x_ref, g_ref, r_ref, dx_ref, dxb_ref, dg_ref):
        _, vjp = jax.vjp(_rms, x_ref[...], g_ref[...])
        dx, dg = vjp(dh_ref[...])
        dx = dx + r_ref[...]
        dx_ref[...] = dx
        dxb_ref[...] = dx.astype(BF)

        @pl.when(pl.program_id(0) == 0)
        def _():
            dg_ref[...] = jnp.zeros_like(dg_ref)

        dg_ref[...] += dg

    row = pl.BlockSpec((tm, d), lambda i: (i, 0))
    vec = pl.BlockSpec((1, d), lambda i: (0, 0))
    return _call(
        body, name="rms_bwd", grid=(t // tm,), in_specs=[row, row, vec, row], out_specs=[row, row, vec],
        out_shape=[SDS((t, d), F32), SDS((t, d), BF), SDS((1, d), F32)],
        compiler_params=_params("arbitrary"))(dh, x, g, dres)


def _mm_nn(a, w, l, bias=None, resid=None, out_dtype=F32, gather=None):
    m, k = a.shape
    _, p_n, kw, n = w.shape
    assert k == kw
    tm = min(1024, m)
    tn = n if k * n * 2 <= 4 * 2**20 else n // 2
    nj = n // tn
    in_specs = [pl.BlockSpec((tm, k), lambda i, p, j: (i, 0)),
                pl.BlockSpec((None, None, k, tn), lambda i, p, j: (l, p, 0, j))]
    args = [a, w]
    if bias is not None:
        in_specs.append(pl.BlockSpec((1, tn), lambda i, p, j: (0, p * nj + j)))
        args.append(bias)
    if resid is not None:
        in_specs.append(pl.BlockSpec((tm, tn), lambda i, p, j: (i, p * nj + j)))
        args.append(resid)

    def body(*refs):
        acc = jnp.dot(refs[0][...], refs[1][...], preferred_element_type=F32)
        nxt = 2
        if bias is not None:
            acc = acc + refs[nxt][...]
            nxt += 1
        if resid is not None:
            acc = refs[nxt][...] + acc
        refs[-1][...] = acc.astype(refs[-1].dtype)

    kw = dict(name="mm_nn", grid=(m // tm, p_n, nj), in_specs=in_specs,
              out_specs=pl.BlockSpec((tm, tn), lambda i, p, j: (i, p * nj + j)),
              out_shape=SDS((m, p_n * n), out_dtype))
    if gather:
        (out,), gathered = _call_gathering(body, gather, args, **kw)
        return out, gathered
    return _call(body, compiler_params=_params("parallel", "parallel", "parallel"), **kw)(*args)


def _mm_nt(dy, w, l, out_dtype=F32):
    m, n_all = dy.shape
    _, p_n, r, n = w.shape
    assert n_all == p_n * n
    tm = min(512, m)

    def body(dy_ref, w_ref, o_ref):
        acc = lax.dot_general(dy_ref[:, 0:n], w_ref[0], NT_DIMS, preferred_element_type=F32)
        for p in range(1, p_n):
            acc = acc + lax.dot_general(dy_ref[:, p * n:(p + 1) * n], w_ref[p], NT_DIMS, preferred_element_type=F32)
        o_ref[...] = acc.astype(o_ref.dtype)

    return _call(
        body, name="mm_nt", grid=(m // tm,),
        in_specs=[pl.BlockSpec((tm, n_all), lambda i: (i, 0)),
                  pl.BlockSpec((None, p_n, r, n), lambda i: (l, 0, 0, 0))],
        out_specs=pl.BlockSpec((tm, r), lambda i: (i, 0)),
        out_shape=SDS((m, r), out_dtype),
        compiler_params=_params("parallel"))(dy, w)


def _mm_tn(a, dy, p_n, n_layers, l, buf=None):
    m, k = a.shape
    n = dy.shape[1] // p_n
    tm = min(2048, m)
    tk = k if k <= 1024 else k // 2
    nm = m // tm

    def body(a_ref, dy_ref, *rest):
        o_ref, acc_ref = rest[-2], rest[-1]
        mi = pl.program_id(2)
        part = lax.dot_general(a_ref[...], dy_ref[...], TN_DIMS, preferred_element_type=F32)

        @pl.when(mi == 0)
        def _():
            acc_ref[...] = part

        @pl.when(mi > 0)
        def _():
            acc_ref[...] += part

        @pl.when(mi == nm - 1)
        def _():
            o_ref[...] = acc_ref[...].astype(o_ref.dtype)

    in_specs = [pl.BlockSpec((tm, tk), lambda p, kk, mi: (mi, kk)),
                pl.BlockSpec((tm, n), lambda p, kk, mi: (mi, p))]
    args = [a, dy]
    aliases = {}
    if buf is not None:
        in_specs.append(pl.BlockSpec(memory_space=pl.ANY))
        args.append(buf)
        aliases = {2: 0}
    return _call(
        body, name="mm_tn", grid=(p_n, k // tk, nm), in_specs=in_specs,
        out_specs=pl.BlockSpec((None, None, tk, n), lambda p, kk, mi: (l, p, kk, 0)),
        out_shape=SDS((n_layers, p_n, k, n), BF), scratch_shapes=[pltpu.VMEM((tk, n), F32)],
        input_output_aliases=aliases,
        compiler_params=_params("parallel", "parallel", "arbitrary"))(*args)


def _loss_grad(y, tgt):
    t, d = y.shape
    tm = min(512, t)

    def body(y_ref, t_ref, l_ref, d_ref, db_ref):
        err = y_ref[...] - t_ref[...]
        dy = err * (1.0 / d)
        d_ref[...] = dy
        db_ref[...] = dy.astype(BF)
        part = 0.5 * jnp.sum(jnp.sum(err * err, axis=1, keepdims=True) * (1.0 / d), axis=0, keepdims=True)

        @pl.when(pl.program_id(0) == 0)
        def _():
            l_ref[...] = jnp.zeros_like(l_ref)

        l_ref[...] += jnp.broadcast_to(part, l_ref.shape)

    row = pl.BlockSpec((tm, d), lambda i: (i, 0))
    return _call(
        body, name="loss_grad", grid=(t // tm,), in_specs=[row, row],
        out_specs=[pl.BlockSpec((1, LANES), lambda i: (0, 0)), row, row],
        out_shape=[SDS((1, LANES), F32), SDS((t, d), F32), SDS((t, d), BF)],
        compiler_params=_params("arbitrary"))(y, tgt)


def _prev_halo(tr, halo, col):
    return lambda i: (jnp.maximum(i * (tr // halo) - 1, 0), col)


def _next_halo(tr, halo, n_rows, col):
    return lambda i: (jnp.minimum((i + 1) * (tr // halo), n_rows // halo - 1), col)


def _ffn_mid_fwd(up, w_dw, b_dw, gather=None):
    t, f2 = up.shape
    f = f2 // 2
    tr = min(256, t)
    h = FFN_HALO

    def body(g_ref, gp_ref, v_ref, w_ref, b_ref, o_ref, xp_ref):
        i = pl.program_id(0)
        xp_ref[pl.ds(0, h), :] = jnp.where(i > 0, gp_ref[...].astype(F32), 0.0)
        xp_ref[pl.ds(h, tr), :] = g_ref[...].astype(F32)

        def strip(c, carry):
            col = pl.ds(pl.multiple_of(c * LANES, LANES), LANES)
            gc = jnp.broadcast_to(b_ref[:, col], (tr, LANES))
            for k in range(FFN_K):
                gc = gc + w_ref[pl.ds(k, 1), col] * xp_ref[pl.ds(h - (FFN_K - 1 - k), tr), col]
            o_ref[:, col] = (gc * jax.nn.sigmoid(gc) * v_ref[:, col].astype(F32)).astype(o_ref.dtype)
            return carry

        lax.fori_loop(0, f // LANES, strip, 0)

    kw = dict(name="ffn_mid_fwd", grid=(t // tr,),
              in_specs=[pl.BlockSpec((tr, f), lambda i: (i, 0)), pl.BlockSpec((h, f), _prev_halo(tr, h, 0)),
                        pl.BlockSpec((tr, f), lambda i: (i, 1)),
                        pl.BlockSpec((8, f), lambda i: (0, 0)), pl.BlockSpec((1, f), lambda i: (0, 0))],
              out_specs=pl.BlockSpec((tr, f), lambda i: (i, 0)), out_shape=SDS((t, f), BF),
              scratch_shapes=[pltpu.VMEM((h + tr, f), F32)])
    args = (up, up, up, w_dw, b_dw)
    if gather:
        (out,), gathered = _call_gathering(body, gather, args, **kw)
        return out, gathered
    return _call(body, compiler_params=_params("parallel"), **kw)(*args)


def _ffn_mid_bwd(up, da, w_dw, b_dw):
    t, f2 = up.shape
    f = f2 // 2
    tr = min(256, t)
    h = FFN_HALO
    n_tiles = t // tr

    def body(g_ref, gp_ref, gn_ref, v_ref, vn_ref, da_ref, dan_ref, w_ref, b_ref, dup_ref, dw_ref, db_ref, xg_ref, dgc_ref):
        i = pl.program_id(0)
        xg_ref[pl.ds(0, h), :] = jnp.where(i > 0, gp_ref[...].astype(F32), 0.0)
        xg_ref[pl.ds(h, tr), :] = g_ref[...].astype(F32)
        xg_ref[pl.ds(h + tr, h), :] = gn_ref[...].astype(F32)
        last = i == n_tiles - 1

        @pl.when(i == 0)
        def _():
            dw_ref[...] = jnp.zeros_like(dw_ref)
            db_ref[...] = jnp.zeros_like(db_ref)

        def dsilu_gate(rows0, n_rows, dav, vv, col):
            gc = jnp.broadcast_to(b_ref[:, col], (n_rows, LANES))
            for k in range(FFN_K):
                gc = gc + w_ref[pl.ds(k, 1), col] * xg_ref[pl.ds(h + rows0 - (FFN_K - 1 - k), n_rows), col]
            sg = jax.nn.sigmoid(gc)
            return gc * sg, dav * vv * (sg * (1.0 + gc * (1.0 - sg)))

        def strip(c, carry):
            col = pl.ds(pl.multiple_of(c * LANES, LANES), LANES)
            dav = da_ref[:, col].astype(F32)
            silu_gc, dgc = dsilu_gate(0, tr, dav, v_ref[:, col].astype(F32), col)
            dup_ref[:, pl.ds(pl.multiple_of(f + c * LANES, LANES), LANES)] = (dav * silu_gc).astype(dup_ref.dtype)
            dgc_ref[pl.ds(0, tr), col] = dgc
            _, dgc_next = dsilu_gate(tr, h, dan_ref[:, col].astype(F32), vn_ref[:, col].astype(F32), col)
            dgc_ref[pl.ds(tr, h), col] = jnp.where(last, 0.0, dgc_next)
            dg = jnp.zeros((tr, LANES), F32)
            for k in range(FFN_K):
                s = FFN_K - 1 - k
                dg = dg + w_ref[pl.ds(k, 1), col] * dgc_ref[pl.ds(s, tr), col]
                dw_ref[pl.ds(k, 1), col] += jnp.sum(xg_ref[pl.ds(h - s, tr), col] * dgc, axis=0, keepdims=True)
            dup_ref[:, col] = dg.astype(dup_ref.dtype)
            db_ref[:, col] += jnp.sum(dgc, axis=0, keepdims=True)
            return carry

        lax.fori_loop(0, f // LANES, strip, 0)

    tile = lambda col: pl.BlockSpec((tr, f), lambda i: (i, col))
    nxt = lambda col: pl.BlockSpec((h, f), _next_halo(tr, h, t, col))
    return _call(
        body, name="ffn_mid_bwd", grid=(n_tiles,),
        in_specs=[tile(0), pl.BlockSpec((h, f), _prev_halo(tr, h, 0)), nxt(0), tile(1), nxt(1), tile(0), nxt(0),
                  pl.BlockSpec((8, f), lambda i: (0, 0)), pl.BlockSpec((1, f), lambda i: (0, 0))],
        out_specs=[pl.BlockSpec((tr, f2), lambda i: (i, 0)), pl.BlockSpec((8, f), lambda i: (0, 0)),
                   pl.BlockSpec((1, f), lambda i: (0, 0))],
        out_shape=[SDS((t, f2), BF), SDS((8, f), F32), SDS((1, f), F32)],
        scratch_shapes=[pltpu.VMEM((h + tr + h, f), F32), pltpu.VMEM((tr + h, f), F32)],
        compiler_params=_params("arbitrary"))(up, up, up, up, up, da, da, w_dw, b_dw)


def _ln_silu(yc, g, b):
    mu = jnp.mean(yc, axis=-1, keepdims=True)
    xc = yc - mu
    y = xc * lax.rsqrt(jnp.mean(xc * xc, axis=-1, keepdims=True) + NORM_EPS)
    return jax.nn.silu(y * g + b)


SUBLANES = 8
CONV_PAD = 24


def _glu(a, g):
    return a.astype(F32) * jax.nn.sigmoid(g.astype(F32))


def _glu_strip(ygs_ref, first_tile, a_ref, ap_ref, g_ref, gp_ref, col, h, tr):
    ygs_ref[pl.ds(0, h), :] = jnp.where(first_tile, 0.0, _glu(ap_ref[:, col], gp_ref[:, col]))
    ygs_ref[pl.ds(h, tr), :] = _glu(a_ref[:, col], g_ref[:, col])


def _shift_past(sh_ref, ygs_ref, h, n):
    for r in range(1, SUBLANES):
        sh_ref[r, pl.ds(0, n + CONV_PAD), :] = ygs_ref[pl.ds(h - CONV_PAD - r, n + CONV_PAD), :]


def _past_rows(sh_ref, ygs_ref, h, n, s):
    a, r = divmod(s, SUBLANES)
    if r == 0:
        return ygs_ref[pl.ds(h - SUBLANES * a, n), :]
    return sh_ref[r, pl.ds(CONV_PAD - SUBLANES * a, n), :]


def _conf_mid_fwd(p1, w_dw, b_dw, ln_g, ln_b, gather=None):
    t, w2 = p1.shape
    w = w2 // 2
    tr = min(256, t)
    h = CONV_HALO
    rc = 32

    def body(a_ref, ap_ref, g_ref, gp_ref, w_ref, b_ref, lg_ref, lb_ref, o_ref, yc_ref, ygs_ref, sh_ref):
        first_tile = pl.program_id(0) == 0

        def strip(c, carry):
            col = pl.ds(pl.multiple_of(c * LANES, LANES), LANES)
            _glu_strip(ygs_ref, first_tile, a_ref, ap_ref, g_ref, gp_ref, col, h, tr)
            _shift_past(sh_ref, ygs_ref, h, tr)
            acc = jnp.broadcast_to(b_ref[:, col], (tr, LANES))
            for k in range(CONV_K):
                acc = acc + w_ref[pl.ds(k, 1), col] * _past_rows(sh_ref, ygs_ref, h, tr, CONV_K - 1 - k)
            yc_ref[:, col] = acc
            return carry

        lax.fori_loop(0, w // LANES, strip, 0)

        def rows(r, carry):
            rs = pl.ds(pl.multiple_of(r * rc, rc), rc)
            o_ref[rs, :] = _ln_silu(yc_ref[rs, :], lg_ref[...], lb_ref[...]).astype(o_ref.dtype)
            return carry

        lax.fori_loop(0, tr // rc, rows, 0)

    vec = pl.BlockSpec((1, w), lambda i: (0, 0))
    tile = pl.BlockSpec((tr, w), lambda i: (i, 0))
    kw = dict(name="conf_mid_fwd", grid=(t // tr,),
              in_specs=[tile, pl.BlockSpec((h, w), _prev_halo(tr, h, 0)),
                        pl.BlockSpec((tr, w), lambda i: (i, 1)), pl.BlockSpec((h, w), _prev_halo(tr, h, 1)),
                        pl.BlockSpec((32, w), lambda i: (0, 0)), vec, vec, vec],
              out_specs=[tile, tile], out_shape=[SDS((t, w), BF), SDS((t, w), F32)],
              scratch_shapes=[pltpu.VMEM((h + tr, LANES), F32), pltpu.VMEM((SUBLANES, tr + CONV_PAD, LANES), F32)])
    args = (p1, p1, p1, p1, w_dw, b_dw, ln_g, ln_b)
    if gather:
        return _call_gathering(body, gather, args, **kw)
    return _call(body, compiler_params=_params("parallel"), **kw)(*args)


def _conf_mid_bwd(p1, yc, dys, dy, w_dw, ln_g, ln_b):
    t, w2 = p1.shape
    w = w2 // 2
    tr = min(256, t)
    h = CONV_HALO
    rc = 32
    n_tiles = t // tr

    def body(a_ref, ap_ref, g_ref, gp_ref, yc_ref, ycn_ref, dys_ref, dysn_ref, dy_ref, w_ref, lg_ref, lb_ref,
             dp_ref, dw_ref, db_ref, dlg_ref, dlb_ref, db1_ref, db2_ref, dyc_ref, ygs_ref, sh_ref, shf_ref):
        i = pl.program_id(0)
        last = i == n_tiles - 1

        @pl.when(i == 0)
        def _():
            for ref in (dw_ref, db_ref, dlg_ref, dlb_ref, db1_ref, db2_ref):
                ref[...] = jnp.zeros_like(ref)

        def ln_rows(r, carry):
            rs = pl.ds(pl.multiple_of(r * rc, rc), rc)
            _, vjp = jax.vjp(_ln_silu, yc_ref[rs, :], lg_ref[...], lb_ref[...])
            dyc, dlg, dlb = vjp(dys_ref[rs, :].astype(F32))
            dyc_ref[rs, :] = dyc
            dlg_ref[...] += dlg
            dlb_ref[...] += dlb
            return carry

        lax.fori_loop(0, tr // rc, ln_rows, 0)
        _, vjp = jax.vjp(_ln_silu, ycn_ref[...], lg_ref[...], lb_ref[...])
        dyc_ref[pl.ds(tr, h), :] = jnp.where(last, 0.0, vjp(dysn_ref[...].astype(F32))[0])
        db2_ref[...] += jnp.sum(dy_ref[...], axis=0, keepdims=True)

        def back(c, carry):
            col = pl.ds(pl.multiple_of(c * LANES, LANES), LANES)
            gcol = pl.ds(pl.multiple_of(w + c * LANES, LANES), LANES)
            _glu_strip(ygs_ref, i == 0, a_ref, ap_ref, g_ref, gp_ref, col, h, tr)
            _shift_past(sh_ref, ygs_ref, h, tr)
            for r in range(1, SUBLANES):
                shf_ref[r, pl.ds(0, tr + CONV_PAD), :] = dyc_ref[pl.ds(r, tr + CONV_PAD), col]
            dyc = dyc_ref[pl.ds(0, tr), col]
            dyg = jnp.zeros((tr, LANES), F32)
            for k in range(CONV_K):
                s = CONV_K - 1 - k
                a, r = divmod(s, SUBLANES)
                if r == 0:
                    future = dyc_ref[pl.ds(SUBLANES * a, tr), col]
                else:
                    future = shf_ref[r, pl.ds(SUBLANES * a, tr), :]
                dyg = dyg + w_ref[pl.ds(k, 1), col] * future
                dw_ref[pl.ds(k, 1), col] += jnp.sum(_past_rows(sh_ref, ygs_ref, h, tr, s) * dyc, axis=0, keepdims=True)
            db_ref[:, col] += jnp.sum(dyc, axis=0, keepdims=True)
            sg = jax.nn.sigmoid(g_ref[:, col].astype(F32))
            da = dyg * sg
            dg = dyg * a_ref[:, col].astype(F32) * sg * (1.0 - sg)
            dp_ref[:, col] = da.astype(dp_ref.dtype)
            dp_ref[:, gcol] = dg.astype(dp_ref.dtype)
            db1_ref[:, col] += jnp.sum(da, axis=0, keepdims=True)
            db1_ref[:, gcol] += jnp.sum(dg, axis=0, keepdims=True)
            return carry

        lax.fori_loop(0, w // LANES, back, 0)

    tile = lambda col: pl.BlockSpec((tr, w), lambda i: (i, col))
    prv = lambda col: pl.BlockSpec((h, w), _prev_halo(tr, h, col))
    nxt = pl.BlockSpec((h, w), _next_halo(tr, h, t, 0))
    vec = pl.BlockSpec((1, w), lambda i: (0, 0))
    return _call(
        body, name="conf_mid_bwd", grid=(n_tiles,),
        in_specs=[tile(0), prv(0), tile(1), prv(1), tile(0), nxt, tile(0), nxt, tile(0),
                  pl.BlockSpec((32, w), lambda i: (0, 0)), vec, vec],
        out_specs=[pl.BlockSpec((tr, w2), lambda i: (i, 0)), pl.BlockSpec((32, w), lambda i: (0, 0)), vec, vec, vec,
                   pl.BlockSpec((1, w2), lambda i: (0, 0)), vec],
        out_shape=[SDS((t, w2), BF), SDS((32, w), F32), SDS((1, w), F32), SDS((1, w), F32), SDS((1, w), F32),
                   SDS((1, w2), F32), SDS((1, w), F32)],
        scratch_shapes=[pltpu.VMEM((tr + h, w), F32), pltpu.VMEM((h + tr, LANES), F32),
                        pltpu.VMEM((SUBLANES, tr + CONV_PAD, LANES), F32), pltpu.VMEM((SUBLANES, tr + CONV_PAD, LANES), F32)],
        compiler_params=_params("arbitrary"))(p1, p1, p1, p1, yc, yc, dys, dys, dy, w_dw, ln_g, ln_b)


def _group_matrices():
    i = lax.broadcasted_iota(jnp.int32, (512, 512), 0)
    j = lax.broadcasted_iota(jnp.int32, (512, 512), 1)
    mean64 = jnp.where(i // HEAD_DIM == j // HEAD_DIM, 1.0 / HEAD_DIM, 0.0).astype(F32)
    fold64 = jnp.where(i % HEAD_DIM == j % HEAD_DIM, 1.0, 0.0).astype(F32)
    return mean64, fold64


def _split_dot(x, mat):
    hi = x.astype(BF)
    lo = (x - hi.astype(F32)).astype(BF)
    mb = mat.astype(BF)
    return jnp.dot(hi, mb, preferred_element_type=F32) + jnp.dot(lo, mb, preferred_element_type=F32)


@jax.custom_vjp
def _group_sum(x, mat):
    return _split_dot(x, mat)


_group_sum.defvjp(lambda x, mat: (_split_dot(x, mat), mat), lambda mat, ct: (_split_dot(ct, mat), jnp.zeros_like(mat)))


def _bf_dot_plain(a, b):
    return jnp.dot(a.astype(BF), b.astype(BF), preferred_element_type=F32)


@jax.custom_vjp
def _bf_dot(a, b):
    return _bf_dot_plain(a, b)


def _bf_dot_bwd(res, ct):
    a, b = res
    cb = ct.astype(BF)
    return (lax.dot_general(cb, b.astype(BF), NT_DIMS, preferred_element_type=F32),
            lax.dot_general(a.astype(BF), cb, TN_DIMS, preferred_element_type=F32))


_bf_dot.defvjp(lambda a, b: (_bf_dot_plain(a, b), (a, b)), _bf_dot_bwd)


def _prep_tile(proj, qg, kg, zg, ws, bexp, mean64, differentiated=False):
    sw = 512
    q, k, v, u, z = (proj[:, n * sw:(n + 1) * sw] for n in range(5))
    group_sum, dot = (_group_sum, _bf_dot) if differentiated else (_split_dot, _bf_dot_plain)

    def group_norm(x):
        return x * lax.rsqrt(group_sum(x * x, mean64) + NORM_EPS)

    qn = group_norm(q) * qg
    kn = group_norm(k) * kg
    zn = group_norm(_gelu(z)) * zg
    row = lax.broadcasted_iota(jnp.int32, (CHUNK, CHUNK), 0)
    col = lax.broadcasted_iota(jnp.int32, (CHUNK, CHUNK), 1)
    first = lax.broadcasted_iota(jnp.int32, (1, LANES), 1) < HEAD_DIM
    parts = []
    for pr in range(sw // LANES):
        zp = zn[:, pr * LANES:(pr + 1) * LANES]
        s0 = dot(jnp.where(col <= row, ws[2 * pr], 0.0), zp)
        s1 = dot(jnp.where(col <= row, ws[2 * pr + 1], 0.0), zp)
        parts.append(jnp.where(first, s0, s1))
    s = jnp.concatenate(parts, axis=1) + bexp
    return qn, kn, v, _gelu(u) * s


def _mix_prep_fwd(proj, qg, kg, zg, w_s, l, bexp, mean64, gather=None):
    t = proj.shape[0]
    tr = CHUNK

    def body(p_ref, qg_ref, kg_ref, zg_ref, ws_ref, be_ref, m_ref, qkv_ref, go_ref):
        qn, kn, v, go = _prep_tile(p_ref[...], qg_ref[...], kg_ref[...], zg_ref[...], ws_ref[...], be_ref[...], m_ref[...])
        qkv_ref[:, 0:512] = qn.astype(BF)
        qkv_ref[:, 512:1024] = kn.astype(BF)
        qkv_ref[:, 1024:1536] = v.astype(BF)
        go_ref[...] = go.astype(BF)

    vec = pl.BlockSpec((1, 512), lambda i: (0, 0))
    kw = dict(name="mix_prep_fwd", grid=(t // tr,),
              in_specs=[pl.BlockSpec((tr, 2560), lambda i: (i, 0)), vec, vec, vec,
                        pl.BlockSpec((None, 8, CHUNK, CHUNK), lambda i: (l, 0, 0, 0)),
                        pl.BlockSpec((CHUNK, 512), lambda i: (0, 0)), pl.BlockSpec((512, 512), lambda i: (0, 0))],
              out_specs=[pl.BlockSpec((tr, 1536), lambda i: (i, 0)), pl.BlockSpec((tr, 512), lambda i: (i, 0))],
              out_shape=[SDS((t, 1536), BF), SDS((t, 512), BF)])
    args = (proj, qg, kg, zg, w_s, bexp, mean64)
    if gather:
        return _call_gathering(body, gather, args, **kw)
    return _call(body, compiler_params=_params("parallel"), **kw)(*args)


def _mix_prep_bwd(proj, dq, dk, dv, dmix, qg, kg, zg, w_s, l, bexp, mean64, fold64):
    t = proj.shape[0]
    tr = CHUNK
    n_tiles = t // tr

    def body(p_ref, dq_ref, dk_ref, dv_ref, dgo_ref, qg_ref, kg_ref, zg_ref, ws_ref, be_ref, m_ref, f_ref,
             dp_ref, dqg_ref, dkg_ref, dzg_ref, dws_ref, dbe_ref):
        i = pl.program_id(0)

        @pl.when(i == 0)
        def _():
            for ref in (dqg_ref, dkg_ref, dzg_ref, dws_ref, dbe_ref):
                ref[...] = jnp.zeros_like(ref)

        fn = functools.partial(_prep_tile, mean64=m_ref[...], differentiated=True)
        _, vjp = jax.vjp(fn, p_ref[...], qg_ref[...], kg_ref[...], zg_ref[...], ws_ref[...], be_ref[...])
        dp, dqg, dkg, dzg, dws, dbe = vjp((dq_ref[...], dk_ref[...], dv_ref[...], dgo_ref[...]))
        dp_ref[...] = dp.astype(BF)
        dqg_ref[pl.ds(0, 1), :] += dqg
        dkg_ref[pl.ds(0, 1), :] += dkg
        dzg_ref[pl.ds(0, 1), :] += dzg
        dws_ref[...] += dws
        dbe_ref[...] += dbe

        @pl.when(i == n_tiles - 1)
        def _():
            dqg_ref[...] = jnp.dot(dqg_ref[...], f_ref[...], precision=HI, preferred_element_type=F32)
            dkg_ref[...] = jnp.dot(dkg_ref[...], f_ref[...], precision=HI, preferred_element_type=F32)
            dbe_ref[...] = jnp.dot(dbe_ref[...], m_ref[...] * float(HEAD_DIM), precision=HI, preferred_element_type=F32)

    vec = pl.BlockSpec((1, 512), lambda i: (0, 0))
    acc = pl.BlockSpec((8, 512), lambda i: (0, 0))
    sq = pl.BlockSpec((512, 512), lambda i: (0, 0))
    row = pl.BlockSpec((tr, 512), lambda i: (i, 0))
    return _call(
        body, name="mix_prep_bwd", grid=(n_tiles,),
        in_specs=[pl.BlockSpec((tr, 2560), lambda i: (i, 0)), row, row, row, pl.BlockSpec((tr, 512), lambda i: (i, 1)),
                  vec, vec, vec, pl.BlockSpec((None, 8, CHUNK, CHUNK), lambda i: (l, 0, 0, 0)),
                  pl.BlockSpec((CHUNK, 512), lambda i: (0, 0)), sq, sq],
        out_specs=[pl.BlockSpec((tr, 2560), lambda i: (i, 0)), acc, acc, acc,
                   pl.BlockSpec((8, CHUNK, CHUNK), lambda i: (0, 0, 0)), pl.BlockSpec((CHUNK, 512), lambda i: (0, 0))],
        out_shape=[SDS((t, 2560), BF), SDS((8, 512), F32), SDS((8, 512), F32), SDS((8, 512), F32),
                   SDS((8, CHUNK, CHUNK), F32), SDS((CHUNK, 512), F32)],
        compiler_params=_params("arbitrary"))(proj, dq, dk, dv, dmix, qg, kg, zg, w_s, bexp, mean64, fold64)


def _sb_logs(qh, kb, valid):
    z = lax.dot_general(qh, kb, NT_DIMS, preferred_element_type=F32) * (HEAD_DIM ** -0.5)
    soft = jnp.log1p(jnp.exp(-jnp.abs(z)))
    lk_raw = -(jnp.maximum(z, 0.0) + soft)
    ls = -(jnp.maximum(-z, 0.0) + soft)
    return lk_raw, ls, jnp.where(valid, lk_raw, 0.0)


def _sb_weights(ls, run, tail, valid):
    return jnp.where(valid, jnp.exp(ls + run + tail), 0.0)


def _att_masks(b):
    row = lax.broadcasted_iota(jnp.int32, (b, b), 0)
    col = lax.broadcasted_iota(jnp.int32, (b, b), 1)
    first = lax.broadcasted_iota(jnp.int32, (1, LANES), 1) < HEAD_DIM
    return row, col, first


N_PAIRS = 4


def _load_kv(qkv_hbm, k_scr, v_scr, sems, group, width):
    ck = pltpu.make_async_copy(qkv_hbm.at[:, pl.ds(pl.multiple_of(512 + group * width, LANES), width)], k_scr, sems.at[0])
    cv = pltpu.make_async_copy(qkv_hbm.at[:, pl.ds(pl.multiple_of(1024 + group * width, LANES), width)], v_scr, sems.at[1])
    ck.start()
    cv.start()
    ck.wait()
    cv.wait()


def _split_heads(ref, pair, first):
    x = ref[:, pair * LANES:(pair + 1) * LANES]
    zero = jnp.zeros_like(x)
    return jnp.where(first, x, zero), jnp.where(first, zero, x)


def _any_weight_left(run_ref, n_heads):
    top = run_ref[0]
    for hh in range(1, n_heads):
        top = jnp.maximum(top, run_ref[hh])
    return jnp.max(jnp.exp(top)) > 0.0


def _attn_fwd(qkv, pairs_per_step=4, gather=None):
    t = qkv.shape[0]
    b = ATT_BLOCK
    nq = t // b
    width = pairs_per_step * LANES
    n_heads = 2 * pairs_per_step

    def body(q_ref, qkv_hbm, ob_ref, o32_ref, k_scr, v_scr, acc_ref, run_ref, sems):
        group, qi = pl.program_id(0), pl.program_id(1)

        @pl.when(qi == 0)
        def _():
            _load_kv(qkv_hbm, k_scr, v_scr, sems, group, width)

        row, col, first = _att_masks(b)
        qh = [x for pr in range(pairs_per_step) for x in _split_heads(q_ref, pr, first)]
        upper = jnp.where(row > col, 1.0, 0.0).astype(BF)
        acc_ref[...] = jnp.zeros_like(acc_ref)
        run_ref[...] = jnp.zeros_like(run_ref)
        heads = range(n_heads)

        def step(carry):
            j, _ = carry
            rows = pl.ds(pl.multiple_of(j * b, b), b)
            valid = jnp.logical_or(j != qi, col < row)
            lanes = [pl.ds((hh // 2) * LANES, LANES) for hh in heads]
            logs = [_sb_logs(qh[hh], k_scr[rows, lanes[hh]], valid) for hh in heads]
            tails = [_split_dot(logs[hh][2], upper) for hh in heads]
            for hh in heads:
                wgt = _sb_weights(logs[hh][1], run_ref[hh], tails[hh], valid)
                acc_ref[hh] += jnp.dot(wgt.astype(BF), v_scr[rows, lanes[hh]], preferred_element_type=F32)
            for hh in heads:
                run_ref[hh] += jnp.sum(logs[hh][2], axis=1, keepdims=True)
            return j - 1, _any_weight_left(run_ref, n_heads)

        lax.while_loop(lambda c: jnp.logical_and(c[0] >= 0, c[1]), step, (qi, jnp.bool_(True)))
        for pr in range(pairs_per_step):
            out = jnp.where(first, acc_ref[2 * pr], acc_ref[2 * pr + 1])
            ob_ref[:, pr * LANES:(pr + 1) * LANES] = out.astype(BF)
            o32_ref[:, pr * LANES:(pr + 1) * LANES] = out

    blk = pl.BlockSpec((b, width), lambda g, qi: (qi, g))
    kw = dict(name="attn_fwd", grid=(N_PAIRS // pairs_per_step, nq),
              in_specs=[blk, pl.BlockSpec(memory_space=pl.ANY)], out_specs=[blk, blk],
              out_shape=[SDS((t, 512), BF), SDS((t, 512), F32)],
              scratch_shapes=[pltpu.VMEM((t, width), BF), pltpu.VMEM((t, width), BF),
                              pltpu.VMEM((n_heads, b, LANES), F32), pltpu.VMEM((n_heads, b, 1), F32),
                              pltpu.SemaphoreType.DMA((2,))])
    if gather:
        return _call_gathering(body, gather, (qkv, qkv), **kw)
    return _call(body, compiler_params=_params("arbitrary", "arbitrary"), **kw)(qkv, qkv)


def _attn_bwd(qkv, a32, dmix, pairs_per_step=2):
    t = qkv.shape[0]
    b = ATT_BLOCK
    nq = t // b
    width = pairs_per_step * LANES
    n_heads = 2 * pairs_per_step

    def body(q_ref, a_ref, da_ref, qkv_hbm, dq_ref, dk_hbm, dv_hbm,
             k_scr, v_scr, dk_scr, dv_scr, dqa_ref, run_ref, rung_ref, sems):
        group, qi = pl.program_id(0), pl.program_id(1)

        @pl.when(qi == 0)
        def _():
            _load_kv(qkv_hbm, k_scr, v_scr, sems, group, width)
            dk_scr[...] = jnp.zeros_like(dk_scr)
            dv_scr[...] = jnp.zeros_like(dv_scr)

        row, col, first = _att_masks(b)
        qh, dah, dtot = [], [], []
        for pr in range(pairs_per_step):
            qh += _split_heads(q_ref, pr, first)
            da = da_ref[:, pr * LANES:(pr + 1) * LANES]
            prod = da * a_ref[:, pr * LANES:(pr + 1) * LANES]
            dtot += [jnp.sum(jnp.where(first, prod, 0.0), axis=1, keepdims=True),
                     jnp.sum(jnp.where(first, 0.0, prod), axis=1, keepdims=True)]
            dah += [jnp.where(first, da, 0.0).astype(BF), jnp.where(first, 0.0, da).astype(BF)]
        upper = jnp.where(row > col, 1.0, 0.0).astype(BF)
        lower_incl = jnp.where(row >= col, 1.0, 0.0).astype(BF)
        dqa_ref[...] = jnp.zeros_like(dqa_ref)
        run_ref[...] = jnp.zeros_like(run_ref)
        rung_ref[...] = jnp.zeros_like(rung_ref)
        heads = range(n_heads)

        def step(carry):
            j, _ = carry
            rows = pl.ds(pl.multiple_of(j * b, b), b)
            valid = jnp.logical_or(j != qi, col < row)
            lanes = [pl.ds((hh // 2) * LANES, LANES) for hh in heads]
            logs = [_sb_logs(qh[hh], k_scr[rows, lanes[hh]], valid) for hh in heads]
            dps = [lax.dot_general(dah[hh], v_scr[rows, lanes[hh]], NT_DIMS, preferred_element_type=F32) for hh in heads]
            tails = [_split_dot(logs[hh][2], upper) for hh in heads]
            wgts = [_sb_weights(logs[hh][1], run_ref[hh], tails[hh], valid) for hh in heads]
            gs = [wgts[hh] * dps[hh] for hh in heads]
            g_froms = [_split_dot(gs[hh], lower_incl) for hh in heads]
            for hh in heads:
                lk_raw, ls, _ = logs[hh]
                dlk = jnp.where(valid, dtot[hh] - rung_ref[hh] - g_froms[hh], 0.0)
                dz = ((gs[hh] * jnp.exp(lk_raw) - dlk * jnp.exp(ls)) * (HEAD_DIM ** -0.5)).astype(BF)
                dqa_ref[hh] += jnp.dot(dz, k_scr[rows, lanes[hh]], preferred_element_type=F32)
                dk_scr[rows, lanes[hh]] += lax.dot_general(dz, qh[hh], TN_DIMS, preferred_element_type=F32)
                dv_scr[rows, lanes[hh]] += lax.dot_general(wgts[hh].astype(BF), dah[hh], TN_DIMS, preferred_element_type=F32)
            for hh in heads:
                rung_ref[hh] += jnp.sum(gs[hh], axis=1, keepdims=True)
                run_ref[hh] += jnp.sum(logs[hh][2], axis=1, keepdims=True)
            return j - 1, _any_weight_left(run_ref, n_heads)

        lax.while_loop(lambda c: jnp.logical_and(c[0] >= 0, c[1]), step, (qi, jnp.bool_(True)))
        for pr in range(pairs_per_step):
            dq_ref[:, pr * LANES:(pr + 1) * LANES] = jnp.where(first, dqa_ref[2 * pr], dqa_ref[2 * pr + 1])

        @pl.when(qi == nq - 1)
        def _():
            cols = pl.ds(pl.multiple_of(group * width, LANES), width)
            ck = pltpu.make_async_copy(dk_scr, dk_hbm.at[:, cols], sems.at[0])
            cv = pltpu.make_async_copy(dv_scr, dv_hbm.at[:, cols], sems.at[1])
            ck.start()
            cv.start()
            ck.wait()
            cv.wait()

    blk = pl.BlockSpec((b, width), lambda g, qi: (qi, g))
    anywhere = pl.BlockSpec(memory_space=pl.ANY)
    return _call(
        body, name="attn_bwd", grid=(N_PAIRS // pairs_per_step, nq),
        in_specs=[blk, blk, blk, anywhere], out_specs=[blk, anywhere, anywhere],
        out_shape=[SDS((t, 512), F32), SDS((t, 512), F32), SDS((t, 512), F32)],
        scratch_shapes=[pltpu.VMEM((t, width), BF), pltpu.VMEM((t, width), BF),
                        pltpu.VMEM((t, width), F32), pltpu.VMEM((t, width), F32),
                        pltpu.VMEM((n_heads, b, LANES), F32), pltpu.VMEM((n_heads, b, 1), F32),
                        pltpu.VMEM((n_heads, b, 1), F32), pltpu.SemaphoreType.DMA((2,))],
        compiler_params=_params("arbitrary", "arbitrary"))(qkv, a32, dmix, qkv)


def _adamw(w, g, m, v):
    n, c = w.shape
    tr = min(256, n)
    assert n % tr == 0

    def body(w_ref, g_ref, m_ref, v_ref, d_ref, nm_ref, nv_ref):
        g = g_ref[...]
        m = ADAM_B1 * m_ref[...] + (1.0 - ADAM_B1) * g
        v = ADAM_B2 * v_ref[...] + (1.0 - ADAM_B2) * jnp.square(g)
        m_hat = m / (1.0 - ADAM_B1 ** ADAM_STEP)
        v_hat = v / (1.0 - ADAM_B2 ** ADAM_STEP)
        d_ref[...] = -ADAM_LR * (m_hat / (jnp.sqrt(v_hat) + ADAM_EPS) + ADAM_WD * w_ref[...])
        nm_ref[...] = m
        nv_ref[...] = v

    blk = pl.BlockSpec((tr, c), lambda i: (i, 0))
    return _call(
        body, name="adamw", grid=(n // tr,), in_specs=[blk] * 4, out_specs=[blk] * 3,
        out_shape=[SDS((n, c), F32)] * 3, compiler_params=_params("parallel"))(w, g, m, v)


def _mesh_pos():
    return lax.axis_index("x"), lax.axis_index("y"), lax.axis_index("c")


def _other_chips(x, y):
    return [(1 - x, y), (x, 1 - y), (1 - x, 1 - y)]


HBM_SPEC = pl.BlockSpec(memory_space=pltpu.HBM)


GATHER_COPIES = 6


def _gather_steps(s_ref, o_ref, send_sems, recv_sems, local_sems, slot):
    h = s_ref.shape[1] // 2
    x, y, c = _mesh_pos()
    sibling = (x, y, 1 - c)
    chips = _other_chips(x, y)
    base = GATHER_COPIES * slot

    def half(px, py, hc):
        return o_ref.at[:, 2 * px + py, pl.ds(hc * h, h), :]

    def copy(k, dst, to, src=None):
        return pltpu.make_async_remote_copy(
            src_ref=dst if src is None else src, dst_ref=dst, send_sem=send_sems.at[base + k],
            recv_sem=recv_sems.at[base + k], device_id=to, device_id_type=MESH)

    mine = pltpu.make_async_copy(s_ref, o_ref.at[:, 2 * x + y], local_sems.at[slot])
    first = [copy(j, half(x, y, c), (*chip, c), src=s_ref.at[:, pl.ds(c * h, h), :]) for j, chip in enumerate(chips)]
    passed = [copy(3 + j, half(*chip, c), sibling) for j, chip in enumerate(chips)]

    def start():
        mine.start()
        for cp in first:
            cp.start()

    def finish():
        for j, chip in enumerate(chips):
            copy(j, half(*chip, c), (x, y, c)).wait_recv()
            passed[j].start()
        for j, chip in enumerate(chips):
            copy(3 + j, half(*chip, 1 - c), (x, y, c)).wait_recv()
        for cp in first + passed:
            cp.wait_send()
        mine.wait()

    return start, finish


def _gather_scratch(n):
    return [pltpu.SemaphoreType.DMA((GATHER_COPIES * n,)), pltpu.SemaphoreType.DMA((GATHER_COPIES * n,)),
            pltpu.SemaphoreType.DMA((n,))]


def _gathered_shape(shard):
    n_l, r, c_w = shard.shape
    return SDS((n_l, N_CHIPS, r, c_w), shard.dtype)


def _all_gather(shard):
    def body(s_ref, o_ref, send_sems, recv_sems, local_sems):
        start, finish = _gather_steps(s_ref, o_ref, send_sems, recv_sems, local_sems, 0)
        start()
        finish()

    return _call(body, name="all_gather", in_specs=[HBM_SPEC], out_specs=HBM_SPEC, out_shape=_gathered_shape(shard),
                 scratch_shapes=_gather_scratch(1))(shard)


def _call_gathering(body, shards, args, *, name, grid, in_specs, out_specs, out_shape, scratch_shapes=()):
    out_specs = list(out_specs) if isinstance(out_specs, (list, tuple)) else [out_specs]
    out_shape = list(out_shape) if isinstance(out_shape, (list, tuple)) else [out_shape]
    n_in, n_out, n_sh, n_scr = len(in_specs), len(out_specs), len(shards), len(scratch_shapes)

    def hosting_body(*refs):
        ins, s_refs = refs[:n_in], refs[n_in:n_in + n_sh]
        outs = refs[n_in + n_sh:n_in + n_sh + n_out]
        o_refs = refs[n_in + n_sh + n_out:n_in + 2 * n_sh + n_out]
        scratch = refs[n_in + 2 * n_sh + n_out:n_in + 2 * n_sh + n_out + n_scr]
        send_sems, recv_sems, local_sems = refs[-3:]
        steps = [_gather_steps(s_refs[k], o_refs[k], send_sems, recv_sems, local_sems, k) for k in range(n_sh)]
        is_first = functools.reduce(jnp.logical_and, [pl.program_id(a) == 0 for a in range(len(grid))])
        is_last = functools.reduce(jnp.logical_and, [pl.program_id(a) == grid[a] - 1 for a in range(len(grid))])

        @pl.when(is_first)
        def _():
            for start, _ in steps:
                start()

        body(*ins, *outs, *scratch)

        @pl.when(is_last)
        def _():
            for _, finish in steps:
                finish()

    res = _call(
        hosting_body, name=name + "_gathering", grid=grid, in_specs=list(in_specs) + [HBM_SPEC] * n_sh,
        out_specs=out_specs + [HBM_SPEC] * n_sh, out_shape=out_shape + [_gathered_shape(s) for s in shards],
        scratch_shapes=list(scratch_shapes) + _gather_scratch(n_sh),
        compiler_params=_params(*(["arbitrary"] * len(grid))))(*args, *shards)
    return res[:n_out], res[n_out:]


def _row_tile(h):
    for cand in (256, 176, 128, 64, 32, 16):
        if h % cand == 0:
            return cand
    raise ValueError(h)


def _reduce_scatter(g, mid_dtype):
    n_l, n_p, r, c_w = g.shape
    h = r // 2
    tr = _row_tile(h)
    nt = h // tr
    x, y, c = _mesh_pos()
    c_arr = jnp.reshape(c, (1,)).astype(jnp.int32)
    p_arr = jnp.reshape(2 * x + y, (1,)).astype(jnp.int32)

    def to_sibling_body(g_ref, a_ref, send_sem, recv_sem):
        x, y, c = _mesh_pos()
        cp = pltpu.make_async_remote_copy(
            src_ref=g_ref.at[:, :, pl.ds((1 - c) * h, h), :], dst_ref=a_ref, send_sem=send_sem, recv_sem=recv_sem,
            device_id=(x, y, 1 - c), device_id_type=MESH)
        cp.start()
        cp.wait()

    from_sibling = _call(
        to_sibling_body, name="rs_pair", in_specs=[HBM_SPEC], out_specs=HBM_SPEC,
        out_shape=SDS((n_l, n_p, h, c_w), g.dtype),
        scratch_shapes=[pltpu.SemaphoreType.DMA, pltpu.SemaphoreType.DMA],
        )(g)

    def pair_add_body(c_ref, g_ref, a_ref, o_ref):
        o_ref[...] = (g_ref[...].astype(F32) + a_ref[...].astype(F32)).astype(o_ref.dtype)

    blk = (None, None, tr, c_w)
    pair_sum = _call(
        pair_add_body, name="rs_pair_add",
        grid_spec=pltpu.PrefetchScalarGridSpec(
            num_scalar_prefetch=1, grid=(n_l, n_p, nt),
            in_specs=[pl.BlockSpec(blk, lambda l, p, t, c_ref: (l, p, c_ref[0] * nt + t, 0)),
                      pl.BlockSpec(blk, lambda l, p, t, c_ref: (l, p, t, 0))],
            out_specs=pl.BlockSpec(blk, lambda l, p, t, c_ref: (l, p, t, 0))),
        out_shape=SDS((n_l, n_p, h, c_w), mid_dtype),
        compiler_params=_params("parallel", "parallel", "parallel"))(c_arr, g, from_sibling)

    def to_chips_body(s_ref, b_ref, send_sems, recv_sems):
        x, y, c = _mesh_pos()
        cps = [pltpu.make_async_remote_copy(
            src_ref=s_ref.at[:, 2 * chip[0] + chip[1]], dst_ref=b_ref.at[j], send_sem=send_sems.at[j],
            recv_sem=recv_sems.at[j], device_id=(*chip, c), device_id_type=MESH)
            for j, chip in enumerate(_other_chips(x, y))]
        for cp in cps:
            cp.start()
        for cp in cps:
            cp.wait()

    from_chips = _call(
        to_chips_body, name="rs_chips", in_specs=[HBM_SPEC], out_specs=HBM_SPEC,
        out_shape=SDS((3, n_l, h, c_w), mid_dtype),
        scratch_shapes=[pltpu.SemaphoreType.DMA((3,)), pltpu.SemaphoreType.DMA((3,))],
        )(pair_sum)

    def chip_add_body(p_ref, c_ref, s_ref, b_ref, o_ref):
        acc = s_ref[...].astype(F32)
        for j in range(3):
            acc = acc + b_ref[j].astype(F32)
        o_ref[...] = acc

    half_sum = _call(
        chip_add_body, name="rs_chip_add",
        grid_spec=pltpu.PrefetchScalarGridSpec(
            num_scalar_prefetch=2, grid=(n_l, nt),
            in_specs=[pl.BlockSpec((None, None, tr, c_w), lambda l, t, p_ref, c_ref: (l, p_ref[0], t, 0)),
                      pl.BlockSpec((3, None, tr, c_w), lambda l, t, p_ref, c_ref: (0, l, t, 0))],
            out_specs=pl.BlockSpec((None, tr, c_w), lambda l, t, p_ref, c_ref: (l, c_ref[0] * nt + t, 0))),
        out_shape=SDS((n_l, r, c_w), F32),
        compiler_params=_params("parallel", "parallel"))(p_arr, c_arr, pair_sum, from_chips)

    def swap_body(i_ref, o_ref, send_sem, recv_sem):
        x, y, c = _mesh_pos()
        mine = o_ref.at[:, pl.ds(c * h, h), :]
        theirs = o_ref.at[:, pl.ds((1 - c) * h, h), :]
        pltpu.make_async_remote_copy(src_ref=mine, dst_ref=mine, send_sem=send_sem, recv_sem=recv_sem,
                                     device_id=(x, y, 1 - c), device_id_type=MESH).start()
        wait = pltpu.make_async_remote_copy(src_ref=mine, dst_ref=theirs, send_sem=send_sem, recv_sem=recv_sem,
                                            device_id=(x, y, 1 - c), device_id_type=MESH)
        wait.wait_send()
        wait.wait_recv()

    return _call(
        swap_body, name="rs_swap", in_specs=[HBM_SPEC], out_specs=HBM_SPEC, out_shape=SDS((n_l, r, c_w), F32),
        input_output_aliases={0: 0},
        scratch_shapes=[pltpu.SemaphoreType.DMA, pltpu.SemaphoreType.DMA],
        )(half_sum)


def _pack(arrays, row_multiple):
    flat = jnp.concatenate([a.reshape(-1).astype(F32) for a in arrays])
    unit = row_multiple * LANES
    padded = -(-flat.shape[0] // unit) * unit
    return jnp.pad(flat, (0, padded - flat.shape[0])).reshape(padded // LANES, LANES)


def _unpack(packed, shapes):
    flat = packed.reshape(-1)
    out, pos = [], 0
    for s in shapes:
        size = 1
        for dim in s:
            size *= dim
        out.append(flat[pos:pos + size].reshape(s))
        pos += size
    return out


BIG_COL = ("sb_w_in", "cv_w_pw1", "ffn_w_up")
BIG_ROW = ("hyb_w_out", "cv_w_pw2", "ffn_w_down")
SMALL_SHARDED = ("cv_b_pw1", "cv_w_dw", "cv_b_dw", "cv_ln_g", "cv_ln_b", "cv_b_pw2", "ffn_w_dw")
SMALL_REPLICATED = ("mix_norm_g", "sb_q_norm_g", "sb_k_norm_g", "sg_z_norm_g", "sg_w_spatial", "sg_b_spatial",
                    "ffn_norm_g", "ffn_b_dw")
WEIGHTS = ("mix_norm_g", "sb_w_in", "sb_q_norm_g", "sb_k_norm_g", "sg_z_norm_g", "sg_w_spatial", "sg_b_spatial",
           "hyb_w_out", "cv_w_pw1", "cv_b_pw1", "cv_w_dw", "cv_b_dw", "cv_ln_g", "cv_ln_b", "cv_w_pw2", "cv_b_pw2",
           "ffn_norm_g", "ffn_w_up", "ffn_w_dw", "ffn_b_dw", "ffn_w_down")


def _pad_rows(a, rows):
    return jnp.pad(a, ((0, rows - a.shape[0]), (0, 0)))


def _step(x, tgt, w, m, v):
    n_layers = w["mix_norm_g"].shape[0]
    xi, yi, ci = _mesh_pos()
    chip = 2 * xi + yi

    assert n_layers == 4
    hosted_by = {("proj", 0): ["hyb_w_out"], ("prep", 0): [("ffn_w_up", 0)],
                 ("attn", 0): [("ffn_w_down", 0), "cv_w_pw1", "cv_w_pw2"],
                 ("up", 0): [("ffn_w_up", 1)], ("ffn_mid", 0): [("ffn_w_down", 1)],
                 ("conf_mid", 1): [("ffn_w_up", 2), ("ffn_w_down", 2)],
                 ("up", 1): [("ffn_w_up", 3)], ("ffn_mid", 1): [("ffn_w_down", 3)]}
    full = {}

    def shard_of(key):
        if isinstance(key, tuple):
            return w[key[0]][key[1]:key[1] + 1].astype(BF)
        return w[key].astype(BF)

    def keep(key, g4):
        if (key[0] if isinstance(key, tuple) else key) in BIG_ROW:
            g4 = g4.reshape(g4.shape[0], 1, g4.shape[1] * g4.shape[2], g4.shape[3])
        full[key] = g4

    def hosting(fn, point, *args, **kw):
        keys = hosted_by.get(point)
        if not keys:
            return fn(*args, **kw)
        out, gathered = fn(*args, gather=[shard_of(k) for k in keys], **kw)
        for key, g4 in zip(keys, gathered):
            keep(key, g4)
        return out

    keep("sb_w_in", _all_gather(shard_of("sb_w_in")))
    small_local = [w[name] for name in SMALL_SHARDED]
    gathered = _all_gather(_pack(small_local, 32)[None])[0]
    per_chip = [_unpack(gathered[p], [a.shape for a in small_local]) for p in range(N_CHIPS)]
    for k, name in enumerate(SMALL_SHARDED):
        full[name] = jnp.concatenate([per_chip[p][k] for p in range(N_CHIPS)], axis=-1)
    for name in SMALL_REPLICATED:
        full[name] = w[name]

    mean64, fold64 = _group_matrices()
    ffn_wdw = [_pad_rows(full["ffn_w_dw"][i], 8) for i in range(n_layers)]
    cv_wdw = [_pad_rows(full["cv_w_dw"][j], 32) for j in range(n_layers // 2)]
    row = lambda a: a.reshape(1, -1)

    saved = []
    cur = x
    for i in range(n_layers):
        j = i // 2
        rec = {"x_in": cur}
        h = _rms_fwd(cur, row(full["mix_norm_g"][i]))
        rec["h_mix"] = h
        if i % 2 == 0:
            proj = hosting(_mm_nn, ("proj", i), h, full["sb_w_in"], j)
            qg = row(jnp.tile(full["sb_q_norm_g"][j], 512 // HEAD_DIM))
            kg = row(jnp.tile(full["sb_k_norm_g"][j], 512 // HEAD_DIM))
            zg = row(full["sg_z_norm_g"][j])
            bexp = jnp.repeat(full["sg_b_spatial"][j].T, HEAD_DIM, axis=1)
            qkv, gated = hosting(_mix_prep_fwd, ("prep", i), proj, qg, kg, zg, full["sg_w_spatial"], j, bexp, mean64)
            att_bf, att_32 = hosting(_attn_fwd, ("attn", i), qkv)
            mix = jnp.concatenate([att_bf, gated], axis=1)
            cur = _mm_nn(mix, full["hyb_w_out"], j, resid=cur)
            rec.update(proj=proj, qkv=qkv, att_32=att_32, mix=mix, qg=qg, kg=kg, zg=zg, bexp=bexp)
        else:
            p1 = _mm_nn(h, full["cv_w_pw1"], j, bias=row(full["cv_b_pw1"][j]), out_dtype=BF)
            ys, yc = hosting(_conf_mid_fwd, ("conf_mid", i), p1, cv_wdw[j], row(full["cv_b_dw"][j]),
                             row(full["cv_ln_g"][j]), row(full["cv_ln_b"][j]))
            cur = _mm_nn(ys, full["cv_w_pw2"], j, bias=row(full["cv_b_pw2"][j]), resid=cur)
            rec.update(p1=p1, ys=ys, yc=yc)
        rec["x_mid"] = cur
        h = _rms_fwd(cur, row(full["ffn_norm_g"][i]))
        up = hosting(_mm_nn, ("up", i), h, full[("ffn_w_up", i)], 0, out_dtype=BF)
        act = hosting(_ffn_mid_fwd, ("ffn_mid", i), up, ffn_wdw[i], row(full["ffn_b_dw"][i]))
        cur = _mm_nn(act, full[("ffn_w_down", i)], 0, resid=cur)
        rec.update(h_ffn=h, up=up, act=act)
        saved.append(rec)

    loss_vec, dy, dy_bf = _loss_grad(cur, tgt)
    loss = lax.psum(loss_vec[0, 0], ("x", "y", "c"))

    gbig = {name: None for name in BIG_COL + BIG_ROW}
    gsmall = {name: [None] * w[name].shape[0] for name in SMALL_SHARDED + SMALL_REPLICATED}
    n_of = {name: w[name].shape[0] for name in BIG_COL + BIG_ROW}
    for i in reversed(range(n_layers)):
        j = i // 2
        rec = saved[i]
        dact = _mm_nt(dy_bf, full[("ffn_w_down", i)], 0, out_dtype=BF)
        gbig["ffn_w_down"] = _mm_tn(rec["act"], dy_bf, 1, n_of["ffn_w_down"], i, gbig["ffn_w_down"])
        dup, dwdw, dbdw = _ffn_mid_bwd(rec["up"], dact, ffn_wdw[i], row(full["ffn_b_dw"][i]))
        gsmall["ffn_w_dw"][i] = dwdw[:FFN_K]
        gsmall["ffn_b_dw"][i] = dbdw[0]
        dh = _mm_nt(dup, full[("ffn_w_up", i)], 0)
        gbig["ffn_w_up"] = _mm_tn(rec["h_ffn"], dup, N_CHIPS, n_of["ffn_w_up"], i, gbig["ffn_w_up"])
        dy, dy_bf, dg = _rms_bwd(dh, rec["x_mid"], row(full["ffn_norm_g"][i]), dy)
        gsmall["ffn_norm_g"][i] = dg[0]
        if i % 2 == 0:
            dmix = _mm_nt(dy_bf, full["hyb_w_out"], j)
            gbig["hyb_w_out"] = _mm_tn(rec["mix"], dy_bf, 1, n_of["hyb_w_out"], j, gbig["hyb_w_out"])
            dq, dk, dv = _attn_bwd(rec["qkv"], rec["att_32"], dmix)
            dproj, dqg, dkg, dzg, dws, dbe = _mix_prep_bwd(
                rec["proj"], dq, dk, dv, dmix, rec["qg"], rec["kg"], rec["zg"], full["sg_w_spatial"], j, rec["bexp"],
                mean64, fold64)
            gsmall["sb_q_norm_g"][j] = dqg[0, :HEAD_DIM]
            gsmall["sb_k_norm_g"][j] = dkg[0, :HEAD_DIM]
            gsmall["sg_z_norm_g"][j] = dzg[0]
            gsmall["sg_w_spatial"][j] = dws
            gsmall["sg_b_spatial"][j] = dbe[:, ::HEAD_DIM].T
            dh = _mm_nt(dproj, full["sb_w_in"], j)
            gbig["sb_w_in"] = _mm_tn(rec["h_mix"], dproj, N_CHIPS, n_of["sb_w_in"], j, gbig["sb_w_in"])
        else:
            dys = _mm_nt(dy_bf, full["cv_w_pw2"], j, out_dtype=BF)
            gbig["cv_w_pw2"] = _mm_tn(rec["ys"], dy_bf, 1, n_of["cv_w_pw2"], j, gbig["cv_w_pw2"])
            dp1, dwdw, dbdw, dlg, dlb, db1, db2 = _conf_mid_bwd(
                rec["p1"], rec["yc"], dys, dy, cv_wdw[j], row(full["cv_ln_g"][j]), row(full["cv_ln_b"][j]))
            gsmall["cv_w_dw"][j] = dwdw[:CONV_K]
            gsmall["cv_b_dw"][j] = dbdw[0]
            gsmall["cv_ln_g"][j] = dlg[0]
            gsmall["cv_ln_b"][j] = dlb[0]
            gsmall["cv_b_pw1"][j] = db1[0]
            gsmall["cv_b_pw2"][j] = db2[0]
            dh = _mm_nt(dp1, full["cv_w_pw1"], j)
            gbig["cv_w_pw1"] = _mm_tn(rec["h_mix"], dp1, N_CHIPS, n_of["cv_w_pw1"], j, gbig["cv_w_pw1"])
        dy, dy_bf, dg = _rms_bwd(dh, rec["x_in"], row(full["mix_norm_g"][i]), dy)
        gsmall["mix_norm_g"][i] = dg[0]

    grads = {}
    for name in BIG_COL:
        grads[name] = _reduce_scatter(gbig[name], BF)
    for name in BIG_ROW:
        g4 = gbig[name]
        r = g4.shape[2] // N_CHIPS
        grads[name] = _reduce_scatter(g4.reshape(g4.shape[0], N_CHIPS, r, g4.shape[3]), BF)
    small_names = SMALL_REPLICATED + SMALL_SHARDED
    small_full = [jnp.stack(gsmall[name]) for name in small_names]
    packed = _pack(small_full, 32 * N_CHIPS)
    rows_q = packed.shape[0] // N_CHIPS
    summed = _reduce_scatter(packed.reshape(1, N_CHIPS, rows_q, LANES), F32)
    summed = _all_gather(summed).reshape(-1, LANES)
    for name, gsum in zip(small_names, _unpack(summed, [a.shape for a in small_full])):
        if name in SMALL_SHARDED:
            n_loc = w[name].shape[-1]
            split = gsum.reshape(gsum.shape[:-1] + (N_CHIPS, n_loc))
            gsum = lax.dynamic_index_in_dim(split, chip, axis=split.ndim - 2, keepdims=False)
        grads[name] = gsum

    delta, new_m, new_v = {}, {}, {}
    for name in BIG_COL + BIG_ROW:
        shp = w[name].shape
        two_d = lambda a: a.reshape(shp[0] * shp[1], shp[2])
        d, nm, nv = _adamw(two_d(w[name]), two_d(grads[name]), two_d(m[name]), two_d(v[name]))
        delta[name], new_m[name], new_v[name] = d.reshape(shp), nm.reshape(shp), nv.reshape(shp)
    shapes = [w[name].shape for name in small_names]
    d, nm, nv = _adamw(*(_pack([src[name] for name in small_names], 256) for src in (w, grads, m, v)))
    for name, a, b_, c_ in zip(small_names, _unpack(d, shapes), _unpack(nm, shapes), _unpack(nv, shapes)):
        delta[name], new_m[name], new_v[name] = a, b_, c_

    return (loss, dy, *[grads[n] for n in WEIGHTS], *[delta[n] for n in WEIGHTS],
            *[new_m[n] for n in WEIGHTS], *[new_v[n] for n in WEIGHTS])


def kernel(x, mix_norm_g, sb_w_in, sb_q_norm_g, sb_k_norm_g, sg_z_norm_g, sg_w_spatial, sg_b_spatial, hyb_w_out, cv_w_pw1, cv_b_pw1, cv_w_dw, cv_b_dw, cv_ln_g, cv_ln_b, cv_w_pw2, cv_b_pw2, ffn_norm_g, ffn_w_up, ffn_w_dw, ffn_b_dw, ffn_w_down, loss_target, m_mix_norm_g, m_sb_w_in, m_sb_q_norm_g, m_sb_k_norm_g, m_sg_z_norm_g, m_sg_w_spatial, m_sg_b_spatial, m_hyb_w_out, m_cv_w_pw1, m_cv_b_pw1, m_cv_w_dw, m_cv_b_dw, m_cv_ln_g, m_cv_ln_b, m_cv_w_pw2, m_cv_b_pw2, m_ffn_norm_g, m_ffn_w_up, m_ffn_w_dw, m_ffn_b_dw, m_ffn_w_down, v_mix_norm_g, v_sb_w_in, v_sb_q_norm_g, v_sb_k_norm_g, v_sg_z_norm_g, v_sg_w_spatial, v_sg_b_spatial, v_hyb_w_out, v_cv_w_pw1, v_cv_b_pw1, v_cv_w_dw, v_cv_b_dw, v_cv_ln_g, v_cv_ln_b, v_cv_w_pw2, v_cv_b_pw2, v_ffn_norm_g, v_ffn_w_up, v_ffn_w_dw, v_ffn_b_dw, v_ffn_w_down):
    given = dict(locals())
    w = {n: given[n] for n in WEIGHTS}
    m = {n: given["m_" + n] for n in WEIGHTS}
    v = {n: given["v_" + n] for n in WEIGHTS}
    out = _step(x[0], loss_target[0], w, m, v)
    return (out[0], out[1][None], *out[2:])
```

```python
import functools

import jax
import jax.numpy as jnp
from jax import lax
from jax.experimental import pallas as pl
from jax.experimental.pallas import tpu as pltpu

F32 = jnp.float32
BF = jnp.bfloat16
SDS = jax.ShapeDtypeStruct
HI = lax.Precision.HIGHEST
MESH = pl.DeviceIdType.MESH

NORM_EPS = 1e-6
HEAD_DIM = 64
ATT_BLOCK = 128
CHUNK = 128
CONV_K = 31
CONV_HALO = 32
FFN_K = 3
FFN_HALO = 16
LANES = 128
N_CHIPS = 4
VMEM_LIMIT_BYTES = 56 * 2**20

ADAM_LR = 0.001
ADAM_B1 = 0.9
ADAM_B2 = 0.999
ADAM_EPS = 1e-08
ADAM_WD = 0.01
ADAM_STEP = 10

NT_DIMS = (((1,), (1,)), ((), ()))
TN_DIMS = (((0,), (0,)), ((), ()))


def _call(body, **kw):
    return pl.pallas_call(body, **kw)


def _params(*sem):
    return pltpu.CompilerParams(dimension_semantics=sem, vmem_limit_bytes=VMEM_LIMIT_BYTES)


def _gelu(x):
    return 0.5 * x * (1.0 + lax.erf(x * 0.7071067811865476))


def _rms(x, g):
    y = x * lax.rsqrt(jnp.mean(x * x, axis=-1, keepdims=True) + NORM_EPS)
    return y * g


def _rms_fwd(x, g):
    t, d = x.shape
    tm = min(512, t)

    def body(x_ref, g_ref, o_ref):
        o_ref[...] = _rms(x_ref[...], g_ref[...]).astype(o_ref.dtype)

    return _call(
        body, name="rms_fwd", grid=(t // tm,),
        in_specs=[pl.BlockSpec((tm, d), lambda i: (i, 0)), pl.BlockSpec((1, d), lambda i: (0, 0))],
        out_specs=pl.BlockSpec((tm, d), lambda i: (i, 0)),
        out_shape=SDS((t, d), BF), compiler_params=_params("parallel"))(x, g)


def _mm_nn(a, w, l, bias=None, resid=None, out_dtype=F32, gather=None, norm_g=None):
    m, k = a.shape
    _, p_n, kw, n = w.shape
    assert k == kw
    normed = norm_g is not None
    assert not normed or (p_n == 1 and not gather)
    tm = min(512 if normed else 1024, m)
    tn = n if (normed or k * n * 2 <= 4 * 2**20) else n // 2
    nj = n // tn
    in_specs = [pl.BlockSpec((tm, k), lambda i, p, j: (i, 0)),
                pl.BlockSpec((None, None, k, tn), lambda i, p, j: (l, p, 0, j))]
    args = [a, w]
    if bias is not None:
        in_specs.append(pl.BlockSpec((1, tn), lambda i, p, j: (0, p * nj + j)))
        args.append(bias)
    if resid is not None:
        in_specs.append(pl.BlockSpec((tm, tn), lambda i, p, j: (i, p * nj + j)))
        args.append(resid)
    if normed:
        in_specs.append(pl.BlockSpec((1, n), lambda i, p, j: (0, 0)))
        args.append(norm_g)
    n_in = len(args)

    def body(*refs):
        acc = jnp.dot(refs[0][...], refs[1][...], preferred_element_type=F32)
        nxt = 2
        if bias is not None:
            acc = acc + refs[nxt][...]
            nxt += 1
        if resid is not None:
            acc = refs[nxt][...] + acc
        refs[n_in][...] = acc.astype(refs[n_in].dtype)
        if normed:
            refs[n_in + 1][...] = _rms(acc, refs[n_in - 1][...]).astype(BF)

    out_spec = pl.BlockSpec((tm, tn), lambda i, p, j: (i, p * nj + j))
    kw = dict(name="mm_nn", grid=(m // tm, p_n, nj), in_specs=in_specs,
              out_specs=[out_spec, out_spec] if normed else out_spec,
              out_shape=[SDS((m, n), out_dtype), SDS((m, n), BF)] if normed else SDS((m, p_n * n), out_dtype))
    if gather:
        (out,), gathered = _call_gathering(body, gather, args, **kw)
        return out, gathered
    return _call(body, compiler_params=_params("parallel", "parallel", "parallel"), **kw)(*args)


def _mm_nt(dy, w, l, out_dtype=F32):
    m, n_all = dy.shape
    _, p_n, r, n = w.shape
    assert n_all == p_n * n
    tm = min(512, m)

    def body(dy_ref, w_ref, o_ref):
        acc = lax.dot_general(dy_ref[:, 0:n], w_ref[0], NT_DIMS, preferred_element_type=F32)
        for p in range(1, p_n):
            acc = acc + lax.dot_general(dy_ref[:, p * n:(p + 1) * n], w_ref[p], NT_DIMS, preferred_element_type=F32)
        o_ref[...] = acc.astype(o_ref.dtype)

    return _call(
        body, name="mm_nt", grid=(m // tm,),
        in_specs=[pl.BlockSpec((tm, n_all), lambda i: (i, 0)),
                  pl.BlockSpec((None, p_n, r, n), lambda i: (l, 0, 0, 0))],
        out_specs=pl.BlockSpec((tm, r), lambda i: (i, 0)),
        out_shape=SDS((m, r), out_dtype),
        compiler_params=_params("parallel"))(dy, w)


def _mm_nt_rms_bwd(dy, w, l, x, g, dres):
    m, n_all = dy.shape
    _, p_n, r, n = w.shape
    assert n_all == p_n * n and x.shape == (m, r)
    tm = min(256, m)

    def body(dy_ref, w_ref, x_ref, g_ref, r_ref, dx_ref, dxb_ref, dg_ref):
        dh = lax.dot_general(dy_ref[:, 0:n], w_ref[0], NT_DIMS, preferred_element_type=F32)
        for p in range(1, p_n):
            dh = dh + lax.dot_general(dy_ref[:, p * n:(p + 1) * n], w_ref[p], NT_DIMS, preferred_element_type=F32)
        _, vjp = jax.vjp(_rms, x_ref[...], g_ref[...])
        dx, dg = vjp(dh)
        dx = dx + r_ref[...]
        dx_ref[...] = dx
        dxb_ref[...] = dx.astype(BF)

        @pl.when(pl.program_id(0) == 0)
        def _():
            dg_ref[...] = jnp.zeros_like(dg_ref)

        dg_ref[...] += dg

    row = pl.BlockSpec((tm, r), lambda i: (i, 0))
    vec = pl.BlockSpec((1, r), lambda i: (0, 0))
    return _call(
        body, name="mm_nt_rms_bwd", grid=(m // tm,),
        in_specs=[pl.BlockSpec((tm, n_all), lambda i: (i, 0)), pl.BlockSpec((None, p_n, r, n), lambda i: (l, 0, 0, 0)),
                  row, vec, row],
        out_specs=[row, row, vec], out_shape=[SDS((m, r), F32), SDS((m, r), BF), SDS((1, r), F32)],
        compiler_params=_params("arbitrary"))(dy, w, x, g, dres)


def _mm_tn(a, dy, p_n, n_layers, l, buf=None):
    m, k = a.shape
    n = dy.shape[1] // p_n
    tm = min(2048, m)
    tk = k if k <= 1024 else k // 2
    nm = m // tm

    def body(a_ref, dy_ref, *rest):
        o_ref, acc_ref = rest[-2], rest[-1]
        mi = pl.program_id(2)
        part = lax.dot_general(a_ref[...], dy_ref[...], TN_DIMS, preferred_element_type=F32)

        @pl.when(mi == 0)
        def _():
            acc_ref[...] = part

        @pl.when(mi > 0)
        def _():
            acc_ref[...] += part

        @pl.when(mi == nm - 1)
        def _():
            o_ref[...] = acc_ref[...].astype(o_ref.dtype)

    in_specs = [pl.BlockSpec((tm, tk), lambda p, kk, mi: (mi, kk)),
                pl.BlockSpec((tm, n), lambda p, kk, mi: (mi, p))]
    args = [a, dy]
    aliases = {}
    if buf is not None:
        in_specs.append(pl.BlockSpec(memory_space=pl.ANY))
        args.append(buf)
        aliases = {2: 0}
    return _call(
        body, name="mm_tn", grid=(p_n, k // tk, nm), in_specs=in_specs,
        out_specs=pl.BlockSpec((None, None, tk, n), lambda p, kk, mi: (l, p, kk, 0)),
        out_shape=SDS((n_layers, p_n, k, n), BF), scratch_shapes=[pltpu.VMEM((tk, n), F32)],
        input_output_aliases=aliases,
        compiler_params=_params("parallel", "parallel", "arbitrary"))(*args)


def _loss_grad(y, tgt):
    t, d = y.shape
    tm = min(512, t)

    def body(y_ref, t_ref, l_ref, d_ref, db_ref):
        err = y_ref[...] - t_ref[...]
        dy = err * (1.0 / d)
        d_ref[...] = dy
        db_ref[...] = dy.astype(BF)
        part = 0.5 * jnp.sum(jnp.sum(err * err, axis=1, keepdims=True) * (1.0 / d), axis=0, keepdims=True)

        @pl.when(pl.program_id(0) == 0)
        def _():
            l_ref[...] = jnp.zeros_like(l_ref)

        l_ref[...] += jnp.broadcast_to(part, l_ref.shape)

    row = pl.BlockSpec((tm, d), lambda i: (i, 0))
    return _call(
        body, name="loss_grad", grid=(t // tm,), in_specs=[row, row],
        out_specs=[pl.BlockSpec((1, LANES), lambda i: (0, 0)), row, row],
        out_shape=[SDS((1, LANES), F32), SDS((t, d), F32), SDS((t, d), BF)],
        compiler_params=_params("arbitrary"))(y, tgt)


def _prev_halo(tr, halo, col):
    return lambda i: (jnp.maximum(i * (tr // halo) - 1, 0), col)


def _next_halo(tr, halo, n_rows, col):
    return lambda i: (jnp.minimum((i + 1) * (tr // halo), n_rows // halo - 1), col)


def _shifted_back(x):
    return pltpu.roll(x, 1, 0), pltpu.roll(x, 2, 0)


def _conv3(x, w_ref, b_ref, col):
    x1, x2 = _shifted_back(x)
    return b_ref[:, col] + w_ref[pl.ds(0, 1), col] * x2 + w_ref[pl.ds(1, 1), col] * x1 + w_ref[pl.ds(2, 1), col] * x


def _ffn_mid_fwd(up, w_dw, b_dw, gather=None):
    t, f2 = up.shape
    f = f2 // 2
    tr = min(256, t)
    h = FFN_HALO

    def body(g_ref, gp_ref, v_ref, w_ref, b_ref, o_ref):
        first_tile = pl.program_id(0) == 0

        def strip(c, carry):
            col = pl.ds(pl.multiple_of(c * LANES, LANES), LANES)
            x = jnp.concatenate([jnp.where(first_tile, 0.0, gp_ref[:, col].astype(F32)), g_ref[:, col].astype(F32)], axis=0)
            gc = _conv3(x, w_ref, b_ref, col)[h:]
            o_ref[:, col] = (gc * jax.nn.sigmoid(gc) * v_ref[:, col].astype(F32)).astype(o_ref.dtype)
            return carry

        lax.fori_loop(0, f // LANES, strip, 0)

    kw = dict(name="ffn_mid_fwd", grid=(t // tr,),
              in_specs=[pl.BlockSpec((tr, f), lambda i: (i, 0)), pl.BlockSpec((h, f), _prev_halo(tr, h, 0)),
                        pl.BlockSpec((tr, f), lambda i: (i, 1)),
                        pl.BlockSpec((8, f), lambda i: (0, 0)), pl.BlockSpec((1, f), lambda i: (0, 0))],
              out_specs=pl.BlockSpec((tr, f), lambda i: (i, 0)), out_shape=SDS((t, f), BF))
    args = (up, up, up, w_dw, b_dw)
    if gather:
        (out,), gathered = _call_gathering(body, gather, args, **kw)
        return out, gathered
    return _call(body, compiler_params=_params("parallel"), **kw)(*args)


def _ffn_mid_bwd(up, da, w_dw, b_dw):
    t, f2 = up.shape
    f = f2 // 2
    tr = min(256, t)
    h = FFN_HALO
    n_tiles = t // tr

    def body(g_ref, gp_ref, gn_ref, v_ref, vn_ref, da_ref, dan_ref, w_ref, b_ref, dup_ref, dw_ref, db_ref):
        i = pl.program_id(0)
        last = i == n_tiles - 1
        n = tr + h

        @pl.when(i == 0)
        def _():
            dw_ref[...] = jnp.zeros_like(dw_ref)
            db_ref[...] = jnp.zeros_like(db_ref)

        def rows(tile_ref, next_ref, col):
            return jnp.concatenate([tile_ref[:, col].astype(F32), next_ref[:, col].astype(F32)], axis=0)

        def strip(c, carry):
            col = pl.ds(pl.multiple_of(c * LANES, LANES), LANES)
            x = jnp.concatenate([jnp.where(i == 0, 0.0, gp_ref[:, col].astype(F32)), rows(g_ref, gn_ref, col)], axis=0)
            x1, x2 = _shifted_back(x)
            w0, w1, w2 = (w_ref[pl.ds(k, 1), col] for k in range(FFN_K))
            gc = (b_ref[:, col] + w0 * x2 + w1 * x1 + w2 * x)[h:]
            dav = rows(da_ref, dan_ref, col)
            sg = jax.nn.sigmoid(gc)
            dup_ref[:, pl.ds(pl.multiple_of(f + c * LANES, LANES), LANES)] = (dav * gc * sg)[:tr].astype(dup_ref.dtype)
            dgc = dav * rows(v_ref, vn_ref, col) * (sg * (1.0 + gc * (1.0 - sg)))
            dgc = jnp.concatenate([dgc[:tr], jnp.where(last, 0.0, dgc[tr:])], axis=0)
            d1, d2 = pltpu.roll(dgc, n - 1, 0), pltpu.roll(dgc, n - 2, 0)
            dup_ref[:, col] = (w2 * dgc + w1 * d1 + w0 * d2)[:tr].astype(dup_ref.dtype)
            dgt = dgc[:tr]
            for k, past in enumerate((x2, x1, x)):
                dw_ref[pl.ds(k, 1), col] += jnp.sum(past[h:h + tr] * dgt, axis=0, keepdims=True)
            db_ref[:, col] += jnp.sum(dgt, axis=0, keepdims=True)
            return carry

        lax.fori_loop(0, f // LANES, strip, 0)

    tile = lambda col: pl.BlockSpec((tr, f), lambda i: (i, col))
    nxt = lambda col: pl.BlockSpec((h, f), _next_halo(tr, h, t, col))
    return _call(
        body, name="ffn_mid_bwd", grid=(n_tiles,),
        in_specs=[tile(0), pl.BlockSpec((h, f), _prev_halo(tr, h, 0)), nxt(0), tile(1), nxt(1), tile(0), nxt(0),
                  pl.BlockSpec((8, f), lambda i: (0, 0)), pl.BlockSpec((1, f), lambda i: (0, 0))],
        out_specs=[pl.BlockSpec((tr, f2), lambda i: (i, 0)), pl.BlockSpec((8, f), lambda i: (0, 0)),
                   pl.BlockSpec((1, f), lambda i: (0, 0))],
        out_shape=[SDS((t, f2), BF), SDS((8, f), F32), SDS((1, f), F32)],
        compiler_params=_params("arbitrary"))(up, up, up, up, up, da, da, w_dw, b_dw)


def _ln_silu(yc, g, b):
    mu = jnp.mean(yc, axis=-1, keepdims=True)
    xc = yc - mu
    y = xc * lax.rsqrt(jnp.mean(xc * xc, axis=-1, keepdims=True) + NORM_EPS)
    return jax.nn.silu(y * g + b)


SUBLANES = 8
CONV_PAD = 24


def _glu(a, g):
    return a.astype(F32) * jax.nn.sigmoid(g.astype(F32))


def _glu_strip(ygs_ref, first_tile, a_ref, ap_ref, g_ref, gp_ref, col, h, tr):
    ygs_ref[pl.ds(0, h), :] = jnp.where(first_tile, 0.0, _glu(ap_ref[:, col], gp_ref[:, col]))
    ygs_ref[pl.ds(h, tr), :] = _glu(a_ref[:, col], g_ref[:, col])


def _shift_past(sh_ref, ygs_ref, h, n):
    x = ygs_ref[...]
    for r in range(1, SUBLANES):
        sh_ref[r, pl.ds(0, n + CONV_PAD), :] = pltpu.roll(x, r, 0)[h - CONV_PAD:]


def _past_rows(sh_ref, ygs_ref, h, n, s):
    a, r = divmod(s, SUBLANES)
    if r == 0:
        return ygs_ref[pl.ds(h - SUBLANES * a, n), :]
    return sh_ref[r, pl.ds(CONV_PAD - SUBLANES * a, n), :]


def _conf_mid_fwd(p1, w_dw, b_dw, ln_g, ln_b, gather=None):
    t, w2 = p1.shape
    w = w2 // 2
    tr = min(256, t)
    h = CONV_HALO
    rc = 32

    def body(a_ref, ap_ref, g_ref, gp_ref, w_ref, b_ref, lg_ref, lb_ref, o_ref, yc_ref, ygs_ref, sh_ref):
        first_tile = pl.program_id(0) == 0

        def strip(c, carry):
            col = pl.ds(pl.multiple_of(c * LANES, LANES), LANES)
            _glu_strip(ygs_ref, first_tile, a_ref, ap_ref, g_ref, gp_ref, col, h, tr)
            _shift_past(sh_ref, ygs_ref, h, tr)
            acc = jnp.broadcast_to(b_ref[:, col], (tr, LANES))
            for k in range(CONV_K):
                acc = acc + w_ref[pl.ds(k, 1), col] * _past_rows(sh_ref, ygs_ref, h, tr, CONV_K - 1 - k)
            yc_ref[:, col] = acc
            return carry

        lax.fori_loop(0, w // LANES, strip, 0)

        def rows(r, carry):
            rs = pl.ds(pl.multiple_of(r * rc, rc), rc)
            o_ref[rs, :] = _ln_silu(yc_ref[rs, :], lg_ref[...], lb_ref[...]).astype(o_ref.dtype)
            return carry

        lax.fori_loop(0, tr // rc, rows, 0)

    vec = pl.BlockSpec((1, w), lambda i: (0, 0))
    tile = pl.BlockSpec((tr, w), lambda i: (i, 0))
    kw = dict(name="conf_mid_fwd", grid=(t // tr,),
              in_specs=[tile, pl.BlockSpec((h, w), _prev_halo(tr, h, 0)),
                        pl.BlockSpec((tr, w), lambda i: (i, 1)), pl.BlockSpec((h, w), _prev_halo(tr, h, 1)),
                        pl.BlockSpec((32, w), lambda i: (0, 0)), vec, vec, vec],
              out_specs=[tile, tile], out_shape=[SDS((t, w), BF), SDS((t, w), F32)],
              scratch_shapes=[pltpu.VMEM((h + tr, LANES), F32), pltpu.VMEM((SUBLANES, tr + CONV_PAD, LANES), F32)])
    args = (p1, p1, p1, p1, w_dw, b_dw, ln_g, ln_b)
    if gather:
        return _call_gathering(body, gather, args, **kw)
    return _call(body, compiler_params=_params("parallel"), **kw)(*args)


def _conf_mid_bwd(p1, yc, dys, dy, w_dw, ln_g, ln_b):
    t, w2 = p1.shape
    w = w2 // 2
    tr = min(256, t)
    h = CONV_HALO
    rc = 32
    n_tiles = t // tr

    def body(a_ref, ap_ref, g_ref, gp_ref, yc_ref, ycn_ref, dys_ref, dysn_ref, dy_ref, w_ref, lg_ref, lb_ref,
             dp_ref, dw_ref, db_ref, dlg_ref, dlb_ref, db1_ref, db2_ref, dyc_ref, ygs_ref, sh_ref, shf_ref):
        i = pl.program_id(0)
        last = i == n_tiles - 1

        @pl.when(i == 0)
        def _():
            for ref in (dw_ref, db_ref, dlg_ref, dlb_ref, db1_ref, db2_ref):
                ref[...] = jnp.zeros_like(ref)

        def ln_rows(r, carry):
            rs = pl.ds(pl.multiple_of(r * rc, rc), rc)
            _, vjp = jax.vjp(_ln_silu, yc_ref[rs, :], lg_ref[...], lb_ref[...])
            dyc, dlg, dlb = vjp(dys_ref[rs, :].astype(F32))
            dyc_ref[rs, :] = dyc
            dlg_ref[...] += dlg
            dlb_ref[...] += dlb
            return carry

        lax.fori_loop(0, tr // rc, ln_rows, 0)
        _, vjp = jax.vjp(_ln_silu, ycn_ref[...], lg_ref[...], lb_ref[...])
        dyc_ref[pl.ds(tr, h), :] = jnp.where(last, 0.0, vjp(dysn_ref[...].astype(F32))[0])
        db2_ref[...] += jnp.sum(dy_ref[...], axis=0, keepdims=True)

        def back(c, carry):
            col = pl.ds(pl.multiple_of(c * LANES, LANES), LANES)
            gcol = pl.ds(pl.multiple_of(w + c * LANES, LANES), LANES)
            _glu_strip(ygs_ref, i == 0, a_ref, ap_ref, g_ref, gp_ref, col, h, tr)
            _shift_past(sh_ref, ygs_ref, h, tr)
            dyc_all = dyc_ref[:, col]
            for r in range(1, SUBLANES):
                shf_ref[r, pl.ds(0, tr + CONV_PAD), :] = pltpu.roll(dyc_all, tr + h - r, 0)[:tr + CONV_PAD]
            dyc = dyc_ref[pl.ds(0, tr), col]
            dyg = jnp.zeros((tr, LANES), F32)
            for k in range(CONV_K):
                s = CONV_K - 1 - k
                a, r = divmod(s, SUBLANES)
                if r == 0:
                    future = dyc_ref[pl.ds(SUBLANES * a, tr), col]
                else:
                    future = shf_ref[r, pl.ds(SUBLANES * a, tr), :]
                dyg = dyg + w_ref[pl.ds(k, 1), col] * future
                dw_ref[pl.ds(k, 1), col] += jnp.sum(_past_rows(sh_ref, ygs_ref, h, tr, s) * dyc, axis=0, keepdims=True)
            db_ref[:, col] += jnp.sum(dyc, axis=0, keepdims=True)
            sg = jax.nn.sigmoid(g_ref[:, col].astype(F32))
            da = dyg * sg
            dg = dyg * a_ref[:, col].astype(F32) * sg * (1.0 - sg)
            dp_ref[:, col] = da.astype(dp_ref.dtype)
            dp_ref[:, gcol] = dg.astype(dp_ref.dtype)
            db1_ref[:, col] += jnp.sum(da, axis=0, keepdims=True)
            db1_ref[:, gcol] += jnp.sum(dg, axis=0, keepdims=True)
            return carry

        lax.fori_loop(0, w // LANES, back, 0)

    tile = lambda col: pl.BlockSpec((tr, w), lambda i: (i, col))
    prv = lambda col: pl.BlockSpec((h, w), _prev_halo(tr, h, col))
    nxt = pl.BlockSpec((h, w), _next_halo(tr, h, t, 0))
    vec = pl.BlockSpec((1, w), lambda i: (0, 0))
    return _call(
        body, name="conf_mid_bwd", grid=(n_tiles,),
        in_specs=[tile(0), prv(0), tile(1), prv(1), tile(0), nxt, tile(0), nxt, tile(0),
                  pl.BlockSpec((32, w), lambda i: (0, 0)), vec, vec],
        out_specs=[pl.BlockSpec((tr, w2), lambda i: (i, 0)), pl.BlockSpec((32, w), lambda i: (0, 0)), vec, vec, vec,
                   pl.BlockSpec((1, w2), lambda i: (0, 0)), vec],
        out_shape=[SDS((t, w2), BF), SDS((32, w), F32), SDS((1, w), F32), SDS((1, w), F32), SDS((1, w), F32),
                   SDS((1, w2), F32), SDS((1, w), F32)],
        scratch_shapes=[pltpu.VMEM((tr + h, w), F32), pltpu.VMEM((h + tr, LANES), F32),
                        pltpu.VMEM((SUBLANES, tr + CONV_PAD, LANES), F32), pltpu.VMEM((SUBLANES, tr + CONV_PAD, LANES), F32)],
        compiler_params=_params("arbitrary"))(p1, p1, p1, p1, yc, yc, dys, dys, dy, w_dw, ln_g, ln_b)


def _group_matrices():
    i = lax.broadcasted_iota(jnp.int32, (512, 512), 0)
    j = lax.broadcasted_iota(jnp.int32, (512, 512), 1)
    mean64 = jnp.where(i // HEAD_DIM == j // HEAD_DIM, 1.0 / HEAD_DIM, 0.0).astype(F32)
    fold64 = jnp.where(i % HEAD_DIM == j % HEAD_DIM, 1.0, 0.0).astype(F32)
    return mean64, fold64


def _split_dot(x, mat):
    hi = x.astype(BF)
    lo = (x - hi.astype(F32)).astype(BF)
    mb = mat.astype(BF)
    return jnp.dot(hi, mb, preferred_element_type=F32) + jnp.dot(lo, mb, preferred_element_type=F32)


@jax.custom_vjp
def _group_sum(x, mat):
    return _split_dot(x, mat)


_group_sum.defvjp(lambda x, mat: (_split_dot(x, mat), mat), lambda mat, ct: (_split_dot(ct, mat), jnp.zeros_like(mat)))


def _bf_dot_plain(a, b):
    return jnp.dot(a.astype(BF), b.astype(BF), preferred_element_type=F32)


@jax.custom_vjp
def _bf_dot(a, b):
    return _bf_dot_plain(a, b)


def _bf_dot_bwd(res, ct):
    a, b = res
    cb = ct.astype(BF)
    return (lax.dot_general(cb, b.astype(BF), NT_DIMS, preferred_element_type=F32),
            lax.dot_general(a.astype(BF), cb, TN_DIMS, preferred_element_type=F32))


_bf_dot.defvjp(lambda a, b: (_bf_dot_plain(a, b), (a, b)), _bf_dot_bwd)


def _prep_tile(proj, qg, kg, zg, ws, bexp, mean64, differentiated=False):
    sw = 512
    q, k, v, u, z = (proj[:, n * sw:(n + 1) * sw] for n in range(5))
    group_sum, dot = (_group_sum, _bf_dot) if differentiated else (_split_dot, _bf_dot_plain)

    def group_norm(x):
        return x * lax.rsqrt(group_sum(x * x, mean64) + NORM_EPS)

    qn = group_norm(q) * qg
    kn = group_norm(k) * kg
    zn = group_norm(_gelu(z)) * zg
    row = lax.broadcasted_iota(jnp.int32, (CHUNK, CHUNK), 0)
    col = lax.broadcasted_iota(jnp.int32, (CHUNK, CHUNK), 1)
    first = lax.broadcasted_iota(jnp.int32, (1, LANES), 1) < HEAD_DIM
    parts = []
    for pr in range(sw // LANES):
        zp = zn[:, pr * LANES:(pr + 1) * LANES]
        s0 = dot(jnp.where(col <= row, ws[2 * pr], 0.0), zp)
        s1 = dot(jnp.where(col <= row, ws[2 * pr + 1], 0.0), zp)
        parts.append(jnp.where(first, s0, s1))
    s = jnp.concatenate(parts, axis=1) + bexp
    return qn, kn, v, _gelu(u) * s


def _mix_prep_fwd(proj, qg, kg, zg, w_s, l, bexp, mean64, gather=None):
    t = proj.shape[0]
    tr = CHUNK

    def body(p_ref, qg_ref, kg_ref, zg_ref, ws_ref, be_ref, m_ref, qkv_ref, go_ref):
        qn, kn, v, go = _prep_tile(p_ref[...], qg_ref[...], kg_ref[...], zg_ref[...], ws_ref[...], be_ref[...], m_ref[...])
        qkv_ref[:, 0:512] = qn.astype(BF)
        qkv_ref[:, 512:1024] = kn.astype(BF)
        qkv_ref[:, 1024:1536] = v.astype(BF)
        go_ref[...] = go.astype(BF)

    vec = pl.BlockSpec((1, 512), lambda i: (0, 0))
    kw = dict(name="mix_prep_fwd", grid=(t // tr,),
              in_specs=[pl.BlockSpec((tr, 2560), lambda i: (i, 0)), vec, vec, vec,
                        pl.BlockSpec((None, 8, CHUNK, CHUNK), lambda i: (l, 0, 0, 0)),
                        pl.BlockSpec((CHUNK, 512), lambda i: (0, 0)), pl.BlockSpec((512, 512), lambda i: (0, 0))],
              out_specs=[pl.BlockSpec((tr, 1536), lambda i: (i, 0)), pl.BlockSpec((tr, 512), lambda i: (i, 0))],
              out_shape=[SDS((t, 1536), BF), SDS((t, 512), BF)])
    args = (proj, qg, kg, zg, w_s, bexp, mean64)
    if gather:
        return _call_gathering(body, gather, args, **kw)
    return _call(body, compiler_params=_params("parallel"), **kw)(*args)


def _mix_prep_bwd(proj, dq, dk, dv, dmix, qg, kg, zg, w_s, l, bexp, mean64, fold64):
    t = proj.shape[0]
    tr = CHUNK
    n_tiles = t // tr

    def body(p_ref, dq_ref, dk_ref, dv_ref, dgo_ref, qg_ref, kg_ref, zg_ref, ws_ref, be_ref, m_ref, f_ref,
             dp_ref, dqg_ref, dkg_ref, dzg_ref, dws_ref, dbe_ref):
        i = pl.program_id(0)

        @pl.when(i == 0)
        def _():
            for ref in (dqg_ref, dkg_ref, dzg_ref, dws_ref, dbe_ref):
                ref[...] = jnp.zeros_like(ref)

        fn = functools.partial(_prep_tile, mean64=m_ref[...], differentiated=True)
        _, vjp = jax.vjp(fn, p_ref[...], qg_ref[...], kg_ref[...], zg_ref[...], ws_ref[...], be_ref[...])
        dp, dqg, dkg, dzg, dws, dbe = vjp((dq_ref[...], dk_ref[...], dv_ref[...], dgo_ref[...]))
        dp_ref[...] = dp.astype(BF)
        dqg_ref[pl.ds(0, 1), :] += dqg
        dkg_ref[pl.ds(0, 1), :] += dkg
        dzg_ref[pl.ds(0, 1), :] += dzg
        dws_ref[...] += dws
        dbe_ref[...] += dbe

        @pl.when(i == n_tiles - 1)
        def _():
            dqg_ref[...] = jnp.dot(dqg_ref[...], f_ref[...], precision=HI, preferred_element_type=F32)
            dkg_ref[...] = jnp.dot(dkg_ref[...], f_ref[...], precision=HI, preferred_element_type=F32)
            dbe_ref[...] = jnp.dot(dbe_ref[...], m_ref[...] * float(HEAD_DIM), precision=HI, preferred_element_type=F32)

    vec = pl.BlockSpec((1, 512), lambda i: (0, 0))
    acc = pl.BlockSpec((8, 512), lambda i: (0, 0))
    sq = pl.BlockSpec((512, 512), lambda i: (0, 0))
    row = pl.BlockSpec((tr, 512), lambda i: (i, 0))
    return _call(
        body, name="mix_prep_bwd", grid=(n_tiles,),
        in_specs=[pl.BlockSpec((tr, 2560), lambda i: (i, 0)), row, row, row, pl.BlockSpec((tr, 512), lambda i: (i, 1)),
                  vec, vec, vec, pl.BlockSpec((None, 8, CHUNK, CHUNK), lambda i: (l, 0, 0, 0)),
                  pl.BlockSpec((CHUNK, 512), lambda i: (0, 0)), sq, sq],
        out_specs=[pl.BlockSpec((tr, 2560), lambda i: (i, 0)), acc, acc, acc,
                   pl.BlockSpec((8, CHUNK, CHUNK), lambda i: (0, 0, 0)), pl.BlockSpec((CHUNK, 512), lambda i: (0, 0))],
        out_shape=[SDS((t, 2560), BF), SDS((8, 512), F32), SDS((8, 512), F32), SDS((8, 512), F32),
                   SDS((8, CHUNK, CHUNK), F32), SDS((CHUNK, 512), F32)],
        compiler_params=_params("arbitrary"))(proj, dq, dk, dv, dmix, qg, kg, zg, w_s, bexp, mean64, fold64)


def _sb_logs(qh, kb, valid):
    z = lax.dot_general(qh, kb, NT_DIMS, preferred_element_type=F32) * (HEAD_DIM ** -0.5)
    soft = jnp.log1p(jnp.exp(-jnp.abs(z)))
    lk_raw = -(jnp.maximum(z, 0.0) + soft)
    ls = -(jnp.maximum(-z, 0.0) + soft)
    return lk_raw, ls, jnp.where(valid, lk_raw, 0.0)


def _sb_weights(ls, run, tail, valid):
    return jnp.where(valid, jnp.exp(ls + run + tail), 0.0)


def _att_masks(b):
    row = lax.broadcasted_iota(jnp.int32, (b, b), 0)
    col = lax.broadcasted_iota(jnp.int32, (b, b), 1)
    first = lax.broadcasted_iota(jnp.int32, (1, LANES), 1) < HEAD_DIM
    return row, col, first


N_PAIRS = 4


def _load_kv(qkv_hbm, k_scr, v_scr, sems, group, width):
    ck = pltpu.make_async_copy(qkv_hbm.at[:, pl.ds(pl.multiple_of(512 + group * width, LANES), width)], k_scr, sems.at[0])
    cv = pltpu.make_async_copy(qkv_hbm.at[:, pl.ds(pl.multiple_of(1024 + group * width, LANES), width)], v_scr, sems.at[1])
    ck.start()
    cv.start()
    ck.wait()
    cv.wait()


def _split_heads(ref, pair, first):
    x = ref[:, pair * LANES:(pair + 1) * LANES]
    zero = jnp.zeros_like(x)
    return jnp.where(first, x, zero), jnp.where(first, zero, x)


def _any_weight_left(run_ref, n_heads):
    top = run_ref[0]
    for hh in range(1, n_heads):
        top = jnp.maximum(top, run_ref[hh])
    return jnp.max(jnp.exp(top)) > 0.0


def _attn_fwd(qkv, pairs_per_step=4, gather=None):
    t = qkv.shape[0]
    b = ATT_BLOCK
    nq = t // b
    width = pairs_per_step * LANES
    n_heads = 2 * pairs_per_step

    def body(q_ref, qkv_hbm, ob_ref, o32_ref, k_scr, v_scr, acc_ref, run_ref, sems):
        group, qi = pl.program_id(0), pl.program_id(1)

        @pl.when(qi == 0)
        def _():
            _load_kv(qkv_hbm, k_scr, v_scr, sems, group, width)

        row, col, first = _att_masks(b)
        qh = [x for pr in range(pairs_per_step) for x in _split_heads(q_ref, pr, first)]
        upper = jnp.where(row > col, 1.0, 0.0).astype(BF)
        acc_ref[...] = jnp.zeros_like(acc_ref)
        run_ref[...] = jnp.zeros_like(run_ref)
        heads = range(n_heads)

        def step(carry):
            j, _ = carry
            rows = pl.ds(pl.multiple_of(j * b, b), b)
            valid = jnp.logical_or(j != qi, col < row)
            lanes = [pl.ds((hh // 2) * LANES, LANES) for hh in heads]
            logs = [_sb_logs(qh[hh], k_scr[rows, lanes[hh]], valid) for hh in heads]
            tails = [_split_dot(logs[hh][2], upper) for hh in heads]
            for hh in heads:
                wgt = _sb_weights(logs[hh][1], run_ref[hh], tails[hh], valid)
                acc_ref[hh] += jnp.dot(wgt.astype(BF), v_scr[rows, lanes[hh]], preferred_element_type=F32)
            for hh in heads:
                run_ref[hh] += jnp.sum(logs[hh][2], axis=1, keepdims=True)
            return j - 1, _any_weight_left(run_ref, n_heads)

        lax.while_loop(lambda c: jnp.logical_and(c[0] >= 0, c[1]), step, (qi, jnp.bool_(True)))
        for pr in range(pairs_per_step):
            out = jnp.where(first, acc_ref[2 * pr], acc_ref[2 * pr + 1])
            ob_ref[:, pr * LANES:(pr + 1) * LANES] = out.astype(BF)
            o32_ref[:, pr * LANES:(pr + 1) * LANES] = out

    blk = pl.BlockSpec((b, width), lambda g, qi: (qi, g))
    kw = dict(name="attn_fwd", grid=(N_PAIRS // pairs_per_step, nq),
              in_specs=[blk, pl.BlockSpec(memory_space=pl.ANY)], out_specs=[blk, blk],
              out_shape=[SDS((t, 512), BF), SDS((t, 512), F32)],
              scratch_shapes=[pltpu.VMEM((t, width), BF), pltpu.VMEM((t, width), BF),
                              pltpu.VMEM((n_heads, b, LANES), F32), pltpu.VMEM((n_heads, b, 1), F32),
                              pltpu.SemaphoreType.DMA((2,))])
    if gather:
        return _call_gathering(body, gather, (qkv, qkv), **kw)
    return _call(body, compiler_params=_params("arbitrary", "arbitrary"), **kw)(qkv, qkv)


def _attn_bwd(qkv, a32, dmix, pairs_per_step=2):
    t = qkv.shape[0]
    b = ATT_BLOCK
    nq = t // b
    width = pairs_per_step * LANES
    n_heads = 2 * pairs_per_step

    def body(q_ref, a_ref, da_ref, qkv_hbm, dq_ref, dk_hbm, dv_hbm,
             k_scr, v_scr, dk_scr, dv_scr, dqa_ref, run_ref, rung_ref, sems):
        group, qi = pl.program_id(0), pl.program_id(1)

        @pl.when(qi == 0)
        def _():
            _load_kv(qkv_hbm, k_scr, v_scr, sems, group, width)
            dk_scr[...] = jnp.zeros_like(dk_scr)
            dv_scr[...] = jnp.zeros_like(dv_scr)

        row, col, first = _att_masks(b)
        qh, dah, dtot = [], [], []
        for pr in range(pairs_per_step):
            qh += _split_heads(q_ref, pr, first)
            da = da_ref[:, pr * LANES:(pr + 1) * LANES]
            prod = da * a_ref[:, pr * LANES:(pr + 1) * LANES]
            dtot += [jnp.sum(jnp.where(first, prod, 0.0), axis=1, keepdims=True),
                     jnp.sum(jnp.where(first, 0.0, prod), axis=1, keepdims=True)]
            dah += [jnp.where(first, da, 0.0).astype(BF), jnp.where(first, 0.0, da).astype(BF)]
        upper = jnp.where(row > col, 1.0, 0.0).astype(BF)
        lower_incl = jnp.where(row >= col, 1.0, 0.0).astype(BF)
        dqa_ref[...] = jnp.zeros_like(dqa_ref)
        run_ref[...] = jnp.zeros_like(run_ref)
        rung_ref[...] = jnp.zeros_like(rung_ref)
        heads = range(n_heads)

        def step(carry):
            j, _ = carry
            rows = pl.ds(pl.multiple_of(j * b, b), b)
            valid = jnp.logical_or(j != qi, col < row)
            lanes = [pl.ds((hh // 2) * LANES, LANES) for hh in heads]
            logs = [_sb_logs(qh[hh], k_scr[rows, lanes[hh]], valid) for hh in heads]
            dps = [lax.dot_general(dah[hh], v_scr[rows, lanes[hh]], NT_DIMS, preferred_element_type=F32) for hh in heads]
            tails = [_split_dot(logs[hh][2], upper) for hh in heads]
            wgts = [_sb_weights(logs[hh][1], run_ref[hh], tails[hh], valid) for hh in heads]
            gs = [wgts[hh] * dps[hh] for hh in heads]
            g_froms = [_split_dot(gs[hh], lower_incl) for hh in heads]
            for hh in heads:
                lk_raw, ls, _ = logs[hh]
                dlk = jnp.where(valid, dtot[hh] - rung_ref[hh] - g_froms[hh], 0.0)
                dz = ((gs[hh] * jnp.exp(lk_raw) - dlk * jnp.exp(ls)) * (HEAD_DIM ** -0.5)).astype(BF)
                dqa_ref[hh] += jnp.dot(dz, k_scr[rows, lanes[hh]], preferred_element_type=F32)
                dk_scr[rows, lanes[hh]] += lax.dot_general(dz, qh[hh], TN_DIMS, preferred_element_type=F32)
                dv_scr[rows, lanes[hh]] += lax.dot_general(wgts[hh].astype(BF), dah[hh], TN_DIMS, preferred_element_type=F32)
            for hh in heads:
                rung_ref[hh] += jnp.sum(gs[hh], axis=1, keepdims=True)
                run_ref[hh] += jnp.sum(logs[hh][2], axis=1, keepdims=True)
            return j - 1, _any_weight_left(run_ref, n_heads)

        lax.while_loop(lambda c: jnp.logical_and(c[0] >= 0, c[1]), step, (qi, jnp.bool_(True)))
        for pr in range(pairs_per_step):
            dq_ref[:, pr * LANES:(pr + 1) * LANES] = jnp.where(first, dqa_ref[2 * pr], dqa_ref[2 * pr + 1])

        @pl.when(qi == nq - 1)
        def _():
            cols = pl.ds(pl.multiple_of(group * width, LANES), width)
            ck = pltpu.make_async_copy(dk_scr, dk_hbm.at[:, cols], sems.at[0])
            cv = pltpu.make_async_copy(dv_scr, dv_hbm.at[:, cols], sems.at[1])
            ck.start()
            cv.start()
            ck.wait()
            cv.wait()

    blk = pl.BlockSpec((b, width), lambda g, qi: (qi, g))
    anywhere = pl.BlockSpec(memory_space=pl.ANY)
    return _call(
        body, name="attn_bwd", grid=(N_PAIRS // pairs_per_step, nq),
        in_specs=[blk, blk, blk, anywhere], out_specs=[blk, anywhere, anywhere],
        out_shape=[SDS((t, 512), F32), SDS((t, 512), F32), SDS((t, 512), F32)],
        scratch_shapes=[pltpu.VMEM((t, width), BF), pltpu.VMEM((t, width), BF),
                        pltpu.VMEM((t, width), F32), pltpu.VMEM((t, width), F32),
                        pltpu.VMEM((n_heads, b, LANES), F32), pltpu.VMEM((n_heads, b, 1), F32),
                        pltpu.VMEM((n_heads, b, 1), F32), pltpu.SemaphoreType.DMA((2,))],
        compiler_params=_params("arbitrary", "arbitrary"))(qkv, a32, dmix, qkv)


def _adamw(w, g, m, v):
    n, c = w.shape
    tr = min(256, n)
    assert n % tr == 0

    def body(w_ref, g_ref, m_ref, v_ref, d_ref, nm_ref, nv_ref):
        g = g_ref[...]
        m = ADAM_B1 * m_ref[...] + (1.0 - ADAM_B1) * g
        v = ADAM_B2 * v_ref[...] + (1.0 - ADAM_B2) * jnp.square(g)
        m_hat = m / (1.0 - ADAM_B1 ** ADAM_STEP)
        v_hat = v / (1.0 - ADAM_B2 ** ADAM_STEP)
        d_ref[...] = -ADAM_LR * (m_hat / (jnp.sqrt(v_hat) + ADAM_EPS) + ADAM_WD * w_ref[...])
        nm_ref[...] = m
        nv_ref[...] = v

    blk = pl.BlockSpec((tr, c), lambda i: (i, 0))
    return _call(
        body, name="adamw", grid=(n // tr,), in_specs=[blk] * 4, out_specs=[blk] * 3,
        out_shape=[SDS((n, c), F32)] * 3, compiler_params=_params("parallel"))(w, g, m, v)


def _mesh_pos():
    return lax.axis_index("x"), lax.axis_index("y"), lax.axis_index("c")


def _other_chips(x, y):
    return [(1 - x, y), (x, 1 - y), (1 - x, 1 - y)]


HBM_SPEC = pl.BlockSpec(memory_space=pltpu.HBM)


GATHER_COPIES = 6


def _gather_steps(s_ref, o_ref, send_sems, recv_sems, local_sems, slot):
    h = s_ref.shape[1] // 2
    x, y, c = _mesh_pos()
    sibling = (x, y, 1 - c)
    chips = _other_chips(x, y)
    base = GATHER_COPIES * slot

    def half(px, py, hc):
        return o_ref.at[:, 2 * px + py, pl.ds(hc * h, h), :]

    def copy(k, dst, to, src=None):
        return pltpu.make_async_remote_copy(
            src_ref=dst if src is None else src, dst_ref=dst, send_sem=send_sems.at[base + k],
            recv_sem=recv_sems.at[base + k], device_id=to, device_id_type=MESH)

    mine = pltpu.make_async_copy(s_ref, o_ref.at[:, 2 * x + y], local_sems.at[slot])
    first = [copy(j, half(x, y, c), (*chip, c), src=s_ref.at[:, pl.ds(c * h, h), :]) for j, chip in enumerate(chips)]
    passed = [copy(3 + j, half(*chip, c), sibling) for j, chip in enumerate(chips)]

    def start():
        mine.start()
        for cp in first:
            cp.start()

    def finish():
        for j, chip in enumerate(chips):
            copy(j, half(*chip, c), (x, y, c)).wait_recv()
            passed[j].start()
        for j, chip in enumerate(chips):
            copy(3 + j, half(*chip, 1 - c), (x, y, c)).wait_recv()
        for cp in first + passed:
            cp.wait_send()
        mine.wait()

    return start, finish


def _gather_scratch(n):
    return [pltpu.SemaphoreType.DMA((GATHER_COPIES * n,)), pltpu.SemaphoreType.DMA((GATHER_COPIES * n,)),
            pltpu.SemaphoreType.DMA((n,))]


def _gathered_shape(shard):
    n_l, r, c_w = shard.shape
    return SDS((n_l, N_CHIPS, r, c_w), shard.dtype)


def _all_gather(shard):
    def body(s_ref, o_ref, send_sems, recv_sems, local_sems):
        start, finish = _gather_steps(s_ref, o_ref, send_sems, recv_sems, local_sems, 0)
        start()
        finish()

    return _call(body, name="all_gather", in_specs=[HBM_SPEC], out_specs=HBM_SPEC, out_shape=_gathered_shape(shard),
                 scratch_shapes=_gather_scratch(1))(shard)


def _call_gathering(body, shards, args, *, name, grid, in_specs, out_specs, out_shape, scratch_shapes=()):
    out_specs = list(out_specs) if isinstance(out_specs, (list, tuple)) else [out_specs]
    out_shape = list(out_shape) if isinstance(out_shape, (list, tuple)) else [out_shape]
    n_in, n_out, n_sh, n_scr = len(in_specs), len(out_specs), len(shards), len(scratch_shapes)

    def hosting_body(*refs):
        ins, s_refs = refs[:n_in], refs[n_in:n_in + n_sh]
        outs = refs[n_in + n_sh:n_in + n_sh + n_out]
        o_refs = refs[n_in + n_sh + n_out:n_in + 2 * n_sh + n_out]
        scratch = refs[n_in + 2 * n_sh + n_out:n_in + 2 * n_sh + n_out + n_scr]
        send_sems, recv_sems, local_sems = refs[-3:]
        steps = [_gather_steps(s_refs[k], o_refs[k], send_sems, recv_sems, local_sems, k) for k in range(n_sh)]
        is_first = functools.reduce(jnp.logical_and, [pl.program_id(a) == 0 for a in range(len(grid))])
        is_last = functools.reduce(jnp.logical_and, [pl.program_id(a) == grid[a] - 1 for a in range(len(grid))])

        @pl.when(is_first)
        def _():
            for start, _ in steps:
                start()

        body(*ins, *outs, *scratch)

        @pl.when(is_last)
        def _():
            for _, finish in steps:
                finish()

    res = _call(
        hosting_body, name=name + "_gathering", grid=grid, in_specs=list(in_specs) + [HBM_SPEC] * n_sh,
        out_specs=out_specs + [HBM_SPEC] * n_sh, out_shape=out_shape + [_gathered_shape(s) for s in shards],
        scratch_shapes=list(scratch_shapes) + _gather_scratch(n_sh),
        compiler_params=_params(*(["arbitrary"] * len(grid))))(*args, *shards)
    return res[:n_out], res[n_out:]


def _row_tile(h):
    for cand in (256, 176, 128, 64, 32, 16):
        if h % cand == 0:
            return cand
    raise ValueError(h)


def _reduce_scatter(g, mid_dtype):
    n_l, n_p, r, c_w = g.shape
    h = r // 2
    tr = _row_tile(h)
    nt = h // tr
    x, y, c = _mesh_pos()
    c_arr = jnp.reshape(c, (1,)).astype(jnp.int32)
    p_arr = jnp.reshape(2 * x + y, (1,)).astype(jnp.int32)

    def to_sibling_body(g_ref, a_ref, send_sem, recv_sem):
        x, y, c = _mesh_pos()
        cp = pltpu.make_async_remote_copy(
            src_ref=g_ref.at[:, :, pl.ds((1 - c) * h, h), :], dst_ref=a_ref, send_sem=send_sem, recv_sem=recv_sem,
            device_id=(x, y, 1 - c), device_id_type=MESH)
        cp.start()
        cp.wait()

    from_sibling = _call(
        to_sibling_body, name="rs_pair", in_specs=[HBM_SPEC], out_specs=HBM_SPEC,
        out_shape=SDS((n_l, n_p, h, c_w), g.dtype),
        scratch_shapes=[pltpu.SemaphoreType.DMA, pltpu.SemaphoreType.DMA],
        )(g)

    def pair_add_body(c_ref, g_ref, a_ref, o_ref):
        o_ref[...] = (g_ref[...].astype(F32) + a_ref[...].astype(F32)).astype(o_ref.dtype)

    blk = (None, None, tr, c_w)
    pair_sum = _call(
        pair_add_body, name="rs_pair_add",
        grid_spec=pltpu.PrefetchScalarGridSpec(
            num_scalar_prefetch=1, grid=(n_l, n_p, nt),
            in_specs=[pl.BlockSpec(blk, lambda l, p, t, c_ref: (l, p, c_ref[0] * nt + t, 0)),
                      pl.BlockSpec(blk, lambda l, p, t, c_ref: (l, p, t, 0))],
            out_specs=pl.BlockSpec(blk, lambda l, p, t, c_ref: (l, p, t, 0))),
        out_shape=SDS((n_l, n_p, h, c_w), mid_dtype),
        compiler_params=_params("parallel", "parallel", "parallel"))(c_arr, g, from_sibling)

    def to_chips_body(s_ref, b_ref, send_sems, recv_sems):
        x, y, c = _mesh_pos()
        cps = [pltpu.make_async_remote_copy(
            src_ref=s_ref.at[:, 2 * chip[0] + chip[1]], dst_ref=b_ref.at[j], send_sem=send_sems.at[j],
            recv_sem=recv_sems.at[j], device_id=(*chip, c), device_id_type=MESH)
            for j, chip in enumerate(_other_chips(x, y))]
        for cp in cps:
            cp.start()
        for cp in cps:
            cp.wait()

    from_chips = _call(
        to_chips_body, name="rs_chips", in_specs=[HBM_SPEC], out_specs=HBM_SPEC,
        out_shape=SDS((3, n_l, h, c_w), mid_dtype),
        scratch_shapes=[pltpu.SemaphoreType.DMA((3,)), pltpu.SemaphoreType.DMA((3,))],
        )(pair_sum)

    def chip_add_body(p_ref, c_ref, s_ref, b_ref, o_ref):
        acc = s_ref[...].astype(F32)
        for j in range(3):
            acc = acc + b_ref[j].astype(F32)
        o_ref[...] = acc

    half_sum = _call(
        chip_add_body, name="rs_chip_add",
        grid_spec=pltpu.PrefetchScalarGridSpec(
            num_scalar_prefetch=2, grid=(n_l, nt),
            in_specs=[pl.BlockSpec((None, None, tr, c_w), lambda l, t, p_ref, c_ref: (l, p_ref[0], t, 0)),
                      pl.BlockSpec((3, None, tr, c_w), lambda l, t, p_ref, c_ref: (0, l, t, 0))],
            out_specs=pl.BlockSpec((None, tr, c_w), lambda l, t, p_ref, c_ref: (l, c_ref[0] * nt + t, 0))),
        out_shape=SDS((n_l, r, c_w), F32),
        compiler_params=_params("parallel", "parallel"))(p_arr, c_arr, pair_sum, from_chips)

    def swap_body(i_ref, o_ref, send_sem, recv_sem):
        x, y, c = _mesh_pos()
        mine = o_ref.at[:, pl.ds(c * h, h), :]
        theirs = o_ref.at[:, pl.ds((1 - c) * h, h), :]
        pltpu.make_async_remote_copy(src_ref=mine, dst_ref=mine, send_sem=send_sem, recv_sem=recv_sem,
                                     device_id=(x, y, 1 - c), device_id_type=MESH).start()
        wait = pltpu.make_async_remote_copy(src_ref=mine, dst_ref=theirs, send_sem=send_sem, recv_sem=recv_sem,
                                            device_id=(x, y, 1 - c), device_id_type=MESH)
        wait.wait_send()
        wait.wait_recv()

    return _call(
        swap_body, name="rs_swap", in_specs=[HBM_SPEC], out_specs=HBM_SPEC, out_shape=SDS((n_l, r, c_w), F32),
        input_output_aliases={0: 0},
        scratch_shapes=[pltpu.SemaphoreType.DMA, pltpu.SemaphoreType.DMA],
        )(half_sum)


def _pack(arrays, row_multiple):
    flat = jnp.concatenate([a.reshape(-1).astype(F32) for a in arrays])
    unit = row_multiple * LANES
    padded = -(-flat.shape[0] // unit) * unit
    return jnp.pad(flat, (0, padded - flat.shape[0])).reshape(padded // LANES, LANES)


def _unpack(packed, shapes):
    flat = packed.reshape(-1)
    out, pos = [], 0
    for s in shapes:
        size = 1
        for dim in s:
            size *= dim
        out.append(flat[pos:pos + size].reshape(s))
        pos += size
    return out


BIG_COL = ("sb_w_in", "cv_w_pw1", "ffn_w_up")
BIG_ROW = ("hyb_w_out", "cv_w_pw2", "ffn_w_down")
SMALL_SHARDED = ("cv_b_pw1", "cv_w_dw", "cv_b_dw", "cv_ln_g", "cv_ln_b", "cv_b_pw2", "ffn_w_dw")
SMALL_REPLICATED = ("mix_norm_g", "sb_q_norm_g", "sb_k_norm_g", "sg_z_norm_g", "sg_w_spatial", "sg_b_spatial",
                    "ffn_norm_g", "ffn_b_dw")
WEIGHTS = ("mix_norm_g", "sb_w_in", "sb_q_norm_g", "sb_k_norm_g", "sg_z_norm_g", "sg_w_spatial", "sg_b_spatial",
           "hyb_w_out", "cv_w_pw1", "cv_b_pw1", "cv_w_dw", "cv_b_dw", "cv_ln_g", "cv_ln_b", "cv_w_pw2", "cv_b_pw2",
           "ffn_norm_g", "ffn_w_up", "ffn_w_dw", "ffn_b_dw", "ffn_w_down")


def _pad_rows(a, rows):
    return jnp.pad(a, ((0, rows - a.shape[0]), (0, 0)))


def _step(x, tgt, w, m, v):
    n_layers = w["mix_norm_g"].shape[0]
    xi, yi, ci = _mesh_pos()
    chip = 2 * xi + yi

    assert n_layers == 4
    hosted_by = {("proj", 0): ["hyb_w_out"], ("prep", 0): [("ffn_w_up", 0)],
                 ("attn", 0): [("ffn_w_down", 0), "cv_w_pw1", "cv_w_pw2"],
                 ("up", 0): [("ffn_w_up", 1)], ("ffn_mid", 0): [("ffn_w_down", 1)],
                 ("conf_mid", 1): [("ffn_w_up", 2), ("ffn_w_down", 2)],
                 ("up", 1): [("ffn_w_up", 3)], ("ffn_mid", 1): [("ffn_w_down", 3)]}
    full = {}

    def shard_of(key):
        if isinstance(key, tuple):
            return w[key[0]][key[1]:key[1] + 1].astype(BF)
        return w[key].astype(BF)

    def keep(key, g4):
        if (key[0] if isinstance(key, tuple) else key) in BIG_ROW:
            g4 = g4.reshape(g4.shape[0], 1, g4.shape[1] * g4.shape[2], g4.shape[3])
        full[key] = g4

    def hosting(fn, point, *args, **kw):
        keys = hosted_by.get(point)
        if not keys:
            return fn(*args, **kw)
        out, gathered = fn(*args, gather=[shard_of(k) for k in keys], **kw)
        for key, g4 in zip(keys, gathered):
            keep(key, g4)
        return out

    keep("sb_w_in", _all_gather(shard_of("sb_w_in")))
    small_local = [w[name] for name in SMALL_SHARDED]
    gathered = _all_gather(_pack(small_local, 32)[None])[0]
    per_chip = [_unpack(gathered[p], [a.shape for a in small_local]) for p in range(N_CHIPS)]
    for k, name in enumerate(SMALL_SHARDED):
        full[name] = jnp.concatenate([per_chip[p][k] for p in range(N_CHIPS)], axis=-1)
    for name in SMALL_REPLICATED:
        full[name] = w[name]

    mean64, fold64 = _group_matrices()
    ffn_wdw = [_pad_rows(full["ffn_w_dw"][i], 8) for i in range(n_layers)]
    cv_wdw = [_pad_rows(full["cv_w_dw"][j], 32) for j in range(n_layers // 2)]
    row = lambda a: a.reshape(1, -1)

    saved = []
    cur = x
    h = _rms_fwd(cur, row(full["mix_norm_g"][0]))
    for i in range(n_layers):
        j = i // 2
        rec = {"x_in": cur, "h_mix": h}
        if i % 2 == 0:
            proj = hosting(_mm_nn, ("proj", i), h, full["sb_w_in"], j)
            qg = row(jnp.tile(full["sb_q_norm_g"][j], 512 // HEAD_DIM))
            kg = row(jnp.tile(full["sb_k_norm_g"][j], 512 // HEAD_DIM))
            zg = row(full["sg_z_norm_g"][j])
            bexp = jnp.repeat(full["sg_b_spatial"][j].T, HEAD_DIM, axis=1)
            qkv, gated = hosting(_mix_prep_fwd, ("prep", i), proj, qg, kg, zg, full["sg_w_spatial"], j, bexp, mean64)
            att_bf, att_32 = hosting(_attn_fwd, ("attn", i), qkv)
            mix = jnp.concatenate([att_bf, gated], axis=1)
            cur, h = _mm_nn(mix, full["hyb_w_out"], j, resid=cur, norm_g=row(full["ffn_norm_g"][i]))
            rec.update(proj=proj, qkv=qkv, att_32=att_32, mix=mix, qg=qg, kg=kg, zg=zg, bexp=bexp)
        else:
            p1 = _mm_nn(h, full["cv_w_pw1"], j, bias=row(full["cv_b_pw1"][j]), out_dtype=BF)
            ys, yc = hosting(_conf_mid_fwd, ("conf_mid", i), p1, cv_wdw[j], row(full["cv_b_dw"][j]),
                             row(full["cv_ln_g"][j]), row(full["cv_ln_b"][j]))
            cur, h = _mm_nn(ys, full["cv_w_pw2"], j, bias=row(full["cv_b_pw2"][j]), resid=cur,
                            norm_g=row(full["ffn_norm_g"][i]))
            rec.update(p1=p1, ys=ys, yc=yc)
        rec["x_mid"] = cur
        up = hosting(_mm_nn, ("up", i), h, full[("ffn_w_up", i)], 0, out_dtype=BF)
        act = hosting(_ffn_mid_fwd, ("ffn_mid", i), up, ffn_wdw[i], row(full["ffn_b_dw"][i]))
        rec.update(h_ffn=h, up=up, act=act)
        if i + 1 < n_layers:
            cur, h = _mm_nn(act, full[("ffn_w_down", i)], 0, resid=cur, norm_g=row(full["mix_norm_g"][i + 1]))
        else:
            cur = _mm_nn(act, full[("ffn_w_down", i)], 0, resid=cur)
        saved.append(rec)

    loss_vec, dy, dy_bf = _loss_grad(cur, tgt)
    loss = lax.psum(loss_vec[0, 0], ("x", "y", "c"))

    gbig = {name: None for name in BIG_COL + BIG_ROW}
    gsmall = {name: [None] * w[name].shape[0] for name in SMALL_SHARDED + SMALL_REPLICATED}
    n_of = {name: w[name].shape[0] for name in BIG_COL + BIG_ROW}
    for i in reversed(range(n_layers)):
        j = i // 2
        rec = saved[i]
        dact = _mm_nt(dy_bf, full[("ffn_w_down", i)], 0, out_dtype=BF)
        gbig["ffn_w_down"] = _mm_tn(rec["act"], dy_bf, 1, n_of["ffn_w_down"], i, gbig["ffn_w_down"])
        dup, dwdw, dbdw = _ffn_mid_bwd(rec["up"], dact, ffn_wdw[i], row(full["ffn_b_dw"][i]))
        gsmall["ffn_w_dw"][i] = dwdw[:FFN_K]
        gsmall["ffn_b_dw"][i] = dbdw[0]
        gbig["ffn_w_up"] = _mm_tn(rec["h_ffn"], dup, N_CHIPS, n_of["ffn_w_up"], i, gbig["ffn_w_up"])
        dy, dy_bf, dg = _mm_nt_rms_bwd(dup, full[("ffn_w_up", i)], 0, rec["x_mid"], row(full["ffn_norm_g"][i]), dy)
        gsmall["ffn_norm_g"][i] = dg[0]
        if i % 2 == 0:
            dmix = _mm_nt(dy_bf, full["hyb_w_out"], j)
            gbig["hyb_w_out"] = _mm_tn(rec["mix"], dy_bf, 1, n_of["hyb_w_out"], j, gbig["hyb_w_out"])
            dq, dk, dv = _attn_bwd(rec["qkv"], rec["att_32"], dmix)
            dproj, dqg, dkg, dzg, dws, dbe = _mix_prep_bwd(
                rec["proj"], dq, dk, dv, dmix, rec["qg"], rec["kg"], rec["zg"], full["sg_w_spatial"], j, rec["bexp"],
                mean64, fold64)
            gsmall["sb_q_norm_g"][j] = dqg[0, :HEAD_DIM]
            gsmall["sb_k_norm_g"][j] = dkg[0, :HEAD_DIM]
            gsmall["sg_z_norm_g"][j] = dzg[0]
            gsmall["sg_w_spatial"][j] = dws
            gsmall["sg_b_spatial"][j] = dbe[:, ::HEAD_DIM].T
            dlast, w_first = dproj, full["sb_w_in"]
            gbig["sb_w_in"] = _mm_tn(rec["h_mix"], dproj, N_CHIPS, n_of["sb_w_in"], j, gbig["sb_w_in"])
        else:
            dys = _mm_nt(dy_bf, full["cv_w_pw2"], j, out_dtype=BF)
            gbig["cv_w_pw2"] = _mm_tn(rec["ys"], dy_bf, 1, n_of["cv_w_pw2"], j, gbig["cv_w_pw2"])
            dp1, dwdw, dbdw, dlg, dlb, db1, db2 = _conf_mid_bwd(
                rec["p1"], rec["yc"], dys, dy, cv_wdw[j], row(full["cv_ln_g"][j]), row(full["cv_ln_b"][j]))
            gsmall["cv_w_dw"][j] = dwdw[:CONV_K]
            gsmall["cv_b_dw"][j] = dbdw[0]
            gsmall["cv_ln_g"][j] = dlg[0]
            gsmall["cv_ln_b"][j] = dlb[0]
            gsmall["cv_b_pw1"][j] = db1[0]
            gsmall["cv_b_pw2"][j] = db2[0]
            dlast, w_first = dp1, full["cv_w_pw1"]
            gbig["cv_w_pw1"] = _mm_tn(rec["h_mix"], dp1, N_CHIPS, n_of["cv_w_pw1"], j, gbig["cv_w_pw1"])
        dy, dy_bf, dg = _mm_nt_rms_bwd(dlast, w_first, j, rec["x_in"], row(full["mix_norm_g"][i]), dy)
        gsmall["mix_norm_g"][i] = dg[0]

    grads = {}
    for name in BIG_COL:
        grads[name] = _reduce_scatter(gbig[name], BF)
    for name in BIG_ROW:
        g4 = gbig[name]
        r = g4.shape[2] // N_CHIPS
        grads[name] = _reduce_scatter(g4.reshape(g4.shape[0], N_CHIPS, r, g4.shape[3]), BF)
    small_names = SMALL_REPLICATED + SMALL_SHARDED
    small_full = [jnp.stack(gsmall[name]) for name in small_names]
    packed = _pack(small_full, 32 * N_CHIPS)
    rows_q = packed.shape[0] // N_CHIPS
    summed = _reduce_scatter(packed.reshape(1, N_CHIPS, rows_q, LANES), F32)
    summed = _all_gather(summed).reshape(-1, LANES)
    for name, gsum in zip(small_names, _unpack(summed, [a.shape for a in small_full])):
        if name in SMALL_SHARDED:
            n_loc = w[name].shape[-1]
            split = gsum.reshape(gsum.shape[:-1] + (N_CHIPS, n_loc))
            gsum = lax.dynamic_index_in_dim(split, chip, axis=split.ndim - 2, keepdims=False)
        grads[name] = gsum

    delta, new_m, new_v = {}, {}, {}
    for name in BIG_COL + BIG_ROW:
        shp = w[name].shape
        two_d = lambda a: a.reshape(shp[0] * shp[1], shp[2])
        d, nm, nv = _adamw(two_d(w[name]), two_d(grads[name]), two_d(m[name]), two_d(v[name]))
        delta[name], new_m[name], new_v[name] = d.reshape(shp), nm.reshape(shp), nv.reshape(shp)
    shapes = [w[name].shape for name in small_names]
    d, nm, nv = _adamw(*(_pack([src[name] for name in small_names], 256) for src in (w, grads, m, v)))
    for name, a, b_, c_ in zip(small_names, _unpack(d, shapes), _unpack(nm, shapes), _unpack(nv, shapes)):
        delta[name], new_m[name], new_v[name] = a, b_, c_

    return (loss, dy, *[grads[n] for n in WEIGHTS], *[delta[n] for n in WEIGHTS],
            *[new_m[n] for n in WEIGHTS], *[new_v[n] for n in WEIGHTS])


def kernel(x, mix_norm_g, sb_w_in, sb_q_norm_g, sb_k_norm_g, sg_z_norm_g, sg_w_spatial, sg_b_spatial, hyb_w_out, cv_w_pw1, cv_b_pw1, cv_w_dw, cv_b_dw, cv_ln_g, cv_ln_b, cv_w_pw2, cv_b_pw2, ffn_norm_g, ffn_w_up, ffn_w_dw, ffn_b_dw, ffn_w_down, loss_target, m_mix_norm_g, m_sb_w_in, m_sb_q_norm_g, m_sb_k_norm_g, m_sg_z_norm_g, m_sg_w_spatial, m_sg_b_spatial, m_hyb_w_out, m_cv_w_pw1, m_cv_b_pw1, m_cv_w_dw, m_cv_b_dw, m_cv_ln_g, m_cv_ln_b, m_cv_w_pw2, m_cv_b_pw2, m_ffn_norm_g, m_ffn_w_up, m_ffn_w_dw, m_ffn_b_dw, m_ffn_w_down, v_mix_norm_g, v_sb_w_in, v_sb_q_norm_g, v_sb_k_norm_g, v_sg_z_norm_g, v_sg_w_spatial, v_sg_b_spatial, v_hyb_w_out, v_cv_w_pw1, v_cv_b_pw1, v_cv_w_dw, v_cv_b_dw, v_cv_ln_g, v_cv_ln_b, v_cv_w_pw2, v_cv_b_pw2, v_ffn_norm_g, v_ffn_w_up, v_ffn_w_dw, v_ffn_b_dw, v_ffn_w_down):
    given = dict(locals())
    w = {n: given[n] for n in WEIGHTS}
    m = {n: given["m_" + n] for n in WEIGHTS}
    v = {n: given["v_" + n] for n in WEIGHTS}
    out = _step(x[0], loss_target[0], w, m, v)
    return (out[0], out[1][None], *out[2:])
```

```python
import functools

import jax
import jax.numpy as jnp
from jax import lax
from jax.experimental import pallas as pl
from jax.experimental.pallas import tpu as pltpu

F32 = jnp.float32
BF = jnp.bfloat16
SDS = jax.ShapeDtypeStruct
HI = lax.Precision.HIGHEST
MESH = pl.DeviceIdType.MESH

NORM_EPS = 1e-6
HEAD_DIM = 64
ATT_BLOCK = 128
CHUNK = 128
CONV_K = 31
CONV_HALO = 32
FFN_K = 3
FFN_HALO = 16
LANES = 128
N_CHIPS = 4
VMEM_LIMIT_BYTES = 56 * 2**20

ADAM_LR = 0.001
ADAM_B1 = 0.9
ADAM_B2 = 0.999
ADAM_EPS = 1e-08
ADAM_WD = 0.01
ADAM_STEP = 10

NT_DIMS = (((1,), (1,)), ((), ()))
TN_DIMS = (((0,), (0,)), ((), ()))


def _call(body, **kw):
    return pl.pallas_call(body, **kw)


def _params(*sem):
    return pltpu.CompilerParams(dimension_semantics=sem, vmem_limit_bytes=VMEM_LIMIT_BYTES)


def _gelu(x):
    return 0.5 * x * (1.0 + lax.erf(x * 0.7071067811865476))


def _rms(x, g):
    y = x * lax.rsqrt(jnp.mean(x * x, axis=-1, keepdims=True) + NORM_EPS)
    return y * g


def _rms_fwd(x, g):
    t, d = x.shape
    tm = min(512, t)

    def body(x_ref, g_ref, o_ref):
        o_ref[...] = _rms(x_ref[...], g_ref[...]).astype(o_ref.dtype)

    return _call(
        body, name="rms_fwd", grid=(t // tm,),
        in_specs=[pl.BlockSpec((tm, d), lambda i: (i, 0)), pl.BlockSpec((1, d), lambda i: (0, 0))],
        out_specs=pl.BlockSpec((tm, d), lambda i: (i, 0)),
        out_shape=SDS((t, d), BF), compiler_params=_params("parallel"))(x, g)


def _mm_nn(a, w, l, bias=None, resid=None, out_dtype=F32, gather=None, norm_g=None):
    m, k = a.shape
    _, p_n, kw, n = w.shape
    assert k == kw
    normed = norm_g is not None
    assert not normed or (p_n == 1 and not gather)
    tm = min(512 if normed else 1024, m)
    tn = n if (normed or k * n * 2 <= 4 * 2**20) else n // 2
    nj = n // tn
    in_specs = [pl.BlockSpec((tm, k), lambda i, p, j: (i, 0)),
                pl.BlockSpec((None, None, k, tn), lambda i, p, j: (l, p, 0, j))]
    args = [a, w]
    if bias is not None:
        in_specs.append(pl.BlockSpec((1, tn), lambda i, p, j: (0, p * nj + j)))
        args.append(bias)
    if resid is not None:
        in_specs.append(pl.BlockSpec((tm, tn), lambda i, p, j: (i, p * nj + j)))
        args.append(resid)
    if normed:
        in_specs.append(pl.BlockSpec((1, n), lambda i, p, j: (0, 0)))
        args.append(norm_g)
    n_in = len(args)

    def body(*refs):
        acc = jnp.dot(refs[0][...], refs[1][...], preferred_element_type=F32)
        nxt = 2
        if bias is not None:
            acc = acc + refs[nxt][...]
            nxt += 1
        if resid is not None:
            acc = refs[nxt][...] + acc
        refs[n_in][...] = acc.astype(refs[n_in].dtype)
        if normed:
            refs[n_in + 1][...] = _rms(acc, refs[n_in - 1][...]).astype(BF)

    out_spec = pl.BlockSpec((tm, tn), lambda i, p, j: (i, p * nj + j))
    kw = dict(name="mm_nn", grid=(m // tm, p_n, nj), in_specs=in_specs,
              out_specs=[out_spec, out_spec] if normed else out_spec,
              out_shape=[SDS((m, n), out_dtype), SDS((m, n), BF)] if normed else SDS((m, p_n * n), out_dtype))
    if gather:
        (out,), gathered = _call_gathering(body, gather, args, **kw)
        return out, gathered
    return _call(body, compiler_params=_params("parallel", "parallel", "parallel"), **kw)(*args)


def _mm_nt(dy, w, l, out_dtype=F32):
    m, n_all = dy.shape
    _, p_n, r, n = w.shape
    assert n_all == p_n * n
    tm = min(512, m)

    def body(dy_ref, w_ref, o_ref):
        acc = lax.dot_general(dy_ref[:, 0:n], w_ref[0], NT_DIMS, preferred_element_type=F32)
        for p in range(1, p_n):
            acc = acc + lax.dot_general(dy_ref[:, p * n:(p + 1) * n], w_ref[p], NT_DIMS, preferred_element_type=F32)
        o_ref[...] = acc.astype(o_ref.dtype)

    return _call(
        body, name="mm_nt", grid=(m // tm,),
        in_specs=[pl.BlockSpec((tm, n_all), lambda i: (i, 0)),
                  pl.BlockSpec((None, p_n, r, n), lambda i: (l, 0, 0, 0))],
        out_specs=pl.BlockSpec((tm, r), lambda i: (i, 0)),
        out_shape=SDS((m, r), out_dtype),
        compiler_params=_params("parallel"))(dy, w)


def _mm_nt_rms_bwd(dy, w, l, x, g, dres):
    m, n_all = dy.shape
    _, p_n, r, n = w.shape
    assert n_all == p_n * n and x.shape == (m, r)
    tm = min(256, m)

    def body(dy_ref, w_ref, x_ref, g_ref, r_ref, dx_ref, dxb_ref, dg_ref):
        dh = lax.dot_general(dy_ref[:, 0:n], w_ref[0], NT_DIMS, preferred_element_type=F32)
        for p in range(1, p_n):
            dh = dh + lax.dot_general(dy_ref[:, p * n:(p + 1) * n], w_ref[p], NT_DIMS, preferred_element_type=F32)
        _, vjp = jax.vjp(_rms, x_ref[...], g_ref[...])
        dx, dg = vjp(dh)
        dx = dx + r_ref[...]
        dx_ref[...] = dx
        dxb_ref[...] = dx.astype(BF)

        @pl.when(pl.program_id(0) == 0)
        def _():
            dg_ref[...] = jnp.zeros_like(dg_ref)

        dg_ref[...] += dg

    row = pl.BlockSpec((tm, r), lambda i: (i, 0))
    vec = pl.BlockSpec((1, r), lambda i: (0, 0))
    return _call(
        body, name="mm_nt_rms_bwd", grid=(m // tm,),
        in_specs=[pl.BlockSpec((tm, n_all), lambda i: (i, 0)), pl.BlockSpec((None, p_n, r, n), lambda i: (l, 0, 0, 0)),
                  row, vec, row],
        out_specs=[row, row, vec], out_shape=[SDS((m, r), F32), SDS((m, r), BF), SDS((1, r), F32)],
        compiler_params=_params("arbitrary"))(dy, w, x, g, dres)


def _mm_tn(a, dy, p_n, n_layers, l, buf=None):
    m, k = a.shape
    n = dy.shape[1] // p_n
    tm = min(2048, m)
    tk = k if k <= 1024 else k // 2
    nm = m // tm

    def body(a_ref, dy_ref, *rest):
        o_ref, acc_ref = rest[-2], rest[-1]
        mi = pl.program_id(2)
        part = lax.dot_general(a_ref[...], dy_ref[...], TN_DIMS, preferred_element_type=F32)

        @pl.when(mi == 0)
        def _():
            acc_ref[...] = part

        @pl.when(mi > 0)
        def _():
            acc_ref[...] += part

        @pl.when(mi == nm - 1)
        def _():
            o_ref[...] = acc_ref[...].astype(o_ref.dtype)

    in_specs = [pl.BlockSpec((tm, tk), lambda p, kk, mi: (mi, kk)),
                pl.BlockSpec((tm, n), lambda p, kk, mi: (mi, p))]
    args = [a, dy]
    aliases = {}
    if buf is not None:
        in_specs.append(pl.BlockSpec(memory_space=pl.ANY))
        args.append(buf)
        aliases = {2: 0}
    return _call(
        body, name="mm_tn", grid=(p_n, k // tk, nm), in_specs=in_specs,
        out_specs=pl.BlockSpec((None, None, tk, n), lambda p, kk, mi: (l, p, kk, 0)),
        out_shape=SDS((n_layers, p_n, k, n), BF), scratch_shapes=[pltpu.VMEM((tk, n), F32)],
        input_output_aliases=aliases,
        compiler_params=_params("parallel", "parallel", "arbitrary"))(*args)


def _loss_grad(y, tgt):
    t, d = y.shape
    tm = min(512, t)

    def body(y_ref, t_ref, l_ref, d_ref, db_ref):
        err = y_ref[...] - t_ref[...]
        dy = err * (1.0 / d)
        d_ref[...] = dy
        db_ref[...] = dy.astype(BF)
        part = 0.5 * jnp.sum(jnp.sum(err * err, axis=1, keepdims=True) * (1.0 / d), axis=0, keepdims=True)

        @pl.when(pl.program_id(0) == 0)
        def _():
            l_ref[...] = jnp.zeros_like(l_ref)

        l_ref[...] += jnp.broadcast_to(part, l_ref.shape)

    row = pl.BlockSpec((tm, d), lambda i: (i, 0))
    return _call(
        body, name="loss_grad", grid=(t // tm,), in_specs=[row, row],
        out_specs=[pl.BlockSpec((1, LANES), lambda i: (0, 0)), row, row],
        out_shape=[SDS((1, LANES), F32), SDS((t, d), F32), SDS((t, d), BF)],
        compiler_params=_params("arbitrary"))(y, tgt)


def _prev_halo(tr, halo, col):
    return lambda i: (jnp.maximum(i * (tr // halo) - 1, 0), col)


def _next_halo(tr, halo, n_rows, col):
    return lambda i: (jnp.minimum((i + 1) * (tr // halo), n_rows // halo - 1), col)


def _shifted_back(x):
    return pltpu.roll(x, 1, 0), pltpu.roll(x, 2, 0)


def _conv3(x, w_ref, b_ref, col):
    x1, x2 = _shifted_back(x)
    return b_ref[:, col] + w_ref[pl.ds(0, 1), col] * x2 + w_ref[pl.ds(1, 1), col] * x1 + w_ref[pl.ds(2, 1), col] * x


def _ffn_mid_fwd(up, w_dw, b_dw, gather=None):
    t, f2 = up.shape
    f = f2 // 2
    tr = min(256, t)
    h = FFN_HALO

    def body(g_ref, gp_ref, v_ref, w_ref, b_ref, o_ref):
        first_tile = pl.program_id(0) == 0

        def strip(c, carry):
            col = pl.ds(pl.multiple_of(c * LANES, LANES), LANES)
            x = jnp.concatenate([jnp.where(first_tile, 0.0, gp_ref[:, col].astype(F32)), g_ref[:, col].astype(F32)], axis=0)
            gc = _conv3(x, w_ref, b_ref, col)[h:]
            o_ref[:, col] = (gc * jax.nn.sigmoid(gc) * v_ref[:, col].astype(F32)).astype(o_ref.dtype)
            return carry

        lax.fori_loop(0, f // LANES, strip, 0)

    kw = dict(name="ffn_mid_fwd", grid=(t // tr,),
              in_specs=[pl.BlockSpec((tr, f), lambda i: (i, 0)), pl.BlockSpec((h, f), _prev_halo(tr, h, 0)),
                        pl.BlockSpec((tr, f), lambda i: (i, 1)),
                        pl.BlockSpec((8, f), lambda i: (0, 0)), pl.BlockSpec((1, f), lambda i: (0, 0))],
              out_specs=pl.BlockSpec((tr, f), lambda i: (i, 0)), out_shape=SDS((t, f), BF))
    args = (up, up, up, w_dw, b_dw)
    if gather:
        (out,), gathered = _call_gathering(body, gather, args, **kw)
        return out, gathered
    return _call(body, compiler_params=_params("parallel"), **kw)(*args)


def _ffn_mid_bwd(up, da, w_dw, b_dw):
    t, f2 = up.shape
    f = f2 // 2
    tr = min(256, t)
    h = FFN_HALO
    n_tiles = t // tr

    def body(g_ref, gp_ref, gn_ref, v_ref, vn_ref, da_ref, dan_ref, w_ref, b_ref, dup_ref, dw_ref, db_ref):
        i = pl.program_id(0)
        last = i == n_tiles - 1
        n = tr + h

        @pl.when(i == 0)
        def _():
            dw_ref[...] = jnp.zeros_like(dw_ref)
            db_ref[...] = jnp.zeros_like(db_ref)

        def rows(tile_ref, next_ref, col):
            return jnp.concatenate([tile_ref[:, col].astype(F32), next_ref[:, col].astype(F32)], axis=0)

        def strip(c, carry):
            col = pl.ds(pl.multiple_of(c * LANES, LANES), LANES)
            x = jnp.concatenate([jnp.where(i == 0, 0.0, gp_ref[:, col].astype(F32)), rows(g_ref, gn_ref, col)], axis=0)
            x1, x2 = _shifted_back(x)
            w0, w1, w2 = (w_ref[pl.ds(k, 1), col] for k in range(FFN_K))
            gc = (b_ref[:, col] + w0 * x2 + w1 * x1 + w2 * x)[h:]
            dav = rows(da_ref, dan_ref, col)
            sg = jax.nn.sigmoid(gc)
            dup_ref[:, pl.ds(pl.multiple_of(f + c * LANES, LANES), LANES)] = (dav * gc * sg)[:tr].astype(dup_ref.dtype)
            dgc = dav * rows(v_ref, vn_ref, col) * (sg * (1.0 + gc * (1.0 - sg)))
            dgc = jnp.concatenate([dgc[:tr], jnp.where(last, 0.0, dgc[tr:])], axis=0)
            d1, d2 = pltpu.roll(dgc, n - 1, 0), pltpu.roll(dgc, n - 2, 0)
            dup_ref[:, col] = (w2 * dgc + w1 * d1 + w0 * d2)[:tr].astype(dup_ref.dtype)
            dgt = dgc[:tr]
            for k, past in enumerate((x2, x1, x)):
                dw_ref[pl.ds(k, 1), col] += jnp.sum(past[h:h + tr] * dgt, axis=0, keepdims=True)
            db_ref[:, col] += jnp.sum(dgt, axis=0, keepdims=True)
            return carry

        lax.fori_loop(0, f // LANES, strip, 0)

    tile = lambda col: pl.BlockSpec((tr, f), lambda i: (i, col))
    nxt = lambda col: pl.BlockSpec((h, f), _next_halo(tr, h, t, col))
    return _call(
        body, name="ffn_mid_bwd", grid=(n_tiles,),
        in_specs=[tile(0), pl.BlockSpec((h, f), _prev_halo(tr, h, 0)), nxt(0), tile(1), nxt(1), tile(0), nxt(0),
                  pl.BlockSpec((8, f), lambda i: (0, 0)), pl.BlockSpec((1, f), lambda i: (0, 0))],
        out_specs=[pl.BlockSpec((tr, f2), lambda i: (i, 0)), pl.BlockSpec((8, f), lambda i: (0, 0)),
                   pl.BlockSpec((1, f), lambda i: (0, 0))],
        out_shape=[SDS((t, f2), BF), SDS((8, f), F32), SDS((1, f), F32)],
        compiler_params=_params("arbitrary"))(up, up, up, up, up, da, da, w_dw, b_dw)


def _ln_silu(yc, g, b):
    mu = jnp.mean(yc, axis=-1, keepdims=True)
    xc = yc - mu
    y = xc * lax.rsqrt(jnp.mean(xc * xc, axis=-1, keepdims=True) + NORM_EPS)
    return jax.nn.silu(y * g + b)


SUBLANES = 8
CONV_PAD = 24
SHIFT_CHUNK = 40
TAP_ROWS = 64


def _glu(a, g):
    return a.astype(F32) * jax.nn.sigmoid(g.astype(F32))


def _glu_strip(ygs_ref, first_tile, a_ref, ap_ref, g_ref, gp_ref, col, h, tr):
    ygs_ref[pl.ds(0, h), :] = jnp.where(first_tile, 0.0, _glu(ap_ref[:, col], gp_ref[:, col]))
    ygs_ref[pl.ds(h, tr), :] = _glu(a_ref[:, col], g_ref[:, col])


def _shift_past(sh_ref, ygs_ref, h, n):
    first = h - CONV_PAD - SUBLANES
    for u0 in range(0, n + CONV_PAD, SHIFT_CHUNK):
        x = ygs_ref[pl.ds(first + u0, SHIFT_CHUNK + SUBLANES), :]
        for r in range(1, SUBLANES):
            sh_ref[r, pl.ds(u0, SHIFT_CHUNK), :] = pltpu.roll(x, r, 0)[SUBLANES:]


def _past_rows(sh_ref, ygs_ref, h, n, s, row0=0):
    a, r = divmod(s, SUBLANES)
    if r == 0:
        return ygs_ref[pl.ds(row0 + h - SUBLANES * a, n), :]
    return sh_ref[r, pl.ds(row0 + CONV_PAD - SUBLANES * a, n), :]


def _conf_mid_fwd(p1, w_dw, b_dw, ln_g, ln_b, gather=None):
    t, w2 = p1.shape
    w = w2 // 2
    tr = min(256, t)
    h = CONV_HALO
    rc = 32

    def body(a_ref, ap_ref, g_ref, gp_ref, w_ref, b_ref, lg_ref, lb_ref, o_ref, yc_ref, ygs_ref, sh_ref):
        first_tile = pl.program_id(0) == 0

        def strip(c, carry):
            col = pl.ds(pl.multiple_of(c * LANES, LANES), LANES)
            _glu_strip(ygs_ref, first_tile, a_ref, ap_ref, g_ref, gp_ref, col, h, tr)
            _shift_past(sh_ref, ygs_ref, h, tr)
            acc = jnp.broadcast_to(b_ref[:, col], (tr, LANES))
            for k in range(CONV_K):
                acc = acc + w_ref[pl.ds(k, 1), col] * _past_rows(sh_ref, ygs_ref, h, tr, CONV_K - 1 - k)
            yc_ref[:, col] = acc
            return carry

        lax.fori_loop(0, w // LANES, strip, 0)

        def rows(r, carry):
            rs = pl.ds(pl.multiple_of(r * rc, rc), rc)
            o_ref[rs, :] = _ln_silu(yc_ref[rs, :], lg_ref[...], lb_ref[...]).astype(o_ref.dtype)
            return carry

        lax.fori_loop(0, tr // rc, rows, 0)

    vec = pl.BlockSpec((1, w), lambda i: (0, 0))
    tile = pl.BlockSpec((tr, w), lambda i: (i, 0))
    kw = dict(name="conf_mid_fwd", grid=(t // tr,),
              in_specs=[tile, pl.BlockSpec((h, w), _prev_halo(tr, h, 0)),
                        pl.BlockSpec((tr, w), lambda i: (i, 1)), pl.BlockSpec((h, w), _prev_halo(tr, h, 1)),
                        pl.BlockSpec((32, w), lambda i: (0, 0)), vec, vec, vec],
              out_specs=[tile, tile], out_shape=[SDS((t, w), BF), SDS((t, w), F32)],
              scratch_shapes=[pltpu.VMEM((h + tr, LANES), F32), pltpu.VMEM((SUBLANES, tr + CONV_PAD, LANES), F32)])
    args = (p1, p1, p1, p1, w_dw, b_dw, ln_g, ln_b)
    if gather:
        return _call_gathering(body, gather, args, **kw)
    return _call(body, compiler_params=_params("parallel"), **kw)(*args)


def _conf_mid_bwd(p1, yc, dys, dy, w_dw, ln_g, ln_b):
    t, w2 = p1.shape
    w = w2 // 2
    tr = min(256, t)
    h = CONV_HALO
    rc = 32
    n_tiles = t // tr

    def body(a_ref, ap_ref, g_ref, gp_ref, yc_ref, ycn_ref, dys_ref, dysn_ref, dy_ref, w_ref, lg_ref, lb_ref,
             dp_ref, dw_ref, db_ref, dlg_ref, dlb_ref, db1_ref, db2_ref, dyc_ref, ygs_ref, sh_ref, shf_ref, dwacc_ref):
        i = pl.program_id(0)
        last = i == n_tiles - 1

        @pl.when(i == 0)
        def _():
            for ref in (dw_ref, db_ref, dlg_ref, dlb_ref, db1_ref, db2_ref):
                ref[...] = jnp.zeros_like(ref)

        def ln_rows(r, carry):
            rs = pl.ds(pl.multiple_of(r * rc, rc), rc)
            _, vjp = jax.vjp(_ln_silu, yc_ref[rs, :], lg_ref[...], lb_ref[...])
            dyc, dlg, dlb = vjp(dys_ref[rs, :].astype(F32))
            dyc_ref[rs, :] = dyc
            dlg_ref[...] += dlg
            dlb_ref[...] += dlb
            return carry

        lax.fori_loop(0, tr // rc, ln_rows, 0)
        _, vjp = jax.vjp(_ln_silu, ycn_ref[...], lg_ref[...], lb_ref[...])
        dyc_ref[pl.ds(tr, h), :] = jnp.where(last, 0.0, vjp(dysn_ref[...].astype(F32))[0])
        db2_ref[...] += jnp.sum(dy_ref[...], axis=0, keepdims=True)

        def back(c, carry):
            col = pl.ds(pl.multiple_of(c * LANES, LANES), LANES)
            gcol = pl.ds(pl.multiple_of(w + c * LANES, LANES), LANES)
            _glu_strip(ygs_ref, i == 0, a_ref, ap_ref, g_ref, gp_ref, col, h, tr)
            _shift_past(sh_ref, ygs_ref, h, tr)
            for u0 in range(0, tr + CONV_PAD, SHIFT_CHUNK):
                part = dyc_ref[pl.ds(u0, SHIFT_CHUNK + SUBLANES), col]
                for r in range(1, SUBLANES):
                    shf_ref[r, pl.ds(u0, SHIFT_CHUNK), :] = pltpu.roll(part, SHIFT_CHUNK + SUBLANES - r, 0)[:SHIFT_CHUNK]
            for r0 in range(0, tr, TAP_ROWS):
                rows = pl.ds(r0, TAP_ROWS)
                dyc = dyc_ref[rows, col]
                dyg = jnp.zeros((TAP_ROWS, LANES), F32)
                for k in range(CONV_K):
                    s = CONV_K - 1 - k
                    a, r = divmod(s, SUBLANES)
                    if r == 0:
                        future = dyc_ref[pl.ds(r0 + SUBLANES * a, TAP_ROWS), col]
                    else:
                        future = shf_ref[r, pl.ds(r0 + SUBLANES * a, TAP_ROWS), :]
                    dyg = dyg + w_ref[pl.ds(k, 1), col] * future
                    prod = _past_rows(sh_ref, ygs_ref, h, TAP_ROWS, s, r0) * dyc
                    part = prod[0:SUBLANES]
                    for q in range(1, TAP_ROWS // SUBLANES):
                        part = part + prod[q * SUBLANES:(q + 1) * SUBLANES]
                    if r0 == 0:
                        dwacc_ref[k] = part
                    else:
                        dwacc_ref[k] += part
                sg = jax.nn.sigmoid(g_ref[rows, col].astype(F32))
                da = dyg * sg
                dg = dyg * a_ref[rows, col].astype(F32) * sg * (1.0 - sg)
                dp_ref[rows, col] = da.astype(dp_ref.dtype)
                dp_ref[rows, gcol] = dg.astype(dp_ref.dtype)
                db_ref[:, col] += jnp.sum(dyc, axis=0, keepdims=True)
                db1_ref[:, col] += jnp.sum(da, axis=0, keepdims=True)
                db1_ref[:, gcol] += jnp.sum(dg, axis=0, keepdims=True)
            for k in range(CONV_K):
                dw_ref[pl.ds(k, 1), col] += jnp.sum(dwacc_ref[k], axis=0, keepdims=True)
            return carry

        lax.fori_loop(0, w // LANES, back, 0)

    tile = lambda col: pl.BlockSpec((tr, w), lambda i: (i, col))
    prv = lambda col: pl.BlockSpec((h, w), _prev_halo(tr, h, col))
    nxt = pl.BlockSpec((h, w), _next_halo(tr, h, t, 0))
    vec = pl.BlockSpec((1, w), lambda i: (0, 0))
    return _call(
        body, name="conf_mid_bwd", grid=(n_tiles,),
        in_specs=[tile(0), prv(0), tile(1), prv(1), tile(0), nxt, tile(0), nxt, tile(0),
                  pl.BlockSpec((32, w), lambda i: (0, 0)), vec, vec],
        out_specs=[pl.BlockSpec((tr, w2), lambda i: (i, 0)), pl.BlockSpec((32, w), lambda i: (0, 0)), vec, vec, vec,
                   pl.BlockSpec((1, w2), lambda i: (0, 0)), vec],
        out_shape=[SDS((t, w2), BF), SDS((32, w), F32), SDS((1, w), F32), SDS((1, w), F32), SDS((1, w), F32),
                   SDS((1, w2), F32), SDS((1, w), F32)],
        scratch_shapes=[pltpu.VMEM((tr + h, w), F32), pltpu.VMEM((h + tr, LANES), F32),
                        pltpu.VMEM((SUBLANES, tr + CONV_PAD, LANES), F32), pltpu.VMEM((SUBLANES, tr + CONV_PAD, LANES), F32),
                        pltpu.VMEM((32, SUBLANES, LANES), F32)],
        compiler_params=_params("arbitrary"))(p1, p1, p1, p1, yc, yc, dys, dys, dy, w_dw, ln_g, ln_b)


def _group_matrices():
    i = lax.broadcasted_iota(jnp.int32, (512, 512), 0)
    j = lax.broadcasted_iota(jnp.int32, (512, 512), 1)
    mean64 = jnp.where(i // HEAD_DIM == j // HEAD_DIM, 1.0 / HEAD_DIM, 0.0).astype(F32)
    fold64 = jnp.where(i % HEAD_DIM == j % HEAD_DIM, 1.0, 0.0).astype(F32)
    return mean64, fold64


def _split_dot(x, mat):
    hi = x.astype(BF)
    lo = (x - hi.astype(F32)).astype(BF)
    mb = mat.astype(BF)
    return jnp.dot(hi, mb, preferred_element_type=F32) + jnp.dot(lo, mb, preferred_element_type=F32)


@jax.custom_vjp
def _group_sum(x, mat):
    return _split_dot(x, mat)


_group_sum.defvjp(lambda x, mat: (_split_dot(x, mat), mat), lambda mat, ct: (_split_dot(ct, mat), jnp.zeros_like(mat)))


def _bf_dot_plain(a, b):
    return jnp.dot(a.astype(BF), b.astype(BF), preferred_element_type=F32)


@jax.custom_vjp
def _bf_dot(a, b):
    return _bf_dot_plain(a, b)


def _bf_dot_bwd(res, ct):
    a, b = res
    cb = ct.astype(BF)
    return (lax.dot_general(cb, b.astype(BF), NT_DIMS, preferred_element_type=F32),
            lax.dot_general(a.astype(BF), cb, TN_DIMS, preferred_element_type=F32))


_bf_dot.defvjp(lambda a, b: (_bf_dot_plain(a, b), (a, b)), _bf_dot_bwd)


def _prep_tile(proj, qg, kg, zg, ws, bexp, mean64, differentiated=False):
    sw = 512
    q, k, v, u, z = (proj[:, n * sw:(n + 1) * sw] for n in range(5))
    group_sum, dot = (_group_sum, _bf_dot) if differentiated else (_split_dot, _bf_dot_plain)

    def group_norm(x):
        return x * lax.rsqrt(group_sum(x * x, mean64) + NORM_EPS)

    qn = group_norm(q) * qg
    kn = group_norm(k) * kg
    zn = group_norm(_gelu(z)) * zg
    row = lax.broadcasted_iota(jnp.int32, (CHUNK, CHUNK), 0)
    col = lax.broadcasted_iota(jnp.int32, (CHUNK, CHUNK), 1)
    first = lax.broadcasted_iota(jnp.int32, (1, LANES), 1) < HEAD_DIM
    parts = []
    for pr in range(sw // LANES):
        zp = zn[:, pr * LANES:(pr + 1) * LANES]
        s0 = dot(jnp.where(col <= row, ws[2 * pr], 0.0), zp)
        s1 = dot(jnp.where(col <= row, ws[2 * pr + 1], 0.0), zp)
        parts.append(jnp.where(first, s0, s1))
    s = jnp.concatenate(parts, axis=1) + bexp
    return qn, kn, v, _gelu(u) * s


def _mix_prep_fwd(proj, qg, kg, zg, w_s, l, bexp, mean64, gather=None):
    t = proj.shape[0]
    tr = CHUNK

    def body(p_ref, qg_ref, kg_ref, zg_ref, ws_ref, be_ref, m_ref, qkv_ref, go_ref):
        qn, kn, v, go = _prep_tile(p_ref[...], qg_ref[...], kg_ref[...], zg_ref[...], ws_ref[...], be_ref[...], m_ref[...])
        qkv_ref[:, 0:512] = qn.astype(BF)
        qkv_ref[:, 512:1024] = kn.astype(BF)
        qkv_ref[:, 1024:1536] = v.astype(BF)
        go_ref[...] = go.astype(BF)

    vec = pl.BlockSpec((1, 512), lambda i: (0, 0))
    kw = dict(name="mix_prep_fwd", grid=(t // tr,),
              in_specs=[pl.BlockSpec((tr, 2560), lambda i: (i, 0)), vec, vec, vec,
                        pl.BlockSpec((None, 8, CHUNK, CHUNK), lambda i: (l, 0, 0, 0)),
                        pl.BlockSpec((CHUNK, 512), lambda i: (0, 0)), pl.BlockSpec((512, 512), lambda i: (0, 0))],
              out_specs=[pl.BlockSpec((tr, 1536), lambda i: (i, 0)), pl.BlockSpec((tr, 512), lambda i: (i, 0))],
              out_shape=[SDS((t, 1536), BF), SDS((t, 512), BF)])
    args = (proj, qg, kg, zg, w_s, bexp, mean64)
    if gather:
        return _call_gathering(body, gather, args, **kw)
    return _call(body, compiler_params=_params("parallel"), **kw)(*args)


def _mix_prep_bwd(proj, dq, dk, dv, dmix, qg, kg, zg, w_s, l, bexp, mean64, fold64):
    t = proj.shape[0]
    tr = CHUNK
    n_tiles = t // tr

    def body(p_ref, dq_ref, dk_ref, dv_ref, dgo_ref, qg_ref, kg_ref, zg_ref, ws_ref, be_ref, m_ref, f_ref,
             dp_ref, dqg_ref, dkg_ref, dzg_ref, dws_ref, dbe_ref):
        i = pl.program_id(0)

        @pl.when(i == 0)
        def _():
            for ref in (dqg_ref, dkg_ref, dzg_ref, dws_ref, dbe_ref):
                ref[...] = jnp.zeros_like(ref)

        fn = functools.partial(_prep_tile, mean64=m_ref[...], differentiated=True)
        _, vjp = jax.vjp(fn, p_ref[...], qg_ref[...], kg_ref[...], zg_ref[...], ws_ref[...], be_ref[...])
        dp, dqg, dkg, dzg, dws, dbe = vjp((dq_ref[...], dk_ref[...], dv_ref[...], dgo_ref[...]))
        dp_ref[...] = dp.astype(BF)
        dqg_ref[pl.ds(0, 1), :] += dqg
        dkg_ref[pl.ds(0, 1), :] += dkg
        dzg_ref[pl.ds(0, 1), :] += dzg
        dws_ref[...] += dws
        dbe_ref[...] += dbe

        @pl.when(i == n_tiles - 1)
        def _():
            dqg_ref[...] = jnp.dot(dqg_ref[...], f_ref[...], precision=HI, preferred_element_type=F32)
            dkg_ref[...] = jnp.dot(dkg_ref[...], f_ref[...], precision=HI, preferred_element_type=F32)
            dbe_ref[...] = jnp.dot(dbe_ref[...], m_ref[...] * float(HEAD_DIM), precision=HI, preferred_element_type=F32)

    vec = pl.BlockSpec((1, 512), lambda i: (0, 0))
    acc = pl.BlockSpec((8, 512), lambda i: (0, 0))
    sq = pl.BlockSpec((512, 512), lambda i: (0, 0))
    row = pl.BlockSpec((tr, 512), lambda i: (i, 0))
    return _call(
        body, name="mix_prep_bwd", grid=(n_tiles,),
        in_specs=[pl.BlockSpec((tr, 2560), lambda i: (i, 0)), row, row, row, pl.BlockSpec((tr, 512), lambda i: (i, 1)),
                  vec, vec, vec, pl.BlockSpec((None, 8, CHUNK, CHUNK), lambda i: (l, 0, 0, 0)),
                  pl.BlockSpec((CHUNK, 512), lambda i: (0, 0)), sq, sq],
        out_specs=[pl.BlockSpec((tr, 2560), lambda i: (i, 0)), acc, acc, acc,
                   pl.BlockSpec((8, CHUNK, CHUNK), lambda i: (0, 0, 0)), pl.BlockSpec((CHUNK, 512), lambda i: (0, 0))],
        out_shape=[SDS((t, 2560), BF), SDS((8, 512), F32), SDS((8, 512), F32), SDS((8, 512), F32),
                   SDS((8, CHUNK, CHUNK), F32), SDS((CHUNK, 512), F32)],
        compiler_params=_params("arbitrary"))(proj, dq, dk, dv, dmix, qg, kg, zg, w_s, bexp, mean64, fold64)


def _sb_logs(qh, kb, valid):
    z = lax.dot_general(qh, kb, NT_DIMS, preferred_element_type=F32) * (HEAD_DIM ** -0.5)
    soft = jnp.log1p(jnp.exp(-jnp.abs(z)))
    lk_raw = -(jnp.maximum(z, 0.0) + soft)
    ls = -(jnp.maximum(-z, 0.0) + soft)
    return lk_raw, ls, jnp.where(valid, lk_raw, 0.0)


def _sb_weights(ls, run, tail, valid):
    return jnp.where(valid, jnp.exp(ls + run + tail), 0.0)


def _att_masks(b):
    row = lax.broadcasted_iota(jnp.int32, (b, b), 0)
    col = lax.broadcasted_iota(jnp.int32, (b, b), 1)
    first = lax.broadcasted_iota(jnp.int32, (1, LANES), 1) < HEAD_DIM
    return row, col, first


N_PAIRS = 4


def _load_kv(qkv_hbm, k_scr, v_scr, sems, group, width):
    ck = pltpu.make_async_copy(qkv_hbm.at[:, pl.ds(pl.multiple_of(512 + group * width, LANES), width)], k_scr, sems.at[0])
    cv = pltpu.make_async_copy(qkv_hbm.at[:, pl.ds(pl.multiple_of(1024 + group * width, LANES), width)], v_scr, sems.at[1])
    ck.start()
    cv.start()
    ck.wait()
    cv.wait()


def _split_heads(ref, pair, first):
    x = ref[:, pair * LANES:(pair + 1) * LANES]
    zero = jnp.zeros_like(x)
    return jnp.where(first, x, zero), jnp.where(first, zero, x)


def _any_weight_left(run_ref, n_heads):
    top = run_ref[0]
    for hh in range(1, n_heads):
        top = jnp.maximum(top, run_ref[hh])
    return jnp.max(jnp.exp(top)) > 0.0


def _attn_fwd(qkv, pairs_per_step=4, gather=None):
    t = qkv.shape[0]
    b = ATT_BLOCK
    nq = t // b
    width = pairs_per_step * LANES
    n_heads = 2 * pairs_per_step

    def body(q_ref, qkv_hbm, ob_ref, o32_ref, k_scr, v_scr, acc_ref, run_ref, sems):
        group, qi = pl.program_id(0), pl.program_id(1)

        @pl.when(qi == 0)
        def _():
            _load_kv(qkv_hbm, k_scr, v_scr, sems, group, width)

        row, col, first = _att_masks(b)
        qh = [x for pr in range(pairs_per_step) for x in _split_heads(q_ref, pr, first)]
        upper = jnp.where(row > col, 1.0, 0.0).astype(BF)
        acc_ref[...] = jnp.zeros_like(acc_ref)
        run_ref[...] = jnp.zeros_like(run_ref)
        heads = range(n_heads)

        def step(carry):
            j, _ = carry
            rows = pl.ds(pl.multiple_of(j * b, b), b)
            valid = jnp.logical_or(j != qi, col < row)
            lanes = [pl.ds((hh // 2) * LANES, LANES) for hh in heads]
            logs = [_sb_logs(qh[hh], k_scr[rows, lanes[hh]], valid) for hh in heads]
            tails = [_split_dot(logs[hh][2], upper) for hh in heads]
            for hh in heads:
                wgt = _sb_weights(logs[hh][1], run_ref[hh], tails[hh], valid)
                acc_ref[hh] += jnp.dot(wgt.astype(BF), v_scr[rows, lanes[hh]], preferred_element_type=F32)
            for hh in heads:
                run_ref[hh] += jnp.sum(logs[hh][2], axis=1, keepdims=True)
            return j - 1, _any_weight_left(run_ref, n_heads)

        lax.while_loop(lambda c: jnp.logical_and(c[0] >= 0, c[1]), step, (qi, jnp.bool_(True)))
        for pr in range(pairs_per_step):
            out = jnp.where(first, acc_ref[2 * pr], acc_ref[2 * pr + 1])
            ob_ref[:, pr * LANES:(pr + 1) * LANES] = out.astype(BF)
            o32_ref[:, pr * LANES:(pr + 1) * LANES] = out

    blk = pl.BlockSpec((b, width), lambda g, qi: (qi, g))
    kw = dict(name="attn_fwd", grid=(N_PAIRS // pairs_per_step, nq),
              in_specs=[blk, pl.BlockSpec(memory_space=pl.ANY)], out_specs=[blk, blk],
              out_shape=[SDS((t, 512), BF), SDS((t, 512), F32)],
              scratch_shapes=[pltpu.VMEM((t, width), BF), pltpu.VMEM((t, width), BF),
                              pltpu.VMEM((n_heads, b, LANES), F32), pltpu.VMEM((n_heads, b, 1), F32),
                              pltpu.SemaphoreType.DMA((2,))])
    if gather:
        return _call_gathering(body, gather, (qkv, qkv), **kw)
    return _call(body, compiler_params=_params("arbitrary", "arbitrary"), **kw)(qkv, qkv)


def _attn_bwd(qkv, a32, dmix, pairs_per_step=2):
    t = qkv.shape[0]
    b = ATT_BLOCK
    nq = t // b
    width = pairs_per_step * LANES
    n_heads = 2 * pairs_per_step

    def body(q_ref, a_ref, da_ref, qkv_hbm, dq_ref, dk_hbm, dv_hbm,
             k_scr, v_scr, dk_scr, dv_scr, dqa_ref, run_ref, rung_ref, sems):
        group, qi = pl.program_id(0), pl.program_id(1)

        @pl.when(qi == 0)
        def _():
            _load_kv(qkv_hbm, k_scr, v_scr, sems, group, width)
            dk_scr[...] = jnp.zeros_like(dk_scr)
            dv_scr[...] = jnp.zeros_like(dv_scr)

        row, col, first = _att_masks(b)
        qh, dah, dtot = [], [], []
        for pr in range(pairs_per_step):
            qh += _split_heads(q_ref, pr, first)
            da = da_ref[:, pr * LANES:(pr + 1) * LANES]
            prod = da * a_ref[:, pr * LANES:(pr + 1) * LANES]
            dtot += [jnp.sum(jnp.where(first, prod, 0.0), axis=1, keepdims=True),
                     jnp.sum(jnp.where(first, 0.0, prod), axis=1, keepdims=True)]
            dah += [jnp.where(first, da, 0.0).astype(BF), jnp.where(first, 0.0, da).astype(BF)]
        upper = jnp.where(row > col, 1.0, 0.0).astype(BF)
        lower_incl = jnp.where(row >= col, 1.0, 0.0).astype(BF)
        dqa_ref[...] = jnp.zeros_like(dqa_ref)
        run_ref[...] = jnp.zeros_like(run_ref)
        rung_ref[...] = jnp.zeros_like(rung_ref)
        heads = range(n_heads)

        def step(carry):
            j, _ = carry
            rows = pl.ds(pl.multiple_of(j * b, b), b)
            valid = jnp.logical_or(j != qi, col < row)
            lanes = [pl.ds((hh // 2) * LANES, LANES) for hh in heads]
            logs = [_sb_logs(qh[hh], k_scr[rows, lanes[hh]], valid) for hh in heads]
            dps = [lax.dot_general(dah[hh], v_scr[rows, lanes[hh]], NT_DIMS, preferred_element_type=F32) for hh in heads]
            tails = [_split_dot(logs[hh][2], upper) for hh in heads]
            wgts = [_sb_weights(logs[hh][1], run_ref[hh], tails[hh], valid) for hh in heads]
            gs = [wgts[hh] * dps[hh] for hh in heads]
            g_froms = [_split_dot(gs[hh], lower_incl) for hh in heads]
            for hh in heads:
                lk_raw, ls, _ = logs[hh]
                dlk = jnp.where(valid, dtot[hh] - rung_ref[hh] - g_froms[hh], 0.0)
                dz = ((gs[hh] * jnp.exp(lk_raw) - dlk * jnp.exp(ls)) * (HEAD_DIM ** -0.5)).astype(BF)
                dqa_ref[hh] += jnp.dot(dz, k_scr[rows, lanes[hh]], preferred_element_type=F32)
                dk_scr[rows, lanes[hh]] += lax.dot_general(dz, qh[hh], TN_DIMS, preferred_element_type=F32)
                dv_scr[rows, lanes[hh]] += lax.dot_general(wgts[hh].astype(BF), dah[hh], TN_DIMS, preferred_element_type=F32)
            for hh in heads:
                rung_ref[hh] += jnp.sum(gs[hh], axis=1, keepdims=True)
                run_ref[hh] += jnp.sum(logs[hh][2], axis=1, keepdims=True)
            return j - 1, _any_weight_left(run_ref, n_heads)

        lax.while_loop(lambda c: jnp.logical_and(c[0] >= 0, c[1]), step, (qi, jnp.bool_(True)))
        for pr in range(pairs_per_step):
            dq_ref[:, pr * LANES:(pr + 1) * LANES] = jnp.where(first, dqa_ref[2 * pr], dqa_ref[2 * pr + 1])

        @pl.when(qi == nq - 1)
        def _():
            cols = pl.ds(pl.multiple_of(group * width, LANES), width)
            ck = pltpu.make_async_copy(dk_scr, dk_hbm.at[:, cols], sems.at[0])
            cv = pltpu.make_async_copy(dv_scr, dv_hbm.at[:, cols], sems.at[1])
            ck.start()
            cv.start()
            ck.wait()
            cv.wait()

    blk = pl.BlockSpec((b, width), lambda g, qi: (qi, g))
    anywhere = pl.BlockSpec(memory_space=pl.ANY)
    return _call(
        body, name="attn_bwd", grid=(N_PAIRS // pairs_per_step, nq),
        in_specs=[blk, blk, blk, anywhere], out_specs=[blk, anywhere, anywhere],
        out_shape=[SDS((t, 512), F32), SDS((t, 512), F32), SDS((t, 512), F32)],
        scratch_shapes=[pltpu.VMEM((t, width), BF), pltpu.VMEM((t, width), BF),
                        pltpu.VMEM((t, width), F32), pltpu.VMEM((t, width), F32),
                        pltpu.VMEM((n_heads, b, LANES), F32), pltpu.VMEM((n_heads, b, 1), F32),
                        pltpu.VMEM((n_heads, b, 1), F32), pltpu.SemaphoreType.DMA((2,))],
        compiler_params=_params("arbitrary", "arbitrary"))(qkv, a32, dmix, qkv)


def _adamw(w, g, m, v):
    n, c = w.shape
    tr = min(256, n)
    assert n % tr == 0

    def body(w_ref, g_ref, m_ref, v_ref, d_ref, nm_ref, nv_ref):
        g = g_ref[...]
        m = ADAM_B1 * m_ref[...] + (1.0 - ADAM_B1) * g
        v = ADAM_B2 * v_ref[...] + (1.0 - ADAM_B2) * jnp.square(g)
        m_hat = m / (1.0 - ADAM_B1 ** ADAM_STEP)
        v_hat = v / (1.0 - ADAM_B2 ** ADAM_STEP)
        d_ref[...] = -ADAM_LR * (m_hat / (jnp.sqrt(v_hat) + ADAM_EPS) + ADAM_WD * w_ref[...])
        nm_ref[...] = m
        nv_ref[...] = v

    blk = pl.BlockSpec((tr, c), lambda i: (i, 0))
    return _call(
        body, name="adamw", grid=(n // tr,), in_specs=[blk] * 4, out_specs=[blk] * 3,
        out_shape=[SDS((n, c), F32)] * 3, compiler_params=_params("parallel"))(w, g, m, v)


def _mesh_pos():
    return lax.axis_index("x"), lax.axis_index("y"), lax.axis_index("c")


def _other_chips(x, y):
    return [(1 - x, y), (x, 1 - y), (1 - x, 1 - y)]


HBM_SPEC = pl.BlockSpec(memory_space=pltpu.HBM)


GATHER_COPIES = 6


def _gather_steps(s_ref, o_ref, send_sems, recv_sems, local_sems, slot):
    h = s_ref.shape[1] // 2
    x, y, c = _mesh_pos()
    sibling = (x, y, 1 - c)
    chips = _other_chips(x, y)
    base = GATHER_COPIES * slot

    def half(px, py, hc):
        return o_ref.at[:, 2 * px + py, pl.ds(hc * h, h), :]

    def copy(k, dst, to, src=None):
        return pltpu.make_async_remote_copy(
            src_ref=dst if src is None else src, dst_ref=dst, send_sem=send_sems.at[base + k],
            recv_sem=recv_sems.at[base + k], device_id=to, device_id_type=MESH)

    mine = pltpu.make_async_copy(s_ref, o_ref.at[:, 2 * x + y], local_sems.at[slot])
    first = [copy(j, half(x, y, c), (*chip, c), src=s_ref.at[:, pl.ds(c * h, h), :]) for j, chip in enumerate(chips)]
    passed = [copy(3 + j, half(*chip, c), sibling) for j, chip in enumerate(chips)]

    def start():
        mine.start()
        for cp in first:
            cp.start()

    def finish():
        for j, chip in enumerate(chips):
            copy(j, half(*chip, c), (x, y, c)).wait_recv()
            passed[j].start()
        for j, chip in enumerate(chips):
            copy(3 + j, half(*chip, 1 - c), (x, y, c)).wait_recv()
        for cp in first + passed:
            cp.wait_send()
        mine.wait()

    return start, finish


def _gather_scratch(n):
    return [pltpu.SemaphoreType.DMA((GATHER_COPIES * n,)), pltpu.SemaphoreType.DMA((GATHER_COPIES * n,)),
            pltpu.SemaphoreType.DMA((n,))]


def _gathered_shape(shard):
    n_l, r, c_w = shard.shape
    return SDS((n_l, N_CHIPS, r, c_w), shard.dtype)


def _all_gather(shard):
    def body(s_ref, o_ref, send_sems, recv_sems, local_sems):
        start, finish = _gather_steps(s_ref, o_ref, send_sems, recv_sems, local_sems, 0)
        start()
        finish()

    return _call(body, name="all_gather", in_specs=[HBM_SPEC], out_specs=HBM_SPEC, out_shape=_gathered_shape(shard),
                 scratch_shapes=_gather_scratch(1))(shard)


def _call_gathering(body, shards, args, *, name, grid, in_specs, out_specs, out_shape, scratch_shapes=()):
    out_specs = list(out_specs) if isinstance(out_specs, (list, tuple)) else [out_specs]
    out_shape = list(out_shape) if isinstance(out_shape, (list, tuple)) else [out_shape]
    n_in, n_out, n_sh, n_scr = len(in_specs), len(out_specs), len(shards), len(scratch_shapes)

    def hosting_body(*refs):
        ins, s_refs = refs[:n_in], refs[n_in:n_in + n_sh]
        outs = refs[n_in + n_sh:n_in + n_sh + n_out]
        o_refs = refs[n_in + n_sh + n_out:n_in + 2 * n_sh + n_out]
        scratch = refs[n_in + 2 * n_sh + n_out:n_in + 2 * n_sh + n_out + n_scr]
        send_sems, recv_sems, local_sems = refs[-3:]
        steps = [_gather_steps(s_refs[k], o_refs[k], send_sems, recv_sems, local_sems, k) for k in range(n_sh)]
        is_first = functools.reduce(jnp.logical_and, [pl.program_id(a) == 0 for a in range(len(grid))])
        is_last = functools.reduce(jnp.logical_and, [pl.program_id(a) == grid[a] - 1 for a in range(len(grid))])

        @pl.when(is_first)
        def _():
            for start, _ in steps:
                start()

        body(*ins, *outs, *scratch)

        @pl.when(is_last)
        def _():
            for _, finish in steps:
                finish()

    res = _call(
        hosting_body, name=name + "_gathering", grid=grid, in_specs=list(in_specs) + [HBM_SPEC] * n_sh,
        out_specs=out_specs + [HBM_SPEC] * n_sh, out_shape=out_shape + [_gathered_shape(s) for s in shards],
        scratch_shapes=list(scratch_shapes) + _gather_scratch(n_sh),
        compiler_params=_params(*(["arbitrary"] * len(grid))))(*args, *shards)
    return res[:n_out], res[n_out:]


def _row_tile(h):
    for cand in (256, 176, 128, 64, 32, 16):
        if h % cand == 0:
            return cand
    raise ValueError(h)


def _reduce_scatter(g, mid_dtype):
    n_l, n_p, r, c_w = g.shape
    h = r // 2
    tr = _row_tile(h)
    nt = h // tr
    x, y, c = _mesh_pos()
    c_arr = jnp.reshape(c, (1,)).astype(jnp.int32)
    p_arr = jnp.reshape(2 * x + y, (1,)).astype(jnp.int32)

    def to_sibling_body(g_ref, a_ref, send_sem, recv_sem):
        x, y, c = _mesh_pos()
        cp = pltpu.make_async_remote_copy(
            src_ref=g_ref.at[:, :, pl.ds((1 - c) * h, h), :], dst_ref=a_ref, send_sem=send_sem, recv_sem=recv_sem,
            device_id=(x, y, 1 - c), device_id_type=MESH)
        cp.start()
        cp.wait()

    from_sibling = _call(
        to_sibling_body, name="rs_pair", in_specs=[HBM_SPEC], out_specs=HBM_SPEC,
        out_shape=SDS((n_l, n_p, h, c_w), g.dtype),
        scratch_shapes=[pltpu.SemaphoreType.DMA, pltpu.SemaphoreType.DMA],
        )(g)

    def pair_add_body(c_ref, g_ref, a_ref, o_ref):
        o_ref[...] = (g_ref[...].astype(F32) + a_ref[...].astype(F32)).astype(o_ref.dtype)

    blk = (None, None, tr, c_w)
    pair_sum = _call(
        pair_add_body, name="rs_pair_add",
        grid_spec=pltpu.PrefetchScalarGridSpec(
            num_scalar_prefetch=1, grid=(n_l, n_p, nt),
            in_specs=[pl.BlockSpec(blk, lambda l, p, t, c_ref: (l, p, c_ref[0] * nt + t, 0)),
                      pl.BlockSpec(blk, lambda l, p, t, c_ref: (l, p, t, 0))],
            out_specs=pl.BlockSpec(blk, lambda l, p, t, c_ref: (l, p, t, 0))),
        out_shape=SDS((n_l, n_p, h, c_w), mid_dtype),
        compiler_params=_params("parallel", "parallel", "parallel"))(c_arr, g, from_sibling)

    def to_chips_body(s_ref, b_ref, send_sems, recv_sems):
        x, y, c = _mesh_pos()
        cps = [pltpu.make_async_remote_copy(
            src_ref=s_ref.at[:, 2 * chip[0] + chip[1]], dst_ref=b_ref.at[j], send_sem=send_sems.at[j],
            recv_sem=recv_sems.at[j], device_id=(*chip, c), device_id_type=MESH)
            for j, chip in enumerate(_other_chips(x, y))]
        for cp in cps:
            cp.start()
        for cp in cps:
            cp.wait()

    from_chips = _call(
        to_chips_body, name="rs_chips", in_specs=[HBM_SPEC], out_specs=HBM_SPEC,
        out_shape=SDS((3, n_l, h, c_w), mid_dtype),
        scratch_shapes=[pltpu.SemaphoreType.DMA((3,)), pltpu.SemaphoreType.DMA((3,))],
        )(pair_sum)

    def chip_add_body(p_ref, c_ref, s_ref, b_ref, o_ref):
        acc = s_ref[...].astype(F32)
        for j in range(3):
            acc = acc + b_ref[j].astype(F32)
        o_ref[...] = acc

    half_sum = _call(
        chip_add_body, name="rs_chip_add",
        grid_spec=pltpu.PrefetchScalarGridSpec(
            num_scalar_prefetch=2, grid=(n_l, nt),
            in_specs=[pl.BlockSpec((None, None, tr, c_w), lambda l, t, p_ref, c_ref: (l, p_ref[0], t, 0)),
                      pl.BlockSpec((3, None, tr, c_w), lambda l, t, p_ref, c_ref: (0, l, t, 0))],
            out_specs=pl.BlockSpec((None, tr, c_w), lambda l, t, p_ref, c_ref: (l, c_ref[0] * nt + t, 0))),
        out_shape=SDS((n_l, r, c_w), F32),
        compiler_params=_params("parallel", "parallel"))(p_arr, c_arr, pair_sum, from_chips)

    def swap_body(i_ref, o_ref, send_sem, recv_sem):
        x, y, c = _mesh_pos()
        mine = o_ref.at[:, pl.ds(c * h, h), :]
        theirs = o_ref.at[:, pl.ds((1 - c) * h, h), :]
        pltpu.make_async_remote_copy(src_ref=mine, dst_ref=mine, send_sem=send_sem, recv_sem=recv_sem,
                                     device_id=(x, y, 1 - c), device_id_type=MESH).start()
        wait = pltpu.make_async_remote_copy(src_ref=mine, dst_ref=theirs, send_sem=send_sem, recv_sem=recv_sem,
                                            device_id=(x, y, 1 - c), device_id_type=MESH)
        wait.wait_send()
        wait.wait_recv()

    return _call(
        swap_body, name="rs_swap", in_specs=[HBM_SPEC], out_specs=HBM_SPEC, out_shape=SDS((n_l, r, c_w), F32),
        input_output_aliases={0: 0},
        scratch_shapes=[pltpu.SemaphoreType.DMA, pltpu.SemaphoreType.DMA],
        )(half_sum)


def _pack(arrays, row_multiple):
    flat = jnp.concatenate([a.reshape(-1).astype(F32) for a in arrays])
    unit = row_multiple * LANES
    padded = -(-flat.shape[0] // unit) * unit
    return jnp.pad(flat, (0, padded - flat.shape[0])).reshape(padded // LANES, LANES)


def _unpack(packed, shapes):
    flat = packed.reshape(-1)
    out, pos = [], 0
    for s in shapes:
        size = 1
        for dim in s:
            size *= dim
        out.append(flat[pos:pos + size].reshape(s))
        pos += size
    return out


BIG_COL = ("sb_w_in", "cv_w_pw1", "ffn_w_up")
BIG_ROW = ("hyb_w_out", "cv_w_pw2", "ffn_w_down")
SMALL_SHARDED = ("cv_b_pw1", "cv_w_dw", "cv_b_dw", "cv_ln_g", "cv_ln_b", "cv_b_pw2", "ffn_w_dw")
SMALL_REPLICATED = ("mix_norm_g", "sb_q_norm_g", "sb_k_norm_g", "sg_z_norm_g", "sg_w_spatial", "sg_b_spatial",
                    "ffn_norm_g", "ffn_b_dw")
WEIGHTS = ("mix_norm_g", "sb_w_in", "sb_q_norm_g", "sb_k_norm_g", "sg_z_norm_g", "sg_w_spatial", "sg_b_spatial",
           "hyb_w_out", "cv_w_pw1", "cv_b_pw1", "cv_w_dw", "cv_b_dw", "cv_ln_g", "cv_ln_b", "cv_w_pw2", "cv_b_pw2",
           "ffn_norm_g", "ffn_w_up", "ffn_w_dw", "ffn_b_dw", "ffn_w_down")


def _pad_rows(a, rows):
    return jnp.pad(a, ((0, rows - a.shape[0]), (0, 0)))


def _step(x, tgt, w, m, v):
    n_layers = w["mix_norm_g"].shape[0]
    xi, yi, ci = _mesh_pos()
    chip = 2 * xi + yi

    assert n_layers == 4
    hosted_by = {("proj", 0): ["hyb_w_out"], ("prep", 0): [("ffn_w_up", 0)],
                 ("attn", 0): [("ffn_w_down", 0), "cv_w_pw1", "cv_w_pw2"],
                 ("up", 0): [("ffn_w_up", 1)], ("ffn_mid", 0): [("ffn_w_down", 1)],
                 ("conf_mid", 1): [("ffn_w_up", 2), ("ffn_w_down", 2)],
                 ("up", 1): [("ffn_w_up", 3)], ("ffn_mid", 1): [("ffn_w_down", 3)]}
    full = {}

    def shard_of(key):
        if isinstance(key, tuple):
            return w[key[0]][key[1]:key[1] + 1].astype(BF)
        return w[key].astype(BF)

    def keep(key, g4):
        if (key[0] if isinstance(key, tuple) else key) in BIG_ROW:
            g4 = g4.reshape(g4.shape[0], 1, g4.shape[1] * g4.shape[2], g4.shape[3])
        full[key] = g4

    def hosting(fn, point, *args, **kw):
        keys = hosted_by.get(point)
        if not keys:
            return fn(*args, **kw)
        out, gathered = fn(*args, gather=[shard_of(k) for k in keys], **kw)
        for key, g4 in zip(keys, gathered):
            keep(key, g4)
        return out

    keep("sb_w_in", _all_gather(shard_of("sb_w_in")))
    small_local = [w[name] for name in SMALL_SHARDED]
    gathered = _all_gather(_pack(small_local, 32)[None])[0]
    per_chip = [_unpack(gathered[p], [a.shape for a in small_local]) for p in range(N_CHIPS)]
    for k, name in enumerate(SMALL_SHARDED):
        full[name] = jnp.concatenate([per_chip[p][k] for p in range(N_CHIPS)], axis=-1)
    for name in SMALL_REPLICATED:
        full[name] = w[name]

    mean64, fold64 = _group_matrices()
    ffn_wdw = [_pad_rows(full["ffn_w_dw"][i], 8) for i in range(n_layers)]
    cv_wdw = [_pad_rows(full["cv_w_dw"][j], 32) for j in range(n_layers // 2)]
    row = lambda a: a.reshape(1, -1)

    saved = []
    cur = x
    h = _rms_fwd(cur, row(full["mix_norm_g"][0]))
    for i in range(n_layers):
        j = i // 2
        rec = {"x_in": cur, "h_mix": h}
        if i % 2 == 0:
            proj = hosting(_mm_nn, ("proj", i), h, full["sb_w_in"], j)
            qg = row(jnp.tile(full["sb_q_norm_g"][j], 512 // HEAD_DIM))
            kg = row(jnp.tile(full["sb_k_norm_g"][j], 512 // HEAD_DIM))
            zg = row(full["sg_z_norm_g"][j])
            bexp = jnp.repeat(full["sg_b_spatial"][j].T, HEAD_DIM, axis=1)
            qkv, gated = hosting(_mix_prep_fwd, ("prep", i), proj, qg, kg, zg, full["sg_w_spatial"], j, bexp, mean64)
            att_bf, att_32 = hosting(_attn_fwd, ("attn", i), qkv)
            mix = jnp.concatenate([att_bf, gated], axis=1)
            cur, h = _mm_nn(mix, full["hyb_w_out"], j, resid=cur, norm_g=row(full["ffn_norm_g"][i]))
            rec.update(proj=proj, qkv=qkv, att_32=att_32, mix=mix, qg=qg, kg=kg, zg=zg, bexp=bexp)
        else:
            p1 = _mm_nn(h, full["cv_w_pw1"], j, bias=row(full["cv_b_pw1"][j]), out_dtype=BF)
            ys, yc = hosting(_conf_mid_fwd, ("conf_mid", i), p1, cv_wdw[j], row(full["cv_b_dw"][j]),
                             row(full["cv_ln_g"][j]), row(full["cv_ln_b"][j]))
            cur, h = _mm_nn(ys, full["cv_w_pw2"], j, bias=row(full["cv_b_pw2"][j]), resid=cur,
                            norm_g=row(full["ffn_norm_g"][i]))
            rec.update(p1=p1, ys=ys, yc=yc)
        rec["x_mid"] = cur
        up = hosting(_mm_nn, ("up", i), h, full[("ffn_w_up", i)], 0, out_dtype=BF)
        act = hosting(_ffn_mid_fwd, ("ffn_mid", i), up, ffn_wdw[i], row(full["ffn_b_dw"][i]))
        rec.update(h_ffn=h, up=up, act=act)
        if i + 1 < n_layers:
            cur, h = _mm_nn(act, full[("ffn_w_down", i)], 0, resid=cur, norm_g=row(full["mix_norm_g"][i + 1]))
        else:
            cur = _mm_nn(act, full[("ffn_w_down", i)], 0, resid=cur)
        saved.append(rec)

    loss_vec, dy, dy_bf = _loss_grad(cur, tgt)
    loss = lax.psum(loss_vec[0, 0], ("x", "y", "c"))

    gbig = {name: None for name in BIG_COL + BIG_ROW}
    gsmall = {name: [None] * w[name].shape[0] for name in SMALL_SHARDED + SMALL_REPLICATED}
    n_of = {name: w[name].shape[0] for name in BIG_COL + BIG_ROW}
    for i in reversed(range(n_layers)):
        j = i // 2
        rec = saved[i]
        dact = _mm_nt(dy_bf, full[("ffn_w_down", i)], 0, out_dtype=BF)
        gbig["ffn_w_down"] = _mm_tn(rec["act"], dy_bf, 1, n_of["ffn_w_down"], i, gbig["ffn_w_down"])
        dup, dwdw, dbdw = _ffn_mid_bwd(rec["up"], dact, ffn_wdw[i], row(full["ffn_b_dw"][i]))
        gsmall["ffn_w_dw"][i] = dwdw[:FFN_K]
        gsmall["ffn_b_dw"][i] = dbdw[0]
        gbig["ffn_w_up"] = _mm_tn(rec["h_ffn"], dup, N_CHIPS, n_of["ffn_w_up"], i, gbig["ffn_w_up"])
        dy, dy_bf, dg = _mm_nt_rms_bwd(dup, full[("ffn_w_up", i)], 0, rec["x_mid"], row(full["ffn_norm_g"][i]), dy)
        gsmall["ffn_norm_g"][i] = dg[0]
        if i % 2 == 0:
            dmix = _mm_nt(dy_bf, full["hyb_w_out"], j)
            gbig["hyb_w_out"] = _mm_tn(rec["mix"], dy_bf, 1, n_of["hyb_w_out"], j, gbig["hyb_w_out"])
            dq, dk, dv = _attn_bwd(rec["qkv"], rec["att_32"], dmix)
            dproj, dqg, dkg, dzg, dws, dbe = _mix_prep_bwd(
                rec["proj"], dq, dk, dv, dmix, rec["qg"], rec["kg"], rec["zg"], full["sg_w_spatial"], j, rec["bexp"],
                mean64, fold64)
            gsmall["sb_q_norm_g"][j] = dqg[0, :HEAD_DIM]
            gsmall["sb_k_norm_g"][j] = dkg[0, :HEAD_DIM]
            gsmall["sg_z_norm_g"][j] = dzg[0]
            gsmall["sg_w_spatial"][j] = dws
            gsmall["sg_b_spatial"][j] = dbe[:, ::HEAD_DIM].T
            dlast, w_first = dproj, full["sb_w_in"]
            gbig["sb_w_in"] = _mm_tn(rec["h_mix"], dproj, N_CHIPS, n_of["sb_w_in"], j, gbig["sb_w_in"])
        else:
            dys = _mm_nt(dy_bf, full["cv_w_pw2"], j, out_dtype=BF)
            gbig["cv_w_pw2"] = _mm_tn(rec["ys"], dy_bf, 1, n_of["cv_w_pw2"], j, gbig["cv_w_pw2"])
            dp1, dwdw, dbdw, dlg, dlb, db1, db2 = _conf_mid_bwd(
                rec["p1"], rec["yc"], dys, dy, cv_wdw[j], row(full["cv_ln_g"][j]), row(full["cv_ln_b"][j]))
            gsmall["cv_w_dw"][j] = dwdw[:CONV_K]
            gsmall["cv_b_dw"][j] = dbdw[0]
            gsmall["cv_ln_g"][j] = dlg[0]
            gsmall["cv_ln_b"][j] = dlb[0]
            gsmall["cv_b_pw1"][j] = db1[0]
            gsmall["cv_b_pw2"][j] = db2[0]
            dlast, w_first = dp1, full["cv_w_pw1"]
            gbig["cv_w_pw1"] = _mm_tn(rec["h_mix"], dp1, N_CHIPS, n_of["cv_w_pw1"], j, gbig["cv_w_pw1"])
        dy, dy_bf, dg = _mm_nt_rms_bwd(dlast, w_first, j, rec["x_in"], row(full["mix_norm_g"][i]), dy)
        gsmall["mix_norm_g"][i] = dg[0]

    grads = {}
    for name in BIG_COL:
        grads[name] = _reduce_scatter(gbig[name], BF)
    for name in BIG_ROW:
        g4 = gbig[name]
        r = g4.shape[2] // N_CHIPS
        grads[name] = _reduce_scatter(g4.reshape(g4.shape[0], N_CHIPS, r, g4.shape[3]), BF)
    small_names = SMALL_REPLICATED + SMALL_SHARDED
    small_full = [jnp.stack(gsmall[name]) for name in small_names]
    packed = _pack(small_full, 32 * N_CHIPS)
    rows_q = packed.shape[0] // N_CHIPS
    summed = _reduce_scatter(packed.reshape(1, N_CHIPS, rows_q, LANES), F32)
    summed = _all_gather(summed).reshape(-1, LANES)
    for name, gsum in zip(small_names, _unpack(summed, [a.shape for a in small_full])):
        if name in SMALL_SHARDED:
            n_loc = w[name].shape[-1]
            split = gsum.reshape(gsum.shape[:-1] + (N_CHIPS, n_loc))
            gsum = lax.dynamic_index_in_dim(split, chip, axis=split.ndim - 2, keepdims=False)
        grads[name] = gsum

    delta, new_m, new_v = {}, {}, {}
    for name in BIG_COL + BIG_ROW:
        shp = w[name].shape
        two_d = lambda a: a.reshape(shp[0] * shp[1], shp[2])
        d, nm, nv = _adamw(two_d(w[name]), two_d(grads[name]), two_d(m[name]), two_d(v[name]))
        delta[name], new_m[name], new_v[name] = d.reshape(shp), nm.reshape(shp), nv.reshape(shp)
    shapes = [w[name].shape for name in small_names]
    d, nm, nv = _adamw(*(_pack([src[name] for name in small_names], 256) for src in (w, grads, m, v)))
    for name, a, b_, c_ in zip(small_names, _unpack(d, shapes), _unpack(nm, shapes), _unpack(nv, shapes)):
        delta[name], new_m[name], new_v[name] = a, b_, c_

    return (loss, dy, *[grads[n] for n in WEIGHTS], *[delta[n] for n in WEIGHTS],
            *[new_m[n] for n in WEIGHTS], *[new_v[n] for n in WEIGHTS])


def kernel(x, mix_norm_g, sb_w_in, sb_q_norm_g, sb_k_norm_g, sg_z_norm_g, sg_w_spatial, sg_b_spatial, hyb_w_out, cv_w_pw1, cv_b_pw1, cv_w_dw, cv_b_dw, cv_ln_g, cv_ln_b, cv_w_pw2, cv_b_pw2, ffn_norm_g, ffn_w_up, ffn_w_dw, ffn_b_dw, ffn_w_down, loss_target, m_mix_norm_g, m_sb_w_in, m_sb_q_norm_g, m_sb_k_norm_g, m_sg_z_norm_g, m_sg_w_spatial, m_sg_b_spatial, m_hyb_w_out, m_cv_w_pw1, m_cv_b_pw1, m_cv_w_dw, m_cv_b_dw, m_cv_ln_g, m_cv_ln_b, m_cv_w_pw2, m_cv_b_pw2, m_ffn_norm_g, m_ffn_w_up, m_ffn_w_dw, m_ffn_b_dw, m_ffn_w_down, v_mix_norm_g, v_sb_w_in, v_sb_q_norm_g, v_sb_k_norm_g, v_sg_z_norm_g, v_sg_w_spatial, v_sg_b_spatial, v_hyb_w_out, v_cv_w_pw1, v_cv_b_pw1, v_cv_w_dw, v_cv_b_dw, v_cv_ln_g, v_cv_ln_b, v_cv_w_pw2, v_cv_b_pw2, v_ffn_norm_g, v_ffn_w_up, v_ffn_w_dw, v_ffn_b_dw, v_ffn_w_down):
    given = dict(locals())
    w = {n: given[n] for n in WEIGHTS}
    m = {n: given["m_" + n] for n in WEIGHTS}
    v = {n: given["v_" + n] for n in WEIGHTS}
    out = _step(x[0], loss_target[0], w, m, v)
    return (out[0], out[1][None], *out[2:])
```

```python
import functools
from typing import Callable, NamedTuple

import jax
import jax.numpy as jnp
from jax import lax
from jax.experimental import pallas as pl
from jax.experimental.pallas import tpu as pltpu

F32 = jnp.float32
BF = jnp.bfloat16
SDS = jax.ShapeDtypeStruct
HI = lax.Precision.HIGHEST
MESH = pl.DeviceIdType.MESH

NORM_EPS = 1e-6
HEAD_DIM = 64
ATT_BLOCK = 128
CHUNK = 128
CONV_K = 31
CONV_HALO = 32
FFN_K = 3
FFN_HALO = 16
LANES = 128
N_CHIPS = 4
VMEM_LIMIT_BYTES = 56 * 2**20

ADAM_LR = 0.001
ADAM_B1 = 0.9
ADAM_B2 = 0.999
ADAM_EPS = 1e-08
ADAM_WD = 0.01
ADAM_STEP = 10

NT_DIMS = (((1,), (1,)), ((), ()))
TN_DIMS = (((0,), (0,)), ((), ()))


def _call(body, **kw):
    return pl.pallas_call(body, **kw)


def _params(*sem):
    return pltpu.CompilerParams(dimension_semantics=sem, vmem_limit_bytes=VMEM_LIMIT_BYTES)


def _gelu(x):
    return 0.5 * x * (1.0 + lax.erf(x * 0.7071067811865476))


def _rms(x, g):
    y = x * lax.rsqrt(jnp.mean(x * x, axis=-1, keepdims=True) + NORM_EPS)
    return y * g


def _rms_fwd(x, g):
    t, d = x.shape
    tm = min(512, t)

    def body(x_ref, g_ref, o_ref):
        o_ref[...] = _rms(x_ref[...], g_ref[...]).astype(o_ref.dtype)

    return _call(
        body, name="rms_fwd", grid=(t // tm,),
        in_specs=[pl.BlockSpec((tm, d), lambda i: (i, 0)), pl.BlockSpec((1, d), lambda i: (0, 0))],
        out_specs=pl.BlockSpec((tm, d), lambda i: (i, 0)),
        out_shape=SDS((t, d), BF), compiler_params=_params("parallel"))(x, g)


def _mm_nn(a, w, l, bias=None, resid=None, out_dtype=F32, gather=None, norm_g=None):
    m, k = a.shape
    _, p_n, kw, n = w.shape
    assert k == kw
    normed = norm_g is not None
    assert not normed or (p_n == 1 and not gather)
    tm = min(512 if normed else 1024, m)
    tn = n if (normed or k * n * 2 <= 4 * 2**20) else n // 2
    nj = n // tn
    in_specs = [pl.BlockSpec((tm, k), lambda i, p, j: (i, 0)),
                pl.BlockSpec((None, None, k, tn), lambda i, p, j: (l, p, 0, j))]
    args = [a, w]
    if bias is not None:
        in_specs.append(pl.BlockSpec((1, tn), lambda i, p, j: (0, p * nj + j)))
        args.append(bias)
    if resid is not None:
        in_specs.append(pl.BlockSpec((tm, tn), lambda i, p, j: (i, p * nj + j)))
        args.append(resid)
    if normed:
        in_specs.append(pl.BlockSpec((1, n), lambda i, p, j: (0, 0)))
        args.append(norm_g)
    n_in = len(args)

    def body(*refs):
        acc = jnp.dot(refs[0][...], refs[1][...], preferred_element_type=F32)
        nxt = 2
        if bias is not None:
            acc = acc + refs[nxt][...]
            nxt += 1
        if resid is not None:
            acc = refs[nxt][...] + acc
        refs[n_in][...] = acc.astype(refs[n_in].dtype)
        if normed:
            refs[n_in + 1][...] = _rms(acc, refs[n_in - 1][...]).astype(BF)

    out_spec = pl.BlockSpec((tm, tn), lambda i, p, j: (i, p * nj + j))
    kw = dict(name="mm_nn", grid=(m // tm, p_n, nj), in_specs=in_specs,
              out_specs=[out_spec, out_spec] if normed else out_spec,
              out_shape=[SDS((m, n), out_dtype), SDS((m, n), BF)] if normed else SDS((m, p_n * n), out_dtype))
    if gather:
        (out,), gathered = _call_gathering(body, gather, args, **kw)
        return out, gathered
    return _call(body, compiler_params=_params("parallel", "parallel", "parallel"), **kw)(*args)


def _mm_nt(dy, w, l, out_dtype=F32):
    m, n_all = dy.shape
    _, p_n, r, n = w.shape
    assert n_all == p_n * n
    tm = min(512, m)

    def body(dy_ref, w_ref, o_ref):
        acc = lax.dot_general(dy_ref[:, 0:n], w_ref[0], NT_DIMS, preferred_element_type=F32)
        for p in range(1, p_n):
            acc = acc + lax.dot_general(dy_ref[:, p * n:(p + 1) * n], w_ref[p], NT_DIMS, preferred_element_type=F32)
        o_ref[...] = acc.astype(o_ref.dtype)

    return _call(
        body, name="mm_nt", grid=(m // tm,),
        in_specs=[pl.BlockSpec((tm, n_all), lambda i: (i, 0)),
                  pl.BlockSpec((None, p_n, r, n), lambda i: (l, 0, 0, 0))],
        out_specs=pl.BlockSpec((tm, r), lambda i: (i, 0)),
        out_shape=SDS((m, r), out_dtype),
        compiler_params=_params("parallel"))(dy, w)


def _mm_nt_rms_bwd(dy, w, l, x, g, dres):
    m, n_all = dy.shape
    _, p_n, r, n = w.shape
    assert n_all == p_n * n and x.shape == (m, r)
    tm = min(256, m)

    def body(dy_ref, w_ref, x_ref, g_ref, r_ref, dx_ref, dxb_ref, dg_ref):
        dh = lax.dot_general(dy_ref[:, 0:n], w_ref[0], NT_DIMS, preferred_element_type=F32)
        for p in range(1, p_n):
            dh = dh + lax.dot_general(dy_ref[:, p * n:(p + 1) * n], w_ref[p], NT_DIMS, preferred_element_type=F32)
        _, vjp = jax.vjp(_rms, x_ref[...], g_ref[...])
        dx, dg = vjp(dh)
        dx = dx + r_ref[...]
        dx_ref[...] = dx
        dxb_ref[...] = dx.astype(BF)

        @pl.when(pl.program_id(0) == 0)
        def _():
            dg_ref[...] = jnp.zeros_like(dg_ref)

        dg_ref[...] += dg

    row = pl.BlockSpec((tm, r), lambda i: (i, 0))
    vec = pl.BlockSpec((1, r), lambda i: (0, 0))
    return _call(
        body, name="mm_nt_rms_bwd", grid=(m // tm,),
        in_specs=[pl.BlockSpec((tm, n_all), lambda i: (i, 0)), pl.BlockSpec((None, p_n, r, n), lambda i: (l, 0, 0, 0)),
                  row, vec, row],
        out_specs=[row, row, vec], out_shape=[SDS((m, r), F32), SDS((m, r), BF), SDS((1, r), F32)],
        compiler_params=_params("arbitrary"))(dy, w, x, g, dres)


def _mm_tn(a, dy, p_n, n_layers, l, buf=None):
    m, k = a.shape
    n = dy.shape[1] // p_n
    tm = min(2048, m)
    tk = k if k <= 1024 else k // 2
    nm = m // tm

    def body(a_ref, dy_ref, *rest):
        o_ref, acc_ref = rest[-2], rest[-1]
        mi = pl.program_id(2)
        part = lax.dot_general(a_ref[...], dy_ref[...], TN_DIMS, preferred_element_type=F32)

        @pl.when(mi == 0)
        def _():
            acc_ref[...] = part

        @pl.when(mi > 0)
        def _():
            acc_ref[...] += part

        @pl.when(mi == nm - 1)
        def _():
            o_ref[...] = acc_ref[...].astype(o_ref.dtype)

    in_specs = [pl.BlockSpec((tm, tk), lambda p, kk, mi: (mi, kk)),
                pl.BlockSpec((tm, n), lambda p, kk, mi: (mi, p))]
    args = [a, dy]
    aliases = {}
    if buf is not None:
        in_specs.append(pl.BlockSpec(memory_space=pl.ANY))
        args.append(buf)
        aliases = {2: 0}
    return _call(
        body, name="mm_tn", grid=(p_n, k // tk, nm), in_specs=in_specs,
        out_specs=pl.BlockSpec((None, None, tk, n), lambda p, kk, mi: (l, p, kk, 0)),
        out_shape=SDS((n_layers, p_n, k, n), BF), scratch_shapes=[pltpu.VMEM((tk, n), F32)],
        input_output_aliases=aliases,
        compiler_params=_params("parallel", "parallel", "arbitrary"))(*args)


def _loss_grad(y, tgt):
    t, d = y.shape
    tm = min(512, t)

    def body(y_ref, t_ref, l_ref, d_ref, db_ref):
        err = y_ref[...] - t_ref[...]
        dy = err * (1.0 / d)
        d_ref[...] = dy
        db_ref[...] = dy.astype(BF)
        part = 0.5 * jnp.sum(jnp.sum(err * err, axis=1, keepdims=True) * (1.0 / d), axis=0, keepdims=True)

        @pl.when(pl.program_id(0) == 0)
        def _():
            l_ref[...] = jnp.zeros_like(l_ref)

        l_ref[...] += jnp.broadcast_to(part, l_ref.shape)

    row = pl.BlockSpec((tm, d), lambda i: (i, 0))
    return _call(
        body, name="loss_grad", grid=(t // tm,), in_specs=[row, row],
        out_specs=[pl.BlockSpec((1, LANES), lambda i: (0, 0)), row, row],
        out_shape=[SDS((1, LANES), F32), SDS((t, d), F32), SDS((t, d), BF)],
        compiler_params=_params("arbitrary"))(y, tgt)


def _prev_halo(tr, halo, col):
    return lambda i: (jnp.maximum(i * (tr // halo) - 1, 0), col)


def _next_halo(tr, halo, n_rows, col):
    return lambda i: (jnp.minimum((i + 1) * (tr // halo), n_rows // halo - 1), col)


def _shifted_back(x):
    return pltpu.roll(x, 1, 0), pltpu.roll(x, 2, 0)


def _conv3(x, w_ref, b_ref, col):
    x1, x2 = _shifted_back(x)
    return b_ref[:, col] + w_ref[pl.ds(0, 1), col] * x2 + w_ref[pl.ds(1, 1), col] * x1 + w_ref[pl.ds(2, 1), col] * x


def _ffn_mid_fwd(up, w_dw, b_dw, gather=None):
    t, f2 = up.shape
    f = f2 // 2
    tr = min(256, t)
    h = FFN_HALO

    def body(g_ref, gp_ref, v_ref, w_ref, b_ref, o_ref):
        first_tile = pl.program_id(0) == 0

        def strip(c, carry):
            col = pl.ds(pl.multiple_of(c * LANES, LANES), LANES)
            x = jnp.concatenate([jnp.where(first_tile, 0.0, gp_ref[:, col].astype(F32)), g_ref[:, col].astype(F32)], axis=0)
            gc = _conv3(x, w_ref, b_ref, col)[h:]
            o_ref[:, col] = (gc * jax.nn.sigmoid(gc) * v_ref[:, col].astype(F32)).astype(o_ref.dtype)
            return carry

        lax.fori_loop(0, f // LANES, strip, 0)

    kw = dict(name="ffn_mid_fwd", grid=(t // tr,),
              in_specs=[pl.BlockSpec((tr, f), lambda i: (i, 0)), pl.BlockSpec((h, f), _prev_halo(tr, h, 0)),
                        pl.BlockSpec((tr, f), lambda i: (i, 1)),
                        pl.BlockSpec((8, f), lambda i: (0, 0)), pl.BlockSpec((1, f), lambda i: (0, 0))],
              out_specs=pl.BlockSpec((tr, f), lambda i: (i, 0)), out_shape=SDS((t, f), BF))
    args = (up, up, up, w_dw, b_dw)
    if gather:
        (out,), gathered = _call_gathering(body, gather, args, **kw)
        return out, gathered
    return _call(body, compiler_params=_params("parallel"), **kw)(*args)


def _ffn_mid_bwd(up, da, w_dw, b_dw):
    t, f2 = up.shape
    f = f2 // 2
    tr = min(256, t)
    h = FFN_HALO
    n_tiles = t // tr

    def body(g_ref, gp_ref, gn_ref, v_ref, vn_ref, da_ref, dan_ref, w_ref, b_ref, dup_ref, dw_ref, db_ref):
        i = pl.program_id(0)
        last = i == n_tiles - 1
        n = tr + h

        @pl.when(i == 0)
        def _():
            dw_ref[...] = jnp.zeros_like(dw_ref)
            db_ref[...] = jnp.zeros_like(db_ref)

        def rows(tile_ref, next_ref, col):
            return jnp.concatenate([tile_ref[:, col].astype(F32), next_ref[:, col].astype(F32)], axis=0)

        def strip(c, carry):
            col = pl.ds(pl.multiple_of(c * LANES, LANES), LANES)
            x = jnp.concatenate([jnp.where(i == 0, 0.0, gp_ref[:, col].astype(F32)), rows(g_ref, gn_ref, col)], axis=0)
            x1, x2 = _shifted_back(x)
            w0, w1, w2 = (w_ref[pl.ds(k, 1), col] for k in range(FFN_K))
            gc = (b_ref[:, col] + w0 * x2 + w1 * x1 + w2 * x)[h:]
            dav = rows(da_ref, dan_ref, col)
            sg = jax.nn.sigmoid(gc)
            dup_ref[:, pl.ds(pl.multiple_of(f + c * LANES, LANES), LANES)] = (dav * gc * sg)[:tr].astype(dup_ref.dtype)
            dgc = dav * rows(v_ref, vn_ref, col) * (sg * (1.0 + gc * (1.0 - sg)))
            dgc = jnp.concatenate([dgc[:tr], jnp.where(last, 0.0, dgc[tr:])], axis=0)
            d1, d2 = pltpu.roll(dgc, n - 1, 0), pltpu.roll(dgc, n - 2, 0)
            dup_ref[:, col] = (w2 * dgc + w1 * d1 + w0 * d2)[:tr].astype(dup_ref.dtype)
            dgt = dgc[:tr]
            for k, past in enumerate((x2, x1, x)):
                dw_ref[pl.ds(k, 1), col] += jnp.sum(past[h:h + tr] * dgt, axis=0, keepdims=True)
            db_ref[:, col] += jnp.sum(dgt, axis=0, keepdims=True)
            return carry

        lax.fori_loop(0, f // LANES, strip, 0)

    tile = lambda col: pl.BlockSpec((tr, f), lambda i: (i, col))
    nxt = lambda col: pl.BlockSpec((h, f), _next_halo(tr, h, t, col))
    return _call(
        body, name="ffn_mid_bwd", grid=(n_tiles,),
        in_specs=[tile(0), pl.BlockSpec((h, f), _prev_halo(tr, h, 0)), nxt(0), tile(1), nxt(1), tile(0), nxt(0),
                  pl.BlockSpec((8, f), lambda i: (0, 0)), pl.BlockSpec((1, f), lambda i: (0, 0))],
        out_specs=[pl.BlockSpec((tr, f2), lambda i: (i, 0)), pl.BlockSpec((8, f), lambda i: (0, 0)),
                   pl.BlockSpec((1, f), lambda i: (0, 0))],
        out_shape=[SDS((t, f2), BF), SDS((8, f), F32), SDS((1, f), F32)],
        compiler_params=_params("arbitrary"))(up, up, up, up, up, da, da, w_dw, b_dw)


def _ln_silu(yc, g, b):
    mu = jnp.mean(yc, axis=-1, keepdims=True)
    xc = yc - mu
    y = xc * lax.rsqrt(jnp.mean(xc * xc, axis=-1, keepdims=True) + NORM_EPS)
    return jax.nn.silu(y * g + b)


SUBLANES = 8
CONV_PAD = 24
SHIFT_CHUNK = 40
TAP_ROWS = 64


def _glu(a, g):
    return a.astype(F32) * jax.nn.sigmoid(g.astype(F32))


def _glu_strip(ygs_ref, first_tile, a_ref, ap_ref, g_ref, gp_ref, col, h, tr):
    ygs_ref[pl.ds(0, h), :] = jnp.where(first_tile, 0.0, _glu(ap_ref[:, col], gp_ref[:, col]))
    ygs_ref[pl.ds(h, tr), :] = _glu(a_ref[:, col], g_ref[:, col])


def _shift_past(sh_ref, ygs_ref, h, n):
    first = h - CONV_PAD - SUBLANES
    for u0 in range(0, n + CONV_PAD, SHIFT_CHUNK):
        x = ygs_ref[pl.ds(first + u0, SHIFT_CHUNK + SUBLANES), :]
        for r in range(1, SUBLANES):
            sh_ref[r, pl.ds(u0, SHIFT_CHUNK), :] = pltpu.roll(x, r, 0)[SUBLANES:]


def _past_rows(sh_ref, ygs_ref, h, n, s, row0=0):
    a, r = divmod(s, SUBLANES)
    if r == 0:
        return ygs_ref[pl.ds(row0 + h - SUBLANES * a, n), :]
    return sh_ref[r, pl.ds(row0 + CONV_PAD - SUBLANES * a, n), :]


def _conf_mid_fwd(p1, w_dw, b_dw, ln_g, ln_b, gather=None):
    t, w2 = p1.shape
    w = w2 // 2
    tr = min(256, t)
    h = CONV_HALO
    rc = 32

    def body(a_ref, ap_ref, g_ref, gp_ref, w_ref, b_ref, lg_ref, lb_ref, o_ref, yc_ref, ygs_ref, sh_ref):
        first_tile = pl.program_id(0) == 0

        def strip(c, carry):
            col = pl.ds(pl.multiple_of(c * LANES, LANES), LANES)
            _glu_strip(ygs_ref, first_tile, a_ref, ap_ref, g_ref, gp_ref, col, h, tr)
            _shift_past(sh_ref, ygs_ref, h, tr)
            acc = jnp.broadcast_to(b_ref[:, col], (tr, LANES))
            for k in range(CONV_K):
                acc = acc + w_ref[pl.ds(k, 1), col] * _past_rows(sh_ref, ygs_ref, h, tr, CONV_K - 1 - k)
            yc_ref[:, col] = acc
            return carry

        lax.fori_loop(0, w // LANES, strip, 0)

        def rows(r, carry):
            rs = pl.ds(pl.multiple_of(r * rc, rc), rc)
            o_ref[rs, :] = _ln_silu(yc_ref[rs, :], lg_ref[...], lb_ref[...]).astype(o_ref.dtype)
            return carry

        lax.fori_loop(0, tr // rc, rows, 0)

    vec = pl.BlockSpec((1, w), lambda i: (0, 0))
    tile = pl.BlockSpec((tr, w), lambda i: (i, 0))
    kw = dict(name="conf_mid_fwd", grid=(t // tr,),
              in_specs=[tile, pl.BlockSpec((h, w), _prev_halo(tr, h, 0)),
                        pl.BlockSpec((tr, w), lambda i: (i, 1)), pl.BlockSpec((h, w), _prev_halo(tr, h, 1)),
                        pl.BlockSpec((32, w), lambda i: (0, 0)), vec, vec, vec],
              out_specs=[tile, tile], out_shape=[SDS((t, w), BF), SDS((t, w), F32)],
              scratch_shapes=[pltpu.VMEM((h + tr, LANES), F32), pltpu.VMEM((SUBLANES, tr + CONV_PAD, LANES), F32)])
    args = (p1, p1, p1, p1, w_dw, b_dw, ln_g, ln_b)
    if gather:
        return _call_gathering(body, gather, args, **kw)
    return _call(body, compiler_params=_params("parallel"), **kw)(*args)


def _conf_mid_bwd(p1, yc, dys, dy, w_dw, ln_g, ln_b, exchange=None):
    t, w2 = p1.shape
    w = w2 // 2
    tr = min(256, t)
    h = CONV_HALO
    rc = 32
    n_tiles = t // tr

    def body(a_ref, ap_ref, g_ref, gp_ref, yc_ref, ycn_ref, dys_ref, dysn_ref, dy_ref, w_ref, lg_ref, lb_ref,
             dp_ref, dw_ref, db_ref, dlg_ref, dlb_ref, db1_ref, db2_ref, dyc_ref, ygs_ref, sh_ref, shf_ref, dwacc_ref):
        i = pl.program_id(0)
        last = i == n_tiles - 1

        @pl.when(i == 0)
        def _():
            for ref in (dw_ref, db_ref, dlg_ref, dlb_ref, db1_ref, db2_ref):
                ref[...] = jnp.zeros_like(ref)

        def ln_rows(r, carry):
            rs = pl.ds(pl.multiple_of(r * rc, rc), rc)
            _, vjp = jax.vjp(_ln_silu, yc_ref[rs, :], lg_ref[...], lb_ref[...])
            dyc, dlg, dlb = vjp(dys_ref[rs, :].astype(F32))
            dyc_ref[rs, :] = dyc
            dlg_ref[...] += dlg
            dlb_ref[...] += dlb
            return carry

        lax.fori_loop(0, tr // rc, ln_rows, 0)
        _, vjp = jax.vjp(_ln_silu, ycn_ref[...], lg_ref[...], lb_ref[...])
        dyc_ref[pl.ds(tr, h), :] = jnp.where(last, 0.0, vjp(dysn_ref[...].astype(F32))[0])
        db2_ref[...] += jnp.sum(dy_ref[...], axis=0, keepdims=True)

        def back(c, carry):
            col = pl.ds(pl.multiple_of(c * LANES, LANES), LANES)
            gcol = pl.ds(pl.multiple_of(w + c * LANES, LANES), LANES)
            _glu_strip(ygs_ref, i == 0, a_ref, ap_ref, g_ref, gp_ref, col, h, tr)
            _shift_past(sh_ref, ygs_ref, h, tr)
            for u0 in range(0, tr + CONV_PAD, SHIFT_CHUNK):
                part = dyc_ref[pl.ds(u0, SHIFT_CHUNK + SUBLANES), col]
                for r in range(1, SUBLANES):
                    shf_ref[r, pl.ds(u0, SHIFT_CHUNK), :] = pltpu.roll(part, SHIFT_CHUNK + SUBLANES - r, 0)[:SHIFT_CHUNK]
            for r0 in range(0, tr, TAP_ROWS):
                rows = pl.ds(r0, TAP_ROWS)
                dyc = dyc_ref[rows, col]
                dyg = jnp.zeros((TAP_ROWS, LANES), F32)
                for k in range(CONV_K):
                    s = CONV_K - 1 - k
                    a, r = divmod(s, SUBLANES)
                    if r == 0:
                        future = dyc_ref[pl.ds(r0 + SUBLANES * a, TAP_ROWS), col]
                    else:
                        future = shf_ref[r, pl.ds(r0 + SUBLANES * a, TAP_ROWS), :]
                    dyg = dyg + w_ref[pl.ds(k, 1), col] * future
                    prod = _past_rows(sh_ref, ygs_ref, h, TAP_ROWS, s, r0) * dyc
                    part = prod[0:SUBLANES]
                    for q in range(1, TAP_ROWS // SUBLANES):
                        part = part + prod[q * SUBLANES:(q + 1) * SUBLANES]
                    if r0 == 0:
                        dwacc_ref[k] = part
                    else:
                        dwacc_ref[k] += part
                sg = jax.nn.sigmoid(g_ref[rows, col].astype(F32))
                da = dyg * sg
                dg = dyg * a_ref[rows, col].astype(F32) * sg * (1.0 - sg)
                dp_ref[rows, col] = da.astype(dp_ref.dtype)
                dp_ref[rows, gcol] = dg.astype(dp_ref.dtype)
                db_ref[:, col] += jnp.sum(dyc, axis=0, keepdims=True)
                db1_ref[:, col] += jnp.sum(da, axis=0, keepdims=True)
                db1_ref[:, gcol] += jnp.sum(dg, axis=0, keepdims=True)
            for k in range(CONV_K):
                dw_ref[pl.ds(k, 1), col] += jnp.sum(dwacc_ref[k], axis=0, keepdims=True)
            return carry

        lax.fori_loop(0, w // LANES, back, 0)

    tile = lambda col: pl.BlockSpec((tr, w), lambda i: (i, col))
    prv = lambda col: pl.BlockSpec((h, w), _prev_halo(tr, h, col))
    nxt = pl.BlockSpec((h, w), _next_halo(tr, h, t, 0))
    vec = pl.BlockSpec((1, w), lambda i: (0, 0))
    kw = dict(
        name="conf_mid_bwd", grid=(n_tiles,),
        in_specs=[tile(0), prv(0), tile(1), prv(1), tile(0), nxt, tile(0), nxt, tile(0),
                  pl.BlockSpec((32, w), lambda i: (0, 0)), vec, vec],
        out_specs=[pl.BlockSpec((tr, w2), lambda i: (i, 0)), pl.BlockSpec((32, w), lambda i: (0, 0)), vec, vec, vec,
                   pl.BlockSpec((1, w2), lambda i: (0, 0)), vec],
        out_shape=[SDS((t, w2), BF), SDS((32, w), F32), SDS((1, w), F32), SDS((1, w), F32), SDS((1, w), F32),
                   SDS((1, w2), F32), SDS((1, w), F32)],
        scratch_shapes=[pltpu.VMEM((tr + h, w), F32), pltpu.VMEM((h + tr, LANES), F32),
                        pltpu.VMEM((SUBLANES, tr + CONV_PAD, LANES), F32), pltpu.VMEM((SUBLANES, tr + CONV_PAD, LANES), F32),
                        pltpu.VMEM((32, SUBLANES, LANES), F32)])
    args = (p1, p1, p1, p1, yc, yc, dys, dys, dy, w_dw, ln_g, ln_b)
    if exchange is not None:
        return _call_hosting(body, exchange, args, **kw)
    return _call(body, compiler_params=_params("arbitrary"), **kw)(*args)


def _group_matrices():
    i = lax.broadcasted_iota(jnp.int32, (512, 512), 0)
    j = lax.broadcasted_iota(jnp.int32, (512, 512), 1)
    mean64 = jnp.where(i // HEAD_DIM == j // HEAD_DIM, 1.0 / HEAD_DIM, 0.0).astype(F32)
    fold64 = jnp.where(i % HEAD_DIM == j % HEAD_DIM, 1.0, 0.0).astype(F32)
    return mean64, fold64


def _split_dot(x, mat):
    hi = x.astype(BF)
    lo = (x - hi.astype(F32)).astype(BF)
    mb = mat.astype(BF)
    return jnp.dot(hi, mb, preferred_element_type=F32) + jnp.dot(lo, mb, preferred_element_type=F32)


@jax.custom_vjp
def _group_sum(x, mat):
    return _split_dot(x, mat)


_group_sum.defvjp(lambda x, mat: (_split_dot(x, mat), mat), lambda mat, ct: (_split_dot(ct, mat), jnp.zeros_like(mat)))


def _bf_dot_plain(a, b):
    return jnp.dot(a.astype(BF), b.astype(BF), preferred_element_type=F32)


@jax.custom_vjp
def _bf_dot(a, b):
    return _bf_dot_plain(a, b)


def _bf_dot_bwd(res, ct):
    a, b = res
    cb = ct.astype(BF)
    return (lax.dot_general(cb, b.astype(BF), NT_DIMS, preferred_element_type=F32),
            lax.dot_general(a.astype(BF), cb, TN_DIMS, preferred_element_type=F32))


_bf_dot.defvjp(lambda a, b: (_bf_dot_plain(a, b), (a, b)), _bf_dot_bwd)


def _prep_tile(proj, qg, kg, zg, ws, bexp, mean64, differentiated=False):
    sw = 512
    q, k, v, u, z = (proj[:, n * sw:(n + 1) * sw] for n in range(5))
    group_sum, dot = (_group_sum, _bf_dot) if differentiated else (_split_dot, _bf_dot_plain)

    def group_norm(x):
        return x * lax.rsqrt(group_sum(x * x, mean64) + NORM_EPS)

    qn = group_norm(q) * qg
    kn = group_norm(k) * kg
    zn = group_norm(_gelu(z)) * zg
    row = lax.broadcasted_iota(jnp.int32, (CHUNK, CHUNK), 0)
    col = lax.broadcasted_iota(jnp.int32, (CHUNK, CHUNK), 1)
    first = lax.broadcasted_iota(jnp.int32, (1, LANES), 1) < HEAD_DIM
    parts = []
    for pr in range(sw // LANES):
        zp = zn[:, pr * LANES:(pr + 1) * LANES]
        s0 = dot(jnp.where(col <= row, ws[2 * pr], 0.0), zp)
        s1 = dot(jnp.where(col <= row, ws[2 * pr + 1], 0.0), zp)
        parts.append(jnp.where(first, s0, s1))
    s = jnp.concatenate(parts, axis=1) + bexp
    return qn, kn, v, _gelu(u) * s


def _mix_prep_fwd(proj, qg, kg, zg, w_s, l, bexp, mean64, gather=None):
    t = proj.shape[0]
    tr = CHUNK

    def body(p_ref, qg_ref, kg_ref, zg_ref, ws_ref, be_ref, m_ref, qkv_ref, go_ref):
        qn, kn, v, go = _prep_tile(p_ref[...], qg_ref[...], kg_ref[...], zg_ref[...], ws_ref[...], be_ref[...], m_ref[...])
        qkv_ref[:, 0:512] = qn.astype(BF)
        qkv_ref[:, 512:1024] = kn.astype(BF)
        qkv_ref[:, 1024:1536] = v.astype(BF)
        go_ref[...] = go.astype(BF)

    vec = pl.BlockSpec((1, 512), lambda i: (0, 0))
    kw = dict(name="mix_prep_fwd", grid=(t // tr,),
              in_specs=[pl.BlockSpec((tr, 2560), lambda i: (i, 0)), vec, vec, vec,
                        pl.BlockSpec((None, 8, CHUNK, CHUNK), lambda i: (l, 0, 0, 0)),
                        pl.BlockSpec((CHUNK, 512), lambda i: (0, 0)), pl.BlockSpec((512, 512), lambda i: (0, 0))],
              out_specs=[pl.BlockSpec((tr, 1536), lambda i: (i, 0)), pl.BlockSpec((tr, 512), lambda i: (i, 0))],
              out_shape=[SDS((t, 1536), BF), SDS((t, 512), BF)])
    args = (proj, qg, kg, zg, w_s, bexp, mean64)
    if gather:
        return _call_gathering(body, gather, args, **kw)
    return _call(body, compiler_params=_params("parallel"), **kw)(*args)


def _mix_prep_bwd(proj, dq, dk, dv, dmix, qg, kg, zg, w_s, l, bexp, mean64, fold64):
    t = proj.shape[0]
    tr = CHUNK
    n_tiles = t // tr

    def body(p_ref, dq_ref, dk_ref, dv_ref, dgo_ref, qg_ref, kg_ref, zg_ref, ws_ref, be_ref, m_ref, f_ref,
             dp_ref, dqg_ref, dkg_ref, dzg_ref, dws_ref, dbe_ref):
        i = pl.program_id(0)

        @pl.when(i == 0)
        def _():
            for ref in (dqg_ref, dkg_ref, dzg_ref, dws_ref, dbe_ref):
                ref[...] = jnp.zeros_like(ref)

        fn = functools.partial(_prep_tile, mean64=m_ref[...], differentiated=True)
        _, vjp = jax.vjp(fn, p_ref[...], qg_ref[...], kg_ref[...], zg_ref[...], ws_ref[...], be_ref[...])
        dp, dqg, dkg, dzg, dws, dbe = vjp((dq_ref[...], dk_ref[...], dv_ref[...], dgo_ref[...]))
        dp_ref[...] = dp.astype(BF)
        dqg_ref[pl.ds(0, 1), :] += dqg
        dkg_ref[pl.ds(0, 1), :] += dkg
        dzg_ref[pl.ds(0, 1), :] += dzg
        dws_ref[...] += dws
        dbe_ref[...] += dbe

        @pl.when(i == n_tiles - 1)
        def _():
            dqg_ref[...] = jnp.dot(dqg_ref[...], f_ref[...], precision=HI, preferred_element_type=F32)
            dkg_ref[...] = jnp.dot(dkg_ref[...], f_ref[...], precision=HI, preferred_element_type=F32)
            dbe_ref[...] = jnp.dot(dbe_ref[...], m_ref[...] * float(HEAD_DIM), precision=HI, preferred_element_type=F32)

    vec = pl.BlockSpec((1, 512), lambda i: (0, 0))
    acc = pl.BlockSpec((8, 512), lambda i: (0, 0))
    sq = pl.BlockSpec((512, 512), lambda i: (0, 0))
    row = pl.BlockSpec((tr, 512), lambda i: (i, 0))
    return _call(
        body, name="mix_prep_bwd", grid=(n_tiles,),
        in_specs=[pl.BlockSpec((tr, 2560), lambda i: (i, 0)), row, row, row, pl.BlockSpec((tr, 512), lambda i: (i, 1)),
                  vec, vec, vec, pl.BlockSpec((None, 8, CHUNK, CHUNK), lambda i: (l, 0, 0, 0)),
                  pl.BlockSpec((CHUNK, 512), lambda i: (0, 0)), sq, sq],
        out_specs=[pl.BlockSpec((tr, 2560), lambda i: (i, 0)), acc, acc, acc,
                   pl.BlockSpec((8, CHUNK, CHUNK), lambda i: (0, 0, 0)), pl.BlockSpec((CHUNK, 512), lambda i: (0, 0))],
        out_shape=[SDS((t, 2560), BF), SDS((8, 512), F32), SDS((8, 512), F32), SDS((8, 512), F32),
                   SDS((8, CHUNK, CHUNK), F32), SDS((CHUNK, 512), F32)],
        compiler_params=_params("arbitrary"))(proj, dq, dk, dv, dmix, qg, kg, zg, w_s, bexp, mean64, fold64)


def _sb_logs(qh, kb, valid):
    z = lax.dot_general(qh, kb, NT_DIMS, preferred_element_type=F32) * (HEAD_DIM ** -0.5)
    soft = jnp.log1p(jnp.exp(-jnp.abs(z)))
    lk_raw = -(jnp.maximum(z, 0.0) + soft)
    ls = -(jnp.maximum(-z, 0.0) + soft)
    return lk_raw, ls, jnp.where(valid, lk_raw, 0.0)


def _sb_weights(ls, run, tail, valid):
    return jnp.where(valid, jnp.exp(ls + run + tail), 0.0)


def _att_masks(b):
    row = lax.broadcasted_iota(jnp.int32, (b, b), 0)
    col = lax.broadcasted_iota(jnp.int32, (b, b), 1)
    first = lax.broadcasted_iota(jnp.int32, (1, LANES), 1) < HEAD_DIM
    return row, col, first


N_PAIRS = 4


def _load_kv(qkv_hbm, k_scr, v_scr, sems, group, width):
    ck = pltpu.make_async_copy(qkv_hbm.at[:, pl.ds(pl.multiple_of(512 + group * width, LANES), width)], k_scr, sems.at[0])
    cv = pltpu.make_async_copy(qkv_hbm.at[:, pl.ds(pl.multiple_of(1024 + group * width, LANES), width)], v_scr, sems.at[1])
    ck.start()
    cv.start()
    ck.wait()
    cv.wait()


def _split_heads(ref, pair, first):
    x = ref[:, pair * LANES:(pair + 1) * LANES]
    zero = jnp.zeros_like(x)
    return jnp.where(first, x, zero), jnp.where(first, zero, x)


def _any_weight_left(run_ref, n_heads):
    top = run_ref[0]
    for hh in range(1, n_heads):
        top = jnp.maximum(top, run_ref[hh])
    return jnp.max(jnp.exp(top)) > 0.0


def _attn_fwd(qkv, pairs_per_step=4, gather=None):
    t = qkv.shape[0]
    b = ATT_BLOCK
    nq = t // b
    width = pairs_per_step * LANES
    n_heads = 2 * pairs_per_step

    def body(q_ref, qkv_hbm, ob_ref, o32_ref, k_scr, v_scr, acc_ref, run_ref, sems):
        group, qi = pl.program_id(0), pl.program_id(1)

        @pl.when(qi == 0)
        def _():
            _load_kv(qkv_hbm, k_scr, v_scr, sems, group, width)

        row, col, first = _att_masks(b)
        qh = [x for pr in range(pairs_per_step) for x in _split_heads(q_ref, pr, first)]
        upper = jnp.where(row > col, 1.0, 0.0).astype(BF)
        acc_ref[...] = jnp.zeros_like(acc_ref)
        run_ref[...] = jnp.zeros_like(run_ref)
        heads = range(n_heads)

        def step(carry):
            j, _ = carry
            rows = pl.ds(pl.multiple_of(j * b, b), b)
            valid = jnp.logical_or(j != qi, col < row)
            lanes = [pl.ds((hh // 2) * LANES, LANES) for hh in heads]
            logs = [_sb_logs(qh[hh], k_scr[rows, lanes[hh]], valid) for hh in heads]
            tails = [_split_dot(logs[hh][2], upper) for hh in heads]
            for hh in heads:
                wgt = _sb_weights(logs[hh][1], run_ref[hh], tails[hh], valid)
                acc_ref[hh] += jnp.dot(wgt.astype(BF), v_scr[rows, lanes[hh]], preferred_element_type=F32)
            for hh in heads:
                run_ref[hh] += jnp.sum(logs[hh][2], axis=1, keepdims=True)
            return j - 1, _any_weight_left(run_ref, n_heads)

        lax.while_loop(lambda c: jnp.logical_and(c[0] >= 0, c[1]), step, (qi, jnp.bool_(True)))
        for pr in range(pairs_per_step):
            out = jnp.where(first, acc_ref[2 * pr], acc_ref[2 * pr + 1])
            ob_ref[:, pr * LANES:(pr + 1) * LANES] = out.astype(BF)
            o32_ref[:, pr * LANES:(pr + 1) * LANES] = out

    blk = pl.BlockSpec((b, width), lambda g, qi: (qi, g))
    kw = dict(name="attn_fwd", grid=(N_PAIRS // pairs_per_step, nq),
              in_specs=[blk, pl.BlockSpec(memory_space=pl.ANY)], out_specs=[blk, blk],
              out_shape=[SDS((t, 512), BF), SDS((t, 512), F32)],
              scratch_shapes=[pltpu.VMEM((t, width), BF), pltpu.VMEM((t, width), BF),
                              pltpu.VMEM((n_heads, b, LANES), F32), pltpu.VMEM((n_heads, b, 1), F32),
                              pltpu.SemaphoreType.DMA((2,))])
    if gather:
        return _call_gathering(body, gather, (qkv, qkv), **kw)
    return _call(body, compiler_params=_params("arbitrary", "arbitrary"), **kw)(qkv, qkv)


def _attn_bwd(qkv, a32, dmix, pairs_per_step=2):
    t = qkv.shape[0]
    b = ATT_BLOCK
    nq = t // b
    width = pairs_per_step * LANES
    n_heads = 2 * pairs_per_step

    def body(q_ref, a_ref, da_ref, qkv_hbm, dq_ref, dk_hbm, dv_hbm,
             k_scr, v_scr, dk_scr, dv_scr, dqa_ref, run_ref, rung_ref, sems):
        group, qi = pl.program_id(0), pl.program_id(1)

        @pl.when(qi == 0)
        def _():
            _load_kv(qkv_hbm, k_scr, v_scr, sems, group, width)
            dk_scr[...] = jnp.zeros_like(dk_scr)
            dv_scr[...] = jnp.zeros_like(dv_scr)

        row, col, first = _att_masks(b)
        qh, dah, dtot = [], [], []
        for pr in range(pairs_per_step):
            qh += _split_heads(q_ref, pr, first)
            da = da_ref[:, pr * LANES:(pr + 1) * LANES]
            prod = da * a_ref[:, pr * LANES:(pr + 1) * LANES]
            dtot += [jnp.sum(jnp.where(first, prod, 0.0), axis=1, keepdims=True),
                     jnp.sum(jnp.where(first, 0.0, prod), axis=1, keepdims=True)]
            dah += [jnp.where(first, da, 0.0).astype(BF), jnp.where(first, 0.0, da).astype(BF)]
        upper = jnp.where(row > col, 1.0, 0.0).astype(BF)
        lower_incl = jnp.where(row >= col, 1.0, 0.0).astype(BF)
        dqa_ref[...] = jnp.zeros_like(dqa_ref)
        run_ref[...] = jnp.zeros_like(run_ref)
        rung_ref[...] = jnp.zeros_like(rung_ref)
        heads = range(n_heads)

        def step(carry):
            j, _ = carry
            rows = pl.ds(pl.multiple_of(j * b, b), b)
            valid = jnp.logical_or(j != qi, col < row)
            lanes = [pl.ds((hh // 2) * LANES, LANES) for hh in heads]
            logs = [_sb_logs(qh[hh], k_scr[rows, lanes[hh]], valid) for hh in heads]
            dps = [lax.dot_general(dah[hh], v_scr[rows, lanes[hh]], NT_DIMS, preferred_element_type=F32) for hh in heads]
            tails = [_split_dot(logs[hh][2], upper) for hh in heads]
            wgts = [_sb_weights(logs[hh][1], run_ref[hh], tails[hh], valid) for hh in heads]
            gs = [wgts[hh] * dps[hh] for hh in heads]
            g_froms = [_split_dot(gs[hh], lower_incl) for hh in heads]
            for hh in heads:
                lk_raw, ls, _ = logs[hh]
                dlk = jnp.where(valid, dtot[hh] - rung_ref[hh] - g_froms[hh], 0.0)
                dz = ((gs[hh] * jnp.exp(lk_raw) - dlk * jnp.exp(ls)) * (HEAD_DIM ** -0.5)).astype(BF)
                dqa_ref[hh] += jnp.dot(dz, k_scr[rows, lanes[hh]], preferred_element_type=F32)
                dk_scr[rows, lanes[hh]] += lax.dot_general(dz, qh[hh], TN_DIMS, preferred_element_type=F32)
                dv_scr[rows, lanes[hh]] += lax.dot_general(wgts[hh].astype(BF), dah[hh], TN_DIMS, preferred_element_type=F32)
            for hh in heads:
                rung_ref[hh] += jnp.sum(gs[hh], axis=1, keepdims=True)
                run_ref[hh] += jnp.sum(logs[hh][2], axis=1, keepdims=True)
            return j - 1, _any_weight_left(run_ref, n_heads)

        lax.while_loop(lambda c: jnp.logical_and(c[0] >= 0, c[1]), step, (qi, jnp.bool_(True)))
        for pr in range(pairs_per_step):
            dq_ref[:, pr * LANES:(pr + 1) * LANES] = jnp.where(first, dqa_ref[2 * pr], dqa_ref[2 * pr + 1])

        @pl.when(qi == nq - 1)
        def _():
            cols = pl.ds(pl.multiple_of(group * width, LANES), width)
            ck = pltpu.make_async_copy(dk_scr, dk_hbm.at[:, cols], sems.at[0])
            cv = pltpu.make_async_copy(dv_scr, dv_hbm.at[:, cols], sems.at[1])
            ck.start()
            cv.start()
            ck.wait()
            cv.wait()

    blk = pl.BlockSpec((b, width), lambda g, qi: (qi, g))
    anywhere = pl.BlockSpec(memory_space=pl.ANY)
    return _call(
        body, name="attn_bwd", grid=(N_PAIRS // pairs_per_step, nq),
        in_specs=[blk, blk, blk, anywhere], out_specs=[blk, anywhere, anywhere],
        out_shape=[SDS((t, 512), F32), SDS((t, 512), F32), SDS((t, 512), F32)],
        scratch_shapes=[pltpu.VMEM((t, width), BF), pltpu.VMEM((t, width), BF),
                        pltpu.VMEM((t, width), F32), pltpu.VMEM((t, width), F32),
                        pltpu.VMEM((n_heads, b, LANES), F32), pltpu.VMEM((n_heads, b, 1), F32),
                        pltpu.VMEM((n_heads, b, 1), F32), pltpu.SemaphoreType.DMA((2,))],
        compiler_params=_params("arbitrary", "arbitrary"))(qkv, a32, dmix, qkv)


def _adamw(w, g, m, v):
    n, c = w.shape
    tr = min(256, n)
    assert n % tr == 0

    def body(w_ref, g_ref, m_ref, v_ref, d_ref, nm_ref, nv_ref):
        g = g_ref[...]
        m = ADAM_B1 * m_ref[...] + (1.0 - ADAM_B1) * g
        v = ADAM_B2 * v_ref[...] + (1.0 - ADAM_B2) * jnp.square(g)
        m_hat = m / (1.0 - ADAM_B1 ** ADAM_STEP)
        v_hat = v / (1.0 - ADAM_B2 ** ADAM_STEP)
        d_ref[...] = -ADAM_LR * (m_hat / (jnp.sqrt(v_hat) + ADAM_EPS) + ADAM_WD * w_ref[...])
        nm_ref[...] = m
        nv_ref[...] = v

    blk = pl.BlockSpec((tr, c), lambda i: (i, 0))
    return _call(
        body, name="adamw", grid=(n // tr,), in_specs=[blk] * 4, out_specs=[blk] * 3,
        out_shape=[SDS((n, c), F32)] * 3, compiler_params=_params("parallel"))(w, g, m, v)


def _mesh_pos():
    return lax.axis_index("x"), lax.axis_index("y"), lax.axis_index("c")


def _other_chips(x, y):
    return [(1 - x, y), (x, 1 - y), (1 - x, 1 - y)]


HBM_SPEC = pl.BlockSpec(memory_space=pltpu.HBM)


GATHER_COPIES = 6


def _gather_steps(s_ref, o_ref, send_sems, recv_sems, local_sems, slot):
    h = s_ref.shape[1] // 2
    x, y, c = _mesh_pos()
    sibling = (x, y, 1 - c)
    chips = _other_chips(x, y)
    base = GATHER_COPIES * slot

    def half(px, py, hc):
        return o_ref.at[:, 2 * px + py, pl.ds(hc * h, h), :]

    def copy(k, dst, to, src=None):
        return pltpu.make_async_remote_copy(
            src_ref=dst if src is None else src, dst_ref=dst, send_sem=send_sems.at[base + k],
            recv_sem=recv_sems.at[base + k], device_id=to, device_id_type=MESH)

    mine = pltpu.make_async_copy(s_ref, o_ref.at[:, 2 * x + y], local_sems.at[slot])
    first = [copy(j, half(x, y, c), (*chip, c), src=s_ref.at[:, pl.ds(c * h, h), :]) for j, chip in enumerate(chips)]
    passed = [copy(3 + j, half(*chip, c), sibling) for j, chip in enumerate(chips)]

    def start():
        mine.start()
        for cp in first:
            cp.start()

    def finish():
        for j, chip in enumerate(chips):
            copy(j, half(*chip, c), (x, y, c)).wait_recv()
            passed[j].start()
        for j, chip in enumerate(chips):
            copy(3 + j, half(*chip, 1 - c), (x, y, c)).wait_recv()
        for cp in first + passed:
            cp.wait_send()
        mine.wait()

    return start, finish


def _gather_scratch(n):
    return [pltpu.SemaphoreType.DMA((GATHER_COPIES * n,)), pltpu.SemaphoreType.DMA((GATHER_COPIES * n,)),
            pltpu.SemaphoreType.DMA((n,))]


def _gathered_shape(shard):
    n_l, r, c_w = shard.shape
    return SDS((n_l, N_CHIPS, r, c_w), shard.dtype)


def _all_gather(shard):
    def body(s_ref, o_ref, send_sems, recv_sems, local_sems):
        start, finish = _gather_steps(s_ref, o_ref, send_sems, recv_sems, local_sems, 0)
        start()
        finish()

    return _call(body, name="all_gather", in_specs=[HBM_SPEC], out_specs=HBM_SPEC, out_shape=_gathered_shape(shard),
                 scratch_shapes=_gather_scratch(1))(shard)


class _Exchange(NamedTuple):
    tag: str
    inputs: list
    out_shapes: list
    scratch: list
    make_steps: Callable


def _gather_exchange(shards):
    n = len(shards)

    def make_steps(s_refs, o_refs, sems):
        steps = [_gather_steps(s_refs[k], o_refs[k], *sems, k) for k in range(n)]
        return (lambda: [start() for start, _ in steps]), (lambda: [finish() for _, finish in steps])

    return _Exchange("gathering", list(shards), [_gathered_shape(s) for s in shards], _gather_scratch(n), make_steps)


def _call_hosting(body, exchange, args, *, name, grid, in_specs, out_specs, out_shape, scratch_shapes=()):
    out_specs = list(out_specs) if isinstance(out_specs, (list, tuple)) else [out_specs]
    out_shape = list(out_shape) if isinstance(out_shape, (list, tuple)) else [out_shape]
    n_in, n_out, n_scr = len(in_specs), len(out_specs), len(scratch_shapes)
    n_xi, n_xo, n_sem = len(exchange.inputs), len(exchange.out_shapes), len(exchange.scratch)

    def hosting_body(*refs):
        ins, x_ins = refs[:n_in], refs[n_in:n_in + n_xi]
        outs = refs[n_in + n_xi:n_in + n_xi + n_out]
        x_outs = refs[n_in + n_xi + n_out:n_in + n_xi + n_out + n_xo]
        scratch = refs[n_in + n_xi + n_out + n_xo:n_in + n_xi + n_out + n_xo + n_scr]
        start, finish = exchange.make_steps(x_ins, x_outs, refs[len(refs) - n_sem:])
        is_first = functools.reduce(jnp.logical_and, [pl.program_id(a) == 0 for a in range(len(grid))])
        is_last = functools.reduce(jnp.logical_and, [pl.program_id(a) == grid[a] - 1 for a in range(len(grid))])

        @pl.when(is_first)
        def _():
            start()

        body(*ins, *outs, *scratch)

        @pl.when(is_last)
        def _():
            finish()

    res = _call(
        hosting_body, name=name + "_" + exchange.tag, grid=grid, in_specs=list(in_specs) + [HBM_SPEC] * n_xi,
        out_specs=out_specs + [HBM_SPEC] * n_xo, out_shape=out_shape + list(exchange.out_shapes),
        scratch_shapes=list(scratch_shapes) + list(exchange.scratch),
        compiler_params=_params(*(["arbitrary"] * len(grid))))(*args, *exchange.inputs)
    return res[:n_out], res[n_out:]


def _call_gathering(body, shards, args, **kw):
    return _call_hosting(body, _gather_exchange(shards), args, **kw)


def _row_tile(h):
    for cand in (256, 176, 128, 64, 32, 16):
        if h % cand == 0:
            return cand
    raise ValueError(h)


def _rs_pair(gs):
    n = len(gs)

    def body(*refs):
        g_refs, a_refs, (send_sems, recv_sems) = refs[:n], refs[n:2 * n], refs[2 * n:]
        x, y, c = _mesh_pos()
        cps = []
        for k in range(n):
            h = g_refs[k].shape[2] // 2
            cps.append(pltpu.make_async_remote_copy(
                src_ref=g_refs[k].at[:, :, pl.ds((1 - c) * h, h), :], dst_ref=a_refs[k], send_sem=send_sems.at[k],
                recv_sem=recv_sems.at[k], device_id=(x, y, 1 - c), device_id_type=MESH))
        for cp in cps:
            cp.start()
        for cp in cps:
            cp.wait()

    out_shape = [SDS((g.shape[0], g.shape[1], g.shape[2] // 2, g.shape[3]), g.dtype) for g in gs]
    return _call(body, name="rs_pair", in_specs=[HBM_SPEC] * n, out_specs=[HBM_SPEC] * n, out_shape=out_shape,
                 scratch_shapes=[pltpu.SemaphoreType.DMA((n,)), pltpu.SemaphoreType.DMA((n,))])(*gs)


def _rs_pair_add(g, from_sibling, mid_dtype):
    n_l, n_p, r, c_w = g.shape
    h = r // 2
    tr = _row_tile(h)
    nt = h // tr
    c_arr = jnp.reshape(lax.axis_index("c"), (1,)).astype(jnp.int32)

    def body(c_ref, g_ref, a_ref, o_ref):
        o_ref[...] = (g_ref[...].astype(F32) + a_ref[...].astype(F32)).astype(o_ref.dtype)

    blk = (None, None, tr, c_w)
    return _call(
        body, name="rs_pair_add",
        grid_spec=pltpu.PrefetchScalarGridSpec(
            num_scalar_prefetch=1, grid=(n_l, n_p, nt),
            in_specs=[pl.BlockSpec(blk, lambda l, p, t, c_ref: (l, p, c_ref[0] * nt + t, 0)),
                      pl.BlockSpec(blk, lambda l, p, t, c_ref: (l, p, t, 0))],
            out_specs=pl.BlockSpec(blk, lambda l, p, t, c_ref: (l, p, t, 0))),
        out_shape=SDS((n_l, n_p, h, c_w), mid_dtype),
        compiler_params=_params("parallel", "parallel", "parallel"))(c_arr, g, from_sibling)


def _chips_exchange(pair_sums):
    n = len(pair_sums)

    def make_steps(s_refs, b_refs, sems):
        send_sems, recv_sems = sems
        x, y, c = _mesh_pos()
        cps = [pltpu.make_async_remote_copy(
            src_ref=s_refs[k].at[:, 2 * chip[0] + chip[1]], dst_ref=b_refs[k].at[j], send_sem=send_sems.at[3 * k + j],
            recv_sem=recv_sems.at[3 * k + j], device_id=(*chip, c), device_id_type=MESH)
            for k in range(n) for j, chip in enumerate(_other_chips(x, y))]
        return (lambda: [cp.start() for cp in cps]), (lambda: [cp.wait() for cp in cps])

    out_shapes = [SDS((3, s.shape[0], s.shape[2], s.shape[3]), s.dtype) for s in pair_sums]
    sems = [pltpu.SemaphoreType.DMA((3 * n,)), pltpu.SemaphoreType.DMA((3 * n,))]
    return _Exchange("scattering", list(pair_sums), out_shapes, sems, make_steps)


def _rs_chips(pair_sums):
    ex = _chips_exchange(pair_sums)
    n = len(pair_sums)

    def body(*refs):
        start, finish = ex.make_steps(refs[:n], refs[n:2 * n], refs[2 * n:])
        start()
        finish()

    return _call(body, name="rs_chips", in_specs=[HBM_SPEC] * n, out_specs=[HBM_SPEC] * n, out_shape=ex.out_shapes,
                 scratch_shapes=ex.scratch)(*pair_sums)


def _rs_chip_add(pair_sum, from_chips):
    n_l, _, h, c_w = pair_sum.shape
    tr = _row_tile(h)
    p_arr = jnp.reshape(2 * lax.axis_index("x") + lax.axis_index("y"), (1,)).astype(jnp.int32)

    def body(p_ref, s_ref, b_ref, o_ref):
        acc = s_ref[...].astype(F32)
        for j in range(3):
            acc = acc + b_ref[j].astype(F32)
        o_ref[...] = acc

    return _call(
        body, name="rs_chip_add",
        grid_spec=pltpu.PrefetchScalarGridSpec(
            num_scalar_prefetch=1, grid=(n_l, h // tr),
            in_specs=[pl.BlockSpec((None, None, tr, c_w), lambda l, t, p_ref: (l, p_ref[0], t, 0)),
                      pl.BlockSpec((3, None, tr, c_w), lambda l, t, p_ref: (0, l, t, 0))],
            out_specs=pl.BlockSpec((None, tr, c_w), lambda l, t, p_ref: (l, t, 0))),
        out_shape=SDS((n_l, h, c_w), F32),
        compiler_params=_params("parallel", "parallel"))(p_arr, pair_sum, from_chips)


def _rs_swap(parts_per_tensor):
    flat = [(k, hs, l0) for k, parts in enumerate(parts_per_tensor) for hs, l0 in parts]
    n_t, n_c = len(parts_per_tensor), len(flat)

    def body(*refs):
        srcs, outs, (send_sems, recv_sems, local_sems) = refs[:n_c], refs[n_c:n_c + n_t], refs[n_c + n_t:]
        x, y, c = _mesh_pos()
        home, away = [], []
        for i, (k, hs, l0) in enumerate(flat):
            n_l, h, _ = hs.shape
            place = lambda half: outs[k].at[pl.ds(l0, n_l), pl.ds(half * h, h), :]
            home.append(pltpu.make_async_copy(srcs[i], place(c), local_sems.at[i]))
            away.append((pltpu.make_async_remote_copy(
                src_ref=srcs[i], dst_ref=place(c), send_sem=send_sems.at[i], recv_sem=recv_sems.at[i],
                device_id=(x, y, 1 - c), device_id_type=MESH), place(1 - c)))
        for cp in home:
            cp.start()
        for cp, _ in away:
            cp.start()
        for i, (cp, landing) in enumerate(away):
            cp.wait_send()
            pltpu.make_async_remote_copy(src_ref=srcs[i], dst_ref=landing, send_sem=send_sems.at[i],
                                         recv_sem=recv_sems.at[i], device_id=(x, y, 1 - c), device_id_type=MESH).wait_recv()
        for cp in home:
            cp.wait()

    out_shape = []
    for parts in parts_per_tensor:
        n_l = sum(hs.shape[0] for hs, _ in parts)
        out_shape.append(SDS((n_l, 2 * parts[0][0].shape[1], parts[0][0].shape[2]), F32))
    dma = pltpu.SemaphoreType.DMA
    return _call(body, name="rs_swap", in_specs=[HBM_SPEC] * n_c, out_specs=[HBM_SPEC] * n_t, out_shape=out_shape,
                 scratch_shapes=[dma((n_c,)), dma((n_c,)), dma((n_c,))])(*[hs for _, hs, _ in flat])


def _pack(arrays, row_multiple):
    flat = jnp.concatenate([a.reshape(-1).astype(F32) for a in arrays])
    unit = row_multiple * LANES
    padded = -(-flat.shape[0] // unit) * unit
    return jnp.pad(flat, (0, padded - flat.shape[0])).reshape(padded // LANES, LANES)


def _unpack(packed, shapes):
    flat = packed.reshape(-1)
    out, pos = [], 0
    for s in shapes:
        size = 1
        for dim in s:
            size *= dim
        out.append(flat[pos:pos + size].reshape(s))
        pos += size
    return out


BIG_COL = ("sb_w_in", "cv_w_pw1", "ffn_w_up")
BIG_ROW = ("hyb_w_out", "cv_w_pw2", "ffn_w_down")
SMALL_SHARDED = ("cv_b_pw1", "cv_w_dw", "cv_b_dw", "cv_ln_g", "cv_ln_b", "cv_b_pw2", "ffn_w_dw")
SMALL_REPLICATED = ("mix_norm_g", "sb_q_norm_g", "sb_k_norm_g", "sg_z_norm_g", "sg_w_spatial", "sg_b_spatial",
                    "ffn_norm_g", "ffn_b_dw")
WEIGHTS = ("mix_norm_g", "sb_w_in", "sb_q_norm_g", "sb_k_norm_g", "sg_z_norm_g", "sg_w_spatial", "sg_b_spatial",
           "hyb_w_out", "cv_w_pw1", "cv_b_pw1", "cv_w_dw", "cv_b_dw", "cv_ln_g", "cv_ln_b", "cv_w_pw2", "cv_b_pw2",
           "ffn_norm_g", "ffn_w_up", "ffn_w_dw", "ffn_b_dw", "ffn_w_down")


def _pad_rows(a, rows):
    return jnp.pad(a, ((0, rows - a.shape[0]), (0, 0)))


def _step(x, tgt, w, m, v):
    n_layers = w["mix_norm_g"].shape[0]
    xi, yi, ci = _mesh_pos()
    chip = 2 * xi + yi

    assert n_layers == 4
    hosted_by = {("proj", 0): ["hyb_w_out"], ("prep", 0): [("ffn_w_up", 0)],
                 ("attn", 0): [("ffn_w_down", 0), "cv_w_pw1", "cv_w_pw2"],
                 ("up", 0): [("ffn_w_up", 1)], ("ffn_mid", 0): [("ffn_w_down", 1)],
                 ("conf_mid", 1): [("ffn_w_up", 2), ("ffn_w_down", 2)],
                 ("up", 1): [("ffn_w_up", 3)], ("ffn_mid", 1): [("ffn_w_down", 3)]}
    full = {}

    def shard_of(key):
        if isinstance(key, tuple):
            return w[key[0]][key[1]:key[1] + 1].astype(BF)
        return w[key].astype(BF)

    def keep(key, g4):
        if (key[0] if isinstance(key, tuple) else key) in BIG_ROW:
            g4 = g4.reshape(g4.shape[0], 1, g4.shape[1] * g4.shape[2], g4.shape[3])
        full[key] = g4

    def hosting(fn, point, *args, **kw):
        keys = hosted_by.get(point)
        if not keys:
            return fn(*args, **kw)
        out, gathered = fn(*args, gather=[shard_of(k) for k in keys], **kw)
        for key, g4 in zip(keys, gathered):
            keep(key, g4)
        return out

    keep("sb_w_in", _all_gather(shard_of("sb_w_in")))
    small_local = [w[name] for name in SMALL_SHARDED]
    gathered = _all_gather(_pack(small_local, 32)[None])[0]
    per_chip = [_unpack(gathered[p], [a.shape for a in small_local]) for p in range(N_CHIPS)]
    for k, name in enumerate(SMALL_SHARDED):
        full[name] = jnp.concatenate([per_chip[p][k] for p in range(N_CHIPS)], axis=-1)
    for name in SMALL_REPLICATED:
        full[name] = w[name]

    mean64, fold64 = _group_matrices()
    ffn_wdw = [_pad_rows(full["ffn_w_dw"][i], 8) for i in range(n_layers)]
    cv_wdw = [_pad_rows(full["cv_w_dw"][j], 32) for j in range(n_layers // 2)]
    row = lambda a: a.reshape(1, -1)

    saved = []
    cur = x
    h = _rms_fwd(cur, row(full["mix_norm_g"][0]))
    for i in range(n_layers):
        j = i // 2
        rec = {"x_in": cur, "h_mix": h}
        if i % 2 == 0:
            proj = hosting(_mm_nn, ("proj", i), h, full["sb_w_in"], j)
            qg = row(jnp.tile(full["sb_q_norm_g"][j], 512 // HEAD_DIM))
            kg = row(jnp.tile(full["sb_k_norm_g"][j], 512 // HEAD_DIM))
            zg = row(full["sg_z_norm_g"][j])
            bexp = jnp.repeat(full["sg_b_spatial"][j].T, HEAD_DIM, axis=1)
            qkv, gated = hosting(_mix_prep_fwd, ("prep", i), proj, qg, kg, zg, full["sg_w_spatial"], j, bexp, mean64)
            att_bf, att_32 = hosting(_attn_fwd, ("attn", i), qkv)
            mix = jnp.concatenate([att_bf, gated], axis=1)
            cur, h = _mm_nn(mix, full["hyb_w_out"], j, resid=cur, norm_g=row(full["ffn_norm_g"][i]))
            rec.update(proj=proj, qkv=qkv, att_32=att_32, mix=mix, qg=qg, kg=kg, zg=zg, bexp=bexp)
        else:
            p1 = _mm_nn(h, full["cv_w_pw1"], j, bias=row(full["cv_b_pw1"][j]), out_dtype=BF)
            ys, yc = hosting(_conf_mid_fwd, ("conf_mid", i), p1, cv_wdw[j], row(full["cv_b_dw"][j]),
                             row(full["cv_ln_g"][j]), row(full["cv_ln_b"][j]))
            cur, h = _mm_nn(ys, full["cv_w_pw2"], j, bias=row(full["cv_b_pw2"][j]), resid=cur,
                            norm_g=row(full["ffn_norm_g"][i]))
            rec.update(p1=p1, ys=ys, yc=yc)
        rec["x_mid"] = cur
        up = hosting(_mm_nn, ("up", i), h, full[("ffn_w_up", i)], 0, out_dtype=BF)
        act = hosting(_ffn_mid_fwd, ("ffn_mid", i), up, ffn_wdw[i], row(full["ffn_b_dw"][i]))
        rec.update(h_ffn=h, up=up, act=act)
        if i + 1 < n_layers:
            cur, h = _mm_nn(act, full[("ffn_w_down", i)], 0, resid=cur, norm_g=row(full["mix_norm_g"][i + 1]))
        else:
            cur = _mm_nn(act, full[("ffn_w_down", i)], 0, resid=cur)
        saved.append(rec)

    loss_vec, dy, dy_bf = _loss_grad(cur, tgt)
    loss = lax.psum(loss_vec[0, 0], ("x", "y", "c"))

    big_names = BIG_COL + BIG_ROW
    gbig = {}
    gsmall = {name: [None] * w[name].shape[0] for name in SMALL_SHARDED + SMALL_REPLICATED}

    def accumulate(name, layer, a, dy_, p_n):
        per_group = w[name].shape[0] // 2
        grp, slot = divmod(layer, per_group)
        gbig[(name, grp)] = _mm_tn(a, dy_, p_n, per_group, slot, gbig.get((name, grp)))

    def group_grads(grp):
        out = []
        for name in big_names:
            g4 = gbig[(name, grp)]
            if name in BIG_ROW:
                g4 = g4.reshape(g4.shape[0], N_CHIPS, g4.shape[2] // N_CHIPS, g4.shape[3])
            out.append(g4)
        return out

    def pair_sums_of(gs, mid_dtypes):
        return [_rs_pair_add(g, a, dt) for g, a, dt in zip(gs, _rs_pair(gs), mid_dtypes)]

    half_sums = {}
    late_pair_sums = None
    for i in reversed(range(n_layers)):
        j = i // 2
        rec = saved[i]
        dact = _mm_nt(dy_bf, full[("ffn_w_down", i)], 0, out_dtype=BF)
        accumulate("ffn_w_down", i, rec["act"], dy_bf, 1)
        dup, dwdw, dbdw = _ffn_mid_bwd(rec["up"], dact, ffn_wdw[i], row(full["ffn_b_dw"][i]))
        gsmall["ffn_w_dw"][i] = dwdw[:FFN_K]
        gsmall["ffn_b_dw"][i] = dbdw[0]
        accumulate("ffn_w_up", i, rec["h_ffn"], dup, N_CHIPS)
        dy, dy_bf, dg = _mm_nt_rms_bwd(dup, full[("ffn_w_up", i)], 0, rec["x_mid"], row(full["ffn_norm_g"][i]), dy)
        gsmall["ffn_norm_g"][i] = dg[0]
        if i % 2 == 0:
            dmix = _mm_nt(dy_bf, full["hyb_w_out"], j)
            accumulate("hyb_w_out", j, rec["mix"], dy_bf, 1)
            dq, dk, dv = _attn_bwd(rec["qkv"], rec["att_32"], dmix)
            dproj, dqg, dkg, dzg, dws, dbe = _mix_prep_bwd(
                rec["proj"], dq, dk, dv, dmix, rec["qg"], rec["kg"], rec["zg"], full["sg_w_spatial"], j, rec["bexp"],
                mean64, fold64)
            gsmall["sb_q_norm_g"][j] = dqg[0, :HEAD_DIM]
            gsmall["sb_k_norm_g"][j] = dkg[0, :HEAD_DIM]
            gsmall["sg_z_norm_g"][j] = dzg[0]
            gsmall["sg_w_spatial"][j] = dws
            gsmall["sg_b_spatial"][j] = dbe[:, ::HEAD_DIM].T
            dlast, w_first = dproj, full["sb_w_in"]
            accumulate("sb_w_in", j, rec["h_mix"], dproj, N_CHIPS)
        else:
            dys = _mm_nt(dy_bf, full["cv_w_pw2"], j, out_dtype=BF)
            accumulate("cv_w_pw2", j, rec["ys"], dy_bf, 1)
            carried = _chips_exchange(late_pair_sums) if late_pair_sums is not None else None
            res = _conf_mid_bwd(rec["p1"], rec["yc"], dys, dy, cv_wdw[j], row(full["cv_ln_g"][j]),
                                row(full["cv_ln_b"][j]), exchange=carried)
            if carried is not None:
                res, from_chips = res
                for name, ps, fc in zip(big_names, late_pair_sums, from_chips):
                    half_sums[(name, 1)] = _rs_chip_add(ps, fc)
                late_pair_sums = None
            dp1, dwdw, dbdw, dlg, dlb, db1, db2 = res
            gsmall["cv_w_dw"][j] = dwdw[:CONV_K]
            gsmall["cv_b_dw"][j] = dbdw[0]
            gsmall["cv_ln_g"][j] = dlg[0]
            gsmall["cv_ln_b"][j] = dlb[0]
            gsmall["cv_b_pw1"][j] = db1[0]
            gsmall["cv_b_pw2"][j] = db2[0]
            dlast, w_first = dp1, full["cv_w_pw1"]
            accumulate("cv_w_pw1", j, rec["h_mix"], dp1, N_CHIPS)
        dy, dy_bf, dg = _mm_nt_rms_bwd(dlast, w_first, j, rec["x_in"], row(full["mix_norm_g"][i]), dy)
        gsmall["mix_norm_g"][i] = dg[0]
        if i == n_layers // 2:
            late_pair_sums = pair_sums_of(group_grads(1), [BF] * len(big_names))

    small_names = SMALL_REPLICATED + SMALL_SHARDED
    small_full = [jnp.stack(gsmall[name]) for name in small_names]
    packed = _pack(small_full, 32 * N_CHIPS)
    rows_q = packed.shape[0] // N_CHIPS
    early = pair_sums_of(group_grads(0) + [packed.reshape(1, N_CHIPS, rows_q, LANES)], [BF] * len(big_names) + [F32])
    early_halves = [_rs_chip_add(ps, fc) for ps, fc in zip(early, _rs_chips(early))]
    parts = []
    for name, half0 in zip(big_names, early_halves):
        parts.append([(half0, 0), (half_sums[(name, 1)], half0.shape[0])])
    parts.append([(early_halves[-1], 0)])
    swapped = _rs_swap(parts)
    grads = dict(zip(big_names, swapped))
    summed = _all_gather(swapped[-1]).reshape(-1, LANES)
    for name, gsum in zip(small_names, _unpack(summed, [a.shape for a in small_full])):
        if name in SMALL_SHARDED:
            n_loc = w[name].shape[-1]
            split = gsum.reshape(gsum.shape[:-1] + (N_CHIPS, n_loc))
            gsum = lax.dynamic_index_in_dim(split, chip, axis=split.ndim - 2, keepdims=False)
        grads[name] = gsum

    delta, new_m, new_v = {}, {}, {}
    for name in BIG_COL + BIG_ROW:
        shp = w[name].shape
        two_d = lambda a: a.reshape(shp[0] * shp[1], shp[2])
        d, nm, nv = _adamw(two_d(w[name]), two_d(grads[name]), two_d(m[name]), two_d(v[name]))
        delta[name], new_m[name], new_v[name] = d.reshape(shp), nm.reshape(shp), nv.reshape(shp)
    shapes = [w[name].shape for name in small_names]
    d, nm, nv = _adamw(*(_pack([src[name] for name in small_names], 256) for src in (w, grads, m, v)))
    for name, a, b_, c_ in zip(small_names, _unpack(d, shapes), _unpack(nm, shapes), _unpack(nv, shapes)):
        delta[name], new_m[name], new_v[name] = a, b_, c_

    return (loss, dy, *[grads[n] for n in WEIGHTS], *[delta[n] for n in WEIGHTS],
            *[new_m[n] for n in WEIGHTS], *[new_v[n] for n in WEIGHTS])


def kernel(x, mix_norm_g, sb_w_in, sb_q_norm_g, sb_k_norm_g, sg_z_norm_g, sg_w_spatial, sg_b_spatial, hyb_w_out, cv_w_pw1, cv_b_pw1, cv_w_dw, cv_b_dw, cv_ln_g, cv_ln_b, cv_w_pw2, cv_b_pw2, ffn_norm_g, ffn_w_up, ffn_w_dw, ffn_b_dw, ffn_w_down, loss_target, m_mix_norm_g, m_sb_w_in, m_sb_q_norm_g, m_sb_k_norm_g, m_sg_z_norm_g, m_sg_w_spatial, m_sg_b_spatial, m_hyb_w_out, m_cv_w_pw1, m_cv_b_pw1, m_cv_w_dw, m_cv_b_dw, m_cv_ln_g, m_cv_ln_b, m_cv_w_pw2, m_cv_b_pw2, m_ffn_norm_g, m_ffn_w_up, m_ffn_w_dw, m_ffn_b_dw, m_ffn_w_down, v_mix_norm_g, v_sb_w_in, v_sb_q_norm_g, v_sb_k_norm_g, v_sg_z_norm_g, v_sg_w_spatial, v_sg_b_spatial, v_hyb_w_out, v_cv_w_pw1, v_cv_b_pw1, v_cv_w_dw, v_cv_b_dw, v_cv_ln_g, v_cv_ln_b, v_cv_w_pw2, v_cv_b_pw2, v_ffn_norm_g, v_ffn_w_up, v_ffn_w_dw, v_ffn_b_dw, v_ffn_w_down):
    given = dict(locals())
    w = {n: given[n] for n in WEIGHTS}
    m = {n: given["m_" + n] for n in WEIGHTS}
    v = {n: given["v_" + n] for n in WEIGHTS}
    out = _step(x[0], loss_target[0], w, m, v)
    return (out[0], out[1][None], *out[2:])
```

```python
import functools
from typing import Callable, NamedTuple

import jax
import jax.numpy as jnp
from jax import lax
from jax.experimental import pallas as pl
from jax.experimental.pallas import tpu as pltpu

F32 = jnp.float32
BF = jnp.bfloat16
SDS = jax.ShapeDtypeStruct
HI = lax.Precision.HIGHEST
MESH = pl.DeviceIdType.MESH

NORM_EPS = 1e-6
HEAD_DIM = 64
ATT_BLOCK = 128
CHUNK = 128
CONV_K = 31
CONV_HALO = 32
FFN_K = 3
FFN_HALO = 16
LANES = 128
N_CHIPS = 4
VMEM_LIMIT_BYTES = 56 * 2**20

ADAM_LR = 0.001
ADAM_B1 = 0.9
ADAM_B2 = 0.999
ADAM_EPS = 1e-08
ADAM_WD = 0.01
ADAM_STEP = 10

NT_DIMS = (((1,), (1,)), ((), ()))
TN_DIMS = (((0,), (0,)), ((), ()))


def _call(body, **kw):
    return pl.pallas_call(body, **kw)


def _params(*sem):
    return pltpu.CompilerParams(dimension_semantics=sem, vmem_limit_bytes=VMEM_LIMIT_BYTES)


def _gelu(x):
    return 0.5 * x * (1.0 + lax.erf(x * 0.7071067811865476))


def _rms(x, g):
    y = x * lax.rsqrt(jnp.mean(x * x, axis=-1, keepdims=True) + NORM_EPS)
    return y * g


def _rms_fwd(x, g):
    t, d = x.shape
    tm = min(512, t)

    def body(x_ref, g_ref, o_ref):
        o_ref[...] = _rms(x_ref[...], g_ref[...]).astype(o_ref.dtype)

    return _call(
        body, name="rms_fwd", grid=(t // tm,),
        in_specs=[pl.BlockSpec((tm, d), lambda i: (i, 0)), pl.BlockSpec((1, d), lambda i: (0, 0))],
        out_specs=pl.BlockSpec((tm, d), lambda i: (i, 0)),
        out_shape=SDS((t, d), BF), compiler_params=_params("parallel"))(x, g)


def _mm_nn(a, w, l, bias=None, resid=None, out_dtype=F32, gather=None, norm_g=None):
    m, k = a.shape
    _, p_n, kw, n = w.shape
    assert k == kw
    normed = norm_g is not None
    assert not normed or (p_n == 1 and not gather)
    tm = min(512 if normed else 1024, m)
    tn = n if (normed or k * n * 2 <= 4 * 2**20) else n // 2
    nj = n // tn
    in_specs = [pl.BlockSpec((tm, k), lambda i, p, j: (i, 0)),
                pl.BlockSpec((None, None, k, tn), lambda i, p, j: (l, p, 0, j))]
    args = [a, w]
    if bias is not None:
        in_specs.append(pl.BlockSpec((1, tn), lambda i, p, j: (0, p * nj + j)))
        args.append(bias)
    if resid is not None:
        in_specs.append(pl.BlockSpec((tm, tn), lambda i, p, j: (i, p * nj + j)))
        args.append(resid)
    if normed:
        in_specs.append(pl.BlockSpec((1, n), lambda i, p, j: (0, 0)))
        args.append(norm_g)
    n_in = len(args)

    def body(*refs):
        acc = jnp.dot(refs[0][...], refs[1][...], preferred_element_type=F32)
        nxt = 2
        if bias is not None:
            acc = acc + refs[nxt][...]
            nxt += 1
        if resid is not None:
            acc = refs[nxt][...] + acc
        refs[n_in][...] = acc.astype(refs[n_in].dtype)
        if normed:
            refs[n_in + 1][...] = _rms(acc, refs[n_in - 1][...]).astype(BF)

    out_spec = pl.BlockSpec((tm, tn), lambda i, p, j: (i, p * nj + j))
    kw = dict(name="mm_nn", grid=(m // tm, p_n, nj), in_specs=in_specs,
              out_specs=[out_spec, out_spec] if normed else out_spec,
              out_shape=[SDS((m, n), out_dtype), SDS((m, n), BF)] if normed else SDS((m, p_n * n), out_dtype))
    if gather:
        (out,), gathered = _call_gathering(body, gather, args, **kw)
        return out, gathered
    return _call(body, compiler_params=_params("parallel", "parallel", "parallel"), **kw)(*args)


def _mm_nt(dy, w, l, out_dtype=F32):
    m, n_all = dy.shape
    _, p_n, r, n = w.shape
    assert n_all == p_n * n
    tm = min(512, m)

    def body(dy_ref, w_ref, o_ref):
        acc = lax.dot_general(dy_ref[:, 0:n], w_ref[0], NT_DIMS, preferred_element_type=F32)
        for p in range(1, p_n):
            acc = acc + lax.dot_general(dy_ref[:, p * n:(p + 1) * n], w_ref[p], NT_DIMS, preferred_element_type=F32)
        o_ref[...] = acc.astype(o_ref.dtype)

    return _call(
        body, name="mm_nt", grid=(m // tm,),
        in_specs=[pl.BlockSpec((tm, n_all), lambda i: (i, 0)),
                  pl.BlockSpec((None, p_n, r, n), lambda i: (l, 0, 0, 0))],
        out_specs=pl.BlockSpec((tm, r), lambda i: (i, 0)),
        out_shape=SDS((m, r), out_dtype),
        compiler_params=_params("parallel"))(dy, w)


def _mm_nt_rms_bwd(dy, w, l, x, g, dres):
    m, n_all = dy.shape
    _, p_n, r, n = w.shape
    assert n_all == p_n * n and x.shape == (m, r)
    tm = min(256, m)

    def body(dy_ref, w_ref, x_ref, g_ref, r_ref, dx_ref, dxb_ref, dg_ref):
        dh = lax.dot_general(dy_ref[:, 0:n], w_ref[0], NT_DIMS, preferred_element_type=F32)
        for p in range(1, p_n):
            dh = dh + lax.dot_general(dy_ref[:, p * n:(p + 1) * n], w_ref[p], NT_DIMS, preferred_element_type=F32)
        _, vjp = jax.vjp(_rms, x_ref[...], g_ref[...])
        dx, dg = vjp(dh)
        dx = dx + r_ref[...]
        dx_ref[...] = dx
        dxb_ref[...] = dx.astype(BF)

        @pl.when(pl.program_id(0) == 0)
        def _():
            dg_ref[...] = jnp.zeros_like(dg_ref)

        dg_ref[...] += dg

    row = pl.BlockSpec((tm, r), lambda i: (i, 0))
    vec = pl.BlockSpec((1, r), lambda i: (0, 0))
    return _call(
        body, name="mm_nt_rms_bwd", grid=(m // tm,),
        in_specs=[pl.BlockSpec((tm, n_all), lambda i: (i, 0)), pl.BlockSpec((None, p_n, r, n), lambda i: (l, 0, 0, 0)),
                  row, vec, row],
        out_specs=[row, row, vec], out_shape=[SDS((m, r), F32), SDS((m, r), BF), SDS((1, r), F32)],
        compiler_params=_params("arbitrary"))(dy, w, x, g, dres)


def _mm_tn(a, dy, p_n, n_layers, l, buf=None):
    m, k = a.shape
    n = dy.shape[1] // p_n
    tm = min(2048, m)
    tk = k if k <= 1024 else k // 2
    nm = m // tm

    def body(a_ref, dy_ref, *rest):
        o_ref, acc_ref = rest[-2], rest[-1]
        mi = pl.program_id(2)
        part = lax.dot_general(a_ref[...], dy_ref[...], TN_DIMS, preferred_element_type=F32)

        @pl.when(mi == 0)
        def _():
            acc_ref[...] = part

        @pl.when(mi > 0)
        def _():
            acc_ref[...] += part

        @pl.when(mi == nm - 1)
        def _():
            o_ref[...] = acc_ref[...].astype(o_ref.dtype)

    in_specs = [pl.BlockSpec((tm, tk), lambda p, kk, mi: (mi, kk)),
                pl.BlockSpec((tm, n), lambda p, kk, mi: (mi, p))]
    args = [a, dy]
    aliases = {}
    if buf is not None:
        in_specs.append(pl.BlockSpec(memory_space=pl.ANY))
        args.append(buf)
        aliases = {2: 0}
    return _call(
        body, name="mm_tn", grid=(p_n, k // tk, nm), in_specs=in_specs,
        out_specs=pl.BlockSpec((None, None, tk, n), lambda p, kk, mi: (l, p, kk, 0)),
        out_shape=SDS((n_layers, p_n, k, n), BF), scratch_shapes=[pltpu.VMEM((tk, n), F32)],
        input_output_aliases=aliases,
        compiler_params=_params("parallel", "parallel", "arbitrary"))(*args)


def _loss_grad(y, tgt):
    t, d = y.shape
    tm = min(512, t)

    def body(y_ref, t_ref, l_ref, d_ref, db_ref):
        err = y_ref[...] - t_ref[...]
        dy = err * (1.0 / d)
        d_ref[...] = dy
        db_ref[...] = dy.astype(BF)
        part = 0.5 * jnp.sum(jnp.sum(err * err, axis=1, keepdims=True) * (1.0 / d), axis=0, keepdims=True)

        @pl.when(pl.program_id(0) == 0)
        def _():
            l_ref[...] = jnp.zeros_like(l_ref)

        l_ref[...] += jnp.broadcast_to(part, l_ref.shape)

    row = pl.BlockSpec((tm, d), lambda i: (i, 0))
    return _call(
        body, name="loss_grad", grid=(t // tm,), in_specs=[row, row],
        out_specs=[pl.BlockSpec((1, LANES), lambda i: (0, 0)), row, row],
        out_shape=[SDS((1, LANES), F32), SDS((t, d), F32), SDS((t, d), BF)],
        compiler_params=_params("arbitrary"))(y, tgt)


def _prev_halo(tr, halo, col):
    return lambda i: (jnp.maximum(i * (tr // halo) - 1, 0), col)


def _next_halo(tr, halo, n_rows, col):
    return lambda i: (jnp.minimum((i + 1) * (tr // halo), n_rows // halo - 1), col)


def _shifted_back(x):
    return pltpu.roll(x, 1, 0), pltpu.roll(x, 2, 0)


def _conv3(x, w_ref, b_ref, col):
    x1, x2 = _shifted_back(x)
    return b_ref[:, col] + w_ref[pl.ds(0, 1), col] * x2 + w_ref[pl.ds(1, 1), col] * x1 + w_ref[pl.ds(2, 1), col] * x


def _ffn_mid_fwd(up, w_dw, b_dw, gather=None):
    t, f2 = up.shape
    f = f2 // 2
    tr = min(256, t)
    h = FFN_HALO

    def body(g_ref, gp_ref, v_ref, w_ref, b_ref, o_ref):
        first_tile = pl.program_id(0) == 0

        def strip(c, carry):
            col = pl.ds(pl.multiple_of(c * LANES, LANES), LANES)
            x = jnp.concatenate([jnp.where(first_tile, 0.0, gp_ref[:, col].astype(F32)), g_ref[:, col].astype(F32)], axis=0)
            gc = _conv3(x, w_ref, b_ref, col)[h:]
            o_ref[:, col] = (gc * jax.nn.sigmoid(gc) * v_ref[:, col].astype(F32)).astype(o_ref.dtype)
            return carry

        lax.fori_loop(0, f // LANES, strip, 0)

    kw = dict(name="ffn_mid_fwd", grid=(t // tr,),
              in_specs=[pl.BlockSpec((tr, f), lambda i: (i, 0)), pl.BlockSpec((h, f), _prev_halo(tr, h, 0)),
                        pl.BlockSpec((tr, f), lambda i: (i, 1)),
                        pl.BlockSpec((8, f), lambda i: (0, 0)), pl.BlockSpec((1, f), lambda i: (0, 0))],
              out_specs=pl.BlockSpec((tr, f), lambda i: (i, 0)), out_shape=SDS((t, f), BF))
    args = (up, up, up, w_dw, b_dw)
    if gather:
        (out,), gathered = _call_gathering(body, gather, args, **kw)
        return out, gathered
    return _call(body, compiler_params=_params("parallel"), **kw)(*args)


def _ffn_mid_bwd(up, da, w_dw, b_dw):
    t, f2 = up.shape
    f = f2 // 2
    tr = min(256, t)
    h = FFN_HALO
    n_tiles = t // tr

    def body(g_ref, gp_ref, gn_ref, v_ref, vn_ref, da_ref, dan_ref, w_ref, b_ref, dup_ref, dw_ref, db_ref):
        i = pl.program_id(0)
        last = i == n_tiles - 1
        n = tr + h

        @pl.when(i == 0)
        def _():
            dw_ref[...] = jnp.zeros_like(dw_ref)
            db_ref[...] = jnp.zeros_like(db_ref)

        def rows(tile_ref, next_ref, col):
            return jnp.concatenate([tile_ref[:, col].astype(F32), next_ref[:, col].astype(F32)], axis=0)

        def strip(c, carry):
            col = pl.ds(pl.multiple_of(c * LANES, LANES), LANES)
            x = jnp.concatenate([jnp.where(i == 0, 0.0, gp_ref[:, col].astype(F32)), rows(g_ref, gn_ref, col)], axis=0)
            x1, x2 = _shifted_back(x)
            w0, w1, w2 = (w_ref[pl.ds(k, 1), col] for k in range(FFN_K))
            gc = (b_ref[:, col] + w0 * x2 + w1 * x1 + w2 * x)[h:]
            dav = rows(da_ref, dan_ref, col)
            sg = jax.nn.sigmoid(gc)
            dup_ref[:, pl.ds(pl.multiple_of(f + c * LANES, LANES), LANES)] = (dav * gc * sg)[:tr].astype(dup_ref.dtype)
            dgc = dav * rows(v_ref, vn_ref, col) * (sg * (1.0 + gc * (1.0 - sg)))
            dgc = jnp.concatenate([dgc[:tr], jnp.where(last, 0.0, dgc[tr:])], axis=0)
            d1, d2 = pltpu.roll(dgc, n - 1, 0), pltpu.roll(dgc, n - 2, 0)
            dup_ref[:, col] = (w2 * dgc + w1 * d1 + w0 * d2)[:tr].astype(dup_ref.dtype)
            dgt = dgc[:tr]
            for k, past in enumerate((x2, x1, x)):
                dw_ref[pl.ds(k, 1), col] += jnp.sum(past[h:h + tr] * dgt, axis=0, keepdims=True)
            db_ref[:, col] += jnp.sum(dgt, axis=0, keepdims=True)
            return carry

        lax.fori_loop(0, f // LANES, strip, 0)

    tile = lambda col: pl.BlockSpec((tr, f), lambda i: (i, col))
    nxt = lambda col: pl.BlockSpec((h, f), _next_halo(tr, h, t, col))
    return _call(
        body, name="ffn_mid_bwd", grid=(n_tiles,),
        in_specs=[tile(0), pl.BlockSpec((h, f), _prev_halo(tr, h, 0)), nxt(0), tile(1), nxt(1), tile(0), nxt(0),
                  pl.BlockSpec((8, f), lambda i: (0, 0)), pl.BlockSpec((1, f), lambda i: (0, 0))],
        out_specs=[pl.BlockSpec((tr, f2), lambda i: (i, 0)), pl.BlockSpec((8, f), lambda i: (0, 0)),
                   pl.BlockSpec((1, f), lambda i: (0, 0))],
        out_shape=[SDS((t, f2), BF), SDS((8, f), F32), SDS((1, f), F32)],
        compiler_params=_params("arbitrary"))(up, up, up, up, up, da, da, w_dw, b_dw)


def _ln_silu(yc, g, b):
    mu = jnp.mean(yc, axis=-1, keepdims=True)
    xc = yc - mu
    y = xc * lax.rsqrt(jnp.mean(xc * xc, axis=-1, keepdims=True) + NORM_EPS)
    return jax.nn.silu(y * g + b)


SUBLANES = 8
CONV_PAD = 24
SHIFT_CHUNK = 40
TAP_ROWS = 64


def _glu(a, g):
    return a.astype(F32) * jax.nn.sigmoid(g.astype(F32))


def _glu_strip(ygs_ref, first_tile, a_ref, ap_ref, g_ref, gp_ref, col, h, tr):
    ygs_ref[pl.ds(0, h), :] = jnp.where(first_tile, 0.0, _glu(ap_ref[:, col], gp_ref[:, col]))
    ygs_ref[pl.ds(h, tr), :] = _glu(a_ref[:, col], g_ref[:, col])


def _shift_past(sh_ref, ygs_ref, h, n):
    first = h - CONV_PAD - SUBLANES
    for u0 in range(0, n + CONV_PAD, SHIFT_CHUNK):
        x = ygs_ref[pl.ds(first + u0, SHIFT_CHUNK + SUBLANES), :]
        for r in range(1, SUBLANES):
            sh_ref[r, pl.ds(u0, SHIFT_CHUNK), :] = pltpu.roll(x, r, 0)[SUBLANES:]


def _past_rows(sh_ref, ygs_ref, h, n, s, row0=0):
    a, r = divmod(s, SUBLANES)
    if r == 0:
        return ygs_ref[pl.ds(row0 + h - SUBLANES * a, n), :]
    return sh_ref[r, pl.ds(row0 + CONV_PAD - SUBLANES * a, n), :]


def _conf_mid_fwd(p1, w_dw, b_dw, ln_g, ln_b, gather=None):
    t, w2 = p1.shape
    w = w2 // 2
    tr = min(256, t)
    h = CONV_HALO
    rc = 32

    def body(a_ref, ap_ref, g_ref, gp_ref, w_ref, b_ref, lg_ref, lb_ref, o_ref, yc_ref, ygs_ref, sh_ref):
        first_tile = pl.program_id(0) == 0

        def strip(c, carry):
            col = pl.ds(pl.multiple_of(c * LANES, LANES), LANES)
            _glu_strip(ygs_ref, first_tile, a_ref, ap_ref, g_ref, gp_ref, col, h, tr)
            _shift_past(sh_ref, ygs_ref, h, tr)
            acc = jnp.broadcast_to(b_ref[:, col], (tr, LANES))
            for k in range(CONV_K):
                acc = acc + w_ref[pl.ds(k, 1), col] * _past_rows(sh_ref, ygs_ref, h, tr, CONV_K - 1 - k)
            yc_ref[:, col] = acc
            return carry

        lax.fori_loop(0, w // LANES, strip, 0)

        def rows(r, carry):
            rs = pl.ds(pl.multiple_of(r * rc, rc), rc)
            o_ref[rs, :] = _ln_silu(yc_ref[rs, :], lg_ref[...], lb_ref[...]).astype(o_ref.dtype)
            return carry

        lax.fori_loop(0, tr // rc, rows, 0)

    vec = pl.BlockSpec((1, w), lambda i: (0, 0))
    tile = pl.BlockSpec((tr, w), lambda i: (i, 0))
    kw = dict(name="conf_mid_fwd", grid=(t // tr,),
              in_specs=[tile, pl.BlockSpec((h, w), _prev_halo(tr, h, 0)),
                        pl.BlockSpec((tr, w), lambda i: (i, 1)), pl.BlockSpec((h, w), _prev_halo(tr, h, 1)),
                        pl.BlockSpec((32, w), lambda i: (0, 0)), vec, vec, vec],
              out_specs=[tile, tile], out_shape=[SDS((t, w), BF), SDS((t, w), F32)],
              scratch_shapes=[pltpu.VMEM((h + tr, LANES), F32), pltpu.VMEM((SUBLANES, tr + CONV_PAD, LANES), F32)])
    args = (p1, p1, p1, p1, w_dw, b_dw, ln_g, ln_b)
    if gather:
        return _call_gathering(body, gather, args, **kw)
    return _call(body, compiler_params=_params("parallel"), **kw)(*args)


def _conf_mid_bwd(p1, yc, dys, dy, w_dw, ln_g, ln_b, exchange=None):
    t, w2 = p1.shape
    w = w2 // 2
    tr = min(256, t)
    h = CONV_HALO
    rc = 32
    n_tiles = t // tr

    def body(a_ref, ap_ref, g_ref, gp_ref, yc_ref, ycn_ref, dys_ref, dysn_ref, dy_ref, w_ref, lg_ref, lb_ref,
             dp_ref, dw_ref, db_ref, dlg_ref, dlb_ref, db1_ref, db2_ref, dyc_ref, ygs_ref, sh_ref, shf_ref, dwacc_ref):
        i = pl.program_id(0)
        last = i == n_tiles - 1

        @pl.when(i == 0)
        def _():
            for ref in (dw_ref, db_ref, dlg_ref, dlb_ref, db1_ref, db2_ref):
                ref[...] = jnp.zeros_like(ref)

        def ln_rows(r, carry):
            rs = pl.ds(pl.multiple_of(r * rc, rc), rc)
            _, vjp = jax.vjp(_ln_silu, yc_ref[rs, :], lg_ref[...], lb_ref[...])
            dyc, dlg, dlb = vjp(dys_ref[rs, :].astype(F32))
            dyc_ref[rs, :] = dyc
            dlg_ref[...] += dlg
            dlb_ref[...] += dlb
            return carry

        lax.fori_loop(0, tr // rc, ln_rows, 0)
        _, vjp = jax.vjp(_ln_silu, ycn_ref[...], lg_ref[...], lb_ref[...])
        dyc_ref[pl.ds(tr, h), :] = jnp.where(last, 0.0, vjp(dysn_ref[...].astype(F32))[0])
        db2_ref[...] += jnp.sum(dy_ref[...], axis=0, keepdims=True)

        def back(c, carry):
            col = pl.ds(pl.multiple_of(c * LANES, LANES), LANES)
            gcol = pl.ds(pl.multiple_of(w + c * LANES, LANES), LANES)
            _glu_strip(ygs_ref, i == 0, a_ref, ap_ref, g_ref, gp_ref, col, h, tr)
            _shift_past(sh_ref, ygs_ref, h, tr)
            for u0 in range(0, tr + CONV_PAD, SHIFT_CHUNK):
                part = dyc_ref[pl.ds(u0, SHIFT_CHUNK + SUBLANES), col]
                for r in range(1, SUBLANES):
                    shf_ref[r, pl.ds(u0, SHIFT_CHUNK), :] = pltpu.roll(part, SHIFT_CHUNK + SUBLANES - r, 0)[:SHIFT_CHUNK]
            for r0 in range(0, tr, TAP_ROWS):
                rows = pl.ds(r0, TAP_ROWS)
                dyc = dyc_ref[rows, col]
                dyg = jnp.zeros((TAP_ROWS, LANES), F32)
                for k in range(CONV_K):
                    s = CONV_K - 1 - k
                    a, r = divmod(s, SUBLANES)
                    if r == 0:
                        future = dyc_ref[pl.ds(r0 + SUBLANES * a, TAP_ROWS), col]
                    else:
                        future = shf_ref[r, pl.ds(r0 + SUBLANES * a, TAP_ROWS), :]
                    dyg = dyg + w_ref[pl.ds(k, 1), col] * future
                    prod = _past_rows(sh_ref, ygs_ref, h, TAP_ROWS, s, r0) * dyc
                    part = prod[0:SUBLANES]
                    for q in range(1, TAP_ROWS // SUBLANES):
                        part = part + prod[q * SUBLANES:(q + 1) * SUBLANES]
                    if r0 == 0:
                        dwacc_ref[k] = part
                    else:
                        dwacc_ref[k] += part
                sg = jax.nn.sigmoid(g_ref[rows, col].astype(F32))
                da = dyg * sg
                dg = dyg * a_ref[rows, col].astype(F32) * sg * (1.0 - sg)
                dp_ref[rows, col] = da.astype(dp_ref.dtype)
                dp_ref[rows, gcol] = dg.astype(dp_ref.dtype)
                db_ref[:, col] += jnp.sum(dyc, axis=0, keepdims=True)
                db1_ref[:, col] += jnp.sum(da, axis=0, keepdims=True)
                db1_ref[:, gcol] += jnp.sum(dg, axis=0, keepdims=True)
            for k in range(CONV_K):
                dw_ref[pl.ds(k, 1), col] += jnp.sum(dwacc_ref[k], axis=0, keepdims=True)
            return carry

        lax.fori_loop(0, w // LANES, back, 0)

    tile = lambda col: pl.BlockSpec((tr, w), lambda i: (i, col))
    prv = lambda col: pl.BlockSpec((h, w), _prev_halo(tr, h, col))
    nxt = pl.BlockSpec((h, w), _next_halo(tr, h, t, 0))
    vec = pl.BlockSpec((1, w), lambda i: (0, 0))
    kw = dict(
        name="conf_mid_bwd", grid=(n_tiles,),
        in_specs=[tile(0), prv(0), tile(1), prv(1), tile(0), nxt, tile(0), nxt, tile(0),
                  pl.BlockSpec((32, w), lambda i: (0, 0)), vec, vec],
        out_specs=[pl.BlockSpec((tr, w2), lambda i: (i, 0)), pl.BlockSpec((32, w), lambda i: (0, 0)), vec, vec, vec,
                   pl.BlockSpec((1, w2), lambda i: (0, 0)), vec],
        out_shape=[SDS((t, w2), BF), SDS((32, w), F32), SDS((1, w), F32), SDS((1, w), F32), SDS((1, w), F32),
                   SDS((1, w2), F32), SDS((1, w), F32)],
        scratch_shapes=[pltpu.VMEM((tr + h, w), F32), pltpu.VMEM((h + tr, LANES), F32),
                        pltpu.VMEM((SUBLANES, tr + CONV_PAD, LANES), F32), pltpu.VMEM((SUBLANES, tr + CONV_PAD, LANES), F32),
                        pltpu.VMEM((32, SUBLANES, LANES), F32)])
    args = (p1, p1, p1, p1, yc, yc, dys, dys, dy, w_dw, ln_g, ln_b)
    if exchange is not None:
        return _call_hosting(body, exchange, args, **kw)
    return _call(body, compiler_params=_params("arbitrary"), **kw)(*args)


def _group_matrices():
    i = lax.broadcasted_iota(jnp.int32, (512, 512), 0)
    j = lax.broadcasted_iota(jnp.int32, (512, 512), 1)
    mean64 = jnp.where(i // HEAD_DIM == j // HEAD_DIM, 1.0 / HEAD_DIM, 0.0).astype(F32)
    fold64 = jnp.where(i % HEAD_DIM == j % HEAD_DIM, 1.0, 0.0).astype(F32)
    return mean64, fold64


def _split_dot(x, mat):
    hi = x.astype(BF)
    lo = (x - hi.astype(F32)).astype(BF)
    mb = mat.astype(BF)
    return jnp.dot(hi, mb, preferred_element_type=F32) + jnp.dot(lo, mb, preferred_element_type=F32)


@jax.custom_vjp
def _group_sum(x, mat):
    return _split_dot(x, mat)


_group_sum.defvjp(lambda x, mat: (_split_dot(x, mat), mat), lambda mat, ct: (_split_dot(ct, mat), jnp.zeros_like(mat)))


def _bf_dot_plain(a, b):
    return jnp.dot(a.astype(BF), b.astype(BF), preferred_element_type=F32)


@jax.custom_vjp
def _bf_dot(a, b):
    return _bf_dot_plain(a, b)


def _bf_dot_bwd(res, ct):
    a, b = res
    cb = ct.astype(BF)
    return (lax.dot_general(cb, b.astype(BF), NT_DIMS, preferred_element_type=F32),
            lax.dot_general(a.astype(BF), cb, TN_DIMS, preferred_element_type=F32))


_bf_dot.defvjp(lambda a, b: (_bf_dot_plain(a, b), (a, b)), _bf_dot_bwd)


def _prep_tile(proj, qg, kg, zg, ws, bexp, mean64, differentiated=False):
    sw = 512
    q, k, v, u, z = (proj[:, n * sw:(n + 1) * sw] for n in range(5))
    group_sum, dot = (_group_sum, _bf_dot) if differentiated else (_split_dot, _bf_dot_plain)

    def group_norm(x):
        return x * lax.rsqrt(group_sum(x * x, mean64) + NORM_EPS)

    qn = group_norm(q) * qg
    kn = group_norm(k) * kg
    zn = group_norm(_gelu(z)) * zg
    row = lax.broadcasted_iota(jnp.int32, (CHUNK, CHUNK), 0)
    col = lax.broadcasted_iota(jnp.int32, (CHUNK, CHUNK), 1)
    first = lax.broadcasted_iota(jnp.int32, (1, LANES), 1) < HEAD_DIM
    parts = []
    for pr in range(sw // LANES):
        zp = zn[:, pr * LANES:(pr + 1) * LANES]
        s0 = dot(jnp.where(col <= row, ws[2 * pr], 0.0), zp)
        s1 = dot(jnp.where(col <= row, ws[2 * pr + 1], 0.0), zp)
        parts.append(jnp.where(first, s0, s1))
    s = jnp.concatenate(parts, axis=1) + bexp
    return qn, kn, v, _gelu(u) * s


def _mix_prep_fwd(proj, qg, kg, zg, w_s, l, bexp, mean64, gather=None):
    t = proj.shape[0]
    tr = CHUNK

    def body(p_ref, qg_ref, kg_ref, zg_ref, ws_ref, be_ref, m_ref, qkv_ref, go_ref):
        qn, kn, v, go = _prep_tile(p_ref[...], qg_ref[...], kg_ref[...], zg_ref[...], ws_ref[...], be_ref[...], m_ref[...])
        qkv_ref[:, 0:512] = qn.astype(BF)
        qkv_ref[:, 512:1024] = kn.astype(BF)
        qkv_ref[:, 1024:1536] = v.astype(BF)
        go_ref[...] = go.astype(BF)

    vec = pl.BlockSpec((1, 512), lambda i: (0, 0))
    kw = dict(name="mix_prep_fwd", grid=(t // tr,),
              in_specs=[pl.BlockSpec((tr, 2560), lambda i: (i, 0)), vec, vec, vec,
                        pl.BlockSpec((None, 8, CHUNK, CHUNK), lambda i: (l, 0, 0, 0)),
                        pl.BlockSpec((CHUNK, 512), lambda i: (0, 0)), pl.BlockSpec((512, 512), lambda i: (0, 0))],
              out_specs=[pl.BlockSpec((tr, 1536), lambda i: (i, 0)), pl.BlockSpec((tr, 512), lambda i: (i, 0))],
              out_shape=[SDS((t, 1536), BF), SDS((t, 512), BF)])
    args = (proj, qg, kg, zg, w_s, bexp, mean64)
    if gather:
        return _call_gathering(body, gather, args, **kw)
    return _call(body, compiler_params=_params("parallel"), **kw)(*args)


def _mix_prep_bwd(proj, dq, dk, dv, dmix, qg, kg, zg, w_s, l, bexp, mean64, fold64):
    t = proj.shape[0]
    tr = CHUNK
    n_tiles = t // tr

    def body(p_ref, dq_ref, dk_ref, dv_ref, dgo_ref, qg_ref, kg_ref, zg_ref, ws_ref, be_ref, m_ref, f_ref,
             dp_ref, dqg_ref, dkg_ref, dzg_ref, dws_ref, dbe_ref):
        i = pl.program_id(0)

        @pl.when(i == 0)
        def _():
            for ref in (dqg_ref, dkg_ref, dzg_ref, dws_ref, dbe_ref):
                ref[...] = jnp.zeros_like(ref)

        fn = functools.partial(_prep_tile, mean64=m_ref[...], differentiated=True)
        _, vjp = jax.vjp(fn, p_ref[...], qg_ref[...], kg_ref[...], zg_ref[...], ws_ref[...], be_ref[...])
        dp, dqg, dkg, dzg, dws, dbe = vjp((dq_ref[...], dk_ref[...], dv_ref[...], dgo_ref[...]))
        dp_ref[...] = dp.astype(BF)
        dqg_ref[pl.ds(0, 1), :] += dqg
        dkg_ref[pl.ds(0, 1), :] += dkg
        dzg_ref[pl.ds(0, 1), :] += dzg
        dws_ref[...] += dws
        dbe_ref[...] += dbe

        @pl.when(i == n_tiles - 1)
        def _():
            dqg_ref[...] = jnp.dot(dqg_ref[...], f_ref[...], precision=HI, preferred_element_type=F32)
            dkg_ref[...] = jnp.dot(dkg_ref[...], f_ref[...], precision=HI, preferred_element_type=F32)
            dbe_ref[...] = jnp.dot(dbe_ref[...], m_ref[...] * float(HEAD_DIM), precision=HI, preferred_element_type=F32)

    vec = pl.BlockSpec((1, 512), lambda i: (0, 0))
    acc = pl.BlockSpec((8, 512), lambda i: (0, 0))
    sq = pl.BlockSpec((512, 512), lambda i: (0, 0))
    row = pl.BlockSpec((tr, 512), lambda i: (i, 0))
    return _call(
        body, name="mix_prep_bwd", grid=(n_tiles,),
        in_specs=[pl.BlockSpec((tr, 2560), lambda i: (i, 0)), row, row, row, pl.BlockSpec((tr, 512), lambda i: (i, 1)),
                  vec, vec, vec, pl.BlockSpec((None, 8, CHUNK, CHUNK), lambda i: (l, 0, 0, 0)),
                  pl.BlockSpec((CHUNK, 512), lambda i: (0, 0)), sq, sq],
        out_specs=[pl.BlockSpec((tr, 2560), lambda i: (i, 0)), acc, acc, acc,
                   pl.BlockSpec((8, CHUNK, CHUNK), lambda i: (0, 0, 0)), pl.BlockSpec((CHUNK, 512), lambda i: (0, 0))],
        out_shape=[SDS((t, 2560), BF), SDS((8, 512), F32), SDS((8, 512), F32), SDS((8, 512), F32),
                   SDS((8, CHUNK, CHUNK), F32), SDS((CHUNK, 512), F32)],
        compiler_params=_params("arbitrary"))(proj, dq, dk, dv, dmix, qg, kg, zg, w_s, bexp, mean64, fold64)


def _sb_logs(qh, kb, valid):
    z = lax.dot_general(qh, kb, NT_DIMS, preferred_element_type=F32) * (HEAD_DIM ** -0.5)
    soft = jnp.log1p(jnp.exp(-jnp.abs(z)))
    lk_raw = -(jnp.maximum(z, 0.0) + soft)
    ls = -(jnp.maximum(-z, 0.0) + soft)
    return lk_raw, ls, jnp.where(valid, lk_raw, 0.0)


def _sb_weights(ls, run, tail, valid):
    return jnp.where(valid, jnp.exp(ls + run + tail), 0.0)


def _att_masks(b):
    row = lax.broadcasted_iota(jnp.int32, (b, b), 0)
    col = lax.broadcasted_iota(jnp.int32, (b, b), 1)
    first = lax.broadcasted_iota(jnp.int32, (1, LANES), 1) < HEAD_DIM
    return row, col, first


N_PAIRS = 4


def _load_kv(qkv_hbm, k_scr, v_scr, sems, group, width):
    ck = pltpu.make_async_copy(qkv_hbm.at[:, pl.ds(pl.multiple_of(512 + group * width, LANES), width)], k_scr, sems.at[0])
    cv = pltpu.make_async_copy(qkv_hbm.at[:, pl.ds(pl.multiple_of(1024 + group * width, LANES), width)], v_scr, sems.at[1])
    ck.start()
    cv.start()
    ck.wait()
    cv.wait()


def _split_heads(ref, pair, first):
    x = ref[:, pair * LANES:(pair + 1) * LANES]
    zero = jnp.zeros_like(x)
    return jnp.where(first, x, zero), jnp.where(first, zero, x)


def _any_weight_left(run_ref, n_heads):
    top = run_ref[0]
    for hh in range(1, n_heads):
        top = jnp.maximum(top, run_ref[hh])
    return jnp.max(jnp.exp(top)) > 0.0


def _attn_fwd(qkv, pairs_per_step=4, gather=None):
    t = qkv.shape[0]
    b = ATT_BLOCK
    nq = t // b
    width = pairs_per_step * LANES
    n_heads = 2 * pairs_per_step

    def body(q_ref, qkv_hbm, ob_ref, o32_ref, k_scr, v_scr, acc_ref, run_ref, sems):
        group, qi = pl.program_id(0), pl.program_id(1)

        @pl.when(qi == 0)
        def _():
            _load_kv(qkv_hbm, k_scr, v_scr, sems, group, width)

        row, col, first = _att_masks(b)
        qh = [x for pr in range(pairs_per_step) for x in _split_heads(q_ref, pr, first)]
        upper = jnp.where(row > col, 1.0, 0.0).astype(BF)
        acc_ref[...] = jnp.zeros_like(acc_ref)
        run_ref[...] = jnp.zeros_like(run_ref)
        heads = range(n_heads)

        def step(carry):
            j, _ = carry
            rows = pl.ds(pl.multiple_of(j * b, b), b)
            valid = jnp.logical_or(j != qi, col < row)
            lanes = [pl.ds((hh // 2) * LANES, LANES) for hh in heads]
            logs = [_sb_logs(qh[hh], k_scr[rows, lanes[hh]], valid) for hh in heads]
            tails = [_split_dot(logs[hh][2], upper) for hh in heads]
            for hh in heads:
                wgt = _sb_weights(logs[hh][1], run_ref[hh], tails[hh], valid)
                acc_ref[hh] += jnp.dot(wgt.astype(BF), v_scr[rows, lanes[hh]], preferred_element_type=F32)
            for hh in heads:
                run_ref[hh] += jnp.sum(logs[hh][2], axis=1, keepdims=True)
            return j - 1, _any_weight_left(run_ref, n_heads)

        lax.while_loop(lambda c: jnp.logical_and(c[0] >= 0, c[1]), step, (qi, jnp.bool_(True)))
        for pr in range(pairs_per_step):
            out = jnp.where(first, acc_ref[2 * pr], acc_ref[2 * pr + 1])
            ob_ref[:, pr * LANES:(pr + 1) * LANES] = out.astype(BF)
            o32_ref[:, pr * LANES:(pr + 1) * LANES] = out

    blk = pl.BlockSpec((b, width), lambda g, qi: (qi, g))
    kw = dict(name="attn_fwd", grid=(N_PAIRS // pairs_per_step, nq),
              in_specs=[blk, pl.BlockSpec(memory_space=pl.ANY)], out_specs=[blk, blk],
              out_shape=[SDS((t, 512), BF), SDS((t, 512), F32)],
              scratch_shapes=[pltpu.VMEM((t, width), BF), pltpu.VMEM((t, width), BF),
                              pltpu.VMEM((n_heads, b, LANES), F32), pltpu.VMEM((n_heads, b, 1), F32),
                              pltpu.SemaphoreType.DMA((2,))])
    if gather:
        return _call_gathering(body, gather, (qkv, qkv), **kw)
    return _call(body, compiler_params=_params("arbitrary", "arbitrary"), **kw)(qkv, qkv)


def _attn_bwd(qkv, a32, dmix, pairs_per_step=2):
    t = qkv.shape[0]
    b = ATT_BLOCK
    nq = t // b
    width = pairs_per_step * LANES
    n_heads = 2 * pairs_per_step

    def body(q_ref, a_ref, da_ref, qkv_hbm, dq_ref, dk_hbm, dv_hbm,
             k_scr, v_scr, dk_scr, dv_scr, dqa_ref, run_ref, rung_ref, sems):
        group, qi = pl.program_id(0), pl.program_id(1)

        @pl.when(qi == 0)
        def _():
            _load_kv(qkv_hbm, k_scr, v_scr, sems, group, width)
            dk_scr[...] = jnp.zeros_like(dk_scr)
            dv_scr[...] = jnp.zeros_like(dv_scr)

        row, col, first = _att_masks(b)
        qh, dah, dtot = [], [], []
        for pr in range(pairs_per_step):
            qh += _split_heads(q_ref, pr, first)
            da = da_ref[:, pr * LANES:(pr + 1) * LANES]
            prod = da * a_ref[:, pr * LANES:(pr + 1) * LANES]
            dtot += [jnp.sum(jnp.where(first, prod, 0.0), axis=1, keepdims=True),
                     jnp.sum(jnp.where(first, 0.0, prod), axis=1, keepdims=True)]
            dah += [jnp.where(first, da, 0.0).astype(BF), jnp.where(first, 0.0, da).astype(BF)]
        upper = jnp.where(row > col, 1.0, 0.0).astype(BF)
        lower_incl = jnp.where(row >= col, 1.0, 0.0).astype(BF)
        dqa_ref[...] = jnp.zeros_like(dqa_ref)
        run_ref[...] = jnp.zeros_like(run_ref)
        rung_ref[...] = jnp.zeros_like(rung_ref)
        heads = range(n_heads)

        def step(carry):
            j, _ = carry
            rows = pl.ds(pl.multiple_of(j * b, b), b)
            valid = jnp.logical_or(j != qi, col < row)
            lanes = [pl.ds((hh // 2) * LANES, LANES) for hh in heads]
            logs = [_sb_logs(qh[hh], k_scr[rows, lanes[hh]], valid) for hh in heads]
            dps = [lax.dot_general(dah[hh], v_scr[rows, lanes[hh]], NT_DIMS, preferred_element_type=F32) for hh in heads]
            tails = [_split_dot(logs[hh][2], upper) for hh in heads]
            wgts = [_sb_weights(logs[hh][1], run_ref[hh], tails[hh], valid) for hh in heads]
            gs = [wgts[hh] * dps[hh] for hh in heads]
            g_froms = [_split_dot(gs[hh], lower_incl) for hh in heads]
            for hh in heads:
                lk_raw, ls, _ = logs[hh]
                dlk = jnp.where(valid, dtot[hh] - rung_ref[hh] - g_froms[hh], 0.0)
                dz = ((gs[hh] * jnp.exp(lk_raw) - dlk * jnp.exp(ls)) * (HEAD_DIM ** -0.5)).astype(BF)
                dqa_ref[hh] += jnp.dot(dz, k_scr[rows, lanes[hh]], preferred_element_type=F32)
                dk_scr[rows, lanes[hh]] += lax.dot_general(dz, qh[hh], TN_DIMS, preferred_element_type=F32)
                dv_scr[rows, lanes[hh]] += lax.dot_general(wgts[hh].astype(BF), dah[hh], TN_DIMS, preferred_element_type=F32)
            for hh in heads:
                rung_ref[hh] += jnp.sum(gs[hh], axis=1, keepdims=True)
                run_ref[hh] += jnp.sum(logs[hh][2], axis=1, keepdims=True)
            return j - 1, _any_weight_left(run_ref, n_heads)

        lax.while_loop(lambda c: jnp.logical_and(c[0] >= 0, c[1]), step, (qi, jnp.bool_(True)))
        for pr in range(pairs_per_step):
            dq_ref[:, pr * LANES:(pr + 1) * LANES] = jnp.where(first, dqa_ref[2 * pr], dqa_ref[2 * pr + 1])

        @pl.when(qi == nq - 1)
        def _():
            cols = pl.ds(pl.multiple_of(group * width, LANES), width)
            ck = pltpu.make_async_copy(dk_scr, dk_hbm.at[:, cols], sems.at[0])
            cv = pltpu.make_async_copy(dv_scr, dv_hbm.at[:, cols], sems.at[1])
            ck.start()
            cv.start()
            ck.wait()
            cv.wait()

    blk = pl.BlockSpec((b, width), lambda g, qi: (qi, g))
    anywhere = pl.BlockSpec(memory_space=pl.ANY)
    return _call(
        body, name="attn_bwd", grid=(N_PAIRS // pairs_per_step, nq),
        in_specs=[blk, blk, blk, anywhere], out_specs=[blk, anywhere, anywhere],
        out_shape=[SDS((t, 512), F32), SDS((t, 512), F32), SDS((t, 512), F32)],
        scratch_shapes=[pltpu.VMEM((t, width), BF), pltpu.VMEM((t, width), BF),
                        pltpu.VMEM((t, width), F32), pltpu.VMEM((t, width), F32),
                        pltpu.VMEM((n_heads, b, LANES), F32), pltpu.VMEM((n_heads, b, 1), F32),
                        pltpu.VMEM((n_heads, b, 1), F32), pltpu.SemaphoreType.DMA((2,))],
        compiler_params=_params("arbitrary", "arbitrary"))(qkv, a32, dmix, qkv)


def _adamw(w, g, m, v):
    n, c = w.shape
    tr = min(256, n)
    assert n % tr == 0

    def body(w_ref, g_ref, m_ref, v_ref, d_ref, nm_ref, nv_ref):
        g = g_ref[...]
        m = ADAM_B1 * m_ref[...] + (1.0 - ADAM_B1) * g
        v = ADAM_B2 * v_ref[...] + (1.0 - ADAM_B2) * jnp.square(g)
        m_hat = m / (1.0 - ADAM_B1 ** ADAM_STEP)
        v_hat = v / (1.0 - ADAM_B2 ** ADAM_STEP)
        d_ref[...] = -ADAM_LR * (m_hat / (jnp.sqrt(v_hat) + ADAM_EPS) + ADAM_WD * w_ref[...])
        nm_ref[...] = m
        nv_ref[...] = v

    blk = pl.BlockSpec((tr, c), lambda i: (i, 0))
    return _call(
        body, name="adamw", grid=(n // tr,), in_specs=[blk] * 4, out_specs=[blk] * 3,
        out_shape=[SDS((n, c), F32)] * 3, compiler_params=_params("parallel"))(w, g, m, v)


def _mesh_pos():
    return lax.axis_index("x"), lax.axis_index("y"), lax.axis_index("c")


def _other_chips(x, y):
    return [(1 - x, y), (x, 1 - y), (1 - x, 1 - y)]


HBM_SPEC = pl.BlockSpec(memory_space=pltpu.HBM)


GATHER_COPIES = 6


def _gather_steps(s_ref, o_ref, send_sems, recv_sems, local_sems, slot):
    h = s_ref.shape[1] // 2
    x, y, c = _mesh_pos()
    sibling = (x, y, 1 - c)
    chips = _other_chips(x, y)
    base = GATHER_COPIES * slot

    def half(px, py, hc):
        return o_ref.at[:, 2 * px + py, pl.ds(hc * h, h), :]

    def copy(k, dst, to, src=None):
        return pltpu.make_async_remote_copy(
            src_ref=dst if src is None else src, dst_ref=dst, send_sem=send_sems.at[base + k],
            recv_sem=recv_sems.at[base + k], device_id=to, device_id_type=MESH)

    mine = pltpu.make_async_copy(s_ref, o_ref.at[:, 2 * x + y], local_sems.at[slot])
    first = [copy(j, half(x, y, c), (*chip, c), src=s_ref.at[:, pl.ds(c * h, h), :]) for j, chip in enumerate(chips)]
    passed = [copy(3 + j, half(*chip, c), sibling) for j, chip in enumerate(chips)]

    def start():
        mine.start()
        for cp in first:
            cp.start()

    def finish():
        for j, chip in enumerate(chips):
            copy(j, half(*chip, c), (x, y, c)).wait_recv()
            passed[j].start()
        for j, chip in enumerate(chips):
            copy(3 + j, half(*chip, 1 - c), (x, y, c)).wait_recv()
        for cp in first + passed:
            cp.wait_send()
        mine.wait()

    return start, finish


def _gather_scratch(n):
    return [pltpu.SemaphoreType.DMA((GATHER_COPIES * n,)), pltpu.SemaphoreType.DMA((GATHER_COPIES * n,)),
            pltpu.SemaphoreType.DMA((n,))]


def _gathered_shape(shard):
    n_l, r, c_w = shard.shape
    return SDS((n_l, N_CHIPS, r, c_w), shard.dtype)


def _all_gather(shard):
    def body(s_ref, o_ref, send_sems, recv_sems, local_sems):
        start, finish = _gather_steps(s_ref, o_ref, send_sems, recv_sems, local_sems, 0)
        start()
        finish()

    return _call(body, name="all_gather", in_specs=[HBM_SPEC], out_specs=HBM_SPEC, out_shape=_gathered_shape(shard),
                 scratch_shapes=_gather_scratch(1))(shard)


class _Exchange(NamedTuple):
    tag: str
    inputs: list
    out_shapes: list
    scratch: list
    make_steps: Callable


def _gather_exchange(shards):
    n = len(shards)

    def make_steps(s_refs, o_refs, sems):
        steps = [_gather_steps(s_refs[k], o_refs[k], *sems, k) for k in range(n)]
        return (lambda: [start() for start, _ in steps]), (lambda: [finish() for _, finish in steps])

    return _Exchange("gathering", list(shards), [_gathered_shape(s) for s in shards], _gather_scratch(n), make_steps)


def _call_hosting(body, exchange, args, *, name, grid, in_specs, out_specs, out_shape, scratch_shapes=()):
    out_specs = list(out_specs) if isinstance(out_specs, (list, tuple)) else [out_specs]
    out_shape = list(out_shape) if isinstance(out_shape, (list, tuple)) else [out_shape]
    n_in, n_out, n_scr = len(in_specs), len(out_specs), len(scratch_shapes)
    n_xi, n_xo, n_sem = len(exchange.inputs), len(exchange.out_shapes), len(exchange.scratch)

    def hosting_body(*refs):
        ins, x_ins = refs[:n_in], refs[n_in:n_in + n_xi]
        outs = refs[n_in + n_xi:n_in + n_xi + n_out]
        x_outs = refs[n_in + n_xi + n_out:n_in + n_xi + n_out + n_xo]
        scratch = refs[n_in + n_xi + n_out + n_xo:n_in + n_xi + n_out + n_xo + n_scr]
        start, finish = exchange.make_steps(x_ins, x_outs, refs[len(refs) - n_sem:])
        is_first = functools.reduce(jnp.logical_and, [pl.program_id(a) == 0 for a in range(len(grid))])
        is_last = functools.reduce(jnp.logical_and, [pl.program_id(a) == grid[a] - 1 for a in range(len(grid))])

        @pl.when(is_first)
        def _():
            start()

        body(*ins, *outs, *scratch)

        @pl.when(is_last)
        def _():
            finish()

    res = _call(
        hosting_body, name=name + "_" + exchange.tag, grid=grid, in_specs=list(in_specs) + [HBM_SPEC] * n_xi,
        out_specs=out_specs + [HBM_SPEC] * n_xo, out_shape=out_shape + list(exchange.out_shapes),
        scratch_shapes=list(scratch_shapes) + list(exchange.scratch),
        compiler_params=_params(*(["arbitrary"] * len(grid))))(*args, *exchange.inputs)
    return res[:n_out], res[n_out:]


def _call_gathering(body, shards, args, **kw):
    return _call_hosting(body, _gather_exchange(shards), args, **kw)


def _row_tile(h):
    for cand in (256, 176, 128, 64, 32, 16):
        if h % cand == 0:
            return cand
    raise ValueError(h)


def _rs_pair(gs):
    n = len(gs)

    def body(*refs):
        g_refs, a_refs, (send_sems, recv_sems) = refs[:n], refs[n:2 * n], refs[2 * n:]
        x, y, c = _mesh_pos()
        cps = []
        for k in range(n):
            h = g_refs[k].shape[2] // 2
            cps.append(pltpu.make_async_remote_copy(
                src_ref=g_refs[k].at[:, :, pl.ds((1 - c) * h, h), :], dst_ref=a_refs[k], send_sem=send_sems.at[k],
                recv_sem=recv_sems.at[k], device_id=(x, y, 1 - c), device_id_type=MESH))
        for cp in cps:
            cp.start()
        for cp in cps:
            cp.wait()

    out_shape = [SDS((g.shape[0], g.shape[1], g.shape[2] // 2, g.shape[3]), g.dtype) for g in gs]
    return _call(body, name="rs_pair", in_specs=[HBM_SPEC] * n, out_specs=[HBM_SPEC] * n, out_shape=out_shape,
                 scratch_shapes=[pltpu.SemaphoreType.DMA((n,)), pltpu.SemaphoreType.DMA((n,))])(*gs)


def _rs_pair_add(g, from_sibling, mid_dtype):
    n_l, n_p, r, c_w = g.shape
    h = r // 2
    tr = _row_tile(h)
    nt = h // tr
    c_arr = jnp.reshape(lax.axis_index("c"), (1,)).astype(jnp.int32)

    def body(c_ref, g_ref, a_ref, o_ref):
        o_ref[...] = (g_ref[...].astype(F32) + a_ref[...].astype(F32)).astype(o_ref.dtype)

    blk = (None, None, tr, c_w)
    return _call(
        body, name="rs_pair_add",
        grid_spec=pltpu.PrefetchScalarGridSpec(
            num_scalar_prefetch=1, grid=(n_l, n_p, nt),
            in_specs=[pl.BlockSpec(blk, lambda l, p, t, c_ref: (l, p, c_ref[0] * nt + t, 0)),
                      pl.BlockSpec(blk, lambda l, p, t, c_ref: (l, p, t, 0))],
            out_specs=pl.BlockSpec(blk, lambda l, p, t, c_ref: (l, p, t, 0))),
        out_shape=SDS((n_l, n_p, h, c_w), mid_dtype),
        compiler_params=_params("parallel", "parallel", "parallel"))(c_arr, g, from_sibling)


def _chips_exchange(pair_sums):
    n = len(pair_sums)

    def make_steps(s_refs, b_refs, sems):
        send_sems, recv_sems = sems
        x, y, c = _mesh_pos()
        cps = [pltpu.make_async_remote_copy(
            src_ref=s_refs[k].at[:, 2 * chip[0] + chip[1]], dst_ref=b_refs[k].at[j], send_sem=send_sems.at[3 * k + j],
            recv_sem=recv_sems.at[3 * k + j], device_id=(*chip, c), device_id_type=MESH)
            for k in range(n) for j, chip in enumerate(_other_chips(x, y))]
        return (lambda: [cp.start() for cp in cps]), (lambda: [cp.wait() for cp in cps])

    out_shapes = [SDS((3, s.shape[0], s.shape[2], s.shape[3]), s.dtype) for s in pair_sums]
    sems = [pltpu.SemaphoreType.DMA((3 * n,)), pltpu.SemaphoreType.DMA((3 * n,))]
    return _Exchange("scattering", list(pair_sums), out_shapes, sems, make_steps)


def _rs_chips(pair_sums):
    ex = _chips_exchange(pair_sums)
    n = len(pair_sums)

    def body(*refs):
        start, finish = ex.make_steps(refs[:n], refs[n:2 * n], refs[2 * n:])
        start()
        finish()

    return _call(body, name="rs_chips", in_specs=[HBM_SPEC] * n, out_specs=[HBM_SPEC] * n, out_shape=ex.out_shapes,
                 scratch_shapes=ex.scratch)(*pair_sums)


def _rs_chip_add(pair_sum, from_chips, n_layers, first_layer, buf=None):
    n_l, _, h, c_w = pair_sum.shape
    tr = _row_tile(h)
    nt = h // tr
    p_arr = jnp.reshape(2 * lax.axis_index("x") + lax.axis_index("y"), (1,)).astype(jnp.int32)
    c_arr = jnp.reshape(lax.axis_index("c"), (1,)).astype(jnp.int32)

    def body(p_ref, c_ref, s_ref, b_ref, *rest):
        acc = s_ref[...].astype(F32)
        for j in range(3):
            acc = acc + b_ref[j].astype(F32)
        rest[-1][...] = acc

    in_specs = [pl.BlockSpec((None, None, tr, c_w), lambda l, t, p_ref, c_ref: (l, p_ref[0], t, 0)),
                pl.BlockSpec((3, None, tr, c_w), lambda l, t, p_ref, c_ref: (0, l, t, 0))]
    args = [p_arr, c_arr, pair_sum, from_chips]
    aliases = {}
    if buf is not None:
        in_specs.append(pl.BlockSpec(memory_space=pl.ANY))
        args.append(buf)
        aliases = {4: 0}
    return _call(
        body, name="rs_chip_add",
        grid_spec=pltpu.PrefetchScalarGridSpec(
            num_scalar_prefetch=2, grid=(n_l, nt), in_specs=in_specs,
            out_specs=pl.BlockSpec((None, tr, c_w), lambda l, t, p_ref, c_ref: (first_layer + l, c_ref[0] * nt + t, 0))),
        out_shape=SDS((n_layers, 2 * h, c_w), F32), input_output_aliases=aliases,
        compiler_params=_params("parallel", "parallel"))(*args)


def _rs_swap(halves):
    n = len(halves)

    def body(*refs):
        outs, (send_sems, recv_sems) = refs[n:2 * n], refs[2 * n:]
        x, y, c = _mesh_pos()

        def copy(k, half):
            h = outs[k].shape[1] // 2
            mine = outs[k].at[:, pl.ds(c * h, h), :]
            return pltpu.make_async_remote_copy(
                src_ref=mine, dst_ref=mine if half == "mine" else outs[k].at[:, pl.ds((1 - c) * h, h), :],
                send_sem=send_sems.at[k], recv_sem=recv_sems.at[k], device_id=(x, y, 1 - c), device_id_type=MESH)

        for k in range(n):
            copy(k, "mine").start()
        for k in range(n):
            copy(k, "theirs").wait_send()
            copy(k, "theirs").wait_recv()

    return _call(body, name="rs_swap", in_specs=[HBM_SPEC] * n, out_specs=[HBM_SPEC] * n,
                 out_shape=[SDS(a.shape, F32) for a in halves], input_output_aliases={k: k for k in range(n)},
                 scratch_shapes=[pltpu.SemaphoreType.DMA((n,)), pltpu.SemaphoreType.DMA((n,))])(*halves)


def _pack(arrays, row_multiple):
    flat = jnp.concatenate([a.reshape(-1).astype(F32) for a in arrays])
    unit = row_multiple * LANES
    padded = -(-flat.shape[0] // unit) * unit
    return jnp.pad(flat, (0, padded - flat.shape[0])).reshape(padded // LANES, LANES)


def _unpack(packed, shapes):
    flat = packed.reshape(-1)
    out, pos = [], 0
    for s in shapes:
        size = 1
        for dim in s:
            size *= dim
        out.append(flat[pos:pos + size].reshape(s))
        pos += size
    return out


BIG_COL = ("sb_w_in", "cv_w_pw1", "ffn_w_up")
BIG_ROW = ("hyb_w_out", "cv_w_pw2", "ffn_w_down")
SMALL_SHARDED = ("cv_b_pw1", "cv_w_dw", "cv_b_dw", "cv_ln_g", "cv_ln_b", "cv_b_pw2", "ffn_w_dw")
SMALL_REPLICATED = ("mix_norm_g", "sb_q_norm_g", "sb_k_norm_g", "sg_z_norm_g", "sg_w_spatial", "sg_b_spatial",
                    "ffn_norm_g", "ffn_b_dw")
WEIGHTS = ("mix_norm_g", "sb_w_in", "sb_q_norm_g", "sb_k_norm_g", "sg_z_norm_g", "sg_w_spatial", "sg_b_spatial",
           "hyb_w_out", "cv_w_pw1", "cv_b_pw1", "cv_w_dw", "cv_b_dw", "cv_ln_g", "cv_ln_b", "cv_w_pw2", "cv_b_pw2",
           "ffn_norm_g", "ffn_w_up", "ffn_w_dw", "ffn_b_dw", "ffn_w_down")


def _pad_rows(a, rows):
    return jnp.pad(a, ((0, rows - a.shape[0]), (0, 0)))


def _step(x, tgt, w, m, v):
    n_layers = w["mix_norm_g"].shape[0]
    xi, yi, ci = _mesh_pos()
    chip = 2 * xi + yi

    assert n_layers == 4
    hosted_by = {("proj", 0): ["hyb_w_out"], ("prep", 0): [("ffn_w_up", 0)],
                 ("attn", 0): [("ffn_w_down", 0), "cv_w_pw1", "cv_w_pw2"],
                 ("up", 0): [("ffn_w_up", 1)], ("ffn_mid", 0): [("ffn_w_down", 1)],
                 ("conf_mid", 1): [("ffn_w_up", 2), ("ffn_w_down", 2)],
                 ("up", 1): [("ffn_w_up", 3)], ("ffn_mid", 1): [("ffn_w_down", 3)]}
    full = {}

    def shard_of(key):
        if isinstance(key, tuple):
            return w[key[0]][key[1]:key[1] + 1].astype(BF)
        return w[key].astype(BF)

    def keep(key, g4):
        if (key[0] if isinstance(key, tuple) else key) in BIG_ROW:
            g4 = g4.reshape(g4.shape[0], 1, g4.shape[1] * g4.shape[2], g4.shape[3])
        full[key] = g4

    def hosting(fn, point, *args, **kw):
        keys = hosted_by.get(point)
        if not keys:
            return fn(*args, **kw)
        out, gathered = fn(*args, gather=[shard_of(k) for k in keys], **kw)
        for key, g4 in zip(keys, gathered):
            keep(key, g4)
        return out

    keep("sb_w_in", _all_gather(shard_of("sb_w_in")))
    small_local = [w[name] for name in SMALL_SHARDED]
    gathered = _all_gather(_pack(small_local, 32)[None])[0]
    per_chip = [_unpack(gathered[p], [a.shape for a in small_local]) for p in range(N_CHIPS)]
    for k, name in enumerate(SMALL_SHARDED):
        full[name] = jnp.concatenate([per_chip[p][k] for p in range(N_CHIPS)], axis=-1)
    for name in SMALL_REPLICATED:
        full[name] = w[name]

    mean64, fold64 = _group_matrices()
    ffn_wdw = [_pad_rows(full["ffn_w_dw"][i], 8) for i in range(n_layers)]
    cv_wdw = [_pad_rows(full["cv_w_dw"][j], 32) for j in range(n_layers // 2)]
    row = lambda a: a.reshape(1, -1)

    saved = []
    cur = x
    h = _rms_fwd(cur, row(full["mix_norm_g"][0]))
    for i in range(n_layers):
        j = i // 2
        rec = {"x_in": cur, "h_mix": h}
        if i % 2 == 0:
            proj = hosting(_mm_nn, ("proj", i), h, full["sb_w_in"], j)
            qg = row(jnp.tile(full["sb_q_norm_g"][j], 512 // HEAD_DIM))
            kg = row(jnp.tile(full["sb_k_norm_g"][j], 512 // HEAD_DIM))
            zg = row(full["sg_z_norm_g"][j])
            bexp = jnp.repeat(full["sg_b_spatial"][j].T, HEAD_DIM, axis=1)
            qkv, gated = hosting(_mix_prep_fwd, ("prep", i), proj, qg, kg, zg, full["sg_w_spatial"], j, bexp, mean64)
            att_bf, att_32 = hosting(_attn_fwd, ("attn", i), qkv)
            mix = jnp.concatenate([att_bf, gated], axis=1)
            cur, h = _mm_nn(mix, full["hyb_w_out"], j, resid=cur, norm_g=row(full["ffn_norm_g"][i]))
            rec.update(proj=proj, qkv=qkv, att_32=att_32, mix=mix, qg=qg, kg=kg, zg=zg, bexp=bexp)
        else:
            p1 = _mm_nn(h, full["cv_w_pw1"], j, bias=row(full["cv_b_pw1"][j]), out_dtype=BF)
            ys, yc = hosting(_conf_mid_fwd, ("conf_mid", i), p1, cv_wdw[j], row(full["cv_b_dw"][j]),
                             row(full["cv_ln_g"][j]), row(full["cv_ln_b"][j]))
            cur, h = _mm_nn(ys, full["cv_w_pw2"], j, bias=row(full["cv_b_pw2"][j]), resid=cur,
                            norm_g=row(full["ffn_norm_g"][i]))
            rec.update(p1=p1, ys=ys, yc=yc)
        rec["x_mid"] = cur
        up = hosting(_mm_nn, ("up", i), h, full[("ffn_w_up", i)], 0, out_dtype=BF)
        act = hosting(_ffn_mid_fwd, ("ffn_mid", i), up, ffn_wdw[i], row(full["ffn_b_dw"][i]))
        rec.update(h_ffn=h, up=up, act=act)
        if i + 1 < n_layers:
            cur, h = _mm_nn(act, full[("ffn_w_down", i)], 0, resid=cur, norm_g=row(full["mix_norm_g"][i + 1]))
        else:
            cur = _mm_nn(act, full[("ffn_w_down", i)], 0, resid=cur)
        saved.append(rec)

    loss_vec, dy, dy_bf = _loss_grad(cur, tgt)
    loss = lax.psum(loss_vec[0, 0], ("x", "y", "c"))

    big_names = BIG_COL + BIG_ROW
    gbig = {}
    gsmall = {name: [None] * w[name].shape[0] for name in SMALL_SHARDED + SMALL_REPLICATED}

    def accumulate(name, layer, a, dy_, p_n):
        per_group = w[name].shape[0] // 2
        grp, slot = divmod(layer, per_group)
        gbig[(name, grp)] = _mm_tn(a, dy_, p_n, per_group, slot, gbig.get((name, grp)))

    def group_grads(grp):
        out = []
        for name in big_names:
            g4 = gbig[(name, grp)]
            if name in BIG_ROW:
                g4 = g4.reshape(g4.shape[0], N_CHIPS, g4.shape[2] // N_CHIPS, g4.shape[3])
            out.append(g4)
        return out

    def pair_sums_of(gs, mid_dtypes):
        return [_rs_pair_add(g, a, dt) for g, a, dt in zip(gs, _rs_pair(gs), mid_dtypes)]

    half_sums = {}
    late_pair_sums = None
    for i in reversed(range(n_layers)):
        j = i // 2
        rec = saved[i]
        dact = _mm_nt(dy_bf, full[("ffn_w_down", i)], 0, out_dtype=BF)
        accumulate("ffn_w_down", i, rec["act"], dy_bf, 1)
        dup, dwdw, dbdw = _ffn_mid_bwd(rec["up"], dact, ffn_wdw[i], row(full["ffn_b_dw"][i]))
        gsmall["ffn_w_dw"][i] = dwdw[:FFN_K]
        gsmall["ffn_b_dw"][i] = dbdw[0]
        accumulate("ffn_w_up", i, rec["h_ffn"], dup, N_CHIPS)
        dy, dy_bf, dg = _mm_nt_rms_bwd(dup, full[("ffn_w_up", i)], 0, rec["x_mid"], row(full["ffn_norm_g"][i]), dy)
        gsmall["ffn_norm_g"][i] = dg[0]
        if i % 2 == 0:
            dmix = _mm_nt(dy_bf, full["hyb_w_out"], j)
            accumulate("hyb_w_out", j, rec["mix"], dy_bf, 1)
            dq, dk, dv = _attn_bwd(rec["qkv"], rec["att_32"], dmix)
            dproj, dqg, dkg, dzg, dws, dbe = _mix_prep_bwd(
                rec["proj"], dq, dk, dv, dmix, rec["qg"], rec["kg"], rec["zg"], full["sg_w_spatial"], j, rec["bexp"],
                mean64, fold64)
            gsmall["sb_q_norm_g"][j] = dqg[0, :HEAD_DIM]
            gsmall["sb_k_norm_g"][j] = dkg[0, :HEAD_DIM]
            gsmall["sg_z_norm_g"][j] = dzg[0]
            gsmall["sg_w_spatial"][j] = dws
            gsmall["sg_b_spatial"][j] = dbe[:, ::HEAD_DIM].T
            dlast, w_first = dproj, full["sb_w_in"]
            accumulate("sb_w_in", j, rec["h_mix"], dproj, N_CHIPS)
        else:
            dys = _mm_nt(dy_bf, full["cv_w_pw2"], j, out_dtype=BF)
            accumulate("cv_w_pw2", j, rec["ys"], dy_bf, 1)
            carried = _chips_exchange(late_pair_sums) if late_pair_sums is not None else None
            res = _conf_mid_bwd(rec["p1"], rec["yc"], dys, dy, cv_wdw[j], row(full["cv_ln_g"][j]),
                                row(full["cv_ln_b"][j]), exchange=carried)
            if carried is not None:
                res, from_chips = res
                for name, ps, fc in zip(big_names, late_pair_sums, from_chips):
                    n_all = w[name].shape[0]
                    half_sums[name] = _rs_chip_add(ps, fc, n_all, n_all // 2)
                late_pair_sums = None
            dp1, dwdw, dbdw, dlg, dlb, db1, db2 = res
            gsmall["cv_w_dw"][j] = dwdw[:CONV_K]
            gsmall["cv_b_dw"][j] = dbdw[0]
            gsmall["cv_ln_g"][j] = dlg[0]
            gsmall["cv_ln_b"][j] = dlb[0]
            gsmall["cv_b_pw1"][j] = db1[0]
            gsmall["cv_b_pw2"][j] = db2[0]
            dlast, w_first = dp1, full["cv_w_pw1"]
            accumulate("cv_w_pw1", j, rec["h_mix"], dp1, N_CHIPS)
        dy, dy_bf, dg = _mm_nt_rms_bwd(dlast, w_first, j, rec["x_in"], row(full["mix_norm_g"][i]), dy)
        gsmall["mix_norm_g"][i] = dg[0]
        if i == n_layers // 2:
            late_pair_sums = pair_sums_of(group_grads(1), [BF] * len(big_names))

    small_names = SMALL_REPLICATED + SMALL_SHARDED
    small_full = [jnp.stack(gsmall[name]) for name in small_names]
    packed = _pack(small_full, 32 * N_CHIPS)
    rows_q = packed.shape[0] // N_CHIPS
    early = pair_sums_of(group_grads(0) + [packed.reshape(1, N_CHIPS, rows_q, LANES)], [BF] * len(big_names) + [F32])
    from_chips = _rs_chips(early)
    halves = [_rs_chip_add(ps, fc, w[name].shape[0], 0, half_sums[name])
              for name, ps, fc in zip(big_names, early, from_chips)]
    halves.append(_rs_chip_add(early[-1], from_chips[-1], 1, 0))
    swapped = _rs_swap(halves)
    grads = dict(zip(big_names, swapped))
    summed = _all_gather(swapped[-1]).reshape(-1, LANES)
    for name, gsum in zip(small_names, _unpack(summed, [a.shape for a in small_full])):
        if name in SMALL_SHARDED:
            n_loc = w[name].shape[-1]
            split = gsum.reshape(gsum.shape[:-1] + (N_CHIPS, n_loc))
            gsum = lax.dynamic_index_in_dim(split, chip, axis=split.ndim - 2, keepdims=False)
        grads[name] = gsum

    delta, new_m, new_v = {}, {}, {}
    for name in BIG_COL + BIG_ROW:
        shp = w[name].shape
        two_d = lambda a: a.reshape(shp[0] * shp[1], shp[2])
        d, nm, nv = _adamw(two_d(w[name]), two_d(grads[name]), two_d(m[name]), two_d(v[name]))
        delta[name], new_m[name], new_v[name] = d.reshape(shp), nm.reshape(shp), nv.reshape(shp)
    shapes = [w[name].shape for name in small_names]
    d, nm, nv = _adamw(*(_pack([src[name] for name in small_names], 256) for src in (w, grads, m, v)))
    for name, a, b_, c_ in zip(small_names, _unpack(d, shapes), _unpack(nm, shapes), _unpack(nv, shapes)):
        delta[name], new_m[name], new_v[name] = a, b_, c_

    return (loss, dy, *[grads[n] for n in WEIGHTS], *[delta[n] for n in WEIGHTS],
            *[new_m[n] for n in WEIGHTS], *[new_v[n] for n in WEIGHTS])


def kernel(x, mix_norm_g, sb_w_in, sb_q_norm_g, sb_k_norm_g, sg_z_norm_g, sg_w_spatial, sg_b_spatial, hyb_w_out, cv_w_pw1, cv_b_pw1, cv_w_dw, cv_b_dw, cv_ln_g, cv_ln_b, cv_w_pw2, cv_b_pw2, ffn_norm_g, ffn_w_up, ffn_w_dw, ffn_b_dw, ffn_w_down, loss_target, m_mix_norm_g, m_sb_w_in, m_sb_q_norm_g, m_sb_k_norm_g, m_sg_z_norm_g, m_sg_w_spatial, m_sg_b_spatial, m_hyb_w_out, m_cv_w_pw1, m_cv_b_pw1, m_cv_w_dw, m_cv_b_dw, m_cv_ln_g, m_cv_ln_b, m_cv_w_pw2, m_cv_b_pw2, m_ffn_norm_g, m_ffn_w_up, m_ffn_w_dw, m_ffn_b_dw, m_ffn_w_down, v_mix_norm_g, v_sb_w_in, v_sb_q_norm_g, v_sb_k_norm_g, v_sg_z_norm_g, v_sg_w_spatial, v_sg_b_spatial, v_hyb_w_out, v_cv_w_pw1, v_cv_b_pw1, v_cv_w_dw, v_cv_b_dw, v_cv_ln_g, v_cv_ln_b, v_cv_w_pw2, v_cv_b_pw2, v_ffn_norm_g, v_ffn_w_up, v_ffn_w_dw, v_ffn_b_dw, v_ffn_w_down):
    given = dict(locals())
    w = {n: given[n] for n in WEIGHTS}
    m = {n: given["m_" + n] for n in WEIGHTS}
    v = {n: given["v_" + n] for n in WEIGHTS}
    out = _step(x[0], loss_target[0], w, m, v)
    return (out[0], out[1][None], *out[2:])
```

```python
import functools
from typing import Callable, NamedTuple

import jax
import jax.numpy as jnp
from jax import lax
from jax.experimental import pallas as pl
from jax.experimental.pallas import tpu as pltpu

F32 = jnp.float32
BF = jnp.bfloat16
SDS = jax.ShapeDtypeStruct
HI = lax.Precision.HIGHEST
MESH = pl.DeviceIdType.MESH

NORM_EPS = 1e-6
HEAD_DIM = 64
ATT_BLOCK = 128
CHUNK = 128
PREP_CHUNKS = 2
CONV_K = 31
CONV_HALO = 32
FFN_K = 3
FFN_HALO = 16
LANES = 128
N_CHIPS = 4
VMEM_LIMIT_BYTES = 56 * 2**20

ADAM_LR = 0.001
ADAM_B1 = 0.9
ADAM_B2 = 0.999
ADAM_EPS = 1e-08
ADAM_WD = 0.01
ADAM_STEP = 10

NT_DIMS = (((1,), (1,)), ((), ()))
TN_DIMS = (((0,), (0,)), ((), ()))


def _call(body, **kw):
    return pl.pallas_call(body, **kw)


def _params(*sem):
    return pltpu.CompilerParams(dimension_semantics=sem, vmem_limit_bytes=VMEM_LIMIT_BYTES)


def _gelu(x):
    return 0.5 * x * (1.0 + lax.erf(x * 0.7071067811865476))


def _rms(x, g):
    y = x * lax.rsqrt(jnp.mean(x * x, axis=-1, keepdims=True) + NORM_EPS)
    return y * g


def _rms_fwd(x, g):
    t, d = x.shape
    tm = min(512, t)

    def body(x_ref, g_ref, o_ref):
        o_ref[...] = _rms(x_ref[...], g_ref[...]).astype(o_ref.dtype)

    return _call(
        body, name="rms_fwd", grid=(t // tm,),
        in_specs=[pl.BlockSpec((tm, d), lambda i: (i, 0)), pl.BlockSpec((1, d), lambda i: (0, 0))],
        out_specs=pl.BlockSpec((tm, d), lambda i: (i, 0)),
        out_shape=SDS((t, d), BF), compiler_params=_params("parallel"))(x, g)


def _mm_nn(a, w, l, bias=None, resid=None, out_dtype=F32, gather=None, norm_g=None):
    m, k = a.shape
    _, p_n, kw, n = w.shape
    assert k == kw
    normed = norm_g is not None
    assert not normed or (p_n == 1 and not gather)
    tm = min(512 if normed else 1024, m)
    tn = n if (normed or k * n * 2 <= 4 * 2**20) else n // 2
    nj = n // tn
    in_specs = [pl.BlockSpec((tm, k), lambda i, p, j: (i, 0)),
                pl.BlockSpec((None, None, k, tn), lambda i, p, j: (l, p, 0, j))]
    args = [a, w]
    if bias is not None:
        in_specs.append(pl.BlockSpec((1, tn), lambda i, p, j: (0, p * nj + j)))
        args.append(bias)
    if resid is not None:
        in_specs.append(pl.BlockSpec((tm, tn), lambda i, p, j: (i, p * nj + j)))
        args.append(resid)
    if normed:
        in_specs.append(pl.BlockSpec((1, n), lambda i, p, j: (0, 0)))
        args.append(norm_g)
    n_in = len(args)

    def body(*refs):
        acc = jnp.dot(refs[0][...], refs[1][...], preferred_element_type=F32)
        nxt = 2
        if bias is not None:
            acc = acc + refs[nxt][...]
            nxt += 1
        if resid is not None:
            acc = refs[nxt][...] + acc
        refs[n_in][...] = acc.astype(refs[n_in].dtype)
        if normed:
            refs[n_in + 1][...] = _rms(acc, refs[n_in - 1][...]).astype(BF)

    out_spec = pl.BlockSpec((tm, tn), lambda i, p, j: (i, p * nj + j))
    kw = dict(name="mm_nn", grid=(m // tm, p_n, nj), in_specs=in_specs,
              out_specs=[out_spec, out_spec] if normed else out_spec,
              out_shape=[SDS((m, n), out_dtype), SDS((m, n), BF)] if normed else SDS((m, p_n * n), out_dtype))
    if gather:
        (out,), gathered = _call_gathering(body, gather, args, **kw)
        return out, gathered
    return _call(body, compiler_params=_params("parallel", "parallel", "parallel"), **kw)(*args)


def _mm_nt(dy, w, l, out_dtype=F32):
    m, n_all = dy.shape
    _, p_n, r, n = w.shape
    assert n_all == p_n * n
    tm = min(512, m)

    def body(dy_ref, w_ref, o_ref):
        acc = lax.dot_general(dy_ref[:, 0:n], w_ref[0], NT_DIMS, preferred_element_type=F32)
        for p in range(1, p_n):
            acc = acc + lax.dot_general(dy_ref[:, p * n:(p + 1) * n], w_ref[p], NT_DIMS, preferred_element_type=F32)
        o_ref[...] = acc.astype(o_ref.dtype)

    return _call(
        body, name="mm_nt", grid=(m // tm,),
        in_specs=[pl.BlockSpec((tm, n_all), lambda i: (i, 0)),
                  pl.BlockSpec((None, p_n, r, n), lambda i: (l, 0, 0, 0))],
        out_specs=pl.BlockSpec((tm, r), lambda i: (i, 0)),
        out_shape=SDS((m, r), out_dtype),
        compiler_params=_params("parallel"))(dy, w)


def _mm_nt_rms_bwd(dy, w, l, x, g, dres):
    m, n_all = dy.shape
    _, p_n, r, n = w.shape
    assert n_all == p_n * n and x.shape == (m, r)
    tm = min(256, m)

    def body(dy_ref, w_ref, x_ref, g_ref, r_ref, dx_ref, dxb_ref, dg_ref):
        dh = lax.dot_general(dy_ref[:, 0:n], w_ref[0], NT_DIMS, preferred_element_type=F32)
        for p in range(1, p_n):
            dh = dh + lax.dot_general(dy_ref[:, p * n:(p + 1) * n], w_ref[p], NT_DIMS, preferred_element_type=F32)
        _, vjp = jax.vjp(_rms, x_ref[...], g_ref[...])
        dx, dg = vjp(dh)
        dx = dx + r_ref[...]
        dx_ref[...] = dx
        dxb_ref[...] = dx.astype(BF)

        @pl.when(pl.program_id(0) == 0)
        def _():
            dg_ref[...] = jnp.zeros_like(dg_ref)

        dg_ref[...] += dg

    row = pl.BlockSpec((tm, r), lambda i: (i, 0))
    vec = pl.BlockSpec((1, r), lambda i: (0, 0))
    return _call(
        body, name="mm_nt_rms_bwd", grid=(m // tm,),
        in_specs=[pl.BlockSpec((tm, n_all), lambda i: (i, 0)), pl.BlockSpec((None, p_n, r, n), lambda i: (l, 0, 0, 0)),
                  row, vec, row],
        out_specs=[row, row, vec], out_shape=[SDS((m, r), F32), SDS((m, r), BF), SDS((1, r), F32)],
        compiler_params=_params("arbitrary"))(dy, w, x, g, dres)


def _mm_tn(a, dy, p_n, n_layers, l, buf=None):
    m, k = a.shape
    n = dy.shape[1] // p_n
    tm = min(2048, m)
    tk = k if k <= 1024 else k // 2
    nm = m // tm

    def body(a_ref, dy_ref, *rest):
        o_ref, acc_ref = rest[-2], rest[-1]
        mi = pl.program_id(2)
        part = lax.dot_general(a_ref[...], dy_ref[...], TN_DIMS, preferred_element_type=F32)

        @pl.when(mi == 0)
        def _():
            acc_ref[...] = part

        @pl.when(mi > 0)
        def _():
            acc_ref[...] += part

        @pl.when(mi == nm - 1)
        def _():
            o_ref[...] = acc_ref[...].astype(o_ref.dtype)

    in_specs = [pl.BlockSpec((tm, tk), lambda p, kk, mi: (mi, kk)),
                pl.BlockSpec((tm, n), lambda p, kk, mi: (mi, p))]
    args = [a, dy]
    aliases = {}
    if buf is not None:
        in_specs.append(pl.BlockSpec(memory_space=pl.ANY))
        args.append(buf)
        aliases = {2: 0}
    return _call(
        body, name="mm_tn", grid=(p_n, k // tk, nm), in_specs=in_specs,
        out_specs=pl.BlockSpec((None, None, tk, n), lambda p, kk, mi: (l, p, kk, 0)),
        out_shape=SDS((n_layers, p_n, k, n), BF), scratch_shapes=[pltpu.VMEM((tk, n), F32)],
        input_output_aliases=aliases,
        compiler_params=_params("parallel", "parallel", "arbitrary"))(*args)


def _loss_grad(y, tgt):
    t, d = y.shape
    tm = min(512, t)

    def body(y_ref, t_ref, l_ref, d_ref, db_ref):
        err = y_ref[...] - t_ref[...]
        dy = err * (1.0 / d)
        d_ref[...] = dy
        db_ref[...] = dy.astype(BF)
        part = 0.5 * jnp.sum(jnp.sum(err * err, axis=1, keepdims=True) * (1.0 / d), axis=0, keepdims=True)

        @pl.when(pl.program_id(0) == 0)
        def _():
            l_ref[...] = jnp.zeros_like(l_ref)

        l_ref[...] += jnp.broadcast_to(part, l_ref.shape)

    row = pl.BlockSpec((tm, d), lambda i: (i, 0))
    return _call(
        body, name="loss_grad", grid=(t // tm,), in_specs=[row, row],
        out_specs=[pl.BlockSpec((1, LANES), lambda i: (0, 0)), row, row],
        out_shape=[SDS((1, LANES), F32), SDS((t, d), F32), SDS((t, d), BF)],
        compiler_params=_params("arbitrary"))(y, tgt)


def _prev_halo(tr, halo, col):
    return lambda i: (jnp.maximum(i * (tr // halo) - 1, 0), col)


def _next_halo(tr, halo, n_rows, col):
    return lambda i: (jnp.minimum((i + 1) * (tr // halo), n_rows // halo - 1), col)


def _shifted_back(x):
    return pltpu.roll(x, 1, 0), pltpu.roll(x, 2, 0)


def _conv3(x, w_ref, b_ref, col):
    x1, x2 = _shifted_back(x)
    return b_ref[:, col] + w_ref[pl.ds(0, 1), col] * x2 + w_ref[pl.ds(1, 1), col] * x1 + w_ref[pl.ds(2, 1), col] * x


def _ffn_mid_fwd(up, w_dw, b_dw, gather=None):
    t, f2 = up.shape
    f = f2 // 2
    tr = min(256, t)
    h = FFN_HALO

    def body(g_ref, gp_ref, v_ref, w_ref, b_ref, o_ref):
        first_tile = pl.program_id(0) == 0

        def strip(c, carry):
            col = pl.ds(pl.multiple_of(c * LANES, LANES), LANES)
            x = jnp.concatenate([jnp.where(first_tile, 0.0, gp_ref[:, col].astype(F32)), g_ref[:, col].astype(F32)], axis=0)
            gc = _conv3(x, w_ref, b_ref, col)[h:]
            o_ref[:, col] = (gc * jax.nn.sigmoid(gc) * v_ref[:, col].astype(F32)).astype(o_ref.dtype)
            return carry

        lax.fori_loop(0, f // LANES, strip, 0)

    kw = dict(name="ffn_mid_fwd", grid=(t // tr,),
              in_specs=[pl.BlockSpec((tr, f), lambda i: (i, 0)), pl.BlockSpec((h, f), _prev_halo(tr, h, 0)),
                        pl.BlockSpec((tr, f), lambda i: (i, 1)),
                        pl.BlockSpec((8, f), lambda i: (0, 0)), pl.BlockSpec((1, f), lambda i: (0, 0))],
              out_specs=pl.BlockSpec((tr, f), lambda i: (i, 0)), out_shape=SDS((t, f), BF))
    args = (up, up, up, w_dw, b_dw)
    if gather:
        (out,), gathered = _call_gathering(body, gather, args, **kw)
        return out, gathered
    return _call(body, compiler_params=_params("parallel"), **kw)(*args)


def _ffn_mid_bwd(up, da, w_dw, b_dw):
    t, f2 = up.shape
    f = f2 // 2
    tr = min(256, t)
    h = FFN_HALO
    n_tiles = t // tr

    def body(g_ref, gp_ref, gn_ref, v_ref, vn_ref, da_ref, dan_ref, w_ref, b_ref, dup_ref, dw_ref, db_ref):
        i = pl.program_id(0)
        last = i == n_tiles - 1
        n = tr + h

        @pl.when(i == 0)
        def _():
            dw_ref[...] = jnp.zeros_like(dw_ref)
            db_ref[...] = jnp.zeros_like(db_ref)

        def rows(tile_ref, next_ref, col):
            return jnp.concatenate([tile_ref[:, col].astype(F32), next_ref[:, col].astype(F32)], axis=0)

        def strip(c, carry):
            col = pl.ds(pl.multiple_of(c * LANES, LANES), LANES)
            x = jnp.concatenate([jnp.where(i == 0, 0.0, gp_ref[:, col].astype(F32)), rows(g_ref, gn_ref, col)], axis=0)
            x1, x2 = _shifted_back(x)
            w0, w1, w2 = (w_ref[pl.ds(k, 1), col] for k in range(FFN_K))
            gc = (b_ref[:, col] + w0 * x2 + w1 * x1 + w2 * x)[h:]
            dav = rows(da_ref, dan_ref, col)
            sg = jax.nn.sigmoid(gc)
            dup_ref[:, pl.ds(pl.multiple_of(f + c * LANES, LANES), LANES)] = (dav * gc * sg)[:tr].astype(dup_ref.dtype)
            dgc = dav * rows(v_ref, vn_ref, col) * (sg * (1.0 + gc * (1.0 - sg)))
            dgc = jnp.concatenate([dgc[:tr], jnp.where(last, 0.0, dgc[tr:])], axis=0)
            d1, d2 = pltpu.roll(dgc, n - 1, 0), pltpu.roll(dgc, n - 2, 0)
            dup_ref[:, col] = (w2 * dgc + w1 * d1 + w0 * d2)[:tr].astype(dup_ref.dtype)
            dgt = dgc[:tr]
            for k, past in enumerate((x2, x1, x)):
                dw_ref[pl.ds(k, 1), col] += jnp.sum(past[h:h + tr] * dgt, axis=0, keepdims=True)
            db_ref[:, col] += jnp.sum(dgt, axis=0, keepdims=True)
            return carry

        lax.fori_loop(0, f // LANES, strip, 0)

    tile = lambda col: pl.BlockSpec((tr, f), lambda i: (i, col))
    nxt = lambda col: pl.BlockSpec((h, f), _next_halo(tr, h, t, col))
    return _call(
        body, name="ffn_mid_bwd", grid=(n_tiles,),
        in_specs=[tile(0), pl.BlockSpec((h, f), _prev_halo(tr, h, 0)), nxt(0), tile(1), nxt(1), tile(0), nxt(0),
                  pl.BlockSpec((8, f), lambda i: (0, 0)), pl.BlockSpec((1, f), lambda i: (0, 0))],
        out_specs=[pl.BlockSpec((tr, f2), lambda i: (i, 0)), pl.BlockSpec((8, f), lambda i: (0, 0)),
                   pl.BlockSpec((1, f), lambda i: (0, 0))],
        out_shape=[SDS((t, f2), BF), SDS((8, f), F32), SDS((1, f), F32)],
        compiler_params=_params("arbitrary"))(up, up, up, up, up, da, da, w_dw, b_dw)


def _ln_silu(yc, g, b):
    mu = jnp.mean(yc, axis=-1, keepdims=True)
    xc = yc - mu
    y = xc * lax.rsqrt(jnp.mean(xc * xc, axis=-1, keepdims=True) + NORM_EPS)
    return jax.nn.silu(y * g + b)


SUBLANES = 8
CONV_PAD = 24
SHIFT_CHUNK = 40
TAP_ROWS = 64


def _glu(a, g):
    return a.astype(F32) * jax.nn.sigmoid(g.astype(F32))


def _glu_strip(ygs_ref, first_tile, a_ref, ap_ref, g_ref, gp_ref, col, h, tr):
    ygs_ref[pl.ds(0, h), :] = jnp.where(first_tile, 0.0, _glu(ap_ref[:, col], gp_ref[:, col]))
    ygs_ref[pl.ds(h, tr), :] = _glu(a_ref[:, col], g_ref[:, col])


def _shift_past(sh_ref, ygs_ref, h, n):
    first = h - CONV_PAD - SUBLANES
    for u0 in range(0, n + CONV_PAD, SHIFT_CHUNK):
        x = ygs_ref[pl.ds(first + u0, SHIFT_CHUNK + SUBLANES), :]
        for r in range(1, SUBLANES):
            sh_ref[r, pl.ds(u0, SHIFT_CHUNK), :] = pltpu.roll(x, r, 0)[SUBLANES:]


def _past_rows(sh_ref, ygs_ref, h, n, s, row0=0):
    a, r = divmod(s, SUBLANES)
    if r == 0:
        return ygs_ref[pl.ds(row0 + h - SUBLANES * a, n), :]
    return sh_ref[r, pl.ds(row0 + CONV_PAD - SUBLANES * a, n), :]


def _conf_mid_fwd(p1, w_dw, b_dw, ln_g, ln_b, gather=None):
    t, w2 = p1.shape
    w = w2 // 2
    tr = min(256, t)
    h = CONV_HALO
    rc = 32

    def body(a_ref, ap_ref, g_ref, gp_ref, w_ref, b_ref, lg_ref, lb_ref, o_ref, yc_ref, ygs_ref, sh_ref):
        first_tile = pl.program_id(0) == 0

        def strip(c, carry):
            col = pl.ds(pl.multiple_of(c * LANES, LANES), LANES)
            _glu_strip(ygs_ref, first_tile, a_ref, ap_ref, g_ref, gp_ref, col, h, tr)
            _shift_past(sh_ref, ygs_ref, h, tr)
            acc = jnp.broadcast_to(b_ref[:, col], (tr, LANES))
            for k in range(CONV_K):
                acc = acc + w_ref[pl.ds(k, 1), col] * _past_rows(sh_ref, ygs_ref, h, tr, CONV_K - 1 - k)
            yc_ref[:, col] = acc
            return carry

        lax.fori_loop(0, w // LANES, strip, 0)

        def rows(r, carry):
            rs = pl.ds(pl.multiple_of(r * rc, rc), rc)
            o_ref[rs, :] = _ln_silu(yc_ref[rs, :], lg_ref[...], lb_ref[...]).astype(o_ref.dtype)
            return carry

        lax.fori_loop(0, tr // rc, rows, 0)

    vec = pl.BlockSpec((1, w), lambda i: (0, 0))
    tile = pl.BlockSpec((tr, w), lambda i: (i, 0))
    kw = dict(name="conf_mid_fwd", grid=(t // tr,),
              in_specs=[tile, pl.BlockSpec((h, w), _prev_halo(tr, h, 0)),
                        pl.BlockSpec((tr, w), lambda i: (i, 1)), pl.BlockSpec((h, w), _prev_halo(tr, h, 1)),
                        pl.BlockSpec((32, w), lambda i: (0, 0)), vec, vec, vec],
              out_specs=[tile, tile], out_shape=[SDS((t, w), BF), SDS((t, w), F32)],
              scratch_shapes=[pltpu.VMEM((h + tr, LANES), F32), pltpu.VMEM((SUBLANES, tr + CONV_PAD, LANES), F32)])
    args = (p1, p1, p1, p1, w_dw, b_dw, ln_g, ln_b)
    if gather:
        return _call_gathering(body, gather, args, **kw)
    return _call(body, compiler_params=_params("parallel"), **kw)(*args)


def _conf_mid_bwd(p1, yc, dys, dy, w_dw, ln_g, ln_b, exchange=None):
    t, w2 = p1.shape
    w = w2 // 2
    tr = min(256, t)
    h = CONV_HALO
    rc = 32
    n_tiles = t // tr

    def body(a_ref, ap_ref, g_ref, gp_ref, yc_ref, ycn_ref, dys_ref, dysn_ref, dy_ref, w_ref, lg_ref, lb_ref,
             dp_ref, dw_ref, db_ref, dlg_ref, dlb_ref, db1_ref, db2_ref, dyc_ref, ygs_ref, sh_ref, shf_ref, dwacc_ref):
        i = pl.program_id(0)
        last = i == n_tiles - 1

        @pl.when(i == 0)
        def _():
            for ref in (dw_ref, db_ref, dlg_ref, dlb_ref, db1_ref, db2_ref):
                ref[...] = jnp.zeros_like(ref)

        def ln_rows(r, carry):
            rs = pl.ds(pl.multiple_of(r * rc, rc), rc)
            _, vjp = jax.vjp(_ln_silu, yc_ref[rs, :], lg_ref[...], lb_ref[...])
            dyc, dlg, dlb = vjp(dys_ref[rs, :].astype(F32))
            dyc_ref[rs, :] = dyc
            dlg_ref[...] += dlg
            dlb_ref[...] += dlb
            return carry

        lax.fori_loop(0, tr // rc, ln_rows, 0)
        _, vjp = jax.vjp(_ln_silu, ycn_ref[...], lg_ref[...], lb_ref[...])
        dyc_ref[pl.ds(tr, h), :] = jnp.where(last, 0.0, vjp(dysn_ref[...].astype(F32))[0])
        db2_ref[...] += jnp.sum(dy_ref[...], axis=0, keepdims=True)

        def back(c, carry):
            col = pl.ds(pl.multiple_of(c * LANES, LANES), LANES)
            gcol = pl.ds(pl.multiple_of(w + c * LANES, LANES), LANES)
            _glu_strip(ygs_ref, i == 0, a_ref, ap_ref, g_ref, gp_ref, col, h, tr)
            _shift_past(sh_ref, ygs_ref, h, tr)
            for u0 in range(0, tr + CONV_PAD, SHIFT_CHUNK):
                part = dyc_ref[pl.ds(u0, SHIFT_CHUNK + SUBLANES), col]
                for r in range(1, SUBLANES):
                    shf_ref[r, pl.ds(u0, SHIFT_CHUNK), :] = pltpu.roll(part, SHIFT_CHUNK + SUBLANES - r, 0)[:SHIFT_CHUNK]
            for r0 in range(0, tr, TAP_ROWS):
                rows = pl.ds(r0, TAP_ROWS)
                dyc = dyc_ref[rows, col]
                dyg = jnp.zeros((TAP_ROWS, LANES), F32)
                for k in range(CONV_K):
                    s = CONV_K - 1 - k
                    a, r = divmod(s, SUBLANES)
                    if r == 0:
                        future = dyc_ref[pl.ds(r0 + SUBLANES * a, TAP_ROWS), col]
                    else:
                        future = shf_ref[r, pl.ds(r0 + SUBLANES * a, TAP_ROWS), :]
                    dyg = dyg + w_ref[pl.ds(k, 1), col] * future
                    prod = _past_rows(sh_ref, ygs_ref, h, TAP_ROWS, s, r0) * dyc
                    part = prod[0:SUBLANES]
                    for q in range(1, TAP_ROWS // SUBLANES):
                        part = part + prod[q * SUBLANES:(q + 1) * SUBLANES]
                    if r0 == 0:
                        dwacc_ref[k] = part
                    else:
                        dwacc_ref[k] += part
                sg = jax.nn.sigmoid(g_ref[rows, col].astype(F32))
                da = dyg * sg
                dg = dyg * a_ref[rows, col].astype(F32) * sg * (1.0 - sg)
                dp_ref[rows, col] = da.astype(dp_ref.dtype)
                dp_ref[rows, gcol] = dg.astype(dp_ref.dtype)
                db_ref[:, col] += jnp.sum(dyc, axis=0, keepdims=True)
                db1_ref[:, col] += jnp.sum(da, axis=0, keepdims=True)
                db1_ref[:, gcol] += jnp.sum(dg, axis=0, keepdims=True)
            for k in range(CONV_K):
                dw_ref[pl.ds(k, 1), col] += jnp.sum(dwacc_ref[k], axis=0, keepdims=True)
            return carry

        lax.fori_loop(0, w // LANES, back, 0)

    tile = lambda col: pl.BlockSpec((tr, w), lambda i: (i, col))
    prv = lambda col: pl.BlockSpec((h, w), _prev_halo(tr, h, col))
    nxt = pl.BlockSpec((h, w), _next_halo(tr, h, t, 0))
    vec = pl.BlockSpec((1, w), lambda i: (0, 0))
    kw = dict(
        name="conf_mid_bwd", grid=(n_tiles,),
        in_specs=[tile(0), prv(0), tile(1), prv(1), tile(0), nxt, tile(0), nxt, tile(0),
                  pl.BlockSpec((32, w), lambda i: (0, 0)), vec, vec],
        out_specs=[pl.BlockSpec((tr, w2), lambda i: (i, 0)), pl.BlockSpec((32, w), lambda i: (0, 0)), vec, vec, vec,
                   pl.BlockSpec((1, w2), lambda i: (0, 0)), vec],
        out_shape=[SDS((t, w2), BF), SDS((32, w), F32), SDS((1, w), F32), SDS((1, w), F32), SDS((1, w), F32),
                   SDS((1, w2), F32), SDS((1, w), F32)],
        scratch_shapes=[pltpu.VMEM((tr + h, w), F32), pltpu.VMEM((h + tr, LANES), F32),
                        pltpu.VMEM((SUBLANES, tr + CONV_PAD, LANES), F32), pltpu.VMEM((SUBLANES, tr + CONV_PAD, LANES), F32),
                        pltpu.VMEM((32, SUBLANES, LANES), F32)])
    args = (p1, p1, p1, p1, yc, yc, dys, dys, dy, w_dw, ln_g, ln_b)
    if exchange is not None:
        return _call_hosting(body, exchange, args, **kw)
    return _call(body, compiler_params=_params("arbitrary"), **kw)(*args)


def _group_matrices():
    i = lax.broadcasted_iota(jnp.int32, (512, 512), 0)
    j = lax.broadcasted_iota(jnp.int32, (512, 512), 1)
    mean64 = jnp.where(i // HEAD_DIM == j // HEAD_DIM, 1.0 / HEAD_DIM, 0.0).astype(F32)
    fold64 = jnp.where(i % HEAD_DIM == j % HEAD_DIM, 1.0, 0.0).astype(F32)
    return mean64, fold64


def _split_dot(x, mat):
    hi = x.astype(BF)
    lo = (x - hi.astype(F32)).astype(BF)
    mb = mat.astype(BF)
    return jnp.dot(hi, mb, preferred_element_type=F32) + jnp.dot(lo, mb, preferred_element_type=F32)


@jax.custom_vjp
def _group_sum(x, mat):
    return _split_dot(x, mat)


_group_sum.defvjp(lambda x, mat: (_split_dot(x, mat), mat), lambda mat, ct: (_split_dot(ct, mat), jnp.zeros_like(mat)))


def _bf_dot_plain(a, b):
    return jnp.dot(a.astype(BF), b.astype(BF), preferred_element_type=F32)


@jax.custom_vjp
def _bf_dot(a, b):
    return _bf_dot_plain(a, b)


def _bf_dot_bwd(res, ct):
    a, b = res
    cb = ct.astype(BF)
    return (lax.dot_general(cb, b.astype(BF), NT_DIMS, preferred_element_type=F32),
            lax.dot_general(a.astype(BF), cb, TN_DIMS, preferred_element_type=F32))


_bf_dot.defvjp(lambda a, b: (_bf_dot_plain(a, b), (a, b)), _bf_dot_bwd)


def _prep_tile(proj, qg, kg, zg, ws, bexp, mean64, differentiated=False):
    sw = 512
    q, k, v, u, z = (proj[:, n * sw:(n + 1) * sw] for n in range(5))
    group_sum, dot = (_group_sum, _bf_dot) if differentiated else (_split_dot, _bf_dot_plain)

    def group_norm(x):
        return x * lax.rsqrt(group_sum(x * x, mean64) + NORM_EPS)

    qn = group_norm(q) * qg
    kn = group_norm(k) * kg
    zn = group_norm(_gelu(z)) * zg
    row = lax.broadcasted_iota(jnp.int32, (CHUNK, CHUNK), 0)
    col = lax.broadcasted_iota(jnp.int32, (CHUNK, CHUNK), 1)
    first = lax.broadcasted_iota(jnp.int32, (1, LANES), 1) < HEAD_DIM
    wm = [jnp.where(col <= row, ws[g], 0.0) for g in range(2 * (sw // LANES))]
    chunks = []
    for ci in range(proj.shape[0] // CHUNK):
        parts = []
        for pr in range(sw // LANES):
            zp = zn[ci * CHUNK:(ci + 1) * CHUNK, pr * LANES:(pr + 1) * LANES]
            parts.append(jnp.where(first, dot(wm[2 * pr], zp), dot(wm[2 * pr + 1], zp)))
        chunks.append(jnp.concatenate(parts, axis=1) + bexp)
    s = chunks[0] if len(chunks) == 1 else jnp.concatenate(chunks, axis=0)
    return qn, kn, v, _gelu(u) * s


def _mix_prep_fwd(proj, qg, kg, zg, w_s, l, bexp, mean64, gather=None):
    t = proj.shape[0]
    tr = PREP_CHUNKS * CHUNK

    def body(p_ref, qg_ref, kg_ref, zg_ref, ws_ref, be_ref, m_ref, qkv_ref, go_ref):
        qn, kn, v, go = _prep_tile(p_ref[...], qg_ref[...], kg_ref[...], zg_ref[...], ws_ref[...], be_ref[...], m_ref[...])
        qkv_ref[:, 0:512] = qn.astype(BF)
        qkv_ref[:, 512:1024] = kn.astype(BF)
        qkv_ref[:, 1024:1536] = v.astype(BF)
        go_ref[...] = go.astype(BF)

    vec = pl.BlockSpec((1, 512), lambda i: (0, 0))
    kw = dict(name="mix_prep_fwd", grid=(t // tr,),
              in_specs=[pl.BlockSpec((tr, 2560), lambda i: (i, 0)), vec, vec, vec,
                        pl.BlockSpec((None, 8, CHUNK, CHUNK), lambda i: (l, 0, 0, 0)),
                        pl.BlockSpec((CHUNK, 512), lambda i: (0, 0)), pl.BlockSpec((512, 512), lambda i: (0, 0))],
              out_specs=[pl.BlockSpec((tr, 1536), lambda i: (i, 0)), pl.BlockSpec((tr, 512), lambda i: (i, 0))],
              out_shape=[SDS((t, 1536), BF), SDS((t, 512), BF)])
    args = (proj, qg, kg, zg, w_s, bexp, mean64)
    if gather:
        return _call_gathering(body, gather, args, **kw)
    return _call(body, compiler_params=_params("parallel"), **kw)(*args)


def _mix_prep_bwd(proj, dq, dk, dv, dmix, qg, kg, zg, w_s, l, bexp, mean64, fold64):
    t = proj.shape[0]
    tr = PREP_CHUNKS * CHUNK
    n_tiles = t // tr

    def body(p_ref, dq_ref, dk_ref, dv_ref, dgo_ref, qg_ref, kg_ref, zg_ref, ws_ref, be_ref, m_ref, f_ref,
             dp_ref, dqg_ref, dkg_ref, dzg_ref, dws_ref, dbe_ref):
        i = pl.program_id(0)

        @pl.when(i == 0)
        def _():
            for ref in (dqg_ref, dkg_ref, dzg_ref, dws_ref, dbe_ref):
                ref[...] = jnp.zeros_like(ref)

        fn = functools.partial(_prep_tile, mean64=m_ref[...], differentiated=True)
        _, vjp = jax.vjp(fn, p_ref[...], qg_ref[...], kg_ref[...], zg_ref[...], ws_ref[...], be_ref[...])
        dp, dqg, dkg, dzg, dws, dbe = vjp((dq_ref[...], dk_ref[...], dv_ref[...], dgo_ref[...]))
        dp_ref[...] = dp.astype(BF)
        dqg_ref[pl.ds(0, 1), :] += dqg
        dkg_ref[pl.ds(0, 1), :] += dkg
        dzg_ref[pl.ds(0, 1), :] += dzg
        dws_ref[...] += dws
        dbe_ref[...] += dbe

        @pl.when(i == n_tiles - 1)
        def _():
            dqg_ref[...] = jnp.dot(dqg_ref[...], f_ref[...], precision=HI, preferred_element_type=F32)
            dkg_ref[...] = jnp.dot(dkg_ref[...], f_ref[...], precision=HI, preferred_element_type=F32)
            dbe_ref[...] = jnp.dot(dbe_ref[...], m_ref[...] * float(HEAD_DIM), precision=HI, preferred_element_type=F32)

    vec = pl.BlockSpec((1, 512), lambda i: (0, 0))
    acc = pl.BlockSpec((8, 512), lambda i: (0, 0))
    sq = pl.BlockSpec((512, 512), lambda i: (0, 0))
    row = pl.BlockSpec((tr, 512), lambda i: (i, 0))
    return _call(
        body, name="mix_prep_bwd", grid=(n_tiles,),
        in_specs=[pl.BlockSpec((tr, 2560), lambda i: (i, 0)), row, row, row, pl.BlockSpec((tr, 512), lambda i: (i, 1)),
                  vec, vec, vec, pl.BlockSpec((None, 8, CHUNK, CHUNK), lambda i: (l, 0, 0, 0)),
                  pl.BlockSpec((CHUNK, 512), lambda i: (0, 0)), sq, sq],
        out_specs=[pl.BlockSpec((tr, 2560), lambda i: (i, 0)), acc, acc, acc,
                   pl.BlockSpec((8, CHUNK, CHUNK), lambda i: (0, 0, 0)), pl.BlockSpec((CHUNK, 512), lambda i: (0, 0))],
        out_shape=[SDS((t, 2560), BF), SDS((8, 512), F32), SDS((8, 512), F32), SDS((8, 512), F32),
                   SDS((8, CHUNK, CHUNK), F32), SDS((CHUNK, 512), F32)],
        compiler_params=_params("arbitrary"))(proj, dq, dk, dv, dmix, qg, kg, zg, w_s, bexp, mean64, fold64)


def _sb_logs(qh, kb, valid):
    z = lax.dot_general(qh, kb, NT_DIMS, preferred_element_type=F32) * (HEAD_DIM ** -0.5)
    soft = jnp.log1p(jnp.exp(-jnp.abs(z)))
    lk_raw = -(jnp.maximum(z, 0.0) + soft)
    ls = -(jnp.maximum(-z, 0.0) + soft)
    return lk_raw, ls, jnp.where(valid, lk_raw, 0.0)


def _sb_weights(ls, run, tail, valid):
    return jnp.where(valid, jnp.exp(ls + run + tail), 0.0)


def _att_masks(b):
    row = lax.broadcasted_iota(jnp.int32, (b, b), 0)
    col = lax.broadcasted_iota(jnp.int32, (b, b), 1)
    first = lax.broadcasted_iota(jnp.int32, (1, LANES), 1) < HEAD_DIM
    return row, col, first


N_PAIRS = 4


def _load_kv(qkv_hbm, k_scr, v_scr, sems, group, width):
    ck = pltpu.make_async_copy(qkv_hbm.at[:, pl.ds(pl.multiple_of(512 + group * width, LANES), width)], k_scr, sems.at[0])
    cv = pltpu.make_async_copy(qkv_hbm.at[:, pl.ds(pl.multiple_of(1024 + group * width, LANES), width)], v_scr, sems.at[1])
    ck.start()
    cv.start()
    ck.wait()
    cv.wait()


def _split_heads(ref, pair, first):
    x = ref[:, pair * LANES:(pair + 1) * LANES]
    zero = jnp.zeros_like(x)
    return jnp.where(first, x, zero), jnp.where(first, zero, x)


def _any_weight_left(run_ref, n_heads):
    top = run_ref[0]
    for hh in range(1, n_heads):
        top = jnp.maximum(top, run_ref[hh])
    return jnp.max(jnp.exp(top)) > 0.0


def _attn_fwd(qkv, pairs_per_step=4, gather=None):
    t = qkv.shape[0]
    b = ATT_BLOCK
    nq = t // b
    width = pairs_per_step * LANES
    n_heads = 2 * pairs_per_step

    def body(q_ref, qkv_hbm, ob_ref, o32_ref, k_scr, v_scr, acc_ref, run_ref, sems):
        group, qi = pl.program_id(0), pl.program_id(1)

        @pl.when(qi == 0)
        def _():
            _load_kv(qkv_hbm, k_scr, v_scr, sems, group, width)

        row, col, first = _att_masks(b)
        qh = [x for pr in range(pairs_per_step) for x in _split_heads(q_ref, pr, first)]
        upper = jnp.where(row > col, 1.0, 0.0).astype(BF)
        acc_ref[...] = jnp.zeros_like(acc_ref)
        run_ref[...] = jnp.zeros_like(run_ref)
        heads = range(n_heads)

        def step(carry):
            j, _ = carry
            rows = pl.ds(pl.multiple_of(j * b, b), b)
            valid = jnp.logical_or(j != qi, col < row)
            lanes = [pl.ds((hh // 2) * LANES, LANES) for hh in heads]
            logs = [_sb_logs(qh[hh], k_scr[rows, lanes[hh]], valid) for hh in heads]
            tails = [_split_dot(logs[hh][2], upper) for hh in heads]
            for hh in heads:
                wgt = _sb_weights(logs[hh][1], run_ref[hh], tails[hh], valid)
                acc_ref[hh] += jnp.dot(wgt.astype(BF), v_scr[rows, lanes[hh]], preferred_element_type=F32)
            for hh in heads:
                run_ref[hh] += jnp.sum(logs[hh][2], axis=1, keepdims=True)
            return j - 1, _any_weight_left(run_ref, n_heads)

        lax.while_loop(lambda c: jnp.logical_and(c[0] >= 0, c[1]), step, (qi, jnp.bool_(True)))
        for pr in range(pairs_per_step):
            out = jnp.where(first, acc_ref[2 * pr], acc_ref[2 * pr + 1])
            ob_ref[:, pr * LANES:(pr + 1) * LANES] = out.astype(BF)
            o32_ref[:, pr * LANES:(pr + 1) * LANES] = out

    blk = pl.BlockSpec((b, width), lambda g, qi: (qi, g))
    kw = dict(name="attn_fwd", grid=(N_PAIRS // pairs_per_step, nq),
              in_specs=[blk, pl.BlockSpec(memory_space=pl.ANY)], out_specs=[blk, blk],
              out_shape=[SDS((t, 512), BF), SDS((t, 512), F32)],
              scratch_shapes=[pltpu.VMEM((t, width), BF), pltpu.VMEM((t, width), BF),
                              pltpu.VMEM((n_heads, b, LANES), F32), pltpu.VMEM((n_heads, b, 1), F32),
                              pltpu.SemaphoreType.DMA((2,))])
    if gather:
        return _call_gathering(body, gather, (qkv, qkv), **kw)
    return _call(body, compiler_params=_params("arbitrary", "arbitrary"), **kw)(qkv, qkv)


def _attn_bwd(qkv, a32, dmix, pairs_per_step=2):
    t = qkv.shape[0]
    b = ATT_BLOCK
    nq = t // b
    width = pairs_per_step * LANES
    n_heads = 2 * pairs_per_step

    def body(q_ref, a_ref, da_ref, qkv_hbm, dq_ref, dk_hbm, dv_hbm,
             k_scr, v_scr, dk_scr, dv_scr, dqa_ref, run_ref, rung_ref, sems):
        group, qi = pl.program_id(0), pl.program_id(1)

        @pl.when(qi == 0)
        def _():
            _load_kv(qkv_hbm, k_scr, v_scr, sems, group, width)
            dk_scr[...] = jnp.zeros_like(dk_scr)
            dv_scr[...] = jnp.zeros_like(dv_scr)

        row, col, first = _att_masks(b)
        qh, dah, dtot = [], [], []
        for pr in range(pairs_per_step):
            qh += _split_heads(q_ref, pr, first)
            da = da_ref[:, pr * LANES:(pr + 1) * LANES]
            prod = da * a_ref[:, pr * LANES:(pr + 1) * LANES]
            dtot += [jnp.sum(jnp.where(first, prod, 0.0), axis=1, keepdims=True),
                     jnp.sum(jnp.where(first, 0.0, prod), axis=1, keepdims=True)]
            dah += [jnp.where(first, da, 0.0).astype(BF), jnp.where(first, 0.0, da).astype(BF)]
        upper = jnp.where(row > col, 1.0, 0.0).astype(BF)
        lower_incl = jnp.where(row >= col, 1.0, 0.0).astype(BF)
        dqa_ref[...] = jnp.zeros_like(dqa_ref)
        run_ref[...] = jnp.zeros_like(run_ref)
        rung_ref[...] = jnp.zeros_like(rung_ref)
        heads = range(n_heads)

        def step(carry):
            j, _ = carry
            rows = pl.ds(pl.multiple_of(j * b, b), b)
            valid = jnp.logical_or(j != qi, col < row)
            lanes = [pl.ds((hh // 2) * LANES, LANES) for hh in heads]
            logs = [_sb_logs(qh[hh], k_scr[rows, lanes[hh]], valid) for hh in heads]
            dps = [lax.dot_general(dah[hh], v_scr[rows, lanes[hh]], NT_DIMS, preferred_element_type=F32) for hh in heads]
            tails = [_split_dot(logs[hh][2], upper) for hh in heads]
            wgts = [_sb_weights(logs[hh][1], run_ref[hh], tails[hh], valid) for hh in heads]
            gs = [wgts[hh] * dps[hh] for hh in heads]
            g_froms = [_split_dot(gs[hh], lower_incl) for hh in heads]
            for hh in heads:
                lk_raw, ls, _ = logs[hh]
                dlk = jnp.where(valid, dtot[hh] - rung_ref[hh] - g_froms[hh], 0.0)
                dz = ((gs[hh] * jnp.exp(lk_raw) - dlk * jnp.exp(ls)) * (HEAD_DIM ** -0.5)).astype(BF)
                dqa_ref[hh] += jnp.dot(dz, k_scr[rows, lanes[hh]], preferred_element_type=F32)
                dk_scr[rows, lanes[hh]] += lax.dot_general(dz, qh[hh], TN_DIMS, preferred_element_type=F32)
                dv_scr[rows, lanes[hh]] += lax.dot_general(wgts[hh].astype(BF), dah[hh], TN_DIMS, preferred_element_type=F32)
            for hh in heads:
                rung_ref[hh] += jnp.sum(gs[hh], axis=1, keepdims=True)
                run_ref[hh] += jnp.sum(logs[hh][2], axis=1, keepdims=True)
            return j - 1, _any_weight_left(run_ref, n_heads)

        lax.while_loop(lambda c: jnp.logical_and(c[0] >= 0, c[1]), step, (qi, jnp.bool_(True)))
        for pr in range(pairs_per_step):
            dq_ref[:, pr * LANES:(pr + 1) * LANES] = jnp.where(first, dqa_ref[2 * pr], dqa_ref[2 * pr + 1])

        @pl.when(qi == nq - 1)
        def _():
            cols = pl.ds(pl.multiple_of(group * width, LANES), width)
            ck = pltpu.make_async_copy(dk_scr, dk_hbm.at[:, cols], sems.at[0])
            cv = pltpu.make_async_copy(dv_scr, dv_hbm.at[:, cols], sems.at[1])
            ck.start()
            cv.start()
            ck.wait()
            cv.wait()

    blk = pl.BlockSpec((b, width), lambda g, qi: (qi, g))
    anywhere = pl.BlockSpec(memory_space=pl.ANY)
    return _call(
        body, name="attn_bwd", grid=(N_PAIRS // pairs_per_step, nq),
        in_specs=[blk, blk, blk, anywhere], out_specs=[blk, anywhere, anywhere],
        out_shape=[SDS((t, 512), F32), SDS((t, 512), F32), SDS((t, 512), F32)],
        scratch_shapes=[pltpu.VMEM((t, width), BF), pltpu.VMEM((t, width), BF),
                        pltpu.VMEM((t, width), F32), pltpu.VMEM((t, width), F32),
                        pltpu.VMEM((n_heads, b, LANES), F32), pltpu.VMEM((n_heads, b, 1), F32),
                        pltpu.VMEM((n_heads, b, 1), F32), pltpu.SemaphoreType.DMA((2,))],
        compiler_params=_params("arbitrary", "arbitrary"))(qkv, a32, dmix, qkv)


def _adamw(w, g, m, v):
    n, c = w.shape
    tr = min(256, n)
    assert n % tr == 0

    def body(w_ref, g_ref, m_ref, v_ref, d_ref, nm_ref, nv_ref):
        g = g_ref[...]
        m = ADAM_B1 * m_ref[...] + (1.0 - ADAM_B1) * g
        v = ADAM_B2 * v_ref[...] + (1.0 - ADAM_B2) * jnp.square(g)
        m_hat = m / (1.0 - ADAM_B1 ** ADAM_STEP)
        v_hat = v / (1.0 - ADAM_B2 ** ADAM_STEP)
        d_ref[...] = -ADAM_LR * (m_hat / (jnp.sqrt(v_hat) + ADAM_EPS) + ADAM_WD * w_ref[...])
        nm_ref[...] = m
        nv_ref[...] = v

    blk = pl.BlockSpec((tr, c), lambda i: (i, 0))
    return _call(
        body, name="adamw", grid=(n // tr,), in_specs=[blk] * 4, out_specs=[blk] * 3,
        out_shape=[SDS((n, c), F32)] * 3, compiler_params=_params("parallel"))(w, g, m, v)


def _mesh_pos():
    return lax.axis_index("x"), lax.axis_index("y"), lax.axis_index("c")


def _other_chips(x, y):
    return [(1 - x, y), (x, 1 - y), (1 - x, 1 - y)]


HBM_SPEC = pl.BlockSpec(memory_space=pltpu.HBM)


GATHER_COPIES = 6


def _gather_steps(s_ref, o_ref, send_sems, recv_sems, local_sems, slot):
    h = s_ref.shape[1] // 2
    x, y, c = _mesh_pos()
    sibling = (x, y, 1 - c)
    chips = _other_chips(x, y)
    base = GATHER_COPIES * slot

    def half(px, py, hc):
        return o_ref.at[:, 2 * px + py, pl.ds(hc * h, h), :]

    def copy(k, dst, to, src=None):
        return pltpu.make_async_remote_copy(
            src_ref=dst if src is None else src, dst_ref=dst, send_sem=send_sems.at[base + k],
            recv_sem=recv_sems.at[base + k], device_id=to, device_id_type=MESH)

    mine = pltpu.make_async_copy(s_ref, o_ref.at[:, 2 * x + y], local_sems.at[slot])
    first = [copy(j, half(x, y, c), (*chip, c), src=s_ref.at[:, pl.ds(c * h, h), :]) for j, chip in enumerate(chips)]
    passed = [copy(3 + j, half(*chip, c), sibling) for j, chip in enumerate(chips)]

    def start():
        mine.start()
        for cp in first:
            cp.start()

    def finish():
        for j, chip in enumerate(chips):
            copy(j, half(*chip, c), (x, y, c)).wait_recv()
            passed[j].start()
        for j, chip in enumerate(chips):
            copy(3 + j, half(*chip, 1 - c), (x, y, c)).wait_recv()
        for cp in first + passed:
            cp.wait_send()
        mine.wait()

    return start, finish


def _gather_scratch(n):
    return [pltpu.SemaphoreType.DMA((GATHER_COPIES * n,)), pltpu.SemaphoreType.DMA((GATHER_COPIES * n,)),
            pltpu.SemaphoreType.DMA((n,))]


def _gathered_shape(shard):
    n_l, r, c_w = shard.shape
    return SDS((n_l, N_CHIPS, r, c_w), shard.dtype)


def _all_gather(shard):
    def body(s_ref, o_ref, send_sems, recv_sems, local_sems):
        start, finish = _gather_steps(s_ref, o_ref, send_sems, recv_sems, local_sems, 0)
        start()
        finish()

    return _call(body, name="all_gather", in_specs=[HBM_SPEC], out_specs=HBM_SPEC, out_shape=_gathered_shape(shard),
                 scratch_shapes=_gather_scratch(1))(shard)


class _Exchange(NamedTuple):
    tag: str
    inputs: list
    out_shapes: list
    scratch: list
    make_steps: Callable


def _gather_exchange(shards):
    n = len(shards)

    def make_steps(s_refs, o_refs, sems):
        steps = [_gather_steps(s_refs[k], o_refs[k], *sems, k) for k in range(n)]
        return (lambda: [start() for start, _ in steps]), (lambda: [finish() for _, finish in steps])

    return _Exchange("gathering", list(shards), [_gathered_shape(s) for s in shards], _gather_scratch(n), make_steps)


def _call_hosting(body, exchange, args, *, name, grid, in_specs, out_specs, out_shape, scratch_shapes=()):
    out_specs = list(out_specs) if isinstance(out_specs, (list, tuple)) else [out_specs]
    out_shape = list(out_shape) if isinstance(out_shape, (list, tuple)) else [out_shape]
    n_in, n_out, n_scr = len(in_specs), len(out_specs), len(scratch_shapes)
    n_xi, n_xo, n_sem = len(exchange.inputs), len(exchange.out_shapes), len(exchange.scratch)

    def hosting_body(*refs):
        ins, x_ins = refs[:n_in], refs[n_in:n_in + n_xi]
        outs = refs[n_in + n_xi:n_in + n_xi + n_out]
        x_outs = refs[n_in + n_xi + n_out:n_in + n_xi + n_out + n_xo]
        scratch = refs[n_in + n_xi + n_out + n_xo:n_in + n_xi + n_out + n_xo + n_scr]
        start, finish = exchange.make_steps(x_ins, x_outs, refs[len(refs) - n_sem:])
        is_first = functools.reduce(jnp.logical_and, [pl.program_id(a) == 0 for a in range(len(grid))])
        is_last = functools.reduce(jnp.logical_and, [pl.program_id(a) == grid[a] - 1 for a in range(len(grid))])

        @pl.when(is_first)
        def _():
            start()

        body(*ins, *outs, *scratch)

        @pl.when(is_last)
        def _():
            finish()

    res = _call(
        hosting_body, name=name + "_" + exchange.tag, grid=grid, in_specs=list(in_specs) + [HBM_SPEC] * n_xi,
        out_specs=out_specs + [HBM_SPEC] * n_xo, out_shape=out_shape + list(exchange.out_shapes),
        scratch_shapes=list(scratch_shapes) + list(exchange.scratch),
        compiler_params=_params(*(["arbitrary"] * len(grid))))(*args, *exchange.inputs)
    return res[:n_out], res[n_out:]


def _call_gathering(body, shards, args, **kw):
    return _call_hosting(body, _gather_exchange(shards), args, **kw)


def _row_tile(h):
    assert h <= 512
    return h


def _rs_pair(gs):
    n = len(gs)

    def body(*refs):
        g_refs, a_refs, (send_sems, recv_sems) = refs[:n], refs[n:2 * n], refs[2 * n:]
        x, y, c = _mesh_pos()
        cps = []
        for k in range(n):
            h = g_refs[k].shape[2] // 2
            cps.append(pltpu.make_async_remote_copy(
                src_ref=g_refs[k].at[:, :, pl.ds((1 - c) * h, h), :], dst_ref=a_refs[k], send_sem=send_sems.at[k],
                recv_sem=recv_sems.at[k], device_id=(x, y, 1 - c), device_id_type=MESH))
        for cp in cps:
            cp.start()
        for cp in cps:
            cp.wait()

    out_shape = [SDS((g.shape[0], g.shape[1], g.shape[2] // 2, g.shape[3]), g.dtype) for g in gs]
    return _call(body, name="rs_pair", in_specs=[HBM_SPEC] * n, out_specs=[HBM_SPEC] * n, out_shape=out_shape,
                 scratch_shapes=[pltpu.SemaphoreType.DMA((n,)), pltpu.SemaphoreType.DMA((n,))])(*gs)


def _rs_pair_add(g, from_sibling, mid_dtype):
    n_l, n_p, r, c_w = g.shape
    h = r // 2
    tr = _row_tile(h)
    nt = h // tr
    c_arr = jnp.reshape(lax.axis_index("c"), (1,)).astype(jnp.int32)

    def body(c_ref, g_ref, a_ref, o_ref):
        o_ref[...] = (g_ref[...].astype(F32) + a_ref[...].astype(F32)).astype(o_ref.dtype)

    blk = (None, None, tr, c_w)
    return _call(
        body, name="rs_pair_add",
        grid_spec=pltpu.PrefetchScalarGridSpec(
            num_scalar_prefetch=1, grid=(n_l, n_p, nt),
            in_specs=[pl.BlockSpec(blk, lambda l, p, t, c_ref: (l, p, c_ref[0] * nt + t, 0)),
                      pl.BlockSpec(blk, lambda l, p, t, c_ref: (l, p, t, 0))],
            out_specs=pl.BlockSpec(blk, lambda l, p, t, c_ref: (l, p, t, 0))),
        out_shape=SDS((n_l, n_p, h, c_w), mid_dtype),
        compiler_params=_params("parallel", "parallel", "parallel"))(c_arr, g, from_sibling)


def _chips_exchange(pair_sums):
    n = len(pair_sums)

    def make_steps(s_refs, b_refs, sems):
        send_sems, recv_sems = sems
        x, y, c = _mesh_pos()
        cps = [pltpu.make_async_remote_copy(
            src_ref=s_refs[k].at[:, 2 * chip[0] + chip[1]], dst_ref=b_refs[k].at[j], send_sem=send_sems.at[3 * k + j],
            recv_sem=recv_sems.at[3 * k + j], device_id=(*chip, c), device_id_type=MESH)
            for k in range(n) for j, chip in enumerate(_other_chips(x, y))]
        return (lambda: [cp.start() for cp in cps]), (lambda: [cp.wait() for cp in cps])

    out_shapes = [SDS((3, s.shape[0], s.shape[2], s.shape[3]), s.dtype) for s in pair_sums]
    sems = [pltpu.SemaphoreType.DMA((3 * n,)), pltpu.SemaphoreType.DMA((3 * n,))]
    return _Exchange("scattering", list(pair_sums), out_shapes, sems, make_steps)


def _rs_chips(pair_sums):
    ex = _chips_exchange(pair_sums)
    n = len(pair_sums)

    def body(*refs):
        start, finish = ex.make_steps(refs[:n], refs[n:2 * n], refs[2 * n:])
        start()
        finish()

    return _call(body, name="rs_chips", in_specs=[HBM_SPEC] * n, out_specs=[HBM_SPEC] * n, out_shape=ex.out_shapes,
                 scratch_shapes=ex.scratch)(*pair_sums)


def _rs_chip_add(pair_sum, from_chips, n_layers, first_layer, buf=None):
    n_l, _, h, c_w = pair_sum.shape
    tr = _row_tile(h)
    nt = h // tr
    p_arr = jnp.reshape(2 * lax.axis_index("x") + lax.axis_index("y"), (1,)).astype(jnp.int32)
    c_arr = jnp.reshape(lax.axis_index("c"), (1,)).astype(jnp.int32)

    def body(p_ref, c_ref, s_ref, b_ref, *rest):
        acc = s_ref[...].astype(F32)
        for j in range(3):
            acc = acc + b_ref[j].astype(F32)
        rest[-1][...] = acc

    in_specs = [pl.BlockSpec((None, None, tr, c_w), lambda l, t, p_ref, c_ref: (l, p_ref[0], t, 0)),
                pl.BlockSpec((3, None, tr, c_w), lambda l, t, p_ref, c_ref: (0, l, t, 0))]
    args = [p_arr, c_arr, pair_sum, from_chips]
    aliases = {}
    if buf is not None:
        in_specs.append(pl.BlockSpec(memory_space=pl.ANY))
        args.append(buf)
        aliases = {4: 0}
    return _call(
        body, name="rs_chip_add",
        grid_spec=pltpu.PrefetchScalarGridSpec(
            num_scalar_prefetch=2, grid=(n_l, nt), in_specs=in_specs,
            out_specs=pl.BlockSpec((None, tr, c_w), lambda l, t, p_ref, c_ref: (first_layer + l, c_ref[0] * nt + t, 0))),
        out_shape=SDS((n_layers, 2 * h, c_w), F32), input_output_aliases=aliases,
        compiler_params=_params("parallel", "parallel"))(*args)


def _rs_swap(halves):
    n = len(halves)

    def body(*refs):
        outs, (send_sems, recv_sems) = refs[n:2 * n], refs[2 * n:]
        x, y, c = _mesh_pos()

        def copy(k, half):
            h = outs[k].shape[1] // 2
            mine = outs[k].at[:, pl.ds(c * h, h), :]
            return pltpu.make_async_remote_copy(
                src_ref=mine, dst_ref=mine if half == "mine" else outs[k].at[:, pl.ds((1 - c) * h, h), :],
                send_sem=send_sems.at[k], recv_sem=recv_sems.at[k], device_id=(x, y, 1 - c), device_id_type=MESH)

        for k in range(n):
            copy(k, "mine").start()
        for k in range(n):
            copy(k, "theirs").wait_send()
            copy(k, "theirs").wait_recv()

    return _call(body, name="rs_swap", in_specs=[HBM_SPEC] * n, out_specs=[HBM_SPEC] * n,
                 out_shape=[SDS(a.shape, F32) for a in halves], input_output_aliases={k: k for k in range(n)},
                 scratch_shapes=[pltpu.SemaphoreType.DMA((n,)), pltpu.SemaphoreType.DMA((n,))])(*halves)


def _pack(arrays, row_multiple):
    flat = jnp.concatenate([a.reshape(-1).astype(F32) for a in arrays])
    unit = row_multiple * LANES
    padded = -(-flat.shape[0] // unit) * unit
    return jnp.pad(flat, (0, padded - flat.shape[0])).reshape(padded // LANES, LANES)


def _unpack(packed, shapes):
    flat = packed.reshape(-1)
    out, pos = [], 0
    for s in shapes:
        size = 1
        for dim in s:
            size *= dim
        out.append(flat[pos:pos + size].reshape(s))
        pos += size
    return out


BIG_COL = ("sb_w_in", "cv_w_pw1", "ffn_w_up")
BIG_ROW = ("hyb_w_out", "cv_w_pw2", "ffn_w_down")
SMALL_SHARDED = ("cv_b_pw1", "cv_w_dw", "cv_b_dw", "cv_ln_g", "cv_ln_b", "cv_b_pw2", "ffn_w_dw")
SMALL_REPLICATED = ("mix_norm_g", "sb_q_norm_g", "sb_k_norm_g", "sg_z_norm_g", "sg_w_spatial", "sg_b_spatial",
                    "ffn_norm_g", "ffn_b_dw")
WEIGHTS = ("mix_norm_g", "sb_w_in", "sb_q_norm_g", "sb_k_norm_g", "sg_z_norm_g", "sg_w_spatial", "sg_b_spatial",
           "hyb_w_out", "cv_w_pw1", "cv_b_pw1", "cv_w_dw", "cv_b_dw", "cv_ln_g", "cv_ln_b", "cv_w_pw2", "cv_b_pw2",
           "ffn_norm_g", "ffn_w_up", "ffn_w_dw", "ffn_b_dw", "ffn_w_down")


def _pad_rows(a, rows):
    return jnp.pad(a, ((0, rows - a.shape[0]), (0, 0)))


def _step(x, tgt, w, m, v):
    n_layers = w["mix_norm_g"].shape[0]
    xi, yi, ci = _mesh_pos()
    chip = 2 * xi + yi

    assert n_layers == 4
    hosted_by = {("proj", 0): ["hyb_w_out"], ("prep", 0): [("ffn_w_up", 0)],
                 ("attn", 0): [("ffn_w_down", 0), "cv_w_pw1", "cv_w_pw2"],
                 ("up", 0): [("ffn_w_up", 1)], ("ffn_mid", 0): [("ffn_w_down", 1)],
                 ("conf_mid", 1): [("ffn_w_up", 2), ("ffn_w_down", 2)],
                 ("up", 1): [("ffn_w_up", 3)], ("ffn_mid", 1): [("ffn_w_down", 3), ("sb_w_in", 1)]}
    full = {}

    def shard_of(key):
        if isinstance(key, tuple):
            return w[key[0]][key[1]:key[1] + 1].astype(BF)
        return w[key].astype(BF)

    def keep(key, g4):
        if (key[0] if isinstance(key, tuple) else key) in BIG_ROW:
            g4 = g4.reshape(g4.shape[0], 1, g4.shape[1] * g4.shape[2], g4.shape[3])
        full[key] = g4

    def hosting(fn, point, *args, **kw):
        keys = hosted_by.get(point)
        if not keys:
            return fn(*args, **kw)
        out, gathered = fn(*args, gather=[shard_of(k) for k in keys], **kw)
        for key, g4 in zip(keys, gathered):
            keep(key, g4)
        return out

    keep(("sb_w_in", 0), _all_gather(shard_of(("sb_w_in", 0))))
    small_local = [w[name] for name in SMALL_SHARDED]
    gathered = _all_gather(_pack(small_local, 32)[None])[0]
    per_chip = [_unpack(gathered[p], [a.shape for a in small_local]) for p in range(N_CHIPS)]
    for k, name in enumerate(SMALL_SHARDED):
        full[name] = jnp.concatenate([per_chip[p][k] for p in range(N_CHIPS)], axis=-1)
    for name in SMALL_REPLICATED:
        full[name] = w[name]

    mean64, fold64 = _group_matrices()
    ffn_wdw = [_pad_rows(full["ffn_w_dw"][i], 8) for i in range(n_layers)]
    cv_wdw = [_pad_rows(full["cv_w_dw"][j], 32) for j in range(n_layers // 2)]
    row = lambda a: a.reshape(1, -1)

    saved = []
    cur = x
    h = _rms_fwd(cur, row(full["mix_norm_g"][0]))
    for i in range(n_layers):
        j = i // 2
        rec = {"x_in": cur, "h_mix": h}
        if i % 2 == 0:
            proj = hosting(_mm_nn, ("proj", i), h, full[("sb_w_in", j)], 0)
            qg = row(jnp.tile(full["sb_q_norm_g"][j], 512 // HEAD_DIM))
            kg = row(jnp.tile(full["sb_k_norm_g"][j], 512 // HEAD_DIM))
            zg = row(full["sg_z_norm_g"][j])
            bexp = jnp.repeat(full["sg_b_spatial"][j].T, HEAD_DIM, axis=1)
            qkv, gated = hosting(_mix_prep_fwd, ("prep", i), proj, qg, kg, zg, full["sg_w_spatial"], j, bexp, mean64)
            att_bf, att_32 = hosting(_attn_fwd, ("attn", i), qkv)
            mix = jnp.concatenate([att_bf, gated], axis=1)
            cur, h = _mm_nn(mix, full["hyb_w_out"], j, resid=cur, norm_g=row(full["ffn_norm_g"][i]))
            rec.update(proj=proj, qkv=qkv, att_32=att_32, mix=mix, qg=qg, kg=kg, zg=zg, bexp=bexp)
        else:
            p1 = _mm_nn(h, full["cv_w_pw1"], j, bias=row(full["cv_b_pw1"][j]), out_dtype=BF)
            ys, yc = hosting(_conf_mid_fwd, ("conf_mid", i), p1, cv_wdw[j], row(full["cv_b_dw"][j]),
                             row(full["cv_ln_g"][j]), row(full["cv_ln_b"][j]))
            cur, h = _mm_nn(ys, full["cv_w_pw2"], j, bias=row(full["cv_b_pw2"][j]), resid=cur,
                            norm_g=row(full["ffn_norm_g"][i]))
            rec.update(p1=p1, ys=ys, yc=yc)
        rec["x_mid"] = cur
        up = hosting(_mm_nn, ("up", i), h, full[("ffn_w_up", i)], 0, out_dtype=BF)
        act = hosting(_ffn_mid_fwd, ("ffn_mid", i), up, ffn_wdw[i], row(full["ffn_b_dw"][i]))
        rec.update(h_ffn=h, up=up, act=act)
        if i + 1 < n_layers:
            cur, h = _mm_nn(act, full[("ffn_w_down", i)], 0, resid=cur, norm_g=row(full["mix_norm_g"][i + 1]))
        else:
            cur = _mm_nn(act, full[("ffn_w_down", i)], 0, resid=cur)
        saved.append(rec)

    loss_vec, dy, dy_bf = _loss_grad(cur, tgt)
    loss = lax.psum(loss_vec[0, 0], ("x", "y", "c"))

    big_names = BIG_COL + BIG_ROW
    gbig = {}
    gsmall = {name: [None] * w[name].shape[0] for name in SMALL_SHARDED + SMALL_REPLICATED}

    def accumulate(name, layer, a, dy_, p_n):
        per_group = w[name].shape[0] // 2
        grp, slot = divmod(layer, per_group)
        gbig[(name, grp)] = _mm_tn(a, dy_, p_n, per_group, slot, gbig.get((name, grp)))

    def group_grads(grp):
        out = []
        for name in big_names:
            g4 = gbig[(name, grp)]
            if name in BIG_ROW:
                g4 = g4.reshape(g4.shape[0], N_CHIPS, g4.shape[2] // N_CHIPS, g4.shape[3])
            out.append(g4)
        return out

    def pair_sums_of(gs, mid_dtypes):
        return [_rs_pair_add(g, a, dt) for g, a, dt in zip(gs, _rs_pair(gs), mid_dtypes)]

    half_sums = {}
    late_pair_sums = None
    for i in reversed(range(n_layers)):
        j = i // 2
        rec = saved[i]
        dact = _mm_nt(dy_bf, full[("ffn_w_down", i)], 0, out_dtype=BF)
        accumulate("ffn_w_down", i, rec["act"], dy_bf, 1)
        dup, dwdw, dbdw = _ffn_mid_bwd(rec["up"], dact, ffn_wdw[i], row(full["ffn_b_dw"][i]))
        gsmall["ffn_w_dw"][i] = dwdw[:FFN_K]
        gsmall["ffn_b_dw"][i] = dbdw[0]
        accumulate("ffn_w_up", i, rec["h_ffn"], dup, N_CHIPS)
        dy, dy_bf, dg = _mm_nt_rms_bwd(dup, full[("ffn_w_up", i)], 0, rec["x_mid"], row(full["ffn_norm_g"][i]), dy)
        gsmall["ffn_norm_g"][i] = dg[0]
        if i % 2 == 0:
            dmix = _mm_nt(dy_bf, full["hyb_w_out"], j)
            accumulate("hyb_w_out", j, rec["mix"], dy_bf, 1)
            dq, dk, dv = _attn_bwd(rec["qkv"], rec["att_32"], dmix)
            dproj, dqg, dkg, dzg, dws, dbe = _mix_prep_bwd(
                rec["proj"], dq, dk, dv, dmix, rec["qg"], rec["kg"], rec["zg"], full["sg_w_spatial"], j, rec["bexp"],
                mean64, fold64)
            gsmall["sb_q_norm_g"][j] = dqg[0, :HEAD_DIM]
            gsmall["sb_k_norm_g"][j] = dkg[0, :HEAD_DIM]
            gsmall["sg_z_norm_g"][j] = dzg[0]
            gsmall["sg_w_spatial"][j] = dws
            gsmall["sg_b_spatial"][j] = dbe[:, ::HEAD_DIM].T
            dlast, w_first, l_first = dproj, full[("sb_w_in", j)], 0
            accumulate("sb_w_in", j, rec["h_mix"], dproj, N_CHIPS)
        else:
            dys = _mm_nt(dy_bf, full["cv_w_pw2"], j, out_dtype=BF)
            accumulate("cv_w_pw2", j, rec["ys"], dy_bf, 1)
            carried = _chips_exchange(late_pair_sums) if late_pair_sums is not None else None
            res = _conf_mid_bwd(rec["p1"], rec["yc"], dys, dy, cv_wdw[j], row(full["cv_ln_g"][j]),
                                row(full["cv_ln_b"][j]), exchange=carried)
            if carried is not None:
                res, from_chips = res
                for name, ps, fc in zip(big_names, late_pair_sums, from_chips):
                    n_all = w[name].shape[0]
                    half_sums[name] = _rs_chip_add(ps, fc, n_all, n_all // 2)
                late_pair_sums = None
            dp1, dwdw, dbdw, dlg, dlb, db1, db2 = res
            gsmall["cv_w_dw"][j] = dwdw[:CONV_K]
            gsmall["cv_b_dw"][j] = dbdw[0]
            gsmall["cv_ln_g"][j] = dlg[0]
            gsmall["cv_ln_b"][j] = dlb[0]
            gsmall["cv_b_pw1"][j] = db1[0]
            gsmall["cv_b_pw2"][j] = db2[0]
            dlast, w_first, l_first = dp1, full["cv_w_pw1"], j
            accumulate("cv_w_pw1", j, rec["h_mix"], dp1, N_CHIPS)
        dy, dy_bf, dg = _mm_nt_rms_bwd(dlast, w_first, l_first, rec["x_in"], row(full["mix_norm_g"][i]), dy)
        gsmall["mix_norm_g"][i] = dg[0]
        if i == n_layers // 2:
            late_pair_sums = pair_sums_of(group_grads(1), [BF] * len(big_names))

    small_names = SMALL_REPLICATED + SMALL_SHARDED
    small_full = [jnp.stack(gsmall[name]) for name in small_names]
    packed = _pack(small_full, 32 * N_CHIPS)
    rows_q = packed.shape[0] // N_CHIPS
    early = pair_sums_of(group_grads(0) + [packed.reshape(1, N_CHIPS, rows_q, LANES)], [BF] * len(big_names) + [F32])
    from_chips = _rs_chips(early)
    halves = [_rs_chip_add(ps, fc, w[name].shape[0], 0, half_sums[name])
              for name, ps, fc in zip(big_names, early, from_chips)]
    halves.append(_rs_chip_add(early[-1], from_chips[-1], 1, 0))
    swapped = _rs_swap(halves)
    grads = dict(zip(big_names, swapped))
    summed = _all_gather(swapped[-1]).reshape(-1, LANES)
    for name, gsum in zip(small_names, _unpack(summed, [a.shape for a in small_full])):
        if name in SMALL_SHARDED:
            n_loc = w[name].shape[-1]
            split = gsum.reshape(gsum.shape[:-1] + (N_CHIPS, n_loc))
            gsum = lax.dynamic_index_in_dim(split, chip, axis=split.ndim - 2, keepdims=False)
        grads[name] = gsum

    delta, new_m, new_v = {}, {}, {}
    for name in BIG_COL + BIG_ROW:
        shp = w[name].shape
        two_d = lambda a: a.reshape(shp[0] * shp[1], shp[2])
        d, nm, nv = _adamw(two_d(w[name]), two_d(grads[name]), two_d(m[name]), two_d(v[name]))
        delta[name], new_m[name], new_v[name] = d.reshape(shp), nm.reshape(shp), nv.reshape(shp)
    shapes = [w[name].shape for name in small_names]
    d, nm, nv = _adamw(*(_pack([src[name] for name in small_names], 256) for src in (w, grads, m, v)))
    for name, a, b_, c_ in zip(small_names, _unpack(d, shapes), _unpack(nm, shapes), _unpack(nv, shapes)):
        delta[name], new_m[name], new_v[name] = a, b_, c_

    return (loss, dy, *[grads[n] for n in WEIGHTS], *[delta[n] for n in WEIGHTS],
            *[new_m[n] for n in WEIGHTS], *[new_v[n] for n in WEIGHTS])


def kernel(x, mix_norm_g, sb_w_in, sb_q_norm_g, sb_k_norm_g, sg_z_norm_g, sg_w_spatial, sg_b_spatial, hyb_w_out, cv_w_pw1, cv_b_pw1, cv_w_dw, cv_b_dw, cv_ln_g, cv_ln_b, cv_w_pw2, cv_b_pw2, ffn_norm_g, ffn_w_up, ffn_w_dw, ffn_b_dw, ffn_w_down, loss_target, m_mix_norm_g, m_sb_w_in, m_sb_q_norm_g, m_sb_k_norm_g, m_sg_z_norm_g, m_sg_w_spatial, m_sg_b_spatial, m_hyb_w_out, m_cv_w_pw1, m_cv_b_pw1, m_cv_w_dw, m_cv_b_dw, m_cv_ln_g, m_cv_ln_b, m_cv_w_pw2, m_cv_b_pw2, m_ffn_norm_g, m_ffn_w_up, m_ffn_w_dw, m_ffn_b_dw, m_ffn_w_down, v_mix_norm_g, v_sb_w_in, v_sb_q_norm_g, v_sb_k_norm_g, v_sg_z_norm_g, v_sg_w_spatial, v_sg_b_spatial, v_hyb_w_out, v_cv_w_pw1, v_cv_b_pw1, v_cv_w_dw, v_cv_b_dw, v_cv_ln_g, v_cv_ln_b, v_cv_w_pw2, v_cv_b_pw2, v_ffn_norm_g, v_ffn_w_up, v_ffn_w_dw, v_ffn_b_dw, v_ffn_w_down):
    given = dict(locals())
    w = {n: given[n] for n in WEIGHTS}
    m = {n: given["m_" + n] for n in WEIGHTS}
    v = {n: given["v_" + n] for n in WEIGHTS}
    out = _step(x[0], loss_target[0], w, m, v)
    return (out[0], out[1][None], *out[2:])
```

```python
import functools
from typing import Callable, NamedTuple

import jax
import jax.numpy as jnp
from jax import lax
from jax.experimental import pallas as pl
from jax.experimental.pallas import tpu as pltpu

F32 = jnp.float32
BF = jnp.bfloat16
SDS = jax.ShapeDtypeStruct
HI = lax.Precision.HIGHEST
MESH = pl.DeviceIdType.MESH

NORM_EPS = 1e-6
HEAD_DIM = 64
ATT_BLOCK = 128
CHUNK = 128
PREP_CHUNKS = 2
CONV_K = 31
CONV_HALO = 32
FFN_K = 3
FFN_HALO = 16
LANES = 128
N_CHIPS = 4
VMEM_LIMIT_BYTES = 56 * 2**20

ADAM_LR = 0.001
ADAM_B1 = 0.9
ADAM_B2 = 0.999
ADAM_EPS = 1e-08
ADAM_WD = 0.01
ADAM_STEP = 10

NT_DIMS = (((1,), (1,)), ((), ()))
TN_DIMS = (((0,), (0,)), ((), ()))


def _call(body, **kw):
    return pl.pallas_call(body, **kw)


def _params(*sem):
    return pltpu.CompilerParams(dimension_semantics=sem, vmem_limit_bytes=VMEM_LIMIT_BYTES)


def _gelu(x):
    return 0.5 * x * (1.0 + lax.erf(x * 0.7071067811865476))


def _rms(x, g):
    y = x * lax.rsqrt(jnp.mean(x * x, axis=-1, keepdims=True) + NORM_EPS)
    return y * g


def _rms_fwd(x, g):
    t, d = x.shape
    tm = min(512, t)

    def body(x_ref, g_ref, o_ref):
        o_ref[...] = _rms(x_ref[...], g_ref[...]).astype(o_ref.dtype)

    return _call(
        body, name="rms_fwd", grid=(t // tm,),
        in_specs=[pl.BlockSpec((tm, d), lambda i: (i, 0)), pl.BlockSpec((1, d), lambda i: (0, 0))],
        out_specs=pl.BlockSpec((tm, d), lambda i: (i, 0)),
        out_shape=SDS((t, d), BF), compiler_params=_params("parallel"))(x, g)


def _mm_nn(a, w, l, bias=None, resid=None, out_dtype=F32, gather=None, norm_g=None):
    m, k = a.shape
    _, p_n, kw, n = w.shape
    assert k == kw
    normed = norm_g is not None
    assert not normed or (p_n == 1 and not gather)
    tm = min(512 if normed else 1024, m)
    tn = n if (normed or k * n * 2 <= 4 * 2**20) else n // 2
    nj = n // tn
    in_specs = [pl.BlockSpec((tm, k), lambda i, p, j: (i, 0)),
                pl.BlockSpec((None, None, k, tn), lambda i, p, j: (l, p, 0, j))]
    args = [a, w]
    if bias is not None:
        in_specs.append(pl.BlockSpec((1, tn), lambda i, p, j: (0, p * nj + j)))
        args.append(bias)
    if resid is not None:
        in_specs.append(pl.BlockSpec((tm, tn), lambda i, p, j: (i, p * nj + j)))
        args.append(resid)
    if normed:
        in_specs.append(pl.BlockSpec((1, n), lambda i, p, j: (0, 0)))
        args.append(norm_g)
    n_in = len(args)

    def body(*refs):
        acc = jnp.dot(refs[0][...], refs[1][...], preferred_element_type=F32)
        nxt = 2
        if bias is not None:
            acc = acc + refs[nxt][...]
            nxt += 1
        if resid is not None:
            acc = refs[nxt][...] + acc
        refs[n_in][...] = acc.astype(refs[n_in].dtype)
        if normed:
            refs[n_in + 1][...] = _rms(acc, refs[n_in - 1][...]).astype(BF)

    out_spec = pl.BlockSpec((tm, tn), lambda i, p, j: (i, p * nj + j))
    kw = dict(name="mm_nn", grid=(m // tm, p_n, nj), in_specs=in_specs,
              out_specs=[out_spec, out_spec] if normed else out_spec,
              out_shape=[SDS((m, n), out_dtype), SDS((m, n), BF)] if normed else SDS((m, p_n * n), out_dtype))
    if gather:
        (out,), gathered = _call_gathering(body, gather, args, **kw)
        return out, gathered
    return _call(body, compiler_params=_params("parallel", "parallel", "parallel"), **kw)(*args)


def _mm_nt(dy, w, l, out_dtype=F32):
    m, n_all = dy.shape
    _, p_n, r, n = w.shape
    assert n_all == p_n * n
    tm = min(512, m)

    def body(dy_ref, w_ref, o_ref):
        acc = lax.dot_general(dy_ref[:, 0:n], w_ref[0], NT_DIMS, preferred_element_type=F32)
        for p in range(1, p_n):
            acc = acc + lax.dot_general(dy_ref[:, p * n:(p + 1) * n], w_ref[p], NT_DIMS, preferred_element_type=F32)
        o_ref[...] = acc.astype(o_ref.dtype)

    return _call(
        body, name="mm_nt", grid=(m // tm,),
        in_specs=[pl.BlockSpec((tm, n_all), lambda i: (i, 0)),
                  pl.BlockSpec((None, p_n, r, n), lambda i: (l, 0, 0, 0))],
        out_specs=pl.BlockSpec((tm, r), lambda i: (i, 0)),
        out_shape=SDS((m, r), out_dtype),
        compiler_params=_params("parallel"))(dy, w)


def _mm_nt_rms_bwd(dy, w, l, x, g, dres):
    m, n_all = dy.shape
    _, p_n, r, n = w.shape
    assert n_all == p_n * n and x.shape == (m, r)
    tm = min(256, m)

    def body(dy_ref, w_ref, x_ref, g_ref, r_ref, dx_ref, dxb_ref, dg_ref):
        dh = lax.dot_general(dy_ref[:, 0:n], w_ref[0], NT_DIMS, preferred_element_type=F32)
        for p in range(1, p_n):
            dh = dh + lax.dot_general(dy_ref[:, p * n:(p + 1) * n], w_ref[p], NT_DIMS, preferred_element_type=F32)
        _, vjp = jax.vjp(_rms, x_ref[...], g_ref[...])
        dx, dg = vjp(dh)
        dx = dx + r_ref[...]
        dx_ref[...] = dx
        dxb_ref[...] = dx.astype(BF)

        @pl.when(pl.program_id(0) == 0)
        def _():
            dg_ref[...] = jnp.zeros_like(dg_ref)

        dg_ref[...] += dg

    row = pl.BlockSpec((tm, r), lambda i: (i, 0))
    vec = pl.BlockSpec((1, r), lambda i: (0, 0))
    return _call(
        body, name="mm_nt_rms_bwd", grid=(m // tm,),
        in_specs=[pl.BlockSpec((tm, n_all), lambda i: (i, 0)), pl.BlockSpec((None, p_n, r, n), lambda i: (l, 0, 0, 0)),
                  row, vec, row],
        out_specs=[row, row, vec], out_shape=[SDS((m, r), F32), SDS((m, r), BF), SDS((1, r), F32)],
        compiler_params=_params("arbitrary"))(dy, w, x, g, dres)


def _mm_tn(a, dy, p_n, n_layers, l, buf=None):
    m, k = a.shape
    n = dy.shape[1] // p_n
    tm = min(2048, m)
    tk = k if k <= 1024 else k // 2
    nm = m // tm

    def body(a_ref, dy_ref, *rest):
        o_ref, acc_ref = rest[-2], rest[-1]
        mi = pl.program_id(2)
        part = lax.dot_general(a_ref[...], dy_ref[...], TN_DIMS, preferred_element_type=F32)

        @pl.when(mi == 0)
        def _():
            acc_ref[...] = part

        @pl.when(mi > 0)
        def _():
            acc_ref[...] += part

        @pl.when(mi == nm - 1)
        def _():
            o_ref[...] = acc_ref[...].astype(o_ref.dtype)

    in_specs = [pl.BlockSpec((tm, tk), lambda p, kk, mi: (mi, kk)),
                pl.BlockSpec((tm, n), lambda p, kk, mi: (mi, p))]
    args = [a, dy]
    aliases = {}
    if buf is not None:
        in_specs.append(pl.BlockSpec(memory_space=pl.ANY))
        args.append(buf)
        aliases = {2: 0}
    return _call(
        body, name="mm_tn", grid=(p_n, k // tk, nm), in_specs=in_specs,
        out_specs=pl.BlockSpec((None, None, tk, n), lambda p, kk, mi: (l, p, kk, 0)),
        out_shape=SDS((n_layers, p_n, k, n), BF), scratch_shapes=[pltpu.VMEM((tk, n), F32)],
        input_output_aliases=aliases,
        compiler_params=_params("parallel", "parallel", "arbitrary"))(*args)


def _loss_grad(y, tgt):
    t, d = y.shape
    tm = min(512, t)

    def body(y_ref, t_ref, l_ref, d_ref, db_ref):
        err = y_ref[...] - t_ref[...]
        dy = err * (1.0 / d)
        d_ref[...] = dy
        db_ref[...] = dy.astype(BF)
        part = 0.5 * jnp.sum(jnp.sum(err * err, axis=1, keepdims=True) * (1.0 / d), axis=0, keepdims=True)

        @pl.when(pl.program_id(0) == 0)
        def _():
            l_ref[...] = jnp.zeros_like(l_ref)

        l_ref[...] += jnp.broadcast_to(part, l_ref.shape)

    row = pl.BlockSpec((tm, d), lambda i: (i, 0))
    return _call(
        body, name="loss_grad", grid=(t // tm,), in_specs=[row, row],
        out_specs=[pl.BlockSpec((1, LANES), lambda i: (0, 0)), row, row],
        out_shape=[SDS((1, LANES), F32), SDS((t, d), F32), SDS((t, d), BF)],
        compiler_params=_params("arbitrary"))(y, tgt)


def _prev_halo(tr, halo, col):
    return lambda i: (jnp.maximum(i * (tr // halo) - 1, 0), col)


def _next_halo(tr, halo, n_rows, col):
    return lambda i: (jnp.minimum((i + 1) * (tr // halo), n_rows // halo - 1), col)


def _shifted_back(x):
    return pltpu.roll(x, 1, 0), pltpu.roll(x, 2, 0)


def _conv3(x, w_ref, b_ref, col):
    x1, x2 = _shifted_back(x)
    return b_ref[:, col] + w_ref[pl.ds(0, 1), col] * x2 + w_ref[pl.ds(1, 1), col] * x1 + w_ref[pl.ds(2, 1), col] * x


def _ffn_mid_fwd(up, w_dw, b_dw, gather=None):
    t, f2 = up.shape
    f = f2 // 2
    tr = min(256, t)
    h = FFN_HALO

    def body(g_ref, gp_ref, v_ref, w_ref, b_ref, o_ref):
        first_tile = pl.program_id(0) == 0

        def strip(c, carry):
            col = pl.ds(pl.multiple_of(c * LANES, LANES), LANES)
            x = jnp.concatenate([jnp.where(first_tile, 0.0, gp_ref[:, col].astype(F32)), g_ref[:, col].astype(F32)], axis=0)
            gc = _conv3(x, w_ref, b_ref, col)[h:]
            o_ref[:, col] = (gc * jax.nn.sigmoid(gc) * v_ref[:, col].astype(F32)).astype(o_ref.dtype)
            return carry

        lax.fori_loop(0, f // LANES, strip, 0)

    kw = dict(name="ffn_mid_fwd", grid=(t // tr,),
              in_specs=[pl.BlockSpec((tr, f), lambda i: (i, 0)), pl.BlockSpec((h, f), _prev_halo(tr, h, 0)),
                        pl.BlockSpec((tr, f), lambda i: (i, 1)),
                        pl.BlockSpec((8, f), lambda i: (0, 0)), pl.BlockSpec((1, f), lambda i: (0, 0))],
              out_specs=pl.BlockSpec((tr, f), lambda i: (i, 0)), out_shape=SDS((t, f), BF))
    args = (up, up, up, w_dw, b_dw)
    if gather:
        (out,), gathered = _call_gathering(body, gather, args, **kw)
        return out, gathered
    return _call(body, compiler_params=_params("parallel"), **kw)(*args)


def _ffn_mid_bwd(up, da, w_dw, b_dw):
    t, f2 = up.shape
    f = f2 // 2
    tr = min(256, t)
    h = FFN_HALO
    n_tiles = t // tr

    def body(g_ref, gp_ref, gn_ref, v_ref, vn_ref, da_ref, dan_ref, w_ref, b_ref, dup_ref, dw_ref, db_ref):
        i = pl.program_id(0)
        last = i == n_tiles - 1
        n = tr + h

        @pl.when(i == 0)
        def _():
            dw_ref[...] = jnp.zeros_like(dw_ref)
            db_ref[...] = jnp.zeros_like(db_ref)

        def rows(tile_ref, next_ref, col):
            return jnp.concatenate([tile_ref[:, col].astype(F32), next_ref[:, col].astype(F32)], axis=0)

        def strip(c, carry):
            col = pl.ds(pl.multiple_of(c * LANES, LANES), LANES)
            x = jnp.concatenate([jnp.where(i == 0, 0.0, gp_ref[:, col].astype(F32)), rows(g_ref, gn_ref, col)], axis=0)
            x1, x2 = _shifted_back(x)
            w0, w1, w2 = (w_ref[pl.ds(k, 1), col] for k in range(FFN_K))
            gc = (b_ref[:, col] + w0 * x2 + w1 * x1 + w2 * x)[h:]
            dav = rows(da_ref, dan_ref, col)
            sg = jax.nn.sigmoid(gc)
            silu = gc * sg
            dup_ref[:, pl.ds(pl.multiple_of(f + c * LANES, LANES), LANES)] = (dav * silu)[:tr].astype(dup_ref.dtype)
            dgc = dav * rows(v_ref, vn_ref, col) * (sg + silu * (1.0 - sg))
            dgc = jnp.concatenate([dgc[:tr], jnp.where(last, 0.0, dgc[tr:])], axis=0)
            d1, d2 = pltpu.roll(dgc, n - 1, 0), pltpu.roll(dgc, n - 2, 0)
            dup_ref[:, col] = (w2 * dgc + w1 * d1 + w0 * d2)[:tr].astype(dup_ref.dtype)
            dgt = dgc[:tr]
            for k, past in enumerate((x2, x1, x)):
                dw_ref[pl.ds(k, 1), col] += jnp.sum(past[h:h + tr] * dgt, axis=0, keepdims=True)
            db_ref[:, col] += jnp.sum(dgt, axis=0, keepdims=True)
            return carry

        lax.fori_loop(0, f // LANES, strip, 0)

    tile = lambda col: pl.BlockSpec((tr, f), lambda i: (i, col))
    nxt = lambda col: pl.BlockSpec((h, f), _next_halo(tr, h, t, col))
    return _call(
        body, name="ffn_mid_bwd", grid=(n_tiles,),
        in_specs=[tile(0), pl.BlockSpec((h, f), _prev_halo(tr, h, 0)), nxt(0), tile(1), nxt(1), tile(0), nxt(0),
                  pl.BlockSpec((8, f), lambda i: (0, 0)), pl.BlockSpec((1, f), lambda i: (0, 0))],
        out_specs=[pl.BlockSpec((tr, f2), lambda i: (i, 0)), pl.BlockSpec((8, f), lambda i: (0, 0)),
                   pl.BlockSpec((1, f), lambda i: (0, 0))],
        out_shape=[SDS((t, f2), BF), SDS((8, f), F32), SDS((1, f), F32)],
        compiler_params=_params("arbitrary"))(up, up, up, up, up, da, da, w_dw, b_dw)


def _ln_silu(yc, g, b):
    mu = jnp.mean(yc, axis=-1, keepdims=True)
    xc = yc - mu
    y = xc * lax.rsqrt(jnp.mean(xc * xc, axis=-1, keepdims=True) + NORM_EPS)
    return jax.nn.silu(y * g + b)


SUBLANES = 8
CONV_PAD = 24
SHIFT_CHUNK = 40
TAP_ROWS = 64


def _glu(a, g):
    return a.astype(F32) * jax.nn.sigmoid(g.astype(F32))


def _glu_strip(ygs_ref, first_tile, a_ref, ap_ref, g_ref, gp_ref, col, h, tr):
    ygs_ref[pl.ds(0, h), :] = jnp.where(first_tile, 0.0, _glu(ap_ref[:, col], gp_ref[:, col]))
    ygs_ref[pl.ds(h, tr), :] = _glu(a_ref[:, col], g_ref[:, col])


def _shift_past(sh_ref, ygs_ref, h, n):
    first = h - CONV_PAD - SUBLANES
    for u0 in range(0, n + CONV_PAD, SHIFT_CHUNK):
        x = ygs_ref[pl.ds(first + u0, SHIFT_CHUNK + SUBLANES), :]
        for r in range(1, SUBLANES):
            sh_ref[r, pl.ds(u0, SHIFT_CHUNK), :] = pltpu.roll(x, r, 0)[SUBLANES:]


def _past_rows(sh_ref, ygs_ref, h, n, s, row0=0):
    a, r = divmod(s, SUBLANES)
    if r == 0:
        return ygs_ref[pl.ds(row0 + h - SUBLANES * a, n), :]
    return sh_ref[r, pl.ds(row0 + CONV_PAD - SUBLANES * a, n), :]


def _conf_mid_fwd(p1, w_dw, b_dw, ln_g, ln_b, gather=None):
    t, w2 = p1.shape
    w = w2 // 2
    tr = min(256, t)
    h = CONV_HALO
    rc = 32

    def body(a_ref, ap_ref, g_ref, gp_ref, w_ref, b_ref, lg_ref, lb_ref, o_ref, yc_ref, ygs_ref, sh_ref):
        first_tile = pl.program_id(0) == 0

        def strip(c, carry):
            col = pl.ds(pl.multiple_of(c * LANES, LANES), LANES)
            _glu_strip(ygs_ref, first_tile, a_ref, ap_ref, g_ref, gp_ref, col, h, tr)
            _shift_past(sh_ref, ygs_ref, h, tr)
            acc = jnp.broadcast_to(b_ref[:, col], (tr, LANES))
            for k in range(CONV_K):
                acc = acc + w_ref[pl.ds(k, 1), col] * _past_rows(sh_ref, ygs_ref, h, tr, CONV_K - 1 - k)
            yc_ref[:, col] = acc
            return carry

        lax.fori_loop(0, w // LANES, strip, 0)

        def rows(r, carry):
            rs = pl.ds(pl.multiple_of(r * rc, rc), rc)
            o_ref[rs, :] = _ln_silu(yc_ref[rs, :], lg_ref[...], lb_ref[...]).astype(o_ref.dtype)
            return carry

        lax.fori_loop(0, tr // rc, rows, 0)

    vec = pl.BlockSpec((1, w), lambda i: (0, 0))
    tile = pl.BlockSpec((tr, w), lambda i: (i, 0))
    kw = dict(name="conf_mid_fwd", grid=(t // tr,),
              in_specs=[tile, pl.BlockSpec((h, w), _prev_halo(tr, h, 0)),
                        pl.BlockSpec((tr, w), lambda i: (i, 1)), pl.BlockSpec((h, w), _prev_halo(tr, h, 1)),
                        pl.BlockSpec((32, w), lambda i: (0, 0)), vec, vec, vec],
              out_specs=[tile, tile], out_shape=[SDS((t, w), BF), SDS((t, w), F32)],
              scratch_shapes=[pltpu.VMEM((h + tr, LANES), F32), pltpu.VMEM((SUBLANES, tr + CONV_PAD, LANES), F32)])
    args = (p1, p1, p1, p1, w_dw, b_dw, ln_g, ln_b)
    if gather:
        return _call_gathering(body, gather, args, **kw)
    return _call(body, compiler_params=_params("parallel"), **kw)(*args)


def _conf_mid_bwd(p1, yc, dys, dy, w_dw, ln_g, ln_b, exchange=None):
    t, w2 = p1.shape
    w = w2 // 2
    tr = min(256, t)
    h = CONV_HALO
    rc = 32
    n_tiles = t // tr

    def body(a_ref, ap_ref, g_ref, gp_ref, yc_ref, ycn_ref, dys_ref, dysn_ref, dy_ref, w_ref, lg_ref, lb_ref,
             dp_ref, dw_ref, db_ref, dlg_ref, dlb_ref, db1_ref, db2_ref, dyc_ref, ygs_ref, sh_ref, shf_ref, dwacc_ref):
        i = pl.program_id(0)
        last = i == n_tiles - 1

        @pl.when(i == 0)
        def _():
            for ref in (dw_ref, db_ref, dlg_ref, dlb_ref, db1_ref, db2_ref):
                ref[...] = jnp.zeros_like(ref)

        def ln_rows(r, carry):
            rs = pl.ds(pl.multiple_of(r * rc, rc), rc)
            _, vjp = jax.vjp(_ln_silu, yc_ref[rs, :], lg_ref[...], lb_ref[...])
            dyc, dlg, dlb = vjp(dys_ref[rs, :].astype(F32))
            dyc_ref[rs, :] = dyc
            dlg_ref[...] += dlg
            dlb_ref[...] += dlb
            return carry

        lax.fori_loop(0, tr // rc, ln_rows, 0)
        _, vjp = jax.vjp(_ln_silu, ycn_ref[...], lg_ref[...], lb_ref[...])
        dyc_ref[pl.ds(tr, h), :] = jnp.where(last, 0.0, vjp(dysn_ref[...].astype(F32))[0])
        db2_ref[...] += jnp.sum(dy_ref[...], axis=0, keepdims=True)

        def back(c, carry):
            col = pl.ds(pl.multiple_of(c * LANES, LANES), LANES)
            gcol = pl.ds(pl.multiple_of(w + c * LANES, LANES), LANES)
            _glu_strip(ygs_ref, i == 0, a_ref, ap_ref, g_ref, gp_ref, col, h, tr)
            _shift_past(sh_ref, ygs_ref, h, tr)
            for u0 in range(0, tr + CONV_PAD, SHIFT_CHUNK):
                part = dyc_ref[pl.ds(u0, SHIFT_CHUNK + SUBLANES), col]
                for r in range(1, SUBLANES):
                    shf_ref[r, pl.ds(u0, SHIFT_CHUNK), :] = pltpu.roll(part, SHIFT_CHUNK + SUBLANES - r, 0)[:SHIFT_CHUNK]
            for r0 in range(0, tr, TAP_ROWS):
                rows = pl.ds(r0, TAP_ROWS)
                dyc = dyc_ref[rows, col]
                dyg = jnp.zeros((TAP_ROWS, LANES), F32)
                for k in range(CONV_K):
                    s = CONV_K - 1 - k
                    a, r = divmod(s, SUBLANES)
                    if r == 0:
                        future = dyc_ref[pl.ds(r0 + SUBLANES * a, TAP_ROWS), col]
                    else:
                        future = shf_ref[r, pl.ds(r0 + SUBLANES * a, TAP_ROWS), :]
                    dyg = dyg + w_ref[pl.ds(k, 1), col] * future
                    prod = _past_rows(sh_ref, ygs_ref, h, TAP_ROWS, s, r0) * dyc
                    part = prod[0:SUBLANES]
                    for q in range(1, TAP_ROWS // SUBLANES):
                        part = part + prod[q * SUBLANES:(q + 1) * SUBLANES]
                    if r0 == 0:
                        dwacc_ref[k] = part
                    else:
                        dwacc_ref[k] += part
                sg = jax.nn.sigmoid(g_ref[rows, col].astype(F32))
                da = dyg * sg
                dg = dyg * a_ref[rows, col].astype(F32) * sg * (1.0 - sg)
                dp_ref[rows, col] = da.astype(dp_ref.dtype)
                dp_ref[rows, gcol] = dg.astype(dp_ref.dtype)
                db_ref[:, col] += jnp.sum(dyc, axis=0, keepdims=True)
                db1_ref[:, col] += jnp.sum(da, axis=0, keepdims=True)
                db1_ref[:, gcol] += jnp.sum(dg, axis=0, keepdims=True)
            for k in range(CONV_K):
                dw_ref[pl.ds(k, 1), col] += jnp.sum(dwacc_ref[k], axis=0, keepdims=True)
            return carry

        lax.fori_loop(0, w // LANES, back, 0)

    tile = lambda col: pl.BlockSpec((tr, w), lambda i: (i, col))
    prv = lambda col: pl.BlockSpec((h, w), _prev_halo(tr, h, col))
    nxt = pl.BlockSpec((h, w), _next_halo(tr, h, t, 0))
    vec = pl.BlockSpec((1, w), lambda i: (0, 0))
    kw = dict(
        name="conf_mid_bwd", grid=(n_tiles,),
        in_specs=[tile(0), prv(0), tile(1), prv(1), tile(0), nxt, tile(0), nxt, tile(0),
                  pl.BlockSpec((32, w), lambda i: (0, 0)), vec, vec],
        out_specs=[pl.BlockSpec((tr, w2), lambda i: (i, 0)), pl.BlockSpec((32, w), lambda i: (0, 0)), vec, vec, vec,
                   pl.BlockSpec((1, w2), lambda i: (0, 0)), vec],
        out_shape=[SDS((t, w2), BF), SDS((32, w), F32), SDS((1, w), F32), SDS((1, w), F32), SDS((1, w), F32),
                   SDS((1, w2), F32), SDS((1, w), F32)],
        scratch_shapes=[pltpu.VMEM((tr + h, w), F32), pltpu.VMEM((h + tr, LANES), F32),
                        pltpu.VMEM((SUBLANES, tr + CONV_PAD, LANES), F32), pltpu.VMEM((SUBLANES, tr + CONV_PAD, LANES), F32),
                        pltpu.VMEM((32, SUBLANES, LANES), F32)])
    args = (p1, p1, p1, p1, yc, yc, dys, dys, dy, w_dw, ln_g, ln_b)
    if exchange is not None:
        return _call_hosting(body, exchange, args, **kw)
    return _call(body, compiler_params=_params("arbitrary"), **kw)(*args)


def _group_matrices():
    i = lax.broadcasted_iota(jnp.int32, (512, 512), 0)
    j = lax.broadcasted_iota(jnp.int32, (512, 512), 1)
    mean64 = jnp.where(i // HEAD_DIM == j // HEAD_DIM, 1.0 / HEAD_DIM, 0.0).astype(F32)
    fold64 = jnp.where(i % HEAD_DIM == j % HEAD_DIM, 1.0, 0.0).astype(F32)
    return mean64, fold64


def _split_dot(x, mat):
    hi = x.astype(BF)
    lo = (x - hi.astype(F32)).astype(BF)
    mb = mat.astype(BF)
    return jnp.dot(hi, mb, preferred_element_type=F32) + jnp.dot(lo, mb, preferred_element_type=F32)


@jax.custom_vjp
def _group_sum(x, mat):
    return _split_dot(x, mat)


_group_sum.defvjp(lambda x, mat: (_split_dot(x, mat), mat), lambda mat, ct: (_split_dot(ct, mat), jnp.zeros_like(mat)))


def _bf_dot_plain(a, b):
    return jnp.dot(a.astype(BF), b.astype(BF), preferred_element_type=F32)


@jax.custom_vjp
def _bf_dot(a, b):
    return _bf_dot_plain(a, b)


def _bf_dot_bwd(res, ct):
    a, b = res
    cb = ct.astype(BF)
    return (lax.dot_general(cb, b.astype(BF), NT_DIMS, preferred_element_type=F32),
            lax.dot_general(a.astype(BF), cb, TN_DIMS, preferred_element_type=F32))


_bf_dot.defvjp(lambda a, b: (_bf_dot_plain(a, b), (a, b)), _bf_dot_bwd)


def _prep_tile(proj, qg, kg, zg, ws, bexp, mean64, differentiated=False):
    sw = 512
    q, k, v, u, z = (proj[:, n * sw:(n + 1) * sw] for n in range(5))
    group_sum, dot = (_group_sum, _bf_dot) if differentiated else (_split_dot, _bf_dot_plain)

    def group_norm(x):
        return x * lax.rsqrt(group_sum(x * x, mean64) + NORM_EPS)

    qn = group_norm(q) * qg
    kn = group_norm(k) * kg
    zn = group_norm(_gelu(z)) * zg
    row = lax.broadcasted_iota(jnp.int32, (CHUNK, CHUNK), 0)
    col = lax.broadcasted_iota(jnp.int32, (CHUNK, CHUNK), 1)
    first = lax.broadcasted_iota(jnp.int32, (1, LANES), 1) < HEAD_DIM
    wm = [jnp.where(col <= row, ws[g], 0.0) for g in range(2 * (sw // LANES))]
    chunks = []
    for ci in range(proj.shape[0] // CHUNK):
        parts = []
        for pr in range(sw // LANES):
            zp = zn[ci * CHUNK:(ci + 1) * CHUNK, pr * LANES:(pr + 1) * LANES]
            parts.append(jnp.where(first, dot(wm[2 * pr], zp), dot(wm[2 * pr + 1], zp)))
        chunks.append(jnp.concatenate(parts, axis=1) + bexp)
    s = chunks[0] if len(chunks) == 1 else jnp.concatenate(chunks, axis=0)
    return qn, kn, v, _gelu(u) * s


def _mix_prep_fwd(proj, qg, kg, zg, w_s, l, bexp, mean64, gather=None):
    t = proj.shape[0]
    tr = PREP_CHUNKS * CHUNK

    def body(p_ref, qg_ref, kg_ref, zg_ref, ws_ref, be_ref, m_ref, qkv_ref, go_ref):
        qn, kn, v, go = _prep_tile(p_ref[...], qg_ref[...], kg_ref[...], zg_ref[...], ws_ref[...], be_ref[...], m_ref[...])
        qkv_ref[:, 0:512] = qn.astype(BF)
        qkv_ref[:, 512:1024] = kn.astype(BF)
        qkv_ref[:, 1024:1536] = v.astype(BF)
        go_ref[...] = go.astype(BF)

    vec = pl.BlockSpec((1, 512), lambda i: (0, 0))
    kw = dict(name="mix_prep_fwd", grid=(t // tr,),
              in_specs=[pl.BlockSpec((tr, 2560), lambda i: (i, 0)), vec, vec, vec,
                        pl.BlockSpec((None, 8, CHUNK, CHUNK), lambda i: (l, 0, 0, 0)),
                        pl.BlockSpec((CHUNK, 512), lambda i: (0, 0)), pl.BlockSpec((512, 512), lambda i: (0, 0))],
              out_specs=[pl.BlockSpec((tr, 1536), lambda i: (i, 0)), pl.BlockSpec((tr, 512), lambda i: (i, 0))],
              out_shape=[SDS((t, 1536), BF), SDS((t, 512), BF)])
    args = (proj, qg, kg, zg, w_s, bexp, mean64)
    if gather:
        return _call_gathering(body, gather, args, **kw)
    return _call(body, compiler_params=_params("parallel"), **kw)(*args)


def _mix_prep_bwd(proj, dq, dk, dv, dmix, qg, kg, zg, w_s, l, bexp, mean64, fold64):
    t = proj.shape[0]
    tr = PREP_CHUNKS * CHUNK
    n_tiles = t // tr

    def body(p_ref, dq_ref, dk_ref, dv_ref, dgo_ref, qg_ref, kg_ref, zg_ref, ws_ref, be_ref, m_ref, f_ref,
             dp_ref, dqg_ref, dkg_ref, dzg_ref, dws_ref, dbe_ref):
        i = pl.program_id(0)

        @pl.when(i == 0)
        def _():
            for ref in (dqg_ref, dkg_ref, dzg_ref, dws_ref, dbe_ref):
                ref[...] = jnp.zeros_like(ref)

        fn = functools.partial(_prep_tile, mean64=m_ref[...], differentiated=True)
        _, vjp = jax.vjp(fn, p_ref[...], qg_ref[...], kg_ref[...], zg_ref[...], ws_ref[...], be_ref[...])
        dp, dqg, dkg, dzg, dws, dbe = vjp((dq_ref[...], dk_ref[...], dv_ref[...], dgo_ref[...]))
        dp_ref[...] = dp.astype(BF)
        dqg_ref[pl.ds(0, 1), :] += dqg
        dkg_ref[pl.ds(0, 1), :] += dkg
        dzg_ref[pl.ds(0, 1), :] += dzg
        dws_ref[...] += dws
        dbe_ref[...] += dbe

        @pl.when(i == n_tiles - 1)
        def _():
            dqg_ref[...] = jnp.dot(dqg_ref[...], f_ref[...], precision=HI, preferred_element_type=F32)
            dkg_ref[...] = jnp.dot(dkg_ref[...], f_ref[...], precision=HI, preferred_element_type=F32)
            dbe_ref[...] = jnp.dot(dbe_ref[...], m_ref[...] * float(HEAD_DIM), precision=HI, preferred_element_type=F32)

    vec = pl.BlockSpec((1, 512), lambda i: (0, 0))
    acc = pl.BlockSpec((8, 512), lambda i: (0, 0))
    sq = pl.BlockSpec((512, 512), lambda i: (0, 0))
    row = pl.BlockSpec((tr, 512), lambda i: (i, 0))
    return _call(
        body, name="mix_prep_bwd", grid=(n_tiles,),
        in_specs=[pl.BlockSpec((tr, 2560), lambda i: (i, 0)), row, row, row, pl.BlockSpec((tr, 512), lambda i: (i, 1)),
                  vec, vec, vec, pl.BlockSpec((None, 8, CHUNK, CHUNK), lambda i: (l, 0, 0, 0)),
                  pl.BlockSpec((CHUNK, 512), lambda i: (0, 0)), sq, sq],
        out_specs=[pl.BlockSpec((tr, 2560), lambda i: (i, 0)), acc, acc, acc,
                   pl.BlockSpec((8, CHUNK, CHUNK), lambda i: (0, 0, 0)), pl.BlockSpec((CHUNK, 512), lambda i: (0, 0))],
        out_shape=[SDS((t, 2560), BF), SDS((8, 512), F32), SDS((8, 512), F32), SDS((8, 512), F32),
                   SDS((8, CHUNK, CHUNK), F32), SDS((CHUNK, 512), F32)],
        compiler_params=_params("arbitrary"))(proj, dq, dk, dv, dmix, qg, kg, zg, w_s, bexp, mean64, fold64)


def _sb_logs(qh, kb, valid):
    z = lax.dot_general(qh, kb, NT_DIMS, preferred_element_type=F32) * (HEAD_DIM ** -0.5)
    soft = jnp.log1p(jnp.exp(-jnp.abs(z)))
    lk_raw = -(jnp.maximum(z, 0.0) + soft)
    ls = -(jnp.maximum(-z, 0.0) + soft)
    return lk_raw, ls, jnp.where(valid, lk_raw, 0.0)


def _sb_weights(ls, run, tail, valid):
    return jnp.where(valid, jnp.exp(ls + run + tail), 0.0)


def _att_masks(b):
    row = lax.broadcasted_iota(jnp.int32, (b, b), 0)
    col = lax.broadcasted_iota(jnp.int32, (b, b), 1)
    first = lax.broadcasted_iota(jnp.int32, (1, LANES), 1) < HEAD_DIM
    return row, col, first


N_PAIRS = 4
ROW_SPLIT = 1


def _load_kv(qkv_hbm, k_scr, v_scr, sems, group, width):
    ck = pltpu.make_async_copy(qkv_hbm.at[:, pl.ds(pl.multiple_of(512 + group * width, LANES), width)], k_scr, sems.at[0])
    cv = pltpu.make_async_copy(qkv_hbm.at[:, pl.ds(pl.multiple_of(1024 + group * width, LANES), width)], v_scr, sems.at[1])
    ck.start()
    cv.start()
    ck.wait()
    cv.wait()


def _split_heads(ref, pair, first):
    x = ref[:, pair * LANES:(pair + 1) * LANES]
    zero = jnp.zeros_like(x)
    return jnp.where(first, x, zero), jnp.where(first, zero, x)


def _any_weight_left(runs):
    top = functools.reduce(jnp.maximum, runs)
    return jnp.max(jnp.exp(top)) > 0.0


def _attn_fwd(qkv, pairs_per_step=4, gather=None):
    t = qkv.shape[0]
    b = ATT_BLOCK
    nq = t // b
    width = pairs_per_step * LANES
    n_heads = 2 * pairs_per_step

    def body(q_ref, qkv_hbm, ob_ref, o32_ref, k_scr, v_scr, acc_ref, run_ref, sems):
        group, qi = pl.program_id(0), pl.program_id(1)

        @pl.when(qi == 0)
        def _():
            _load_kv(qkv_hbm, k_scr, v_scr, sems, group, width)

        row, col, first = _att_masks(b)
        qh = [x for pr in range(pairs_per_step) for x in _split_heads(q_ref, pr, first)]
        upper = jnp.where(row > col, 1.0, 0.0).astype(BF)
        acc_ref[...] = jnp.zeros_like(acc_ref)
        run_ref[...] = jnp.zeros_like(run_ref)
        heads = range(n_heads)

        def step(carry):
            j, _ = carry
            rows = pl.ds(pl.multiple_of(j * b, b), b)
            valid = jnp.logical_or(j != qi, col < row)
            lanes = [pl.ds((hh // 2) * LANES, LANES) for hh in heads]
            logs = [_sb_logs(qh[hh], k_scr[rows, lanes[hh]], valid) for hh in heads]
            tails = [_split_dot(logs[hh][2], upper) for hh in heads]
            for hh in heads:
                wgt = _sb_weights(logs[hh][1], run_ref[hh], tails[hh], valid)
                acc_ref[hh] += jnp.dot(wgt.astype(BF), v_scr[rows, lanes[hh]], preferred_element_type=F32)
            for hh in heads:
                run_ref[hh] += jnp.sum(logs[hh][2], axis=1, keepdims=True)
            return j - 1, _any_weight_left([run_ref[hh] for hh in heads])

        lax.while_loop(lambda c: jnp.logical_and(c[0] >= 0, c[1]), step, (qi, jnp.bool_(True)))
        for pr in range(pairs_per_step):
            out = jnp.where(first, acc_ref[2 * pr], acc_ref[2 * pr + 1])
            ob_ref[:, pr * LANES:(pr + 1) * LANES] = out.astype(BF)
            o32_ref[:, pr * LANES:(pr + 1) * LANES] = out

    blk = pl.BlockSpec((b, width), lambda g, qi: (qi, g))
    kw = dict(name="attn_fwd", grid=(N_PAIRS // pairs_per_step, nq),
              in_specs=[blk, pl.BlockSpec(memory_space=pl.ANY)], out_specs=[blk, blk],
              out_shape=[SDS((t, 512), BF), SDS((t, 512), F32)],
              scratch_shapes=[pltpu.VMEM((t, width), BF), pltpu.VMEM((t, width), BF),
                              pltpu.VMEM((n_heads, b, LANES), F32), pltpu.VMEM((n_heads, b, 1), F32),
                              pltpu.SemaphoreType.DMA((2,))])
    if gather:
        return _call_gathering(body, gather, (qkv, qkv), **kw)
    return _call(body, compiler_params=_params("arbitrary", "arbitrary"), **kw)(qkv, qkv)


def _attn_bwd(qkv, a32, dmix, pairs_per_step=2):
    t = qkv.shape[0]
    b = ATT_BLOCK
    bh = b // ROW_SPLIT
    nq = t // b
    width = pairs_per_step * LANES
    n_heads = 2 * pairs_per_step

    def body(q_ref, a_ref, da_ref, qkv_hbm, dq_ref, dk_hbm, dv_hbm,
             k_scr, v_scr, dk_scr, dv_scr, dqa_ref, run_ref, rung_ref, sems):
        group, qi = pl.program_id(0), pl.program_id(1)

        @pl.when(qi == 0)
        def _():
            _load_kv(qkv_hbm, k_scr, v_scr, sems, group, width)
            dk_scr[...] = jnp.zeros_like(dk_scr)
            dv_scr[...] = jnp.zeros_like(dv_scr)

        row, col, first = _att_masks(b)
        qh, dah, dtot = [], [], []
        for pr in range(pairs_per_step):
            qh += _split_heads(q_ref, pr, first)
            da = da_ref[:, pr * LANES:(pr + 1) * LANES]
            prod = da * a_ref[:, pr * LANES:(pr + 1) * LANES]
            dtot += [jnp.sum(jnp.where(first, prod, 0.0), axis=1, keepdims=True),
                     jnp.sum(jnp.where(first, 0.0, prod), axis=1, keepdims=True)]
            dah += [jnp.where(first, da, 0.0).astype(BF), jnp.where(first, 0.0, da).astype(BF)]
        upper = jnp.where(row > col, 1.0, 0.0).astype(BF)
        lower_incl = jnp.where(row >= col, 1.0, 0.0).astype(BF)
        dqa_ref[...] = jnp.zeros_like(dqa_ref)
        run_ref[...] = jnp.zeros_like(run_ref)
        rung_ref[...] = jnp.zeros_like(rung_ref)
        chains = [(hh, s) for hh in range(n_heads) for s in range(ROW_SPLIT)]
        ids = range(len(chains))
        part = lambda x, s: x[s * bh:(s + 1) * bh]
        row_h = lax.broadcasted_iota(jnp.int32, (bh, b), 0)
        col_h = lax.broadcasted_iota(jnp.int32, (bh, b), 1)
        causal = [col_h < row_h + s * bh for s in range(ROW_SPLIT)]
        qc = [part(qh[hh], s) for hh, s in chains]
        dac = [part(dah[hh], s) for hh, s in chains]
        dtc = [part(dtot[hh], s) for hh, s in chains]
        lanes = [pl.ds((hh // 2) * LANES, LANES) for hh, _ in chains]

        def step(carry):
            j, _ = carry
            rows = pl.ds(pl.multiple_of(j * b, b), b)
            valid = [jnp.logical_or(j != qi, causal[s]) for _, s in chains]
            logs = [_sb_logs(qc[c], k_scr[rows, lanes[c]], valid[c]) for c in ids]
            runs = [run_ref[c] for c in ids]
            new_runs = [runs[c] + jnp.sum(logs[c][2], axis=1, keepdims=True) for c in ids]
            alive = _any_weight_left(new_runs)
            dps = [lax.dot_general(dac[c], v_scr[rows, lanes[c]], NT_DIMS, preferred_element_type=F32) for c in ids]
            tails = [_split_dot(logs[c][2], upper) for c in ids]
            wgts = [_sb_weights(logs[c][1], runs[c], tails[c], valid[c]) for c in ids]
            gs = [wgts[c] * dps[c] for c in ids]
            g_froms = [_split_dot(gs[c], lower_incl) for c in ids]
            for c in ids:
                lk_raw, ls, _ = logs[c]
                dlk = jnp.where(valid[c], dtc[c] - rung_ref[c] - g_froms[c], 0.0)
                dz = ((gs[c] * jnp.exp(lk_raw) - dlk * jnp.exp(ls)) * (HEAD_DIM ** -0.5)).astype(BF)
                dqa_ref[c] += jnp.dot(dz, k_scr[rows, lanes[c]], preferred_element_type=F32)
                dk_scr[rows, lanes[c]] += lax.dot_general(dz, qc[c], TN_DIMS, preferred_element_type=F32)
                dv_scr[rows, lanes[c]] += lax.dot_general(wgts[c].astype(BF), dac[c], TN_DIMS, preferred_element_type=F32)
            for c in ids:
                rung_ref[c] += jnp.sum(gs[c], axis=1, keepdims=True)
                run_ref[c] = new_runs[c]
            return j - 1, alive

        lax.while_loop(lambda c: jnp.logical_and(c[0] >= 0, c[1]), step, (qi, jnp.bool_(True)))
        for pr in range(pairs_per_step):
            for s in range(ROW_SPLIT):
                c0 = 2 * pr * ROW_SPLIT + s
                dq_ref[pl.ds(s * bh, bh), pr * LANES:(pr + 1) * LANES] = jnp.where(first, dqa_ref[c0], dqa_ref[c0 + ROW_SPLIT])

        @pl.when(qi == nq - 1)
        def _():
            cols = pl.ds(pl.multiple_of(group * width, LANES), width)
            ck = pltpu.make_async_copy(dk_scr, dk_hbm.at[:, cols], sems.at[0])
            cv = pltpu.make_async_copy(dv_scr, dv_hbm.at[:, cols], sems.at[1])
            ck.start()
            cv.start()
            ck.wait()
            cv.wait()

    blk = pl.BlockSpec((b, width), lambda g, qi: (qi, g))
    anywhere = pl.BlockSpec(memory_space=pl.ANY)
    return _call(
        body, name="attn_bwd", grid=(N_PAIRS // pairs_per_step, nq),
        in_specs=[blk, blk, blk, anywhere], out_specs=[blk, anywhere, anywhere],
        out_shape=[SDS((t, 512), F32), SDS((t, 512), F32), SDS((t, 512), F32)],
        scratch_shapes=[pltpu.VMEM((t, width), BF), pltpu.VMEM((t, width), BF),
                        pltpu.VMEM((t, width), F32), pltpu.VMEM((t, width), F32),
                        pltpu.VMEM((n_heads * ROW_SPLIT, bh, LANES), F32), pltpu.VMEM((n_heads * ROW_SPLIT, bh, 1), F32),
                        pltpu.VMEM((n_heads * ROW_SPLIT, bh, 1), F32), pltpu.SemaphoreType.DMA((2,))],
        compiler_params=_params("arbitrary", "arbitrary"))(qkv, a32, dmix, qkv)


def _adamw(w, g, m, v):
    n, c = w.shape
    tr = min(256, n)
    assert n % tr == 0

    def body(w_ref, g_ref, m_ref, v_ref, d_ref, nm_ref, nv_ref):
        g = g_ref[...]
        m = ADAM_B1 * m_ref[...] + (1.0 - ADAM_B1) * g
        v = ADAM_B2 * v_ref[...] + (1.0 - ADAM_B2) * jnp.square(g)
        m_hat = m / (1.0 - ADAM_B1 ** ADAM_STEP)
        v_hat = v / (1.0 - ADAM_B2 ** ADAM_STEP)
        d_ref[...] = -ADAM_LR * (m_hat / (jnp.sqrt(v_hat) + ADAM_EPS) + ADAM_WD * w_ref[...])
        nm_ref[...] = m
        nv_ref[...] = v

    blk = pl.BlockSpec((tr, c), lambda i: (i, 0))
    return _call(
        body, name="adamw", grid=(n // tr,), in_specs=[blk] * 4, out_specs=[blk] * 3,
        out_shape=[SDS((n, c), F32)] * 3, compiler_params=_params("parallel"))(w, g, m, v)


def _mesh_pos():
    return lax.axis_index("x"), lax.axis_index("y"), lax.axis_index("c")


def _other_chips(x, y):
    return [(1 - x, y), (x, 1 - y), (1 - x, 1 - y)]


HBM_SPEC = pl.BlockSpec(memory_space=pltpu.HBM)


GATHER_COPIES = 6


def _gather_steps(s_ref, o_ref, send_sems, recv_sems, local_sems, slot):
    h = s_ref.shape[1] // 2
    x, y, c = _mesh_pos()
    sibling = (x, y, 1 - c)
    chips = _other_chips(x, y)
    base = GATHER_COPIES * slot

    def half(px, py, hc):
        return o_ref.at[:, 2 * px + py, pl.ds(hc * h, h), :]

    def copy(k, dst, to, src=None):
        return pltpu.make_async_remote_copy(
            src_ref=dst if src is None else src, dst_ref=dst, send_sem=send_sems.at[base + k],
            recv_sem=recv_sems.at[base + k], device_id=to, device_id_type=MESH)

    mine = pltpu.make_async_copy(s_ref, o_ref.at[:, 2 * x + y], local_sems.at[slot])
    first = [copy(j, half(x, y, c), (*chip, c), src=s_ref.at[:, pl.ds(c * h, h), :]) for j, chip in enumerate(chips)]
    passed = [copy(3 + j, half(*chip, c), sibling) for j, chip in enumerate(chips)]

    def start():
        mine.start()
        for cp in first:
            cp.start()

    def finish():
        for j, chip in enumerate(chips):
            copy(j, half(*chip, c), (x, y, c)).wait_recv()
            passed[j].start()
        for j, chip in enumerate(chips):
            copy(3 + j, half(*chip, 1 - c), (x, y, c)).wait_recv()
        for cp in first + passed:
            cp.wait_send()
        mine.wait()

    return start, finish


def _gather_scratch(n):
    return [pltpu.SemaphoreType.DMA((GATHER_COPIES * n,)), pltpu.SemaphoreType.DMA((GATHER_COPIES * n,)),
            pltpu.SemaphoreType.DMA((n,))]


def _gathered_shape(shard):
    n_l, r, c_w = shard.shape
    return SDS((n_l, N_CHIPS, r, c_w), shard.dtype)


def _all_gather(shard):
    def body(s_ref, o_ref, send_sems, recv_sems, local_sems):
        start, finish = _gather_steps(s_ref, o_ref, send_sems, recv_sems, local_sems, 0)
        start()
        finish()

    return _call(body, name="all_gather", in_specs=[HBM_SPEC], out_specs=HBM_SPEC, out_shape=_gathered_shape(shard),
                 scratch_shapes=_gather_scratch(1))(shard)


class _Exchange(NamedTuple):
    tag: str
    inputs: list
    out_shapes: list
    scratch: list
    make_steps: Callable


def _gather_exchange(shards):
    n = len(shards)

    def make_steps(s_refs, o_refs, sems):
        steps = [_gather_steps(s_refs[k], o_refs[k], *sems, k) for k in range(n)]
        return (lambda: [start() for start, _ in steps]), (lambda: [finish() for _, finish in steps])

    return _Exchange("gathering", list(shards), [_gathered_shape(s) for s in shards], _gather_scratch(n), make_steps)


def _call_hosting(body, exchange, args, *, name, grid, in_specs, out_specs, out_shape, scratch_shapes=()):
    out_specs = list(out_specs) if isinstance(out_specs, (list, tuple)) else [out_specs]
    out_shape = list(out_shape) if isinstance(out_shape, (list, tuple)) else [out_shape]
    n_in, n_out, n_scr = len(in_specs), len(out_specs), len(scratch_shapes)
    n_xi, n_xo, n_sem = len(exchange.inputs), len(exchange.out_shapes), len(exchange.scratch)

    def hosting_body(*refs):
        ins, x_ins = refs[:n_in], refs[n_in:n_in + n_xi]
        outs = refs[n_in + n_xi:n_in + n_xi + n_out]
        x_outs = refs[n_in + n_xi + n_out:n_in + n_xi + n_out + n_xo]
        scratch = refs[n_in + n_xi + n_out + n_xo:n_in + n_xi + n_out + n_xo + n_scr]
        start, finish = exchange.make_steps(x_ins, x_outs, refs[len(refs) - n_sem:])
        is_first = functools.reduce(jnp.logical_and, [pl.program_id(a) == 0 for a in range(len(grid))])
        is_last = functools.reduce(jnp.logical_and, [pl.program_id(a) == grid[a] - 1 for a in range(len(grid))])

        @pl.when(is_first)
        def _():
            start()

        body(*ins, *outs, *scratch)

        @pl.when(is_last)
        def _():
            finish()

    res = _call(
        hosting_body, name=name + "_" + exchange.tag, grid=grid, in_specs=list(in_specs) + [HBM_SPEC] * n_xi,
        out_specs=out_specs + [HBM_SPEC] * n_xo, out_shape=out_shape + list(exchange.out_shapes),
        scratch_shapes=list(scratch_shapes) + list(exchange.scratch),
        compiler_params=_params(*(["arbitrary"] * len(grid))))(*args, *exchange.inputs)
    return res[:n_out], res[n_out:]


def _call_gathering(body, shards, args, **kw):
    return _call_hosting(body, _gather_exchange(shards), args, **kw)


def _row_tile(h):
    assert h <= 512
    return h


def _rs_pair(gs):
    n = len(gs)

    def body(*refs):
        g_refs, a_refs, (send_sems, recv_sems) = refs[:n], refs[n:2 * n], refs[2 * n:]
        x, y, c = _mesh_pos()
        cps = []
        for k in range(n):
            h = g_refs[k].shape[2] // 2
            cps.append(pltpu.make_async_remote_copy(
                src_ref=g_refs[k].at[:, :, pl.ds((1 - c) * h, h), :], dst_ref=a_refs[k], send_sem=send_sems.at[k],
                recv_sem=recv_sems.at[k], device_id=(x, y, 1 - c), device_id_type=MESH))
        for cp in cps:
            cp.start()
        for cp in cps:
            cp.wait()

    out_shape = [SDS((g.shape[0], g.shape[1], g.shape[2] // 2, g.shape[3]), g.dtype) for g in gs]
    return _call(body, name="rs_pair", in_specs=[HBM_SPEC] * n, out_specs=[HBM_SPEC] * n, out_shape=out_shape,
                 scratch_shapes=[pltpu.SemaphoreType.DMA((n,)), pltpu.SemaphoreType.DMA((n,))])(*gs)


def _rs_pair_add(g, from_sibling, mid_dtype):
    n_l, n_p, r, c_w = g.shape
    h = r // 2
    tr = _row_tile(h)
    nt = h // tr
    c_arr = jnp.reshape(lax.axis_index("c"), (1,)).astype(jnp.int32)

    def body(c_ref, g_ref, a_ref, o_ref):
        o_ref[...] = (g_ref[...].astype(F32) + a_ref[...].astype(F32)).astype(o_ref.dtype)

    blk = (None, None, tr, c_w)
    return _call(
        body, name="rs_pair_add",
        grid_spec=pltpu.PrefetchScalarGridSpec(
            num_scalar_prefetch=1, grid=(n_l, n_p, nt),
            in_specs=[pl.BlockSpec(blk, lambda l, p, t, c_ref: (l, p, c_ref[0] * nt + t, 0)),
                      pl.BlockSpec(blk, lambda l, p, t, c_ref: (l, p, t, 0))],
            out_specs=pl.BlockSpec(blk, lambda l, p, t, c_ref: (l, p, t, 0))),
        out_shape=SDS((n_l, n_p, h, c_w), mid_dtype),
        compiler_params=_params("parallel", "parallel", "parallel"))(c_arr, g, from_sibling)


def _chips_exchange(pair_sums):
    n = len(pair_sums)

    def make_steps(s_refs, b_refs, sems):
        send_sems, recv_sems = sems
        x, y, c = _mesh_pos()
        cps = [pltpu.make_async_remote_copy(
            src_ref=s_refs[k].at[:, 2 * chip[0] + chip[1]], dst_ref=b_refs[k].at[j], send_sem=send_sems.at[3 * k + j],
            recv_sem=recv_sems.at[3 * k + j], device_id=(*chip, c), device_id_type=MESH)
            for k in range(n) for j, chip in enumerate(_other_chips(x, y))]
        return (lambda: [cp.start() for cp in cps]), (lambda: [cp.wait() for cp in cps])

    out_shapes = [SDS((3, s.shape[0], s.shape[2], s.shape[3]), s.dtype) for s in pair_sums]
    sems = [pltpu.SemaphoreType.DMA((3 * n,)), pltpu.SemaphoreType.DMA((3 * n,))]
    return _Exchange("scattering", list(pair_sums), out_shapes, sems, make_steps)


def _rs_chips(pair_sums):
    ex = _chips_exchange(pair_sums)
    n = len(pair_sums)

    def body(*refs):
        start, finish = ex.make_steps(refs[:n], refs[n:2 * n], refs[2 * n:])
        start()
        finish()

    return _call(body, name="rs_chips", in_specs=[HBM_SPEC] * n, out_specs=[HBM_SPEC] * n, out_shape=ex.out_shapes,
                 scratch_shapes=ex.scratch)(*pair_sums)


def _rs_chip_add(pair_sum, from_chips, n_layers, first_layer, buf=None):
    n_l, _, h, c_w = pair_sum.shape
    tr = _row_tile(h)
    nt = h // tr
    p_arr = jnp.reshape(2 * lax.axis_index("x") + lax.axis_index("y"), (1,)).astype(jnp.int32)
    c_arr = jnp.reshape(lax.axis_index("c"), (1,)).astype(jnp.int32)

    def body(p_ref, c_ref, s_ref, b_ref, *rest):
        acc = s_ref[...].astype(F32)
        for j in range(3):
            acc = acc + b_ref[j].astype(F32)
        rest[-1][...] = acc

    in_specs = [pl.BlockSpec((None, None, tr, c_w), lambda l, t, p_ref, c_ref: (l, p_ref[0], t, 0)),
                pl.BlockSpec((3, None, tr, c_w), lambda l, t, p_ref, c_ref: (0, l, t, 0))]
    args = [p_arr, c_arr, pair_sum, from_chips]
    aliases = {}
    if buf is not None:
        in_specs.append(pl.BlockSpec(memory_space=pl.ANY))
        args.append(buf)
        aliases = {4: 0}
    return _call(
        body, name="rs_chip_add",
        grid_spec=pltpu.PrefetchScalarGridSpec(
            num_scalar_prefetch=2, grid=(n_l, nt), in_specs=in_specs,
            out_specs=pl.BlockSpec((None, tr, c_w), lambda l, t, p_ref, c_ref: (first_layer + l, c_ref[0] * nt + t, 0))),
        out_shape=SDS((n_layers, 2 * h, c_w), F32), input_output_aliases=aliases,
        compiler_params=_params("parallel", "parallel"))(*args)


def _rs_swap(halves):
    n = len(halves)

    def body(*refs):
        outs, (send_sems, recv_sems) = refs[n:2 * n], refs[2 * n:]
        x, y, c = _mesh_pos()

        def copy(k, half):
            h = outs[k].shape[1] // 2
            mine = outs[k].at[:, pl.ds(c * h, h), :]
            return pltpu.make_async_remote_copy(
                src_ref=mine, dst_ref=mine if half == "mine" else outs[k].at[:, pl.ds((1 - c) * h, h), :],
                send_sem=send_sems.at[k], recv_sem=recv_sems.at[k], device_id=(x, y, 1 - c), device_id_type=MESH)

        for k in range(n):
            copy(k, "mine").start()
        for k in range(n):
            copy(k, "theirs").wait_send()
            copy(k, "theirs").wait_recv()

    return _call(body, name="rs_swap", in_specs=[HBM_SPEC] * n, out_specs=[HBM_SPEC] * n,
                 out_shape=[SDS(a.shape, F32) for a in halves], input_output_aliases={k: k for k in range(n)},
                 scratch_shapes=[pltpu.SemaphoreType.DMA((n,)), pltpu.SemaphoreType.DMA((n,))])(*halves)


def _pack(arrays, row_multiple):
    flat = jnp.concatenate([a.reshape(-1).astype(F32) for a in arrays])
    unit = row_multiple * LANES
    padded = -(-flat.shape[0] // unit) * unit
    return jnp.pad(flat, (0, padded - flat.shape[0])).reshape(padded // LANES, LANES)


def _unpack(packed, shapes):
    flat = packed.reshape(-1)
    out, pos = [], 0
    for s in shapes:
        size = 1
        for dim in s:
            size *= dim
        out.append(flat[pos:pos + size].reshape(s))
        pos += size
    return out


BIG_COL = ("sb_w_in", "cv_w_pw1", "ffn_w_up")
BIG_ROW = ("hyb_w_out", "cv_w_pw2", "ffn_w_down")
SMALL_SHARDED = ("cv_b_pw1", "cv_w_dw", "cv_b_dw", "cv_ln_g", "cv_ln_b", "cv_b_pw2", "ffn_w_dw")
SMALL_REPLICATED = ("mix_norm_g", "sb_q_norm_g", "sb_k_norm_g", "sg_z_norm_g", "sg_w_spatial", "sg_b_spatial",
                    "ffn_norm_g", "ffn_b_dw")
WEIGHTS = ("mix_norm_g", "sb_w_in", "sb_q_norm_g", "sb_k_norm_g", "sg_z_norm_g", "sg_w_spatial", "sg_b_spatial",
           "hyb_w_out", "cv_w_pw1", "cv_b_pw1", "cv_w_dw", "cv_b_dw", "cv_ln_g", "cv_ln_b", "cv_w_pw2", "cv_b_pw2",
           "ffn_norm_g", "ffn_w_up", "ffn_w_dw", "ffn_b_dw", "ffn_w_down")


def _pad_rows(a, rows):
    return jnp.pad(a, ((0, rows - a.shape[0]), (0, 0)))


def _step(x, tgt, w, m, v):
    n_layers = w["mix_norm_g"].shape[0]
    xi, yi, ci = _mesh_pos()
    chip = 2 * xi + yi

    assert n_layers == 4
    hosted_by = {("proj", 0): ["hyb_w_out"], ("prep", 0): [("ffn_w_up", 0)],
                 ("attn", 0): [("ffn_w_down", 0), "cv_w_pw1", "cv_w_pw2"],
                 ("up", 0): [("ffn_w_up", 1)], ("ffn_mid", 0): [("ffn_w_down", 1)],
                 ("conf_mid", 1): [("ffn_w_up", 2), ("ffn_w_down", 2)],
                 ("up", 1): [("ffn_w_up", 3)], ("ffn_mid", 1): [("ffn_w_down", 3), ("sb_w_in", 1)]}
    full = {}

    def shard_of(key):
        if isinstance(key, tuple):
            return w[key[0]][key[1]:key[1] + 1].astype(BF)
        return w[key].astype(BF)

    def keep(key, g4):
        if (key[0] if isinstance(key, tuple) else key) in BIG_ROW:
            g4 = g4.reshape(g4.shape[0], 1, g4.shape[1] * g4.shape[2], g4.shape[3])
        full[key] = g4

    def hosting(fn, point, *args, **kw):
        keys = hosted_by.get(point)
        if not keys:
            return fn(*args, **kw)
        out, gathered = fn(*args, gather=[shard_of(k) for k in keys], **kw)
        for key, g4 in zip(keys, gathered):
            keep(key, g4)
        return out

    keep(("sb_w_in", 0), _all_gather(shard_of(("sb_w_in", 0))))
    small_local = [w[name] for name in SMALL_SHARDED]
    gathered = _all_gather(_pack(small_local, 32)[None])[0]
    per_chip = [_unpack(gathered[p], [a.shape for a in small_local]) for p in range(N_CHIPS)]
    for k, name in enumerate(SMALL_SHARDED):
        full[name] = jnp.concatenate([per_chip[p][k] for p in range(N_CHIPS)], axis=-1)
    for name in SMALL_REPLICATED:
        full[name] = w[name]

    mean64, fold64 = _group_matrices()
    ffn_wdw = [_pad_rows(full["ffn_w_dw"][i], 8) for i in range(n_layers)]
    cv_wdw = [_pad_rows(full["cv_w_dw"][j], 32) for j in range(n_layers // 2)]
    row = lambda a: a.reshape(1, -1)

    saved = []
    cur = x
    h = _rms_fwd(cur, row(full["mix_norm_g"][0]))
    for i in range(n_layers):
        j = i // 2
        rec = {"x_in": cur, "h_mix": h}
        if i % 2 == 0:
            proj = hosting(_mm_nn, ("proj", i), h, full[("sb_w_in", j)], 0)
            qg = row(jnp.tile(full["sb_q_norm_g"][j], 512 // HEAD_DIM))
            kg = row(jnp.tile(full["sb_k_norm_g"][j], 512 // HEAD_DIM))
            zg = row(full["sg_z_norm_g"][j])
            bexp = jnp.repeat(full["sg_b_spatial"][j].T, HEAD_DIM, axis=1)
            qkv, gated = hosting(_mix_prep_fwd, ("prep", i), proj, qg, kg, zg, full["sg_w_spatial"], j, bexp, mean64)
            att_bf, att_32 = hosting(_attn_fwd, ("attn", i), qkv)
            mix = jnp.concatenate([att_bf, gated], axis=1)
            cur, h = _mm_nn(mix, full["hyb_w_out"], j, resid=cur, norm_g=row(full["ffn_norm_g"][i]))
            rec.update(proj=proj, qkv=qkv, att_32=att_32, mix=mix, qg=qg, kg=kg, zg=zg, bexp=bexp)
        else:
            p1 = _mm_nn(h, full["cv_w_pw1"], j, bias=row(full["cv_b_pw1"][j]), out_dtype=BF)
            ys, yc = hosting(_conf_mid_fwd, ("conf_mid", i), p1, cv_wdw[j], row(full["cv_b_dw"][j]),
                             row(full["cv_ln_g"][j]), row(full["cv_ln_b"][j]))
            cur, h = _mm_nn(ys, full["cv_w_pw2"], j, bias=row(full["cv_b_pw2"][j]), resid=cur,
                            norm_g=row(full["ffn_norm_g"][i]))
            rec.update(p1=p1, ys=ys, yc=yc)
        rec["x_mid"] = cur
        up = hosting(_mm_nn, ("up", i), h, full[("ffn_w_up", i)], 0, out_dtype=BF)
        act = hosting(_ffn_mid_fwd, ("ffn_mid", i), up, ffn_wdw[i], row(full["ffn_b_dw"][i]))
        rec.update(h_ffn=h, up=up, act=act)
        if i + 1 < n_layers:
            cur, h = _mm_nn(act, full[("ffn_w_down", i)], 0, resid=cur, norm_g=row(full["mix_norm_g"][i + 1]))
        else:
            cur = _mm_nn(act, full[("ffn_w_down", i)], 0, resid=cur)
        saved.append(rec)

    loss_vec, dy, dy_bf = _loss_grad(cur, tgt)
    loss = lax.psum(loss_vec[0, 0], ("x", "y", "c"))

    big_names = BIG_COL + BIG_ROW
    gbig = {}
    gsmall = {name: [None] * w[name].shape[0] for name in SMALL_SHARDED + SMALL_REPLICATED}

    def accumulate(name, layer, a, dy_, p_n):
        per_group = w[name].shape[0] // 2
        grp, slot = divmod(layer, per_group)
        gbig[(name, grp)] = _mm_tn(a, dy_, p_n, per_group, slot, gbig.get((name, grp)))

    def group_grads(grp):
        out = []
        for name in big_names:
            g4 = gbig[(name, grp)]
            if name in BIG_ROW:
                g4 = g4.reshape(g4.shape[0], N_CHIPS, g4.shape[2] // N_CHIPS, g4.shape[3])
            out.append(g4)
        return out

    def pair_sums_of(gs, mid_dtypes):
        return [_rs_pair_add(g, a, dt) for g, a, dt in zip(gs, _rs_pair(gs), mid_dtypes)]

    half_sums = {}
    late_pair_sums = None
    for i in reversed(range(n_layers)):
        j = i // 2
        rec = saved[i]
        dact = _mm_nt(dy_bf, full[("ffn_w_down", i)], 0, out_dtype=BF)
        accumulate("ffn_w_down", i, rec["act"], dy_bf, 1)
        dup, dwdw, dbdw = _ffn_mid_bwd(rec["up"], dact, ffn_wdw[i], row(full["ffn_b_dw"][i]))
        gsmall["ffn_w_dw"][i] = dwdw[:FFN_K]
        gsmall["ffn_b_dw"][i] = dbdw[0]
        accumulate("ffn_w_up", i, rec["h_ffn"], dup, N_CHIPS)
        dy, dy_bf, dg = _mm_nt_rms_bwd(dup, full[("ffn_w_up", i)], 0, rec["x_mid"], row(full["ffn_norm_g"][i]), dy)
        gsmall["ffn_norm_g"][i] = dg[0]
        if i % 2 == 0:
            dmix = _mm_nt(dy_bf, full["hyb_w_out"], j)
            accumulate("hyb_w_out", j, rec["mix"], dy_bf, 1)
            dq, dk, dv = _attn_bwd(rec["qkv"], rec["att_32"], dmix)
            dproj, dqg, dkg, dzg, dws, dbe = _mix_prep_bwd(
                rec["proj"], dq, dk, dv, dmix, rec["qg"], rec["kg"], rec["zg"], full["sg_w_spatial"], j, rec["bexp"],
                mean64, fold64)
            gsmall["sb_q_norm_g"][j] = dqg[0, :HEAD_DIM]
            gsmall["sb_k_norm_g"][j] = dkg[0, :HEAD_DIM]
            gsmall["sg_z_norm_g"][j] = dzg[0]
            gsmall["sg_w_spatial"][j] = dws
            gsmall["sg_b_spatial"][j] = dbe[:, ::HEAD_DIM].T
            dlast, w_first, l_first = dproj, full[("sb_w_in", j)], 0
            accumulate("sb_w_in", j, rec["h_mix"], dproj, N_CHIPS)
        else:
            dys = _mm_nt(dy_bf, full["cv_w_pw2"], j, out_dtype=BF)
            accumulate("cv_w_pw2", j, rec["ys"], dy_bf, 1)
            carried = _chips_exchange(late_pair_sums) if late_pair_sums is not None else None
            res = _conf_mid_bwd(rec["p1"], rec["yc"], dys, dy, cv_wdw[j], row(full["cv_ln_g"][j]),
                                row(full["cv_ln_b"][j]), exchange=carried)
            if carried is not None:
                res, from_chips = res
                for name, ps, fc in zip(big_names, late_pair_sums, from_chips):
                    n_all = w[name].shape[0]
                    half_sums[name] = _rs_chip_add(ps, fc, n_all, n_all // 2)
                late_pair_sums = None
            dp1, dwdw, dbdw, dlg, dlb, db1, db2 = res
            gsmall["cv_w_dw"][j] = dwdw[:CONV_K]
            gsmall["cv_b_dw"][j] = dbdw[0]
            gsmall["cv_ln_g"][j] = dlg[0]
            gsmall["cv_ln_b"][j] = dlb[0]
            gsmall["cv_b_pw1"][j] = db1[0]
            gsmall["cv_b_pw2"][j] = db2[0]
            dlast, w_first, l_first = dp1, full["cv_w_pw1"], j
            accumulate("cv_w_pw1", j, rec["h_mix"], dp1, N_CHIPS)
        dy, dy_bf, dg = _mm_nt_rms_bwd(dlast, w_first, l_first, rec["x_in"], row(full["mix_norm_g"][i]), dy)
        gsmall["mix_norm_g"][i] = dg[0]
        if i == n_layers // 2:
            late_pair_sums = pair_sums_of(group_grads(1), [BF] * len(big_names))

    small_names = SMALL_REPLICATED + SMALL_SHARDED
    small_full = [jnp.stack(gsmall[name]) for name in small_names]
    packed = _pack(small_full, 32 * N_CHIPS)
    rows_q = packed.shape[0] // N_CHIPS
    early = pair_sums_of(group_grads(0) + [packed.reshape(1, N_CHIPS, rows_q, LANES)], [BF] * len(big_names) + [F32])
    from_chips = _rs_chips(early)
    halves = [_rs_chip_add(ps, fc, w[name].shape[0], 0, half_sums[name])
              for name, ps, fc in zip(big_names, early, from_chips)]
    halves.append(_rs_chip_add(early[-1], from_chips[-1], 1, 0))
    swapped = _rs_swap(halves)
    grads = dict(zip(big_names, swapped))
    summed = _all_gather(swapped[-1]).reshape(-1, LANES)
    for name, gsum in zip(small_names, _unpack(summed, [a.shape for a in small_full])):
        if name in SMALL_SHARDED:
            n_loc = w[name].shape[-1]
            split = gsum.reshape(gsum.shape[:-1] + (N_CHIPS, n_loc))
            gsum = lax.dynamic_index_in_dim(split, chip, axis=split.ndim - 2, keepdims=False)
        grads[name] = gsum

    delta, new_m, new_v = {}, {}, {}
    for name in BIG_COL + BIG_ROW:
        shp = w[name].shape
        two_d = lambda a: a.reshape(shp[0] * shp[1], shp[2])
        d, nm, nv = _adamw(two_d(w[name]), two_d(grads[name]), two_d(m[name]), two_d(v[name]))
        delta[name], new_m[name], new_v[name] = d.reshape(shp), nm.reshape(shp), nv.reshape(shp)
    shapes = [w[name].shape for name in small_names]
    d, nm, nv = _adamw(*(_pack([src[name] for name in small_names], 256) for src in (w, grads, m, v)))
    for name, a, b_, c_ in zip(small_names, _unpack(d, shapes), _unpack(nm, shapes), _unpack(nv, shapes)):
        delta[name], new_m[name], new_v[name] = a, b_, c_

    return (loss, dy, *[grads[n] for n in WEIGHTS], *[delta[n] for n in WEIGHTS],
            *[new_m[n] for n in WEIGHTS], *[new_v[n] for n in WEIGHTS])


def kernel(x, mix_norm_g, sb_w_in, sb_q_norm_g, sb_k_norm_g, sg_z_norm_g, sg_w_spatial, sg_b_spatial, hyb_w_out, cv_w_pw1, cv_b_pw1, cv_w_dw, cv_b_dw, cv_ln_g, cv_ln_b, cv_w_pw2, cv_b_pw2, ffn_norm_g, ffn_w_up, ffn_w_dw, ffn_b_dw, ffn_w_down, loss_target, m_mix_norm_g, m_sb_w_in, m_sb_q_norm_g, m_sb_k_norm_g, m_sg_z_norm_g, m_sg_w_spatial, m_sg_b_spatial, m_hyb_w_out, m_cv_w_pw1, m_cv_b_pw1, m_cv_w_dw, m_cv_b_dw, m_cv_ln_g, m_cv_ln_b, m_cv_w_pw2, m_cv_b_pw2, m_ffn_norm_g, m_ffn_w_up, m_ffn_w_dw, m_ffn_b_dw, m_ffn_w_down, v_mix_norm_g, v_sb_w_in, v_sb_q_norm_g, v_sb_k_norm_g, v_sg_z_norm_g, v_sg_w_spatial, v_sg_b_spatial, v_hyb_w_out, v_cv_w_pw1, v_cv_b_pw1, v_cv_w_dw, v_cv_b_dw, v_cv_ln_g, v_cv_ln_b, v_cv_w_pw2, v_cv_b_pw2, v_ffn_norm_g, v_ffn_w_up, v_ffn_w_dw, v_ffn_b_dw, v_ffn_w_down):
    given = dict(locals())
    w = {n: given[n] for n in WEIGHTS}
    m = {n: given["m_" + n] for n in WEIGHTS}
    v = {n: given["v_" + n] for n in WEIGHTS}
    out = _step(x[0], loss_target[0], w, m, v)
    return (out[0], out[1][None], *out[2:])
```

```python
import functools
from typing import Callable, NamedTuple

import jax
import jax.numpy as jnp
from jax import lax
from jax.experimental import pallas as pl
from jax.experimental.pallas import tpu as pltpu

F32 = jnp.float32
BF = jnp.bfloat16
SDS = jax.ShapeDtypeStruct
HI = lax.Precision.HIGHEST
MESH = pl.DeviceIdType.MESH

NORM_EPS = 1e-6
HEAD_DIM = 64
ATT_BLOCK = 128
CHUNK = 128
PREP_CHUNKS = 2
CONV_K = 31
CONV_HALO = 32
FFN_K = 3
FFN_HALO = 16
LANES = 128
N_CHIPS = 4
VMEM_LIMIT_BYTES = 56 * 2**20

ADAM_LR = 0.001
ADAM_B1 = 0.9
ADAM_B2 = 0.999
ADAM_EPS = 1e-08
ADAM_WD = 0.01
ADAM_STEP = 10

NT_DIMS = (((1,), (1,)), ((), ()))
TN_DIMS = (((0,), (0,)), ((), ()))


def _call(body, **kw):
    return pl.pallas_call(body, **kw)


def _params(*sem):
    return pltpu.CompilerParams(dimension_semantics=sem, vmem_limit_bytes=VMEM_LIMIT_BYTES)


def _gelu(x):
    return 0.5 * x * (1.0 + lax.erf(x * 0.7071067811865476))


def _rms(x, g):
    y = x * lax.rsqrt(jnp.mean(x * x, axis=-1, keepdims=True) + NORM_EPS)
    return y * g


def _rms_fwd(x, g):
    t, d = x.shape
    tm = min(512, t)

    def body(x_ref, g_ref, o_ref):
        o_ref[...] = _rms(x_ref[...], g_ref[...]).astype(o_ref.dtype)

    return _call(
        body, name="rms_fwd", grid=(t // tm,),
        in_specs=[pl.BlockSpec((tm, d), lambda i: (i, 0)), pl.BlockSpec((1, d), lambda i: (0, 0))],
        out_specs=pl.BlockSpec((tm, d), lambda i: (i, 0)),
        out_shape=SDS((t, d), BF), compiler_params=_params("parallel"))(x, g)


def _mm_nn(a, w, l, bias=None, resid=None, out_dtype=F32, gather=None, norm_g=None):
    m, k = a.shape
    _, p_n, kw, n = w.shape
    assert k == kw
    normed = norm_g is not None
    assert not normed or (p_n == 1 and not gather)
    tm = min(512 if normed else 1024, m)
    tn = n if (normed or k * n * 2 <= 4 * 2**20) else n // 2
    nj = n // tn
    in_specs = [pl.BlockSpec((tm, k), lambda i, p, j: (i, 0)),
                pl.BlockSpec((None, None, k, tn), lambda i, p, j: (l, p, 0, j))]
    args = [a, w]
    if bias is not None:
        in_specs.append(pl.BlockSpec((1, tn), lambda i, p, j: (0, p * nj + j)))
        args.append(bias)
    if resid is not None:
        in_specs.append(pl.BlockSpec((tm, tn), lambda i, p, j: (i, p * nj + j)))
        args.append(resid)
    if normed:
        in_specs.append(pl.BlockSpec((1, n), lambda i, p, j: (0, 0)))
        args.append(norm_g)
    n_in = len(args)

    def body(*refs):
        acc = jnp.dot(refs[0][...], refs[1][...], preferred_element_type=F32)
        nxt = 2
        if bias is not None:
            acc = acc + refs[nxt][...]
            nxt += 1
        if resid is not None:
            acc = refs[nxt][...] + acc
        refs[n_in][...] = acc.astype(refs[n_in].dtype)
        if normed:
            refs[n_in + 1][...] = _rms(acc, refs[n_in - 1][...]).astype(BF)

    out_spec = pl.BlockSpec((tm, tn), lambda i, p, j: (i, p * nj + j))
    kw = dict(name="mm_nn", grid=(m // tm, p_n, nj), in_specs=in_specs,
              out_specs=[out_spec, out_spec] if normed else out_spec,
              out_shape=[SDS((m, n), out_dtype), SDS((m, n), BF)] if normed else SDS((m, p_n * n), out_dtype))
    if gather:
        (out,), gathered = _call_gathering(body, gather, args, **kw)
        return out, gathered
    return _call(body, compiler_params=_params("parallel", "parallel", "parallel"), **kw)(*args)


def _mm_nt(dy, w, l, out_dtype=F32):
    m, n_all = dy.shape
    _, p_n, r, n = w.shape
    assert n_all == p_n * n
    tm = min(512, m)

    def body(dy_ref, w_ref, o_ref):
        acc = lax.dot_general(dy_ref[:, 0:n], w_ref[0], NT_DIMS, preferred_element_type=F32)
        for p in range(1, p_n):
            acc = acc + lax.dot_general(dy_ref[:, p * n:(p + 1) * n], w_ref[p], NT_DIMS, preferred_element_type=F32)
        o_ref[...] = acc.astype(o_ref.dtype)

    return _call(
        body, name="mm_nt", grid=(m // tm,),
        in_specs=[pl.BlockSpec((tm, n_all), lambda i: (i, 0)),
                  pl.BlockSpec((None, p_n, r, n), lambda i: (l, 0, 0, 0))],
        out_specs=pl.BlockSpec((tm, r), lambda i: (i, 0)),
        out_shape=SDS((m, r), out_dtype),
        compiler_params=_params("parallel"))(dy, w)


def _mm_nt_rms_bwd(dy, w, l, x, g, dres):
    m, n_all = dy.shape
    _, p_n, r, n = w.shape
    assert n_all == p_n * n and x.shape == (m, r)
    tm = min(256, m)

    def body(dy_ref, w_ref, x_ref, g_ref, r_ref, dx_ref, dxb_ref, dg_ref):
        dh = lax.dot_general(dy_ref[:, 0:n], w_ref[0], NT_DIMS, preferred_element_type=F32)
        for p in range(1, p_n):
            dh = dh + lax.dot_general(dy_ref[:, p * n:(p + 1) * n], w_ref[p], NT_DIMS, preferred_element_type=F32)
        _, vjp = jax.vjp(_rms, x_ref[...], g_ref[...])
        dx, dg = vjp(dh)
        dx = dx + r_ref[...]
        dx_ref[...] = dx
        dxb_ref[...] = dx.astype(BF)

        @pl.when(pl.program_id(0) == 0)
        def _():
            dg_ref[...] = jnp.zeros_like(dg_ref)

        dg_ref[...] += dg

    row = pl.BlockSpec((tm, r), lambda i: (i, 0))
    vec = pl.BlockSpec((1, r), lambda i: (0, 0))
    return _call(
        body, name="mm_nt_rms_bwd", grid=(m // tm,),
        in_specs=[pl.BlockSpec((tm, n_all), lambda i: (i, 0)), pl.BlockSpec((None, p_n, r, n), lambda i: (l, 0, 0, 0)),
                  row, vec, row],
        out_specs=[row, row, vec], out_shape=[SDS((m, r), F32), SDS((m, r), BF), SDS((1, r), F32)],
        compiler_params=_params("arbitrary"))(dy, w, x, g, dres)


def _mm_tn(a, dy, p_n, n_layers, l, buf=None):
    m, k = a.shape
    n = dy.shape[1] // p_n
    tm = min(2048, m)
    tk = k if k <= 1024 else k // 2
    nm = m // tm

    def body(a_ref, dy_ref, *rest):
        o_ref, acc_ref = rest[-2], rest[-1]
        mi = pl.program_id(2)
        part = lax.dot_general(a_ref[...], dy_ref[...], TN_DIMS, preferred_element_type=F32)

        @pl.when(mi == 0)
        def _():
            acc_ref[...] = part

        @pl.when(mi > 0)
        def _():
            acc_ref[...] += part

        @pl.when(mi == nm - 1)
        def _():
            o_ref[...] = acc_ref[...].astype(o_ref.dtype)

    in_specs = [pl.BlockSpec((tm, tk), lambda p, kk, mi: (mi, kk)),
                pl.BlockSpec((tm, n), lambda p, kk, mi: (mi, p))]
    args = [a, dy]
    aliases = {}
    if buf is not None:
        in_specs.append(pl.BlockSpec(memory_space=pl.ANY))
        args.append(buf)
        aliases = {2: 0}
    return _call(
        body, name="mm_tn", grid=(p_n, k // tk, nm), in_specs=in_specs,
        out_specs=pl.BlockSpec((None, None, tk, n), lambda p, kk, mi: (l, p, kk, 0)),
        out_shape=SDS((n_layers, p_n, k, n), BF), scratch_shapes=[pltpu.VMEM((tk, n), F32)],
        input_output_aliases=aliases,
        compiler_params=_params("parallel", "parallel", "arbitrary"))(*args)


def _loss_grad(y, tgt):
    t, d = y.shape
    tm = min(512, t)

    def body(y_ref, t_ref, l_ref, d_ref, db_ref):
        err = y_ref[...] - t_ref[...]
        dy = err * (1.0 / d)
        d_ref[...] = dy
        db_ref[...] = dy.astype(BF)
        part = 0.5 * jnp.sum(jnp.sum(err * err, axis=1, keepdims=True) * (1.0 / d), axis=0, keepdims=True)

        @pl.when(pl.program_id(0) == 0)
        def _():
            l_ref[...] = jnp.zeros_like(l_ref)

        l_ref[...] += jnp.broadcast_to(part, l_ref.shape)

    row = pl.BlockSpec((tm, d), lambda i: (i, 0))
    return _call(
        body, name="loss_grad", grid=(t // tm,), in_specs=[row, row],
        out_specs=[pl.BlockSpec((1, LANES), lambda i: (0, 0)), row, row],
        out_shape=[SDS((1, LANES), F32), SDS((t, d), F32), SDS((t, d), BF)],
        compiler_params=_params("arbitrary"))(y, tgt)


def _prev_halo(tr, halo, col):
    return lambda i: (jnp.maximum(i * (tr // halo) - 1, 0), col)


def _next_halo(tr, halo, n_rows, col):
    return lambda i: (jnp.minimum((i + 1) * (tr // halo), n_rows // halo - 1), col)


def _shifted_back(x):
    return pltpu.roll(x, 1, 0), pltpu.roll(x, 2, 0)


def _conv3(x, w_ref, b_ref, col):
    x1, x2 = _shifted_back(x)
    return b_ref[:, col] + w_ref[pl.ds(0, 1), col] * x2 + w_ref[pl.ds(1, 1), col] * x1 + w_ref[pl.ds(2, 1), col] * x


def _ffn_mid_fwd(up, w_dw, b_dw, gather=None):
    t, f2 = up.shape
    f = f2 // 2
    tr = min(256, t)
    h = FFN_HALO

    def body(g_ref, gp_ref, v_ref, w_ref, b_ref, o_ref):
        first_tile = pl.program_id(0) == 0

        def strip(c, carry):
            col = pl.ds(pl.multiple_of(c * LANES, LANES), LANES)
            x = jnp.concatenate([jnp.where(first_tile, 0.0, gp_ref[:, col].astype(F32)), g_ref[:, col].astype(F32)], axis=0)
            gc = _conv3(x, w_ref, b_ref, col)[h:]
            o_ref[:, col] = (gc * jax.nn.sigmoid(gc) * v_ref[:, col].astype(F32)).astype(o_ref.dtype)
            return carry

        lax.fori_loop(0, f // LANES, strip, 0)

    kw = dict(name="ffn_mid_fwd", grid=(t // tr,),
              in_specs=[pl.BlockSpec((tr, f), lambda i: (i, 0)), pl.BlockSpec((h, f), _prev_halo(tr, h, 0)),
                        pl.BlockSpec((tr, f), lambda i: (i, 1)),
                        pl.BlockSpec((8, f), lambda i: (0, 0)), pl.BlockSpec((1, f), lambda i: (0, 0))],
              out_specs=pl.BlockSpec((tr, f), lambda i: (i, 0)), out_shape=SDS((t, f), BF))
    args = (up, up, up, w_dw, b_dw)
    if gather:
        (out,), gathered = _call_gathering(body, gather, args, **kw)
        return out, gathered
    return _call(body, compiler_params=_params("parallel"), **kw)(*args)


def _ffn_mid_bwd(up, da, w_dw, b_dw):
    t, f2 = up.shape
    f = f2 // 2
    tr = min(256, t)
    h = FFN_HALO
    n_tiles = t // tr

    def body(g_ref, gp_ref, gn_ref, v_ref, vn_ref, da_ref, dan_ref, w_ref, b_ref, dup_ref, dw_ref, db_ref):
        i = pl.program_id(0)
        last = i == n_tiles - 1
        n = tr + h

        @pl.when(i == 0)
        def _():
            dw_ref[...] = jnp.zeros_like(dw_ref)
            db_ref[...] = jnp.zeros_like(db_ref)

        def rows(tile_ref, next_ref, col):
            return jnp.concatenate([tile_ref[:, col].astype(F32), next_ref[:, col].astype(F32)], axis=0)

        def strip(c, carry):
            col = pl.ds(pl.multiple_of(c * LANES, LANES), LANES)
            x = jnp.concatenate([jnp.where(i == 0, 0.0, gp_ref[:, col].astype(F32)), rows(g_ref, gn_ref, col)], axis=0)
            x1, x2 = _shifted_back(x)
            w0, w1, w2 = (w_ref[pl.ds(k, 1), col] for k in range(FFN_K))
            gc = (b_ref[:, col] + w0 * x2 + w1 * x1 + w2 * x)[h:]
            dav = rows(da_ref, dan_ref, col)
            sg = jax.nn.sigmoid(gc)
            silu = gc * sg
            dup_ref[:, pl.ds(pl.multiple_of(f + c * LANES, LANES), LANES)] = (dav * silu)[:tr].astype(dup_ref.dtype)
            dgc = dav * rows(v_ref, vn_ref, col) * (sg + silu * (1.0 - sg))
            dgc = jnp.concatenate([dgc[:tr], jnp.where(last, 0.0, dgc[tr:])], axis=0)
            d1, d2 = pltpu.roll(dgc, n - 1, 0), pltpu.roll(dgc, n - 2, 0)
            dup_ref[:, col] = (w2 * dgc + w1 * d1 + w0 * d2)[:tr].astype(dup_ref.dtype)
            dgt = dgc[:tr]
            for k, past in enumerate((x2, x1, x)):
                dw_ref[pl.ds(k, 1), col] += jnp.sum(past[h:h + tr] * dgt, axis=0, keepdims=True)
            db_ref[:, col] += jnp.sum(dgt, axis=0, keepdims=True)
            return carry

        lax.fori_loop(0, f // LANES, strip, 0)

    tile = lambda col: pl.BlockSpec((tr, f), lambda i: (i, col))
    nxt = lambda col: pl.BlockSpec((h, f), _next_halo(tr, h, t, col))
    return _call(
        body, name="ffn_mid_bwd", grid=(n_tiles,),
        in_specs=[tile(0), pl.BlockSpec((h, f), _prev_halo(tr, h, 0)), nxt(0), tile(1), nxt(1), tile(0), nxt(0),
                  pl.BlockSpec((8, f), lambda i: (0, 0)), pl.BlockSpec((1, f), lambda i: (0, 0))],
        out_specs=[pl.BlockSpec((tr, f2), lambda i: (i, 0)), pl.BlockSpec((8, f), lambda i: (0, 0)),
                   pl.BlockSpec((1, f), lambda i: (0, 0))],
        out_shape=[SDS((t, f2), BF), SDS((8, f), F32), SDS((1, f), F32)],
        compiler_params=_params("arbitrary"))(up, up, up, up, up, da, da, w_dw, b_dw)


def _ln_silu(yc, g, b):
    mu = jnp.mean(yc, axis=-1, keepdims=True)
    xc = yc - mu
    y = xc * lax.rsqrt(jnp.mean(xc * xc, axis=-1, keepdims=True) + NORM_EPS)
    return jax.nn.silu(y * g + b)


SUBLANES = 8
CONV_PAD = 24
SHIFT_CHUNK = 40
TAP_ROWS = 64


def _glu(a, g):
    return a.astype(F32) * jax.nn.sigmoid(g.astype(F32))


def _glu_strip(ygs_ref, first_tile, a_ref, ap_ref, g_ref, gp_ref, col, h, tr):
    ygs_ref[pl.ds(0, h), :] = jnp.where(first_tile, 0.0, _glu(ap_ref[:, col], gp_ref[:, col]))
    ygs_ref[pl.ds(h, tr), :] = _glu(a_ref[:, col], g_ref[:, col])


def _shift_past(sh_ref, ygs_ref, h, n):
    first = h - CONV_PAD - SUBLANES
    for u0 in range(0, n + CONV_PAD, SHIFT_CHUNK):
        x = ygs_ref[pl.ds(first + u0, SHIFT_CHUNK + SUBLANES), :]
        for r in range(1, SUBLANES):
            sh_ref[r, pl.ds(u0, SHIFT_CHUNK), :] = pltpu.roll(x, r, 0)[SUBLANES:]


def _past_rows(sh_ref, ygs_ref, h, n, s, row0=0):
    a, r = divmod(s, SUBLANES)
    if r == 0:
        return ygs_ref[pl.ds(row0 + h - SUBLANES * a, n), :]
    return sh_ref[r, pl.ds(row0 + CONV_PAD - SUBLANES * a, n), :]


def _conf_mid_fwd(p1, w_dw, b_dw, ln_g, ln_b, gather=None):
    t, w2 = p1.shape
    w = w2 // 2
    tr = min(256, t)
    h = CONV_HALO
    rc = 32

    def body(a_ref, ap_ref, g_ref, gp_ref, w_ref, b_ref, lg_ref, lb_ref, o_ref, yc_ref, ygs_ref, sh_ref):
        first_tile = pl.program_id(0) == 0

        def strip(c, carry):
            col = pl.ds(pl.multiple_of(c * LANES, LANES), LANES)
            _glu_strip(ygs_ref, first_tile, a_ref, ap_ref, g_ref, gp_ref, col, h, tr)
            _shift_past(sh_ref, ygs_ref, h, tr)
            acc = jnp.broadcast_to(b_ref[:, col], (tr, LANES))
            for k in range(CONV_K):
                acc = acc + w_ref[pl.ds(k, 1), col] * _past_rows(sh_ref, ygs_ref, h, tr, CONV_K - 1 - k)
            yc_ref[:, col] = acc
            return carry

        lax.fori_loop(0, w // LANES, strip, 0)

        def rows(r, carry):
            rs = pl.ds(pl.multiple_of(r * rc, rc), rc)
            o_ref[rs, :] = _ln_silu(yc_ref[rs, :], lg_ref[...], lb_ref[...]).astype(o_ref.dtype)
            return carry

        lax.fori_loop(0, tr // rc, rows, 0, unroll=4)

    vec = pl.BlockSpec((1, w), lambda i: (0, 0))
    tile = pl.BlockSpec((tr, w), lambda i: (i, 0))
    kw = dict(name="conf_mid_fwd", grid=(t // tr,),
              in_specs=[tile, pl.BlockSpec((h, w), _prev_halo(tr, h, 0)),
                        pl.BlockSpec((tr, w), lambda i: (i, 1)), pl.BlockSpec((h, w), _prev_halo(tr, h, 1)),
                        pl.BlockSpec((32, w), lambda i: (0, 0)), vec, vec, vec],
              out_specs=[tile, tile], out_shape=[SDS((t, w), BF), SDS((t, w), F32)],
              scratch_shapes=[pltpu.VMEM((h + tr, LANES), F32), pltpu.VMEM((SUBLANES, tr + CONV_PAD, LANES), F32)])
    args = (p1, p1, p1, p1, w_dw, b_dw, ln_g, ln_b)
    if gather:
        return _call_gathering(body, gather, args, **kw)
    return _call(body, compiler_params=_params("parallel"), **kw)(*args)


def _conf_mid_bwd(p1, yc, dys, dy, w_dw, ln_g, ln_b, exchange=None):
    t, w2 = p1.shape
    w = w2 // 2
    tr = min(256, t)
    h = CONV_HALO
    rc = 32
    n_tiles = t // tr

    def body(a_ref, ap_ref, g_ref, gp_ref, yc_ref, ycn_ref, dys_ref, dysn_ref, dy_ref, w_ref, lg_ref, lb_ref,
             dp_ref, dw_ref, db_ref, dlg_ref, dlb_ref, db1_ref, db2_ref, dyc_ref, ygs_ref, sh_ref, shf_ref, dwacc_ref):
        i = pl.program_id(0)
        last = i == n_tiles - 1

        @pl.when(i == 0)
        def _():
            for ref in (dw_ref, db_ref, dlg_ref, dlb_ref, db1_ref, db2_ref):
                ref[...] = jnp.zeros_like(ref)

        def ln_rows(r, carry):
            rs = pl.ds(pl.multiple_of(r * rc, rc), rc)
            _, vjp = jax.vjp(_ln_silu, yc_ref[rs, :], lg_ref[...], lb_ref[...])
            dyc, dlg, dlb = vjp(dys_ref[rs, :].astype(F32))
            dyc_ref[rs, :] = dyc
            dlg_ref[...] += dlg
            dlb_ref[...] += dlb
            return carry

        lax.fori_loop(0, tr // rc, ln_rows, 0, unroll=4)
        _, vjp = jax.vjp(_ln_silu, ycn_ref[...], lg_ref[...], lb_ref[...])
        dyc_ref[pl.ds(tr, h), :] = jnp.where(last, 0.0, vjp(dysn_ref[...].astype(F32))[0])
        db2_ref[...] += jnp.sum(dy_ref[...], axis=0, keepdims=True)

        def back(c, carry):
            col = pl.ds(pl.multiple_of(c * LANES, LANES), LANES)
            gcol = pl.ds(pl.multiple_of(w + c * LANES, LANES), LANES)
            _glu_strip(ygs_ref, i == 0, a_ref, ap_ref, g_ref, gp_ref, col, h, tr)
            _shift_past(sh_ref, ygs_ref, h, tr)
            for u0 in range(0, tr + CONV_PAD, SHIFT_CHUNK):
                part = dyc_ref[pl.ds(u0, SHIFT_CHUNK + SUBLANES), col]
                for r in range(1, SUBLANES):
                    shf_ref[r, pl.ds(u0, SHIFT_CHUNK), :] = pltpu.roll(part, SHIFT_CHUNK + SUBLANES - r, 0)[:SHIFT_CHUNK]
            for r0 in range(0, tr, TAP_ROWS):
                rows = pl.ds(r0, TAP_ROWS)
                dyc = dyc_ref[rows, col]
                dyg = jnp.zeros((TAP_ROWS, LANES), F32)
                for k in range(CONV_K):
                    s = CONV_K - 1 - k
                    a, r = divmod(s, SUBLANES)
                    if r == 0:
                        future = dyc_ref[pl.ds(r0 + SUBLANES * a, TAP_ROWS), col]
                    else:
                        future = shf_ref[r, pl.ds(r0 + SUBLANES * a, TAP_ROWS), :]
                    dyg = dyg + w_ref[pl.ds(k, 1), col] * future
                    prod = _past_rows(sh_ref, ygs_ref, h, TAP_ROWS, s, r0) * dyc
                    part = prod[0:SUBLANES]
                    for q in range(1, TAP_ROWS // SUBLANES):
                        part = part + prod[q * SUBLANES:(q + 1) * SUBLANES]
                    if r0 == 0:
                        dwacc_ref[k] = part
                    else:
                        dwacc_ref[k] += part
                sg = jax.nn.sigmoid(g_ref[rows, col].astype(F32))
                da = dyg * sg
                dg = dyg * a_ref[rows, col].astype(F32) * sg * (1.0 - sg)
                dp_ref[rows, col] = da.astype(dp_ref.dtype)
                dp_ref[rows, gcol] = dg.astype(dp_ref.dtype)
                db_ref[:, col] += jnp.sum(dyc, axis=0, keepdims=True)
                db1_ref[:, col] += jnp.sum(da, axis=0, keepdims=True)
                db1_ref[:, gcol] += jnp.sum(dg, axis=0, keepdims=True)
            for k in range(CONV_K):
                dw_ref[pl.ds(k, 1), col] += jnp.sum(dwacc_ref[k], axis=0, keepdims=True)
            return carry

        lax.fori_loop(0, w // LANES, back, 0)

    tile = lambda col: pl.BlockSpec((tr, w), lambda i: (i, col))
    prv = lambda col: pl.BlockSpec((h, w), _prev_halo(tr, h, col))
    nxt = pl.BlockSpec((h, w), _next_halo(tr, h, t, 0))
    vec = pl.BlockSpec((1, w), lambda i: (0, 0))
    kw = dict(
        name="conf_mid_bwd", grid=(n_tiles,),
        in_specs=[tile(0), prv(0), tile(1), prv(1), tile(0), nxt, tile(0), nxt, tile(0),
                  pl.BlockSpec((32, w), lambda i: (0, 0)), vec, vec],
        out_specs=[pl.BlockSpec((tr, w2), lambda i: (i, 0)), pl.BlockSpec((32, w), lambda i: (0, 0)), vec, vec, vec,
                   pl.BlockSpec((1, w2), lambda i: (0, 0)), vec],
        out_shape=[SDS((t, w2), BF), SDS((32, w), F32), SDS((1, w), F32), SDS((1, w), F32), SDS((1, w), F32),
                   SDS((1, w2), F32), SDS((1, w), F32)],
        scratch_shapes=[pltpu.VMEM((tr + h, w), F32), pltpu.VMEM((h + tr, LANES), F32),
                        pltpu.VMEM((SUBLANES, tr + CONV_PAD, LANES), F32), pltpu.VMEM((SUBLANES, tr + CONV_PAD, LANES), F32),
                        pltpu.VMEM((32, SUBLANES, LANES), F32)])
    args = (p1, p1, p1, p1, yc, yc, dys, dys, dy, w_dw, ln_g, ln_b)
    if exchange is not None:
        return _call_hosting(body, exchange, args, **kw)
    return _call(body, compiler_params=_params("arbitrary"), **kw)(*args)


def _group_matrices():
    i = lax.broadcasted_iota(jnp.int32, (512, 512), 0)
    j = lax.broadcasted_iota(jnp.int32, (512, 512), 1)
    mean64 = jnp.where(i // HEAD_DIM == j // HEAD_DIM, 1.0 / HEAD_DIM, 0.0).astype(F32)
    fold64 = jnp.where(i % HEAD_DIM == j % HEAD_DIM, 1.0, 0.0).astype(F32)
    return mean64, fold64


def _split_dot(x, mat):
    hi = x.astype(BF)
    lo = (x - hi.astype(F32)).astype(BF)
    mb = mat.astype(BF)
    return jnp.dot(hi, mb, preferred_element_type=F32) + jnp.dot(lo, mb, preferred_element_type=F32)


@jax.custom_vjp
def _group_sum(x, mat):
    return _split_dot(x, mat)


_group_sum.defvjp(lambda x, mat: (_split_dot(x, mat), mat), lambda mat, ct: (_split_dot(ct, mat), jnp.zeros_like(mat)))


def _bf_dot_plain(a, b):
    return jnp.dot(a.astype(BF), b.astype(BF), preferred_element_type=F32)


@jax.custom_vjp
def _bf_dot(a, b):
    return _bf_dot_plain(a, b)


def _bf_dot_bwd(res, ct):
    a, b = res
    cb = ct.astype(BF)
    return (lax.dot_general(cb, b.astype(BF), NT_DIMS, preferred_element_type=F32),
            lax.dot_general(a.astype(BF), cb, TN_DIMS, preferred_element_type=F32))


_bf_dot.defvjp(lambda a, b: (_bf_dot_plain(a, b), (a, b)), _bf_dot_bwd)


def _prep_tile(proj, qg, kg, zg, ws, bexp, mean64, differentiated=False):
    sw = 512
    q, k, v, u, z = (proj[:, n * sw:(n + 1) * sw] for n in range(5))
    group_sum, dot = (_group_sum, _bf_dot) if differentiated else (_split_dot, _bf_dot_plain)

    def group_norm(x):
        return x * lax.rsqrt(group_sum(x * x, mean64) + NORM_EPS)

    qn = group_norm(q) * qg
    kn = group_norm(k) * kg
    zn = group_norm(_gelu(z)) * zg
    row = lax.broadcasted_iota(jnp.int32, (CHUNK, CHUNK), 0)
    col = lax.broadcasted_iota(jnp.int32, (CHUNK, CHUNK), 1)
    first = lax.broadcasted_iota(jnp.int32, (1, LANES), 1) < HEAD_DIM
    wm = [jnp.where(col <= row, ws[g], 0.0) for g in range(2 * (sw // LANES))]
    chunks = []
    for ci in range(proj.shape[0] // CHUNK):
        parts = []
        for pr in range(sw // LANES):
            zp = zn[ci * CHUNK:(ci + 1) * CHUNK, pr * LANES:(pr + 1) * LANES]
            parts.append(jnp.where(first, dot(wm[2 * pr], zp), dot(wm[2 * pr + 1], zp)))
        chunks.append(jnp.concatenate(parts, axis=1) + bexp)
    s = chunks[0] if len(chunks) == 1 else jnp.concatenate(chunks, axis=0)
    return qn, kn, v, _gelu(u) * s


def _mix_prep_fwd(proj, qg, kg, zg, w_s, l, bexp, mean64, gather=None):
    t = proj.shape[0]
    tr = PREP_CHUNKS * CHUNK

    def body(p_ref, qg_ref, kg_ref, zg_ref, ws_ref, be_ref, m_ref, qkv_ref, go_ref):
        qn, kn, v, go = _prep_tile(p_ref[...], qg_ref[...], kg_ref[...], zg_ref[...], ws_ref[...], be_ref[...], m_ref[...])
        qkv_ref[:, 0:512] = qn.astype(BF)
        qkv_ref[:, 512:1024] = kn.astype(BF)
        qkv_ref[:, 1024:1536] = v.astype(BF)
        go_ref[...] = go.astype(BF)

    vec = pl.BlockSpec((1, 512), lambda i: (0, 0))
    kw = dict(name="mix_prep_fwd", grid=(t // tr,),
              in_specs=[pl.BlockSpec((tr, 2560), lambda i: (i, 0)), vec, vec, vec,
                        pl.BlockSpec((None, 8, CHUNK, CHUNK), lambda i: (l, 0, 0, 0)),
                        pl.BlockSpec((CHUNK, 512), lambda i: (0, 0)), pl.BlockSpec((512, 512), lambda i: (0, 0))],
              out_specs=[pl.BlockSpec((tr, 1536), lambda i: (i, 0)), pl.BlockSpec((tr, 512), lambda i: (i, 0))],
              out_shape=[SDS((t, 1536), BF), SDS((t, 512), BF)])
    args = (proj, qg, kg, zg, w_s, bexp, mean64)
    if gather:
        return _call_gathering(body, gather, args, **kw)
    return _call(body, compiler_params=_params("parallel"), **kw)(*args)


def _mix_prep_bwd(proj, dq, dk, dv, dmix, qg, kg, zg, w_s, l, bexp, mean64, fold64):
    t = proj.shape[0]
    tr = PREP_CHUNKS * CHUNK
    n_tiles = t // tr

    def body(p_ref, dq_ref, dk_ref, dv_ref, dgo_ref, qg_ref, kg_ref, zg_ref, ws_ref, be_ref, m_ref, f_ref,
             dp_ref, dqg_ref, dkg_ref, dzg_ref, dws_ref, dbe_ref):
        i = pl.program_id(0)

        @pl.when(i == 0)
        def _():
            for ref in (dqg_ref, dkg_ref, dzg_ref, dws_ref, dbe_ref):
                ref[...] = jnp.zeros_like(ref)

        fn = functools.partial(_prep_tile, mean64=m_ref[...], differentiated=True)
        _, vjp = jax.vjp(fn, p_ref[...], qg_ref[...], kg_ref[...], zg_ref[...], ws_ref[...], be_ref[...])
        dp, dqg, dkg, dzg, dws, dbe = vjp((dq_ref[...], dk_ref[...], dv_ref[...], dgo_ref[...]))
        dp_ref[...] = dp.astype(BF)
        dqg_ref[pl.ds(0, 1), :] += dqg
        dkg_ref[pl.ds(0, 1), :] += dkg
        dzg_ref[pl.ds(0, 1), :] += dzg
        dws_ref[...] += dws
        dbe_ref[...] += dbe

        @pl.when(i == n_tiles - 1)
        def _():
            dqg_ref[...] = jnp.dot(dqg_ref[...], f_ref[...], precision=HI, preferred_element_type=F32)
            dkg_ref[...] = jnp.dot(dkg_ref[...], f_ref[...], precision=HI, preferred_element_type=F32)
            dbe_ref[...] = jnp.dot(dbe_ref[...], m_ref[...] * float(HEAD_DIM), precision=HI, preferred_element_type=F32)

    vec = pl.BlockSpec((1, 512), lambda i: (0, 0))
    acc = pl.BlockSpec((8, 512), lambda i: (0, 0))
    sq = pl.BlockSpec((512, 512), lambda i: (0, 0))
    row = pl.BlockSpec((tr, 512), lambda i: (i, 0))
    return _call(
        body, name="mix_prep_bwd", grid=(n_tiles,),
        in_specs=[pl.BlockSpec((tr, 2560), lambda i: (i, 0)), row, row, row, pl.BlockSpec((tr, 512), lambda i: (i, 1)),
                  vec, vec, vec, pl.BlockSpec((None, 8, CHUNK, CHUNK), lambda i: (l, 0, 0, 0)),
                  pl.BlockSpec((CHUNK, 512), lambda i: (0, 0)), sq, sq],
        out_specs=[pl.BlockSpec((tr, 2560), lambda i: (i, 0)), acc, acc, acc,
                   pl.BlockSpec((8, CHUNK, CHUNK), lambda i: (0, 0, 0)), pl.BlockSpec((CHUNK, 512), lambda i: (0, 0))],
        out_shape=[SDS((t, 2560), BF), SDS((8, 512), F32), SDS((8, 512), F32), SDS((8, 512), F32),
                   SDS((8, CHUNK, CHUNK), F32), SDS((CHUNK, 512), F32)],
        compiler_params=_params("arbitrary"))(proj, dq, dk, dv, dmix, qg, kg, zg, w_s, bexp, mean64, fold64)


def _sb_logs(qh, kb, valid):
    z = lax.dot_general(qh, kb, NT_DIMS, preferred_element_type=F32) * (HEAD_DIM ** -0.5)
    soft = jnp.log1p(jnp.exp(-jnp.abs(z)))
    lk_raw = -(jnp.maximum(z, 0.0) + soft)
    ls = -(jnp.maximum(-z, 0.0) + soft)
    return lk_raw, ls, jnp.where(valid, lk_raw, 0.0)


def _sb_weights(ls, run, tail, valid):
    return jnp.where(valid, jnp.exp(ls + run + tail), 0.0)


def _att_masks(b):
    row = lax.broadcasted_iota(jnp.int32, (b, b), 0)
    col = lax.broadcasted_iota(jnp.int32, (b, b), 1)
    first = lax.broadcasted_iota(jnp.int32, (1, LANES), 1) < HEAD_DIM
    return row, col, first


N_PAIRS = 4
ROW_SPLIT = 1


def _load_kv(qkv_hbm, k_scr, v_scr, sems, group, width):
    ck = pltpu.make_async_copy(qkv_hbm.at[:, pl.ds(pl.multiple_of(512 + group * width, LANES), width)], k_scr, sems.at[0])
    cv = pltpu.make_async_copy(qkv_hbm.at[:, pl.ds(pl.multiple_of(1024 + group * width, LANES), width)], v_scr, sems.at[1])
    ck.start()
    cv.start()
    ck.wait()
    cv.wait()


def _split_heads(ref, pair, first):
    x = ref[:, pair * LANES:(pair + 1) * LANES]
    zero = jnp.zeros_like(x)
    return jnp.where(first, x, zero), jnp.where(first, zero, x)


def _any_weight_left(runs):
    top = functools.reduce(jnp.maximum, runs)
    return jnp.max(jnp.exp(top)) > 0.0


def _attn_fwd(qkv, pairs_per_step=4, gather=None):
    t = qkv.shape[0]
    b = ATT_BLOCK
    nq = t // b
    width = pairs_per_step * LANES
    n_heads = 2 * pairs_per_step

    def body(q_ref, qkv_hbm, ob_ref, o32_ref, k_scr, v_scr, acc_ref, run_ref, sems):
        group, qi = pl.program_id(0), pl.program_id(1)

        @pl.when(qi == 0)
        def _():
            _load_kv(qkv_hbm, k_scr, v_scr, sems, group, width)

        row, col, first = _att_masks(b)
        qh = [x for pr in range(pairs_per_step) for x in _split_heads(q_ref, pr, first)]
        upper = jnp.where(row > col, 1.0, 0.0).astype(BF)
        acc_ref[...] = jnp.zeros_like(acc_ref)
        run_ref[...] = jnp.zeros_like(run_ref)
        heads = range(n_heads)

        def step(carry):
            j, _ = carry
            rows = pl.ds(pl.multiple_of(j * b, b), b)
            valid = jnp.logical_or(j != qi, col < row)
            lanes = [pl.ds((hh // 2) * LANES, LANES) for hh in heads]
            logs = [_sb_logs(qh[hh], k_scr[rows, lanes[hh]], valid) for hh in heads]
            tails = [_split_dot(logs[hh][2], upper) for hh in heads]
            for hh in heads:
                wgt = _sb_weights(logs[hh][1], run_ref[hh], tails[hh], valid)
                acc_ref[hh] += jnp.dot(wgt.astype(BF), v_scr[rows, lanes[hh]], preferred_element_type=F32)
            for hh in heads:
                run_ref[hh] += jnp.sum(logs[hh][2], axis=1, keepdims=True)
            return j - 1, _any_weight_left([run_ref[hh] for hh in heads])

        lax.while_loop(lambda c: jnp.logical_and(c[0] >= 0, c[1]), step, (qi, jnp.bool_(True)))
        for pr in range(pairs_per_step):
            out = jnp.where(first, acc_ref[2 * pr], acc_ref[2 * pr + 1])
            ob_ref[:, pr * LANES:(pr + 1) * LANES] = out.astype(BF)
            o32_ref[:, pr * LANES:(pr + 1) * LANES] = out

    blk = pl.BlockSpec((b, width), lambda g, qi: (qi, g))
    kw = dict(name="attn_fwd", grid=(N_PAIRS // pairs_per_step, nq),
              in_specs=[blk, pl.BlockSpec(memory_space=pl.ANY)], out_specs=[blk, blk],
              out_shape=[SDS((t, 512), BF), SDS((t, 512), F32)],
              scratch_shapes=[pltpu.VMEM((t, width), BF), pltpu.VMEM((t, width), BF),
                              pltpu.VMEM((n_heads, b, LANES), F32), pltpu.VMEM((n_heads, b, 1), F32),
                              pltpu.SemaphoreType.DMA((2,))])
    if gather:
        return _call_gathering(body, gather, (qkv, qkv), **kw)
    return _call(body, compiler_params=_params("arbitrary", "arbitrary"), **kw)(qkv, qkv)


def _attn_bwd(qkv, a32, dmix, pairs_per_step=2):
    t = qkv.shape[0]
    b = ATT_BLOCK
    bh = b // ROW_SPLIT
    nq = t // b
    width = pairs_per_step * LANES
    n_heads = 2 * pairs_per_step

    def body(q_ref, a_ref, da_ref, qkv_hbm, dq_ref, dk_hbm, dv_hbm,
             k_scr, v_scr, dk_scr, dv_scr, dqa_ref, run_ref, rung_ref, sems):
        group, qi = pl.program_id(0), pl.program_id(1)

        @pl.when(qi == 0)
        def _():
            _load_kv(qkv_hbm, k_scr, v_scr, sems, group, width)
            dk_scr[...] = jnp.zeros_like(dk_scr)
            dv_scr[...] = jnp.zeros_like(dv_scr)

        row, col, first = _att_masks(b)
        qh, dah, dtot = [], [], []
        for pr in range(pairs_per_step):
            qh += _split_heads(q_ref, pr, first)
            da = da_ref[:, pr * LANES:(pr + 1) * LANES]
            prod = da * a_ref[:, pr * LANES:(pr + 1) * LANES]
            dtot += [jnp.sum(jnp.where(first, prod, 0.0), axis=1, keepdims=True),
                     jnp.sum(jnp.where(first, 0.0, prod), axis=1, keepdims=True)]
            dah += [jnp.where(first, da, 0.0).astype(BF), jnp.where(first, 0.0, da).astype(BF)]
        upper = jnp.where(row > col, 1.0, 0.0).astype(BF)
        lower_incl = jnp.where(row >= col, 1.0, 0.0).astype(BF)
        dqa_ref[...] = jnp.zeros_like(dqa_ref)
        run_ref[...] = jnp.zeros_like(run_ref)
        rung_ref[...] = jnp.zeros_like(rung_ref)
        chains = [(hh, s) for hh in range(n_heads) for s in range(ROW_SPLIT)]
        ids = range(len(chains))
        part = lambda x, s: x[s * bh:(s + 1) * bh]
        row_h = lax.broadcasted_iota(jnp.int32, (bh, b), 0)
        col_h = lax.broadcasted_iota(jnp.int32, (bh, b), 1)
        causal = [col_h < row_h + s * bh for s in range(ROW_SPLIT)]
        qc = [part(qh[hh], s) for hh, s in chains]
        dac = [part(dah[hh], s) for hh, s in chains]
        dtc = [part(dtot[hh], s) for hh, s in chains]
        lanes = [pl.ds((hh // 2) * LANES, LANES) for hh, _ in chains]

        def step(carry):
            j, _ = carry
            rows = pl.ds(pl.multiple_of(j * b, b), b)
            valid = [jnp.logical_or(j != qi, causal[s]) for _, s in chains]
            logs = [_sb_logs(qc[c], k_scr[rows, lanes[c]], valid[c]) for c in ids]
            runs = [run_ref[c] for c in ids]
            new_runs = [runs[c] + jnp.sum(logs[c][2], axis=1, keepdims=True) for c in ids]
            alive = _any_weight_left(new_runs)
            dps = [lax.dot_general(dac[c], v_scr[rows, lanes[c]], NT_DIMS, preferred_element_type=F32) for c in ids]
            tails = [_split_dot(logs[c][2], upper) for c in ids]
            wgts = [_sb_weights(logs[c][1], runs[c], tails[c], valid[c]) for c in ids]
            gs = [wgts[c] * dps[c] for c in ids]
            g_froms = [_split_dot(gs[c], lower_incl) for c in ids]
            for c in ids:
                lk_raw, ls, _ = logs[c]
                dlk = jnp.where(valid[c], dtc[c] - rung_ref[c] - g_froms[c], 0.0)
                dz = ((gs[c] * jnp.exp(lk_raw) - dlk * jnp.exp(ls)) * (HEAD_DIM ** -0.5)).astype(BF)
                dqa_ref[c] += jnp.dot(dz, k_scr[rows, lanes[c]], preferred_element_type=F32)
                dk_scr[rows, lanes[c]] += lax.dot_general(dz, qc[c], TN_DIMS, preferred_element_type=F32)
                dv_scr[rows, lanes[c]] += lax.dot_general(wgts[c].astype(BF), dac[c], TN_DIMS, preferred_element_type=F32)
            for c in ids:
                rung_ref[c] += jnp.sum(gs[c], axis=1, keepdims=True)
                run_ref[c] = new_runs[c]
            return j - 1, alive

        lax.while_loop(lambda c: jnp.logical_and(c[0] >= 0, c[1]), step, (qi, jnp.bool_(True)))
        for pr in range(pairs_per_step):
            for s in range(ROW_SPLIT):
                c0 = 2 * pr * ROW_SPLIT + s
                dq_ref[pl.ds(s * bh, bh), pr * LANES:(pr + 1) * LANES] = jnp.where(first, dqa_ref[c0], dqa_ref[c0 + ROW_SPLIT])

        @pl.when(qi == nq - 1)
        def _():
            cols = pl.ds(pl.multiple_of(group * width, LANES), width)
            ck = pltpu.make_async_copy(dk_scr, dk_hbm.at[:, cols], sems.at[0])
            cv = pltpu.make_async_copy(dv_scr, dv_hbm.at[:, cols], sems.at[1])
            ck.start()
            cv.start()
            ck.wait()
            cv.wait()

    blk = pl.BlockSpec((b, width), lambda g, qi: (qi, g))
    anywhere = pl.BlockSpec(memory_space=pl.ANY)
    return _call(
        body, name="attn_bwd", grid=(N_PAIRS // pairs_per_step, nq),
        in_specs=[blk, blk, blk, anywhere], out_specs=[blk, anywhere, anywhere],
        out_shape=[SDS((t, 512), F32), SDS((t, 512), F32), SDS((t, 512), F32)],
        scratch_shapes=[pltpu.VMEM((t, width), BF), pltpu.VMEM((t, width), BF),
                        pltpu.VMEM((t, width), F32), pltpu.VMEM((t, width), F32),
                        pltpu.VMEM((n_heads * ROW_SPLIT, bh, LANES), F32), pltpu.VMEM((n_heads * ROW_SPLIT, bh, 1), F32),
                        pltpu.VMEM((n_heads * ROW_SPLIT, bh, 1), F32), pltpu.SemaphoreType.DMA((2,))],
        compiler_params=_params("arbitrary", "arbitrary"))(qkv, a32, dmix, qkv)


def _adamw(w, g, m, v):
    n, c = w.shape
    tr = min(256, n)
    assert n % tr == 0

    def body(w_ref, g_ref, m_ref, v_ref, d_ref, nm_ref, nv_ref):
        g = g_ref[...]
        m = ADAM_B1 * m_ref[...] + (1.0 - ADAM_B1) * g
        v = ADAM_B2 * v_ref[...] + (1.0 - ADAM_B2) * jnp.square(g)
        m_hat = m / (1.0 - ADAM_B1 ** ADAM_STEP)
        v_hat = v / (1.0 - ADAM_B2 ** ADAM_STEP)
        d_ref[...] = -ADAM_LR * (m_hat / (jnp.sqrt(v_hat) + ADAM_EPS) + ADAM_WD * w_ref[...])
        nm_ref[...] = m
        nv_ref[...] = v

    blk = pl.BlockSpec((tr, c), lambda i: (i, 0))
    return _call(
        body, name="adamw", grid=(n // tr,), in_specs=[blk] * 4, out_specs=[blk] * 3,
        out_shape=[SDS((n, c), F32)] * 3, compiler_params=_params("parallel"))(w, g, m, v)


def _mesh_pos():
    return lax.axis_index("x"), lax.axis_index("y"), lax.axis_index("c")


def _other_chips(x, y):
    return [(1 - x, y), (x, 1 - y), (1 - x, 1 - y)]


HBM_SPEC = pl.BlockSpec(memory_space=pltpu.HBM)


GATHER_COPIES = 6


def _gather_steps(s_ref, o_ref, send_sems, recv_sems, local_sems, slot):
    h = s_ref.shape[1] // 2
    x, y, c = _mesh_pos()
    sibling = (x, y, 1 - c)
    chips = _other_chips(x, y)
    base = GATHER_COPIES * slot

    def half(px, py, hc):
        return o_ref.at[:, 2 * px + py, pl.ds(hc * h, h), :]

    def copy(k, dst, to, src=None):
        return pltpu.make_async_remote_copy(
            src_ref=dst if src is None else src, dst_ref=dst, send_sem=send_sems.at[base + k],
            recv_sem=recv_sems.at[base + k], device_id=to, device_id_type=MESH)

    mine = pltpu.make_async_copy(s_ref, o_ref.at[:, 2 * x + y], local_sems.at[slot])
    first = [copy(j, half(x, y, c), (*chip, c), src=s_ref.at[:, pl.ds(c * h, h), :]) for j, chip in enumerate(chips)]
    passed = [copy(3 + j, half(*chip, c), sibling) for j, chip in enumerate(chips)]

    def start():
        mine.start()
        for cp in first:
            cp.start()

    def finish():
        for j, chip in enumerate(chips):
            copy(j, half(*chip, c), (x, y, c)).wait_recv()
            passed[j].start()
        for j, chip in enumerate(chips):
            copy(3 + j, half(*chip, 1 - c), (x, y, c)).wait_recv()
        for cp in first + passed:
            cp.wait_send()
        mine.wait()

    return start, finish


def _gather_scratch(n):
    return [pltpu.SemaphoreType.DMA((GATHER_COPIES * n,)), pltpu.SemaphoreType.DMA((GATHER_COPIES * n,)),
            pltpu.SemaphoreType.DMA((n,))]


def _gathered_shape(shard):
    n_l, r, c_w = shard.shape
    return SDS((n_l, N_CHIPS, r, c_w), shard.dtype)


def _all_gather(shard):
    def body(s_ref, o_ref, send_sems, recv_sems, local_sems):
        start, finish = _gather_steps(s_ref, o_ref, send_sems, recv_sems, local_sems, 0)
        start()
        finish()

    return _call(body, name="all_gather", in_specs=[HBM_SPEC], out_specs=HBM_SPEC, out_shape=_gathered_shape(shard),
                 scratch_shapes=_gather_scratch(1))(shard)


class _Exchange(NamedTuple):
    tag: str
    inputs: list
    out_shapes: list
    scratch: list
    make_steps: Callable


def _gather_exchange(shards):
    n = len(shards)

    def make_steps(s_refs, o_refs, sems):
        steps = [_gather_steps(s_refs[k], o_refs[k], *sems, k) for k in range(n)]
        return (lambda: [start() for start, _ in steps]), (lambda: [finish() for _, finish in steps])

    return _Exchange("gathering", list(shards), [_gathered_shape(s) for s in shards], _gather_scratch(n), make_steps)


def _call_hosting(body, exchange, args, *, name, grid, in_specs, out_specs, out_shape, scratch_shapes=()):
    out_specs = list(out_specs) if isinstance(out_specs, (list, tuple)) else [out_specs]
    out_shape = list(out_shape) if isinstance(out_shape, (list, tuple)) else [out_shape]
    n_in, n_out, n_scr = len(in_specs), len(out_specs), len(scratch_shapes)
    n_xi, n_xo, n_sem = len(exchange.inputs), len(exchange.out_shapes), len(exchange.scratch)

    def hosting_body(*refs):
        ins, x_ins = refs[:n_in], refs[n_in:n_in + n_xi]
        outs = refs[n_in + n_xi:n_in + n_xi + n_out]
        x_outs = refs[n_in + n_xi + n_out:n_in + n_xi + n_out + n_xo]
        scratch = refs[n_in + n_xi + n_out + n_xo:n_in + n_xi + n_out + n_xo + n_scr]
        start, finish = exchange.make_steps(x_ins, x_outs, refs[len(refs) - n_sem:])
        is_first = functools.reduce(jnp.logical_and, [pl.program_id(a) == 0 for a in range(len(grid))])
        is_last = functools.reduce(jnp.logical_and, [pl.program_id(a) == grid[a] - 1 for a in range(len(grid))])

        @pl.when(is_first)
        def _():
            start()

        body(*ins, *outs, *scratch)

        @pl.when(is_last)
        def _():
            finish()

    res = _call(
        hosting_body, name=name + "_" + exchange.tag, grid=grid, in_specs=list(in_specs) + [HBM_SPEC] * n_xi,
        out_specs=out_specs + [HBM_SPEC] * n_xo, out_shape=out_shape + list(exchange.out_shapes),
        scratch_shapes=list(scratch_shapes) + list(exchange.scratch),
        compiler_params=_params(*(["arbitrary"] * len(grid))))(*args, *exchange.inputs)
    return res[:n_out], res[n_out:]


def _call_gathering(body, shards, args, **kw):
    return _call_hosting(body, _gather_exchange(shards), args, **kw)


def _row_tile(h):
    assert h <= 512
    return h


def _rs_pair(gs):
    n = len(gs)

    def body(*refs):
        g_refs, a_refs, (send_sems, recv_sems) = refs[:n], refs[n:2 * n], refs[2 * n:]
        x, y, c = _mesh_pos()
        cps = []
        for k in range(n):
            h = g_refs[k].shape[2] // 2
            cps.append(pltpu.make_async_remote_copy(
                src_ref=g_refs[k].at[:, :, pl.ds((1 - c) * h, h), :], dst_ref=a_refs[k], send_sem=send_sems.at[k],
                recv_sem=recv_sems.at[k], device_id=(x, y, 1 - c), device_id_type=MESH))
        for cp in cps:
            cp.start()
        for cp in cps:
            cp.wait()

    out_shape = [SDS((g.shape[0], g.shape[1], g.shape[2] // 2, g.shape[3]), g.dtype) for g in gs]
    return _call(body, name="rs_pair", in_specs=[HBM_SPEC] * n, out_specs=[HBM_SPEC] * n, out_shape=out_shape,
                 scratch_shapes=[pltpu.SemaphoreType.DMA((n,)), pltpu.SemaphoreType.DMA((n,))])(*gs)


def _rs_pair_add(g, from_sibling, mid_dtype):
    n_l, n_p, r, c_w = g.shape
    h = r // 2
    tr = _row_tile(h)
    nt = h // tr
    c_arr = jnp.reshape(lax.axis_index("c"), (1,)).astype(jnp.int32)

    def body(c_ref, g_ref, a_ref, o_ref):
        o_ref[...] = (g_ref[...].astype(F32) + a_ref[...].astype(F32)).astype(o_ref.dtype)

    blk = (None, None, tr, c_w)
    return _call(
        body, name="rs_pair_add",
        grid_spec=pltpu.PrefetchScalarGridSpec(
            num_scalar_prefetch=1, grid=(n_l, n_p, nt),
            in_specs=[pl.BlockSpec(blk, lambda l, p, t, c_ref: (l, p, c_ref[0] * nt + t, 0)),
                      pl.BlockSpec(blk, lambda l, p, t, c_ref: (l, p, t, 0))],
            out_specs=pl.BlockSpec(blk, lambda l, p, t, c_ref: (l, p, t, 0))),
        out_shape=SDS((n_l, n_p, h, c_w), mid_dtype),
        compiler_params=_params("parallel", "parallel", "parallel"))(c_arr, g, from_sibling)


def _chips_exchange(pair_sums):
    n = len(pair_sums)

    def make_steps(s_refs, b_refs, sems):
        send_sems, recv_sems = sems
        x, y, c = _mesh_pos()
        cps = [pltpu.make_async_remote_copy(
            src_ref=s_refs[k].at[:, 2 * chip[0] + chip[1]], dst_ref=b_refs[k].at[j], send_sem=send_sems.at[3 * k + j],
            recv_sem=recv_sems.at[3 * k + j], device_id=(*chip, c), device_id_type=MESH)
            for k in range(n) for j, chip in enumerate(_other_chips(x, y))]
        return (lambda: [cp.start() for cp in cps]), (lambda: [cp.wait() for cp in cps])

    out_shapes = [SDS((3, s.shape[0], s.shape[2], s.shape[3]), s.dtype) for s in pair_sums]
    sems = [pltpu.SemaphoreType.DMA((3 * n,)), pltpu.SemaphoreType.DMA((3 * n,))]
    return _Exchange("scattering", list(pair_sums), out_shapes, sems, make_steps)


def _rs_chips(pair_sums):
    ex = _chips_exchange(pair_sums)
    n = len(pair_sums)

    def body(*refs):
        start, finish = ex.make_steps(refs[:n], refs[n:2 * n], refs[2 * n:])
        start()
        finish()

    return _call(body, name="rs_chips", in_specs=[HBM_SPEC] * n, out_specs=[HBM_SPEC] * n, out_shape=ex.out_shapes,
                 scratch_shapes=ex.scratch)(*pair_sums)


def _rs_chip_add(pair_sum, from_chips, n_layers, first_layer, buf=None):
    n_l, _, h, c_w = pair_sum.shape
    tr = _row_tile(h)
    nt = h // tr
    p_arr = jnp.reshape(2 * lax.axis_index("x") + lax.axis_index("y"), (1,)).astype(jnp.int32)
    c_arr = jnp.reshape(lax.axis_index("c"), (1,)).astype(jnp.int32)

    def body(p_ref, c_ref, s_ref, b_ref, *rest):
        acc = s_ref[...].astype(F32)
        for j in range(3):
            acc = acc + b_ref[j].astype(F32)
        rest[-1][...] = acc

    in_specs = [pl.BlockSpec((None, None, tr, c_w), lambda l, t, p_ref, c_ref: (l, p_ref[0], t, 0)),
                pl.BlockSpec((3, None, tr, c_w), lambda l, t, p_ref, c_ref: (0, l, t, 0))]
    args = [p_arr, c_arr, pair_sum, from_chips]
    aliases = {}
    if buf is not None:
        in_specs.append(pl.BlockSpec(memory_space=pl.ANY))
        args.append(buf)
        aliases = {4: 0}
    return _call(
        body, name="rs_chip_add",
        grid_spec=pltpu.PrefetchScalarGridSpec(
            num_scalar_prefetch=2, grid=(n_l, nt), in_specs=in_specs,
            out_specs=pl.BlockSpec((None, tr, c_w), lambda l, t, p_ref, c_ref: (first_layer + l, c_ref[0] * nt + t, 0))),
        out_shape=SDS((n_layers, 2 * h, c_w), F32), input_output_aliases=aliases,
        compiler_params=_params("parallel", "parallel"))(*args)


def _rs_swap(halves):
    n = len(halves)

    def body(*refs):
        outs, (send_sems, recv_sems) = refs[n:2 * n], refs[2 * n:]
        x, y, c = _mesh_pos()

        def copy(k, half):
            h = outs[k].shape[1] // 2
            mine = outs[k].at[:, pl.ds(c * h, h), :]
            return pltpu.make_async_remote_copy(
                src_ref=mine, dst_ref=mine if half == "mine" else outs[k].at[:, pl.ds((1 - c) * h, h), :],
                send_sem=send_sems.at[k], recv_sem=recv_sems.at[k], device_id=(x, y, 1 - c), device_id_type=MESH)

        for k in range(n):
            copy(k, "mine").start()
        for k in range(n):
            copy(k, "theirs").wait_send()
            copy(k, "theirs").wait_recv()

    return _call(body, name="rs_swap", in_specs=[HBM_SPEC] * n, out_specs=[HBM_SPEC] * n,
                 out_shape=[SDS(a.shape, F32) for a in halves], input_output_aliases={k: k for k in range(n)},
                 scratch_shapes=[pltpu.SemaphoreType.DMA((n,)), pltpu.SemaphoreType.DMA((n,))])(*halves)


def _pack(arrays, row_multiple):
    flat = jnp.concatenate([a.reshape(-1).astype(F32) for a in arrays])
    unit = row_multiple * LANES
    padded = -(-flat.shape[0] // unit) * unit
    return jnp.pad(flat, (0, padded - flat.shape[0])).reshape(padded // LANES, LANES)


def _unpack(packed, shapes):
    flat = packed.reshape(-1)
    out, pos = [], 0
    for s in shapes:
        size = 1
        for dim in s:
            size *= dim
        out.append(flat[pos:pos + size].reshape(s))
        pos += size
    return out


BIG_COL = ("sb_w_in", "cv_w_pw1", "ffn_w_up")
BIG_ROW = ("hyb_w_out", "cv_w_pw2", "ffn_w_down")
SMALL_SHARDED = ("cv_b_pw1", "cv_w_dw", "cv_b_dw", "cv_ln_g", "cv_ln_b", "cv_b_pw2", "ffn_w_dw")
SMALL_REPLICATED = ("mix_norm_g", "sb_q_norm_g", "sb_k_norm_g", "sg_z_norm_g", "sg_w_spatial", "sg_b_spatial",
                    "ffn_norm_g", "ffn_b_dw")
WEIGHTS = ("mix_norm_g", "sb_w_in", "sb_q_norm_g", "sb_k_norm_g", "sg_z_norm_g", "sg_w_spatial", "sg_b_spatial",
           "hyb_w_out", "cv_w_pw1", "cv_b_pw1", "cv_w_dw", "cv_b_dw", "cv_ln_g", "cv_ln_b", "cv_w_pw2", "cv_b_pw2",
           "ffn_norm_g", "ffn_w_up", "ffn_w_dw", "ffn_b_dw", "ffn_w_down")


def _pad_rows(a, rows):
    return jnp.pad(a, ((0, rows - a.shape[0]), (0, 0)))


def _step(x, tgt, w, m, v):
    n_layers = w["mix_norm_g"].shape[0]
    xi, yi, ci = _mesh_pos()
    chip = 2 * xi + yi

    assert n_layers == 4
    hosted_by = {("proj", 0): ["hyb_w_out"], ("prep", 0): [("ffn_w_up", 0)],
                 ("attn", 0): [("ffn_w_down", 0), "cv_w_pw1", "cv_w_pw2"],
                 ("up", 0): [("ffn_w_up", 1)], ("ffn_mid", 0): [("ffn_w_down", 1)],
                 ("conf_mid", 1): [("ffn_w_up", 2), ("ffn_w_down", 2)],
                 ("up", 1): [("ffn_w_up", 3)], ("ffn_mid", 1): [("ffn_w_down", 3), ("sb_w_in", 1)]}
    full = {}

    def shard_of(key):
        if isinstance(key, tuple):
            return w[key[0]][key[1]:key[1] + 1].astype(BF)
        return w[key].astype(BF)

    def keep(key, g4):
        if (key[0] if isinstance(key, tuple) else key) in BIG_ROW:
            g4 = g4.reshape(g4.shape[0], 1, g4.shape[1] * g4.shape[2], g4.shape[3])
        full[key] = g4

    def hosting(fn, point, *args, **kw):
        keys = hosted_by.get(point)
        if not keys:
            return fn(*args, **kw)
        out, gathered = fn(*args, gather=[shard_of(k) for k in keys], **kw)
        for key, g4 in zip(keys, gathered):
            keep(key, g4)
        return out

    keep(("sb_w_in", 0), _all_gather(shard_of(("sb_w_in", 0))))
    small_local = [w[name] for name in SMALL_SHARDED]
    gathered = _all_gather(_pack(small_local, 32)[None])[0]
    per_chip = [_unpack(gathered[p], [a.shape for a in small_local]) for p in range(N_CHIPS)]
    for k, name in enumerate(SMALL_SHARDED):
        full[name] = jnp.concatenate([per_chip[p][k] for p in range(N_CHIPS)], axis=-1)
    for name in SMALL_REPLICATED:
        full[name] = w[name]

    mean64, fold64 = _group_matrices()
    ffn_wdw = [_pad_rows(full["ffn_w_dw"][i], 8) for i in range(n_layers)]
    cv_wdw = [_pad_rows(full["cv_w_dw"][j], 32) for j in range(n_layers // 2)]
    row = lambda a: a.reshape(1, -1)

    saved = []
    cur = x
    h = _rms_fwd(cur, row(full["mix_norm_g"][0]))
    for i in range(n_layers):
        j = i // 2
        rec = {"x_in": cur, "h_mix": h}
        if i % 2 == 0:
            proj = hosting(_mm_nn, ("proj", i), h, full[("sb_w_in", j)], 0)
            qg = row(jnp.tile(full["sb_q_norm_g"][j], 512 // HEAD_DIM))
            kg = row(jnp.tile(full["sb_k_norm_g"][j], 512 // HEAD_DIM))
            zg = row(full["sg_z_norm_g"][j])
            bexp = jnp.repeat(full["sg_b_spatial"][j].T, HEAD_DIM, axis=1)
            qkv, gated = hosting(_mix_prep_fwd, ("prep", i), proj, qg, kg, zg, full["sg_w_spatial"], j, bexp, mean64)
            att_bf, att_32 = hosting(_attn_fwd, ("attn", i), qkv)
            mix = jnp.concatenate([att_bf, gated], axis=1)
            cur, h = _mm_nn(mix, full["hyb_w_out"], j, resid=cur, norm_g=row(full["ffn_norm_g"][i]))
            rec.update(proj=proj, qkv=qkv, att_32=att_32, mix=mix, qg=qg, kg=kg, zg=zg, bexp=bexp)
        else:
            p1 = _mm_nn(h, full["cv_w_pw1"], j, bias=row(full["cv_b_pw1"][j]), out_dtype=BF)
            ys, yc = hosting(_conf_mid_fwd, ("conf_mid", i), p1, cv_wdw[j], row(full["cv_b_dw"][j]),
                             row(full["cv_ln_g"][j]), row(full["cv_ln_b"][j]))
            cur, h = _mm_nn(ys, full["cv_w_pw2"], j, bias=row(full["cv_b_pw2"][j]), resid=cur,
                            norm_g=row(full["ffn_norm_g"][i]))
            rec.update(p1=p1, ys=ys, yc=yc)
        rec["x_mid"] = cur
        up = hosting(_mm_nn, ("up", i), h, full[("ffn_w_up", i)], 0, out_dtype=BF)
        act = hosting(_ffn_mid_fwd, ("ffn_mid", i), up, ffn_wdw[i], row(full["ffn_b_dw"][i]))
        rec.update(h_ffn=h, up=up, act=act)
        if i + 1 < n_layers:
            cur, h = _mm_nn(act, full[("ffn_w_down", i)], 0, resid=cur, norm_g=row(full["mix_norm_g"][i + 1]))
        else:
            cur = _mm_nn(act, full[("ffn_w_down", i)], 0, resid=cur)
        saved.append(rec)

    loss_vec, dy, dy_bf = _loss_grad(cur, tgt)
    loss = lax.psum(loss_vec[0, 0], ("x", "y", "c"))

    big_names = BIG_COL + BIG_ROW
    gbig = {}
    gsmall = {name: [None] * w[name].shape[0] for name in SMALL_SHARDED + SMALL_REPLICATED}

    def accumulate(name, layer, a, dy_, p_n):
        per_group = w[name].shape[0] // 2
        grp, slot = divmod(layer, per_group)
        gbig[(name, grp)] = _mm_tn(a, dy_, p_n, per_group, slot, gbig.get((name, grp)))

    def group_grads(grp):
        out = []
        for name in big_names:
            g4 = gbig[(name, grp)]
            if name in BIG_ROW:
                g4 = g4.reshape(g4.shape[0], N_CHIPS, g4.shape[2] // N_CHIPS, g4.shape[3])
            out.append(g4)
        return out

    def pair_sums_of(gs, mid_dtypes):
        return [_rs_pair_add(g, a, dt) for g, a, dt in zip(gs, _rs_pair(gs), mid_dtypes)]

    half_sums = {}
    late_pair_sums = None
    for i in reversed(range(n_layers)):
        j = i // 2
        rec = saved[i]
        dact = _mm_nt(dy_bf, full[("ffn_w_down", i)], 0, out_dtype=BF)
        accumulate("ffn_w_down", i, rec["act"], dy_bf, 1)
        dup, dwdw, dbdw = _ffn_mid_bwd(rec["up"], dact, ffn_wdw[i], row(full["ffn_b_dw"][i]))
        gsmall["ffn_w_dw"][i] = dwdw[:FFN_K]
        gsmall["ffn_b_dw"][i] = dbdw[0]
        accumulate("ffn_w_up", i, rec["h_ffn"], dup, N_CHIPS)
        dy, dy_bf, dg = _mm_nt_rms_bwd(dup, full[("ffn_w_up", i)], 0, rec["x_mid"], row(full["ffn_norm_g"][i]), dy)
        gsmall["ffn_norm_g"][i] = dg[0]
        if i % 2 == 0:
            dmix = _mm_nt(dy_bf, full["hyb_w_out"], j)
            accumulate("hyb_w_out", j, rec["mix"], dy_bf, 1)
            dq, dk, dv = _attn_bwd(rec["qkv"], rec["att_32"], dmix)
            dproj, dqg, dkg, dzg, dws, dbe = _mix_prep_bwd(
                rec["proj"], dq, dk, dv, dmix, rec["qg"], rec["kg"], rec["zg"], full["sg_w_spatial"], j, rec["bexp"],
                mean64, fold64)
            gsmall["sb_q_norm_g"][j] = dqg[0, :HEAD_DIM]
            gsmall["sb_k_norm_g"][j] = dkg[0, :HEAD_DIM]
            gsmall["sg_z_norm_g"][j] = dzg[0]
            gsmall["sg_w_spatial"][j] = dws
            gsmall["sg_b_spatial"][j] = dbe[:, ::HEAD_DIM].T
            dlast, w_first, l_first = dproj, full[("sb_w_in", j)], 0
            accumulate("sb_w_in", j, rec["h_mix"], dproj, N_CHIPS)
        else:
            dys = _mm_nt(dy_bf, full["cv_w_pw2"], j, out_dtype=BF)
            accumulate("cv_w_pw2", j, rec["ys"], dy_bf, 1)
            carried = _chips_exchange(late_pair_sums) if late_pair_sums is not None else None
            res = _conf_mid_bwd(rec["p1"], rec["yc"], dys, dy, cv_wdw[j], row(full["cv_ln_g"][j]),
                                row(full["cv_ln_b"][j]), exchange=carried)
            if carried is not None:
                res, from_chips = res
                for name, ps, fc in zip(big_names, late_pair_sums, from_chips):
                    n_all = w[name].shape[0]
                    half_sums[name] = _rs_chip_add(ps, fc, n_all, n_all // 2)
                late_pair_sums = None
            dp1, dwdw, dbdw, dlg, dlb, db1, db2 = res
            gsmall["cv_w_dw"][j] = dwdw[:CONV_K]
            gsmall["cv_b_dw"][j] = dbdw[0]
            gsmall["cv_ln_g"][j] = dlg[0]
            gsmall["cv_ln_b"][j] = dlb[0]
            gsmall["cv_b_pw1"][j] = db1[0]
            gsmall["cv_b_pw2"][j] = db2[0]
            dlast, w_first, l_first = dp1, full["cv_w_pw1"], j
            accumulate("cv_w_pw1", j, rec["h_mix"], dp1, N_CHIPS)
        dy, dy_bf, dg = _mm_nt_rms_bwd(dlast, w_first, l_first, rec["x_in"], row(full["mix_norm_g"][i]), dy)
        gsmall["mix_norm_g"][i] = dg[0]
        if i == n_layers // 2:
            late_pair_sums = pair_sums_of(group_grads(1), [BF] * len(big_names))

    small_names = SMALL_REPLICATED + SMALL_SHARDED
    small_full = [jnp.stack(gsmall[name]) for name in small_names]
    packed = _pack(small_full, 32 * N_CHIPS)
    rows_q = packed.shape[0] // N_CHIPS
    early = pair_sums_of(group_grads(0) + [packed.reshape(1, N_CHIPS, rows_q, LANES)], [BF] * len(big_names) + [F32])
    from_chips = _rs_chips(early)
    halves = [_rs_chip_add(ps, fc, w[name].shape[0], 0, half_sums[name])
              for name, ps, fc in zip(big_names, early, from_chips)]
    halves.append(_rs_chip_add(early[-1], from_chips[-1], 1, 0))
    swapped = _rs_swap(halves)
    grads = dict(zip(big_names, swapped))
    summed = _all_gather(swapped[-1]).reshape(-1, LANES)
    for name, gsum in zip(small_names, _unpack(summed, [a.shape for a in small_full])):
        if name in SMALL_SHARDED:
            n_loc = w[name].shape[-1]
            split = gsum.reshape(gsum.shape[:-1] + (N_CHIPS, n_loc))
            gsum = lax.dynamic_index_in_dim(split, chip, axis=split.ndim - 2, keepdims=False)
        grads[name] = gsum

    delta, new_m, new_v = {}, {}, {}
    for name in BIG_COL + BIG_ROW:
        shp = w[name].shape
        two_d = lambda a: a.reshape(shp[0] * shp[1], shp[2])
        d, nm, nv = _adamw(two_d(w[name]), two_d(grads[name]), two_d(m[name]), two_d(v[name]))
        delta[name], new_m[name], new_v[name] = d.reshape(shp), nm.reshape(shp), nv.reshape(shp)
    shapes = [w[name].shape for name in small_names]
    d, nm, nv = _adamw(*(_pack([src[name] for name in small_names], 256) for src in (w, grads, m, v)))
    for name, a, b_, c_ in zip(small_names, _unpack(d, shapes), _unpack(nm, shapes), _unpack(nv, shapes)):
        delta[name], new_m[name], new_v[name] = a, b_, c_

    return (loss, dy, *[grads[n] for n in WEIGHTS], *[delta[n] for n in WEIGHTS],
            *[new_m[n] for n in WEIGHTS], *[new_v[n] for n in WEIGHTS])


def kernel(x, mix_norm_g, sb_w_in, sb_q_norm_g, sb_k_norm_g, sg_z_norm_g, sg_w_spatial, sg_b_spatial, hyb_w_out, cv_w_pw1, cv_b_pw1, cv_w_dw, cv_b_dw, cv_ln_g, cv_ln_b, cv_w_pw2, cv_b_pw2, ffn_norm_g, ffn_w_up, ffn_w_dw, ffn_b_dw, ffn_w_down, loss_target, m_mix_norm_g, m_sb_w_in, m_sb_q_norm_g, m_sb_k_norm_g, m_sg_z_norm_g, m_sg_w_spatial, m_sg_b_spatial, m_hyb_w_out, m_cv_w_pw1, m_cv_b_pw1, m_cv_w_dw, m_cv_b_dw, m_cv_ln_g, m_cv_ln_b, m_cv_w_pw2, m_cv_b_pw2, m_ffn_norm_g, m_ffn_w_up, m_ffn_w_dw, m_ffn_b_dw, m_ffn_w_down, v_mix_norm_g, v_sb_w_in, v_sb_q_norm_g, v_sb_k_norm_g, v_sg_z_norm_g, v_sg_w_spatial, v_sg_b_spatial, v_hyb_w_out, v_cv_w_pw1, v_cv_b_pw1, v_cv_w_dw, v_cv_b_dw, v_cv_ln_g, v_cv_ln_b, v_cv_w_pw2, v_cv_b_pw2, v_ffn_norm_g, v_ffn_w_up, v_ffn_w_dw, v_ffn_b_dw, v_ffn_w_down):
    given = dict(locals())
    w = {n: given[n] for n in WEIGHTS}
    m = {n: given["m_" + n] for n in WEIGHTS}
    v = {n: given["v_" + n] for n in WEIGHTS}
    out = _step(x[0], loss_target[0], w, m, v)
    return (out[0], out[1][None], *out[2:])
```

```python
import functools
from typing import Callable, NamedTuple

import jax
import jax.numpy as jnp
from jax import lax
from jax.experimental import pallas as pl
from jax.experimental.pallas import tpu as pltpu

F32 = jnp.float32
BF = jnp.bfloat16
SDS = jax.ShapeDtypeStruct
HI = lax.Precision.HIGHEST
MESH = pl.DeviceIdType.MESH

NORM_EPS = 1e-6
HEAD_DIM = 64
ATT_BLOCK = 128
CHUNK = 128
PREP_CHUNKS = 4
CONV_K = 31
CONV_HALO = 32
FFN_K = 3
FFN_HALO = 16
LANES = 128
N_CHIPS = 4
VMEM_LIMIT_BYTES = 56 * 2**20

ADAM_LR = 0.001
ADAM_B1 = 0.9
ADAM_B2 = 0.999
ADAM_EPS = 1e-08
ADAM_WD = 0.01
ADAM_STEP = 10

NT_DIMS = (((1,), (1,)), ((), ()))
TN_DIMS = (((0,), (0,)), ((), ()))


def _call(body, **kw):
    return pl.pallas_call(body, **kw)


def _params(*sem):
    return pltpu.CompilerParams(dimension_semantics=sem, vmem_limit_bytes=VMEM_LIMIT_BYTES)


def _gelu(x):
    return 0.5 * x * (1.0 + lax.erf(x * 0.7071067811865476))


def _rms(x, g):
    y = x * lax.rsqrt(jnp.mean(x * x, axis=-1, keepdims=True) + NORM_EPS)
    return y * g


def _rms_fwd(x, g):
    t, d = x.shape
    tm = min(512, t)

    def body(x_ref, g_ref, o_ref):
        o_ref[...] = _rms(x_ref[...], g_ref[...]).astype(o_ref.dtype)

    return _call(
        body, name="rms_fwd", grid=(t // tm,),
        in_specs=[pl.BlockSpec((tm, d), lambda i: (i, 0)), pl.BlockSpec((1, d), lambda i: (0, 0))],
        out_specs=pl.BlockSpec((tm, d), lambda i: (i, 0)),
        out_shape=SDS((t, d), BF), compiler_params=_params("parallel"))(x, g)


def _mm_nn(a, w, l, bias=None, resid=None, out_dtype=F32, gather=None, norm_g=None):
    m, k = a.shape
    _, p_n, kw, n = w.shape
    assert k == kw
    normed = norm_g is not None
    assert not normed or (p_n == 1 and not gather)
    tm = min(512 if normed else 1024, m)
    tn = n if (normed or k * n * 2 <= 4 * 2**20) else n // 2
    nj = n // tn
    in_specs = [pl.BlockSpec((tm, k), lambda i, p, j: (i, 0)),
                pl.BlockSpec((None, None, k, tn), lambda i, p, j: (l, p, 0, j))]
    args = [a, w]
    if bias is not None:
        in_specs.append(pl.BlockSpec((1, tn), lambda i, p, j: (0, p * nj + j)))
        args.append(bias)
    if resid is not None:
        in_specs.append(pl.BlockSpec((tm, tn), lambda i, p, j: (i, p * nj + j)))
        args.append(resid)
    if normed:
        in_specs.append(pl.BlockSpec((1, n), lambda i, p, j: (0, 0)))
        args.append(norm_g)
    n_in = len(args)

    def body(*refs):
        acc = jnp.dot(refs[0][...], refs[1][...], preferred_element_type=F32)
        nxt = 2
        if bias is not None:
            acc = acc + refs[nxt][...]
            nxt += 1
        if resid is not None:
            acc = refs[nxt][...] + acc
        refs[n_in][...] = acc.astype(refs[n_in].dtype)
        if normed:
            refs[n_in + 1][...] = _rms(acc, refs[n_in - 1][...]).astype(BF)

    out_spec = pl.BlockSpec((tm, tn), lambda i, p, j: (i, p * nj + j))
    kw = dict(name="mm_nn", grid=(m // tm, p_n, nj), in_specs=in_specs,
              out_specs=[out_spec, out_spec] if normed else out_spec,
              out_shape=[SDS((m, n), out_dtype), SDS((m, n), BF)] if normed else SDS((m, p_n * n), out_dtype))
    if gather:
        (out,), gathered = _call_gathering(body, gather, args, **kw)
        return out, gathered
    return _call(body, compiler_params=_params("parallel", "parallel", "parallel"), **kw)(*args)


def _mm_nt(dy, w, l, out_dtype=F32):
    m, n_all = dy.shape
    _, p_n, r, n = w.shape
    assert n_all == p_n * n
    tm = min(512, m)

    def body(dy_ref, w_ref, o_ref):
        acc = lax.dot_general(dy_ref[:, 0:n], w_ref[0], NT_DIMS, preferred_element_type=F32)
        for p in range(1, p_n):
            acc = acc + lax.dot_general(dy_ref[:, p * n:(p + 1) * n], w_ref[p], NT_DIMS, preferred_element_type=F32)
        o_ref[...] = acc.astype(o_ref.dtype)

    return _call(
        body, name="mm_nt", grid=(m // tm,),
        in_specs=[pl.BlockSpec((tm, n_all), lambda i: (i, 0)),
                  pl.BlockSpec((None, p_n, r, n), lambda i: (l, 0, 0, 0))],
        out_specs=pl.BlockSpec((tm, r), lambda i: (i, 0)),
        out_shape=SDS((m, r), out_dtype),
        compiler_params=_params("parallel"))(dy, w)


def _mm_nt_rms_bwd(dy, w, l, x, g, dres):
    m, n_all = dy.shape
    _, p_n, r, n = w.shape
    assert n_all == p_n * n and x.shape == (m, r)
    tm = min(256, m)

    def body(dy_ref, w_ref, x_ref, g_ref, r_ref, dx_ref, dxb_ref, dg_ref):
        dh = lax.dot_general(dy_ref[:, 0:n], w_ref[0], NT_DIMS, preferred_element_type=F32)
        for p in range(1, p_n):
            dh = dh + lax.dot_general(dy_ref[:, p * n:(p + 1) * n], w_ref[p], NT_DIMS, preferred_element_type=F32)
        _, vjp = jax.vjp(_rms, x_ref[...], g_ref[...])
        dx, dg = vjp(dh)
        dx = dx + r_ref[...]
        dx_ref[...] = dx
        dxb_ref[...] = dx.astype(BF)

        @pl.when(pl.program_id(0) == 0)
        def _():
            dg_ref[...] = jnp.zeros_like(dg_ref)

        dg_ref[...] += dg

    row = pl.BlockSpec((tm, r), lambda i: (i, 0))
    vec = pl.BlockSpec((1, r), lambda i: (0, 0))
    return _call(
        body, name="mm_nt_rms_bwd", grid=(m // tm,),
        in_specs=[pl.BlockSpec((tm, n_all), lambda i: (i, 0)), pl.BlockSpec((None, p_n, r, n), lambda i: (l, 0, 0, 0)),
                  row, vec, row],
        out_specs=[row, row, vec], out_shape=[SDS((m, r), F32), SDS((m, r), BF), SDS((1, r), F32)],
        compiler_params=_params("arbitrary"))(dy, w, x, g, dres)


def _mm_tn(a, dy, p_n, n_layers, l, buf=None):
    m, k = a.shape
    n = dy.shape[1] // p_n
    tm = min(2048, m)
    tk = k if k <= 1024 else k // 2
    nm = m // tm

    def body(a_ref, dy_ref, *rest):
        o_ref, acc_ref = rest[-2], rest[-1]
        mi = pl.program_id(2)
        part = lax.dot_general(a_ref[...], dy_ref[...], TN_DIMS, preferred_element_type=F32)

        @pl.when(mi == 0)
        def _():
            acc_ref[...] = part

        @pl.when(mi > 0)
        def _():
            acc_ref[...] += part

        @pl.when(mi == nm - 1)
        def _():
            o_ref[...] = acc_ref[...].astype(o_ref.dtype)

    in_specs = [pl.BlockSpec((tm, tk), lambda p, kk, mi: (mi, kk)),
                pl.BlockSpec((tm, n), lambda p, kk, mi: (mi, p))]
    args = [a, dy]
    aliases = {}
    if buf is not None:
        in_specs.append(pl.BlockSpec(memory_space=pl.ANY))
        args.append(buf)
        aliases = {2: 0}
    return _call(
        body, name="mm_tn", grid=(p_n, k // tk, nm), in_specs=in_specs,
        out_specs=pl.BlockSpec((None, None, tk, n), lambda p, kk, mi: (l, p, kk, 0)),
        out_shape=SDS((n_layers, p_n, k, n), BF), scratch_shapes=[pltpu.VMEM((tk, n), F32)],
        input_output_aliases=aliases,
        compiler_params=_params("parallel", "parallel", "arbitrary"))(*args)


def _loss_grad(y, tgt):
    t, d = y.shape
    tm = min(512, t)

    def body(y_ref, t_ref, l_ref, d_ref, db_ref):
        err = y_ref[...] - t_ref[...]
        dy = err * (1.0 / d)
        d_ref[...] = dy
        db_ref[...] = dy.astype(BF)
        part = 0.5 * jnp.sum(jnp.sum(err * err, axis=1, keepdims=True) * (1.0 / d), axis=0, keepdims=True)

        @pl.when(pl.program_id(0) == 0)
        def _():
            l_ref[...] = jnp.zeros_like(l_ref)

        l_ref[...] += jnp.broadcast_to(part, l_ref.shape)

    row = pl.BlockSpec((tm, d), lambda i: (i, 0))
    return _call(
        body, name="loss_grad", grid=(t // tm,), in_specs=[row, row],
        out_specs=[pl.BlockSpec((1, LANES), lambda i: (0, 0)), row, row],
        out_shape=[SDS((1, LANES), F32), SDS((t, d), F32), SDS((t, d), BF)],
        compiler_params=_params("arbitrary"))(y, tgt)


def _prev_halo(tr, halo, col):
    return lambda i: (jnp.maximum(i * (tr // halo) - 1, 0), col)


def _next_halo(tr, halo, n_rows, col):
    return lambda i: (jnp.minimum((i + 1) * (tr // halo), n_rows // halo - 1), col)


def _shifted_back(x):
    return pltpu.roll(x, 1, 0), pltpu.roll(x, 2, 0)


def _conv3(x, w_ref, b_ref, col):
    x1, x2 = _shifted_back(x)
    return b_ref[:, col] + w_ref[pl.ds(0, 1), col] * x2 + w_ref[pl.ds(1, 1), col] * x1 + w_ref[pl.ds(2, 1), col] * x


def _ffn_mid_fwd(up, w_dw, b_dw, gather=None):
    t, f2 = up.shape
    f = f2 // 2
    tr = min(256, t)
    h = FFN_HALO

    def body(g_ref, gp_ref, v_ref, w_ref, b_ref, o_ref):
        first_tile = pl.program_id(0) == 0

        def strip(c, carry):
            col = pl.ds(pl.multiple_of(c * LANES, LANES), LANES)
            x = jnp.concatenate([jnp.where(first_tile, 0.0, gp_ref[:, col].astype(F32)), g_ref[:, col].astype(F32)], axis=0)
            gc = _conv3(x, w_ref, b_ref, col)[h:]
            o_ref[:, col] = (gc * jax.nn.sigmoid(gc) * v_ref[:, col].astype(F32)).astype(o_ref.dtype)
            return carry

        lax.fori_loop(0, f // LANES, strip, 0)

    kw = dict(name="ffn_mid_fwd", grid=(t // tr,),
              in_specs=[pl.BlockSpec((tr, f), lambda i: (i, 0)), pl.BlockSpec((h, f), _prev_halo(tr, h, 0)),
                        pl.BlockSpec((tr, f), lambda i: (i, 1)),
                        pl.BlockSpec((8, f), lambda i: (0, 0)), pl.BlockSpec((1, f), lambda i: (0, 0))],
              out_specs=pl.BlockSpec((tr, f), lambda i: (i, 0)), out_shape=SDS((t, f), BF))
    args = (up, up, up, w_dw, b_dw)
    if gather:
        (out,), gathered = _call_gathering(body, gather, args, **kw)
        return out, gathered
    return _call(body, compiler_params=_params("parallel"), **kw)(*args)


def _ffn_mid_bwd(up, da, w_dw, b_dw):
    t, f2 = up.shape
    f = f2 // 2
    tr = min(256, t)
    h = FFN_HALO
    n_tiles = t // tr

    def body(g_ref, gp_ref, gn_ref, v_ref, vn_ref, da_ref, dan_ref, w_ref, b_ref, dup_ref, dw_ref, db_ref):
        i = pl.program_id(0)
        last = i == n_tiles - 1
        n = tr + h

        @pl.when(i == 0)
        def _():
            dw_ref[...] = jnp.zeros_like(dw_ref)
            db_ref[...] = jnp.zeros_like(db_ref)

        def rows(tile_ref, next_ref, col):
            return jnp.concatenate([tile_ref[:, col].astype(F32), next_ref[:, col].astype(F32)], axis=0)

        def strip(c, carry):
            col = pl.ds(pl.multiple_of(c * LANES, LANES), LANES)
            x = jnp.concatenate([jnp.where(i == 0, 0.0, gp_ref[:, col].astype(F32)), rows(g_ref, gn_ref, col)], axis=0)
            x1, x2 = _shifted_back(x)
            w0, w1, w2 = (w_ref[pl.ds(k, 1), col] for k in range(FFN_K))
            gc = (b_ref[:, col] + w0 * x2 + w1 * x1 + w2 * x)[h:]
            dav = rows(da_ref, dan_ref, col)
            sg = jax.nn.sigmoid(gc)
            silu = gc * sg
            dup_ref[:, pl.ds(pl.multiple_of(f + c * LANES, LANES), LANES)] = (dav * silu)[:tr].astype(dup_ref.dtype)
            dgc = dav * rows(v_ref, vn_ref, col) * (sg + silu * (1.0 - sg))
            dgc = jnp.concatenate([dgc[:tr], jnp.where(last, 0.0, dgc[tr:])], axis=0)
            d1, d2 = pltpu.roll(dgc, n - 1, 0), pltpu.roll(dgc, n - 2, 0)
            dup_ref[:, col] = (w2 * dgc + w1 * d1 + w0 * d2)[:tr].astype(dup_ref.dtype)
            dgt = dgc[:tr]
            for k, past in enumerate((x2, x1, x)):
                dw_ref[pl.ds(k, 1), col] += jnp.sum(past[h:h + tr] * dgt, axis=0, keepdims=True)
            db_ref[:, col] += jnp.sum(dgt, axis=0, keepdims=True)
            return carry

        lax.fori_loop(0, f // LANES, strip, 0)

    tile = lambda col: pl.BlockSpec((tr, f), lambda i: (i, col))
    nxt = lambda col: pl.BlockSpec((h, f), _next_halo(tr, h, t, col))
    return _call(
        body, name="ffn_mid_bwd", grid=(n_tiles,),
        in_specs=[tile(0), pl.BlockSpec((h, f), _prev_halo(tr, h, 0)), nxt(0), tile(1), nxt(1), tile(0), nxt(0),
                  pl.BlockSpec((8, f), lambda i: (0, 0)), pl.BlockSpec((1, f), lambda i: (0, 0))],
        out_specs=[pl.BlockSpec((tr, f2), lambda i: (i, 0)), pl.BlockSpec((8, f), lambda i: (0, 0)),
                   pl.BlockSpec((1, f), lambda i: (0, 0))],
        out_shape=[SDS((t, f2), BF), SDS((8, f), F32), SDS((1, f), F32)],
        compiler_params=_params("arbitrary"))(up, up, up, up, up, da, da, w_dw, b_dw)


def _ln_silu(yc, g, b):
    mu = jnp.mean(yc, axis=-1, keepdims=True)
    xc = yc - mu
    y = xc * lax.rsqrt(jnp.mean(xc * xc, axis=-1, keepdims=True) + NORM_EPS)
    return jax.nn.silu(y * g + b)


SUBLANES = 8
CONV_PAD = 24
SHIFT_CHUNK = 40
TAP_ROWS = 64


def _glu(a, g):
    return a.astype(F32) * jax.nn.sigmoid(g.astype(F32))


def _glu_strip(ygs_ref, first_tile, a_ref, ap_ref, g_ref, gp_ref, col, h, tr):
    ygs_ref[pl.ds(0, h), :] = jnp.where(first_tile, 0.0, _glu(ap_ref[:, col], gp_ref[:, col]))
    ygs_ref[pl.ds(h, tr), :] = _glu(a_ref[:, col], g_ref[:, col])


def _shift_past(sh_ref, ygs_ref, h, n):
    first = h - CONV_PAD - SUBLANES
    for u0 in range(0, n + CONV_PAD, SHIFT_CHUNK):
        x = ygs_ref[pl.ds(first + u0, SHIFT_CHUNK + SUBLANES), :]
        for r in range(1, SUBLANES):
            sh_ref[r, pl.ds(u0, SHIFT_CHUNK), :] = pltpu.roll(x, r, 0)[SUBLANES:]


def _past_rows(sh_ref, ygs_ref, h, n, s, row0=0):
    a, r = divmod(s, SUBLANES)
    if r == 0:
        return ygs_ref[pl.ds(row0 + h - SUBLANES * a, n), :]
    return sh_ref[r, pl.ds(row0 + CONV_PAD - SUBLANES * a, n), :]


def _conf_mid_fwd(p1, w_dw, b_dw, ln_g, ln_b, gather=None):
    t, w2 = p1.shape
    w = w2 // 2
    tr = min(256, t)
    h = CONV_HALO
    rc = 32

    def body(a_ref, ap_ref, g_ref, gp_ref, w_ref, b_ref, lg_ref, lb_ref, o_ref, yc_ref, ygs_ref, sh_ref):
        first_tile = pl.program_id(0) == 0

        def strip(c, carry):
            col = pl.ds(pl.multiple_of(c * LANES, LANES), LANES)
            _glu_strip(ygs_ref, first_tile, a_ref, ap_ref, g_ref, gp_ref, col, h, tr)
            _shift_past(sh_ref, ygs_ref, h, tr)
            acc = jnp.broadcast_to(b_ref[:, col], (tr, LANES))
            for k in range(CONV_K):
                acc = acc + w_ref[pl.ds(k, 1), col] * _past_rows(sh_ref, ygs_ref, h, tr, CONV_K - 1 - k)
            yc_ref[:, col] = acc
            return carry

        lax.fori_loop(0, w // LANES, strip, 0)

        def rows(r, carry):
            rs = pl.ds(pl.multiple_of(r * rc, rc), rc)
            o_ref[rs, :] = _ln_silu(yc_ref[rs, :], lg_ref[...], lb_ref[...]).astype(o_ref.dtype)
            return carry

        lax.fori_loop(0, tr // rc, rows, 0, unroll=4)

    vec = pl.BlockSpec((1, w), lambda i: (0, 0))
    tile = pl.BlockSpec((tr, w), lambda i: (i, 0))
    kw = dict(name="conf_mid_fwd", grid=(t // tr,),
              in_specs=[tile, pl.BlockSpec((h, w), _prev_halo(tr, h, 0)),
                        pl.BlockSpec((tr, w), lambda i: (i, 1)), pl.BlockSpec((h, w), _prev_halo(tr, h, 1)),
                        pl.BlockSpec((32, w), lambda i: (0, 0)), vec, vec, vec],
              out_specs=[tile, tile], out_shape=[SDS((t, w), BF), SDS((t, w), F32)],
              scratch_shapes=[pltpu.VMEM((h + tr, LANES), F32), pltpu.VMEM((SUBLANES, tr + CONV_PAD, LANES), F32)])
    args = (p1, p1, p1, p1, w_dw, b_dw, ln_g, ln_b)
    if gather:
        return _call_gathering(body, gather, args, **kw)
    return _call(body, compiler_params=_params("parallel"), **kw)(*args)


def _conf_mid_bwd(p1, yc, dys, dy, w_dw, ln_g, ln_b, exchange=None):
    t, w2 = p1.shape
    w = w2 // 2
    tr = min(256, t)
    h = CONV_HALO
    rc = 32
    n_tiles = t // tr

    def body(a_ref, ap_ref, g_ref, gp_ref, yc_ref, ycn_ref, dys_ref, dysn_ref, dy_ref, w_ref, lg_ref, lb_ref,
             dp_ref, dw_ref, db_ref, dlg_ref, dlb_ref, db1_ref, db2_ref, dyc_ref, ygs_ref, sh_ref, shf_ref, dwacc_ref):
        i = pl.program_id(0)
        last = i == n_tiles - 1

        @pl.when(i == 0)
        def _():
            for ref in (dw_ref, db_ref, dlg_ref, dlb_ref, db1_ref, db2_ref):
                ref[...] = jnp.zeros_like(ref)

        def ln_rows(r, carry):
            rs = pl.ds(pl.multiple_of(r * rc, rc), rc)
            _, vjp = jax.vjp(_ln_silu, yc_ref[rs, :], lg_ref[...], lb_ref[...])
            dyc, dlg, dlb = vjp(dys_ref[rs, :].astype(F32))
            dyc_ref[rs, :] = dyc
            dlg_ref[...] += dlg
            dlb_ref[...] += dlb
            return carry

        lax.fori_loop(0, tr // rc, ln_rows, 0, unroll=8)
        _, vjp = jax.vjp(_ln_silu, ycn_ref[...], lg_ref[...], lb_ref[...])
        dyc_ref[pl.ds(tr, h), :] = jnp.where(last, 0.0, vjp(dysn_ref[...].astype(F32))[0])
        db2_ref[...] += jnp.sum(dy_ref[...], axis=0, keepdims=True)

        def back(c, carry):
            col = pl.ds(pl.multiple_of(c * LANES, LANES), LANES)
            gcol = pl.ds(pl.multiple_of(w + c * LANES, LANES), LANES)
            _glu_strip(ygs_ref, i == 0, a_ref, ap_ref, g_ref, gp_ref, col, h, tr)
            _shift_past(sh_ref, ygs_ref, h, tr)
            for u0 in range(0, tr + CONV_PAD, SHIFT_CHUNK):
                part = dyc_ref[pl.ds(u0, SHIFT_CHUNK + SUBLANES), col]
                for r in range(1, SUBLANES):
                    shf_ref[r, pl.ds(u0, SHIFT_CHUNK), :] = pltpu.roll(part, SHIFT_CHUNK + SUBLANES - r, 0)[:SHIFT_CHUNK]
            for r0 in range(0, tr, TAP_ROWS):
                rows = pl.ds(r0, TAP_ROWS)
                dyc = dyc_ref[rows, col]
                dyg = jnp.zeros((TAP_ROWS, LANES), F32)
                for k in range(CONV_K):
                    s = CONV_K - 1 - k
                    a, r = divmod(s, SUBLANES)
                    if r == 0:
                        future = dyc_ref[pl.ds(r0 + SUBLANES * a, TAP_ROWS), col]
                    else:
                        future = shf_ref[r, pl.ds(r0 + SUBLANES * a, TAP_ROWS), :]
                    dyg = dyg + w_ref[pl.ds(k, 1), col] * future
                    prod = _past_rows(sh_ref, ygs_ref, h, TAP_ROWS, s, r0) * dyc
                    part = prod[0:SUBLANES]
                    for q in range(1, TAP_ROWS // SUBLANES):
                        part = part + prod[q * SUBLANES:(q + 1) * SUBLANES]
                    if r0 == 0:
                        dwacc_ref[k] = part
                    else:
                        dwacc_ref[k] += part
                sg = jax.nn.sigmoid(g_ref[rows, col].astype(F32))
                da = dyg * sg
                dg = dyg * a_ref[rows, col].astype(F32) * sg * (1.0 - sg)
                dp_ref[rows, col] = da.astype(dp_ref.dtype)
                dp_ref[rows, gcol] = dg.astype(dp_ref.dtype)
                db_ref[:, col] += jnp.sum(dyc, axis=0, keepdims=True)
                db1_ref[:, col] += jnp.sum(da, axis=0, keepdims=True)
                db1_ref[:, gcol] += jnp.sum(dg, axis=0, keepdims=True)
            for k in range(CONV_K):
                dw_ref[pl.ds(k, 1), col] += jnp.sum(dwacc_ref[k], axis=0, keepdims=True)
            return carry

        lax.fori_loop(0, w // LANES, back, 0)

    tile = lambda col: pl.BlockSpec((tr, w), lambda i: (i, col))
    prv = lambda col: pl.BlockSpec((h, w), _prev_halo(tr, h, col))
    nxt = pl.BlockSpec((h, w), _next_halo(tr, h, t, 0))
    vec = pl.BlockSpec((1, w), lambda i: (0, 0))
    kw = dict(
        name="conf_mid_bwd", grid=(n_tiles,),
        in_specs=[tile(0), prv(0), tile(1), prv(1), tile(0), nxt, tile(0), nxt, tile(0),
                  pl.BlockSpec((32, w), lambda i: (0, 0)), vec, vec],
        out_specs=[pl.BlockSpec((tr, w2), lambda i: (i, 0)), pl.BlockSpec((32, w), lambda i: (0, 0)), vec, vec, vec,
                   pl.BlockSpec((1, w2), lambda i: (0, 0)), vec],
        out_shape=[SDS((t, w2), BF), SDS((32, w), F32), SDS((1, w), F32), SDS((1, w), F32), SDS((1, w), F32),
                   SDS((1, w2), F32), SDS((1, w), F32)],
        scratch_shapes=[pltpu.VMEM((tr + h, w), F32), pltpu.VMEM((h + tr, LANES), F32),
                        pltpu.VMEM((SUBLANES, tr + CONV_PAD, LANES), F32), pltpu.VMEM((SUBLANES, tr + CONV_PAD, LANES), F32),
                        pltpu.VMEM((32, SUBLANES, LANES), F32)])
    args = (p1, p1, p1, p1, yc, yc, dys, dys, dy, w_dw, ln_g, ln_b)
    if exchange is not None:
        return _call_hosting(body, exchange, args, **kw)
    return _call(body, compiler_params=_params("arbitrary"), **kw)(*args)


def _group_matrices():
    i = lax.broadcasted_iota(jnp.int32, (512, 512), 0)
    j = lax.broadcasted_iota(jnp.int32, (512, 512), 1)
    mean64 = jnp.where(i // HEAD_DIM == j // HEAD_DIM, 1.0 / HEAD_DIM, 0.0).astype(F32)
    fold64 = jnp.where(i % HEAD_DIM == j % HEAD_DIM, 1.0, 0.0).astype(F32)
    return mean64, fold64


def _split_dot(x, mat):
    hi = x.astype(BF)
    lo = (x - hi.astype(F32)).astype(BF)
    mb = mat.astype(BF)
    return jnp.dot(hi, mb, preferred_element_type=F32) + jnp.dot(lo, mb, preferred_element_type=F32)


@jax.custom_vjp
def _group_sum(x, mat):
    return _split_dot(x, mat)


_group_sum.defvjp(lambda x, mat: (_split_dot(x, mat), mat), lambda mat, ct: (_split_dot(ct, mat), jnp.zeros_like(mat)))


def _bf_dot_plain(a, b):
    return jnp.dot(a.astype(BF), b.astype(BF), preferred_element_type=F32)


@jax.custom_vjp
def _bf_dot(a, b):
    return _bf_dot_plain(a, b)


def _bf_dot_bwd(res, ct):
    a, b = res
    cb = ct.astype(BF)
    return (lax.dot_general(cb, b.astype(BF), NT_DIMS, preferred_element_type=F32),
            lax.dot_general(a.astype(BF), cb, TN_DIMS, preferred_element_type=F32))


_bf_dot.defvjp(lambda a, b: (_bf_dot_plain(a, b), (a, b)), _bf_dot_bwd)


def _prep_tile(proj, qg, kg, zg, ws, bexp, mean64, differentiated=False):
    sw = 512
    q, k, v, u, z = (proj[:, n * sw:(n + 1) * sw] for n in range(5))
    group_sum, dot = (_group_sum, _bf_dot) if differentiated else (_split_dot, _bf_dot_plain)

    def group_norm(x):
        return x * lax.rsqrt(group_sum(x * x, mean64) + NORM_EPS)

    qn = group_norm(q) * qg
    kn = group_norm(k) * kg
    zn = group_norm(_gelu(z)) * zg
    row = lax.broadcasted_iota(jnp.int32, (CHUNK, CHUNK), 0)
    col = lax.broadcasted_iota(jnp.int32, (CHUNK, CHUNK), 1)
    first = lax.broadcasted_iota(jnp.int32, (1, LANES), 1) < HEAD_DIM
    wm = [jnp.where(col <= row, ws[g], 0.0) for g in range(2 * (sw // LANES))]
    chunks = []
    for ci in range(proj.shape[0] // CHUNK):
        parts = []
        for pr in range(sw // LANES):
            zp = zn[ci * CHUNK:(ci + 1) * CHUNK, pr * LANES:(pr + 1) * LANES]
            parts.append(jnp.where(first, dot(wm[2 * pr], zp), dot(wm[2 * pr + 1], zp)))
        chunks.append(jnp.concatenate(parts, axis=1) + bexp)
    s = chunks[0] if len(chunks) == 1 else jnp.concatenate(chunks, axis=0)
    return qn, kn, v, _gelu(u) * s


def _mix_prep_fwd(proj, qg, kg, zg, w_s, l, bexp, mean64, gather=None):
    t = proj.shape[0]
    tr = PREP_CHUNKS * CHUNK

    def body(p_ref, qg_ref, kg_ref, zg_ref, ws_ref, be_ref, m_ref, qkv_ref, go_ref):
        qn, kn, v, go = _prep_tile(p_ref[...], qg_ref[...], kg_ref[...], zg_ref[...], ws_ref[...], be_ref[...], m_ref[...])
        qkv_ref[:, 0:512] = qn.astype(BF)
        qkv_ref[:, 512:1024] = kn.astype(BF)
        qkv_ref[:, 1024:1536] = v.astype(BF)
        go_ref[...] = go.astype(BF)

    vec = pl.BlockSpec((1, 512), lambda i: (0, 0))
    kw = dict(name="mix_prep_fwd", grid=(t // tr,),
              in_specs=[pl.BlockSpec((tr, 2560), lambda i: (i, 0)), vec, vec, vec,
                        pl.BlockSpec((None, 8, CHUNK, CHUNK), lambda i: (l, 0, 0, 0)),
                        pl.BlockSpec((CHUNK, 512), lambda i: (0, 0)), pl.BlockSpec((512, 512), lambda i: (0, 0))],
              out_specs=[pl.BlockSpec((tr, 1536), lambda i: (i, 0)), pl.BlockSpec((tr, 512), lambda i: (i, 0))],
              out_shape=[SDS((t, 1536), BF), SDS((t, 512), BF)])
    args = (proj, qg, kg, zg, w_s, bexp, mean64)
    if gather:
        return _call_gathering(body, gather, args, **kw)
    return _call(body, compiler_params=_params("parallel"), **kw)(*args)


def _mix_prep_bwd(proj, dq, dk, dv, dmix, qg, kg, zg, w_s, l, bexp, mean64, fold64):
    t = proj.shape[0]
    tr = PREP_CHUNKS * CHUNK
    n_tiles = t // tr

    def body(p_ref, dq_ref, dk_ref, dv_ref, dgo_ref, qg_ref, kg_ref, zg_ref, ws_ref, be_ref, m_ref, f_ref,
             dp_ref, dqg_ref, dkg_ref, dzg_ref, dws_ref, dbe_ref):
        i = pl.program_id(0)

        @pl.when(i == 0)
        def _():
            for ref in (dqg_ref, dkg_ref, dzg_ref, dws_ref, dbe_ref):
                ref[...] = jnp.zeros_like(ref)

        fn = functools.partial(_prep_tile, mean64=m_ref[...], differentiated=True)
        _, vjp = jax.vjp(fn, p_ref[...], qg_ref[...], kg_ref[...], zg_ref[...], ws_ref[...], be_ref[...])
        dp, dqg, dkg, dzg, dws, dbe = vjp((dq_ref[...], dk_ref[...], dv_ref[...], dgo_ref[...]))
        dp_ref[...] = dp.astype(BF)
        dqg_ref[pl.ds(0, 1), :] += dqg
        dkg_ref[pl.ds(0, 1), :] += dkg
        dzg_ref[pl.ds(0, 1), :] += dzg
        dws_ref[...] += dws
        dbe_ref[...] += dbe

        @pl.when(i == n_tiles - 1)
        def _():
            dqg_ref[...] = jnp.dot(dqg_ref[...], f_ref[...], precision=HI, preferred_element_type=F32)
            dkg_ref[...] = jnp.dot(dkg_ref[...], f_ref[...], precision=HI, preferred_element_type=F32)
            dbe_ref[...] = jnp.dot(dbe_ref[...], m_ref[...] * float(HEAD_DIM), precision=HI, preferred_element_type=F32)

    vec = pl.BlockSpec((1, 512), lambda i: (0, 0))
    acc = pl.BlockSpec((8, 512), lambda i: (0, 0))
    sq = pl.BlockSpec((512, 512), lambda i: (0, 0))
    row = pl.BlockSpec((tr, 512), lambda i: (i, 0))
    return _call(
        body, name="mix_prep_bwd", grid=(n_tiles,),
        in_specs=[pl.BlockSpec((tr, 2560), lambda i: (i, 0)), row, row, row, pl.BlockSpec((tr, 512), lambda i: (i, 1)),
                  vec, vec, vec, pl.BlockSpec((None, 8, CHUNK, CHUNK), lambda i: (l, 0, 0, 0)),
                  pl.BlockSpec((CHUNK, 512), lambda i: (0, 0)), sq, sq],
        out_specs=[pl.BlockSpec((tr, 2560), lambda i: (i, 0)), acc, acc, acc,
                   pl.BlockSpec((8, CHUNK, CHUNK), lambda i: (0, 0, 0)), pl.BlockSpec((CHUNK, 512), lambda i: (0, 0))],
        out_shape=[SDS((t, 2560), BF), SDS((8, 512), F32), SDS((8, 512), F32), SDS((8, 512), F32),
                   SDS((8, CHUNK, CHUNK), F32), SDS((CHUNK, 512), F32)],
        compiler_params=_params("arbitrary"))(proj, dq, dk, dv, dmix, qg, kg, zg, w_s, bexp, mean64, fold64)


def _sb_logs(qh, kb, valid):
    z = lax.dot_general(qh, kb, NT_DIMS, preferred_element_type=F32) * (HEAD_DIM ** -0.5)
    soft = jnp.log1p(jnp.exp(-jnp.abs(z)))
    lk_raw = -(jnp.maximum(z, 0.0) + soft)
    ls = -(jnp.maximum(-z, 0.0) + soft)
    return lk_raw, ls, jnp.where(valid, lk_raw, 0.0)


def _sb_weights(ls, run, tail, valid):
    return jnp.where(valid, jnp.exp(ls + run + tail), 0.0)


def _att_masks(b):
    row = lax.broadcasted_iota(jnp.int32, (b, b), 0)
    col = lax.broadcasted_iota(jnp.int32, (b, b), 1)
    first = lax.broadcasted_iota(jnp.int32, (1, LANES), 1) < HEAD_DIM
    return row, col, first


N_PAIRS = 4
ROW_SPLIT = 1


def _load_kv(qkv_hbm, k_scr, v_scr, sems, group, width):
    ck = pltpu.make_async_copy(qkv_hbm.at[:, pl.ds(pl.multiple_of(512 + group * width, LANES), width)], k_scr, sems.at[0])
    cv = pltpu.make_async_copy(qkv_hbm.at[:, pl.ds(pl.multiple_of(1024 + group * width, LANES), width)], v_scr, sems.at[1])
    ck.start()
    cv.start()
    ck.wait()
    cv.wait()


def _split_heads(ref, pair, first):
    x = ref[:, pair * LANES:(pair + 1) * LANES]
    zero = jnp.zeros_like(x)
    return jnp.where(first, x, zero), jnp.where(first, zero, x)


def _any_weight_left(runs):
    top = functools.reduce(jnp.maximum, runs)
    return jnp.max(jnp.exp(top)) > 0.0


def _attn_fwd(qkv, pairs_per_step=4, gather=None):
    t = qkv.shape[0]
    b = ATT_BLOCK
    nq = t // b
    width = pairs_per_step * LANES
    n_heads = 2 * pairs_per_step

    def body(q_ref, qkv_hbm, ob_ref, o32_ref, k_scr, v_scr, acc_ref, run_ref, sems):
        group, qi = pl.program_id(0), pl.program_id(1)

        @pl.when(qi == 0)
        def _():
            _load_kv(qkv_hbm, k_scr, v_scr, sems, group, width)

        row, col, first = _att_masks(b)
        qh = [x for pr in range(pairs_per_step) for x in _split_heads(q_ref, pr, first)]
        upper = jnp.where(row > col, 1.0, 0.0).astype(BF)
        acc_ref[...] = jnp.zeros_like(acc_ref)
        run_ref[...] = jnp.zeros_like(run_ref)
        heads = range(n_heads)

        def step(carry):
            j, _ = carry
            rows = pl.ds(pl.multiple_of(j * b, b), b)
            valid = jnp.logical_or(j != qi, col < row)
            lanes = [pl.ds((hh // 2) * LANES, LANES) for hh in heads]
            logs = [_sb_logs(qh[hh], k_scr[rows, lanes[hh]], valid) for hh in heads]
            tails = [_split_dot(logs[hh][2], upper) for hh in heads]
            for hh in heads:
                wgt = _sb_weights(logs[hh][1], run_ref[hh], tails[hh], valid)
                acc_ref[hh] += jnp.dot(wgt.astype(BF), v_scr[rows, lanes[hh]], preferred_element_type=F32)
            for hh in heads:
                run_ref[hh] += jnp.sum(logs[hh][2], axis=1, keepdims=True)
            return j - 1, _any_weight_left([run_ref[hh] for hh in heads])

        lax.while_loop(lambda c: jnp.logical_and(c[0] >= 0, c[1]), step, (qi, jnp.bool_(True)))
        for pr in range(pairs_per_step):
            out = jnp.where(first, acc_ref[2 * pr], acc_ref[2 * pr + 1])
            ob_ref[:, pr * LANES:(pr + 1) * LANES] = out.astype(BF)
            o32_ref[:, pr * LANES:(pr + 1) * LANES] = out

    blk = pl.BlockSpec((b, width), lambda g, qi: (qi, g))
    kw = dict(name="attn_fwd", grid=(N_PAIRS // pairs_per_step, nq),
              in_specs=[blk, pl.BlockSpec(memory_space=pl.ANY)], out_specs=[blk, blk],
              out_shape=[SDS((t, 512), BF), SDS((t, 512), F32)],
              scratch_shapes=[pltpu.VMEM((t, width), BF), pltpu.VMEM((t, width), BF),
                              pltpu.VMEM((n_heads, b, LANES), F32), pltpu.VMEM((n_heads, b, 1), F32),
                              pltpu.SemaphoreType.DMA((2,))])
    if gather:
        return _call_gathering(body, gather, (qkv, qkv), **kw)
    return _call(body, compiler_params=_params("arbitrary", "arbitrary"), **kw)(qkv, qkv)


def _attn_bwd(qkv, a32, dmix, pairs_per_step=2):
    t = qkv.shape[0]
    b = ATT_BLOCK
    bh = b // ROW_SPLIT
    nq = t // b
    width = pairs_per_step * LANES
    n_heads = 2 * pairs_per_step

    def body(q_ref, a_ref, da_ref, qkv_hbm, dq_ref, dk_hbm, dv_hbm,
             k_scr, v_scr, dk_scr, dv_scr, dqa_ref, run_ref, rung_ref, sems):
        group, qi = pl.program_id(0), pl.program_id(1)

        @pl.when(qi == 0)
        def _():
            _load_kv(qkv_hbm, k_scr, v_scr, sems, group, width)
            dk_scr[...] = jnp.zeros_like(dk_scr)
            dv_scr[...] = jnp.zeros_like(dv_scr)

        row, col, first = _att_masks(b)
        qh, dah, dtot = [], [], []
        for pr in range(pairs_per_step):
            qh += _split_heads(q_ref, pr, first)
            da = da_ref[:, pr * LANES:(pr + 1) * LANES]
            prod = da * a_ref[:, pr * LANES:(pr + 1) * LANES]
            dtot += [jnp.sum(jnp.where(first, prod, 0.0), axis=1, keepdims=True),
                     jnp.sum(jnp.where(first, 0.0, prod), axis=1, keepdims=True)]
            dah += [jnp.where(first, da, 0.0).astype(BF), jnp.where(first, 0.0, da).astype(BF)]
        upper = jnp.where(row > col, 1.0, 0.0).astype(BF)
        lower_incl = jnp.where(row >= col, 1.0, 0.0).astype(BF)
        dqa_ref[...] = jnp.zeros_like(dqa_ref)
        run_ref[...] = jnp.zeros_like(run_ref)
        rung_ref[...] = jnp.zeros_like(rung_ref)
        chains = [(hh, s) for hh in range(n_heads) for s in range(ROW_SPLIT)]
        ids = range(len(chains))
        part = lambda x, s: x[s * bh:(s + 1) * bh]
        row_h = lax.broadcasted_iota(jnp.int32, (bh, b), 0)
        col_h = lax.broadcasted_iota(jnp.int32, (bh, b), 1)
        causal = [col_h < row_h + s * bh for s in range(ROW_SPLIT)]
        qc = [part(qh[hh], s) for hh, s in chains]
        dac = [part(dah[hh], s) for hh, s in chains]
        dtc = [part(dtot[hh], s) for hh, s in chains]
        lanes = [pl.ds((hh // 2) * LANES, LANES) for hh, _ in chains]

        def step(carry):
            j, _ = carry
            rows = pl.ds(pl.multiple_of(j * b, b), b)
            valid = [jnp.logical_or(j != qi, causal[s]) for _, s in chains]
            logs = [_sb_logs(qc[c], k_scr[rows, lanes[c]], valid[c]) for c in ids]
            runs = [run_ref[c] for c in ids]
            new_runs = [runs[c] + jnp.sum(logs[c][2], axis=1, keepdims=True) for c in ids]
            alive = _any_weight_left(new_runs)
            dps = [lax.dot_general(dac[c], v_scr[rows, lanes[c]], NT_DIMS, preferred_element_type=F32) for c in ids]
            tails = [_split_dot(logs[c][2], upper) for c in ids]
            wgts = [_sb_weights(logs[c][1], runs[c], tails[c], valid[c]) for c in ids]
            gs = [wgts[c] * dps[c] for c in ids]
            g_froms = [_split_dot(gs[c], lower_incl) for c in ids]
            for c in ids:
                lk_raw, ls, _ = logs[c]
                dlk = jnp.where(valid[c], dtc[c] - rung_ref[c] - g_froms[c], 0.0)
                dz = ((gs[c] * jnp.exp(lk_raw) - dlk * jnp.exp(ls)) * (HEAD_DIM ** -0.5)).astype(BF)
                dqa_ref[c] += jnp.dot(dz, k_scr[rows, lanes[c]], preferred_element_type=F32)
                dk_scr[rows, lanes[c]] += lax.dot_general(dz, qc[c], TN_DIMS, preferred_element_type=F32)
                dv_scr[rows, lanes[c]] += lax.dot_general(wgts[c].astype(BF), dac[c], TN_DIMS, preferred_element_type=F32)
            for c in ids:
                rung_ref[c] += jnp.sum(gs[c], axis=1, keepdims=True)
                run_ref[c] = new_runs[c]
            return j - 1, alive

        lax.while_loop(lambda c: jnp.logical_and(c[0] >= 0, c[1]), step, (qi, jnp.bool_(True)))
        for pr in range(pairs_per_step):
            for s in range(ROW_SPLIT):
                c0 = 2 * pr * ROW_SPLIT + s
                dq_ref[pl.ds(s * bh, bh), pr * LANES:(pr + 1) * LANES] = jnp.where(first, dqa_ref[c0], dqa_ref[c0 + ROW_SPLIT])

        @pl.when(qi == nq - 1)
        def _():
            cols = pl.ds(pl.multiple_of(group * width, LANES), width)
            ck = pltpu.make_async_copy(dk_scr, dk_hbm.at[:, cols], sems.at[0])
            cv = pltpu.make_async_copy(dv_scr, dv_hbm.at[:, cols], sems.at[1])
            ck.start()
            cv.start()
            ck.wait()
            cv.wait()

    blk = pl.BlockSpec((b, width), lambda g, qi: (qi, g))
    anywhere = pl.BlockSpec(memory_space=pl.ANY)
    return _call(
        body, name="attn_bwd", grid=(N_PAIRS // pairs_per_step, nq),
        in_specs=[blk, blk, blk, anywhere], out_specs=[blk, anywhere, anywhere],
        out_shape=[SDS((t, 512), F32), SDS((t, 512), F32), SDS((t, 512), F32)],
        scratch_shapes=[pltpu.VMEM((t, width), BF), pltpu.VMEM((t, width), BF),
                        pltpu.VMEM((t, width), F32), pltpu.VMEM((t, width), F32),
                        pltpu.VMEM((n_heads * ROW_SPLIT, bh, LANES), F32), pltpu.VMEM((n_heads * ROW_SPLIT, bh, 1), F32),
                        pltpu.VMEM((n_heads * ROW_SPLIT, bh, 1), F32), pltpu.SemaphoreType.DMA((2,))],
        compiler_params=_params("arbitrary", "arbitrary"))(qkv, a32, dmix, qkv)


def _adamw(w, g, m, v):
    n, c = w.shape
    tr = min(256, n)
    assert n % tr == 0

    def body(w_ref, g_ref, m_ref, v_ref, d_ref, nm_ref, nv_ref):
        g = g_ref[...]
        m = ADAM_B1 * m_ref[...] + (1.0 - ADAM_B1) * g
        v = ADAM_B2 * v_ref[...] + (1.0 - ADAM_B2) * jnp.square(g)
        m_hat = m / (1.0 - ADAM_B1 ** ADAM_STEP)
        v_hat = v / (1.0 - ADAM_B2 ** ADAM_STEP)
        d_ref[...] = -ADAM_LR * (m_hat / (jnp.sqrt(v_hat) + ADAM_EPS) + ADAM_WD * w_ref[...])
        nm_ref[...] = m
        nv_ref[...] = v

    blk = pl.BlockSpec((tr, c), lambda i: (i, 0))
    return _call(
        body, name="adamw", grid=(n // tr,), in_specs=[blk] * 4, out_specs=[blk] * 3,
        out_shape=[SDS((n, c), F32)] * 3, compiler_params=_params("parallel"))(w, g, m, v)


def _mesh_pos():
    return lax.axis_index("x"), lax.axis_index("y"), lax.axis_index("c")


def _other_chips(x, y):
    return [(1 - x, y), (x, 1 - y), (1 - x, 1 - y)]


HBM_SPEC = pl.BlockSpec(memory_space=pltpu.HBM)


GATHER_COPIES = 6


def _gather_steps(s_ref, o_ref, send_sems, recv_sems, local_sems, slot):
    h = s_ref.shape[1] // 2
    x, y, c = _mesh_pos()
    sibling = (x, y, 1 - c)
    chips = _other_chips(x, y)
    base = GATHER_COPIES * slot

    def half(px, py, hc):
        return o_ref.at[:, 2 * px + py, pl.ds(hc * h, h), :]

    def copy(k, dst, to, src=None):
        return pltpu.make_async_remote_copy(
            src_ref=dst if src is None else src, dst_ref=dst, send_sem=send_sems.at[base + k],
            recv_sem=recv_sems.at[base + k], device_id=to, device_id_type=MESH)

    mine = pltpu.make_async_copy(s_ref, o_ref.at[:, 2 * x + y], local_sems.at[slot])
    first = [copy(j, half(x, y, c), (*chip, c), src=s_ref.at[:, pl.ds(c * h, h), :]) for j, chip in enumerate(chips)]
    passed = [copy(3 + j, half(*chip, c), sibling) for j, chip in enumerate(chips)]

    def start():
        mine.start()
        for cp in first:
            cp.start()

    def finish():
        for j, chip in enumerate(chips):
            copy(j, half(*chip, c), (x, y, c)).wait_recv()
            passed[j].start()
        for j, chip in enumerate(chips):
            copy(3 + j, half(*chip, 1 - c), (x, y, c)).wait_recv()
        for cp in first + passed:
            cp.wait_send()
        mine.wait()

    return start, finish


def _gather_scratch(n):
    return [pltpu.SemaphoreType.DMA((GATHER_COPIES * n,)), pltpu.SemaphoreType.DMA((GATHER_COPIES * n,)),
            pltpu.SemaphoreType.DMA((n,))]


def _gathered_shape(shard):
    n_l, r, c_w = shard.shape
    return SDS((n_l, N_CHIPS, r, c_w), shard.dtype)


def _all_gather(shard):
    def body(s_ref, o_ref, send_sems, recv_sems, local_sems):
        start, finish = _gather_steps(s_ref, o_ref, send_sems, recv_sems, local_sems, 0)
        start()
        finish()

    return _call(body, name="all_gather", in_specs=[HBM_SPEC], out_specs=HBM_SPEC, out_shape=_gathered_shape(shard),
                 scratch_shapes=_gather_scratch(1))(shard)


class _Exchange(NamedTuple):
    tag: str
    inputs: list
    out_shapes: list
    scratch: list
    make_steps: Callable


def _gather_exchange(shards):
    n = len(shards)

    def make_steps(s_refs, o_refs, sems):
        steps = [_gather_steps(s_refs[k], o_refs[k], *sems, k) for k in range(n)]
        return (lambda: [start() for start, _ in steps]), (lambda: [finish() for _, finish in steps])

    return _Exchange("gathering", list(shards), [_gathered_shape(s) for s in shards], _gather_scratch(n), make_steps)


def _call_hosting(body, exchange, args, *, name, grid, in_specs, out_specs, out_shape, scratch_shapes=()):
    out_specs = list(out_specs) if isinstance(out_specs, (list, tuple)) else [out_specs]
    out_shape = list(out_shape) if isinstance(out_shape, (list, tuple)) else [out_shape]
    n_in, n_out, n_scr = len(in_specs), len(out_specs), len(scratch_shapes)
    n_xi, n_xo, n_sem = len(exchange.inputs), len(exchange.out_shapes), len(exchange.scratch)

    def hosting_body(*refs):
        ins, x_ins = refs[:n_in], refs[n_in:n_in + n_xi]
        outs = refs[n_in + n_xi:n_in + n_xi + n_out]
        x_outs = refs[n_in + n_xi + n_out:n_in + n_xi + n_out + n_xo]
        scratch = refs[n_in + n_xi + n_out + n_xo:n_in + n_xi + n_out + n_xo + n_scr]
        start, finish = exchange.make_steps(x_ins, x_outs, refs[len(refs) - n_sem:])
        is_first = functools.reduce(jnp.logical_and, [pl.program_id(a) == 0 for a in range(len(grid))])
        is_last = functools.reduce(jnp.logical_and, [pl.program_id(a) == grid[a] - 1 for a in range(len(grid))])

        @pl.when(is_first)
        def _():
            start()

        body(*ins, *outs, *scratch)

        @pl.when(is_last)
        def _():
            finish()

    res = _call(
        hosting_body, name=name + "_" + exchange.tag, grid=grid, in_specs=list(in_specs) + [HBM_SPEC] * n_xi,
        out_specs=out_specs + [HBM_SPEC] * n_xo, out_shape=out_shape + list(exchange.out_shapes),
        scratch_shapes=list(scratch_shapes) + list(exchange.scratch),
        compiler_params=_params(*(["arbitrary"] * len(grid))))(*args, *exchange.inputs)
    return res[:n_out], res[n_out:]


def _call_gathering(body, shards, args, **kw):
    return _call_hosting(body, _gather_exchange(shards), args, **kw)


def _row_tile(h):
    assert h <= 512
    return h


def _rs_pair(gs):
    n = len(gs)

    def body(*refs):
        g_refs, a_refs, (send_sems, recv_sems) = refs[:n], refs[n:2 * n], refs[2 * n:]
        x, y, c = _mesh_pos()
        cps = []
        for k in range(n):
            h = g_refs[k].shape[2] // 2
            cps.append(pltpu.make_async_remote_copy(
                src_ref=g_refs[k].at[:, :, pl.ds((1 - c) * h, h), :], dst_ref=a_refs[k], send_sem=send_sems.at[k],
                recv_sem=recv_sems.at[k], device_id=(x, y, 1 - c), device_id_type=MESH))
        for cp in cps:
            cp.start()
        for cp in cps:
            cp.wait()

    out_shape = [SDS((g.shape[0], g.shape[1], g.shape[2] // 2, g.shape[3]), g.dtype) for g in gs]
    return _call(body, name="rs_pair", in_specs=[HBM_SPEC] * n, out_specs=[HBM_SPEC] * n, out_shape=out_shape,
                 scratch_shapes=[pltpu.SemaphoreType.DMA((n,)), pltpu.SemaphoreType.DMA((n,))])(*gs)


def _rs_pair_add(g, from_sibling, mid_dtype):
    n_l, n_p, r, c_w = g.shape
    h = r // 2
    tr = _row_tile(h)
    nt = h // tr
    c_arr = jnp.reshape(lax.axis_index("c"), (1,)).astype(jnp.int32)

    def body(c_ref, g_ref, a_ref, o_ref):
        o_ref[...] = (g_ref[...].astype(F32) + a_ref[...].astype(F32)).astype(o_ref.dtype)

    blk = (None, None, tr, c_w)
    return _call(
        body, name="rs_pair_add",
        grid_spec=pltpu.PrefetchScalarGridSpec(
            num_scalar_prefetch=1, grid=(n_l, n_p, nt),
            in_specs=[pl.BlockSpec(blk, lambda l, p, t, c_ref: (l, p, c_ref[0] * nt + t, 0)),
                      pl.BlockSpec(blk, lambda l, p, t, c_ref: (l, p, t, 0))],
            out_specs=pl.BlockSpec(blk, lambda l, p, t, c_ref: (l, p, t, 0))),
        out_shape=SDS((n_l, n_p, h, c_w), mid_dtype),
        compiler_params=_params("parallel", "parallel", "parallel"))(c_arr, g, from_sibling)


def _chips_exchange(pair_sums):
    n = len(pair_sums)

    def make_steps(s_refs, b_refs, sems):
        send_sems, recv_sems = sems
        x, y, c = _mesh_pos()
        cps = [pltpu.make_async_remote_copy(
            src_ref=s_refs[k].at[:, 2 * chip[0] + chip[1]], dst_ref=b_refs[k].at[j], send_sem=send_sems.at[3 * k + j],
            recv_sem=recv_sems.at[3 * k + j], device_id=(*chip, c), device_id_type=MESH)
            for k in range(n) for j, chip in enumerate(_other_chips(x, y))]
        return (lambda: [cp.start() for cp in cps]), (lambda: [cp.wait() for cp in cps])

    out_shapes = [SDS((3, s.shape[0], s.shape[2], s.shape[3]), s.dtype) for s in pair_sums]
    sems = [pltpu.SemaphoreType.DMA((3 * n,)), pltpu.SemaphoreType.DMA((3 * n,))]
    return _Exchange("scattering", list(pair_sums), out_shapes, sems, make_steps)


def _rs_chips(pair_sums):
    ex = _chips_exchange(pair_sums)
    n = len(pair_sums)

    def body(*refs):
        start, finish = ex.make_steps(refs[:n], refs[n:2 * n], refs[2 * n:])
        start()
        finish()

    return _call(body, name="rs_chips", in_specs=[HBM_SPEC] * n, out_specs=[HBM_SPEC] * n, out_shape=ex.out_shapes,
                 scratch_shapes=ex.scratch)(*pair_sums)


def _rs_chip_add(pair_sum, from_chips, n_layers, first_layer, buf=None):
    n_l, _, h, c_w = pair_sum.shape
    tr = _row_tile(h)
    nt = h // tr
    p_arr = jnp.reshape(2 * lax.axis_index("x") + lax.axis_index("y"), (1,)).astype(jnp.int32)
    c_arr = jnp.reshape(lax.axis_index("c"), (1,)).astype(jnp.int32)

    def body(p_ref, c_ref, s_ref, b_ref, *rest):
        acc = s_ref[...].astype(F32)
        for j in range(3):
            acc = acc + b_ref[j].astype(F32)
        rest[-1][...] = acc

    in_specs = [pl.BlockSpec((None, None, tr, c_w), lambda l, t, p_ref, c_ref: (l, p_ref[0], t, 0)),
                pl.BlockSpec((3, None, tr, c_w), lambda l, t, p_ref, c_ref: (0, l, t, 0))]
    args = [p_arr, c_arr, pair_sum, from_chips]
    aliases = {}
    if buf is not None:
        in_specs.append(pl.BlockSpec(memory_space=pl.ANY))
        args.append(buf)
        aliases = {4: 0}
    return _call(
        body, name="rs_chip_add",
        grid_spec=pltpu.PrefetchScalarGridSpec(
            num_scalar_prefetch=2, grid=(n_l, nt), in_specs=in_specs,
            out_specs=pl.BlockSpec((None, tr, c_w), lambda l, t, p_ref, c_ref: (first_layer + l, c_ref[0] * nt + t, 0))),
        out_shape=SDS((n_layers, 2 * h, c_w), F32), input_output_aliases=aliases,
        compiler_params=_params("parallel", "parallel"))(*args)


def _rs_swap(halves):
    n = len(halves)

    def body(*refs):
        outs, (send_sems, recv_sems) = refs[n:2 * n], refs[2 * n:]
        x, y, c = _mesh_pos()

        def copy(k, half):
            h = outs[k].shape[1] // 2
            mine = outs[k].at[:, pl.ds(c * h, h), :]
            return pltpu.make_async_remote_copy(
                src_ref=mine, dst_ref=mine if half == "mine" else outs[k].at[:, pl.ds((1 - c) * h, h), :],
                send_sem=send_sems.at[k], recv_sem=recv_sems.at[k], device_id=(x, y, 1 - c), device_id_type=MESH)

        for k in range(n):
            copy(k, "mine").start()
        for k in range(n):
            copy(k, "theirs").wait_send()
            copy(k, "theirs").wait_recv()

    return _call(body, name="rs_swap", in_specs=[HBM_SPEC] * n, out_specs=[HBM_SPEC] * n,
                 out_shape=[SDS(a.shape, F32) for a in halves], input_output_aliases={k: k for k in range(n)},
                 scratch_shapes=[pltpu.SemaphoreType.DMA((n,)), pltpu.SemaphoreType.DMA((n,))])(*halves)


def _pack(arrays, row_multiple):
    flat = jnp.concatenate([a.reshape(-1).astype(F32) for a in arrays])
    unit = row_multiple * LANES
    padded = -(-flat.shape[0] // unit) * unit
    return jnp.pad(flat, (0, padded - flat.shape[0])).reshape(padded // LANES, LANES)


def _unpack(packed, shapes):
    flat = packed.reshape(-1)
    out, pos = [], 0
    for s in shapes:
        size = 1
        for dim in s:
            size *= dim
        out.append(flat[pos:pos + size].reshape(s))
        pos += size
    return out


BIG_COL = ("sb_w_in", "cv_w_pw1", "ffn_w_up")
BIG_ROW = ("hyb_w_out", "cv_w_pw2", "ffn_w_down")
SMALL_SHARDED = ("cv_b_pw1", "cv_w_dw", "cv_b_dw", "cv_ln_g", "cv_ln_b", "cv_b_pw2", "ffn_w_dw")
SMALL_REPLICATED = ("mix_norm_g", "sb_q_norm_g", "sb_k_norm_g", "sg_z_norm_g", "sg_w_spatial", "sg_b_spatial",
                    "ffn_norm_g", "ffn_b_dw")
WEIGHTS = ("mix_norm_g", "sb_w_in", "sb_q_norm_g", "sb_k_norm_g", "sg_z_norm_g", "sg_w_spatial", "sg_b_spatial",
           "hyb_w_out", "cv_w_pw1", "cv_b_pw1", "cv_w_dw", "cv_b_dw", "cv_ln_g", "cv_ln_b", "cv_w_pw2", "cv_b_pw2",
           "ffn_norm_g", "ffn_w_up", "ffn_w_dw", "ffn_b_dw", "ffn_w_down")


def _pad_rows(a, rows):
    return jnp.pad(a, ((0, rows - a.shape[0]), (0, 0)))


def _step(x, tgt, w, m, v):
    n_layers = w["mix_norm_g"].shape[0]
    xi, yi, ci = _mesh_pos()
    chip = 2 * xi + yi

    assert n_layers == 4
    hosted_by = {("proj", 0): ["hyb_w_out"], ("prep", 0): [("ffn_w_up", 0)],
                 ("attn", 0): [("ffn_w_down", 0), "cv_w_pw1", "cv_w_pw2"],
                 ("up", 0): [("ffn_w_up", 1)], ("ffn_mid", 0): [("ffn_w_down", 1)],
                 ("conf_mid", 1): [("ffn_w_up", 2), ("ffn_w_down", 2)],
                 ("up", 1): [("ffn_w_up", 3)], ("ffn_mid", 1): [("ffn_w_down", 3), ("sb_w_in", 1)]}
    full = {}

    def shard_of(key):
        if isinstance(key, tuple):
            return w[key[0]][key[1]:key[1] + 1].astype(BF)
        return w[key].astype(BF)

    def keep(key, g4):
        if (key[0] if isinstance(key, tuple) else key) in BIG_ROW:
            g4 = g4.reshape(g4.shape[0], 1, g4.shape[1] * g4.shape[2], g4.shape[3])
        full[key] = g4

    def hosting(fn, point, *args, **kw):
        keys = hosted_by.get(point)
        if not keys:
            return fn(*args, **kw)
        out, gathered = fn(*args, gather=[shard_of(k) for k in keys], **kw)
        for key, g4 in zip(keys, gathered):
            keep(key, g4)
        return out

    keep(("sb_w_in", 0), _all_gather(shard_of(("sb_w_in", 0))))
    small_local = [w[name] for name in SMALL_SHARDED]
    gathered = _all_gather(_pack(small_local, 32)[None])[0]
    per_chip = [_unpack(gathered[p], [a.shape for a in small_local]) for p in range(N_CHIPS)]
    for k, name in enumerate(SMALL_SHARDED):
        full[name] = jnp.concatenate([per_chip[p][k] for p in range(N_CHIPS)], axis=-1)
    for name in SMALL_REPLICATED:
        full[name] = w[name]

    mean64, fold64 = _group_matrices()
    ffn_wdw = [_pad_rows(full["ffn_w_dw"][i], 8) for i in range(n_layers)]
    cv_wdw = [_pad_rows(full["cv_w_dw"][j], 32) for j in range(n_layers // 2)]
    row = lambda a: a.reshape(1, -1)

    saved = []
    cur = x
    h = _rms_fwd(cur, row(full["mix_norm_g"][0]))
    for i in range(n_layers):
        j = i // 2
        rec = {"x_in": cur, "h_mix": h}
        if i % 2 == 0:
            proj = hosting(_mm_nn, ("proj", i), h, full[("sb_w_in", j)], 0)
            qg = row(jnp.tile(full["sb_q_norm_g"][j], 512 // HEAD_DIM))
            kg = row(jnp.tile(full["sb_k_norm_g"][j], 512 // HEAD_DIM))
            zg = row(full["sg_z_norm_g"][j])
            bexp = jnp.repeat(full["sg_b_spatial"][j].T, HEAD_DIM, axis=1)
            qkv, gated = hosting(_mix_prep_fwd, ("prep", i), proj, qg, kg, zg, full["sg_w_spatial"], j, bexp, mean64)
            att_bf, att_32 = hosting(_attn_fwd, ("attn", i), qkv)
            mix = jnp.concatenate([att_bf, gated], axis=1)
            cur, h = _mm_nn(mix, full["hyb_w_out"], j, resid=cur, norm_g=row(full["ffn_norm_g"][i]))
            rec.update(proj=proj, qkv=qkv, att_32=att_32, mix=mix, qg=qg, kg=kg, zg=zg, bexp=bexp)
        else:
            p1 = _mm_nn(h, full["cv_w_pw1"], j, bias=row(full["cv_b_pw1"][j]), out_dtype=BF)
            ys, yc = hosting(_conf_mid_fwd, ("conf_mid", i), p1, cv_wdw[j], row(full["cv_b_dw"][j]),
                             row(full["cv_ln_g"][j]), row(full["cv_ln_b"][j]))
            cur, h = _mm_nn(ys, full["cv_w_pw2"], j, bias=row(full["cv_b_pw2"][j]), resid=cur,
                            norm_g=row(full["ffn_norm_g"][i]))
            rec.update(p1=p1, ys=ys, yc=yc)
        rec["x_mid"] = cur
        up = hosting(_mm_nn, ("up", i), h, full[("ffn_w_up", i)], 0, out_dtype=BF)
        act = hosting(_ffn_mid_fwd, ("ffn_mid", i), up, ffn_wdw[i], row(full["ffn_b_dw"][i]))
        rec.update(h_ffn=h, up=up, act=act)
        if i + 1 < n_layers:
            cur, h = _mm_nn(act, full[("ffn_w_down", i)], 0, resid=cur, norm_g=row(full["mix_norm_g"][i + 1]))
        else:
            cur = _mm_nn(act, full[("ffn_w_down", i)], 0, resid=cur)
        saved.append(rec)

    loss_vec, dy, dy_bf = _loss_grad(cur, tgt)
    loss = lax.psum(loss_vec[0, 0], ("x", "y", "c"))

    big_names = BIG_COL + BIG_ROW
    gbig = {}
    gsmall = {name: [None] * w[name].shape[0] for name in SMALL_SHARDED + SMALL_REPLICATED}

    def accumulate(name, layer, a, dy_, p_n):
        per_group = w[name].shape[0] // 2
        grp, slot = divmod(layer, per_group)
        gbig[(name, grp)] = _mm_tn(a, dy_, p_n, per_group, slot, gbig.get((name, grp)))

    def group_grads(grp):
        out = []
        for name in big_names:
            g4 = gbig[(name, grp)]
            if name in BIG_ROW:
                g4 = g4.reshape(g4.shape[0], N_CHIPS, g4.shape[2] // N_CHIPS, g4.shape[3])
            out.append(g4)
        return out

    def pair_sums_of(gs, mid_dtypes):
        return [_rs_pair_add(g, a, dt) for g, a, dt in zip(gs, _rs_pair(gs), mid_dtypes)]

    half_sums = {}
    late_pair_sums = None
    for i in reversed(range(n_layers)):
        j = i // 2
        rec = saved[i]
        dact = _mm_nt(dy_bf, full[("ffn_w_down", i)], 0, out_dtype=BF)
        accumulate("ffn_w_down", i, rec["act"], dy_bf, 1)
        dup, dwdw, dbdw = _ffn_mid_bwd(rec["up"], dact, ffn_wdw[i], row(full["ffn_b_dw"][i]))
        gsmall["ffn_w_dw"][i] = dwdw[:FFN_K]
        gsmall["ffn_b_dw"][i] = dbdw[0]
        accumulate("ffn_w_up", i, rec["h_ffn"], dup, N_CHIPS)
        dy, dy_bf, dg = _mm_nt_rms_bwd(dup, full[("ffn_w_up", i)], 0, rec["x_mid"], row(full["ffn_norm_g"][i]), dy)
        gsmall["ffn_norm_g"][i] = dg[0]
        if i % 2 == 0:
            dmix = _mm_nt(dy_bf, full["hyb_w_out"], j)
            accumulate("hyb_w_out", j, rec["mix"], dy_bf, 1)
            dq, dk, dv = _attn_bwd(rec["qkv"], rec["att_32"], dmix)
            dproj, dqg, dkg, dzg, dws, dbe = _mix_prep_bwd(
                rec["proj"], dq, dk, dv, dmix, rec["qg"], rec["kg"], rec["zg"], full["sg_w_spatial"], j, rec["bexp"],
                mean64, fold64)
            gsmall["sb_q_norm_g"][j] = dqg[0, :HEAD_DIM]
            gsmall["sb_k_norm_g"][j] = dkg[0, :HEAD_DIM]
            gsmall["sg_z_norm_g"][j] = dzg[0]
            gsmall["sg_w_spatial"][j] = dws
            gsmall["sg_b_spatial"][j] = dbe[:, ::HEAD_DIM].T
            dlast, w_first, l_first = dproj, full[("sb_w_in", j)], 0
            accumulate("sb_w_in", j, rec["h_mix"], dproj, N_CHIPS)
        else:
            dys = _mm_nt(dy_bf, full["cv_w_pw2"], j, out_dtype=BF)
            accumulate("cv_w_pw2", j, rec["ys"], dy_bf, 1)
            carried = _chips_exchange(late_pair_sums) if late_pair_sums is not None else None
            res = _conf_mid_bwd(rec["p1"], rec["yc"], dys, dy, cv_wdw[j], row(full["cv_ln_g"][j]),
                                row(full["cv_ln_b"][j]), exchange=carried)
            if carried is not None:
                res, from_chips = res
                for name, ps, fc in zip(big_names, late_pair_sums, from_chips):
                    n_all = w[name].shape[0]
                    half_sums[name] = _rs_chip_add(ps, fc, n_all, n_all // 2)
                late_pair_sums = None
            dp1, dwdw, dbdw, dlg, dlb, db1, db2 = res
            gsmall["cv_w_dw"][j] = dwdw[:CONV_K]
            gsmall["cv_b_dw"][j] = dbdw[0]
            gsmall["cv_ln_g"][j] = dlg[0]
            gsmall["cv_ln_b"][j] = dlb[0]
            gsmall["cv_b_pw1"][j] = db1[0]
            gsmall["cv_b_pw2"][j] = db2[0]
            dlast, w_first, l_first = dp1, full["cv_w_pw1"], j
            accumulate("cv_w_pw1", j, rec["h_mix"], dp1, N_CHIPS)
        dy, dy_bf, dg = _mm_nt_rms_bwd(dlast, w_first, l_first, rec["x_in"], row(full["mix_norm_g"][i]), dy)
        gsmall["mix_norm_g"][i] = dg[0]
        if i == n_layers // 2:
            late_pair_sums = pair_sums_of(group_grads(1), [BF] * len(big_names))

    small_names = SMALL_REPLICATED + SMALL_SHARDED
    small_full = [jnp.stack(gsmall[name]) for name in small_names]
    packed = _pack(small_full, 32 * N_CHIPS)
    rows_q = packed.shape[0] // N_CHIPS
    early = pair_sums_of(group_grads(0) + [packed.reshape(1, N_CHIPS, rows_q, LANES)], [BF] * len(big_names) + [F32])
    from_chips = _rs_chips(early)
    halves = [_rs_chip_add(ps, fc, w[name].shape[0], 0, half_sums[name])
              for name, ps, fc in zip(big_names, early, from_chips)]
    halves.append(_rs_chip_add(early[-1], from_chips[-1], 1, 0))
    swapped = _rs_swap(halves)
    grads = dict(zip(big_names, swapped))
    summed = _all_gather(swapped[-1]).reshape(-1, LANES)
    for name, gsum in zip(small_names, _unpack(summed, [a.shape for a in small_full])):
        if name in SMALL_SHARDED:
            n_loc = w[name].shape[-1]
            split = gsum.reshape(gsum.shape[:-1] + (N_CHIPS, n_loc))
            gsum = lax.dynamic_index_in_dim(split, chip, axis=split.ndim - 2, keepdims=False)
        grads[name] = gsum

    delta, new_m, new_v = {}, {}, {}
    for name in BIG_COL + BIG_ROW:
        shp = w[name].shape
        two_d = lambda a: a.reshape(shp[0] * shp[1], shp[2])
        d, nm, nv = _adamw(two_d(w[name]), two_d(grads[name]), two_d(m[name]), two_d(v[name]))
        delta[name], new_m[name], new_v[name] = d.reshape(shp), nm.reshape(shp), nv.reshape(shp)
    shapes = [w[name].shape for name in small_names]
    d, nm, nv = _adamw(*(_pack([src[name] for name in small_names], 256) for src in (w, grads, m, v)))
    for name, a, b_, c_ in zip(small_names, _unpack(d, shapes), _unpack(nm, shapes), _unpack(nv, shapes)):
        delta[name], new_m[name], new_v[name] = a, b_, c_

    return (loss, dy, *[grads[n] for n in WEIGHTS], *[delta[n] for n in WEIGHTS],
            *[new_m[n] for n in WEIGHTS], *[new_v[n] for n in WEIGHTS])


def kernel(x, mix_norm_g, sb_w_in, sb_q_norm_g, sb_k_norm_g, sg_z_norm_g, sg_w_spatial, sg_b_spatial, hyb_w_out, cv_w_pw1, cv_b_pw1, cv_w_dw, cv_b_dw, cv_ln_g, cv_ln_b, cv_w_pw2, cv_b_pw2, ffn_norm_g, ffn_w_up, ffn_w_dw, ffn_b_dw, ffn_w_down, loss_target, m_mix_norm_g, m_sb_w_in, m_sb_q_norm_g, m_sb_k_norm_g, m_sg_z_norm_g, m_sg_w_spatial, m_sg_b_spatial, m_hyb_w_out, m_cv_w_pw1, m_cv_b_pw1, m_cv_w_dw, m_cv_b_dw, m_cv_ln_g, m_cv_ln_b, m_cv_w_pw2, m_cv_b_pw2, m_ffn_norm_g, m_ffn_w_up, m_ffn_w_dw, m_ffn_b_dw, m_ffn_w_down, v_mix_norm_g, v_sb_w_in, v_sb_q_norm_g, v_sb_k_norm_g, v_sg_z_norm_g, v_sg_w_spatial, v_sg_b_spatial, v_hyb_w_out, v_cv_w_pw1, v_cv_b_pw1, v_cv_w_dw, v_cv_b_dw, v_cv_ln_g, v_cv_ln_b, v_cv_w_pw2, v_cv_b_pw2, v_ffn_norm_g, v_ffn_w_up, v_ffn_w_dw, v_ffn_b_dw, v_ffn_w_down):
    given = dict(locals())
    w = {n: given[n] for n in WEIGHTS}
    m = {n: given["m_" + n] for n in WEIGHTS}
    v = {n: given["v_" + n] for n in WEIGHTS}
    out = _step(x[0], loss_target[0], w, m, v)
    return (out[0], out[1][None], *out[2:])
```

```python
import functools
from typing import Callable, NamedTuple

import jax
import jax.numpy as jnp
from jax import lax
from jax.experimental import pallas as pl
from jax.experimental.pallas import tpu as pltpu

F32 = jnp.float32
BF = jnp.bfloat16
SDS = jax.ShapeDtypeStruct
HI = lax.Precision.HIGHEST
MESH = pl.DeviceIdType.MESH

NORM_EPS = 1e-6
HEAD_DIM = 64
ATT_BLOCK = 128
CHUNK = 128
PREP_CHUNKS = 4
CONV_K = 31
CONV_HALO = 32
FFN_K = 3
FFN_HALO = 16
LANES = 128
N_CHIPS = 4
VMEM_LIMIT_BYTES = 56 * 2**20

ADAM_LR = 0.001
ADAM_B1 = 0.9
ADAM_B2 = 0.999
ADAM_EPS = 1e-08
ADAM_WD = 0.01
ADAM_STEP = 10

NT_DIMS = (((1,), (1,)), ((), ()))
TN_DIMS = (((0,), (0,)), ((), ()))


def _call(body, **kw):
    return pl.pallas_call(body, **kw)


def _params(*sem):
    return pltpu.CompilerParams(dimension_semantics=sem, vmem_limit_bytes=VMEM_LIMIT_BYTES)


def _gelu(x):
    return 0.5 * x * (1.0 + lax.erf(x * 0.7071067811865476))


def _rms(x, g):
    y = x * lax.rsqrt(jnp.mean(x * x, axis=-1, keepdims=True) + NORM_EPS)
    return y * g


def _rms_fwd(x, g):
    t, d = x.shape
    tm = min(512, t)

    def body(x_ref, g_ref, o_ref):
        o_ref[...] = _rms(x_ref[...], g_ref[...]).astype(o_ref.dtype)

    return _call(
        body, name="rms_fwd", grid=(t // tm,),
        in_specs=[pl.BlockSpec((tm, d), lambda i: (i, 0)), pl.BlockSpec((1, d), lambda i: (0, 0))],
        out_specs=pl.BlockSpec((tm, d), lambda i: (i, 0)),
        out_shape=SDS((t, d), BF), compiler_params=_params("parallel"))(x, g)


def _mm_nn(a, w, l, bias=None, resid=None, out_dtype=F32, gather=None, norm_g=None):
    m, k = a.shape
    _, p_n, kw, n = w.shape
    assert k == kw
    normed = norm_g is not None
    assert not normed or (p_n == 1 and not gather)
    tm = min(512 if normed else 1024, m)
    tn = n if (normed or k * n * 2 <= 4 * 2**20) else n // 2
    nj = n // tn
    in_specs = [pl.BlockSpec((tm, k), lambda i, p, j: (i, 0)),
                pl.BlockSpec((None, None, k, tn), lambda i, p, j: (l, p, 0, j))]
    args = [a, w]
    if bias is not None:
        in_specs.append(pl.BlockSpec((1, tn), lambda i, p, j: (0, p * nj + j)))
        args.append(bias)
    if resid is not None:
        in_specs.append(pl.BlockSpec((tm, tn), lambda i, p, j: (i, p * nj + j)))
        args.append(resid)
    if normed:
        in_specs.append(pl.BlockSpec((1, n), lambda i, p, j: (0, 0)))
        args.append(norm_g)
    n_in = len(args)

    def body(*refs):
        acc = jnp.dot(refs[0][...], refs[1][...], preferred_element_type=F32)
        nxt = 2
        if bias is not None:
            acc = acc + refs[nxt][...]
            nxt += 1
        if resid is not None:
            acc = refs[nxt][...] + acc
        refs[n_in][...] = acc.astype(refs[n_in].dtype)
        if normed:
            refs[n_in + 1][...] = _rms(acc, refs[n_in - 1][...]).astype(BF)

    out_spec = pl.BlockSpec((tm, tn), lambda i, p, j: (i, p * nj + j))
    kw = dict(name="mm_nn", grid=(m // tm, p_n, nj), in_specs=in_specs,
              out_specs=[out_spec, out_spec] if normed else out_spec,
              out_shape=[SDS((m, n), out_dtype), SDS((m, n), BF)] if normed else SDS((m, p_n * n), out_dtype))
    if gather:
        (out,), gathered = _call_gathering(body, gather, args, **kw)
        return out, gathered
    return _call(body, compiler_params=_params("parallel", "parallel", "parallel"), **kw)(*args)


def _mm_nt(dy, w, l, out_dtype=F32):
    m, n_all = dy.shape
    _, p_n, r, n = w.shape
    assert n_all == p_n * n
    tm = min(512, m)

    def body(dy_ref, w_ref, o_ref):
        acc = lax.dot_general(dy_ref[:, 0:n], w_ref[0], NT_DIMS, preferred_element_type=F32)
        for p in range(1, p_n):
            acc = acc + lax.dot_general(dy_ref[:, p * n:(p + 1) * n], w_ref[p], NT_DIMS, preferred_element_type=F32)
        o_ref[...] = acc.astype(o_ref.dtype)

    return _call(
        body, name="mm_nt", grid=(m // tm,),
        in_specs=[pl.BlockSpec((tm, n_all), lambda i: (i, 0)),
                  pl.BlockSpec((None, p_n, r, n), lambda i: (l, 0, 0, 0))],
        out_specs=pl.BlockSpec((tm, r), lambda i: (i, 0)),
        out_shape=SDS((m, r), out_dtype),
        compiler_params=_params("parallel"))(dy, w)


def _mm_nt_rms_bwd(dy, w, l, x, g, dres):
    m, n_all = dy.shape
    _, p_n, r, n = w.shape
    assert n_all == p_n * n and x.shape == (m, r)
    tm = min(256, m)

    def body(dy_ref, w_ref, x_ref, g_ref, r_ref, dx_ref, dxb_ref, dg_ref):
        dh = lax.dot_general(dy_ref[:, 0:n], w_ref[0], NT_DIMS, preferred_element_type=F32)
        for p in range(1, p_n):
            dh = dh + lax.dot_general(dy_ref[:, p * n:(p + 1) * n], w_ref[p], NT_DIMS, preferred_element_type=F32)
        _, vjp = jax.vjp(_rms, x_ref[...], g_ref[...])
        dx, dg = vjp(dh)
        dx = dx + r_ref[...]
        dx_ref[...] = dx
        dxb_ref[...] = dx.astype(BF)

        @pl.when(pl.program_id(0) == 0)
        def _():
            dg_ref[...] = jnp.zeros_like(dg_ref)

        dg_ref[...] += dg

    row = pl.BlockSpec((tm, r), lambda i: (i, 0))
    vec = pl.BlockSpec((1, r), lambda i: (0, 0))
    return _call(
        body, name="mm_nt_rms_bwd", grid=(m // tm,),
        in_specs=[pl.BlockSpec((tm, n_all), lambda i: (i, 0)), pl.BlockSpec((None, p_n, r, n), lambda i: (l, 0, 0, 0)),
                  row, vec, row],
        out_specs=[row, row, vec], out_shape=[SDS((m, r), F32), SDS((m, r), BF), SDS((1, r), F32)],
        compiler_params=_params("arbitrary"))(dy, w, x, g, dres)


def _mm_tn(a, dy, p_n, n_layers, l, buf=None):
    m, k = a.shape
    n = dy.shape[1] // p_n
    tm = min(2048, m)
    tk = k if k <= 1024 else k // 2
    nm = m // tm

    def body(a_ref, dy_ref, *rest):
        o_ref, acc_ref = rest[-2], rest[-1]
        mi = pl.program_id(2)
        part = lax.dot_general(a_ref[...], dy_ref[...], TN_DIMS, preferred_element_type=F32)

        @pl.when(mi == 0)
        def _():
            acc_ref[...] = part

        @pl.when(mi > 0)
        def _():
            acc_ref[...] += part

        @pl.when(mi == nm - 1)
        def _():
            o_ref[...] = acc_ref[...].astype(o_ref.dtype)

    in_specs = [pl.BlockSpec((tm, tk), lambda p, kk, mi: (mi, kk)),
                pl.BlockSpec((tm, n), lambda p, kk, mi: (mi, p))]
    args = [a, dy]
    aliases = {}
    if buf is not None:
        in_specs.append(pl.BlockSpec(memory_space=pl.ANY))
        args.append(buf)
        aliases = {2: 0}
    return _call(
        body, name="mm_tn", grid=(p_n, k // tk, nm), in_specs=in_specs,
        out_specs=pl.BlockSpec((None, None, tk, n), lambda p, kk, mi: (l, p, kk, 0)),
        out_shape=SDS((n_layers, p_n, k, n), BF), scratch_shapes=[pltpu.VMEM((tk, n), F32)],
        input_output_aliases=aliases,
        compiler_params=_params("parallel", "parallel", "arbitrary"))(*args)


def _loss_grad(y, tgt):
    t, d = y.shape
    tm = min(512, t)

    def body(y_ref, t_ref, l_ref, d_ref, db_ref):
        err = y_ref[...] - t_ref[...]
        dy = err * (1.0 / d)
        d_ref[...] = dy
        db_ref[...] = dy.astype(BF)
        part = 0.5 * jnp.sum(jnp.sum(err * err, axis=1, keepdims=True) * (1.0 / d), axis=0, keepdims=True)

        @pl.when(pl.program_id(0) == 0)
        def _():
            l_ref[...] = jnp.zeros_like(l_ref)

        l_ref[...] += jnp.broadcast_to(part, l_ref.shape)

    row = pl.BlockSpec((tm, d), lambda i: (i, 0))
    return _call(
        body, name="loss_grad", grid=(t // tm,), in_specs=[row, row],
        out_specs=[pl.BlockSpec((1, LANES), lambda i: (0, 0)), row, row],
        out_shape=[SDS((1, LANES), F32), SDS((t, d), F32), SDS((t, d), BF)],
        compiler_params=_params("arbitrary"))(y, tgt)


def _prev_halo(tr, halo, col):
    return lambda i: (jnp.maximum(i * (tr // halo) - 1, 0), col)


def _next_halo(tr, halo, n_rows, col):
    return lambda i: (jnp.minimum((i + 1) * (tr // halo), n_rows // halo - 1), col)


def _shifted_back(x):
    return pltpu.roll(x, 1, 0), pltpu.roll(x, 2, 0)


def _conv3(x, w_ref, b_ref, col):
    x1, x2 = _shifted_back(x)
    return b_ref[:, col] + w_ref[pl.ds(0, 1), col] * x2 + w_ref[pl.ds(1, 1), col] * x1 + w_ref[pl.ds(2, 1), col] * x


def _ffn_mid_fwd(up, w_dw, b_dw, gather=None):
    t, f2 = up.shape
    f = f2 // 2
    tr = min(256, t)
    h = FFN_HALO

    def body(g_ref, gp_ref, v_ref, w_ref, b_ref, o_ref):
        first_tile = pl.program_id(0) == 0

        def strip(c, carry):
            col = pl.ds(pl.multiple_of(c * LANES, LANES), LANES)
            x = jnp.concatenate([jnp.where(first_tile, 0.0, gp_ref[:, col].astype(F32)), g_ref[:, col].astype(F32)], axis=0)
            gc = _conv3(x, w_ref, b_ref, col)[h:]
            o_ref[:, col] = (gc * jax.nn.sigmoid(gc) * v_ref[:, col].astype(F32)).astype(o_ref.dtype)
            return carry

        lax.fori_loop(0, f // LANES, strip, 0)

    kw = dict(name="ffn_mid_fwd", grid=(t // tr,),
              in_specs=[pl.BlockSpec((tr, f), lambda i: (i, 0)), pl.BlockSpec((h, f), _prev_halo(tr, h, 0)),
                        pl.BlockSpec((tr, f), lambda i: (i, 1)),
                        pl.BlockSpec((8, f), lambda i: (0, 0)), pl.BlockSpec((1, f), lambda i: (0, 0))],
              out_specs=pl.BlockSpec((tr, f), lambda i: (i, 0)), out_shape=SDS((t, f), BF))
    args = (up, up, up, w_dw, b_dw)
    if gather:
        (out,), gathered = _call_gathering(body, gather, args, **kw)
        return out, gathered
    return _call(body, compiler_params=_params("parallel"), **kw)(*args)


def _ffn_mid_bwd(up, da, w_dw, b_dw, exchange=None):
    t, f2 = up.shape
    f = f2 // 2
    tr = min(256, t)
    h = FFN_HALO
    n_tiles = t // tr

    def body(g_ref, gp_ref, gn_ref, v_ref, vn_ref, da_ref, dan_ref, w_ref, b_ref, dup_ref, dw_ref, db_ref):
        i = pl.program_id(0)
        last = i == n_tiles - 1
        n = tr + h

        @pl.when(i == 0)
        def _():
            dw_ref[...] = jnp.zeros_like(dw_ref)
            db_ref[...] = jnp.zeros_like(db_ref)

        def rows(tile_ref, next_ref, col):
            return jnp.concatenate([tile_ref[:, col].astype(F32), next_ref[:, col].astype(F32)], axis=0)

        def strip(c, carry):
            col = pl.ds(pl.multiple_of(c * LANES, LANES), LANES)
            x = jnp.concatenate([jnp.where(i == 0, 0.0, gp_ref[:, col].astype(F32)), rows(g_ref, gn_ref, col)], axis=0)
            x1, x2 = _shifted_back(x)
            w0, w1, w2 = (w_ref[pl.ds(k, 1), col] for k in range(FFN_K))
            gc = (b_ref[:, col] + w0 * x2 + w1 * x1 + w2 * x)[h:]
            dav = rows(da_ref, dan_ref, col)
            sg = jax.nn.sigmoid(gc)
            silu = gc * sg
            dup_ref[:, pl.ds(pl.multiple_of(f + c * LANES, LANES), LANES)] = (dav * silu)[:tr].astype(dup_ref.dtype)
            dgc = dav * rows(v_ref, vn_ref, col) * (sg + silu * (1.0 - sg))
            dgc = jnp.concatenate([dgc[:tr], jnp.where(last, 0.0, dgc[tr:])], axis=0)
            d1, d2 = pltpu.roll(dgc, n - 1, 0), pltpu.roll(dgc, n - 2, 0)
            dup_ref[:, col] = (w2 * dgc + w1 * d1 + w0 * d2)[:tr].astype(dup_ref.dtype)
            dgt = dgc[:tr]
            for k, past in enumerate((x2, x1, x)):
                dw_ref[pl.ds(k, 1), col] += jnp.sum(past[h:h + tr] * dgt, axis=0, keepdims=True)
            db_ref[:, col] += jnp.sum(dgt, axis=0, keepdims=True)
            return carry

        lax.fori_loop(0, f // LANES, strip, 0)

    tile = lambda col: pl.BlockSpec((tr, f), lambda i: (i, col))
    nxt = lambda col: pl.BlockSpec((h, f), _next_halo(tr, h, t, col))
    kw = dict(
        name="ffn_mid_bwd", grid=(n_tiles,),
        in_specs=[tile(0), pl.BlockSpec((h, f), _prev_halo(tr, h, 0)), nxt(0), tile(1), nxt(1), tile(0), nxt(0),
                  pl.BlockSpec((8, f), lambda i: (0, 0)), pl.BlockSpec((1, f), lambda i: (0, 0))],
        out_specs=[pl.BlockSpec((tr, f2), lambda i: (i, 0)), pl.BlockSpec((8, f), lambda i: (0, 0)),
                   pl.BlockSpec((1, f), lambda i: (0, 0))],
        out_shape=[SDS((t, f2), BF), SDS((8, f), F32), SDS((1, f), F32)])
    args = (up, up, up, up, up, da, da, w_dw, b_dw)
    if exchange is not None:
        return _call_hosting(body, exchange, args, **kw)
    return _call(body, compiler_params=_params("arbitrary"), **kw)(*args)


def _ln_silu(yc, g, b):
    mu = jnp.mean(yc, axis=-1, keepdims=True)
    xc = yc - mu
    y = xc * lax.rsqrt(jnp.mean(xc * xc, axis=-1, keepdims=True) + NORM_EPS)
    return jax.nn.silu(y * g + b)


SUBLANES = 8
CONV_PAD = 24
SHIFT_CHUNK = 40
TAP_ROWS = 64


def _glu(a, g):
    return a.astype(F32) * jax.nn.sigmoid(g.astype(F32))


def _glu_strip(ygs_ref, first_tile, a_ref, ap_ref, g_ref, gp_ref, col, h, tr):
    ygs_ref[pl.ds(0, h), :] = jnp.where(first_tile, 0.0, _glu(ap_ref[:, col], gp_ref[:, col]))
    ygs_ref[pl.ds(h, tr), :] = _glu(a_ref[:, col], g_ref[:, col])


def _shift_past(sh_ref, ygs_ref, h, n):
    first = h - CONV_PAD - SUBLANES
    for u0 in range(0, n + CONV_PAD, SHIFT_CHUNK):
        x = ygs_ref[pl.ds(first + u0, SHIFT_CHUNK + SUBLANES), :]
        for r in range(1, SUBLANES):
            sh_ref[r, pl.ds(u0, SHIFT_CHUNK), :] = pltpu.roll(x, r, 0)[SUBLANES:]


def _past_rows(sh_ref, ygs_ref, h, n, s, row0=0):
    a, r = divmod(s, SUBLANES)
    if r == 0:
        return ygs_ref[pl.ds(row0 + h - SUBLANES * a, n), :]
    return sh_ref[r, pl.ds(row0 + CONV_PAD - SUBLANES * a, n), :]


def _conf_mid_fwd(p1, w_dw, b_dw, ln_g, ln_b, gather=None):
    t, w2 = p1.shape
    w = w2 // 2
    tr = min(256, t)
    h = CONV_HALO
    rc = 32

    def body(a_ref, ap_ref, g_ref, gp_ref, w_ref, b_ref, lg_ref, lb_ref, o_ref, yc_ref, ygs_ref, sh_ref):
        first_tile = pl.program_id(0) == 0

        def strip(c, carry):
            col = pl.ds(pl.multiple_of(c * LANES, LANES), LANES)
            _glu_strip(ygs_ref, first_tile, a_ref, ap_ref, g_ref, gp_ref, col, h, tr)
            _shift_past(sh_ref, ygs_ref, h, tr)
            acc = jnp.broadcast_to(b_ref[:, col], (tr, LANES))
            for k in range(CONV_K):
                acc = acc + w_ref[pl.ds(k, 1), col] * _past_rows(sh_ref, ygs_ref, h, tr, CONV_K - 1 - k)
            yc_ref[:, col] = acc
            return carry

        lax.fori_loop(0, w // LANES, strip, 0)

        def rows(r, carry):
            rs = pl.ds(pl.multiple_of(r * rc, rc), rc)
            o_ref[rs, :] = _ln_silu(yc_ref[rs, :], lg_ref[...], lb_ref[...]).astype(o_ref.dtype)
            return carry

        lax.fori_loop(0, tr // rc, rows, 0, unroll=4)

    vec = pl.BlockSpec((1, w), lambda i: (0, 0))
    tile = pl.BlockSpec((tr, w), lambda i: (i, 0))
    kw = dict(name="conf_mid_fwd", grid=(t // tr,),
              in_specs=[tile, pl.BlockSpec((h, w), _prev_halo(tr, h, 0)),
                        pl.BlockSpec((tr, w), lambda i: (i, 1)), pl.BlockSpec((h, w), _prev_halo(tr, h, 1)),
                        pl.BlockSpec((32, w), lambda i: (0, 0)), vec, vec, vec],
              out_specs=[tile, tile], out_shape=[SDS((t, w), BF), SDS((t, w), F32)],
              scratch_shapes=[pltpu.VMEM((h + tr, LANES), F32), pltpu.VMEM((SUBLANES, tr + CONV_PAD, LANES), F32)])
    args = (p1, p1, p1, p1, w_dw, b_dw, ln_g, ln_b)
    if gather:
        return _call_gathering(body, gather, args, **kw)
    return _call(body, compiler_params=_params("parallel"), **kw)(*args)


def _conf_mid_bwd(p1, yc, dys, dy, w_dw, ln_g, ln_b, exchange=None):
    t, w2 = p1.shape
    w = w2 // 2
    tr = min(256, t)
    h = CONV_HALO
    rc = 32
    n_tiles = t // tr

    def body(a_ref, ap_ref, g_ref, gp_ref, yc_ref, ycn_ref, dys_ref, dysn_ref, dy_ref, w_ref, lg_ref, lb_ref,
             dp_ref, dw_ref, db_ref, dlg_ref, dlb_ref, db1_ref, db2_ref, dyc_ref, ygs_ref, sh_ref, shf_ref, dwacc_ref):
        i = pl.program_id(0)
        last = i == n_tiles - 1

        @pl.when(i == 0)
        def _():
            for ref in (dw_ref, db_ref, dlg_ref, dlb_ref, db1_ref, db2_ref):
                ref[...] = jnp.zeros_like(ref)

        def ln_rows(r, carry):
            rs = pl.ds(pl.multiple_of(r * rc, rc), rc)
            _, vjp = jax.vjp(_ln_silu, yc_ref[rs, :], lg_ref[...], lb_ref[...])
            dyc, dlg, dlb = vjp(dys_ref[rs, :].astype(F32))
            dyc_ref[rs, :] = dyc
            dlg_ref[...] += dlg
            dlb_ref[...] += dlb
            return carry

        lax.fori_loop(0, tr // rc, ln_rows, 0, unroll=8)
        _, vjp = jax.vjp(_ln_silu, ycn_ref[...], lg_ref[...], lb_ref[...])
        dyc_ref[pl.ds(tr, h), :] = jnp.where(last, 0.0, vjp(dysn_ref[...].astype(F32))[0])
        db2_ref[...] += jnp.sum(dy_ref[...], axis=0, keepdims=True)

        def back(c, carry):
            col = pl.ds(pl.multiple_of(c * LANES, LANES), LANES)
            gcol = pl.ds(pl.multiple_of(w + c * LANES, LANES), LANES)
            _glu_strip(ygs_ref, i == 0, a_ref, ap_ref, g_ref, gp_ref, col, h, tr)
            _shift_past(sh_ref, ygs_ref, h, tr)
            for u0 in range(0, tr + CONV_PAD, SHIFT_CHUNK):
                part = dyc_ref[pl.ds(u0, SHIFT_CHUNK + SUBLANES), col]
                for r in range(1, SUBLANES):
                    shf_ref[r, pl.ds(u0, SHIFT_CHUNK), :] = pltpu.roll(part, SHIFT_CHUNK + SUBLANES - r, 0)[:SHIFT_CHUNK]
            for r0 in range(0, tr, TAP_ROWS):
                rows = pl.ds(r0, TAP_ROWS)
                dyc = dyc_ref[rows, col]
                dyg = jnp.zeros((TAP_ROWS, LANES), F32)
                for k in range(CONV_K):
                    s = CONV_K - 1 - k
                    a, r = divmod(s, SUBLANES)
                    if r == 0:
                        future = dyc_ref[pl.ds(r0 + SUBLANES * a, TAP_ROWS), col]
                    else:
                        future = shf_ref[r, pl.ds(r0 + SUBLANES * a, TAP_ROWS), :]
                    dyg = dyg + w_ref[pl.ds(k, 1), col] * future
                    prod = _past_rows(sh_ref, ygs_ref, h, TAP_ROWS, s, r0) * dyc
                    part = prod[0:SUBLANES]
                    for q in range(1, TAP_ROWS // SUBLANES):
                        part = part + prod[q * SUBLANES:(q + 1) * SUBLANES]
                    if r0 == 0:
                        dwacc_ref[k] = part
                    else:
                        dwacc_ref[k] += part
                sg = jax.nn.sigmoid(g_ref[rows, col].astype(F32))
                da = dyg * sg
                dg = dyg * a_ref[rows, col].astype(F32) * sg * (1.0 - sg)
                dp_ref[rows, col] = da.astype(dp_ref.dtype)
                dp_ref[rows, gcol] = dg.astype(dp_ref.dtype)
                db_ref[:, col] += jnp.sum(dyc, axis=0, keepdims=True)
                db1_ref[:, col] += jnp.sum(da, axis=0, keepdims=True)
                db1_ref[:, gcol] += jnp.sum(dg, axis=0, keepdims=True)
            for k in range(CONV_K):
                dw_ref[pl.ds(k, 1), col] += jnp.sum(dwacc_ref[k], axis=0, keepdims=True)
            return carry

        lax.fori_loop(0, w // LANES, back, 0)

    tile = lambda col: pl.BlockSpec((tr, w), lambda i: (i, col))
    prv = lambda col: pl.BlockSpec((h, w), _prev_halo(tr, h, col))
    nxt = pl.BlockSpec((h, w), _next_halo(tr, h, t, 0))
    vec = pl.BlockSpec((1, w), lambda i: (0, 0))
    kw = dict(
        name="conf_mid_bwd", grid=(n_tiles,),
        in_specs=[tile(0), prv(0), tile(1), prv(1), tile(0), nxt, tile(0), nxt, tile(0),
                  pl.BlockSpec((32, w), lambda i: (0, 0)), vec, vec],
        out_specs=[pl.BlockSpec((tr, w2), lambda i: (i, 0)), pl.BlockSpec((32, w), lambda i: (0, 0)), vec, vec, vec,
                   pl.BlockSpec((1, w2), lambda i: (0, 0)), vec],
        out_shape=[SDS((t, w2), BF), SDS((32, w), F32), SDS((1, w), F32), SDS((1, w), F32), SDS((1, w), F32),
                   SDS((1, w2), F32), SDS((1, w), F32)],
        scratch_shapes=[pltpu.VMEM((tr + h, w), F32), pltpu.VMEM((h + tr, LANES), F32),
                        pltpu.VMEM((SUBLANES, tr + CONV_PAD, LANES), F32), pltpu.VMEM((SUBLANES, tr + CONV_PAD, LANES), F32),
                        pltpu.VMEM((32, SUBLANES, LANES), F32)])
    args = (p1, p1, p1, p1, yc, yc, dys, dys, dy, w_dw, ln_g, ln_b)
    if exchange is not None:
        return _call_hosting(body, exchange, args, **kw)
    return _call(body, compiler_params=_params("arbitrary"), **kw)(*args)


def _group_matrices():
    i = lax.broadcasted_iota(jnp.int32, (512, 512), 0)
    j = lax.broadcasted_iota(jnp.int32, (512, 512), 1)
    mean64 = jnp.where(i // HEAD_DIM == j // HEAD_DIM, 1.0 / HEAD_DIM, 0.0).astype(F32)
    fold64 = jnp.where(i % HEAD_DIM == j % HEAD_DIM, 1.0, 0.0).astype(F32)
    return mean64, fold64


def _split_dot(x, mat):
    hi = x.astype(BF)
    lo = (x - hi.astype(F32)).astype(BF)
    mb = mat.astype(BF)
    return jnp.dot(hi, mb, preferred_element_type=F32) + jnp.dot(lo, mb, preferred_element_type=F32)


@jax.custom_vjp
def _group_sum(x, mat):
    return _split_dot(x, mat)


_group_sum.defvjp(lambda x, mat: (_split_dot(x, mat), mat), lambda mat, ct: (_split_dot(ct, mat), jnp.zeros_like(mat)))


def _bf_dot_plain(a, b):
    return jnp.dot(a.astype(BF), b.astype(BF), preferred_element_type=F32)


@jax.custom_vjp
def _bf_dot(a, b):
    return _bf_dot_plain(a, b)


def _bf_dot_bwd(res, ct):
    a, b = res
    cb = ct.astype(BF)
    return (lax.dot_general(cb, b.astype(BF), NT_DIMS, preferred_element_type=F32),
            lax.dot_general(a.astype(BF), cb, TN_DIMS, preferred_element_type=F32))


_bf_dot.defvjp(lambda a, b: (_bf_dot_plain(a, b), (a, b)), _bf_dot_bwd)


def _prep_tile(proj, qg, kg, zg, ws, bexp, mean64, differentiated=False):
    sw = 512
    q, k, v, u, z = (proj[:, n * sw:(n + 1) * sw] for n in range(5))
    group_sum, dot = (_group_sum, _bf_dot) if differentiated else (_split_dot, _bf_dot_plain)

    def group_norm(x):
        return x * lax.rsqrt(group_sum(x * x, mean64) + NORM_EPS)

    qn = group_norm(q) * qg
    kn = group_norm(k) * kg
    zn = group_norm(_gelu(z)) * zg
    row = lax.broadcasted_iota(jnp.int32, (CHUNK, CHUNK), 0)
    col = lax.broadcasted_iota(jnp.int32, (CHUNK, CHUNK), 1)
    first = lax.broadcasted_iota(jnp.int32, (1, LANES), 1) < HEAD_DIM
    wm = [jnp.where(col <= row, ws[g], 0.0) for g in range(2 * (sw // LANES))]
    chunks = []
    for ci in range(proj.shape[0] // CHUNK):
        parts = []
        for pr in range(sw // LANES):
            zp = zn[ci * CHUNK:(ci + 1) * CHUNK, pr * LANES:(pr + 1) * LANES]
            parts.append(jnp.where(first, dot(wm[2 * pr], zp), dot(wm[2 * pr + 1], zp)))
        chunks.append(jnp.concatenate(parts, axis=1) + bexp)
    s = chunks[0] if len(chunks) == 1 else jnp.concatenate(chunks, axis=0)
    return qn, kn, v, _gelu(u) * s


def _mix_prep_fwd(proj, qg, kg, zg, w_s, l, bexp, mean64, gather=None):
    t = proj.shape[0]
    tr = PREP_CHUNKS * CHUNK

    def body(p_ref, qg_ref, kg_ref, zg_ref, ws_ref, be_ref, m_ref, qkv_ref, go_ref):
        qn, kn, v, go = _prep_tile(p_ref[...], qg_ref[...], kg_ref[...], zg_ref[...], ws_ref[...], be_ref[...], m_ref[...])
        qkv_ref[:, 0:512] = qn.astype(BF)
        qkv_ref[:, 512:1024] = kn.astype(BF)
        qkv_ref[:, 1024:1536] = v.astype(BF)
        go_ref[...] = go.astype(BF)

    vec = pl.BlockSpec((1, 512), lambda i: (0, 0))
    kw = dict(name="mix_prep_fwd", grid=(t // tr,),
              in_specs=[pl.BlockSpec((tr, 2560), lambda i: (i, 0)), vec, vec, vec,
                        pl.BlockSpec((None, 8, CHUNK, CHUNK), lambda i: (l, 0, 0, 0)),
                        pl.BlockSpec((CHUNK, 512), lambda i: (0, 0)), pl.BlockSpec((512, 512), lambda i: (0, 0))],
              out_specs=[pl.BlockSpec((tr, 1536), lambda i: (i, 0)), pl.BlockSpec((tr, 512), lambda i: (i, 0))],
              out_shape=[SDS((t, 1536), BF), SDS((t, 512), BF)])
    args = (proj, qg, kg, zg, w_s, bexp, mean64)
    if gather:
        return _call_gathering(body, gather, args, **kw)
    return _call(body, compiler_params=_params("parallel"), **kw)(*args)


def _mix_prep_bwd(proj, dq, dk, dv, dmix, qg, kg, zg, w_s, l, bexp, mean64, fold64):
    t = proj.shape[0]
    tr = PREP_CHUNKS * CHUNK
    n_tiles = t // tr

    def body(p_ref, dq_ref, dk_ref, dv_ref, dgo_ref, qg_ref, kg_ref, zg_ref, ws_ref, be_ref, m_ref, f_ref,
             dp_ref, dqg_ref, dkg_ref, dzg_ref, dws_ref, dbe_ref):
        i = pl.program_id(0)

        @pl.when(i == 0)
        def _():
            for ref in (dqg_ref, dkg_ref, dzg_ref, dws_ref, dbe_ref):
                ref[...] = jnp.zeros_like(ref)

        fn = functools.partial(_prep_tile, mean64=m_ref[...], differentiated=True)
        _, vjp = jax.vjp(fn, p_ref[...], qg_ref[...], kg_ref[...], zg_ref[...], ws_ref[...], be_ref[...])
        dp, dqg, dkg, dzg, dws, dbe = vjp((dq_ref[...], dk_ref[...], dv_ref[...], dgo_ref[...]))
        dp_ref[...] = dp.astype(BF)
        dqg_ref[pl.ds(0, 1), :] += dqg
        dkg_ref[pl.ds(0, 1), :] += dkg
        dzg_ref[pl.ds(0, 1), :] += dzg
        dws_ref[...] += dws
        dbe_ref[...] += dbe

        @pl.when(i == n_tiles - 1)
        def _():
            dqg_ref[...] = jnp.dot(dqg_ref[...], f_ref[...], precision=HI, preferred_element_type=F32)
            dkg_ref[...] = jnp.dot(dkg_ref[...], f_ref[...], precision=HI, preferred_element_type=F32)
            dbe_ref[...] = jnp.dot(dbe_ref[...], m_ref[...] * float(HEAD_DIM), precision=HI, preferred_element_type=F32)

    vec = pl.BlockSpec((1, 512), lambda i: (0, 0))
    acc = pl.BlockSpec((8, 512), lambda i: (0, 0))
    sq = pl.BlockSpec((512, 512), lambda i: (0, 0))
    row = pl.BlockSpec((tr, 512), lambda i: (i, 0))
    return _call(
        body, name="mix_prep_bwd", grid=(n_tiles,),
        in_specs=[pl.BlockSpec((tr, 2560), lambda i: (i, 0)), row, row, row, pl.BlockSpec((tr, 512), lambda i: (i, 1)),
                  vec, vec, vec, pl.BlockSpec((None, 8, CHUNK, CHUNK), lambda i: (l, 0, 0, 0)),
                  pl.BlockSpec((CHUNK, 512), lambda i: (0, 0)), sq, sq],
        out_specs=[pl.BlockSpec((tr, 2560), lambda i: (i, 0)), acc, acc, acc,
                   pl.BlockSpec((8, CHUNK, CHUNK), lambda i: (0, 0, 0)), pl.BlockSpec((CHUNK, 512), lambda i: (0, 0))],
        out_shape=[SDS((t, 2560), BF), SDS((8, 512), F32), SDS((8, 512), F32), SDS((8, 512), F32),
                   SDS((8, CHUNK, CHUNK), F32), SDS((CHUNK, 512), F32)],
        compiler_params=_params("arbitrary"))(proj, dq, dk, dv, dmix, qg, kg, zg, w_s, bexp, mean64, fold64)


def _sb_logs(qh, kb, valid):
    z = lax.dot_general(qh, kb, NT_DIMS, preferred_element_type=F32) * (HEAD_DIM ** -0.5)
    soft = jnp.log1p(jnp.exp(-jnp.abs(z)))
    lk_raw = -(jnp.maximum(z, 0.0) + soft)
    ls = -(jnp.maximum(-z, 0.0) + soft)
    return lk_raw, ls, jnp.where(valid, lk_raw, 0.0)


def _sb_weights(ls, run, tail, valid):
    return jnp.where(valid, jnp.exp(ls + run + tail), 0.0)


def _att_masks(b):
    row = lax.broadcasted_iota(jnp.int32, (b, b), 0)
    col = lax.broadcasted_iota(jnp.int32, (b, b), 1)
    first = lax.broadcasted_iota(jnp.int32, (1, LANES), 1) < HEAD_DIM
    return row, col, first


N_PAIRS = 4
ROW_SPLIT = 1


def _load_kv(qkv_hbm, k_scr, v_scr, sems, group, width):
    ck = pltpu.make_async_copy(qkv_hbm.at[:, pl.ds(pl.multiple_of(512 + group * width, LANES), width)], k_scr, sems.at[0])
    cv = pltpu.make_async_copy(qkv_hbm.at[:, pl.ds(pl.multiple_of(1024 + group * width, LANES), width)], v_scr, sems.at[1])
    ck.start()
    cv.start()
    ck.wait()
    cv.wait()


def _split_heads(ref, pair, first):
    x = ref[:, pair * LANES:(pair + 1) * LANES]
    zero = jnp.zeros_like(x)
    return jnp.where(first, x, zero), jnp.where(first, zero, x)


def _any_weight_left(runs):
    top = functools.reduce(jnp.maximum, runs)
    return jnp.max(jnp.exp(top)) > 0.0


def _attn_fwd(qkv, pairs_per_step=4, gather=None):
    t = qkv.shape[0]
    b = ATT_BLOCK
    nq = t // b
    width = pairs_per_step * LANES
    n_heads = 2 * pairs_per_step

    def body(q_ref, qkv_hbm, ob_ref, o32_ref, k_scr, v_scr, acc_ref, run_ref, sems):
        group, qi = pl.program_id(0), pl.program_id(1)

        @pl.when(qi == 0)
        def _():
            _load_kv(qkv_hbm, k_scr, v_scr, sems, group, width)

        row, col, first = _att_masks(b)
        qh = [x for pr in range(pairs_per_step) for x in _split_heads(q_ref, pr, first)]
        upper = jnp.where(row > col, 1.0, 0.0).astype(BF)
        acc_ref[...] = jnp.zeros_like(acc_ref)
        run_ref[...] = jnp.zeros_like(run_ref)
        heads = range(n_heads)

        def step(carry):
            j, _ = carry
            rows = pl.ds(pl.multiple_of(j * b, b), b)
            valid = jnp.logical_or(j != qi, col < row)
            lanes = [pl.ds((hh // 2) * LANES, LANES) for hh in heads]
            logs = [_sb_logs(qh[hh], k_scr[rows, lanes[hh]], valid) for hh in heads]
            tails = [_split_dot(logs[hh][2], upper) for hh in heads]
            for hh in heads:
                wgt = _sb_weights(logs[hh][1], run_ref[hh], tails[hh], valid)
                acc_ref[hh] += jnp.dot(wgt.astype(BF), v_scr[rows, lanes[hh]], preferred_element_type=F32)
            for hh in heads:
                run_ref[hh] += jnp.sum(logs[hh][2], axis=1, keepdims=True)
            return j - 1, _any_weight_left([run_ref[hh] for hh in heads])

        lax.while_loop(lambda c: jnp.logical_and(c[0] >= 0, c[1]), step, (qi, jnp.bool_(True)))
        for pr in range(pairs_per_step):
            out = jnp.where(first, acc_ref[2 * pr], acc_ref[2 * pr + 1])
            ob_ref[:, pr * LANES:(pr + 1) * LANES] = out.astype(BF)
            o32_ref[:, pr * LANES:(pr + 1) * LANES] = out

    blk = pl.BlockSpec((b, width), lambda g, qi: (qi, g))
    kw = dict(name="attn_fwd", grid=(N_PAIRS // pairs_per_step, nq),
              in_specs=[blk, pl.BlockSpec(memory_space=pl.ANY)], out_specs=[blk, blk],
              out_shape=[SDS((t, 512), BF), SDS((t, 512), F32)],
              scratch_shapes=[pltpu.VMEM((t, width), BF), pltpu.VMEM((t, width), BF),
                              pltpu.VMEM((n_heads, b, LANES), F32), pltpu.VMEM((n_heads, b, 1), F32),
                              pltpu.SemaphoreType.DMA((2,))])
    if gather:
        return _call_gathering(body, gather, (qkv, qkv), **kw)
    return _call(body, compiler_params=_params("arbitrary", "arbitrary"), **kw)(qkv, qkv)


def _attn_bwd(qkv, a32, dmix, pairs_per_step=2):
    t = qkv.shape[0]
    b = ATT_BLOCK
    bh = b // ROW_SPLIT
    nq = t // b
    width = pairs_per_step * LANES
    n_heads = 2 * pairs_per_step

    def body(q_ref, a_ref, da_ref, qkv_hbm, dq_ref, dk_hbm, dv_hbm,
             k_scr, v_scr, dk_scr, dv_scr, dqa_ref, run_ref, rung_ref, sems):
        group, qi = pl.program_id(0), pl.program_id(1)

        @pl.when(qi == 0)
        def _():
            _load_kv(qkv_hbm, k_scr, v_scr, sems, group, width)
            dk_scr[...] = jnp.zeros_like(dk_scr)
            dv_scr[...] = jnp.zeros_like(dv_scr)

        row, col, first = _att_masks(b)
        qh, dah, dtot = [], [], []
        for pr in range(pairs_per_step):
            qh += _split_heads(q_ref, pr, first)
            da = da_ref[:, pr * LANES:(pr + 1) * LANES]
            prod = da * a_ref[:, pr * LANES:(pr + 1) * LANES]
            dtot += [jnp.sum(jnp.where(first, prod, 0.0), axis=1, keepdims=True),
                     jnp.sum(jnp.where(first, 0.0, prod), axis=1, keepdims=True)]
            dah += [jnp.where(first, da, 0.0).astype(BF), jnp.where(first, 0.0, da).astype(BF)]
        upper = jnp.where(row > col, 1.0, 0.0).astype(BF)
        lower_incl = jnp.where(row >= col, 1.0, 0.0).astype(BF)
        dqa_ref[...] = jnp.zeros_like(dqa_ref)
        run_ref[...] = jnp.zeros_like(run_ref)
        rung_ref[...] = jnp.zeros_like(rung_ref)
        chains = [(hh, s) for hh in range(n_heads) for s in range(ROW_SPLIT)]
        ids = range(len(chains))
        part = lambda x, s: x[s * bh:(s + 1) * bh]
        row_h = lax.broadcasted_iota(jnp.int32, (bh, b), 0)
        col_h = lax.broadcasted_iota(jnp.int32, (bh, b), 1)
        causal = [col_h < row_h + s * bh for s in range(ROW_SPLIT)]
        qc = [part(qh[hh], s) for hh, s in chains]
        dac = [part(dah[hh], s) for hh, s in chains]
        dtc = [part(dtot[hh], s) for hh, s in chains]
        lanes = [pl.ds((hh // 2) * LANES, LANES) for hh, _ in chains]

        def step(carry):
            j, _ = carry
            rows = pl.ds(pl.multiple_of(j * b, b), b)
            valid = [jnp.logical_or(j != qi, causal[s]) for _, s in chains]
            logs = [_sb_logs(qc[c], k_scr[rows, lanes[c]], valid[c]) for c in ids]
            runs = [run_ref[c] for c in ids]
            new_runs = [runs[c] + jnp.sum(logs[c][2], axis=1, keepdims=True) for c in ids]
            alive = _any_weight_left(new_runs)
            dps = [lax.dot_general(dac[c], v_scr[rows, lanes[c]], NT_DIMS, preferred_element_type=F32) for c in ids]
            tails = [_split_dot(logs[c][2], upper) for c in ids]
            wgts = [_sb_weights(logs[c][1], runs[c], tails[c], valid[c]) for c in ids]
            gs = [wgts[c] * dps[c] for c in ids]
            g_froms = [_split_dot(gs[c], lower_incl) for c in ids]
            for c in ids:
                lk_raw, ls, _ = logs[c]
                dlk = jnp.where(valid[c], dtc[c] - rung_ref[c] - g_froms[c], 0.0)
                dz = ((gs[c] * jnp.exp(lk_raw) - dlk * jnp.exp(ls)) * (HEAD_DIM ** -0.5)).astype(BF)
                dqa_ref[c] += jnp.dot(dz, k_scr[rows, lanes[c]], preferred_element_type=F32)
                dk_scr[rows, lanes[c]] += lax.dot_general(dz, qc[c], TN_DIMS, preferred_element_type=F32)
                dv_scr[rows, lanes[c]] += lax.dot_general(wgts[c].astype(BF), dac[c], TN_DIMS, preferred_element_type=F32)
            for c in ids:
                rung_ref[c] += jnp.sum(gs[c], axis=1, keepdims=True)
                run_ref[c] = new_runs[c]
            return j - 1, alive

        lax.while_loop(lambda c: jnp.logical_and(c[0] >= 0, c[1]), step, (qi, jnp.bool_(True)))
        for pr in range(pairs_per_step):
            for s in range(ROW_SPLIT):
                c0 = 2 * pr * ROW_SPLIT + s
                dq_ref[pl.ds(s * bh, bh), pr * LANES:(pr + 1) * LANES] = jnp.where(first, dqa_ref[c0], dqa_ref[c0 + ROW_SPLIT])

        @pl.when(qi == nq - 1)
        def _():
            cols = pl.ds(pl.multiple_of(group * width, LANES), width)
            ck = pltpu.make_async_copy(dk_scr, dk_hbm.at[:, cols], sems.at[0])
            cv = pltpu.make_async_copy(dv_scr, dv_hbm.at[:, cols], sems.at[1])
            ck.start()
            cv.start()
            ck.wait()
            cv.wait()

    blk = pl.BlockSpec((b, width), lambda g, qi: (qi, g))
    anywhere = pl.BlockSpec(memory_space=pl.ANY)
    return _call(
        body, name="attn_bwd", grid=(N_PAIRS // pairs_per_step, nq),
        in_specs=[blk, blk, blk, anywhere], out_specs=[blk, anywhere, anywhere],
        out_shape=[SDS((t, 512), F32), SDS((t, 512), F32), SDS((t, 512), F32)],
        scratch_shapes=[pltpu.VMEM((t, width), BF), pltpu.VMEM((t, width), BF),
                        pltpu.VMEM((t, width), F32), pltpu.VMEM((t, width), F32),
                        pltpu.VMEM((n_heads * ROW_SPLIT, bh, LANES), F32), pltpu.VMEM((n_heads * ROW_SPLIT, bh, 1), F32),
                        pltpu.VMEM((n_heads * ROW_SPLIT, bh, 1), F32), pltpu.SemaphoreType.DMA((2,))],
        compiler_params=_params("arbitrary", "arbitrary"))(qkv, a32, dmix, qkv)


def _adamw(w, g, m, v):
    n, c = w.shape
    tr = min(256, n)
    assert n % tr == 0

    def body(w_ref, g_ref, m_ref, v_ref, d_ref, nm_ref, nv_ref):
        g = g_ref[...]
        m = ADAM_B1 * m_ref[...] + (1.0 - ADAM_B1) * g
        v = ADAM_B2 * v_ref[...] + (1.0 - ADAM_B2) * jnp.square(g)
        m_hat = m / (1.0 - ADAM_B1 ** ADAM_STEP)
        v_hat = v / (1.0 - ADAM_B2 ** ADAM_STEP)
        d_ref[...] = -ADAM_LR * (m_hat / (jnp.sqrt(v_hat) + ADAM_EPS) + ADAM_WD * w_ref[...])
        nm_ref[...] = m
        nv_ref[...] = v

    blk = pl.BlockSpec((tr, c), lambda i: (i, 0))
    return _call(
        body, name="adamw", grid=(n // tr,), in_specs=[blk] * 4, out_specs=[blk] * 3,
        out_shape=[SDS((n, c), F32)] * 3, compiler_params=_params("parallel"))(w, g, m, v)


def _mesh_pos():
    return lax.axis_index("x"), lax.axis_index("y"), lax.axis_index("c")


def _other_chips(x, y):
    return [(1 - x, y), (x, 1 - y), (1 - x, 1 - y)]


HBM_SPEC = pl.BlockSpec(memory_space=pltpu.HBM)


GATHER_COPIES = 6


def _gather_steps(s_ref, o_ref, send_sems, recv_sems, local_sems, slot):
    h = s_ref.shape[1] // 2
    x, y, c = _mesh_pos()
    sibling = (x, y, 1 - c)
    chips = _other_chips(x, y)
    base = GATHER_COPIES * slot

    def half(px, py, hc):
        return o_ref.at[:, 2 * px + py, pl.ds(hc * h, h), :]

    def copy(k, dst, to, src=None):
        return pltpu.make_async_remote_copy(
            src_ref=dst if src is None else src, dst_ref=dst, send_sem=send_sems.at[base + k],
            recv_sem=recv_sems.at[base + k], device_id=to, device_id_type=MESH)

    mine = pltpu.make_async_copy(s_ref, o_ref.at[:, 2 * x + y], local_sems.at[slot])
    first = [copy(j, half(x, y, c), (*chip, c), src=s_ref.at[:, pl.ds(c * h, h), :]) for j, chip in enumerate(chips)]
    passed = [copy(3 + j, half(*chip, c), sibling) for j, chip in enumerate(chips)]

    def start():
        mine.start()
        for cp in first:
            cp.start()

    def finish():
        for j, chip in enumerate(chips):
            copy(j, half(*chip, c), (x, y, c)).wait_recv()
            passed[j].start()
        for j, chip in enumerate(chips):
            copy(3 + j, half(*chip, 1 - c), (x, y, c)).wait_recv()
        for cp in first + passed:
            cp.wait_send()
        mine.wait()

    return start, finish


def _gather_scratch(n):
    return [pltpu.SemaphoreType.DMA((GATHER_COPIES * n,)), pltpu.SemaphoreType.DMA((GATHER_COPIES * n,)),
            pltpu.SemaphoreType.DMA((n,))]


def _gathered_shape(shard):
    n_l, r, c_w = shard.shape
    return SDS((n_l, N_CHIPS, r, c_w), shard.dtype)


def _all_gather(shard):
    def body(s_ref, o_ref, send_sems, recv_sems, local_sems):
        start, finish = _gather_steps(s_ref, o_ref, send_sems, recv_sems, local_sems, 0)
        start()
        finish()

    return _call(body, name="all_gather", in_specs=[HBM_SPEC], out_specs=HBM_SPEC, out_shape=_gathered_shape(shard),
                 scratch_shapes=_gather_scratch(1))(shard)


class _Exchange(NamedTuple):
    tag: str
    inputs: list
    out_shapes: list
    scratch: list
    make_steps: Callable


def _gather_exchange(shards):
    n = len(shards)

    def make_steps(s_refs, o_refs, sems):
        steps = [_gather_steps(s_refs[k], o_refs[k], *sems, k) for k in range(n)]
        return (lambda: [start() for start, _ in steps]), (lambda: [finish() for _, finish in steps])

    return _Exchange("gathering", list(shards), [_gathered_shape(s) for s in shards], _gather_scratch(n), make_steps)


def _call_hosting(body, exchange, args, *, name, grid, in_specs, out_specs, out_shape, scratch_shapes=()):
    out_specs = list(out_specs) if isinstance(out_specs, (list, tuple)) else [out_specs]
    out_shape = list(out_shape) if isinstance(out_shape, (list, tuple)) else [out_shape]
    n_in, n_out, n_scr = len(in_specs), len(out_specs), len(scratch_shapes)
    n_xi, n_xo, n_sem = len(exchange.inputs), len(exchange.out_shapes), len(exchange.scratch)

    def hosting_body(*refs):
        ins, x_ins = refs[:n_in], refs[n_in:n_in + n_xi]
        outs = refs[n_in + n_xi:n_in + n_xi + n_out]
        x_outs = refs[n_in + n_xi + n_out:n_in + n_xi + n_out + n_xo]
        scratch = refs[n_in + n_xi + n_out + n_xo:n_in + n_xi + n_out + n_xo + n_scr]
        start, finish = exchange.make_steps(x_ins, x_outs, refs[len(refs) - n_sem:])
        is_first = functools.reduce(jnp.logical_and, [pl.program_id(a) == 0 for a in range(len(grid))])
        is_last = functools.reduce(jnp.logical_and, [pl.program_id(a) == grid[a] - 1 for a in range(len(grid))])

        @pl.when(is_first)
        def _():
            start()

        body(*ins, *outs, *scratch)

        @pl.when(is_last)
        def _():
            finish()

    res = _call(
        hosting_body, name=name + "_" + exchange.tag, grid=grid, in_specs=list(in_specs) + [HBM_SPEC] * n_xi,
        out_specs=out_specs + [HBM_SPEC] * n_xo, out_shape=out_shape + list(exchange.out_shapes),
        scratch_shapes=list(scratch_shapes) + list(exchange.scratch),
        compiler_params=_params(*(["arbitrary"] * len(grid))))(*args, *exchange.inputs)
    return res[:n_out], res[n_out:]


def _call_gathering(body, shards, args, **kw):
    return _call_hosting(body, _gather_exchange(shards), args, **kw)


def _row_tile(h):
    assert h <= 512
    return h


def _rs_pair(gs):
    n = len(gs)

    def body(*refs):
        g_refs, a_refs, (send_sems, recv_sems) = refs[:n], refs[n:2 * n], refs[2 * n:]
        x, y, c = _mesh_pos()
        cps = []
        for k in range(n):
            h = g_refs[k].shape[2] // 2
            cps.append(pltpu.make_async_remote_copy(
                src_ref=g_refs[k].at[:, :, pl.ds((1 - c) * h, h), :], dst_ref=a_refs[k], send_sem=send_sems.at[k],
                recv_sem=recv_sems.at[k], device_id=(x, y, 1 - c), device_id_type=MESH))
        for cp in cps:
            cp.start()
        for cp in cps:
            cp.wait()

    out_shape = [SDS((g.shape[0], g.shape[1], g.shape[2] // 2, g.shape[3]), g.dtype) for g in gs]
    return _call(body, name="rs_pair", in_specs=[HBM_SPEC] * n, out_specs=[HBM_SPEC] * n, out_shape=out_shape,
                 scratch_shapes=[pltpu.SemaphoreType.DMA((n,)), pltpu.SemaphoreType.DMA((n,))])(*gs)


def _rs_pair_add(g, from_sibling, mid_dtype):
    n_l, n_p, r, c_w = g.shape
    h = r // 2
    tr = _row_tile(h)
    nt = h // tr
    c_arr = jnp.reshape(lax.axis_index("c"), (1,)).astype(jnp.int32)

    def body(c_ref, g_ref, a_ref, o_ref):
        o_ref[...] = (g_ref[...].astype(F32) + a_ref[...].astype(F32)).astype(o_ref.dtype)

    blk = (None, None, tr, c_w)
    return _call(
        body, name="rs_pair_add",
        grid_spec=pltpu.PrefetchScalarGridSpec(
            num_scalar_prefetch=1, grid=(n_l, n_p, nt),
            in_specs=[pl.BlockSpec(blk, lambda l, p, t, c_ref: (l, p, c_ref[0] * nt + t, 0)),
                      pl.BlockSpec(blk, lambda l, p, t, c_ref: (l, p, t, 0))],
            out_specs=pl.BlockSpec(blk, lambda l, p, t, c_ref: (l, p, t, 0))),
        out_shape=SDS((n_l, n_p, h, c_w), mid_dtype),
        compiler_params=_params("parallel", "parallel", "parallel"))(c_arr, g, from_sibling)


def _chips_exchange(pair_sums):
    n = len(pair_sums)

    def make_steps(s_refs, b_refs, sems):
        send_sems, recv_sems = sems
        x, y, c = _mesh_pos()
        cps = [pltpu.make_async_remote_copy(
            src_ref=s_refs[k].at[:, 2 * chip[0] + chip[1]], dst_ref=b_refs[k].at[j], send_sem=send_sems.at[3 * k + j],
            recv_sem=recv_sems.at[3 * k + j], device_id=(*chip, c), device_id_type=MESH)
            for k in range(n) for j, chip in enumerate(_other_chips(x, y))]
        return (lambda: [cp.start() for cp in cps]), (lambda: [cp.wait() for cp in cps])

    out_shapes = [SDS((3, s.shape[0], s.shape[2], s.shape[3]), s.dtype) for s in pair_sums]
    sems = [pltpu.SemaphoreType.DMA((3 * n,)), pltpu.SemaphoreType.DMA((3 * n,))]
    return _Exchange("scattering", list(pair_sums), out_shapes, sems, make_steps)


def _rs_chips(pair_sums):
    ex = _chips_exchange(pair_sums)
    n = len(pair_sums)

    def body(*refs):
        start, finish = ex.make_steps(refs[:n], refs[n:2 * n], refs[2 * n:])
        start()
        finish()

    return _call(body, name="rs_chips", in_specs=[HBM_SPEC] * n, out_specs=[HBM_SPEC] * n, out_shape=ex.out_shapes,
                 scratch_shapes=ex.scratch)(*pair_sums)


def _rs_chip_add(pair_sum, from_chips, n_layers, first_layer, buf=None):
    n_l, _, h, c_w = pair_sum.shape
    tr = _row_tile(h)
    nt = h // tr
    p_arr = jnp.reshape(2 * lax.axis_index("x") + lax.axis_index("y"), (1,)).astype(jnp.int32)
    c_arr = jnp.reshape(lax.axis_index("c"), (1,)).astype(jnp.int32)

    def body(p_ref, c_ref, s_ref, b_ref, *rest):
        acc = s_ref[...].astype(F32)
        for j in range(3):
            acc = acc + b_ref[j].astype(F32)
        rest[-1][...] = acc

    in_specs = [pl.BlockSpec((None, None, tr, c_w), lambda l, t, p_ref, c_ref: (l, p_ref[0], t, 0)),
                pl.BlockSpec((3, None, tr, c_w), lambda l, t, p_ref, c_ref: (0, l, t, 0))]
    args = [p_arr, c_arr, pair_sum, from_chips]
    aliases = {}
    if buf is not None:
        in_specs.append(pl.BlockSpec(memory_space=pl.ANY))
        args.append(buf)
        aliases = {4: 0}
    return _call(
        body, name="rs_chip_add",
        grid_spec=pltpu.PrefetchScalarGridSpec(
            num_scalar_prefetch=2, grid=(n_l, nt), in_specs=in_specs,
            out_specs=pl.BlockSpec((None, tr, c_w), lambda l, t, p_ref, c_ref: (first_layer + l, c_ref[0] * nt + t, 0))),
        out_shape=SDS((n_layers, 2 * h, c_w), F32), input_output_aliases=aliases,
        compiler_params=_params("parallel", "parallel"))(*args)


def _rs_swap(halves):
    n = len(halves)

    def body(*refs):
        outs, (send_sems, recv_sems) = refs[n:2 * n], refs[2 * n:]
        x, y, c = _mesh_pos()

        def copy(k, half):
            h = outs[k].shape[1] // 2
            mine = outs[k].at[:, pl.ds(c * h, h), :]
            return pltpu.make_async_remote_copy(
                src_ref=mine, dst_ref=mine if half == "mine" else outs[k].at[:, pl.ds((1 - c) * h, h), :],
                send_sem=send_sems.at[k], recv_sem=recv_sems.at[k], device_id=(x, y, 1 - c), device_id_type=MESH)

        for k in range(n):
            copy(k, "mine").start()
        for k in range(n):
            copy(k, "theirs").wait_send()
            copy(k, "theirs").wait_recv()

    return _call(body, name="rs_swap", in_specs=[HBM_SPEC] * n, out_specs=[HBM_SPEC] * n,
                 out_shape=[SDS(a.shape, F32) for a in halves], input_output_aliases={k: k for k in range(n)},
                 scratch_shapes=[pltpu.SemaphoreType.DMA((n,)), pltpu.SemaphoreType.DMA((n,))])(*halves)


def _pack(arrays, row_multiple):
    flat = jnp.concatenate([a.reshape(-1).astype(F32) for a in arrays])
    unit = row_multiple * LANES
    padded = -(-flat.shape[0] // unit) * unit
    return jnp.pad(flat, (0, padded - flat.shape[0])).reshape(padded // LANES, LANES)


def _unpack(packed, shapes):
    flat = packed.reshape(-1)
    out, pos = [], 0
    for s in shapes:
        size = 1
        for dim in s:
            size *= dim
        out.append(flat[pos:pos + size].reshape(s))
        pos += size
    return out


BIG_COL = ("sb_w_in", "cv_w_pw1", "ffn_w_up")
BIG_ROW = ("hyb_w_out", "cv_w_pw2", "ffn_w_down")
SMALL_SHARDED = ("cv_b_pw1", "cv_w_dw", "cv_b_dw", "cv_ln_g", "cv_ln_b", "cv_b_pw2", "ffn_w_dw")
SMALL_REPLICATED = ("mix_norm_g", "sb_q_norm_g", "sb_k_norm_g", "sg_z_norm_g", "sg_w_spatial", "sg_b_spatial",
                    "ffn_norm_g", "ffn_b_dw")
WEIGHTS = ("mix_norm_g", "sb_w_in", "sb_q_norm_g", "sb_k_norm_g", "sg_z_norm_g", "sg_w_spatial", "sg_b_spatial",
           "hyb_w_out", "cv_w_pw1", "cv_b_pw1", "cv_w_dw", "cv_b_dw", "cv_ln_g", "cv_ln_b", "cv_w_pw2", "cv_b_pw2",
           "ffn_norm_g", "ffn_w_up", "ffn_w_dw", "ffn_b_dw", "ffn_w_down")


def _pad_rows(a, rows):
    return jnp.pad(a, ((0, rows - a.shape[0]), (0, 0)))


def _step(x, tgt, w, m, v):
    n_layers = w["mix_norm_g"].shape[0]
    xi, yi, ci = _mesh_pos()
    chip = 2 * xi + yi

    assert n_layers == 4
    hosted_by = {("proj", 0): ["hyb_w_out"], ("prep", 0): [("ffn_w_up", 0)],
                 ("attn", 0): [("ffn_w_down", 0), "cv_w_pw1", "cv_w_pw2"],
                 ("up", 0): [("ffn_w_up", 1)], ("ffn_mid", 0): [("ffn_w_down", 1)],
                 ("conf_mid", 1): [("ffn_w_up", 2), ("ffn_w_down", 2)],
                 ("up", 1): [("ffn_w_up", 3)], ("ffn_mid", 1): [("ffn_w_down", 3), ("sb_w_in", 1)]}
    full = {}

    def shard_of(key):
        if isinstance(key, tuple):
            return w[key[0]][key[1]:key[1] + 1].astype(BF)
        return w[key].astype(BF)

    def keep(key, g4):
        if (key[0] if isinstance(key, tuple) else key) in BIG_ROW:
            g4 = g4.reshape(g4.shape[0], 1, g4.shape[1] * g4.shape[2], g4.shape[3])
        full[key] = g4

    def hosting(fn, point, *args, **kw):
        keys = hosted_by.get(point)
        if not keys:
            return fn(*args, **kw)
        out, gathered = fn(*args, gather=[shard_of(k) for k in keys], **kw)
        for key, g4 in zip(keys, gathered):
            keep(key, g4)
        return out

    keep(("sb_w_in", 0), _all_gather(shard_of(("sb_w_in", 0))))
    small_local = [w[name] for name in SMALL_SHARDED]
    gathered = _all_gather(_pack(small_local, 32)[None])[0]
    per_chip = [_unpack(gathered[p], [a.shape for a in small_local]) for p in range(N_CHIPS)]
    for k, name in enumerate(SMALL_SHARDED):
        full[name] = jnp.concatenate([per_chip[p][k] for p in range(N_CHIPS)], axis=-1)
    for name in SMALL_REPLICATED:
        full[name] = w[name]

    mean64, fold64 = _group_matrices()
    ffn_wdw = [_pad_rows(full["ffn_w_dw"][i], 8) for i in range(n_layers)]
    cv_wdw = [_pad_rows(full["cv_w_dw"][j], 32) for j in range(n_layers // 2)]
    row = lambda a: a.reshape(1, -1)

    saved = []
    cur = x
    h = _rms_fwd(cur, row(full["mix_norm_g"][0]))
    for i in range(n_layers):
        j = i // 2
        rec = {"x_in": cur, "h_mix": h}
        if i % 2 == 0:
            proj = hosting(_mm_nn, ("proj", i), h, full[("sb_w_in", j)], 0)
            qg = row(jnp.tile(full["sb_q_norm_g"][j], 512 // HEAD_DIM))
            kg = row(jnp.tile(full["sb_k_norm_g"][j], 512 // HEAD_DIM))
            zg = row(full["sg_z_norm_g"][j])
            bexp = jnp.repeat(full["sg_b_spatial"][j].T, HEAD_DIM, axis=1)
            qkv, gated = hosting(_mix_prep_fwd, ("prep", i), proj, qg, kg, zg, full["sg_w_spatial"], j, bexp, mean64)
            att_bf, att_32 = hosting(_attn_fwd, ("attn", i), qkv)
            mix = jnp.concatenate([att_bf, gated], axis=1)
            cur, h = _mm_nn(mix, full["hyb_w_out"], j, resid=cur, norm_g=row(full["ffn_norm_g"][i]))
            rec.update(proj=proj, qkv=qkv, att_32=att_32, mix=mix, qg=qg, kg=kg, zg=zg, bexp=bexp)
        else:
            p1 = _mm_nn(h, full["cv_w_pw1"], j, bias=row(full["cv_b_pw1"][j]), out_dtype=BF)
            ys, yc = hosting(_conf_mid_fwd, ("conf_mid", i), p1, cv_wdw[j], row(full["cv_b_dw"][j]),
                             row(full["cv_ln_g"][j]), row(full["cv_ln_b"][j]))
            cur, h = _mm_nn(ys, full["cv_w_pw2"], j, bias=row(full["cv_b_pw2"][j]), resid=cur,
                            norm_g=row(full["ffn_norm_g"][i]))
            rec.update(p1=p1, ys=ys, yc=yc)
        rec["x_mid"] = cur
        up = hosting(_mm_nn, ("up", i), h, full[("ffn_w_up", i)], 0, out_dtype=BF)
        act = hosting(_ffn_mid_fwd, ("ffn_mid", i), up, ffn_wdw[i], row(full["ffn_b_dw"][i]))
        rec.update(h_ffn=h, up=up, act=act)
        if i + 1 < n_layers:
            cur, h = _mm_nn(act, full[("ffn_w_down", i)], 0, resid=cur, norm_g=row(full["mix_norm_g"][i + 1]))
        else:
            cur = _mm_nn(act, full[("ffn_w_down", i)], 0, resid=cur)
        saved.append(rec)

    loss_vec, dy, dy_bf = _loss_grad(cur, tgt)
    loss = lax.psum(loss_vec[0, 0], ("x", "y", "c"))

    big_names = BIG_COL + BIG_ROW
    gbig = {}
    gsmall = {name: [None] * w[name].shape[0] for name in SMALL_SHARDED + SMALL_REPLICATED}

    first_layer = {}
    group_of = lambda i: min(i, 2)

    def accumulate(name, layer, a, dy_, p_n):
        g = group_of(i)
        both = g == 2 and w[name].shape[0] == n_layers
        first_layer[(name, g)] = (w[name].shape[0] // 2) if g == 2 else layer
        gbig[(name, g)] = _mm_tn(a, dy_, p_n, 2 if both else 1, layer - first_layer[(name, g)], gbig.get((name, g)))

    def group_grads(g):
        names = [name for name in big_names if (name, g) in gbig]
        out = []
        for name in names:
            g4 = gbig[(name, g)]
            if name in BIG_ROW:
                g4 = g4.reshape(g4.shape[0], N_CHIPS, g4.shape[2] // N_CHIPS, g4.shape[3])
            out.append(g4)
        return names, out

    def pair_sums_of(gs, mid_dtypes):
        return [_rs_pair_add(g, a, dt) for g, a, dt in zip(gs, _rs_pair(gs), mid_dtypes)]

    finals = {}

    def finish_group(g, names, pair_sums, from_chips):
        for name, ps, fc in zip(names, pair_sums, from_chips):
            finals[name] = _rs_chip_add(ps, fc, w[name].shape[0], first_layer[(name, g)], finals.get(name))

    pending = None
    for i in reversed(range(n_layers)):
        j = i // 2
        rec = saved[i]
        dact = _mm_nt(dy_bf, full[("ffn_w_down", i)], 0, out_dtype=BF)
        accumulate("ffn_w_down", i, rec["act"], dy_bf, 1)
        carried = _chips_exchange(pending[2]) if (pending is not None and i == 0) else None
        res = _ffn_mid_bwd(rec["up"], dact, ffn_wdw[i], row(full["ffn_b_dw"][i]), exchange=carried)
        if carried is not None:
            res, from_chips = res
            finish_group(pending[0], pending[1], pending[2], from_chips)
            pending = None
        dup, dwdw, dbdw = res
        gsmall["ffn_w_dw"][i] = dwdw[:FFN_K]
        gsmall["ffn_b_dw"][i] = dbdw[0]
        accumulate("ffn_w_up", i, rec["h_ffn"], dup, N_CHIPS)
        dy, dy_bf, dg = _mm_nt_rms_bwd(dup, full[("ffn_w_up", i)], 0, rec["x_mid"], row(full["ffn_norm_g"][i]), dy)
        gsmall["ffn_norm_g"][i] = dg[0]
        if i % 2 == 0:
            dmix = _mm_nt(dy_bf, full["hyb_w_out"], j)
            accumulate("hyb_w_out", j, rec["mix"], dy_bf, 1)
            dq, dk, dv = _attn_bwd(rec["qkv"], rec["att_32"], dmix)
            dproj, dqg, dkg, dzg, dws, dbe = _mix_prep_bwd(
                rec["proj"], dq, dk, dv, dmix, rec["qg"], rec["kg"], rec["zg"], full["sg_w_spatial"], j, rec["bexp"],
                mean64, fold64)
            gsmall["sb_q_norm_g"][j] = dqg[0, :HEAD_DIM]
            gsmall["sb_k_norm_g"][j] = dkg[0, :HEAD_DIM]
            gsmall["sg_z_norm_g"][j] = dzg[0]
            gsmall["sg_w_spatial"][j] = dws
            gsmall["sg_b_spatial"][j] = dbe[:, ::HEAD_DIM].T
            dlast, w_first, l_first = dproj, full[("sb_w_in", j)], 0
            accumulate("sb_w_in", j, rec["h_mix"], dproj, N_CHIPS)
        else:
            dys = _mm_nt(dy_bf, full["cv_w_pw2"], j, out_dtype=BF)
            accumulate("cv_w_pw2", j, rec["ys"], dy_bf, 1)
            carried = _chips_exchange(pending[2]) if pending is not None else None
            res = _conf_mid_bwd(rec["p1"], rec["yc"], dys, dy, cv_wdw[j], row(full["cv_ln_g"][j]),
                                row(full["cv_ln_b"][j]), exchange=carried)
            if carried is not None:
                res, from_chips = res
                finish_group(pending[0], pending[1], pending[2], from_chips)
                pending = None
            dp1, dwdw, dbdw, dlg, dlb, db1, db2 = res
            gsmall["cv_w_dw"][j] = dwdw[:CONV_K]
            gsmall["cv_b_dw"][j] = dbdw[0]
            gsmall["cv_ln_g"][j] = dlg[0]
            gsmall["cv_ln_b"][j] = dlb[0]
            gsmall["cv_b_pw1"][j] = db1[0]
            gsmall["cv_b_pw2"][j] = db2[0]
            dlast, w_first, l_first = dp1, full["cv_w_pw1"], j
            accumulate("cv_w_pw1", j, rec["h_mix"], dp1, N_CHIPS)
        dy, dy_bf, dg = _mm_nt_rms_bwd(dlast, w_first, l_first, rec["x_in"], row(full["mix_norm_g"][i]), dy)
        gsmall["mix_norm_g"][i] = dg[0]
        if i in (1, 2):
            names, gs = group_grads(group_of(i))
            pending = (group_of(i), names, pair_sums_of(gs, [BF] * len(gs)))

    small_names = SMALL_REPLICATED + SMALL_SHARDED
    small_full = [jnp.stack(gsmall[name]) for name in small_names]
    packed = _pack(small_full, 32 * N_CHIPS)
    rows_q = packed.shape[0] // N_CHIPS
    names, gs = group_grads(0)
    last = pair_sums_of(gs + [packed.reshape(1, N_CHIPS, rows_q, LANES)], [BF] * len(gs) + [F32])
    from_chips = _rs_chips(last)
    finish_group(0, names, last[:-1], from_chips[:-1])
    halves = [finals[name] for name in big_names] + [_rs_chip_add(last[-1], from_chips[-1], 1, 0)]
    swapped = _rs_swap(halves)
    grads = dict(zip(big_names, swapped))
    summed = _all_gather(swapped[-1]).reshape(-1, LANES)
    for name, gsum in zip(small_names, _unpack(summed, [a.shape for a in small_full])):
        if name in SMALL_SHARDED:
            n_loc = w[name].shape[-1]
            split = gsum.reshape(gsum.shape[:-1] + (N_CHIPS, n_loc))
            gsum = lax.dynamic_index_in_dim(split, chip, axis=split.ndim - 2, keepdims=False)
        grads[name] = gsum

    delta, new_m, new_v = {}, {}, {}
    for name in BIG_COL + BIG_ROW:
        shp = w[name].shape
        two_d = lambda a: a.reshape(shp[0] * shp[1], shp[2])
        d, nm, nv = _adamw(two_d(w[name]), two_d(grads[name]), two_d(m[name]), two_d(v[name]))
        delta[name], new_m[name], new_v[name] = d.reshape(shp), nm.reshape(shp), nv.reshape(shp)
    shapes = [w[name].shape for name in small_names]
    d, nm, nv = _adamw(*(_pack([src[name] for name in small_names], 256) for src in (w, grads, m, v)))
    for name, a, b_, c_ in zip(small_names, _unpack(d, shapes), _unpack(nm, shapes), _unpack(nv, shapes)):
        delta[name], new_m[name], new_v[name] = a, b_, c_

    return (loss, dy, *[grads[n] for n in WEIGHTS], *[delta[n] for n in WEIGHTS],
            *[new_m[n] for n in WEIGHTS], *[new_v[n] for n in WEIGHTS])


def kernel(x, mix_norm_g, sb_w_in, sb_q_norm_g, sb_k_norm_g, sg_z_norm_g, sg_w_spatial, sg_b_spatial, hyb_w_out, cv_w_pw1, cv_b_pw1, cv_w_dw, cv_b_dw, cv_ln_g, cv_ln_b, cv_w_pw2, cv_b_pw2, ffn_norm_g, ffn_w_up, ffn_w_dw, ffn_b_dw, ffn_w_down, loss_target, m_mix_norm_g, m_sb_w_in, m_sb_q_norm_g, m_sb_k_norm_g, m_sg_z_norm_g, m_sg_w_spatial, m_sg_b_spatial, m_hyb_w_out, m_cv_w_pw1, m_cv_b_pw1, m_cv_w_dw, m_cv_b_dw, m_cv_ln_g, m_cv_ln_b, m_cv_w_pw2, m_cv_b_pw2, m_ffn_norm_g, m_ffn_w_up, m_ffn_w_dw, m_ffn_b_dw, m_ffn_w_down, v_mix_norm_g, v_sb_w_in, v_sb_q_norm_g, v_sb_k_norm_g, v_sg_z_norm_g, v_sg_w_spatial, v_sg_b_spatial, v_hyb_w_out, v_cv_w_pw1, v_cv_b_pw1, v_cv_w_dw, v_cv_b_dw, v_cv_ln_g, v_cv_ln_b, v_cv_w_pw2, v_cv_b_pw2, v_ffn_norm_g, v_ffn_w_up, v_ffn_w_dw, v_ffn_b_dw, v_ffn_w_down):
    given = dict(locals())
    w = {n: given[n] for n in WEIGHTS}
    m = {n: given["m_" + n] for n in WEIGHTS}
    v = {n: given["v_" + n] for n in WEIGHTS}
    out = _step(x[0], loss_target[0], w, m, v)
    return (out[0], out[1][None], *out[2:])
```
